```python
import math
import jax, jax.numpy as jnp
from jax import lax
import numpy as np

D_MODEL = 1024
BATCH = 2
SEQ = 8192
DEPTH = 2

N_MEM = 256
N_A_LAYERS = DEPTH // 2
N_B_LAYERS = DEPTH - N_A_LAYERS
HEAD_DIM = 64
FOX_HEADS = 12
MEM_HEADS = 4
MLA_HEADS = 12
Q_LORA = 384
KV_LORA = 256
QK_NOPE = 64
QK_ROPE = 32
V_DIM = 64
ROPE_THETA = 10000.0
N_EXPERTS = 32
TOP_K = 4
D_EXPERT = D_MODEL
SWIGLU_LIMIT = 7.0
SWIGLU_ALPHA = 1.702
MOE_BLOCK = 128
Q_BLOCK = 128
LN_EPS = 1e-5
RMS_EPS = 1e-6
NEG_INF = -1e30
DEEPNORM_ALPHA = (2 * DEPTH) ** 0.25
DEEPNORM_BETA = (8 * DEPTH) ** -0.25
FOX_WIDTH = FOX_HEADS * HEAD_DIM
MEM_WIDTH = MEM_HEADS * HEAD_DIM
MLA_V_WIDTH = MLA_HEADS * V_DIM
FOX_IN = 3 * FOX_WIDTH + FOX_HEADS + MEM_WIDTH
MLA_IN = Q_LORA + MEM_WIDTH
MIX_OUT_A = FOX_WIDTH + MEM_WIDTH
MIX_OUT_B = MLA_V_WIDTH + MEM_WIDTH

kernel_name = 'hybrid_fox_mla_yoco_moe_block'


def layer_norm(x, g, b):
    xf = x.astype(jnp.float32)
    mu = jnp.mean(xf, axis=-1, keepdims=True)
    var = jnp.mean(jnp.square(xf - mu), axis=-1, keepdims=True)
    y = (xf - mu) * lax.rsqrt(var + LN_EPS) * g.astype(jnp.float32) + b.astype(jnp.float32)
    return y.astype(x.dtype)


def rms_norm(x, g):
    xf = x.astype(jnp.float32)
    y = xf * lax.rsqrt(jnp.mean(jnp.square(xf), axis=-1, keepdims=True) + RMS_EPS) * g.astype(jnp.float32)
    return y.astype(x.dtype)


def rope(x, positions):
    r = x.shape[-1]
    half = r // 2
    inv_freq = ROPE_THETA ** (-jnp.arange(half, dtype=jnp.float32) * 2.0 / r)
    ang = positions.astype(jnp.float32)[..., None] * inv_freq
    cos = jnp.cos(ang)[:, :, None, :]
    sin = jnp.sin(ang)[:, :, None, :]
    xf = x.astype(jnp.float32)
    x1, x2 = xf[..., :half], xf[..., half:]
    return jnp.concatenate([x1 * cos - x2 * sin, x1 * sin + x2 * cos], axis=-1).astype(x.dtype)


def causal_block_attention(q, k, v, scale, cum=None):
    B, S, H, Dk = q.shape
    nb = S // Q_BLOCK
    qb = q.reshape(B, nb, Q_BLOCK, H, Dk).swapaxes(0, 1)
    key_pos = jnp.arange(S)
    blk_idx = jnp.arange(nb)
    if cum is not None:
        cum_k = cum.transpose(0, 2, 1)
        cum_q = cum.reshape(B, nb, Q_BLOCK, H).swapaxes(0, 1)

    def one_block(args):
        i, qi = args[0], args[1]
        s = jnp.einsum('bqhd,bkhd->bhqk', qi, k, preferred_element_type=jnp.float32) * scale
        if cum is not None:
            ci = args[2]
            s = s + ci.transpose(0, 2, 1)[..., None] - cum_k[:, :, None, :]
        q_pos = i * Q_BLOCK + jnp.arange(Q_BLOCK)
        s = jnp.where(key_pos[None, :] <= q_pos[:, None], s, NEG_INF)
        p = jax.nn.softmax(s, axis=-1)
        return jnp.einsum('bhqk,bkhd->bqhd', p.astype(v.dtype), v)

    xs = (blk_idx, qb) if cum is None else (blk_idx, qb, cum_q)
    out = lax.map(one_block, xs)
    return out.swapaxes(0, 1).reshape(B, S, H, v.shape[-1])


def memory_attention(q_flat, mem, w_mkv):
    B, S, _ = q_flat.shape
    M = mem.shape[1]
    q = q_flat.reshape(B, S, MEM_HEADS, HEAD_DIM)
    mkv = jnp.einsum('bmd,de->bme', mem, w_mkv)
    mk = mkv[..., :MEM_WIDTH].reshape(B, M, MEM_HEADS, HEAD_DIM)
    mv = mkv[..., MEM_WIDTH:].reshape(B, M, MEM_HEADS, HEAD_DIM)
    s = jnp.einsum('bshd,bmhd->bhsm', q, mk, preferred_element_type=jnp.float32) * (HEAD_DIM ** -0.5)
    p = jax.nn.softmax(s, axis=-1)
    o = jnp.einsum('bhsm,bmhd->bshd', p.astype(mv.dtype), mv)
    return o.reshape(B, S, MEM_WIDTH)


def fox_mixer(x, mem, w_in, b_f, w_out, w_mkv):
    B, S, _ = x.shape
    proj = jnp.einsum('bsd,de->bse', x, w_in)
    q = proj[..., :FOX_WIDTH].reshape(B, S, FOX_HEADS, HEAD_DIM)
    k = proj[..., FOX_WIDTH:2 * FOX_WIDTH].reshape(B, S, FOX_HEADS, HEAD_DIM)
    v = proj[..., 2 * FOX_WIDTH:3 * FOX_WIDTH].reshape(B, S, FOX_HEADS, HEAD_DIM)
    f_logit = proj[..., 3 * FOX_WIDTH:3 * FOX_WIDTH + FOX_HEADS]
    mq = proj[..., 3 * FOX_WIDTH + FOX_HEADS:]
    log_f = jax.nn.log_sigmoid(f_logit.astype(jnp.float32) + b_f.astype(jnp.float32))
    cum = jnp.cumsum(log_f, axis=1)
    fox = causal_block_attention(q, k, v, HEAD_DIM ** -0.5, cum)
    memo = memory_attention(mq, mem, w_mkv)
    merged = jnp.concatenate([fox.reshape(B, S, FOX_WIDTH), memo], axis=-1)
    return jnp.einsum('bse,ed->bsd', merged, w_out)


def shared_mla_kv(h, positions, w_dkv, g_kv, w_ukv):
    B, S, _ = h.shape
    lat = jnp.einsum('bsd,de->bse', h, w_dkv)
    c_kv = rms_norm(lat[..., :KV_LORA], g_kv)
    k_r = lat[..., KV_LORA:]
    kv = jnp.einsum('bsc,ce->bse', c_kv, w_ukv).reshape(B, S, MLA_HEADS, QK_NOPE + V_DIM)
    k_nope, v = kv[..., :QK_NOPE], kv[..., QK_NOPE:]
    k_rope = rope(k_r[:, :, None, :], positions)
    k = jnp.concatenate([k_nope, jnp.broadcast_to(k_rope, (B, S, MLA_HEADS, QK_ROPE))], axis=-1)
    return k, v


def mla_mixer(x, mem, positions, k, v, w_in, g_q, w_uq, w_out, w_mkv):
    B, S, _ = x.shape
    proj = jnp.einsum('bsd,de->bse', x, w_in)
    c_q = rms_norm(proj[..., :Q_LORA], g_q)
    mq = proj[..., Q_LORA:]
    q = jnp.einsum('bsc,ce->bse', c_q, w_uq).reshape(B, S, MLA_HEADS, QK_NOPE + QK_ROPE)
    q = jnp.concatenate([q[..., :QK_NOPE], rope(q[..., QK_NOPE:], positions)], axis=-1)
    att = causal_block_attention(q, k, v, (QK_NOPE + QK_ROPE) ** -0.5)
    memo = memory_attention(mq, mem, w_mkv)
    merged = jnp.concatenate([att.reshape(B, S, MLA_V_WIDTH), memo], axis=-1)
    return jnp.einsum('bse,ed->bsd', merged, w_out)


def moe_ffn(x, w_r, b_r, w_gu, b_gu, w_dn, b_dn):
    B, S, D = x.shape
    T = B * S
    xt = x.reshape(T, D)
    logits = jnp.dot(xt, w_r, preferred_element_type=jnp.float32) + b_r.astype(jnp.float32)
    top_logit, top_idx = lax.top_k(logits, TOP_K)
    gates = jax.nn.softmax(top_logit, axis=-1)
    M = T * TOP_K
    expert_flat = top_idx.reshape(M)
    token_flat = jnp.arange(M, dtype=jnp.int32) // TOP_K
    gate_flat = gates.reshape(M)
    order = jnp.argsort(expert_flat)
    e_sorted = expert_flat[order]
    tok_sorted = token_flat[order]
    gate_sorted = gate_flat[order]
    counts = jnp.bincount(expert_flat, length=N_EXPERTS)
    padded = ((counts + MOE_BLOCK - 1) // MOE_BLOCK) * MOE_BLOCK
    start = jnp.cumsum(counts) - counts
    pend = jnp.cumsum(padded)
    pstart = pend - padded
    dest = pstart[e_sorted] + (jnp.arange(M) - start[e_sorted])
    n_blocks = -(-M // MOE_BLOCK) + N_EXPERTS
    P = n_blocks * MOE_BLOCK
    slot_tok = jnp.full((P,), T, jnp.int32).at[dest].set(tok_sorted)
    x_pad = jnp.concatenate([xt, jnp.zeros((1, D), xt.dtype)], axis=0)
    xb = x_pad[slot_tok].reshape(n_blocks, MOE_BLOCK, D)
    block_expert = jnp.minimum(
        jnp.searchsorted(pend, jnp.arange(n_blocks) * MOE_BLOCK, side='right'), N_EXPERTS - 1)

    def expert_block(args):
        xi, e = args
        gu = jnp.dot(xi, w_gu[e]) + b_gu[e]
        g, u = gu[:, :D_EXPERT], gu[:, D_EXPERT:]
        g = jnp.minimum(g, SWIGLU_LIMIT)
        u = jnp.clip(u, -SWIGLU_LIMIT, SWIGLU_LIMIT)
        h = (u + 1.0) * (g * jax.nn.sigmoid(SWIGLU_ALPHA * g))
        return jnp.dot(h, w_dn[e]) + b_dn[e]

    yb = lax.map(expert_block, (xb, block_expert)).reshape(P, D)
    y_sorted = yb[dest].astype(jnp.float32) * gate_sorted[:, None]
    out = jax.ops.segment_sum(y_sorted, tok_sorted, num_segments=T)
    return out.astype(x.dtype).reshape(B, S, D)


def setup_inputs(seed: int = 0) -> dict:
    key = jax.random.key(seed)
    ks = jax.random.split(key, 24)

    def nrm(k, shape, scale):
        return jax.random.normal(k, shape, jnp.float32) * scale

    x = nrm(ks[0], (BATCH, SEQ, D_MODEL), 1.0)
    mem = nrm(ks[1], (BATCH, N_MEM, D_MODEL), 1.0)
    positions = jnp.broadcast_to(jnp.arange(SEQ, dtype=jnp.int32)[None, :], (BATCH, SEQ))
    a_w_in = nrm(ks[2], (N_A_LAYERS, D_MODEL, FOX_IN), D_MODEL ** -0.5)
    a_w_in = a_w_in.at[:, :, 2 * FOX_WIDTH:3 * FOX_WIDTH].multiply(DEEPNORM_BETA)
    a_b_f = nrm(ks[3], (N_A_LAYERS, FOX_HEADS), 0.1)
    a_w_out = nrm(ks[4], (N_A_LAYERS, MIX_OUT_A, D_MODEL), MIX_OUT_A ** -0.5 * DEEPNORM_BETA)
    b_w_in = nrm(ks[5], (N_B_LAYERS, D_MODEL, MLA_IN), D_MODEL ** -0.5)
    b_g_q = 1.0 + nrm(ks[6], (N_B_LAYERS, Q_LORA), 0.01)
    b_w_uq = nrm(ks[7], (N_B_LAYERS, Q_LORA, MLA_HEADS * (QK_NOPE + QK_ROPE)), Q_LORA ** -0.5)
    b_w_out = nrm(ks[8], (N_B_LAYERS, MIX_OUT_B, D_MODEL), MIX_OUT_B ** -0.5 * DEEPNORM_BETA)
    kv_w_dkv = nrm(ks[9], (D_MODEL, KV_LORA + QK_ROPE), D_MODEL ** -0.5)
    kv_g = 1.0 + nrm(ks[10], (KV_LORA,), 0.01)
    w_uk = nrm(ks[11], (KV_LORA, MLA_HEADS, QK_NOPE), KV_LORA ** -0.5)
    w_uv = nrm(ks[12], (KV_LORA, MLA_HEADS, V_DIM), KV_LORA ** -0.5 * DEEPNORM_BETA)
    kv_w_ukv = jnp.concatenate([w_uk, w_uv], axis=-1).reshape(KV_LORA, MLA_HEADS * (QK_NOPE + V_DIM))
    mem_w_kv = nrm(ks[13], (DEPTH, D_MODEL, 2 * MEM_WIDTH), D_MODEL ** -0.5)
    mem_w_kv = mem_w_kv.at[:, :, MEM_WIDTH:].multiply(DEEPNORM_BETA)
    ln_g = 1.0 + nrm(ks[14], (DEPTH, 2, D_MODEL), 0.01)
    ln_b = nrm(ks[15], (DEPTH, 2, D_MODEL), 0.01)
    moe_w_r = nrm(ks[16], (DEPTH, D_MODEL, N_EXPERTS), D_MODEL ** -0.5)
    moe_b_r = nrm(ks[17], (DEPTH, N_EXPERTS), 0.01)
    moe_w_gu = nrm(ks[18], (DEPTH, N_EXPERTS, D_MODEL, 2 * D_EXPERT), D_MODEL ** -0.5)
    moe_b_gu = nrm(ks[19], (DEPTH, N_EXPERTS, 2 * D_EXPERT), 0.01)
    moe_w_dn = nrm(ks[20], (DEPTH, N_EXPERTS, D_EXPERT, D_MODEL), D_EXPERT ** -0.5 * DEEPNORM_BETA)
    moe_b_dn = nrm(ks[21], (DEPTH, N_EXPERTS, D_MODEL), 0.01)
    return {'x': x, 'mem': mem, 'positions': positions,
            'a_w_in': a_w_in, 'a_b_f': a_b_f, 'a_w_out': a_w_out,
            'b_w_in': b_w_in, 'b_g_q': b_g_q, 'b_w_uq': b_w_uq, 'b_w_out': b_w_out,
            'kv_w_dkv': kv_w_dkv, 'kv_g': kv_g, 'kv_w_ukv': kv_w_ukv,
            'mem_w_kv': mem_w_kv, 'ln_g': ln_g, 'ln_b': ln_b,
            'moe_w_r': moe_w_r, 'moe_b_r': moe_b_r, 'moe_w_gu': moe_w_gu, 'moe_b_gu': moe_b_gu,
            'moe_w_dn': moe_w_dn, 'moe_b_dn': moe_b_dn}


def reference(x, mem, positions, a_w_in, a_b_f, a_w_out, b_w_in, b_g_q, b_w_uq, b_w_out,
              kv_w_dkv, kv_g, kv_w_ukv, mem_w_kv, ln_g, ln_b,
              moe_w_r, moe_b_r, moe_w_gu, moe_b_gu, moe_w_dn, moe_b_dn):
    shared_k = None
    shared_v = None
    for l in range(DEPTH):
        if l < N_A_LAYERS:
            mix = fox_mixer(x, mem, a_w_in[l], a_b_f[l], a_w_out[l], mem_w_kv[l])
        else:
            if shared_k is None:
                shared_k, shared_v = shared_mla_kv(x, positions, kv_w_dkv, kv_g, kv_w_ukv)
            b = l - N_A_LAYERS
            mix = mla_mixer(x, mem, positions, shared_k, shared_v,
                            b_w_in[b], b_g_q[b], b_w_uq[b], b_w_out[b], mem_w_kv[l])
        x = layer_norm(DEEPNORM_ALPHA * x + mix, ln_g[l, 0], ln_b[l, 0])
        ffn = moe_ffn(x, moe_w_r[l], moe_b_r[l], moe_w_gu[l], moe_b_gu[l], moe_w_dn[l], moe_b_dn[l])
        x = layer_norm(DEEPNORM_ALPHA * x + ffn, ln_g[l, 1], ln_b[l, 1])
    return x
```

```python
import functools
import math

import jax
import jax.numpy as jnp
from jax import lax
from jax.experimental import pallas as pl
from jax.experimental.pallas import tpu as pltpu

F32 = jnp.float32
BF16 = jnp.bfloat16

D_MODEL = 1024
DEPTH = 2
N_MEM = 256
HEAD_DIM = 64
FOX_HEADS = 12
MEM_HEADS = 4
MLA_HEADS = 12
Q_LORA = 384
KV_LORA = 256
QK_NOPE = 64
QK_ROPE = 32
V_DIM = 64
ROPE_THETA = 10000.0
N_EXPERTS = 32
TOP_K = 4
D_EXPERT = D_MODEL
SWIGLU_LIMIT = 7.0
SWIGLU_ALPHA = 1.702
LN_EPS = 1e-5
RMS_EPS = 1e-6
NEG_INF = -1e30
DEEPNORM_ALPHA = (2 * DEPTH) ** 0.25
FOX_WIDTH = FOX_HEADS * HEAD_DIM
MEM_WIDTH = MEM_HEADS * HEAD_DIM
MLA_V_WIDTH = MLA_HEADS * V_DIM

LANES = 128
LOG2E = math.log2(math.e)
VMEM_LIMIT = 48 * 1024 * 1024

PROJ_TM = 512
ATT_TQ = 512
MOE_TM = 256
MLA_SLAB = LANES


def _cparams(n_axes):
    return pltpu.CompilerParams(dimension_semantics=("arbitrary",) * n_axes,
                                vmem_limit_bytes=VMEM_LIMIT)


def _split3(x):
    hi = x.astype(BF16)
    r1 = x - hi.astype(F32)
    mid = r1.astype(BF16)
    lo = (r1 - mid.astype(F32)).astype(BF16)
    return hi, mid, lo


def _layer_norm(y, g, b):
    mu = jnp.mean(y, axis=-1, keepdims=True)
    yc = y - mu
    var = jnp.mean(yc * yc, axis=-1, keepdims=True)
    return yc * lax.rsqrt(var + LN_EPS) * g + b


def _rms_norm(y, g):
    return y * lax.rsqrt(jnp.mean(y * y, axis=-1, keepdims=True) + RMS_EPS) * g


def _mm_kernel(a_ref, b_ref, o_ref):
    o_ref[...] = jnp.dot(a_ref[...].astype(BF16), b_ref[...],
                         preferred_element_type=F32).astype(o_ref.dtype)


def _matmul(a, b, out_dtype, tm):
    m, k = a.shape
    n = b.shape[1]
    return pl.pallas_call(
        _mm_kernel,
        grid=(m // tm,),
        in_specs=[pl.BlockSpec((tm, k), lambda i: (i, 0)),
                  pl.BlockSpec((k, n), lambda i: (0, 0))],
        out_specs=pl.BlockSpec((tm, n), lambda i: (i, 0)),
        out_shape=jax.ShapeDtypeStruct((m, n), out_dtype),
        compiler_params=_cparams(1),
        name="matmul",
    )(a, b)


def _fox_proj_kernel(x_ref, w_ref, bf_ref, q_ref, k_ref, v_ref, mq_ref, cum_ref, cumt_ref,
                     carry_ref, *, tiles_per_batch):
    t = pl.program_id(0)

    @pl.when(t % tiles_per_batch == 0)
    def _():
        carry_ref[...] = jnp.zeros_like(carry_ref)

    tm = x_ref.shape[0]
    proj = jnp.dot(x_ref[...].astype(BF16), w_ref[...], preferred_element_type=F32)
    qscale = HEAD_DIM ** -0.5 * LOG2E
    q_ref[...] = (proj[:, :FOX_WIDTH] * qscale).astype(BF16)
    k_ref[...] = proj[:, FOX_WIDTH:2 * FOX_WIDTH].astype(BF16)
    v_ref[...] = proj[:, 2 * FOX_WIDTH:3 * FOX_WIDTH].astype(BF16)
    f = proj[:, 3 * FOX_WIDTH:3 * FOX_WIDTH + LANES] + bf_ref[...]
    mq_ref[...] = (proj[:, 3 * FOX_WIDTH + LANES:] * qscale).astype(BF16)
    log_f = jnp.minimum(f, 0.0) - jnp.log1p(jnp.exp(-jnp.abs(f)))
    row = lax.broadcasted_iota(jnp.int32, (tm, tm), 0)
    col = lax.broadcasted_iota(jnp.int32, (tm, tm), 1)
    tri = jnp.where(row >= col, 1.0, 0.0).astype(BF16)
    hi, mid, lo = _split3(log_f)
    cum = (jnp.dot(tri, hi, preferred_element_type=F32)
           + jnp.dot(tri, mid, preferred_element_type=F32)
           + jnp.dot(tri, lo, preferred_element_type=F32)) + carry_ref[...]
    carry_ref[...] = cum[tm - 1:tm, :]
    cum2 = cum * LOG2E
    cum_ref[...] = cum2
    cumt_ref[0] = cum2.T[:16, :]


def _fox_proj(x2d, w, bf, seq):
    t_tokens = x2d.shape[0]
    tm = PROJ_TM
    nb = t_tokens // seq
    n = w.shape[1]
    tiles_per_batch = seq // tm
    row_spec = lambda width: pl.BlockSpec((tm, width), lambda i: (i, 0))
    return pl.pallas_call(
        functools.partial(_fox_proj_kernel, tiles_per_batch=tiles_per_batch),
        grid=(t_tokens // tm,),
        in_specs=[row_spec(D_MODEL),
                  pl.BlockSpec((D_MODEL, n), lambda i: (0, 0)),
                  pl.BlockSpec((1, LANES), lambda i: (0, 0))],
        out_specs=[row_spec(FOX_WIDTH), row_spec(FOX_WIDTH), row_spec(FOX_WIDTH),
                   row_spec(MEM_WIDTH), row_spec(LANES),
                   pl.BlockSpec((1, 16, tm),
                                lambda i: (i // tiles_per_batch, 0, i % tiles_per_batch))],
        out_shape=[jax.ShapeDtypeStruct((t_tokens, FOX_WIDTH), BF16),
                   jax.ShapeDtypeStruct((t_tokens, FOX_WIDTH), BF16),
                   jax.ShapeDtypeStruct((t_tokens, FOX_WIDTH), BF16),
                   jax.ShapeDtypeStruct((t_tokens, MEM_WIDTH), BF16),
                   jax.ShapeDtypeStruct((t_tokens, LANES), F32),
                   jax.ShapeDtypeStruct((nb, 16, seq), F32)],
        scratch_shapes=[pltpu.VMEM((1, LANES), F32)],
        compiler_params=_cparams(1),
        name="fox_proj",
    )(x2d, w, bf)


def _rope_tables(pos_ref, invf_ref):
    ang = pos_ref[...].astype(F32) * invf_ref[...]
    cos = jnp.cos(ang)
    sin = jnp.sin(ang)
    lane = lax.broadcasted_iota(jnp.int32, ang.shape, 1)
    half = QK_ROPE // 2
    in_x1 = (lane >= QK_NOPE) & (lane < QK_NOPE + half)
    in_x2 = (lane >= QK_NOPE + half) & (lane < QK_NOPE + QK_ROPE)
    c_tab = jnp.where(lane < QK_NOPE, 1.0, jnp.where(in_x1 | in_x2, cos, 0.0))
    s_up = jnp.where(in_x2, sin, 0.0)
    s_dn = jnp.where(in_x1, -sin, 0.0)
    return c_tab, s_up, s_dn


def _rope_slab(slab, tables):
    c_tab, s_up, s_dn = tables
    half = QK_ROPE // 2
    up = pltpu.roll(slab, half, 1)
    dn = pltpu.roll(slab, LANES - half, 1)
    return slab * c_tab + up * s_up + dn * s_dn


def _mla_proj_kernel(x_ref, pos_ref, invf_ref, w_ref, gq_ref, wuq_ref, gkv_ref, wk_ref, wv_ref,
                     q_ref, mq_ref, k_ref, v_ref):
    proj = jnp.dot(x_ref[...].astype(BF16), w_ref[...], preferred_element_type=F32)
    tables = _rope_tables(pos_ref, invf_ref)
    c_q = _rms_norm(proj[:, :Q_LORA], gq_ref[...])
    mq_ref[...] = (proj[:, Q_LORA:Q_LORA + MEM_WIDTH] * (HEAD_DIM ** -0.5 * LOG2E)).astype(BF16)
    kv_off = Q_LORA + MEM_WIDTH
    c_kv = _rms_norm(proj[:, kv_off:kv_off + KV_LORA], gkv_ref[...])
    kr = _rope_slab(proj[:, kv_off + KV_LORA:], tables)
    q = jnp.dot(c_q.astype(BF16), wuq_ref[...], preferred_element_type=F32)
    kn = jnp.dot(c_kv.astype(BF16), wk_ref[...], preferred_element_type=F32)
    qscale = (QK_NOPE + QK_ROPE) ** -0.5 * LOG2E
    for h in range(MLA_HEADS):
        sl = slice(h * MLA_SLAB, (h + 1) * MLA_SLAB)
        q_ref[:, sl] = (_rope_slab(q[:, sl], tables) * qscale).astype(BF16)
        k_ref[:, sl] = (kn[:, sl] + kr).astype(BF16)
    v_ref[...] = jnp.dot(c_kv.astype(BF16), wv_ref[...], preferred_element_type=F32).astype(BF16)


def _mla_proj(x2d, pos2d, invf, w, gq, wuq, gkv, wk, wv):
    t_tokens = x2d.shape[0]
    tm = PROJ_TM
    row_spec = lambda width: pl.BlockSpec((tm, width), lambda i: (i, 0))
    full = lambda a: pl.BlockSpec(a.shape, lambda i: (0, 0))
    slabs = MLA_HEADS * MLA_SLAB
    return pl.pallas_call(
        _mla_proj_kernel,
        grid=(t_tokens // tm,),
        in_specs=[row_spec(D_MODEL), row_spec(1), full(invf), full(w), full(gq), full(wuq),
                  full(gkv), full(wk), full(wv)],
        out_specs=[row_spec(slabs), row_spec(MEM_WIDTH), row_spec(slabs), row_spec(MLA_V_WIDTH)],
        out_shape=[jax.ShapeDtypeStruct((t_tokens, slabs), BF16),
                   jax.ShapeDtypeStruct((t_tokens, MEM_WIDTH), BF16),
                   jax.ShapeDtypeStruct((t_tokens, slabs), BF16),
                   jax.ShapeDtypeStruct((t_tokens, MLA_V_WIDTH), BF16)],
        compiler_params=_cparams(1),
        name="mla_proj",
    )(x2d, pos2d, invf, w, gq, wuq, gkv, wk, wv)


def _softmax_chunk(h, s, vh, m_ref, l_ref, acc_ref):
    m_prev = m_ref[h]
    m_new = jnp.maximum(m_prev, jnp.max(s, axis=1, keepdims=True))
    alpha = jnp.exp2(m_prev - m_new)
    p = jnp.exp2(s - m_new)
    l_ref[h] = alpha * l_ref[h] + jnp.sum(p, axis=1, keepdims=True)
    acc_ref[h] = acc_ref[h] * alpha + jnp.dot(p.astype(BF16), vh, preferred_element_type=F32)
    m_ref[h] = m_new


def _init_state(m_ref, l_ref, acc_ref):
    m_ref[...] = jnp.full(m_ref.shape, NEG_INF, F32)
    l_ref[...] = jnp.zeros(l_ref.shape, F32)
    acc_ref[...] = jnp.zeros(acc_ref.shape, F32)


def _head_operands(h, q, kc, vc, q_slab):
    lane = lax.broadcasted_iota(jnp.int32, (1, LANES), 1)
    mine = (lane < HEAD_DIM) if h == 0 else (lane >= HEAD_DIM)
    vh = jnp.where(mine, vc, jnp.zeros_like(vc))
    if q_slab:
        sl = slice(h * MLA_SLAB, (h + 1) * MLA_SLAB)
        return q[:, sl], kc[:, sl], vh
    return q, jnp.where(mine, kc, jnp.zeros_like(kc)), vh


def _finish(o_ref, l_ref, acc_ref):
    lane = lax.broadcasted_iota(jnp.int32, (1, LANES), 1)
    o0 = acc_ref[0] / l_ref[0]
    o1 = acc_ref[1] / l_ref[1]
    o_ref[0] = jnp.where(lane < HEAD_DIM, o0, o1).astype(o_ref.dtype)


def _causal_attn_kernel(*refs, tq, fox, q_slab):
    if fox:
        q_ref, k_ref, v_ref, cq_ref, ck_ref, o_ref, m_ref, l_ref, acc_ref = refs
    else:
        q_ref, k_ref, v_ref, o_ref, m_ref, l_ref, acc_ref = refs
    pair = pl.program_id(1)
    i = pl.program_id(2)
    _init_state(m_ref, l_ref, acc_ref)
    q = q_ref[0]
    if fox:
        lane = lax.broadcasted_iota(jnp.int32, (1, LANES), 1)
        cq_blk = cq_ref[0]
        cq = [jnp.sum(jnp.where(lane == 2 * pair + h, cq_blk, 0.0), axis=1, keepdims=True)
              for h in range(2)]

    def chunk(j, causal_mask):
        off = pl.multiple_of(j * tq, tq)
        kc = k_ref[0, pl.ds(off, tq), :]
        vc = v_ref[0, pl.ds(off, tq), :]
        for h in range(2):
            qh, kh, vh = _head_operands(h, q, kc, vc, q_slab)
            s = lax.dot_general(qh, kh, (((1,), (1,)), ((), ())), preferred_element_type=F32)
            if fox:
                s = s + cq[h] - ck_ref[0, pl.ds(2 * pair + h, 1), pl.ds(off, tq)]
            if causal_mask is not None:
                s = jnp.where(causal_mask, s, NEG_INF)
            _softmax_chunk(h, s, vh, m_ref, l_ref, acc_ref)

    def body(j, carry):
        chunk(j, None)
        return carry

    lax.fori_loop(0, i, body, 0)
    row = lax.broadcasted_iota(jnp.int32, (tq, tq), 0)
    col = lax.broadcasted_iota(jnp.int32, (tq, tq), 1)
    chunk(i, col <= row)
    _finish(o_ref, l_ref, acc_ref)


def _causal_attention(q, k, v, cum, cumt, *, fox, q_slab):
    nb, seq, _ = q.shape
    tq = ATT_TQ
    n_pairs = FOX_HEADS // 2
    qw = 2 * MLA_SLAB if q_slab else LANES
    in_specs = [pl.BlockSpec((1, tq, qw), lambda b, p, i: (b, i, p)),
                pl.BlockSpec((1, seq, qw), lambda b, p, i: (b, 0, p)),
                pl.BlockSpec((1, seq, LANES), lambda b, p, i: (b, 0, p))]
    args = [q, k, v]
    if fox:
        in_specs += [pl.BlockSpec((1, tq, LANES), lambda b, p, i: (b, i, 0)),
                     pl.BlockSpec((1, 16, seq), lambda b, p, i: (b, 0, 0))]
        args += [cum, cumt]
    return pl.pallas_call(
        functools.partial(_causal_attn_kernel, tq=tq, fox=fox, q_slab=q_slab),
        grid=(nb, n_pairs, seq // tq),
        in_specs=in_specs,
        out_specs=pl.BlockSpec((1, tq, LANES), lambda b, p, i: (b, i, p)),
        out_shape=jax.ShapeDtypeStruct((nb, seq, n_pairs * LANES), BF16),
        scratch_shapes=[pltpu.VMEM((2, tq, 1), F32), pltpu.VMEM((2, tq, 1), F32),
                        pltpu.VMEM((2, tq, LANES), F32)],
        compiler_params=_cparams(3),
        name="fox_attention" if fox else "mla_attention",
    )(*args)


def _mem_attn_kernel(q_ref, k_ref, v_ref, o_ref, m_ref, l_ref, acc_ref):
    _init_state(m_ref, l_ref, acc_ref)
    q = q_ref[0]
    kc = k_ref[0]
    vc = v_ref[0]
    for h in range(2):
        qh, kh, vh = _head_operands(h, q, kc, vc, False)
        s = lax.dot_general(qh, kh, (((1,), (1,)), ((), ())), preferred_element_type=F32)
        _softmax_chunk(h, s, vh, m_ref, l_ref, acc_ref)
    _finish(o_ref, l_ref, acc_ref)


def _memory_attention(mq, mk, mv):
    nb, seq, _ = mq.shape
    tq = ATT_TQ
    n_pairs = MEM_HEADS // 2
    return pl.pallas_call(
        _mem_attn_kernel,
        grid=(nb, n_pairs, seq // tq),
        in_specs=[pl.BlockSpec((1, tq, LANES), lambda b, p, i: (b, i, p)),
                  pl.BlockSpec((1, N_MEM, LANES), lambda b, p, i: (b, 0, p)),
                  pl.BlockSpec((1, N_MEM, LANES), lambda b, p, i: (b, 0, p))],
        out_specs=pl.BlockSpec((1, tq, LANES), lambda b, p, i: (b, i, p)),
        out_shape=jax.ShapeDtypeStruct((nb, seq, MEM_WIDTH), BF16),
        scratch_shapes=[pltpu.VMEM((2, tq, 1), F32), pltpu.VMEM((2, tq, 1), F32),
                        pltpu.VMEM((2, tq, LANES), F32)],
        compiler_params=_cparams(3),
        name="memory_attention",
    )(mq, mk, mv)


def _post_attn_kernel(x_ref, att_ref, memo_ref, wa_ref, wm_ref, g_ref, b_ref, wr_ref, br_ref,
                      x1_ref, x1b_ref, idx_ref, rank_ref, gate_ref, cnt_ref, carry_ref):
    t = pl.program_id(0)

    @pl.when(t == 0)
    def _():
        carry_ref[...] = jnp.zeros_like(carry_ref)

    tm = x_ref.shape[0]
    mix = (jnp.dot(att_ref[...], wa_ref[...], preferred_element_type=F32)
           + jnp.dot(memo_ref[...], wm_ref[...], preferred_element_type=F32))
    x1 = _layer_norm(DEEPNORM_ALPHA * x_ref[...] + mix, g_ref[...], b_ref[...])
    x1_ref[...] = x1
    x1b = x1.astype(BF16)
    x1b_ref[...] = x1b
    logits = jnp.dot(x1b, wr_ref[...], preferred_element_type=F32) + br_ref[...]
    lane = lax.broadcasted_iota(jnp.int32, (tm, LANES), 1)
    work = jnp.where(lane < N_EXPERTS, logits, -jnp.inf)
    idxs, vals = [], []
    onehot = jnp.zeros((tm, LANES), F32)
    for _ in range(TOP_K):
        best = jnp.max(work, axis=1, keepdims=True)
        where_best = jnp.min(jnp.where(work == best, lane, LANES), axis=1, keepdims=True)
        hit = lane == where_best
        onehot = jnp.where(hit, 1.0, onehot)
        work = jnp.where(hit, -jnp.inf, work)
        idxs.append(where_best)
        vals.append(best)
    exps = [jnp.exp(v - vals[0]) for v in vals]
    denom = exps[0] + exps[1] + exps[2] + exps[3]
    row = lax.broadcasted_iota(jnp.int32, (tm, tm), 0)
    col = lax.broadcasted_iota(jnp.int32, (tm, tm), 1)
    strict = jnp.where(row > col, 1.0, 0.0).astype(BF16)
    before = jnp.dot(strict, onehot.astype(BF16), preferred_element_type=F32) + carry_ref[...]
    idx_out = jnp.zeros((tm, LANES), jnp.int32)
    rank_out = jnp.zeros((tm, LANES), jnp.int32)
    gate_out = jnp.zeros((tm, LANES), F32)
    for r in range(TOP_K):
        rank_r = jnp.sum(jnp.where(lane == idxs[r], before, 0.0), axis=1, keepdims=True)
        idx_out = jnp.where(lane == r, idxs[r], idx_out)
        rank_out = jnp.where(lane == r, rank_r.astype(jnp.int32), rank_out)
        gate_out = jnp.where(lane == r, exps[r] / denom, gate_out)
    idx_ref[...] = idx_out
    rank_ref[...] = rank_out
    gate_ref[...] = gate_out
    total = carry_ref[...] + jnp.sum(onehot, axis=0, keepdims=True)
    carry_ref[...] = total
    cnt_ref[...] = jnp.broadcast_to(total, cnt_ref.shape)


def _post_attn(x2d, att, memo, wa, wm, g, b, wr, br):
    t_tokens = x2d.shape[0]
    tm = PROJ_TM
    row_spec = lambda width: pl.BlockSpec((tm, width), lambda i: (i, 0))
    full = lambda a: pl.BlockSpec(a.shape, lambda i: (0, 0))
    return pl.pallas_call(
        _post_attn_kernel,
        grid=(t_tokens // tm,),
        in_specs=[row_spec(D_MODEL), row_spec(att.shape[1]), row_spec(MEM_WIDTH),
                  full(wa), full(wm), full(g), full(b), full(wr), full(br)],
        out_specs=[row_spec(D_MODEL), row_spec(D_MODEL), row_spec(LANES), row_spec(LANES),
                   row_spec(LANES), pl.BlockSpec((8, LANES), lambda i: (0, 0))],
        out_shape=[jax.ShapeDtypeStruct((t_tokens, D_MODEL), F32),
                   jax.ShapeDtypeStruct((t_tokens, D_MODEL), BF16),
                   jax.ShapeDtypeStruct((t_tokens, LANES), jnp.int32),
                   jax.ShapeDtypeStruct((t_tokens, LANES), jnp.int32),
                   jax.ShapeDtypeStruct((t_tokens, LANES), F32),
                   jax.ShapeDtypeStruct((8, LANES), F32)],
        scratch_shapes=[pltpu.VMEM((1, LANES), F32)],
        compiler_params=_cparams(1),
        name="outproj_ln_router",
    )(x2d, att, memo, wa, wm, g, b, wr, br)


def _expert_kernel(be_ref, nu_ref, x_ref, wgu_ref, bgu_ref, wdn_ref, bdn_ref, o_ref, *, n_chunks):
    blk = pl.program_id(0)

    @pl.when(blk < nu_ref[0])
    def _():
        x = x_ref[...]
        cw = D_EXPERT // n_chunks
        acc = jnp.zeros(o_ref.shape, F32)
        for c in range(n_chunks):
            gs = slice(c * cw, (c + 1) * cw)
            us = slice(D_EXPERT + c * cw, D_EXPERT + (c + 1) * cw)
            g = jnp.dot(x, wgu_ref[0, :, gs], preferred_element_type=F32) + bgu_ref[0, :, gs]
            u = jnp.dot(x, wgu_ref[0, :, us], preferred_element_type=F32) + bgu_ref[0, :, us]
            g = jnp.minimum(g, SWIGLU_LIMIT)
            u = jnp.clip(u, -SWIGLU_LIMIT, SWIGLU_LIMIT)
            hmid = (u + 1.0) * (g * jax.nn.sigmoid(SWIGLU_ALPHA * g))
            acc = acc + jnp.dot(hmid.astype(BF16), wdn_ref[0, gs, :], preferred_element_type=F32)
        o_ref[...] = acc + bdn_ref[0]

    @pl.when(blk >= nu_ref[0])
    def _():
        o_ref[...] = jnp.zeros_like(o_ref)


def _experts(block_expert, n_used, xb, wgu, bgu, wdn, bdn):
    p_rows = xb.shape[0]
    tm = MOE_TM
    grid_spec = pltpu.PrefetchScalarGridSpec(
        num_scalar_prefetch=2,
        grid=(p_rows // tm,),
        in_specs=[pl.BlockSpec((tm, D_MODEL), lambda i, be, nu: (i, 0)),
                  pl.BlockSpec((1, D_MODEL, 2 * D_EXPERT), lambda i, be, nu: (be[i], 0, 0)),
                  pl.BlockSpec((1, 1, 2 * D_EXPERT), lambda i, be, nu: (be[i], 0, 0)),
                  pl.BlockSpec((1, D_EXPERT, D_MODEL), lambda i, be, nu: (be[i], 0, 0)),
                  pl.BlockSpec((1, 1, D_MODEL), lambda i, be, nu: (be[i], 0, 0))],
        out_specs=pl.BlockSpec((tm, D_MODEL), lambda i, be, nu: (i, 0)),
    )
    return pl.pallas_call(
        functools.partial(_expert_kernel, n_chunks=4),
        grid_spec=grid_spec,
        out_shape=jax.ShapeDtypeStruct((p_rows, D_MODEL), F32),
        compiler_params=_cparams(1),
        name="experts",
    )(block_expert, n_used, xb, wgu, bgu, wdn, bdn)


def _combine_kernel(x1_ref, yg_ref, gate_ref, g_ref, b_ref, o_ref):
    gates = gate_ref[...]
    ffn = yg_ref[:, 0, :] * gates[:, 0:1]
    for r in range(1, TOP_K):
        ffn = ffn + yg_ref[:, r, :] * gates[:, r:r + 1]
    o_ref[...] = _layer_norm(DEEPNORM_ALPHA * x1_ref[...] + ffn, g_ref[...], b_ref[...])


def _combine(x1, yg, gates, g, b):
    t_tokens = x1.shape[0]
    tm = 256
    return pl.pallas_call(
        _combine_kernel,
        grid=(t_tokens // tm,),
        in_specs=[pl.BlockSpec((tm, D_MODEL), lambda i: (i, 0)),
                  pl.BlockSpec((tm, TOP_K, D_MODEL), lambda i: (i, 0, 0)),
                  pl.BlockSpec((tm, LANES), lambda i: (i, 0)),
                  pl.BlockSpec((1, D_MODEL), lambda i: (0, 0)),
                  pl.BlockSpec((1, D_MODEL), lambda i: (0, 0))],
        out_specs=pl.BlockSpec((tm, D_MODEL), lambda i: (i, 0)),
        out_shape=jax.ShapeDtypeStruct((t_tokens, D_MODEL), F32),
        compiler_params=_cparams(1),
        name="combine_ln",
    )(x1, yg, gates, g, b)


def _moe(x1, x1b, idx, rank, gates, counts, wgu, bgu, wdn, bdn, g, b):
    t_tokens = x1.shape[0]
    tm = MOE_TM
    counts = counts[0, :N_EXPERTS].astype(jnp.int32)
    padded = ((counts + tm - 1) // tm) * tm
    pend = jnp.cumsum(padded)
    pstart = pend - padded
    idx4 = idx[:, :TOP_K]
    dest = (pstart[idx4] + rank[:, :TOP_K]).reshape(-1)
    n_blocks = (t_tokens * TOP_K) // tm + N_EXPERTS
    p_rows = n_blocks * tm
    block_expert = jnp.minimum(
        jnp.searchsorted(pend, jnp.arange(n_blocks, dtype=jnp.int32) * tm, side="right"),
        N_EXPERTS - 1).astype(jnp.int32)
    n_used = (pend[-1:] // tm).astype(jnp.int32)
    tok = jnp.arange(t_tokens * TOP_K, dtype=jnp.int32) // TOP_K
    slot_tok = jnp.full((p_rows,), t_tokens, jnp.int32).at[dest].set(tok)
    x_pad = jnp.concatenate([x1b, jnp.zeros((1, D_MODEL), x1b.dtype)], axis=0)
    xb = x_pad[slot_tok]
    y = _experts(block_expert, n_used, xb, wgu, bgu, wdn, bdn)
    yg = y[dest].reshape(t_tokens, TOP_K, D_MODEL)
    return _combine(x1, yg, gates, g, b)


def _row(v, width=None):
    v = v.astype(F32).reshape(1, -1)
    if width is not None and v.shape[1] < width:
        v = jnp.pad(v, ((0, 0), (0, width - v.shape[1])))
    return v


def _pad_cols(w, width):
    return jnp.pad(w, ((0, 0), (0, width - w.shape[1])))


def kernel(x, mem, positions, a_w_in, a_b_f, a_w_out, b_w_in, b_g_q, b_w_uq, b_w_out,
           kv_w_dkv, kv_g, kv_w_ukv, mem_w_kv, ln_g, ln_b,
           moe_w_r, moe_b_r, moe_w_gu, moe_b_gu, moe_w_dn, moe_b_dn):
    nb, seq, d = x.shape
    t_tokens = nb * seq
    n_a = a_w_in.shape[0]
    x2d = x.reshape(t_tokens, d)
    mem2d = mem.reshape(nb * N_MEM, d)
    pos2d = positions.reshape(t_tokens, 1)
    half = QK_ROPE // 2
    inv_freq = ROPE_THETA ** (-jnp.arange(half, dtype=F32) * 2.0 / QK_ROPE)
    invf = jnp.zeros((1, LANES), F32)
    invf = invf.at[0, QK_NOPE:QK_NOPE + half].set(inv_freq)
    invf = invf.at[0, QK_NOPE + half:QK_NOPE + QK_ROPE].set(inv_freq)

    shared_kv = None
    for l in range(DEPTH):
        mkv = _matmul(mem2d, mem_w_kv[l].astype(BF16), BF16, tm=nb * N_MEM)
        mk = mkv[:, :MEM_WIDTH].reshape(nb, N_MEM, MEM_WIDTH)
        mv = mkv[:, MEM_WIDTH:].reshape(nb, N_MEM, MEM_WIDTH)
        if l < n_a:
            w_in = a_w_in[l]
            w = jnp.concatenate([w_in[:, :3 * FOX_WIDTH],
                                 _pad_cols(w_in[:, 3 * FOX_WIDTH:3 * FOX_WIDTH + FOX_HEADS], LANES),
                                 w_in[:, 3 * FOX_WIDTH + FOX_HEADS:]], axis=1).astype(BF16)
            q, k, v, mq, cum, cumt = _fox_proj(x2d, w, _row(a_b_f[l], LANES), seq)
            att = _causal_attention(q.reshape(nb, seq, -1), k.reshape(nb, seq, -1),
                                    v.reshape(nb, seq, -1), cum.reshape(nb, seq, LANES), cumt,
                                    fox=True, q_slab=False)
            w_out = a_w_out[l]
        else:
            bl = l - n_a
            w = jnp.concatenate([b_w_in[bl], kv_w_dkv[:, :KV_LORA],
                                 jnp.zeros((d, QK_NOPE), F32), kv_w_dkv[:, KV_LORA:],
                                 jnp.zeros((d, LANES - QK_NOPE - QK_ROPE), F32)], axis=1).astype(BF16)
            wuq = b_w_uq[bl].reshape(Q_LORA, MLA_HEADS, QK_NOPE + QK_ROPE)
            wuq = jnp.pad(wuq, ((0, 0), (0, 0), (0, MLA_SLAB - QK_NOPE - QK_ROPE)))
            wuq = wuq.reshape(Q_LORA, MLA_HEADS * MLA_SLAB).astype(BF16)
            wukv = kv_w_ukv.reshape(KV_LORA, MLA_HEADS, QK_NOPE + V_DIM)
            wk = jnp.pad(wukv[:, :, :QK_NOPE], ((0, 0), (0, 0), (0, MLA_SLAB - QK_NOPE)))
            wk = wk.reshape(KV_LORA, MLA_HEADS * MLA_SLAB).astype(BF16)
            wv = wukv[:, :, QK_NOPE:].reshape(KV_LORA, MLA_V_WIDTH).astype(BF16)
            q, mq, k_new, v_new = _mla_proj(x2d, pos2d, invf, w, _row(b_g_q[bl]), wuq,
                                            _row(kv_g), wk, wv)
            if shared_kv is None:
                shared_kv = (k_new.reshape(nb, seq, -1), v_new.reshape(nb, seq, -1))
            att = _causal_attention(q.reshape(nb, seq, -1), shared_kv[0], shared_kv[1], None, None,
                                    fox=False, q_slab=True)
            w_out = b_w_out[bl]
        memo = _memory_attention(mq.reshape(nb, seq, MEM_WIDTH), mk, mv)
        n_att = w_out.shape[0] - MEM_WIDTH
        x1, x1b, idx, rank, gates, counts = _post_attn(
            x2d, att.reshape(t_tokens, -1), memo.reshape(t_tokens, MEM_WIDTH),
            w_out[:n_att].astype(BF16), w_out[n_att:].astype(BF16),
            _row(ln_g[l, 0]), _row(ln_b[l, 0]),
            _pad_cols(moe_w_r[l], LANES).astype(BF16), _row(moe_b_r[l], LANES))
        x2d = _moe(x1, x1b, idx, rank, gates, counts,
                   moe_w_gu[l].astype(BF16), moe_b_gu[l].reshape(N_EXPERTS, 1, -1),
                   moe_w_dn[l].astype(BF16), moe_b_dn[l].reshape(N_EXPERTS, 1, -1),
                   _row(ln_g[l, 1]), _row(ln_b[l, 1]))
    return x2d.reshape(nb, seq, d)
```

```python
import functools
import math

import jax
import jax.numpy as jnp
from jax import lax
from jax.experimental import pallas as pl
from jax.experimental.pallas import tpu as pltpu

F32 = jnp.float32
BF16 = jnp.bfloat16

D_MODEL = 1024
DEPTH = 2
N_MEM = 256
HEAD_DIM = 64
FOX_HEADS = 12
MEM_HEADS = 4
MLA_HEADS = 12
Q_LORA = 384
KV_LORA = 256
QK_NOPE = 64
QK_ROPE = 32
V_DIM = 64
ROPE_THETA = 10000.0
N_EXPERTS = 32
TOP_K = 4
D_EXPERT = D_MODEL
SWIGLU_LIMIT = 7.0
SWIGLU_ALPHA = 1.702
LN_EPS = 1e-5
RMS_EPS = 1e-6
NEG_INF = -1e30
DEEPNORM_ALPHA = (2 * DEPTH) ** 0.25
FOX_WIDTH = FOX_HEADS * HEAD_DIM
MEM_WIDTH = MEM_HEADS * HEAD_DIM
MLA_V_WIDTH = MLA_HEADS * V_DIM

LANES = 128
LOG2E = math.log2(math.e)
VMEM_LIMIT = 48 * 1024 * 1024

PROJ_TM = 512
ATT_TQ = 512
MOE_TM = 256
MLA_SLAB = LANES


def _cparams(n_axes):
    return pltpu.CompilerParams(dimension_semantics=("arbitrary",) * n_axes,
                                vmem_limit_bytes=VMEM_LIMIT)


def _split3(x):
    hi = x.astype(BF16)
    r1 = x - hi.astype(F32)
    mid = r1.astype(BF16)
    lo = (r1 - mid.astype(F32)).astype(BF16)
    return hi, mid, lo


def _layer_norm(y, g, b):
    mu = jnp.mean(y, axis=-1, keepdims=True)
    yc = y - mu
    var = jnp.mean(yc * yc, axis=-1, keepdims=True)
    return yc * lax.rsqrt(var + LN_EPS) * g + b


def _rms_norm(y, g):
    return y * lax.rsqrt(jnp.mean(y * y, axis=-1, keepdims=True) + RMS_EPS) * g


def _mm_kernel(a_ref, b_ref, o_ref):
    o_ref[...] = jnp.dot(a_ref[...].astype(BF16), b_ref[...],
                         preferred_element_type=F32).astype(o_ref.dtype)


def _matmul(a, b, out_dtype, tm):
    m, k = a.shape
    n = b.shape[1]
    return pl.pallas_call(
        _mm_kernel,
        grid=(m // tm,),
        in_specs=[pl.BlockSpec((tm, k), lambda i: (i, 0)),
                  pl.BlockSpec((k, n), lambda i: (0, 0))],
        out_specs=pl.BlockSpec((tm, n), lambda i: (i, 0)),
        out_shape=jax.ShapeDtypeStruct((m, n), out_dtype),
        compiler_params=_cparams(1),
        name="matmul",
    )(a, b)


def _store_transposed(dst_ref, val):
    for s in range(val.shape[1] // LANES):
        sl = slice(s * LANES, (s + 1) * LANES)
        dst_ref[0, sl, :] = val[:, sl].T.astype(dst_ref.dtype)


def _fox_proj_kernel(x_ref, w_ref, bf_ref, seg_ref, qt_ref, k_ref, vt_ref, mq_ref, cum_ref,
                     cumt_ref, cfirst_ref, clast_ref, knorm_ref, carry_ref, *, tiles_per_batch):
    t = pl.program_id(0)

    @pl.when(t % tiles_per_batch == 0)
    def _():
        carry_ref[...] = jnp.zeros_like(carry_ref)

    tm = x_ref.shape[0]
    proj = jnp.dot(x_ref[...].astype(BF16), w_ref[...], preferred_element_type=F32)
    qscale = HEAD_DIM ** -0.5 * LOG2E
    _store_transposed(qt_ref, proj[:, :FOX_WIDTH] * qscale)
    kb = proj[:, FOX_WIDTH:2 * FOX_WIDTH].astype(BF16)
    k_ref[...] = kb
    _store_transposed(vt_ref, proj[:, 2 * FOX_WIDTH:3 * FOX_WIDTH])
    kf = kb.astype(F32)
    ksq = jnp.dot((kf * kf).astype(BF16), seg_ref[...], preferred_element_type=F32)
    knorm_ref[0] = jnp.sqrt(jnp.max(ksq, axis=0, keepdims=True))
    f = proj[:, 3 * FOX_WIDTH:3 * FOX_WIDTH + LANES] + bf_ref[...]
    mq_ref[...] = (proj[:, 3 * FOX_WIDTH + LANES:] * qscale).astype(BF16)
    log_f = jnp.minimum(f, 0.0) - jnp.log1p(jnp.exp(-jnp.abs(f)))
    row = lax.broadcasted_iota(jnp.int32, (tm, tm), 0)
    col = lax.broadcasted_iota(jnp.int32, (tm, tm), 1)
    tri = jnp.where(row >= col, 1.0, 0.0).astype(BF16)
    hi, mid, lo = _split3(log_f)
    cum = (jnp.dot(tri, hi, preferred_element_type=F32)
           + jnp.dot(tri, mid, preferred_element_type=F32)
           + jnp.dot(tri, lo, preferred_element_type=F32)) + carry_ref[...]
    carry_ref[...] = cum[tm - 1:tm, :]
    cum2 = cum * LOG2E
    cum_ref[...] = cum2
    cumt_ref[0] = cum2.T[:16, :]
    cfirst_ref[0] = cum2[0:1, :]
    clast_ref[0] = cum2[tm - 1:tm, :]


def _fox_proj(x2d, w, bf, seq):
    t_tokens = x2d.shape[0]
    tm = PROJ_TM
    nb = t_tokens // seq
    n = w.shape[1]
    tiles_per_batch = seq // tm
    n_tiles = t_tokens // tm
    seg = (jnp.arange(FOX_WIDTH)[:, None] // HEAD_DIM == jnp.arange(LANES)[None, :]).astype(BF16)
    row_spec = lambda width: pl.BlockSpec((tm, width), lambda i: (i, 0))
    t_spec = lambda rows: pl.BlockSpec(
        (1, rows, tm), lambda i: (i // tiles_per_batch, 0, i % tiles_per_batch))
    stat_spec = pl.BlockSpec((1, 1, LANES), lambda i: (i, 0, 0))
    stat_shape = jax.ShapeDtypeStruct((n_tiles, 1, LANES), F32)
    return pl.pallas_call(
        functools.partial(_fox_proj_kernel, tiles_per_batch=tiles_per_batch),
        grid=(n_tiles,),
        in_specs=[row_spec(D_MODEL),
                  pl.BlockSpec((D_MODEL, n), lambda i: (0, 0)),
                  pl.BlockSpec((1, LANES), lambda i: (0, 0)),
                  pl.BlockSpec((FOX_WIDTH, LANES), lambda i: (0, 0))],
        out_specs=[t_spec(FOX_WIDTH), row_spec(FOX_WIDTH), t_spec(FOX_WIDTH),
                   row_spec(MEM_WIDTH), row_spec(LANES), t_spec(16),
                   stat_spec, stat_spec, stat_spec],
        out_shape=[jax.ShapeDtypeStruct((nb, FOX_WIDTH, seq), BF16),
                   jax.ShapeDtypeStruct((t_tokens, FOX_WIDTH), BF16),
                   jax.ShapeDtypeStruct((nb, FOX_WIDTH, seq), BF16),
                   jax.ShapeDtypeStruct((t_tokens, MEM_WIDTH), BF16),
                   jax.ShapeDtypeStruct((t_tokens, LANES), F32),
                   jax.ShapeDtypeStruct((nb, 16, seq), F32),
                   stat_shape, stat_shape, stat_shape],
        scratch_shapes=[pltpu.VMEM((1, LANES), F32)],
        compiler_params=_cparams(1),
        name="fox_proj",
    )(x2d, w, bf, seg)


def _rope_tables(pos_ref, invf_ref):
    ang = pos_ref[...].astype(F32) * invf_ref[...]
    cos = jnp.cos(ang)
    sin = jnp.sin(ang)
    lane = lax.broadcasted_iota(jnp.int32, ang.shape, 1)
    half = QK_ROPE // 2
    in_x1 = (lane >= QK_NOPE) & (lane < QK_NOPE + half)
    in_x2 = (lane >= QK_NOPE + half) & (lane < QK_NOPE + QK_ROPE)
    c_tab = jnp.where(lane < QK_NOPE, 1.0, jnp.where(in_x1 | in_x2, cos, 0.0))
    s_up = jnp.where(in_x2, sin, 0.0)
    s_dn = jnp.where(in_x1, -sin, 0.0)
    return c_tab, s_up, s_dn


def _rope_slab(slab, tables):
    c_tab, s_up, s_dn = tables
    half = QK_ROPE // 2
    up = pltpu.roll(slab, half, 1)
    dn = pltpu.roll(slab, LANES - half, 1)
    return slab * c_tab + up * s_up + dn * s_dn


def _mla_proj_kernel(x_ref, pos_ref, invf_ref, w_ref, gq_ref, wuq_ref, gkv_ref, wk_ref, wv_ref,
                     qt_ref, mq_ref, k_ref, vt_ref):
    proj = jnp.dot(x_ref[...].astype(BF16), w_ref[...], preferred_element_type=F32)
    tables = _rope_tables(pos_ref, invf_ref)
    c_q = _rms_norm(proj[:, :Q_LORA], gq_ref[...])
    mq_ref[...] = (proj[:, Q_LORA:Q_LORA + MEM_WIDTH] * (HEAD_DIM ** -0.5 * LOG2E)).astype(BF16)
    kv_off = Q_LORA + MEM_WIDTH
    c_kv = _rms_norm(proj[:, kv_off:kv_off + KV_LORA], gkv_ref[...])
    kr = _rope_slab(proj[:, kv_off + KV_LORA:], tables)
    q = jnp.dot(c_q.astype(BF16), wuq_ref[...], preferred_element_type=F32)
    kn = jnp.dot(c_kv.astype(BF16), wk_ref[...], preferred_element_type=F32)
    qscale = (QK_NOPE + QK_ROPE) ** -0.5 * LOG2E
    for h in range(MLA_HEADS):
        sl = slice(h * MLA_SLAB, (h + 1) * MLA_SLAB)
        qt_ref[0, sl, :] = (_rope_slab(q[:, sl], tables) * qscale).T.astype(BF16)
        k_ref[:, sl] = (kn[:, sl] + kr).astype(BF16)
    _store_transposed(vt_ref, jnp.dot(c_kv.astype(BF16), wv_ref[...], preferred_element_type=F32))


def _mla_proj(x2d, pos2d, invf, w, gq, wuq, gkv, wk, wv, seq):
    t_tokens = x2d.shape[0]
    tm = PROJ_TM
    nb = t_tokens // seq
    tiles_per_batch = seq // tm
    row_spec = lambda width: pl.BlockSpec((tm, width), lambda i: (i, 0))
    t_spec = lambda rows: pl.BlockSpec(
        (1, rows, tm), lambda i: (i // tiles_per_batch, 0, i % tiles_per_batch))
    full = lambda a: pl.BlockSpec(a.shape, lambda i: (0, 0))
    slabs = MLA_HEADS * MLA_SLAB
    return pl.pallas_call(
        _mla_proj_kernel,
        grid=(t_tokens // tm,),
        in_specs=[row_spec(D_MODEL), row_spec(1), full(invf), full(w), full(gq), full(wuq),
                  full(gkv), full(wk), full(wv)],
        out_specs=[t_spec(slabs), row_spec(MEM_WIDTH), row_spec(slabs), t_spec(MLA_V_WIDTH)],
        out_shape=[jax.ShapeDtypeStruct((nb, slabs, seq), BF16),
                   jax.ShapeDtypeStruct((t_tokens, MEM_WIDTH), BF16),
                   jax.ShapeDtypeStruct((t_tokens, slabs), BF16),
                   jax.ShapeDtypeStruct((nb, MLA_V_WIDTH, seq), BF16)],
        compiler_params=_cparams(1),
        name="mla_proj",
    )(x2d, pos2d, invf, w, gq, wuq, gkv, wk, wv)


def _softmax_chunk(h, s, vh, m_ref, l_ref, acc_ref):
    m_prev = m_ref[h]
    m_new = jnp.maximum(m_prev, jnp.max(s, axis=1, keepdims=True))
    alpha = jnp.exp2(m_prev - m_new)
    p = jnp.exp2(s - m_new)
    l_ref[h] = alpha * l_ref[h] + jnp.sum(p, axis=1, keepdims=True)
    acc_ref[h] = acc_ref[h] * alpha + jnp.dot(p.astype(BF16), vh, preferred_element_type=F32)
    m_ref[h] = m_new


def _init_state(m_ref, l_ref, acc_ref):
    m_ref[...] = jnp.full(m_ref.shape, NEG_INF, F32)
    l_ref[...] = jnp.zeros(l_ref.shape, F32)
    acc_ref[...] = jnp.zeros(acc_ref.shape, F32)


def _head_operands(h, q, kc, vc, q_slab):
    lane = lax.broadcasted_iota(jnp.int32, (1, LANES), 1)
    mine = (lane < HEAD_DIM) if h == 0 else (lane >= HEAD_DIM)
    vh = jnp.where(mine, vc, jnp.zeros_like(vc))
    if q_slab:
        sl = slice(h * MLA_SLAB, (h + 1) * MLA_SLAB)
        return q[:, sl], kc[:, sl], vh
    return q, jnp.where(mine, kc, jnp.zeros_like(kc)), vh


def _finish(o_ref, l_ref, acc_ref):
    lane = lax.broadcasted_iota(jnp.int32, (1, LANES), 1)
    o0 = acc_ref[0] / l_ref[0]
    o1 = acc_ref[1] / l_ref[1]
    o_ref[0] = jnp.where(lane < HEAD_DIM, o0, o1).astype(o_ref.dtype)


SKIP_LOG2 = -160.0
NORM_SLACK = 1.02


def _causal_attn_kernel(*refs, tq, n_tiles, fox, slab):
    if fox:
        (cf_ref, cl_ref, kn_ref, qt_ref, k_ref, vt_ref, cq_ref, ck_ref,
         o_ref, m_ref, l_ref, acc_ref) = refs
    else:
        qt_ref, k_ref, vt_ref, o_ref, m_ref, l_ref, acc_ref = refs
    b = pl.program_id(0)
    pair = pl.program_id(1)
    i = pl.program_id(2)
    m_ref[...] = jnp.full(m_ref.shape, NEG_INF, F32)
    l_ref[...] = jnp.zeros(l_ref.shape, F32)
    acc_ref[...] = jnp.zeros(acc_ref.shape, F32)
    qt = qt_ref[0]
    if slab:
        qth = [qt[:MLA_SLAB], qt[MLA_SLAB:]]
    else:
        rowi = lax.broadcasted_iota(jnp.int32, qt.shape, 0)
        zero = jnp.zeros_like(qt)
        qth = [jnp.where(rowi < HEAD_DIM, qt, zero), jnp.where(rowi >= HEAD_DIM, qt, zero)]
    if fox:
        lane = lax.broadcasted_iota(jnp.int32, (1, LANES), 1)
        cq = [cq_ref[0, pl.ds(2 * pair + h, 1), :] for h in range(2)]

    def chunk(j, causal_mask):
        off = pl.multiple_of(j * tq, tq)
        kc = k_ref[0, pl.ds(off, tq), :]
        if fox:
            ck_blk = ck_ref[0, pl.ds(off, tq), :]
        logits = []
        for h in range(2):
            kh = kc[:, h * MLA_SLAB:(h + 1) * MLA_SLAB] if slab else kc
            logits.append(jnp.dot(kh, qth[h], preferred_element_type=F32))
        for h in range(2):
            s = logits[h]
            if fox:
                ck = jnp.sum(jnp.where(lane == 2 * pair + h, ck_blk, 0.0), axis=1, keepdims=True)
                s = s + cq[h] - ck
            if causal_mask is not None:
                s = jnp.where(causal_mask, s, NEG_INF)
            m_prev = m_ref[h]
            m_new = jnp.maximum(m_prev, jnp.max(s, axis=0, keepdims=True))
            alpha = jnp.exp2(m_prev - m_new)
            p = jnp.exp2(s - m_new)
            l_ref[h] = alpha * l_ref[h] + jnp.sum(p, axis=0, keepdims=True)
            vth = vt_ref[0, h * HEAD_DIM:(h + 1) * HEAD_DIM, pl.ds(off, tq)]
            acc_ref[h] = acc_ref[h] * alpha + jnp.dot(vth, p.astype(BF16),
                                                      preferred_element_type=F32)
            m_ref[h] = m_new

    key = lax.broadcasted_iota(jnp.int32, (tq, tq), 0)
    qry = lax.broadcasted_iota(jnp.int32, (tq, tq), 1)
    chunk(i, key <= qry)

    if fox:
        base = b * n_tiles
        bound0 = []
        for h in range(2):
            qf = qth[h].astype(F32)
            qnorm = jnp.sqrt(jnp.max(jnp.sum(qf * qf, axis=0, keepdims=True)))
            bound0.append((qnorm * NORM_SLACK, cf_ref[base + i, 2 * pair + h] - jnp.min(m_ref[h])))

        def body(jj, carry):
            j = i - 1 - jj
            ub = [bound0[h][0] * kn_ref[base + j, 2 * pair + h] + bound0[h][1]
                  - cl_ref[base + j, 2 * pair + h] for h in range(2)]

            @pl.when(jnp.maximum(ub[0], ub[1]) > SKIP_LOG2)
            def _():
                chunk(j, None)

            return carry
    else:
        def body(jj, carry):
            chunk(i - 1 - jj, None)
            return carry

    lax.fori_loop(0, i, body, 0)
    out_t = jnp.concatenate([acc_ref[h] * (1.0 / l_ref[h]) for h in range(2)], axis=0)
    o_ref[0] = out_t.T.astype(o_ref.dtype)


def _causal_attention(qt, k, vt, fox_args, *, slab):
    nb, seq, _ = k.shape
    tq = ATT_TQ
    n_tiles = seq // tq
    n_pairs = FOX_HEADS // 2
    rows = 2 * MLA_SLAB if slab else LANES
    fox = fox_args is not None
    in_specs = [pl.BlockSpec((1, rows, tq), lambda b, p, i: (b, p, i)),
                pl.BlockSpec((1, seq, rows), lambda b, p, i: (b, 0, p)),
                pl.BlockSpec((1, 2 * HEAD_DIM, seq), lambda b, p, i: (b, p, 0))]
    args = [qt, k, vt]
    if fox:
        cum, cumt, cfirst, clast, knorm = fox_args
        smem = pl.BlockSpec(memory_space=pltpu.SMEM)
        in_specs = [smem, smem, smem] + in_specs + [
            pl.BlockSpec((1, 16, tq), lambda b, p, i: (b, 0, i)),
            pl.BlockSpec((1, seq, LANES), lambda b, p, i: (b, 0, 0))]
        args = [cfirst, clast, knorm] + args + [cumt, cum]
    return pl.pallas_call(
        functools.partial(_causal_attn_kernel, tq=tq, n_tiles=n_tiles, fox=fox, slab=slab),
        grid=(nb, n_pairs, n_tiles),
        in_specs=in_specs,
        out_specs=pl.BlockSpec((1, tq, LANES), lambda b, p, i: (b, i, p)),
        out_shape=jax.ShapeDtypeStruct((nb, seq, n_pairs * LANES), BF16),
        scratch_shapes=[pltpu.VMEM((2, 1, tq), F32), pltpu.VMEM((2, 1, tq), F32),
                        pltpu.VMEM((2, HEAD_DIM, tq), F32)],
        compiler_params=_cparams(3),
        name="fox_attention" if fox else "mla_attention",
    )(*args)


def _mem_attn_kernel(q_ref, k_ref, v_ref, o_ref, m_ref, l_ref, acc_ref):
    _init_state(m_ref, l_ref, acc_ref)
    q = q_ref[0]
    kc = k_ref[0]
    vc = v_ref[0]
    for h in range(2):
        qh, kh, vh = _head_operands(h, q, kc, vc, False)
        s = lax.dot_general(qh, kh, (((1,), (1,)), ((), ())), preferred_element_type=F32)
        _softmax_chunk(h, s, vh, m_ref, l_ref, acc_ref)
    _finish(o_ref, l_ref, acc_ref)


def _memory_attention(mq, mk, mv):
    nb, seq, _ = mq.shape
    tq = ATT_TQ
    n_pairs = MEM_HEADS // 2
    return pl.pallas_call(
        _mem_attn_kernel,
        grid=(nb, n_pairs, seq // tq),
        in_specs=[pl.BlockSpec((1, tq, LANES), lambda b, p, i: (b, i, p)),
                  pl.BlockSpec((1, N_MEM, LANES), lambda b, p, i: (b, 0, p)),
                  pl.BlockSpec((1, N_MEM, LANES), lambda b, p, i: (b, 0, p))],
        out_specs=pl.BlockSpec((1, tq, LANES), lambda b, p, i: (b, i, p)),
        out_shape=jax.ShapeDtypeStruct((nb, seq, MEM_WIDTH), BF16),
        scratch_shapes=[pltpu.VMEM((2, tq, 1), F32), pltpu.VMEM((2, tq, 1), F32),
                        pltpu.VMEM((2, tq, LANES), F32)],
        compiler_params=_cparams(3),
        name="memory_attention",
    )(mq, mk, mv)


def _post_attn_kernel(x_ref, att_ref, memo_ref, wa_ref, wm_ref, g_ref, b_ref, wr_ref, br_ref,
                      x1_ref, x1b_ref, idx_ref, rank_ref, gate_ref, cnt_ref, carry_ref):
    t = pl.program_id(0)

    @pl.when(t == 0)
    def _():
        carry_ref[...] = jnp.zeros_like(carry_ref)

    tm = x_ref.shape[0]
    mix = (jnp.dot(att_ref[...], wa_ref[...], preferred_element_type=F32)
           + jnp.dot(memo_ref[...], wm_ref[...], preferred_element_type=F32))
    x1 = _layer_norm(DEEPNORM_ALPHA * x_ref[...] + mix, g_ref[...], b_ref[...])
    x1_ref[...] = x1
    x1b = x1.astype(BF16)
    x1b_ref[...] = x1b
    logits = jnp.dot(x1b, wr_ref[...], preferred_element_type=F32) + br_ref[...]
    lane = lax.broadcasted_iota(jnp.int32, (tm, LANES), 1)
    work = jnp.where(lane < N_EXPERTS, logits, -jnp.inf)
    idxs, vals = [], []
    onehot = jnp.zeros((tm, LANES), F32)
    for _ in range(TOP_K):
        best = jnp.max(work, axis=1, keepdims=True)
        where_best = jnp.min(jnp.where(work == best, lane, LANES), axis=1, keepdims=True)
        hit = lane == where_best
        onehot = jnp.where(hit, 1.0, onehot)
        work = jnp.where(hit, -jnp.inf, work)
        idxs.append(where_best)
        vals.append(best)
    exps = [jnp.exp(v - vals[0]) for v in vals]
    denom = exps[0] + exps[1] + exps[2] + exps[3]
    row = lax.broadcasted_iota(jnp.int32, (tm, tm), 0)
    col = lax.broadcasted_iota(jnp.int32, (tm, tm), 1)
    strict = jnp.where(row > col, 1.0, 0.0).astype(BF16)
    before = jnp.dot(strict, onehot.astype(BF16), preferred_element_type=F32) + carry_ref[...]
    idx_out = jnp.zeros((tm, LANES), jnp.int32)
    rank_out = jnp.zeros((tm, LANES), jnp.int32)
    gate_out = jnp.zeros((tm, LANES), F32)
    for r in range(TOP_K):
        rank_r = jnp.sum(jnp.where(lane == idxs[r], before, 0.0), axis=1, keepdims=True)
        idx_out = jnp.where(lane == r, idxs[r], idx_out)
        rank_out = jnp.where(lane == r, rank_r.astype(jnp.int32), rank_out)
        gate_out = jnp.where(lane == r, exps[r] / denom, gate_out)
    idx_ref[...] = idx_out
    rank_ref[...] = rank_out
    gate_ref[...] = gate_out
    total = carry_ref[...] + jnp.sum(onehot, axis=0, keepdims=True)
    carry_ref[...] = total
    cnt_ref[...] = jnp.broadcast_to(total, cnt_ref.shape)


def _post_attn(x2d, att, memo, wa, wm, g, b, wr, br):
    t_tokens = x2d.shape[0]
    tm = PROJ_TM
    row_spec = lambda width: pl.BlockSpec((tm, width), lambda i: (i, 0))
    full = lambda a: pl.BlockSpec(a.shape, lambda i: (0, 0))
    return pl.pallas_call(
        _post_attn_kernel,
        grid=(t_tokens // tm,),
        in_specs=[row_spec(D_MODEL), row_spec(att.shape[1]), row_spec(MEM_WIDTH),
                  full(wa), full(wm), full(g), full(b), full(wr), full(br)],
        out_specs=[row_spec(D_MODEL), row_spec(D_MODEL), row_spec(LANES), row_spec(LANES),
                   row_spec(LANES), pl.BlockSpec((8, LANES), lambda i: (0, 0))],
        out_shape=[jax.ShapeDtypeStruct((t_tokens, D_MODEL), F32),
                   jax.ShapeDtypeStruct((t_tokens, D_MODEL), BF16),
                   jax.ShapeDtypeStruct((t_tokens, LANES), jnp.int32),
                   jax.ShapeDtypeStruct((t_tokens, LANES), jnp.int32),
                   jax.ShapeDtypeStruct((t_tokens, LANES), F32),
                   jax.ShapeDtypeStruct((8, LANES), F32)],
        scratch_shapes=[pltpu.VMEM((1, LANES), F32)],
        compiler_params=_cparams(1),
        name="outproj_ln_router",
    )(x2d, att, memo, wa, wm, g, b, wr, br)


def _expert_kernel(be_ref, nu_ref, x_ref, wgu_ref, bgu_ref, wdn_ref, bdn_ref, o_ref, *, n_chunks):
    blk = pl.program_id(0)

    @pl.when(blk < nu_ref[0])
    def _():
        x = x_ref[...]
        cw = D_EXPERT // n_chunks
        acc = jnp.zeros(o_ref.shape, F32)
        for c in range(n_chunks):
            gs = slice(c * cw, (c + 1) * cw)
            us = slice(D_EXPERT + c * cw, D_EXPERT + (c + 1) * cw)
            g = jnp.dot(x, wgu_ref[0, :, gs], preferred_element_type=F32) + bgu_ref[0, :, gs]
            u = jnp.dot(x, wgu_ref[0, :, us], preferred_element_type=F32) + bgu_ref[0, :, us]
            g = jnp.minimum(g, SWIGLU_LIMIT)
            u = jnp.clip(u, -SWIGLU_LIMIT, SWIGLU_LIMIT)
            hmid = (u + 1.0) * (g * jax.nn.sigmoid(SWIGLU_ALPHA * g))
            acc = acc + jnp.dot(hmid.astype(BF16), wdn_ref[0, gs, :], preferred_element_type=F32)
        o_ref[...] = acc + bdn_ref[0]

    @pl.when(blk >= nu_ref[0])
    def _():
        o_ref[...] = jnp.zeros_like(o_ref)


def _experts(block_expert, n_used, xb, wgu, bgu, wdn, bdn):
    p_rows = xb.shape[0]
    tm = MOE_TM
    grid_spec = pltpu.PrefetchScalarGridSpec(
        num_scalar_prefetch=2,
        grid=(p_rows // tm,),
        in_specs=[pl.BlockSpec((tm, D_MODEL), lambda i, be, nu: (i, 0)),
                  pl.BlockSpec((1, D_MODEL, 2 * D_EXPERT), lambda i, be, nu: (be[i], 0, 0)),
                  pl.BlockSpec((1, 1, 2 * D_EXPERT), lambda i, be, nu: (be[i], 0, 0)),
                  pl.BlockSpec((1, D_EXPERT, D_MODEL), lambda i, be, nu: (be[i], 0, 0)),
                  pl.BlockSpec((1, 1, D_MODEL), lambda i, be, nu: (be[i], 0, 0))],
        out_specs=pl.BlockSpec((tm, D_MODEL), lambda i, be, nu: (i, 0)),
    )
    return pl.pallas_call(
        functools.partial(_expert_kernel, n_chunks=4),
        grid_spec=grid_spec,
        out_shape=jax.ShapeDtypeStruct((p_rows, D_MODEL), F32),
        compiler_params=_cparams(1),
        name="experts",
    )(block_expert, n_used, xb, wgu, bgu, wdn, bdn)


def _combine_kernel(x1_ref, yg_ref, gate_ref, g_ref, b_ref, o_ref):
    gates = gate_ref[...]
    ffn = yg_ref[:, 0, :] * gates[:, 0:1]
    for r in range(1, TOP_K):
        ffn = ffn + yg_ref[:, r, :] * gates[:, r:r + 1]
    o_ref[...] = _layer_norm(DEEPNORM_ALPHA * x1_ref[...] + ffn, g_ref[...], b_ref[...])


def _combine(x1, yg, gates, g, b):
    t_tokens = x1.shape[0]
    tm = 256
    return pl.pallas_call(
        _combine_kernel,
        grid=(t_tokens // tm,),
        in_specs=[pl.BlockSpec((tm, D_MODEL), lambda i: (i, 0)),
                  pl.BlockSpec((tm, TOP_K, D_MODEL), lambda i: (i, 0, 0)),
                  pl.BlockSpec((tm, LANES), lambda i: (i, 0)),
                  pl.BlockSpec((1, D_MODEL), lambda i: (0, 0)),
                  pl.BlockSpec((1, D_MODEL), lambda i: (0, 0))],
        out_specs=pl.BlockSpec((tm, D_MODEL), lambda i: (i, 0)),
        out_shape=jax.ShapeDtypeStruct((t_tokens, D_MODEL), F32),
        compiler_params=_cparams(1),
        name="combine_ln",
    )(x1, yg, gates, g, b)


def _moe(x1, x1b, idx, rank, gates, counts, wgu, bgu, wdn, bdn, g, b):
    t_tokens = x1.shape[0]
    tm = MOE_TM
    counts = counts[0, :N_EXPERTS].astype(jnp.int32)
    padded = ((counts + tm - 1) // tm) * tm
    pend = jnp.cumsum(padded)
    pstart = pend - padded
    idx4 = idx[:, :TOP_K]
    dest = (pstart[idx4] + rank[:, :TOP_K]).reshape(-1)
    n_blocks = (t_tokens * TOP_K) // tm + N_EXPERTS
    p_rows = n_blocks * tm
    block_start = jnp.arange(n_blocks, dtype=jnp.int32) * tm
    block_expert = jnp.minimum(
        jnp.sum((pend[None, :] <= block_start[:, None]).astype(jnp.int32), axis=1), N_EXPERTS - 1)
    n_used = (pend[-1:] // tm).astype(jnp.int32)
    tok = jnp.arange(t_tokens * TOP_K, dtype=jnp.int32) // TOP_K
    slot_tok = jnp.full((p_rows,), t_tokens, jnp.int32).at[dest].set(tok)
    x_pad = jnp.concatenate([x1b, jnp.zeros((1, D_MODEL), x1b.dtype)], axis=0)
    xb = x_pad[slot_tok]
    y = _experts(block_expert, n_used, xb, wgu, bgu, wdn, bdn)
    yg = y[dest].reshape(t_tokens, TOP_K, D_MODEL)
    return _combine(x1, yg, gates, g, b)


def _row(v, width=None):
    v = v.astype(F32).reshape(1, -1)
    if width is not None and v.shape[1] < width:
        v = jnp.pad(v, ((0, 0), (0, width - v.shape[1])))
    return v


def _pad_cols(w, width):
    return jnp.pad(w, ((0, 0), (0, width - w.shape[1])))


def kernel(x, mem, positions, a_w_in, a_b_f, a_w_out, b_w_in, b_g_q, b_w_uq, b_w_out,
           kv_w_dkv, kv_g, kv_w_ukv, mem_w_kv, ln_g, ln_b,
           moe_w_r, moe_b_r, moe_w_gu, moe_b_gu, moe_w_dn, moe_b_dn):
    nb, seq, d = x.shape
    t_tokens = nb * seq
    n_a = a_w_in.shape[0]
    x2d = x.reshape(t_tokens, d)
    mem2d = mem.reshape(nb * N_MEM, d)
    pos2d = positions.reshape(t_tokens, 1)
    half = QK_ROPE // 2
    inv_freq = ROPE_THETA ** (-jnp.arange(half, dtype=F32) * 2.0 / QK_ROPE)
    invf = jnp.zeros((1, LANES), F32)
    invf = invf.at[0, QK_NOPE:QK_NOPE + half].set(inv_freq)
    invf = invf.at[0, QK_NOPE + half:QK_NOPE + QK_ROPE].set(inv_freq)

    shared_kv = None
    for l in range(DEPTH):
        mkv = _matmul(mem2d, mem_w_kv[l].astype(BF16), BF16, tm=nb * N_MEM)
        mk = mkv[:, :MEM_WIDTH].reshape(nb, N_MEM, MEM_WIDTH)
        mv = mkv[:, MEM_WIDTH:].reshape(nb, N_MEM, MEM_WIDTH)
        if l < n_a:
            w_in = a_w_in[l]
            w = jnp.concatenate([w_in[:, :3 * FOX_WIDTH],
                                 _pad_cols(w_in[:, 3 * FOX_WIDTH:3 * FOX_WIDTH + FOX_HEADS], LANES),
                                 w_in[:, 3 * FOX_WIDTH + FOX_HEADS:]], axis=1).astype(BF16)
            qt, k, vt, mq, cum, cumt, cfirst, clast, knorm = _fox_proj(
                x2d, w, _row(a_b_f[l], LANES), seq)
            stats = [s.reshape(-1, LANES) for s in (cfirst, clast, knorm)]
            att = _causal_attention(qt, k.reshape(nb, seq, -1), vt,
                                    [cum.reshape(nb, seq, LANES), cumt] + stats, slab=False)
            w_out = a_w_out[l]
        else:
            bl = l - n_a
            w = jnp.concatenate([b_w_in[bl], kv_w_dkv[:, :KV_LORA],
                                 jnp.zeros((d, QK_NOPE), F32), kv_w_dkv[:, KV_LORA:],
                                 jnp.zeros((d, LANES - QK_NOPE - QK_ROPE), F32)], axis=1).astype(BF16)
            wuq = b_w_uq[bl].reshape(Q_LORA, MLA_HEADS, QK_NOPE + QK_ROPE)
            wuq = jnp.pad(wuq, ((0, 0), (0, 0), (0, MLA_SLAB - QK_NOPE - QK_ROPE)))
            wuq = wuq.reshape(Q_LORA, MLA_HEADS * MLA_SLAB).astype(BF16)
            wukv = kv_w_ukv.reshape(KV_LORA, MLA_HEADS, QK_NOPE + V_DIM)
            wk = jnp.pad(wukv[:, :, :QK_NOPE], ((0, 0), (0, 0), (0, MLA_SLAB - QK_NOPE)))
            wk = wk.reshape(KV_LORA, MLA_HEADS * MLA_SLAB).astype(BF16)
            wv = wukv[:, :, QK_NOPE:].reshape(KV_LORA, MLA_V_WIDTH).astype(BF16)
            qt, mq, k_new, vt_new = _mla_proj(x2d, pos2d, invf, w, _row(b_g_q[bl]), wuq,
                                              _row(kv_g), wk, wv, seq)
            if shared_kv is None:
                shared_kv = (k_new.reshape(nb, seq, -1), vt_new)
            att = _causal_attention(qt, shared_kv[0], shared_kv[1], None, slab=True)
            w_out = b_w_out[bl]
        memo = _memory_attention(mq.reshape(nb, seq, MEM_WIDTH), mk, mv)
        n_att = w_out.shape[0] - MEM_WIDTH
        x1, x1b, idx, rank, gates, counts = _post_attn(
            x2d, att.reshape(t_tokens, -1), memo.reshape(t_tokens, MEM_WIDTH),
            w_out[:n_att].astype(BF16), w_out[n_att:].astype(BF16),
            _row(ln_g[l, 0]), _row(ln_b[l, 0]),
            _pad_cols(moe_w_r[l], LANES).astype(BF16), _row(moe_b_r[l], LANES))
        x2d = _moe(x1, x1b, idx, rank, gates, counts,
                   moe_w_gu[l].astype(BF16), moe_b_gu[l].reshape(N_EXPERTS, 1, -1),
                   moe_w_dn[l].astype(BF16), moe_b_dn[l].reshape(N_EXPERTS, 1, -1),
                   _row(ln_g[l, 1]), _row(ln_b[l, 1]))
    return x2d.reshape(nb, seq, d)
```

```python
import functools
import math

import jax
import jax.numpy as jnp
from jax import lax
from jax.experimental import pallas as pl
from jax.experimental.pallas import tpu as pltpu

F32 = jnp.float32
BF16 = jnp.bfloat16

D_MODEL = 1024
DEPTH = 2
N_MEM = 256
HEAD_DIM = 64
FOX_HEADS = 12
MEM_HEADS = 4
MLA_HEADS = 12
Q_LORA = 384
KV_LORA = 256
QK_NOPE = 64
QK_ROPE = 32
V_DIM = 64
ROPE_THETA = 10000.0
N_EXPERTS = 32
TOP_K = 4
D_EXPERT = D_MODEL
SWIGLU_LIMIT = 7.0
SWIGLU_ALPHA = 1.702
LN_EPS = 1e-5
RMS_EPS = 1e-6
NEG_INF = -1e30
DEEPNORM_ALPHA = (2 * DEPTH) ** 0.25
FOX_WIDTH = FOX_HEADS * HEAD_DIM
MEM_WIDTH = MEM_HEADS * HEAD_DIM
MLA_V_WIDTH = MLA_HEADS * V_DIM

LANES = 128
LOG2E = math.log2(math.e)
VMEM_LIMIT = 48 * 1024 * 1024
EXPERT_VMEM_LIMIT = 56 * 1024 * 1024

PROJ_TM = 512
ATT_TQ = 512
MOE_TM = 256
MLA_SLAB = LANES
VT_ROWS = HEAD_DIM + 16


def _cparams(n_axes):
    return pltpu.CompilerParams(dimension_semantics=("arbitrary",) * n_axes,
                                vmem_limit_bytes=VMEM_LIMIT)


def _split3(x):
    hi = x.astype(BF16)
    r1 = x - hi.astype(F32)
    mid = r1.astype(BF16)
    lo = (r1 - mid.astype(F32)).astype(BF16)
    return hi, mid, lo


def _layer_norm(y, g, b):
    mu = jnp.mean(y, axis=-1, keepdims=True)
    yc = y - mu
    var = jnp.mean(yc * yc, axis=-1, keepdims=True)
    return yc * lax.rsqrt(var + LN_EPS) * g + b


def _rms_norm(y, g):
    return y * lax.rsqrt(jnp.mean(y * y, axis=-1, keepdims=True) + RMS_EPS) * g


def _mm_kernel(a_ref, b_ref, o_ref):
    o_ref[...] = jnp.dot(a_ref[...].astype(BF16), b_ref[...],
                         preferred_element_type=F32).astype(o_ref.dtype)


def _matmul(a, b, out_dtype, tm):
    m, k = a.shape
    n = b.shape[1]
    return pl.pallas_call(
        _mm_kernel,
        grid=(m // tm,),
        in_specs=[pl.BlockSpec((tm, k), lambda i: (i, 0)),
                  pl.BlockSpec((k, n), lambda i: (0, 0))],
        out_specs=pl.BlockSpec((tm, n), lambda i: (i, 0)),
        out_shape=jax.ShapeDtypeStruct((m, n), out_dtype),
        compiler_params=_cparams(1),
        name="matmul",
    )(a, b)


def _store_transposed(dst_ref, val):
    for s in range(val.shape[1] // LANES):
        sl = slice(s * LANES, (s + 1) * LANES)
        dst_ref[0, sl, :] = val[:, sl].T.astype(dst_ref.dtype)


def _store_values_transposed(vt_ref, val):
    tm = val.shape[0]
    ones = jnp.ones((VT_ROWS - HEAD_DIM, tm), vt_ref.dtype)
    for s in range(val.shape[1] // LANES):
        pair_t = val[:, s * LANES:(s + 1) * LANES].T.astype(vt_ref.dtype)
        for h in range(2):
            r0 = (2 * s + h) * VT_ROWS
            vt_ref[0, r0:r0 + HEAD_DIM, :] = pair_t[h * HEAD_DIM:(h + 1) * HEAD_DIM]
            vt_ref[0, r0 + HEAD_DIM:r0 + VT_ROWS, :] = ones


def _fox_proj_kernel(x_ref, w_ref, bf_ref, seg_ref, qt_ref, k_ref, vt_ref, mq_ref, cum_ref,
                     cumt_ref, cfirst_ref, clast_ref, knorm_ref, carry_ref, *, tiles_per_batch):
    t = pl.program_id(0)

    @pl.when(t % tiles_per_batch == 0)
    def _():
        carry_ref[...] = jnp.zeros_like(carry_ref)

    tm = x_ref.shape[0]
    proj = jnp.dot(x_ref[...].astype(BF16), w_ref[...], preferred_element_type=F32)
    qscale = HEAD_DIM ** -0.5 * LOG2E
    _store_transposed(qt_ref, proj[:, :FOX_WIDTH] * qscale)
    kb = proj[:, FOX_WIDTH:2 * FOX_WIDTH].astype(BF16)
    k_ref[...] = kb
    _store_values_transposed(vt_ref, proj[:, 2 * FOX_WIDTH:3 * FOX_WIDTH])
    kf = kb.astype(F32)
    ksq = jnp.dot((kf * kf).astype(BF16), seg_ref[...], preferred_element_type=F32)
    knorm_ref[0] = jnp.sqrt(jnp.max(ksq, axis=0, keepdims=True))
    f = proj[:, 3 * FOX_WIDTH:3 * FOX_WIDTH + LANES] + bf_ref[...]
    mq_ref[...] = (proj[:, 3 * FOX_WIDTH + LANES:] * qscale).astype(BF16)
    log_f = jnp.minimum(f, 0.0) - jnp.log1p(jnp.exp(-jnp.abs(f)))
    row = lax.broadcasted_iota(jnp.int32, (tm, tm), 0)
    col = lax.broadcasted_iota(jnp.int32, (tm, tm), 1)
    tri = jnp.where(row >= col, 1.0, 0.0).astype(BF16)
    hi, mid, lo = _split3(log_f)
    cum = (jnp.dot(tri, hi, preferred_element_type=F32)
           + jnp.dot(tri, mid, preferred_element_type=F32)
           + jnp.dot(tri, lo, preferred_element_type=F32)) + carry_ref[...]
    carry_ref[...] = cum[tm - 1:tm, :]
    cum2 = cum * LOG2E
    cum_ref[...] = cum2
    cumt_ref[0] = cum2.T[:16, :]
    cfirst_ref[0] = cum2[0:1, :]
    clast_ref[0] = cum2[tm - 1:tm, :]


def _fox_proj(x2d, w, bf, seq):
    t_tokens = x2d.shape[0]
    tm = PROJ_TM
    nb = t_tokens // seq
    n = w.shape[1]
    tiles_per_batch = seq // tm
    n_tiles = t_tokens // tm
    seg = (jnp.arange(FOX_WIDTH)[:, None] // HEAD_DIM == jnp.arange(LANES)[None, :]).astype(BF16)
    row_spec = lambda width: pl.BlockSpec((tm, width), lambda i: (i, 0))
    t_spec = lambda rows: pl.BlockSpec(
        (1, rows, tm), lambda i: (i // tiles_per_batch, 0, i % tiles_per_batch))
    stat_spec = pl.BlockSpec((1, 1, LANES), lambda i: (i, 0, 0))
    stat_shape = jax.ShapeDtypeStruct((n_tiles, 1, LANES), F32)
    return pl.pallas_call(
        functools.partial(_fox_proj_kernel, tiles_per_batch=tiles_per_batch),
        grid=(n_tiles,),
        in_specs=[row_spec(D_MODEL),
                  pl.BlockSpec((D_MODEL, n), lambda i: (0, 0)),
                  pl.BlockSpec((1, LANES), lambda i: (0, 0)),
                  pl.BlockSpec((FOX_WIDTH, LANES), lambda i: (0, 0))],
        out_specs=[t_spec(FOX_WIDTH), row_spec(FOX_WIDTH), t_spec(FOX_HEADS * VT_ROWS),
                   row_spec(MEM_WIDTH), row_spec(LANES), t_spec(16),
                   stat_spec, stat_spec, stat_spec],
        out_shape=[jax.ShapeDtypeStruct((nb, FOX_WIDTH, seq), BF16),
                   jax.ShapeDtypeStruct((t_tokens, FOX_WIDTH), BF16),
                   jax.ShapeDtypeStruct((nb, FOX_HEADS * VT_ROWS, seq), BF16),
                   jax.ShapeDtypeStruct((t_tokens, MEM_WIDTH), BF16),
                   jax.ShapeDtypeStruct((t_tokens, LANES), F32),
                   jax.ShapeDtypeStruct((nb, 16, seq), F32),
                   stat_shape, stat_shape, stat_shape],
        scratch_shapes=[pltpu.VMEM((1, LANES), F32)],
        compiler_params=_cparams(1),
        name="fox_proj",
    )(x2d, w, bf, seg)


def _rope_tables(pos_ref, invf_ref):
    ang = pos_ref[...].astype(F32) * invf_ref[...]
    cos = jnp.cos(ang)
    sin = jnp.sin(ang)
    lane = lax.broadcasted_iota(jnp.int32, ang.shape, 1)
    half = QK_ROPE // 2
    in_x1 = (lane >= QK_NOPE) & (lane < QK_NOPE + half)
    in_x2 = (lane >= QK_NOPE + half) & (lane < QK_NOPE + QK_ROPE)
    c_tab = jnp.where(lane < QK_NOPE, 1.0, jnp.where(in_x1 | in_x2, cos, 0.0))
    s_up = jnp.where(in_x2, sin, 0.0)
    s_dn = jnp.where(in_x1, -sin, 0.0)
    return c_tab, s_up, s_dn


def _rope_slab(slab, tables):
    c_tab, s_up, s_dn = tables
    half = QK_ROPE // 2
    up = pltpu.roll(slab, half, 1)
    dn = pltpu.roll(slab, LANES - half, 1)
    return slab * c_tab + up * s_up + dn * s_dn


def _mla_proj_kernel(x_ref, pos_ref, invf_ref, w_ref, gq_ref, wuq_ref, gkv_ref, wk_ref, wv_ref,
                     qt_ref, mq_ref, k_ref, vt_ref):
    proj = jnp.dot(x_ref[...].astype(BF16), w_ref[...], preferred_element_type=F32)
    tables = _rope_tables(pos_ref, invf_ref)
    c_q = _rms_norm(proj[:, :Q_LORA], gq_ref[...])
    mq_ref[...] = (proj[:, Q_LORA:Q_LORA + MEM_WIDTH] * (HEAD_DIM ** -0.5 * LOG2E)).astype(BF16)
    kv_off = Q_LORA + MEM_WIDTH
    c_kv = _rms_norm(proj[:, kv_off:kv_off + KV_LORA], gkv_ref[...])
    kr = _rope_slab(proj[:, kv_off + KV_LORA:], tables)
    q = jnp.dot(c_q.astype(BF16), wuq_ref[...], preferred_element_type=F32)
    kn = jnp.dot(c_kv.astype(BF16), wk_ref[...], preferred_element_type=F32)
    qscale = (QK_NOPE + QK_ROPE) ** -0.5 * LOG2E
    for h in range(MLA_HEADS):
        sl = slice(h * MLA_SLAB, (h + 1) * MLA_SLAB)
        qt_ref[0, sl, :] = (_rope_slab(q[:, sl], tables) * qscale).T.astype(BF16)
        k_ref[:, sl] = (kn[:, sl] + kr).astype(BF16)
    _store_values_transposed(vt_ref, jnp.dot(c_kv.astype(BF16), wv_ref[...],
                                             preferred_element_type=F32))


def _mla_proj(x2d, pos2d, invf, w, gq, wuq, gkv, wk, wv, seq):
    t_tokens = x2d.shape[0]
    tm = PROJ_TM
    nb = t_tokens // seq
    tiles_per_batch = seq // tm
    row_spec = lambda width: pl.BlockSpec((tm, width), lambda i: (i, 0))
    t_spec = lambda rows: pl.BlockSpec(
        (1, rows, tm), lambda i: (i // tiles_per_batch, 0, i % tiles_per_batch))
    full = lambda a: pl.BlockSpec(a.shape, lambda i: (0, 0))
    slabs = MLA_HEADS * MLA_SLAB
    return pl.pallas_call(
        _mla_proj_kernel,
        grid=(t_tokens // tm,),
        in_specs=[row_spec(D_MODEL), row_spec(1), full(invf), full(w), full(gq), full(wuq),
                  full(gkv), full(wk), full(wv)],
        out_specs=[t_spec(slabs), row_spec(MEM_WIDTH), row_spec(slabs),
                   t_spec(MLA_HEADS * VT_ROWS)],
        out_shape=[jax.ShapeDtypeStruct((nb, slabs, seq), BF16),
                   jax.ShapeDtypeStruct((t_tokens, MEM_WIDTH), BF16),
                   jax.ShapeDtypeStruct((t_tokens, slabs), BF16),
                   jax.ShapeDtypeStruct((nb, MLA_HEADS * VT_ROWS, seq), BF16)],
        compiler_params=_cparams(1),
        name="mla_proj",
    )(x2d, pos2d, invf, w, gq, wuq, gkv, wk, wv)


def _softmax_chunk(h, s, vh, m_ref, l_ref, acc_ref):
    m_prev = m_ref[h]
    m_new = jnp.maximum(m_prev, jnp.max(s, axis=1, keepdims=True))
    alpha = jnp.exp2(m_prev - m_new)
    p = jnp.exp2(s - m_new)
    l_ref[h] = alpha * l_ref[h] + jnp.sum(p, axis=1, keepdims=True)
    acc_ref[h] = acc_ref[h] * alpha + jnp.dot(p.astype(BF16), vh, preferred_element_type=F32)
    m_ref[h] = m_new


def _init_state(m_ref, l_ref, acc_ref):
    m_ref[...] = jnp.full(m_ref.shape, NEG_INF, F32)
    l_ref[...] = jnp.zeros(l_ref.shape, F32)
    acc_ref[...] = jnp.zeros(acc_ref.shape, F32)


def _head_operands(h, q, kc, vc, q_slab):
    lane = lax.broadcasted_iota(jnp.int32, (1, LANES), 1)
    mine = (lane < HEAD_DIM) if h == 0 else (lane >= HEAD_DIM)
    vh = jnp.where(mine, vc, jnp.zeros_like(vc))
    if q_slab:
        sl = slice(h * MLA_SLAB, (h + 1) * MLA_SLAB)
        return q[:, sl], kc[:, sl], vh
    return q, jnp.where(mine, kc, jnp.zeros_like(kc)), vh


def _finish(o_ref, l_ref, acc_ref):
    lane = lax.broadcasted_iota(jnp.int32, (1, LANES), 1)
    o0 = acc_ref[0] / l_ref[0]
    o1 = acc_ref[1] / l_ref[1]
    o_ref[0] = jnp.where(lane < HEAD_DIM, o0, o1).astype(o_ref.dtype)


SKIP_LOG2 = -160.0
NORM_SLACK = 1.02


def _causal_attn_kernel(*refs, tq, n_tiles, fox, slab):
    if fox:
        (cf_ref, cl_ref, kn_ref, qt_ref, k_ref, vt_ref, cq_ref, ck_ref,
         o_ref, m_ref, acc_ref, s_ref) = refs
    else:
        qt_ref, k_ref, vt_ref, o_ref, m_ref, acc_ref, s_ref = refs
    b = pl.program_id(0)
    pair = pl.program_id(1)
    i = pl.program_id(2)
    m_ref[...] = jnp.full(m_ref.shape, NEG_INF, F32)
    acc_ref[...] = jnp.zeros(acc_ref.shape, F32)
    qt = qt_ref[0]
    if slab:
        qth = [qt[:MLA_SLAB], qt[MLA_SLAB:]]
    else:
        rowi = lax.broadcasted_iota(jnp.int32, qt.shape, 0)
        zero = jnp.zeros_like(qt)
        qth = [jnp.where(rowi < HEAD_DIM, qt, zero), jnp.where(rowi >= HEAD_DIM, qt, zero)]
    if fox:
        lane = lax.broadcasted_iota(jnp.int32, (1, LANES), 1)
        cq = [cq_ref[0, pl.ds(2 * pair + h, 1), :] for h in range(2)]

    def logits_to(slot, j):
        off = pl.multiple_of(j * tq, tq)
        kc = k_ref[0, pl.ds(off, tq), :]
        for h in range(2):
            kh = kc[:, h * MLA_SLAB:(h + 1) * MLA_SLAB] if slab else kc
            s = jnp.dot(kh, qth[h], preferred_element_type=F32)
            if fox:
                ck_blk = ck_ref[0, pl.ds(off, tq), :]
                ck = jnp.sum(jnp.where(lane == 2 * pair + h, ck_blk, 0.0), axis=1, keepdims=True)
                s = s + cq[h] - ck
            s_ref[slot, h] = s

    def softmax_pv(slot, j, causal_mask):
        off = pl.multiple_of(j * tq, tq)
        for h in range(2):
            s = s_ref[slot, h]
            if causal_mask is not None:
                s = jnp.where(causal_mask, s, NEG_INF)
            m_prev = m_ref[h]
            m_new = jnp.maximum(m_prev, jnp.max(s, axis=0, keepdims=True))
            alpha = jnp.exp2(m_prev - m_new)
            p = jnp.exp2(s - m_new).astype(BF16)
            vth = vt_ref[0, h * VT_ROWS:(h + 1) * VT_ROWS, pl.ds(off, tq)]
            acc_ref[h] = acc_ref[h] * alpha + jnp.dot(vth, p, preferred_element_type=F32)
            m_ref[h] = m_new

    if fox:
        base = b * n_tiles
        qn = []
        for h in range(2):
            qf = qth[h].astype(F32)
            qn.append(jnp.sqrt(jnp.max(jnp.sum(qf * qf, axis=0, keepdims=True))) * NORM_SLACK)

        def first_live(j, j0):
            ub = [qn[h] * (kn_ref[base + j, 2 * pair + h] + kn_ref[base + i, 2 * pair + h])
                  + cf_ref[base + i, 2 * pair + h] - cl_ref[base + j, 2 * pair + h] for h in range(2)]
            return jnp.where(jnp.maximum(ub[0], ub[1]) > SKIP_LOG2, jnp.minimum(j0, j), j0)

        j0 = lax.fori_loop(0, i, first_live, i)
    else:
        j0 = 0
    n_before = i - j0
    odd = (n_before % 2) == 1

    @pl.when(odd)
    def _():
        logits_to(1, j0)

    @pl.when(jnp.logical_not(odd))
    def _():
        logits_to(0, j0)

    @pl.when(odd)
    def _():
        logits_to(0, j0 + 1)
        softmax_pv(1, j0, None)

    def body(t, carry):
        j = j0 + (n_before % 2) + 2 * t
        logits_to(1, j + 1)
        softmax_pv(0, j, None)
        logits_to(0, j + 2)
        softmax_pv(1, j + 1, None)
        return carry

    lax.fori_loop(0, n_before // 2, body, 0)
    key = lax.broadcasted_iota(jnp.int32, (tq, tq), 0)
    qry = lax.broadcasted_iota(jnp.int32, (tq, tq), 1)
    softmax_pv(0, i, key <= qry)
    out_t = jnp.concatenate(
        [acc_ref[h, :HEAD_DIM] * (1.0 / acc_ref[h, HEAD_DIM:HEAD_DIM + 1]) for h in range(2)],
        axis=0)
    o_ref[0] = out_t.T.astype(o_ref.dtype)


def _causal_attention(qt, k, vt, fox_args, *, slab):
    nb, seq, _ = k.shape
    tq = ATT_TQ
    n_tiles = seq // tq
    n_pairs = FOX_HEADS // 2
    rows = 2 * MLA_SLAB if slab else LANES
    fox = fox_args is not None
    in_specs = [pl.BlockSpec((1, rows, tq), lambda b, p, i: (b, p, i)),
                pl.BlockSpec((1, seq, rows), lambda b, p, i: (b, 0, p)),
                pl.BlockSpec((1, 2 * VT_ROWS, seq), lambda b, p, i: (b, p, 0))]
    args = [qt, k, vt]
    if fox:
        cum, cumt, cfirst, clast, knorm = fox_args
        smem = pl.BlockSpec(memory_space=pltpu.SMEM)
        in_specs = [smem, smem, smem] + in_specs + [
            pl.BlockSpec((1, 16, tq), lambda b, p, i: (b, 0, i)),
            pl.BlockSpec((1, seq, LANES), lambda b, p, i: (b, 0, 0))]
        args = [cfirst, clast, knorm] + args + [cumt, cum]
    return pl.pallas_call(
        functools.partial(_causal_attn_kernel, tq=tq, n_tiles=n_tiles, fox=fox, slab=slab),
        grid=(nb, n_pairs, n_tiles),
        in_specs=in_specs,
        out_specs=pl.BlockSpec((1, tq, LANES), lambda b, p, i: (b, i, p)),
        out_shape=jax.ShapeDtypeStruct((nb, seq, n_pairs * LANES), BF16),
        scratch_shapes=[pltpu.VMEM((2, 1, tq), F32), pltpu.VMEM((2, VT_ROWS, tq), F32),
                        pltpu.VMEM((2, 2, tq, tq), F32)],
        compiler_params=_cparams(3),
        name="fox_attention" if fox else "mla_attention",
    )(*args)


def _mem_attn_kernel(q_ref, k_ref, v_ref, o_ref, m_ref, l_ref, acc_ref):
    _init_state(m_ref, l_ref, acc_ref)
    q = q_ref[0]
    kc = k_ref[0]
    vc = v_ref[0]
    for h in range(2):
        qh, kh, vh = _head_operands(h, q, kc, vc, False)
        s = lax.dot_general(qh, kh, (((1,), (1,)), ((), ())), preferred_element_type=F32)
        _softmax_chunk(h, s, vh, m_ref, l_ref, acc_ref)
    _finish(o_ref, l_ref, acc_ref)


def _memory_attention(mq, mk, mv):
    nb, seq, _ = mq.shape
    tq = ATT_TQ
    n_pairs = MEM_HEADS // 2
    return pl.pallas_call(
        _mem_attn_kernel,
        grid=(nb, n_pairs, seq // tq),
        in_specs=[pl.BlockSpec((1, tq, LANES), lambda b, p, i: (b, i, p)),
                  pl.BlockSpec((1, N_MEM, LANES), lambda b, p, i: (b, 0, p)),
                  pl.BlockSpec((1, N_MEM, LANES), lambda b, p, i: (b, 0, p))],
        out_specs=pl.BlockSpec((1, tq, LANES), lambda b, p, i: (b, i, p)),
        out_shape=jax.ShapeDtypeStruct((nb, seq, MEM_WIDTH), BF16),
        scratch_shapes=[pltpu.VMEM((2, tq, 1), F32), pltpu.VMEM((2, tq, 1), F32),
                        pltpu.VMEM((2, tq, LANES), F32)],
        compiler_params=_cparams(3),
        name="memory_attention",
    )(mq, mk, mv)


def _post_attn_kernel(x_ref, att_ref, memo_ref, wa_ref, wm_ref, g_ref, b_ref, wr_ref, br_ref,
                      x1_ref, x1b_ref, idx_ref, rank_ref, gate_ref, cnt_ref, carry_ref):
    t = pl.program_id(0)

    @pl.when(t == 0)
    def _():
        carry_ref[...] = jnp.zeros_like(carry_ref)

    tm = x_ref.shape[0]
    mix = (jnp.dot(att_ref[...], wa_ref[...], preferred_element_type=F32)
           + jnp.dot(memo_ref[...], wm_ref[...], preferred_element_type=F32))
    x1 = _layer_norm(DEEPNORM_ALPHA * x_ref[...] + mix, g_ref[...], b_ref[...])
    x1_ref[...] = x1
    x1b = x1.astype(BF16)
    x1b_ref[...] = x1b
    logits = jnp.dot(x1b, wr_ref[...], preferred_element_type=F32) + br_ref[...]
    lane = lax.broadcasted_iota(jnp.int32, (tm, LANES), 1)
    work = jnp.where(lane < N_EXPERTS, logits, -jnp.inf)
    idxs, vals = [], []
    onehot = jnp.zeros((tm, LANES), F32)
    for _ in range(TOP_K):
        best = jnp.max(work, axis=1, keepdims=True)
        where_best = jnp.min(jnp.where(work == best, lane, LANES), axis=1, keepdims=True)
        hit = lane == where_best
        onehot = jnp.where(hit, 1.0, onehot)
        work = jnp.where(hit, -jnp.inf, work)
        idxs.append(where_best)
        vals.append(best)
    exps = [jnp.exp(v - vals[0]) for v in vals]
    denom = exps[0] + exps[1] + exps[2] + exps[3]
    row = lax.broadcasted_iota(jnp.int32, (tm, tm), 0)
    col = lax.broadcasted_iota(jnp.int32, (tm, tm), 1)
    strict = jnp.where(row > col, 1.0, 0.0).astype(BF16)
    before = jnp.dot(strict, onehot.astype(BF16), preferred_element_type=F32) + carry_ref[...]
    idx_out = jnp.zeros((tm, LANES), jnp.int32)
    rank_out = jnp.zeros((tm, LANES), jnp.int32)
    gate_out = jnp.zeros((tm, LANES), F32)
    for r in range(TOP_K):
        rank_r = jnp.sum(jnp.where(lane == idxs[r], before, 0.0), axis=1, keepdims=True)
        idx_out = jnp.where(lane == r, idxs[r], idx_out)
        rank_out = jnp.where(lane == r, rank_r.astype(jnp.int32), rank_out)
        gate_out = jnp.where(lane == r, exps[r] / denom, gate_out)
    idx_ref[...] = idx_out
    rank_ref[...] = rank_out
    gate_ref[...] = gate_out
    total = carry_ref[...] + jnp.sum(onehot, axis=0, keepdims=True)
    carry_ref[...] = total
    cnt_ref[...] = jnp.broadcast_to(total, cnt_ref.shape)


def _post_attn(x2d, att, memo, wa, wm, g, b, wr, br):
    t_tokens = x2d.shape[0]
    tm = PROJ_TM
    row_spec = lambda width: pl.BlockSpec((tm, width), lambda i: (i, 0))
    full = lambda a: pl.BlockSpec(a.shape, lambda i: (0, 0))
    return pl.pallas_call(
        _post_attn_kernel,
        grid=(t_tokens // tm,),
        in_specs=[row_spec(D_MODEL), row_spec(att.shape[1]), row_spec(MEM_WIDTH),
                  full(wa), full(wm), full(g), full(b), full(wr), full(br)],
        out_specs=[row_spec(D_MODEL), row_spec(D_MODEL), row_spec(LANES), row_spec(LANES),
                   row_spec(LANES), pl.BlockSpec((8, LANES), lambda i: (0, 0))],
        out_shape=[jax.ShapeDtypeStruct((t_tokens, D_MODEL), F32),
                   jax.ShapeDtypeStruct((t_tokens, D_MODEL), BF16),
                   jax.ShapeDtypeStruct((t_tokens, LANES), jnp.int32),
                   jax.ShapeDtypeStruct((t_tokens, LANES), jnp.int32),
                   jax.ShapeDtypeStruct((t_tokens, LANES), F32),
                   jax.ShapeDtypeStruct((8, LANES), F32)],
        scratch_shapes=[pltpu.VMEM((1, LANES), F32)],
        compiler_params=_cparams(1),
        name="outproj_ln_router",
    )(x2d, att, memo, wa, wm, g, b, wr, br)


def _expert_kernel(be_ref, nu_ref, x_ref, wgu_ref, bgu_ref, wdn_ref, bdn_ref, o_ref,
                   wgu_bf, wdn_bf, *, n_chunks):
    blk = pl.program_id(0)

    @pl.when((blk == 0) | (be_ref[blk] != be_ref[jnp.maximum(blk - 1, 0)]))
    def _():
        wgu_bf[...] = wgu_ref[0, 0].astype(BF16)
        wdn_bf[...] = wdn_ref[0, 0].astype(BF16)

    @pl.when(blk < nu_ref[0])
    def _():
        x = x_ref[...]
        cw = D_EXPERT // n_chunks
        acc = jnp.zeros(o_ref.shape, F32)
        for c in range(n_chunks):
            gs = slice(c * cw, (c + 1) * cw)
            us = slice(D_EXPERT + c * cw, D_EXPERT + (c + 1) * cw)
            g = jnp.dot(x, wgu_bf[:, gs], preferred_element_type=F32) + bgu_ref[0, 0, :, gs]
            u = jnp.dot(x, wgu_bf[:, us], preferred_element_type=F32) + bgu_ref[0, 0, :, us]
            g = jnp.minimum(g, SWIGLU_LIMIT)
            u = jnp.clip(u, -SWIGLU_LIMIT, SWIGLU_LIMIT)
            hmid = (u + 1.0) * (g * jax.nn.sigmoid(SWIGLU_ALPHA * g))
            acc = acc + jnp.dot(hmid.astype(BF16), wdn_bf[gs, :], preferred_element_type=F32)
        o_ref[...] = acc + bdn_ref[0, 0]

    @pl.when(blk >= nu_ref[0])
    def _():
        o_ref[...] = jnp.zeros_like(o_ref)


def _experts(block_expert, n_used, xb, layer, wgu, bgu, wdn, bdn):
    p_rows = xb.shape[0]
    tm = MOE_TM
    grid_spec = pltpu.PrefetchScalarGridSpec(
        num_scalar_prefetch=2,
        grid=(p_rows // tm,),
        in_specs=[pl.BlockSpec((tm, D_MODEL), lambda i, be, nu: (i, 0)),
                  pl.BlockSpec((1, 1, D_MODEL, 2 * D_EXPERT), lambda i, be, nu: (layer, be[i], 0, 0)),
                  pl.BlockSpec((1, 1, 1, 2 * D_EXPERT), lambda i, be, nu: (layer, be[i], 0, 0)),
                  pl.BlockSpec((1, 1, D_EXPERT, D_MODEL), lambda i, be, nu: (layer, be[i], 0, 0)),
                  pl.BlockSpec((1, 1, 1, D_MODEL), lambda i, be, nu: (layer, be[i], 0, 0))],
        out_specs=pl.BlockSpec((tm, D_MODEL), lambda i, be, nu: (i, 0)),
        scratch_shapes=[pltpu.VMEM((D_MODEL, 2 * D_EXPERT), BF16),
                        pltpu.VMEM((D_EXPERT, D_MODEL), BF16)],
    )
    return pl.pallas_call(
        functools.partial(_expert_kernel, n_chunks=4),
        grid_spec=grid_spec,
        out_shape=jax.ShapeDtypeStruct((p_rows, D_MODEL), F32),
        compiler_params=pltpu.CompilerParams(dimension_semantics=("arbitrary",),
                                             vmem_limit_bytes=EXPERT_VMEM_LIMIT),
        name="experts",
    )(block_expert, n_used, xb, wgu, bgu, wdn, bdn)


def _combine_kernel(x1_ref, yg_ref, gate_ref, g_ref, b_ref, o_ref):
    gates = gate_ref[...]
    ffn = yg_ref[:, 0, :] * gates[:, 0:1]
    for r in range(1, TOP_K):
        ffn = ffn + yg_ref[:, r, :] * gates[:, r:r + 1]
    o_ref[...] = _layer_norm(DEEPNORM_ALPHA * x1_ref[...] + ffn, g_ref[...], b_ref[...])


def _combine(x1, yg, gates, g, b):
    t_tokens = x1.shape[0]
    tm = 256
    return pl.pallas_call(
        _combine_kernel,
        grid=(t_tokens // tm,),
        in_specs=[pl.BlockSpec((tm, D_MODEL), lambda i: (i, 0)),
                  pl.BlockSpec((tm, TOP_K, D_MODEL), lambda i: (i, 0, 0)),
                  pl.BlockSpec((tm, LANES), lambda i: (i, 0)),
                  pl.BlockSpec((1, D_MODEL), lambda i: (0, 0)),
                  pl.BlockSpec((1, D_MODEL), lambda i: (0, 0))],
        out_specs=pl.BlockSpec((tm, D_MODEL), lambda i: (i, 0)),
        out_shape=jax.ShapeDtypeStruct((t_tokens, D_MODEL), F32),
        compiler_params=_cparams(1),
        name="combine_ln",
    )(x1, yg, gates, g, b)


def _moe(x1, x1b, idx, rank, gates, counts, layer, wgu, bgu, wdn, bdn, g, b):
    t_tokens = x1.shape[0]
    tm = MOE_TM
    counts = counts[0, :N_EXPERTS].astype(jnp.int32)
    padded = ((counts + tm - 1) // tm) * tm
    pend = jnp.cumsum(padded)
    pstart = pend - padded
    idx4 = idx[:, :TOP_K]
    dest = (pstart[idx4] + rank[:, :TOP_K]).reshape(-1)
    n_blocks = (t_tokens * TOP_K) // tm + N_EXPERTS
    p_rows = n_blocks * tm
    block_start = jnp.arange(n_blocks, dtype=jnp.int32) * tm
    block_expert = jnp.minimum(
        jnp.sum((pend[None, :] <= block_start[:, None]).astype(jnp.int32), axis=1), N_EXPERTS - 1)
    n_used = (pend[-1:] // tm).astype(jnp.int32)
    tok = jnp.arange(t_tokens * TOP_K, dtype=jnp.int32) // TOP_K
    slot_tok = jnp.full((p_rows,), t_tokens, jnp.int32).at[dest].set(tok)
    x_pad = jnp.concatenate([x1b, jnp.zeros((1, D_MODEL), x1b.dtype)], axis=0)
    xb = x_pad[slot_tok]
    y = _experts(block_expert, n_used, xb, layer, wgu, bgu, wdn, bdn)
    yg = y[dest].reshape(t_tokens, TOP_K, D_MODEL)
    return _combine(x1, yg, gates, g, b)


def _row(v, width=None):
    v = v.astype(F32).reshape(1, -1)
    if width is not None and v.shape[1] < width:
        v = jnp.pad(v, ((0, 0), (0, width - v.shape[1])))
    return v


def _pad_cols(w, width):
    return jnp.pad(w, ((0, 0), (0, width - w.shape[1])))


def kernel(x, mem, positions, a_w_in, a_b_f, a_w_out, b_w_in, b_g_q, b_w_uq, b_w_out,
           kv_w_dkv, kv_g, kv_w_ukv, mem_w_kv, ln_g, ln_b,
           moe_w_r, moe_b_r, moe_w_gu, moe_b_gu, moe_w_dn, moe_b_dn):
    nb, seq, d = x.shape
    t_tokens = nb * seq
    n_a = a_w_in.shape[0]
    x2d = x.reshape(t_tokens, d)
    mem2d = mem.reshape(nb * N_MEM, d)
    pos2d = positions.reshape(t_tokens, 1)
    half = QK_ROPE // 2
    inv_freq = ROPE_THETA ** (-jnp.arange(half, dtype=F32) * 2.0 / QK_ROPE)
    invf = jnp.zeros((1, LANES), F32)
    invf = invf.at[0, QK_NOPE:QK_NOPE + half].set(inv_freq)
    invf = invf.at[0, QK_NOPE + half:QK_NOPE + QK_ROPE].set(inv_freq)

    shared_kv = None
    for l in range(DEPTH):
        mkv = _matmul(mem2d, mem_w_kv[l].astype(BF16), BF16, tm=nb * N_MEM)
        mk = mkv[:, :MEM_WIDTH].reshape(nb, N_MEM, MEM_WIDTH)
        mv = mkv[:, MEM_WIDTH:].reshape(nb, N_MEM, MEM_WIDTH)
        if l < n_a:
            w_in = a_w_in[l]
            w = jnp.concatenate([w_in[:, :3 * FOX_WIDTH],
                                 _pad_cols(w_in[:, 3 * FOX_WIDTH:3 * FOX_WIDTH + FOX_HEADS], LANES),
                                 w_in[:, 3 * FOX_WIDTH + FOX_HEADS:]], axis=1).astype(BF16)
            qt, k, vt, mq, cum, cumt, cfirst, clast, knorm = _fox_proj(
                x2d, w, _row(a_b_f[l], LANES), seq)
            stats = [s.reshape(-1, LANES) for s in (cfirst, clast, knorm)]
            att = _causal_attention(qt, k.reshape(nb, seq, -1), vt,
                                    [cum.reshape(nb, seq, LANES), cumt] + stats, slab=False)
            w_out = a_w_out[l]
        else:
            bl = l - n_a
            w = jnp.concatenate([b_w_in[bl], kv_w_dkv[:, :KV_LORA],
                                 jnp.zeros((d, QK_NOPE), F32), kv_w_dkv[:, KV_LORA:],
                                 jnp.zeros((d, LANES - QK_NOPE - QK_ROPE), F32)], axis=1).astype(BF16)
            wuq = b_w_uq[bl].reshape(Q_LORA, MLA_HEADS, QK_NOPE + QK_ROPE)
            wuq = jnp.pad(wuq, ((0, 0), (0, 0), (0, MLA_SLAB - QK_NOPE - QK_ROPE)))
            wuq = wuq.reshape(Q_LORA, MLA_HEADS * MLA_SLAB).astype(BF16)
            wukv = kv_w_ukv.reshape(KV_LORA, MLA_HEADS, QK_NOPE + V_DIM)
            wk = jnp.pad(wukv[:, :, :QK_NOPE], ((0, 0), (0, 0), (0, MLA_SLAB - QK_NOPE)))
            wk = wk.reshape(KV_LORA, MLA_HEADS * MLA_SLAB).astype(BF16)
            wv = wukv[:, :, QK_NOPE:].reshape(KV_LORA, MLA_V_WIDTH).astype(BF16)
            qt, mq, k_new, vt_new = _mla_proj(x2d, pos2d, invf, w, _row(b_g_q[bl]), wuq,
                                              _row(kv_g), wk, wv, seq)
            if shared_kv is None:
                shared_kv = (k_new.reshape(nb, seq, -1), vt_new)
            att = _causal_attention(qt, shared_kv[0], shared_kv[1], None, slab=True)
            w_out = b_w_out[bl]
        memo = _memory_attention(mq.reshape(nb, seq, MEM_WIDTH), mk, mv)
        n_att = w_out.shape[0] - MEM_WIDTH
        x1, x1b, idx, rank, gates, counts = _post_attn(
            x2d, att.reshape(t_tokens, -1), memo.reshape(t_tokens, MEM_WIDTH),
            w_out[:n_att].astype(BF16), w_out[n_att:].astype(BF16),
            _row(ln_g[l, 0]), _row(ln_b[l, 0]),
            _pad_cols(moe_w_r[l], LANES).astype(BF16), _row(moe_b_r[l], LANES))
        x2d = _moe(x1, x1b, idx, rank, gates, counts, l,
                   moe_w_gu, moe_b_gu.reshape(DEPTH, N_EXPERTS, 1, -1),
                   moe_w_dn, moe_b_dn.reshape(DEPTH, N_EXPERTS, 1, -1),
                   _row(ln_g[l, 1]), _row(ln_b[l, 1]))
    return x2d.reshape(nb, seq, d)
```

```python
import functools
import math

import jax
import jax.numpy as jnp
from jax import lax
from jax.experimental import pallas as pl
from jax.experimental.pallas import tpu as pltpu
from jax.experimental.pallas import tpu_sc as plsc

F32 = jnp.float32
BF16 = jnp.bfloat16

D_MODEL = 1024
DEPTH = 2
N_MEM = 256
HEAD_DIM = 64
FOX_HEADS = 12
MEM_HEADS = 4
MLA_HEADS = 12
Q_LORA = 384
KV_LORA = 256
QK_NOPE = 64
QK_ROPE = 32
V_DIM = 64
ROPE_THETA = 10000.0
N_EXPERTS = 32
TOP_K = 4
D_EXPERT = D_MODEL
SWIGLU_LIMIT = 7.0
SWIGLU_ALPHA = 1.702
LN_EPS = 1e-5
RMS_EPS = 1e-6
NEG_INF = -1e30
DEEPNORM_ALPHA = (2 * DEPTH) ** 0.25
FOX_WIDTH = FOX_HEADS * HEAD_DIM
MEM_WIDTH = MEM_HEADS * HEAD_DIM
MLA_V_WIDTH = MLA_HEADS * V_DIM

LANES = 128
LOG2E = math.log2(math.e)
VMEM_LIMIT = 48 * 1024 * 1024
EXPERT_VMEM_LIMIT = 56 * 1024 * 1024

PROJ_TM = 512
ATT_TQ = 512
MOE_TM = 256
MLA_SLAB = LANES
VT_ROWS = HEAD_DIM + 16
SC_CORES = 2
SC_SUBCORES = 16
SC_WORKERS = SC_CORES * SC_SUBCORES
SC_CHUNK = 32


def _cparams(n_axes):
    return pltpu.CompilerParams(dimension_semantics=("arbitrary",) * n_axes,
                                vmem_limit_bytes=VMEM_LIMIT)


def _split3(x):
    hi = x.astype(BF16)
    r1 = x - hi.astype(F32)
    mid = r1.astype(BF16)
    lo = (r1 - mid.astype(F32)).astype(BF16)
    return hi, mid, lo


def _layer_norm(y, g, b):
    mu = jnp.mean(y, axis=-1, keepdims=True)
    yc = y - mu
    var = jnp.mean(yc * yc, axis=-1, keepdims=True)
    return yc * lax.rsqrt(var + LN_EPS) * g + b


def _rms_norm(y, g):
    return y * lax.rsqrt(jnp.mean(y * y, axis=-1, keepdims=True) + RMS_EPS) * g


def _mm_kernel(a_ref, b_ref, o_ref):
    o_ref[...] = jnp.dot(a_ref[...].astype(BF16), b_ref[...],
                         preferred_element_type=F32).astype(o_ref.dtype)


def _matmul(a, b, out_dtype, tm):
    m, k = a.shape
    n = b.shape[1]
    return pl.pallas_call(
        _mm_kernel,
        grid=(m // tm,),
        in_specs=[pl.BlockSpec((tm, k), lambda i: (i, 0)),
                  pl.BlockSpec((k, n), lambda i: (0, 0))],
        out_specs=pl.BlockSpec((tm, n), lambda i: (i, 0)),
        out_shape=jax.ShapeDtypeStruct((m, n), out_dtype),
        compiler_params=_cparams(1),
        name="matmul",
    )(a, b)


def _store_transposed(dst_ref, val):
    for s in range(val.shape[1] // LANES):
        sl = slice(s * LANES, (s + 1) * LANES)
        dst_ref[0, sl, :] = val[:, sl].T.astype(dst_ref.dtype)


def _store_values_transposed(vt_ref, val):
    tm = val.shape[0]
    ones = jnp.ones((VT_ROWS - HEAD_DIM, tm), vt_ref.dtype)
    for s in range(val.shape[1] // LANES):
        pair_t = val[:, s * LANES:(s + 1) * LANES].T.astype(vt_ref.dtype)
        for h in range(2):
            r0 = (2 * s + h) * VT_ROWS
            vt_ref[0, r0:r0 + HEAD_DIM, :] = pair_t[h * HEAD_DIM:(h + 1) * HEAD_DIM]
            vt_ref[0, r0 + HEAD_DIM:r0 + VT_ROWS, :] = ones


def _fox_proj_kernel(x_ref, w_ref, bf_ref, seg_ref, qt_ref, k_ref, vt_ref, mq_ref, cum_ref,
                     cumt_ref, cfirst_ref, clast_ref, knorm_ref, carry_ref, *, tiles_per_batch):
    t = pl.program_id(0)

    @pl.when(t % tiles_per_batch == 0)
    def _():
        carry_ref[...] = jnp.zeros_like(carry_ref)

    tm = x_ref.shape[0]
    proj = jnp.dot(x_ref[...].astype(BF16), w_ref[...], preferred_element_type=F32)
    qscale = HEAD_DIM ** -0.5 * LOG2E
    _store_transposed(qt_ref, proj[:, :FOX_WIDTH] * qscale)
    kb = proj[:, FOX_WIDTH:2 * FOX_WIDTH].astype(BF16)
    k_ref[...] = kb
    _store_values_transposed(vt_ref, proj[:, 2 * FOX_WIDTH:3 * FOX_WIDTH])
    kf = kb.astype(F32)
    ksq = jnp.dot((kf * kf).astype(BF16), seg_ref[...], preferred_element_type=F32)
    knorm_ref[0] = jnp.sqrt(jnp.max(ksq, axis=0, keepdims=True))
    f = proj[:, 3 * FOX_WIDTH:3 * FOX_WIDTH + LANES] + bf_ref[...]
    mq_ref[...] = (proj[:, 3 * FOX_WIDTH + LANES:] * qscale).astype(BF16)
    log_f = jnp.minimum(f, 0.0) - jnp.log1p(jnp.exp(-jnp.abs(f)))
    row = lax.broadcasted_iota(jnp.int32, (tm, tm), 0)
    col = lax.broadcasted_iota(jnp.int32, (tm, tm), 1)
    tri = jnp.where(row >= col, 1.0, 0.0).astype(BF16)
    hi, mid, lo = _split3(log_f)
    cum = (jnp.dot(tri, hi, preferred_element_type=F32)
           + jnp.dot(tri, mid, preferred_element_type=F32)
           + jnp.dot(tri, lo, preferred_element_type=F32)) + carry_ref[...]
    carry_ref[...] = cum[tm - 1:tm, :]
    cum2 = cum * LOG2E
    cum_ref[...] = cum2
    cumt_ref[0] = cum2.T[:16, :]
    cfirst_ref[0] = cum2[0:1, :]
    clast_ref[0] = cum2[tm - 1:tm, :]


def _fox_proj(x2d, w, bf, seq):
    t_tokens = x2d.shape[0]
    tm = PROJ_TM
    nb = t_tokens // seq
    n = w.shape[1]
    tiles_per_batch = seq // tm
    n_tiles = t_tokens // tm
    seg = (jnp.arange(FOX_WIDTH)[:, None] // HEAD_DIM == jnp.arange(LANES)[None, :]).astype(BF16)
    row_spec = lambda width: pl.BlockSpec((tm, width), lambda i: (i, 0))
    t_spec = lambda rows: pl.BlockSpec(
        (1, rows, tm), lambda i: (i // tiles_per_batch, 0, i % tiles_per_batch))
    stat_spec = pl.BlockSpec((1, 1, LANES), lambda i: (i, 0, 0))
    stat_shape = jax.ShapeDtypeStruct((n_tiles, 1, LANES), F32)
    return pl.pallas_call(
        functools.partial(_fox_proj_kernel, tiles_per_batch=tiles_per_batch),
        grid=(n_tiles,),
        in_specs=[row_spec(D_MODEL),
                  pl.BlockSpec((D_MODEL, n), lambda i: (0, 0)),
                  pl.BlockSpec((1, LANES), lambda i: (0, 0)),
                  pl.BlockSpec((FOX_WIDTH, LANES), lambda i: (0, 0))],
        out_specs=[t_spec(FOX_WIDTH), row_spec(FOX_WIDTH), t_spec(FOX_HEADS * VT_ROWS),
                   row_spec(MEM_WIDTH), row_spec(LANES), t_spec(16),
                   stat_spec, stat_spec, stat_spec],
        out_shape=[jax.ShapeDtypeStruct((nb, FOX_WIDTH, seq), BF16),
                   jax.ShapeDtypeStruct((t_tokens, FOX_WIDTH), BF16),
                   jax.ShapeDtypeStruct((nb, FOX_HEADS * VT_ROWS, seq), BF16),
                   jax.ShapeDtypeStruct((t_tokens, MEM_WIDTH), BF16),
                   jax.ShapeDtypeStruct((t_tokens, LANES), F32),
                   jax.ShapeDtypeStruct((nb, 16, seq), F32),
                   stat_shape, stat_shape, stat_shape],
        scratch_shapes=[pltpu.VMEM((1, LANES), F32)],
        compiler_params=_cparams(1),
        name="fox_proj",
    )(x2d, w, bf, seg)


def _rope_tables(pos_ref, invf_ref):
    ang = pos_ref[...].astype(F32) * invf_ref[...]
    cos = jnp.cos(ang)
    sin = jnp.sin(ang)
    lane = lax.broadcasted_iota(jnp.int32, ang.shape, 1)
    half = QK_ROPE // 2
    in_x1 = (lane >= QK_NOPE) & (lane < QK_NOPE + half)
    in_x2 = (lane >= QK_NOPE + half) & (lane < QK_NOPE + QK_ROPE)
    c_tab = jnp.where(lane < QK_NOPE, 1.0, jnp.where(in_x1 | in_x2, cos, 0.0))
    s_up = jnp.where(in_x2, sin, 0.0)
    s_dn = jnp.where(in_x1, -sin, 0.0)
    return c_tab, s_up, s_dn


def _rope_slab(slab, tables):
    c_tab, s_up, s_dn = tables
    half = QK_ROPE // 2
    up = pltpu.roll(slab, half, 1)
    dn = pltpu.roll(slab, LANES - half, 1)
    return slab * c_tab + up * s_up + dn * s_dn


def _mla_proj_kernel(x_ref, pos_ref, invf_ref, w_ref, gq_ref, wuq_ref, gkv_ref, wk_ref, wv_ref,
                     qt_ref, mq_ref, k_ref, vt_ref):
    proj = jnp.dot(x_ref[...].astype(BF16), w_ref[...], preferred_element_type=F32)
    tables = _rope_tables(pos_ref, invf_ref)
    c_q = _rms_norm(proj[:, :Q_LORA], gq_ref[...])
    mq_ref[...] = (proj[:, Q_LORA:Q_LORA + MEM_WIDTH] * (HEAD_DIM ** -0.5 * LOG2E)).astype(BF16)
    kv_off = Q_LORA + MEM_WIDTH
    c_kv = _rms_norm(proj[:, kv_off:kv_off + KV_LORA], gkv_ref[...])
    kr = _rope_slab(proj[:, kv_off + KV_LORA:], tables)
    q = jnp.dot(c_q.astype(BF16), wuq_ref[...], preferred_element_type=F32)
    kn = jnp.dot(c_kv.astype(BF16), wk_ref[...], preferred_element_type=F32)
    qscale = (QK_NOPE + QK_ROPE) ** -0.5 * LOG2E
    for h in range(MLA_HEADS):
        sl = slice(h * MLA_SLAB, (h + 1) * MLA_SLAB)
        qt_ref[0, sl, :] = (_rope_slab(q[:, sl], tables) * qscale).T.astype(BF16)
        k_ref[:, sl] = (kn[:, sl] + kr).astype(BF16)
    _store_values_transposed(vt_ref, jnp.dot(c_kv.astype(BF16), wv_ref[...],
                                             preferred_element_type=F32))


def _mla_proj(x2d, pos2d, invf, w, gq, wuq, gkv, wk, wv, seq):
    t_tokens = x2d.shape[0]
    tm = PROJ_TM
    nb = t_tokens // seq
    tiles_per_batch = seq // tm
    row_spec = lambda width: pl.BlockSpec((tm, width), lambda i: (i, 0))
    t_spec = lambda rows: pl.BlockSpec(
        (1, rows, tm), lambda i: (i // tiles_per_batch, 0, i % tiles_per_batch))
    full = lambda a: pl.BlockSpec(a.shape, lambda i: (0, 0))
    slabs = MLA_HEADS * MLA_SLAB
    return pl.pallas_call(
        _mla_proj_kernel,
        grid=(t_tokens // tm,),
        in_specs=[row_spec(D_MODEL), row_spec(1), full(invf), full(w), full(gq), full(wuq),
                  full(gkv), full(wk), full(wv)],
        out_specs=[t_spec(slabs), row_spec(MEM_WIDTH), row_spec(slabs),
                   t_spec(MLA_HEADS * VT_ROWS)],
        out_shape=[jax.ShapeDtypeStruct((nb, slabs, seq), BF16),
                   jax.ShapeDtypeStruct((t_tokens, MEM_WIDTH), BF16),
                   jax.ShapeDtypeStruct((t_tokens, slabs), BF16),
                   jax.ShapeDtypeStruct((nb, MLA_HEADS * VT_ROWS, seq), BF16)],
        compiler_params=_cparams(1),
        name="mla_proj",
    )(x2d, pos2d, invf, w, gq, wuq, gkv, wk, wv)


def _softmax_chunk(h, s, vh, m_ref, l_ref, acc_ref):
    m_prev = m_ref[h]
    m_new = jnp.maximum(m_prev, jnp.max(s, axis=1, keepdims=True))
    alpha = jnp.exp2(m_prev - m_new)
    p = jnp.exp2(s - m_new)
    l_ref[h] = alpha * l_ref[h] + jnp.sum(p, axis=1, keepdims=True)
    acc_ref[h] = acc_ref[h] * alpha + jnp.dot(p.astype(BF16), vh, preferred_element_type=F32)
    m_ref[h] = m_new


def _init_state(m_ref, l_ref, acc_ref):
    m_ref[...] = jnp.full(m_ref.shape, NEG_INF, F32)
    l_ref[...] = jnp.zeros(l_ref.shape, F32)
    acc_ref[...] = jnp.zeros(acc_ref.shape, F32)


def _head_operands(h, q, kc, vc, q_slab):
    lane = lax.broadcasted_iota(jnp.int32, (1, LANES), 1)
    mine = (lane < HEAD_DIM) if h == 0 else (lane >= HEAD_DIM)
    vh = jnp.where(mine, vc, jnp.zeros_like(vc))
    if q_slab:
        sl = slice(h * MLA_SLAB, (h + 1) * MLA_SLAB)
        return q[:, sl], kc[:, sl], vh
    return q, jnp.where(mine, kc, jnp.zeros_like(kc)), vh


def _finish(o_ref, l_ref, acc_ref):
    lane = lax.broadcasted_iota(jnp.int32, (1, LANES), 1)
    o0 = acc_ref[0] / l_ref[0]
    o1 = acc_ref[1] / l_ref[1]
    o_ref[0] = jnp.where(lane < HEAD_DIM, o0, o1).astype(o_ref.dtype)


SKIP_LOG2 = -160.0
NORM_SLACK = 1.02


def _causal_attn_kernel(*refs, tq, n_tiles, fox, slab):
    if fox:
        (cf_ref, cl_ref, kn_ref, qt_ref, k_ref, vt_ref, cq_ref, ck_ref,
         o_ref, m_ref, acc_ref, s_ref) = refs
    else:
        qt_ref, k_ref, vt_ref, o_ref, m_ref, acc_ref, s_ref = refs
    b = pl.program_id(0)
    pair = pl.program_id(1)
    i = pl.program_id(2)
    m_ref[...] = jnp.full(m_ref.shape, NEG_INF, F32)
    acc_ref[...] = jnp.zeros(acc_ref.shape, F32)
    qt = qt_ref[0]
    if slab:
        qth = [qt[:MLA_SLAB], qt[MLA_SLAB:]]
    else:
        rowi = lax.broadcasted_iota(jnp.int32, qt.shape, 0)
        zero = jnp.zeros_like(qt)
        qth = [jnp.where(rowi < HEAD_DIM, qt, zero), jnp.where(rowi >= HEAD_DIM, qt, zero)]
    if fox:
        lane = lax.broadcasted_iota(jnp.int32, (1, LANES), 1)
        cq = [cq_ref[0, pl.ds(2 * pair + h, 1), :] for h in range(2)]

    def logits_to(slot, j):
        off = pl.multiple_of(j * tq, tq)
        kc = k_ref[0, pl.ds(off, tq), :]
        for h in range(2):
            kh = kc[:, h * MLA_SLAB:(h + 1) * MLA_SLAB] if slab else kc
            s = jnp.dot(kh, qth[h], preferred_element_type=F32)
            if fox:
                ck_blk = ck_ref[0, pl.ds(off, tq), :]
                ck = jnp.sum(jnp.where(lane == 2 * pair + h, ck_blk, 0.0), axis=1, keepdims=True)
                s = s + cq[h] - ck
            s_ref[slot, h] = s

    def softmax_pv(slot, j, causal_mask):
        off = pl.multiple_of(j * tq, tq)
        for h in range(2):
            s = s_ref[slot, h]
            if causal_mask is not None:
                s = jnp.where(causal_mask, s, NEG_INF)
            m_prev = m_ref[h]
            m_new = jnp.maximum(m_prev, jnp.max(s, axis=0, keepdims=True))
            alpha = jnp.exp2(m_prev - m_new)
            p = jnp.exp2(s - m_new).astype(BF16)
            vth = vt_ref[0, h * VT_ROWS:(h + 1) * VT_ROWS, pl.ds(off, tq)]
            acc_ref[h] = acc_ref[h] * alpha + jnp.dot(vth, p, preferred_element_type=F32)
            m_ref[h] = m_new

    if fox:
        base = b * n_tiles
        qn = []
        for h in range(2):
            qf = qth[h].astype(F32)
            qn.append(jnp.sqrt(jnp.max(jnp.sum(qf * qf, axis=0, keepdims=True))) * NORM_SLACK)

        def first_live(j, j0):
            ub = [qn[h] * (kn_ref[base + j, 2 * pair + h] + kn_ref[base + i, 2 * pair + h])
                  + cf_ref[base + i, 2 * pair + h] - cl_ref[base + j, 2 * pair + h] for h in range(2)]
            return jnp.where(jnp.maximum(ub[0], ub[1]) > SKIP_LOG2, jnp.minimum(j0, j), j0)

        j0 = lax.fori_loop(0, i, first_live, i)
    else:
        j0 = 0
    n_before = i - j0
    odd = (n_before % 2) == 1

    @pl.when(odd)
    def _():
        logits_to(1, j0)

    @pl.when(jnp.logical_not(odd))
    def _():
        logits_to(0, j0)

    @pl.when(odd)
    def _():
        logits_to(0, j0 + 1)
        softmax_pv(1, j0, None)

    def body(t, carry):
        j = j0 + (n_before % 2) + 2 * t
        logits_to(1, j + 1)
        softmax_pv(0, j, None)
        logits_to(0, j + 2)
        softmax_pv(1, j + 1, None)
        return carry

    lax.fori_loop(0, n_before // 2, body, 0)
    key = lax.broadcasted_iota(jnp.int32, (tq, tq), 0)
    qry = lax.broadcasted_iota(jnp.int32, (tq, tq), 1)
    softmax_pv(0, i, key <= qry)
    out_t = jnp.concatenate(
        [acc_ref[h, :HEAD_DIM] * (1.0 / acc_ref[h, HEAD_DIM:HEAD_DIM + 1]) for h in range(2)],
        axis=0)
    o_ref[0] = out_t.T.astype(o_ref.dtype)


def _causal_attention(qt, k, vt, fox_args, *, slab):
    nb, seq, _ = k.shape
    tq = ATT_TQ
    n_tiles = seq // tq
    n_pairs = FOX_HEADS // 2
    rows = 2 * MLA_SLAB if slab else LANES
    fox = fox_args is not None
    in_specs = [pl.BlockSpec((1, rows, tq), lambda b, p, i: (b, p, i)),
                pl.BlockSpec((1, seq, rows), lambda b, p, i: (b, 0, p)),
                pl.BlockSpec((1, 2 * VT_ROWS, seq), lambda b, p, i: (b, p, 0))]
    args = [qt, k, vt]
    if fox:
        cum, cumt, cfirst, clast, knorm = fox_args
        smem = pl.BlockSpec(memory_space=pltpu.SMEM)
        in_specs = [smem, smem, smem] + in_specs + [
            pl.BlockSpec((1, 16, tq), lambda b, p, i: (b, 0, i)),
            pl.BlockSpec((1, seq, LANES), lambda b, p, i: (b, 0, 0))]
        args = [cfirst, clast, knorm] + args + [cumt, cum]
    return pl.pallas_call(
        functools.partial(_causal_attn_kernel, tq=tq, n_tiles=n_tiles, fox=fox, slab=slab),
        grid=(nb, n_pairs, n_tiles),
        in_specs=in_specs,
        out_specs=pl.BlockSpec((1, tq, LANES), lambda b, p, i: (b, i, p)),
        out_shape=jax.ShapeDtypeStruct((nb, seq, n_pairs * LANES), BF16),
        scratch_shapes=[pltpu.VMEM((2, 1, tq), F32), pltpu.VMEM((2, VT_ROWS, tq), F32),
                        pltpu.VMEM((2, 2, tq, tq), F32)],
        compiler_params=_cparams(3),
        name="fox_attention" if fox else "mla_attention",
    )(*args)


def _mem_attn_kernel(q_ref, k_ref, v_ref, o_ref, m_ref, l_ref, acc_ref):
    _init_state(m_ref, l_ref, acc_ref)
    q = q_ref[0]
    kc = k_ref[0]
    vc = v_ref[0]
    for h in range(2):
        qh, kh, vh = _head_operands(h, q, kc, vc, False)
        s = lax.dot_general(qh, kh, (((1,), (1,)), ((), ())), preferred_element_type=F32)
        _softmax_chunk(h, s, vh, m_ref, l_ref, acc_ref)
    _finish(o_ref, l_ref, acc_ref)


def _memory_attention(mq, mk, mv):
    nb, seq, _ = mq.shape
    tq = ATT_TQ
    n_pairs = MEM_HEADS // 2
    return pl.pallas_call(
        _mem_attn_kernel,
        grid=(nb, n_pairs, seq // tq),
        in_specs=[pl.BlockSpec((1, tq, LANES), lambda b, p, i: (b, i, p)),
                  pl.BlockSpec((1, N_MEM, LANES), lambda b, p, i: (b, 0, p)),
                  pl.BlockSpec((1, N_MEM, LANES), lambda b, p, i: (b, 0, p))],
        out_specs=pl.BlockSpec((1, tq, LANES), lambda b, p, i: (b, i, p)),
        out_shape=jax.ShapeDtypeStruct((nb, seq, MEM_WIDTH), BF16),
        scratch_shapes=[pltpu.VMEM((2, tq, 1), F32), pltpu.VMEM((2, tq, 1), F32),
                        pltpu.VMEM((2, tq, LANES), F32)],
        compiler_params=_cparams(3),
        name="memory_attention",
    )(mq, mk, mv)


def _post_attn_kernel(x_ref, att_ref, memo_ref, wa_ref, wm_ref, g_ref, b_ref, wr_ref, br_ref,
                      x1_ref, x1b_ref, idx_ref, rank_ref, gate_ref, cnt_ref, carry_ref):
    t = pl.program_id(0)

    @pl.when(t == 0)
    def _():
        carry_ref[...] = jnp.zeros_like(carry_ref)

    tm = x_ref.shape[0]
    mix = (jnp.dot(att_ref[...], wa_ref[...], preferred_element_type=F32)
           + jnp.dot(memo_ref[...], wm_ref[...], preferred_element_type=F32))
    x1 = _layer_norm(DEEPNORM_ALPHA * x_ref[...] + mix, g_ref[...], b_ref[...])
    x1_ref[...] = x1
    x1b = x1.astype(BF16)
    x1b_ref[...] = x1b
    logits = jnp.dot(x1b, wr_ref[...], preferred_element_type=F32) + br_ref[...]
    lane = lax.broadcasted_iota(jnp.int32, (tm, LANES), 1)
    work = jnp.where(lane < N_EXPERTS, logits, -jnp.inf)
    idxs, vals = [], []
    onehot = jnp.zeros((tm, LANES), F32)
    for _ in range(TOP_K):
        best = jnp.max(work, axis=1, keepdims=True)
        where_best = jnp.min(jnp.where(work == best, lane, LANES), axis=1, keepdims=True)
        hit = lane == where_best
        onehot = jnp.where(hit, 1.0, onehot)
        work = jnp.where(hit, -jnp.inf, work)
        idxs.append(where_best)
        vals.append(best)
    exps = [jnp.exp(v - vals[0]) for v in vals]
    denom = exps[0] + exps[1] + exps[2] + exps[3]
    row = lax.broadcasted_iota(jnp.int32, (tm, tm), 0)
    col = lax.broadcasted_iota(jnp.int32, (tm, tm), 1)
    strict = jnp.where(row > col, 1.0, 0.0).astype(BF16)
    before = jnp.dot(strict, onehot.astype(BF16), preferred_element_type=F32) + carry_ref[...]
    idx_out = jnp.zeros((tm, LANES), jnp.int32)
    rank_out = jnp.zeros((tm, LANES), jnp.int32)
    gate_out = jnp.zeros((tm, LANES), F32)
    for r in range(TOP_K):
        rank_r = jnp.sum(jnp.where(lane == idxs[r], before, 0.0), axis=1, keepdims=True)
        idx_out = jnp.where(lane == r, idxs[r], idx_out)
        rank_out = jnp.where(lane == r, rank_r.astype(jnp.int32), rank_out)
        gate_out = jnp.where(lane == r, exps[r] / denom, gate_out)
    idx_ref[...] = idx_out
    rank_ref[...] = rank_out
    gate_ref[...] = gate_out
    total = carry_ref[...] + jnp.sum(onehot, axis=0, keepdims=True)
    carry_ref[...] = total
    cnt_ref[...] = jnp.broadcast_to(total, cnt_ref.shape)


def _post_attn(x2d, att, memo, wa, wm, g, b, wr, br):
    t_tokens = x2d.shape[0]
    tm = PROJ_TM
    row_spec = lambda width: pl.BlockSpec((tm, width), lambda i: (i, 0))
    full = lambda a: pl.BlockSpec(a.shape, lambda i: (0, 0))
    return pl.pallas_call(
        _post_attn_kernel,
        grid=(t_tokens // tm,),
        in_specs=[row_spec(D_MODEL), row_spec(att.shape[1]), row_spec(MEM_WIDTH),
                  full(wa), full(wm), full(g), full(b), full(wr), full(br)],
        out_specs=[row_spec(D_MODEL), row_spec(D_MODEL), row_spec(LANES), row_spec(LANES),
                   row_spec(LANES), pl.BlockSpec((8, LANES), lambda i: (0, 0))],
        out_shape=[jax.ShapeDtypeStruct((t_tokens, D_MODEL), F32),
                   jax.ShapeDtypeStruct((t_tokens, D_MODEL), BF16),
                   jax.ShapeDtypeStruct((t_tokens, LANES), jnp.int32),
                   jax.ShapeDtypeStruct((t_tokens, LANES), jnp.int32),
                   jax.ShapeDtypeStruct((t_tokens, LANES), F32),
                   jax.ShapeDtypeStruct((8, LANES), F32)],
        scratch_shapes=[pltpu.VMEM((1, LANES), F32)],
        compiler_params=_cparams(1),
        name="outproj_ln_router",
    )(x2d, att, memo, wa, wm, g, b, wr, br)


def _expert_kernel(be_ref, nu_ref, x_ref, wgu_ref, bgu_ref, wdn_ref, bdn_ref, o_ref,
                   wgu_bf, wdn_bf, *, n_chunks):
    blk = pl.program_id(0)

    @pl.when((blk == 0) | (be_ref[blk] != be_ref[jnp.maximum(blk - 1, 0)]))
    def _():
        wgu_bf[...] = wgu_ref[0, 0].astype(BF16)
        wdn_bf[...] = wdn_ref[0, 0].astype(BF16)

    @pl.when(blk < nu_ref[0])
    def _():
        x = x_ref[...]
        cw = D_EXPERT // n_chunks
        acc = jnp.zeros(o_ref.shape, F32)
        for c in range(n_chunks):
            gs = slice(c * cw, (c + 1) * cw)
            us = slice(D_EXPERT + c * cw, D_EXPERT + (c + 1) * cw)
            g = jnp.dot(x, wgu_bf[:, gs], preferred_element_type=F32) + bgu_ref[0, 0, :, gs]
            u = jnp.dot(x, wgu_bf[:, us], preferred_element_type=F32) + bgu_ref[0, 0, :, us]
            g = jnp.minimum(g, SWIGLU_LIMIT)
            u = jnp.clip(u, -SWIGLU_LIMIT, SWIGLU_LIMIT)
            hmid = (u + 1.0) * (g * jax.nn.sigmoid(SWIGLU_ALPHA * g))
            acc = acc + jnp.dot(hmid.astype(BF16), wdn_bf[gs, :], preferred_element_type=F32)
        o_ref[...] = acc + bdn_ref[0, 0]

    @pl.when(blk >= nu_ref[0])
    def _():
        o_ref[...] = jnp.zeros_like(o_ref)


def _experts(block_expert, n_used, xb, layer, wgu, bgu, wdn, bdn):
    p_rows = xb.shape[0]
    tm = MOE_TM
    grid_spec = pltpu.PrefetchScalarGridSpec(
        num_scalar_prefetch=2,
        grid=(p_rows // tm,),
        in_specs=[pl.BlockSpec((tm, D_MODEL), lambda i, be, nu: (i, 0)),
                  pl.BlockSpec((1, 1, D_MODEL, 2 * D_EXPERT), lambda i, be, nu: (layer, be[i], 0, 0)),
                  pl.BlockSpec((1, 1, 1, 2 * D_EXPERT), lambda i, be, nu: (layer, be[i], 0, 0)),
                  pl.BlockSpec((1, 1, D_EXPERT, D_MODEL), lambda i, be, nu: (layer, be[i], 0, 0)),
                  pl.BlockSpec((1, 1, 1, D_MODEL), lambda i, be, nu: (layer, be[i], 0, 0))],
        out_specs=pl.BlockSpec((tm, D_MODEL), lambda i, be, nu: (i, 0)),
        scratch_shapes=[pltpu.VMEM((D_MODEL, 2 * D_EXPERT), BF16),
                        pltpu.VMEM((D_EXPERT, D_MODEL), BF16)],
    )
    return pl.pallas_call(
        functools.partial(_expert_kernel, n_chunks=4),
        grid_spec=grid_spec,
        out_shape=jax.ShapeDtypeStruct((p_rows, D_MODEL), F32),
        compiler_params=pltpu.CompilerParams(dimension_semantics=("arbitrary",),
                                             vmem_limit_bytes=EXPERT_VMEM_LIMIT),
        name="experts",
    )(block_expert, n_used, xb, wgu, bgu, wdn, bdn)


def _sc_gather_rows(table, idx):
    n_idx = idx.shape[0]
    width = table.shape[1]
    per_worker = n_idx // SC_WORKERS
    n_chunks = per_worker // SC_CHUNK
    mesh = plsc.VectorSubcoreMesh(core_axis_name="c", subcore_axis_name="s",
                                  num_cores=SC_CORES, num_subcores=SC_SUBCORES)

    def body(table_hbm, idx_hbm, out_hbm, idx_v, rows_v, sem):
        wid = lax.axis_index("s") * SC_CORES + lax.axis_index("c")
        base = wid * per_worker
        pltpu.sync_copy(idx_hbm.at[pl.ds(base, per_worker)], idx_v)

        @pl.loop(0, n_chunks)
        def _(c):
            off = pl.multiple_of(c * SC_CHUNK, SC_CHUNK)
            pltpu.async_copy(table_hbm.at[idx_v.at[pl.ds(off, SC_CHUNK)]], rows_v, sem).wait()
            pltpu.sync_copy(rows_v, out_hbm.at[pl.ds(base + off, SC_CHUNK)])

    return pl.kernel(
        body,
        out_type=jax.ShapeDtypeStruct((n_idx, width), table.dtype),
        mesh=mesh,
        scratch_types=[pltpu.VMEM((per_worker,), jnp.int32),
                       pltpu.VMEM((SC_CHUNK, width), table.dtype),
                       pltpu.SemaphoreType.DMA],
        name="sc_gather_rows",
    )(table, idx)


def _combine_kernel(x1_ref, yg_ref, gate_ref, g_ref, b_ref, o_ref):
    gates = gate_ref[...]
    ffn = yg_ref[0] * gates[:, 0:1]
    for r in range(1, TOP_K):
        ffn = ffn + yg_ref[r] * gates[:, r:r + 1]
    o_ref[...] = _layer_norm(DEEPNORM_ALPHA * x1_ref[...] + ffn, g_ref[...], b_ref[...])


def _combine(x1, yg, gates, g, b):
    t_tokens = x1.shape[0]
    tm = 256
    return pl.pallas_call(
        _combine_kernel,
        grid=(t_tokens // tm,),
        in_specs=[pl.BlockSpec((tm, D_MODEL), lambda i: (i, 0)),
                  pl.BlockSpec((TOP_K, tm, D_MODEL), lambda i: (0, i, 0)),
                  pl.BlockSpec((tm, LANES), lambda i: (i, 0)),
                  pl.BlockSpec((1, D_MODEL), lambda i: (0, 0)),
                  pl.BlockSpec((1, D_MODEL), lambda i: (0, 0))],
        out_specs=pl.BlockSpec((tm, D_MODEL), lambda i: (i, 0)),
        out_shape=jax.ShapeDtypeStruct((t_tokens, D_MODEL), F32),
        compiler_params=_cparams(1),
        name="combine_ln",
    )(x1, yg, gates, g, b)


def _moe(x1, x1b, idx, rank, gates, counts, layer, wgu, bgu, wdn, bdn, g, b):
    t_tokens = x1.shape[0]
    tm = MOE_TM
    counts = counts[0, :N_EXPERTS].astype(jnp.int32)
    padded = ((counts + tm - 1) // tm) * tm
    pend = jnp.cumsum(padded)
    pstart = pend - padded
    idx4 = idx[:, :TOP_K]
    dest = (pstart[idx4] + rank[:, :TOP_K]).reshape(-1)
    n_blocks = (t_tokens * TOP_K) // tm + N_EXPERTS
    p_rows = n_blocks * tm
    block_start = jnp.arange(n_blocks, dtype=jnp.int32) * tm
    block_expert = jnp.minimum(
        jnp.sum((pend[None, :] <= block_start[:, None]).astype(jnp.int32), axis=1), N_EXPERTS - 1)
    n_used = (pend[-1:] // tm).astype(jnp.int32)
    tok = jnp.arange(t_tokens * TOP_K, dtype=jnp.int32) // TOP_K
    slot_tok = jnp.full((p_rows,), t_tokens, jnp.int32).at[dest].set(tok)
    x_pad = jnp.concatenate([x1b, jnp.zeros((1, D_MODEL), x1b.dtype)], axis=0)
    xb = x_pad[slot_tok]
    y = _experts(block_expert, n_used, xb, layer, wgu, bgu, wdn, bdn)
    dest_by_choice = dest.reshape(t_tokens, TOP_K).T.reshape(-1)
    yg = _sc_gather_rows(y, dest_by_choice).reshape(TOP_K, t_tokens, D_MODEL)
    return _combine(x1, yg, gates, g, b)


def _row(v, width=None):
    v = v.astype(F32).reshape(1, -1)
    if width is not None and v.shape[1] < width:
        v = jnp.pad(v, ((0, 0), (0, width - v.shape[1])))
    return v


def _pad_cols(w, width):
    return jnp.pad(w, ((0, 0), (0, width - w.shape[1])))


def kernel(x, mem, positions, a_w_in, a_b_f, a_w_out, b_w_in, b_g_q, b_w_uq, b_w_out,
           kv_w_dkv, kv_g, kv_w_ukv, mem_w_kv, ln_g, ln_b,
           moe_w_r, moe_b_r, moe_w_gu, moe_b_gu, moe_w_dn, moe_b_dn):
    nb, seq, d = x.shape
    t_tokens = nb * seq
    n_a = a_w_in.shape[0]
    x2d = x.reshape(t_tokens, d)
    mem2d = mem.reshape(nb * N_MEM, d)
    pos2d = positions.reshape(t_tokens, 1)
    half = QK_ROPE // 2
    inv_freq = ROPE_THETA ** (-jnp.arange(half, dtype=F32) * 2.0 / QK_ROPE)
    invf = jnp.zeros((1, LANES), F32)
    invf = invf.at[0, QK_NOPE:QK_NOPE + half].set(inv_freq)
    invf = invf.at[0, QK_NOPE + half:QK_NOPE + QK_ROPE].set(inv_freq)

    shared_kv = None
    for l in range(DEPTH):
        mkv = _matmul(mem2d, mem_w_kv[l].astype(BF16), BF16, tm=nb * N_MEM)
        mk = mkv[:, :MEM_WIDTH].reshape(nb, N_MEM, MEM_WIDTH)
        mv = mkv[:, MEM_WIDTH:].reshape(nb, N_MEM, MEM_WIDTH)
        if l < n_a:
            w_in = a_w_in[l]
            w = jnp.concatenate([w_in[:, :3 * FOX_WIDTH],
                                 _pad_cols(w_in[:, 3 * FOX_WIDTH:3 * FOX_WIDTH + FOX_HEADS], LANES),
                                 w_in[:, 3 * FOX_WIDTH + FOX_HEADS:]], axis=1).astype(BF16)
            qt, k, vt, mq, cum, cumt, cfirst, clast, knorm = _fox_proj(
                x2d, w, _row(a_b_f[l], LANES), seq)
            stats = [s.reshape(-1, LANES) for s in (cfirst, clast, knorm)]
            att = _causal_attention(qt, k.reshape(nb, seq, -1), vt,
                                    [cum.reshape(nb, seq, LANES), cumt] + stats, slab=False)
            w_out = a_w_out[l]
        else:
            bl = l - n_a
            w = jnp.concatenate([b_w_in[bl], kv_w_dkv[:, :KV_LORA],
                                 jnp.zeros((d, QK_NOPE), F32), kv_w_dkv[:, KV_LORA:],
                                 jnp.zeros((d, LANES - QK_NOPE - QK_ROPE), F32)], axis=1).astype(BF16)
            wuq = b_w_uq[bl].reshape(Q_LORA, MLA_HEADS, QK_NOPE + QK_ROPE)
            wuq = jnp.pad(wuq, ((0, 0), (0, 0), (0, MLA_SLAB - QK_NOPE - QK_ROPE)))
            wuq = wuq.reshape(Q_LORA, MLA_HEADS * MLA_SLAB).astype(BF16)
            wukv = kv_w_ukv.reshape(KV_LORA, MLA_HEADS, QK_NOPE + V_DIM)
            wk = jnp.pad(wukv[:, :, :QK_NOPE], ((0, 0), (0, 0), (0, MLA_SLAB - QK_NOPE)))
            wk = wk.reshape(KV_LORA, MLA_HEADS * MLA_SLAB).astype(BF16)
            wv = wukv[:, :, QK_NOPE:].reshape(KV_LORA, MLA_V_WIDTH).astype(BF16)
            qt, mq, k_new, vt_new = _mla_proj(x2d, pos2d, invf, w, _row(b_g_q[bl]), wuq,
                                              _row(kv_g), wk, wv, seq)
            if shared_kv is None:
                shared_kv = (k_new.reshape(nb, seq, -1), vt_new)
            att = _causal_attention(qt, shared_kv[0], shared_kv[1], None, slab=True)
            w_out = b_w_out[bl]
        memo = _memory_attention(mq.reshape(nb, seq, MEM_WIDTH), mk, mv)
        n_att = w_out.shape[0] - MEM_WIDTH
        x1, x1b, idx, rank, gates, counts = _post_attn(
            x2d, att.reshape(t_tokens, -1), memo.reshape(t_tokens, MEM_WIDTH),
            w_out[:n_att].astype(BF16), w_out[n_att:].astype(BF16),
            _row(ln_g[l, 0]), _row(ln_b[l, 0]),
            _pad_cols(moe_w_r[l], LANES).astype(BF16), _row(moe_b_r[l], LANES))
        x2d = _moe(x1, x1b, idx, rank, gates, counts, l,
                   moe_w_gu, moe_b_gu.reshape(DEPTH, N_EXPERTS, 1, -1),
                   moe_w_dn, moe_b_dn.reshape(DEPTH, N_EXPERTS, 1, -1),
                   _row(ln_g[l, 1]), _row(ln_b[l, 1]))
    return x2d.reshape(nb, seq, d)
```

```python
import functools
import math

import jax
import jax.numpy as jnp
from jax import lax
from jax.experimental import pallas as pl
from jax.experimental.pallas import tpu as pltpu
from jax.experimental.pallas import tpu_sc as plsc

F32 = jnp.float32
BF16 = jnp.bfloat16

D_MODEL = 1024
DEPTH = 2
N_MEM = 256
HEAD_DIM = 64
FOX_HEADS = 12
MEM_HEADS = 4
MLA_HEADS = 12
Q_LORA = 384
KV_LORA = 256
QK_NOPE = 64
QK_ROPE = 32
V_DIM = 64
ROPE_THETA = 10000.0
N_EXPERTS = 32
TOP_K = 4
D_EXPERT = D_MODEL
SWIGLU_LIMIT = 7.0
SWIGLU_ALPHA = 1.702
LN_EPS = 1e-5
RMS_EPS = 1e-6
NEG_INF = -1e30
DEEPNORM_ALPHA = (2 * DEPTH) ** 0.25
FOX_WIDTH = FOX_HEADS * HEAD_DIM
MEM_WIDTH = MEM_HEADS * HEAD_DIM
MLA_V_WIDTH = MLA_HEADS * V_DIM

LANES = 128
LOG2E = math.log2(math.e)
VMEM_LIMIT = 48 * 1024 * 1024
EXPERT_VMEM_LIMIT = 56 * 1024 * 1024

PROJ_TM = 512
ATT_TQ = 512
MOE_TM = 256
MLA_SLAB = LANES
VT_ROWS = HEAD_DIM + 16
SC_CORES = 2
SC_SUBCORES = 16
SC_WORKERS = SC_CORES * SC_SUBCORES
SC_CHUNK = 32


def _cparams(n_axes):
    return pltpu.CompilerParams(dimension_semantics=("arbitrary",) * n_axes,
                                vmem_limit_bytes=VMEM_LIMIT)


def _split3(x):
    hi = x.astype(BF16)
    r1 = x - hi.astype(F32)
    mid = r1.astype(BF16)
    lo = (r1 - mid.astype(F32)).astype(BF16)
    return hi, mid, lo


def _layer_norm(y, g, b):
    mu = jnp.mean(y, axis=-1, keepdims=True)
    yc = y - mu
    var = jnp.mean(yc * yc, axis=-1, keepdims=True)
    return yc * lax.rsqrt(var + LN_EPS) * g + b


def _rms_norm(y, g):
    return y * lax.rsqrt(jnp.mean(y * y, axis=-1, keepdims=True) + RMS_EPS) * g


def _mm_kernel(a_ref, b_ref, o_ref):
    o_ref[...] = jnp.dot(a_ref[...].astype(BF16), b_ref[...],
                         preferred_element_type=F32).astype(o_ref.dtype)


def _matmul(a, b, out_dtype, tm):
    m, k = a.shape
    n = b.shape[1]
    return pl.pallas_call(
        _mm_kernel,
        grid=(m // tm,),
        in_specs=[pl.BlockSpec((tm, k), lambda i: (i, 0)),
                  pl.BlockSpec((k, n), lambda i: (0, 0))],
        out_specs=pl.BlockSpec((tm, n), lambda i: (i, 0)),
        out_shape=jax.ShapeDtypeStruct((m, n), out_dtype),
        compiler_params=_cparams(1),
        name="matmul",
    )(a, b)


def _store_transposed(dst_ref, val):
    for s in range(val.shape[1] // LANES):
        sl = slice(s * LANES, (s + 1) * LANES)
        dst_ref[0, sl, :] = val[:, sl].T.astype(dst_ref.dtype)


def _store_values_transposed(vt_ref, val):
    tm = val.shape[0]
    ones = jnp.ones((VT_ROWS - HEAD_DIM, tm), vt_ref.dtype)
    for s in range(val.shape[1] // LANES):
        pair_t = val[:, s * LANES:(s + 1) * LANES].T.astype(vt_ref.dtype)
        for h in range(2):
            r0 = (2 * s + h) * VT_ROWS
            vt_ref[0, r0:r0 + HEAD_DIM, :] = pair_t[h * HEAD_DIM:(h + 1) * HEAD_DIM]
            vt_ref[0, r0 + HEAD_DIM:r0 + VT_ROWS, :] = ones


def _fox_proj_kernel(x_ref, w_ref, bf_ref, seg_ref, qt_ref, k_ref, vt_ref, mq_ref, cum_ref,
                     cumt_ref, cfirst_ref, clast_ref, knorm_ref, carry_ref, *, tiles_per_batch):
    t = pl.program_id(0)

    @pl.when(t % tiles_per_batch == 0)
    def _():
        carry_ref[...] = jnp.zeros_like(carry_ref)

    tm = x_ref.shape[0]
    proj = jnp.dot(x_ref[...].astype(BF16), w_ref[...], preferred_element_type=F32)
    qscale = HEAD_DIM ** -0.5 * LOG2E
    _store_transposed(qt_ref, proj[:, :FOX_WIDTH] * qscale)
    kb = proj[:, FOX_WIDTH:2 * FOX_WIDTH].astype(BF16)
    k_ref[...] = kb
    _store_values_transposed(vt_ref, proj[:, 2 * FOX_WIDTH:3 * FOX_WIDTH])
    kf = kb.astype(F32)
    ksq = jnp.dot((kf * kf).astype(BF16), seg_ref[...], preferred_element_type=F32)
    knorm_ref[0] = jnp.sqrt(jnp.max(ksq, axis=0, keepdims=True))
    f = proj[:, 3 * FOX_WIDTH:3 * FOX_WIDTH + LANES] + bf_ref[...]
    mq_ref[...] = (proj[:, 3 * FOX_WIDTH + LANES:] * qscale).astype(BF16)
    log_f = jnp.minimum(f, 0.0) - jnp.log1p(jnp.exp(-jnp.abs(f)))
    row = lax.broadcasted_iota(jnp.int32, (tm, tm), 0)
    col = lax.broadcasted_iota(jnp.int32, (tm, tm), 1)
    tri = jnp.where(row >= col, 1.0, 0.0).astype(BF16)
    hi, mid, lo = _split3(log_f)
    cum = (jnp.dot(tri, hi, preferred_element_type=F32)
           + jnp.dot(tri, mid, preferred_element_type=F32)
           + jnp.dot(tri, lo, preferred_element_type=F32)) + carry_ref[...]
    carry_ref[...] = cum[tm - 1:tm, :]
    cum2 = cum * LOG2E
    cum_ref[...] = cum2
    cumt_ref[0] = cum2.T[:16, :]
    cfirst_ref[0] = cum2[0:1, :]
    clast_ref[0] = cum2[tm - 1:tm, :]


def _fox_proj(x2d, w, bf, seq):
    t_tokens = x2d.shape[0]
    tm = PROJ_TM
    nb = t_tokens // seq
    n = w.shape[1]
    tiles_per_batch = seq // tm
    n_tiles = t_tokens // tm
    seg = (jnp.arange(FOX_WIDTH)[:, None] // HEAD_DIM == jnp.arange(LANES)[None, :]).astype(BF16)
    row_spec = lambda width: pl.BlockSpec((tm, width), lambda i: (i, 0))
    t_spec = lambda rows: pl.BlockSpec(
        (1, rows, tm), lambda i: (i // tiles_per_batch, 0, i % tiles_per_batch))
    stat_spec = pl.BlockSpec((1, 1, LANES), lambda i: (i, 0, 0))
    stat_shape = jax.ShapeDtypeStruct((n_tiles, 1, LANES), F32)
    return pl.pallas_call(
        functools.partial(_fox_proj_kernel, tiles_per_batch=tiles_per_batch),
        grid=(n_tiles,),
        in_specs=[row_spec(D_MODEL),
                  pl.BlockSpec((D_MODEL, n), lambda i: (0, 0)),
                  pl.BlockSpec((1, LANES), lambda i: (0, 0)),
                  pl.BlockSpec((FOX_WIDTH, LANES), lambda i: (0, 0))],
        out_specs=[t_spec(FOX_WIDTH), row_spec(FOX_WIDTH), t_spec(FOX_HEADS * VT_ROWS),
                   row_spec(MEM_WIDTH), row_spec(LANES), t_spec(16),
                   stat_spec, stat_spec, stat_spec],
        out_shape=[jax.ShapeDtypeStruct((nb, FOX_WIDTH, seq), BF16),
                   jax.ShapeDtypeStruct((t_tokens, FOX_WIDTH), BF16),
                   jax.ShapeDtypeStruct((nb, FOX_HEADS * VT_ROWS, seq), BF16),
                   jax.ShapeDtypeStruct((t_tokens, MEM_WIDTH), BF16),
                   jax.ShapeDtypeStruct((t_tokens, LANES), F32),
                   jax.ShapeDtypeStruct((nb, 16, seq), F32),
                   stat_shape, stat_shape, stat_shape],
        scratch_shapes=[pltpu.VMEM((1, LANES), F32)],
        compiler_params=_cparams(1),
        name="fox_proj",
    )(x2d, w, bf, seg)


def _rope_tables(pos_ref, invf_ref):
    ang = pos_ref[...].astype(F32) * invf_ref[...]
    cos = jnp.cos(ang)
    sin = jnp.sin(ang)
    lane = lax.broadcasted_iota(jnp.int32, ang.shape, 1)
    half = QK_ROPE // 2
    in_x1 = (lane >= QK_NOPE) & (lane < QK_NOPE + half)
    in_x2 = (lane >= QK_NOPE + half) & (lane < QK_NOPE + QK_ROPE)
    c_tab = jnp.where(lane < QK_NOPE, 1.0, jnp.where(in_x1 | in_x2, cos, 0.0))
    s_up = jnp.where(in_x2, sin, 0.0)
    s_dn = jnp.where(in_x1, -sin, 0.0)
    return c_tab, s_up, s_dn


def _rope_slab(slab, tables):
    c_tab, s_up, s_dn = tables
    half = QK_ROPE // 2
    up = pltpu.roll(slab, half, 1)
    dn = pltpu.roll(slab, LANES - half, 1)
    return slab * c_tab + up * s_up + dn * s_dn


def _mla_proj_kernel(x_ref, pos_ref, invf_ref, w_ref, gq_ref, wuq_ref, gkv_ref, wk_ref, wv_ref,
                     qt_ref, mq_ref, k_ref, vt_ref):
    proj = jnp.dot(x_ref[...].astype(BF16), w_ref[...], preferred_element_type=F32)
    tables = _rope_tables(pos_ref, invf_ref)
    c_q = _rms_norm(proj[:, :Q_LORA], gq_ref[...])
    mq_ref[...] = (proj[:, Q_LORA:Q_LORA + MEM_WIDTH] * (HEAD_DIM ** -0.5 * LOG2E)).astype(BF16)
    kv_off = Q_LORA + MEM_WIDTH
    c_kv = _rms_norm(proj[:, kv_off:kv_off + KV_LORA], gkv_ref[...])
    kr = _rope_slab(proj[:, kv_off + KV_LORA:], tables)
    q = jnp.dot(c_q.astype(BF16), wuq_ref[...], preferred_element_type=F32)
    kn = jnp.dot(c_kv.astype(BF16), wk_ref[...], preferred_element_type=F32)
    qscale = (QK_NOPE + QK_ROPE) ** -0.5 * LOG2E
    for h in range(MLA_HEADS):
        sl = slice(h * MLA_SLAB, (h + 1) * MLA_SLAB)
        qt_ref[0, sl, :] = (_rope_slab(q[:, sl], tables) * qscale).T.astype(BF16)
        k_ref[:, sl] = (kn[:, sl] + kr).astype(BF16)
    _store_values_transposed(vt_ref, jnp.dot(c_kv.astype(BF16), wv_ref[...],
                                             preferred_element_type=F32))


def _mla_proj(x2d, pos2d, invf, w, gq, wuq, gkv, wk, wv, seq):
    t_tokens = x2d.shape[0]
    tm = PROJ_TM
    nb = t_tokens // seq
    tiles_per_batch = seq // tm
    row_spec = lambda width: pl.BlockSpec((tm, width), lambda i: (i, 0))
    t_spec = lambda rows: pl.BlockSpec(
        (1, rows, tm), lambda i: (i // tiles_per_batch, 0, i % tiles_per_batch))
    full = lambda a: pl.BlockSpec(a.shape, lambda i: (0, 0))
    slabs = MLA_HEADS * MLA_SLAB
    return pl.pallas_call(
        _mla_proj_kernel,
        grid=(t_tokens // tm,),
        in_specs=[row_spec(D_MODEL), row_spec(1), full(invf), full(w), full(gq), full(wuq),
                  full(gkv), full(wk), full(wv)],
        out_specs=[t_spec(slabs), row_spec(MEM_WIDTH), row_spec(slabs),
                   t_spec(MLA_HEADS * VT_ROWS)],
        out_shape=[jax.ShapeDtypeStruct((nb, slabs, seq), BF16),
                   jax.ShapeDtypeStruct((t_tokens, MEM_WIDTH), BF16),
                   jax.ShapeDtypeStruct((t_tokens, slabs), BF16),
                   jax.ShapeDtypeStruct((nb, MLA_HEADS * VT_ROWS, seq), BF16)],
        compiler_params=_cparams(1),
        name="mla_proj",
    )(x2d, pos2d, invf, w, gq, wuq, gkv, wk, wv)


def _softmax_chunk(h, s, vh, m_ref, l_ref, acc_ref):
    m_prev = m_ref[h]
    m_new = jnp.maximum(m_prev, jnp.max(s, axis=1, keepdims=True))
    alpha = jnp.exp2(m_prev - m_new)
    p = jnp.exp2(s - m_new)
    l_ref[h] = alpha * l_ref[h] + jnp.sum(p, axis=1, keepdims=True)
    acc_ref[h] = acc_ref[h] * alpha + jnp.dot(p.astype(BF16), vh, preferred_element_type=F32)
    m_ref[h] = m_new


def _init_state(m_ref, l_ref, acc_ref):
    m_ref[...] = jnp.full(m_ref.shape, NEG_INF, F32)
    l_ref[...] = jnp.zeros(l_ref.shape, F32)
    acc_ref[...] = jnp.zeros(acc_ref.shape, F32)


def _head_operands(h, q, kc, vc, q_slab):
    lane = lax.broadcasted_iota(jnp.int32, (1, LANES), 1)
    mine = (lane < HEAD_DIM) if h == 0 else (lane >= HEAD_DIM)
    vh = jnp.where(mine, vc, jnp.zeros_like(vc))
    if q_slab:
        sl = slice(h * MLA_SLAB, (h + 1) * MLA_SLAB)
        return q[:, sl], kc[:, sl], vh
    return q, jnp.where(mine, kc, jnp.zeros_like(kc)), vh


def _finish(o_ref, l_ref, acc_ref):
    lane = lax.broadcasted_iota(jnp.int32, (1, LANES), 1)
    o0 = acc_ref[0] / l_ref[0]
    o1 = acc_ref[1] / l_ref[1]
    o_ref[0] = jnp.where(lane < HEAD_DIM, o0, o1).astype(o_ref.dtype)


SKIP_LOG2 = -160.0
NORM_SLACK = 1.02


def _causal_attn_kernel(*refs, tq, n_tiles, fox, slab):
    if fox:
        (cf_ref, cl_ref, kn_ref, qt_ref, k_ref, vt_ref, cq_ref, ck_ref,
         o_ref, m_ref, acc_ref, s_ref) = refs
    else:
        qt_ref, k_ref, vt_ref, o_ref, m_ref, acc_ref, s_ref = refs
    b = pl.program_id(0)
    pair = pl.program_id(1)
    i = pl.program_id(2)
    m_ref[...] = jnp.full(m_ref.shape, NEG_INF, F32)
    acc_ref[...] = jnp.zeros(acc_ref.shape, F32)
    qt = qt_ref[0]
    if slab:
        qth = [qt[:MLA_SLAB], qt[MLA_SLAB:]]
    else:
        rowi = lax.broadcasted_iota(jnp.int32, qt.shape, 0)
        zero = jnp.zeros_like(qt)
        qth = [jnp.where(rowi < HEAD_DIM, qt, zero), jnp.where(rowi >= HEAD_DIM, qt, zero)]
    if fox:
        lane = lax.broadcasted_iota(jnp.int32, (1, LANES), 1)
        cq = [cq_ref[0, pl.ds(2 * pair + h, 1), :] for h in range(2)]

    def logits_to(slot, j):
        off = pl.multiple_of(j * tq, tq)
        kc = k_ref[0, pl.ds(off, tq), :]
        for h in range(2):
            kh = kc[:, h * MLA_SLAB:(h + 1) * MLA_SLAB] if slab else kc
            s = jnp.dot(kh, qth[h], preferred_element_type=F32)
            if fox:
                ck_blk = ck_ref[0, pl.ds(off, tq), :]
                ck = jnp.sum(jnp.where(lane == 2 * pair + h, ck_blk, 0.0), axis=1, keepdims=True)
                s = s + cq[h] - ck
            s_ref[slot, h] = s

    def softmax_pv(slot, j, causal_mask):
        off = pl.multiple_of(j * tq, tq)
        for h in range(2):
            s = s_ref[slot, h]
            if causal_mask is not None:
                s = jnp.where(causal_mask, s, NEG_INF)
            m_prev = m_ref[h]
            m_new = jnp.maximum(m_prev, jnp.max(s, axis=0, keepdims=True))
            alpha = jnp.exp2(m_prev - m_new)
            p = jnp.exp2(s - m_new).astype(BF16)
            vth = vt_ref[0, h * VT_ROWS:(h + 1) * VT_ROWS, pl.ds(off, tq)]
            acc_ref[h] = acc_ref[h] * alpha + jnp.dot(vth, p, preferred_element_type=F32)
            m_ref[h] = m_new

    if fox:
        base = b * n_tiles
        qn = []
        for h in range(2):
            qf = qth[h].astype(F32)
            qn.append(jnp.sqrt(jnp.max(jnp.sum(qf * qf, axis=0, keepdims=True))) * NORM_SLACK)

        def first_live(j, j0):
            ub = [qn[h] * (kn_ref[base + j, 2 * pair + h] + kn_ref[base + i, 2 * pair + h])
                  + cf_ref[base + i, 2 * pair + h] - cl_ref[base + j, 2 * pair + h] for h in range(2)]
            return jnp.where(jnp.maximum(ub[0], ub[1]) > SKIP_LOG2, jnp.minimum(j0, j), j0)

        j0 = lax.fori_loop(0, i, first_live, i)
    else:
        j0 = 0
    n_before = i - j0
    odd = (n_before % 2) == 1

    @pl.when(odd)
    def _():
        logits_to(1, j0)

    @pl.when(jnp.logical_not(odd))
    def _():
        logits_to(0, j0)

    @pl.when(odd)
    def _():
        logits_to(0, j0 + 1)
        softmax_pv(1, j0, None)

    def body(t, carry):
        j = j0 + (n_before % 2) + 2 * t
        logits_to(1, j + 1)
        softmax_pv(0, j, None)
        logits_to(0, j + 2)
        softmax_pv(1, j + 1, None)
        return carry

    lax.fori_loop(0, n_before // 2, body, 0)
    key = lax.broadcasted_iota(jnp.int32, (tq, tq), 0)
    qry = lax.broadcasted_iota(jnp.int32, (tq, tq), 1)
    softmax_pv(0, i, key <= qry)
    out_t = jnp.concatenate(
        [acc_ref[h, :HEAD_DIM] * (1.0 / acc_ref[h, HEAD_DIM:HEAD_DIM + 1]) for h in range(2)],
        axis=0)
    o_ref[0] = out_t.T.astype(o_ref.dtype)


def _causal_attention(qt, k, vt, fox_args, *, slab):
    nb, seq, _ = k.shape
    tq = ATT_TQ
    n_tiles = seq // tq
    n_pairs = FOX_HEADS // 2
    rows = 2 * MLA_SLAB if slab else LANES
    fox = fox_args is not None
    in_specs = [pl.BlockSpec((1, rows, tq), lambda b, p, i: (b, p, i)),
                pl.BlockSpec((1, seq, rows), lambda b, p, i: (b, 0, p)),
                pl.BlockSpec((1, 2 * VT_ROWS, seq), lambda b, p, i: (b, p, 0))]
    args = [qt, k, vt]
    if fox:
        cum, cumt, cfirst, clast, knorm = fox_args
        smem = pl.BlockSpec(memory_space=pltpu.SMEM)
        in_specs = [smem, smem, smem] + in_specs + [
            pl.BlockSpec((1, 16, tq), lambda b, p, i: (b, 0, i)),
            pl.BlockSpec((1, seq, LANES), lambda b, p, i: (b, 0, 0))]
        args = [cfirst, clast, knorm] + args + [cumt, cum]
    return pl.pallas_call(
        functools.partial(_causal_attn_kernel, tq=tq, n_tiles=n_tiles, fox=fox, slab=slab),
        grid=(nb, n_pairs, n_tiles),
        in_specs=in_specs,
        out_specs=pl.BlockSpec((1, tq, LANES), lambda b, p, i: (b, i, p)),
        out_shape=jax.ShapeDtypeStruct((nb, seq, n_pairs * LANES), BF16),
        scratch_shapes=[pltpu.VMEM((2, 1, tq), F32), pltpu.VMEM((2, VT_ROWS, tq), F32),
                        pltpu.VMEM((2, 2, tq, tq), F32)],
        compiler_params=_cparams(3),
        name="fox_attention" if fox else "mla_attention",
    )(*args)


def _mem_attn_kernel(q_ref, k_ref, v_ref, o_ref, m_ref, l_ref, acc_ref):
    _init_state(m_ref, l_ref, acc_ref)
    q = q_ref[0]
    kc = k_ref[0]
    vc = v_ref[0]
    for h in range(2):
        qh, kh, vh = _head_operands(h, q, kc, vc, False)
        s = lax.dot_general(qh, kh, (((1,), (1,)), ((), ())), preferred_element_type=F32)
        _softmax_chunk(h, s, vh, m_ref, l_ref, acc_ref)
    _finish(o_ref, l_ref, acc_ref)


def _memory_attention(mq, mk, mv):
    nb, seq, _ = mq.shape
    tq = ATT_TQ
    n_pairs = MEM_HEADS // 2
    return pl.pallas_call(
        _mem_attn_kernel,
        grid=(nb, n_pairs, seq // tq),
        in_specs=[pl.BlockSpec((1, tq, LANES), lambda b, p, i: (b, i, p)),
                  pl.BlockSpec((1, N_MEM, LANES), lambda b, p, i: (b, 0, p)),
                  pl.BlockSpec((1, N_MEM, LANES), lambda b, p, i: (b, 0, p))],
        out_specs=pl.BlockSpec((1, tq, LANES), lambda b, p, i: (b, i, p)),
        out_shape=jax.ShapeDtypeStruct((nb, seq, MEM_WIDTH), BF16),
        scratch_shapes=[pltpu.VMEM((2, tq, 1), F32), pltpu.VMEM((2, tq, 1), F32),
                        pltpu.VMEM((2, tq, LANES), F32)],
        compiler_params=_cparams(3),
        name="memory_attention",
    )(mq, mk, mv)


def _post_attn_kernel(x_ref, att_ref, memo_ref, wa_ref, wm_ref, g_ref, b_ref, wr_ref, br_ref,
                      x1_ref, x1p_ref, idx_ref, rank_ref, gate_ref, cnt_ref, carry_ref):
    t = pl.program_id(0)

    @pl.when(t == 0)
    def _():
        carry_ref[...] = jnp.zeros_like(carry_ref)

    tm = x_ref.shape[0]
    mix = (jnp.dot(att_ref[...], wa_ref[...], preferred_element_type=F32)
           + jnp.dot(memo_ref[...], wm_ref[...], preferred_element_type=F32))
    x1 = _layer_norm(DEEPNORM_ALPHA * x_ref[...] + mix, g_ref[...], b_ref[...])
    x1_ref[...] = x1
    x1b = x1.astype(BF16)
    bits = pltpu.bitcast(x1b.astype(F32), jnp.uint32)
    half = D_MODEL // 2
    x1p_ref[...] = (bits[:, :half] >> 16) | bits[:, half:]
    logits = jnp.dot(x1b, wr_ref[...], preferred_element_type=F32) + br_ref[...]
    lane = lax.broadcasted_iota(jnp.int32, (tm, LANES), 1)
    work = jnp.where(lane < N_EXPERTS, logits, -jnp.inf)
    idxs, vals = [], []
    onehot = jnp.zeros((tm, LANES), F32)
    for _ in range(TOP_K):
        best = jnp.max(work, axis=1, keepdims=True)
        where_best = jnp.min(jnp.where(work == best, lane, LANES), axis=1, keepdims=True)
        hit = lane == where_best
        onehot = jnp.where(hit, 1.0, onehot)
        work = jnp.where(hit, -jnp.inf, work)
        idxs.append(where_best)
        vals.append(best)
    exps = [jnp.exp(v - vals[0]) for v in vals]
    denom = exps[0] + exps[1] + exps[2] + exps[3]
    row = lax.broadcasted_iota(jnp.int32, (tm, tm), 0)
    col = lax.broadcasted_iota(jnp.int32, (tm, tm), 1)
    strict = jnp.where(row > col, 1.0, 0.0).astype(BF16)
    before = jnp.dot(strict, onehot.astype(BF16), preferred_element_type=F32) + carry_ref[...]
    idx_out = jnp.zeros((tm, LANES), jnp.int32)
    rank_out = jnp.zeros((tm, LANES), jnp.int32)
    gate_out = jnp.zeros((tm, LANES), F32)
    for r in range(TOP_K):
        rank_r = jnp.sum(jnp.where(lane == idxs[r], before, 0.0), axis=1, keepdims=True)
        idx_out = jnp.where(lane == r, idxs[r], idx_out)
        rank_out = jnp.where(lane == r, rank_r.astype(jnp.int32), rank_out)
        gate_out = jnp.where(lane == r, exps[r] / denom, gate_out)
    idx_ref[...] = idx_out
    rank_ref[...] = rank_out
    gate_ref[...] = gate_out
    total = carry_ref[...] + jnp.sum(onehot, axis=0, keepdims=True)
    carry_ref[...] = total
    cnt_ref[...] = jnp.broadcast_to(total, cnt_ref.shape)


def _post_attn(x2d, att, memo, wa, wm, g, b, wr, br):
    t_tokens = x2d.shape[0]
    tm = PROJ_TM
    row_spec = lambda width: pl.BlockSpec((tm, width), lambda i: (i, 0))
    full = lambda a: pl.BlockSpec(a.shape, lambda i: (0, 0))
    return pl.pallas_call(
        _post_attn_kernel,
        grid=(t_tokens // tm,),
        in_specs=[row_spec(D_MODEL), row_spec(att.shape[1]), row_spec(MEM_WIDTH),
                  full(wa), full(wm), full(g), full(b), full(wr), full(br)],
        out_specs=[row_spec(D_MODEL), row_spec(D_MODEL // 2), row_spec(LANES), row_spec(LANES),
                   row_spec(LANES), pl.BlockSpec((8, LANES), lambda i: (0, 0))],
        out_shape=[jax.ShapeDtypeStruct((t_tokens, D_MODEL), F32),
                   jax.ShapeDtypeStruct((t_tokens, D_MODEL // 2), jnp.uint32),
                   jax.ShapeDtypeStruct((t_tokens, LANES), jnp.int32),
                   jax.ShapeDtypeStruct((t_tokens, LANES), jnp.int32),
                   jax.ShapeDtypeStruct((t_tokens, LANES), F32),
                   jax.ShapeDtypeStruct((8, LANES), F32)],
        scratch_shapes=[pltpu.VMEM((1, LANES), F32)],
        compiler_params=_cparams(1),
        name="outproj_ln_router",
    )(x2d, att, memo, wa, wm, g, b, wr, br)


def _expert_kernel(be_ref, nu_ref, x_ref, wgu_ref, bgu_ref, wdn_ref, bdn_ref, o_ref,
                   wgu_bf, wdn_bf, *, n_chunks):
    blk = pl.program_id(0)

    @pl.when((blk == 0) | (be_ref[blk] != be_ref[jnp.maximum(blk - 1, 0)]))
    def _():
        wgu_bf[...] = wgu_ref[0, 0].astype(BF16)
        wdn_bf[...] = wdn_ref[0, 0].astype(BF16)

    @pl.when(blk < nu_ref[0])
    def _():
        words = x_ref[...]
        x = jnp.concatenate(
            [pltpu.bitcast(words << 16, F32), pltpu.bitcast(words & jnp.uint32(0xFFFF0000), F32)],
            axis=1).astype(BF16)
        cw = D_EXPERT // n_chunks
        acc = jnp.zeros(o_ref.shape, F32)
        for c in range(n_chunks):
            gs = slice(c * cw, (c + 1) * cw)
            us = slice(D_EXPERT + c * cw, D_EXPERT + (c + 1) * cw)
            g = jnp.dot(x, wgu_bf[:, gs], preferred_element_type=F32) + bgu_ref[0, 0, :, gs]
            u = jnp.dot(x, wgu_bf[:, us], preferred_element_type=F32) + bgu_ref[0, 0, :, us]
            g = jnp.minimum(g, SWIGLU_LIMIT)
            u = jnp.clip(u, -SWIGLU_LIMIT, SWIGLU_LIMIT)
            hmid = (u + 1.0) * (g * jax.nn.sigmoid(SWIGLU_ALPHA * g))
            acc = acc + jnp.dot(hmid.astype(BF16), wdn_bf[gs, :], preferred_element_type=F32)
        o_ref[...] = acc + bdn_ref[0, 0]

    @pl.when(blk >= nu_ref[0])
    def _():
        o_ref[...] = jnp.zeros_like(o_ref)


def _experts(block_expert, n_used, xb, layer, wgu, bgu, wdn, bdn):
    p_rows = xb.shape[0]
    tm = MOE_TM
    grid_spec = pltpu.PrefetchScalarGridSpec(
        num_scalar_prefetch=2,
        grid=(p_rows // tm,),
        in_specs=[pl.BlockSpec((tm, D_MODEL // 2), lambda i, be, nu: (i, 0)),
                  pl.BlockSpec((1, 1, D_MODEL, 2 * D_EXPERT), lambda i, be, nu: (layer, be[i], 0, 0)),
                  pl.BlockSpec((1, 1, 1, 2 * D_EXPERT), lambda i, be, nu: (layer, be[i], 0, 0)),
                  pl.BlockSpec((1, 1, D_EXPERT, D_MODEL), lambda i, be, nu: (layer, be[i], 0, 0)),
                  pl.BlockSpec((1, 1, 1, D_MODEL), lambda i, be, nu: (layer, be[i], 0, 0))],
        out_specs=pl.BlockSpec((tm, D_MODEL), lambda i, be, nu: (i, 0)),
        scratch_shapes=[pltpu.VMEM((D_MODEL, 2 * D_EXPERT), BF16),
                        pltpu.VMEM((D_EXPERT, D_MODEL), BF16)],
    )
    return pl.pallas_call(
        functools.partial(_expert_kernel, n_chunks=4),
        grid_spec=grid_spec,
        out_shape=jax.ShapeDtypeStruct((p_rows, D_MODEL), F32),
        compiler_params=pltpu.CompilerParams(dimension_semantics=("arbitrary",),
                                             vmem_limit_bytes=EXPERT_VMEM_LIMIT),
        name="experts",
    )(block_expert, n_used, xb, wgu, bgu, wdn, bdn)


def _sc_gather_rows(table, idx):
    n_idx = idx.shape[0]
    width = table.shape[1]
    per_worker = n_idx // SC_WORKERS
    n_chunks = per_worker // SC_CHUNK
    mesh = plsc.VectorSubcoreMesh(core_axis_name="c", subcore_axis_name="s",
                                  num_cores=SC_CORES, num_subcores=SC_SUBCORES)

    def body(table_hbm, idx_hbm, out_hbm, idx_v, rows_v, sem):
        wid = lax.axis_index("s") * SC_CORES + lax.axis_index("c")
        base = wid * per_worker
        pltpu.sync_copy(idx_hbm.at[pl.ds(base, per_worker)], idx_v)

        @pl.loop(0, n_chunks)
        def _(c):
            off = pl.multiple_of(c * SC_CHUNK, SC_CHUNK)
            pltpu.async_copy(table_hbm.at[idx_v.at[pl.ds(off, SC_CHUNK)]], rows_v, sem).wait()
            pltpu.sync_copy(rows_v, out_hbm.at[pl.ds(base + off, SC_CHUNK)])

    return pl.kernel(
        body,
        out_type=jax.ShapeDtypeStruct((n_idx, width), table.dtype),
        mesh=mesh,
        scratch_types=[pltpu.VMEM((per_worker,), jnp.int32),
                       pltpu.VMEM((SC_CHUNK, width), table.dtype),
                       pltpu.SemaphoreType.DMA],
        name="sc_gather_rows",
    )(table, idx)


def _combine_kernel(x1_ref, yg_ref, gate_ref, g_ref, b_ref, o_ref):
    gates = gate_ref[...]
    ffn = yg_ref[0] * gates[:, 0:1]
    for r in range(1, TOP_K):
        ffn = ffn + yg_ref[r] * gates[:, r:r + 1]
    o_ref[...] = _layer_norm(DEEPNORM_ALPHA * x1_ref[...] + ffn, g_ref[...], b_ref[...])


def _combine(x1, yg, gates, g, b):
    t_tokens = x1.shape[0]
    tm = 256
    return pl.pallas_call(
        _combine_kernel,
        grid=(t_tokens // tm,),
        in_specs=[pl.BlockSpec((tm, D_MODEL), lambda i: (i, 0)),
                  pl.BlockSpec((TOP_K, tm, D_MODEL), lambda i: (0, i, 0)),
                  pl.BlockSpec((tm, LANES), lambda i: (i, 0)),
                  pl.BlockSpec((1, D_MODEL), lambda i: (0, 0)),
                  pl.BlockSpec((1, D_MODEL), lambda i: (0, 0))],
        out_specs=pl.BlockSpec((tm, D_MODEL), lambda i: (i, 0)),
        out_shape=jax.ShapeDtypeStruct((t_tokens, D_MODEL), F32),
        compiler_params=_cparams(1),
        name="combine_ln",
    )(x1, yg, gates, g, b)


def _moe(x1, x1p, idx, rank, gates, counts, layer, wgu, bgu, wdn, bdn, g, b):
    t_tokens = x1.shape[0]
    tm = MOE_TM
    counts = counts[0, :N_EXPERTS].astype(jnp.int32)
    padded = ((counts + tm - 1) // tm) * tm
    pend = jnp.cumsum(padded)
    pstart = pend - padded
    idx4 = idx[:, :TOP_K]
    dest = (pstart[idx4] + rank[:, :TOP_K]).reshape(-1)
    n_blocks = (t_tokens * TOP_K) // tm + N_EXPERTS
    p_rows = n_blocks * tm
    block_start = jnp.arange(n_blocks, dtype=jnp.int32) * tm
    block_expert = jnp.minimum(
        jnp.sum((pend[None, :] <= block_start[:, None]).astype(jnp.int32), axis=1), N_EXPERTS - 1)
    n_used = (pend[-1:] // tm).astype(jnp.int32)
    tok = jnp.arange(t_tokens * TOP_K, dtype=jnp.int32) // TOP_K
    slot_tok = jnp.zeros((p_rows,), jnp.int32).at[dest].set(tok)
    xb = _sc_gather_rows(x1p, slot_tok)
    y = _experts(block_expert, n_used, xb, layer, wgu, bgu, wdn, bdn)
    dest_by_choice = dest.reshape(t_tokens, TOP_K).T.reshape(-1)
    yg = _sc_gather_rows(y, dest_by_choice).reshape(TOP_K, t_tokens, D_MODEL)
    return _combine(x1, yg, gates, g, b)


def _row(v, width=None):
    v = v.astype(F32).reshape(1, -1)
    if width is not None and v.shape[1] < width:
        v = jnp.pad(v, ((0, 0), (0, width - v.shape[1])))
    return v


def _pad_cols(w, width):
    return jnp.pad(w, ((0, 0), (0, width - w.shape[1])))


def kernel(x, mem, positions, a_w_in, a_b_f, a_w_out, b_w_in, b_g_q, b_w_uq, b_w_out,
           kv_w_dkv, kv_g, kv_w_ukv, mem_w_kv, ln_g, ln_b,
           moe_w_r, moe_b_r, moe_w_gu, moe_b_gu, moe_w_dn, moe_b_dn):
    nb, seq, d = x.shape
    t_tokens = nb * seq
    n_a = a_w_in.shape[0]
    x2d = x.reshape(t_tokens, d)
    mem2d = mem.reshape(nb * N_MEM, d)
    pos2d = positions.reshape(t_tokens, 1)
    half = QK_ROPE // 2
    inv_freq = ROPE_THETA ** (-jnp.arange(half, dtype=F32) * 2.0 / QK_ROPE)
    invf = jnp.zeros((1, LANES), F32)
    invf = invf.at[0, QK_NOPE:QK_NOPE + half].set(inv_freq)
    invf = invf.at[0, QK_NOPE + half:QK_NOPE + QK_ROPE].set(inv_freq)

    shared_kv = None
    for l in range(DEPTH):
        mkv = _matmul(mem2d, mem_w_kv[l].astype(BF16), BF16, tm=nb * N_MEM)
        mk = mkv[:, :MEM_WIDTH].reshape(nb, N_MEM, MEM_WIDTH)
        mv = mkv[:, MEM_WIDTH:].reshape(nb, N_MEM, MEM_WIDTH)
        if l < n_a:
            w_in = a_w_in[l]
            w = jnp.concatenate([w_in[:, :3 * FOX_WIDTH],
                                 _pad_cols(w_in[:, 3 * FOX_WIDTH:3 * FOX_WIDTH + FOX_HEADS], LANES),
                                 w_in[:, 3 * FOX_WIDTH + FOX_HEADS:]], axis=1).astype(BF16)
            qt, k, vt, mq, cum, cumt, cfirst, clast, knorm = _fox_proj(
                x2d, w, _row(a_b_f[l], LANES), seq)
            stats = [s.reshape(-1, LANES) for s in (cfirst, clast, knorm)]
            att = _causal_attention(qt, k.reshape(nb, seq, -1), vt,
                                    [cum.reshape(nb, seq, LANES), cumt] + stats, slab=False)
            w_out = a_w_out[l]
        else:
            bl = l - n_a
            w = jnp.concatenate([b_w_in[bl], kv_w_dkv[:, :KV_LORA],
                                 jnp.zeros((d, QK_NOPE), F32), kv_w_dkv[:, KV_LORA:],
                                 jnp.zeros((d, LANES - QK_NOPE - QK_ROPE), F32)], axis=1).astype(BF16)
            wuq = b_w_uq[bl].reshape(Q_LORA, MLA_HEADS, QK_NOPE + QK_ROPE)
            wuq = jnp.pad(wuq, ((0, 0), (0, 0), (0, MLA_SLAB - QK_NOPE - QK_ROPE)))
            wuq = wuq.reshape(Q_LORA, MLA_HEADS * MLA_SLAB).astype(BF16)
            wukv = kv_w_ukv.reshape(KV_LORA, MLA_HEADS, QK_NOPE + V_DIM)
            wk = jnp.pad(wukv[:, :, :QK_NOPE], ((0, 0), (0, 0), (0, MLA_SLAB - QK_NOPE)))
            wk = wk.reshape(KV_LORA, MLA_HEADS * MLA_SLAB).astype(BF16)
            wv = wukv[:, :, QK_NOPE:].reshape(KV_LORA, MLA_V_WIDTH).astype(BF16)
            qt, mq, k_new, vt_new = _mla_proj(x2d, pos2d, invf, w, _row(b_g_q[bl]), wuq,
                                              _row(kv_g), wk, wv, seq)
            if shared_kv is None:
                shared_kv = (k_new.reshape(nb, seq, -1), vt_new)
            att = _causal_attention(qt, shared_kv[0], shared_kv[1], None, slab=True)
            w_out = b_w_out[bl]
        memo = _memory_attention(mq.reshape(nb, seq, MEM_WIDTH), mk, mv)
        n_att = w_out.shape[0] - MEM_WIDTH
        x1, x1p, idx, rank, gates, counts = _post_attn(
            x2d, att.reshape(t_tokens, -1), memo.reshape(t_tokens, MEM_WIDTH),
            w_out[:n_att].astype(BF16), w_out[n_att:].astype(BF16),
            _row(ln_g[l, 0]), _row(ln_b[l, 0]),
            _pad_cols(moe_w_r[l], LANES).astype(BF16), _row(moe_b_r[l], LANES))
        x2d = _moe(x1, x1p, idx, rank, gates, counts, l,
                   moe_w_gu, moe_b_gu.reshape(DEPTH, N_EXPERTS, 1, -1),
                   moe_w_dn, moe_b_dn.reshape(DEPTH, N_EXPERTS, 1, -1),
                   _row(ln_g[l, 1]), _row(ln_b[l, 1]))
    return x2d.reshape(nb, seq, d)
```

```python
import functools
import math

import jax
import jax.numpy as jnp
from jax import lax
from jax.experimental import pallas as pl
from jax.experimental.pallas import tpu as pltpu
from jax.experimental.pallas import tpu_sc as plsc

F32 = jnp.float32
BF16 = jnp.bfloat16

D_MODEL = 1024
DEPTH = 2
N_MEM = 256
HEAD_DIM = 64
FOX_HEADS = 12
MEM_HEADS = 4
MLA_HEADS = 12
Q_LORA = 384
KV_LORA = 256
QK_NOPE = 64
QK_ROPE = 32
V_DIM = 64
ROPE_THETA = 10000.0
N_EXPERTS = 32
TOP_K = 4
D_EXPERT = D_MODEL
SWIGLU_LIMIT = 7.0
SWIGLU_ALPHA = 1.702
LN_EPS = 1e-5
RMS_EPS = 1e-6
NEG_INF = -1e30
DEEPNORM_ALPHA = (2 * DEPTH) ** 0.25
FOX_WIDTH = FOX_HEADS * HEAD_DIM
MEM_WIDTH = MEM_HEADS * HEAD_DIM
MLA_V_WIDTH = MLA_HEADS * V_DIM

LANES = 128
LOG2E = math.log2(math.e)
VMEM_LIMIT = 48 * 1024 * 1024
EXPERT_VMEM_LIMIT = 56 * 1024 * 1024

PROJ_TM = 512
ATT_TQ = 512
MOE_TM = 512
MLA_SLAB = LANES
VT_ROWS = HEAD_DIM + 16
SC_CORES = 2
SC_SUBCORES = 16
SC_WORKERS = SC_CORES * SC_SUBCORES
SC_CHUNK = 32
SC_LANES = 16
SC_SCAN = 8192


def _cparams(n_axes):
    return pltpu.CompilerParams(dimension_semantics=("arbitrary",) * n_axes,
                                vmem_limit_bytes=VMEM_LIMIT)


def _split3(x):
    hi = x.astype(BF16)
    r1 = x - hi.astype(F32)
    mid = r1.astype(BF16)
    lo = (r1 - mid.astype(F32)).astype(BF16)
    return hi, mid, lo


def _layer_norm(y, g, b):
    mu = jnp.mean(y, axis=-1, keepdims=True)
    yc = y - mu
    var = jnp.mean(yc * yc, axis=-1, keepdims=True)
    return yc * lax.rsqrt(var + LN_EPS) * g + b


def _rms_norm(y, g):
    return y * lax.rsqrt(jnp.mean(y * y, axis=-1, keepdims=True) + RMS_EPS) * g


def _mm_kernel(a_ref, b_ref, o_ref):
    o_ref[...] = jnp.dot(a_ref[...].astype(BF16), b_ref[...],
                         preferred_element_type=F32).astype(o_ref.dtype)


def _matmul(a, b, out_dtype, tm):
    m, k = a.shape
    n = b.shape[1]
    return pl.pallas_call(
        _mm_kernel,
        grid=(m // tm,),
        in_specs=[pl.BlockSpec((tm, k), lambda i: (i, 0)),
                  pl.BlockSpec((k, n), lambda i: (0, 0))],
        out_specs=pl.BlockSpec((tm, n), lambda i: (i, 0)),
        out_shape=jax.ShapeDtypeStruct((m, n), out_dtype),
        compiler_params=_cparams(1),
        name="matmul",
    )(a, b)


def _store_transposed(dst_ref, val):
    for s in range(val.shape[1] // LANES):
        sl = slice(s * LANES, (s + 1) * LANES)
        dst_ref[0, sl, :] = val[:, sl].T.astype(dst_ref.dtype)


def _store_values_transposed(vt_ref, val):
    tm = val.shape[0]
    ones = jnp.ones((VT_ROWS - HEAD_DIM, tm), vt_ref.dtype)
    for s in range(val.shape[1] // LANES):
        pair_t = val[:, s * LANES:(s + 1) * LANES].T.astype(vt_ref.dtype)
        for h in range(2):
            r0 = (2 * s + h) * VT_ROWS
            vt_ref[0, r0:r0 + HEAD_DIM, :] = pair_t[h * HEAD_DIM:(h + 1) * HEAD_DIM]
            vt_ref[0, r0 + HEAD_DIM:r0 + VT_ROWS, :] = ones


def _fox_proj_kernel(x_ref, w_ref, bf_ref, seg_ref, qt_ref, k_ref, vt_ref, mq_ref, cum_ref,
                     cumt_ref, cfirst_ref, clast_ref, knorm_ref, carry_ref, *, tiles_per_batch):
    t = pl.program_id(0)

    @pl.when(t % tiles_per_batch == 0)
    def _():
        carry_ref[...] = jnp.zeros_like(carry_ref)

    tm = x_ref.shape[0]
    proj = jnp.dot(x_ref[...].astype(BF16), w_ref[...], preferred_element_type=F32)
    qscale = HEAD_DIM ** -0.5 * LOG2E
    _store_transposed(qt_ref, proj[:, :FOX_WIDTH] * qscale)
    kb = proj[:, FOX_WIDTH:2 * FOX_WIDTH].astype(BF16)
    k_ref[...] = kb
    _store_values_transposed(vt_ref, proj[:, 2 * FOX_WIDTH:3 * FOX_WIDTH])
    kf = kb.astype(F32)
    ksq = jnp.dot((kf * kf).astype(BF16), seg_ref[...], preferred_element_type=F32)
    knorm_ref[0] = jnp.sqrt(jnp.max(ksq, axis=0, keepdims=True))
    f = proj[:, 3 * FOX_WIDTH:3 * FOX_WIDTH + LANES] + bf_ref[...]
    mq_ref[...] = (proj[:, 3 * FOX_WIDTH + LANES:] * qscale).astype(BF16)
    log_f = jnp.minimum(f, 0.0) - jnp.log1p(jnp.exp(-jnp.abs(f)))
    row = lax.broadcasted_iota(jnp.int32, (tm, tm), 0)
    col = lax.broadcasted_iota(jnp.int32, (tm, tm), 1)
    tri = jnp.where(row >= col, 1.0, 0.0).astype(BF16)
    hi, mid, lo = _split3(log_f)
    cum = (jnp.dot(tri, hi, preferred_element_type=F32)
           + jnp.dot(tri, mid, preferred_element_type=F32)
           + jnp.dot(tri, lo, preferred_element_type=F32)) + carry_ref[...]
    carry_ref[...] = cum[tm - 1:tm, :]
    cum2 = cum * LOG2E
    cum_ref[...] = cum2
    cumt_ref[0] = cum2.T[:16, :]
    cfirst_ref[0] = cum2[0:1, :]
    clast_ref[0] = cum2[tm - 1:tm, :]


def _fox_proj(x2d, w, bf, seq):
    t_tokens = x2d.shape[0]
    tm = PROJ_TM
    nb = t_tokens // seq
    n = w.shape[1]
    tiles_per_batch = seq // tm
    n_tiles = t_tokens // tm
    seg = (jnp.arange(FOX_WIDTH)[:, None] // HEAD_DIM == jnp.arange(LANES)[None, :]).astype(BF16)
    row_spec = lambda width: pl.BlockSpec((tm, width), lambda i: (i, 0))
    t_spec = lambda rows: pl.BlockSpec(
        (1, rows, tm), lambda i: (i // tiles_per_batch, 0, i % tiles_per_batch))
    stat_spec = pl.BlockSpec((1, 1, LANES), lambda i: (i, 0, 0))
    stat_shape = jax.ShapeDtypeStruct((n_tiles, 1, LANES), F32)
    return pl.pallas_call(
        functools.partial(_fox_proj_kernel, tiles_per_batch=tiles_per_batch),
        grid=(n_tiles,),
        in_specs=[row_spec(D_MODEL),
                  pl.BlockSpec((D_MODEL, n), lambda i: (0, 0)),
                  pl.BlockSpec((1, LANES), lambda i: (0, 0)),
                  pl.BlockSpec((FOX_WIDTH, LANES), lambda i: (0, 0))],
        out_specs=[t_spec(FOX_WIDTH), row_spec(FOX_WIDTH), t_spec(FOX_HEADS * VT_ROWS),
                   row_spec(MEM_WIDTH), row_spec(LANES), t_spec(16),
                   stat_spec, stat_spec, stat_spec],
        out_shape=[jax.ShapeDtypeStruct((nb, FOX_WIDTH, seq), BF16),
                   jax.ShapeDtypeStruct((t_tokens, FOX_WIDTH), BF16),
                   jax.ShapeDtypeStruct((nb, FOX_HEADS * VT_ROWS, seq), BF16),
                   jax.ShapeDtypeStruct((t_tokens, MEM_WIDTH), BF16),
                   jax.ShapeDtypeStruct((t_tokens, LANES), F32),
                   jax.ShapeDtypeStruct((nb, 16, seq), F32),
                   stat_shape, stat_shape, stat_shape],
        scratch_shapes=[pltpu.VMEM((1, LANES), F32)],
        compiler_params=_cparams(1),
        name="fox_proj",
    )(x2d, w, bf, seg)


def _rope_tables(pos_ref, invf_ref):
    ang = pos_ref[...].astype(F32) * invf_ref[...]
    cos = jnp.cos(ang)
    sin = jnp.sin(ang)
    lane = lax.broadcasted_iota(jnp.int32, ang.shape, 1)
    half = QK_ROPE // 2
    in_x1 = (lane >= QK_NOPE) & (lane < QK_NOPE + half)
    in_x2 = (lane >= QK_NOPE + half) & (lane < QK_NOPE + QK_ROPE)
    c_tab = jnp.where(lane < QK_NOPE, 1.0, jnp.where(in_x1 | in_x2, cos, 0.0))
    s_up = jnp.where(in_x2, sin, 0.0)
    s_dn = jnp.where(in_x1, -sin, 0.0)
    return c_tab, s_up, s_dn


def _rope_slab(slab, tables):
    c_tab, s_up, s_dn = tables
    half = QK_ROPE // 2
    up = pltpu.roll(slab, half, 1)
    dn = pltpu.roll(slab, LANES - half, 1)
    return slab * c_tab + up * s_up + dn * s_dn


def _mla_proj_kernel(x_ref, pos_ref, invf_ref, w_ref, gq_ref, wuq_ref, gkv_ref, wk_ref, wv_ref,
                     qt_ref, mq_ref, k_ref, vt_ref):
    proj = jnp.dot(x_ref[...].astype(BF16), w_ref[...], preferred_element_type=F32)
    tables = _rope_tables(pos_ref, invf_ref)
    c_q = _rms_norm(proj[:, :Q_LORA], gq_ref[...])
    mq_ref[...] = (proj[:, Q_LORA:Q_LORA + MEM_WIDTH] * (HEAD_DIM ** -0.5 * LOG2E)).astype(BF16)
    kv_off = Q_LORA + MEM_WIDTH
    c_kv = _rms_norm(proj[:, kv_off:kv_off + KV_LORA], gkv_ref[...])
    kr = _rope_slab(proj[:, kv_off + KV_LORA:], tables)
    q = jnp.dot(c_q.astype(BF16), wuq_ref[...], preferred_element_type=F32)
    kn = jnp.dot(c_kv.astype(BF16), wk_ref[...], preferred_element_type=F32)
    qscale = (QK_NOPE + QK_ROPE) ** -0.5 * LOG2E
    for h in range(MLA_HEADS):
        sl = slice(h * MLA_SLAB, (h + 1) * MLA_SLAB)
        qt_ref[0, sl, :] = (_rope_slab(q[:, sl], tables) * qscale).T.astype(BF16)
        k_ref[:, sl] = (kn[:, sl] + kr).astype(BF16)
    _store_values_transposed(vt_ref, jnp.dot(c_kv.astype(BF16), wv_ref[...],
                                             preferred_element_type=F32))


def _mla_proj(x2d, pos2d, invf, w, gq, wuq, gkv, wk, wv, seq):
    t_tokens = x2d.shape[0]
    tm = PROJ_TM
    nb = t_tokens // seq
    tiles_per_batch = seq // tm
    row_spec = lambda width: pl.BlockSpec((tm, width), lambda i: (i, 0))
    t_spec = lambda rows: pl.BlockSpec(
        (1, rows, tm), lambda i: (i // tiles_per_batch, 0, i % tiles_per_batch))
    full = lambda a: pl.BlockSpec(a.shape, lambda i: (0, 0))
    slabs = MLA_HEADS * MLA_SLAB
    return pl.pallas_call(
        _mla_proj_kernel,
        grid=(t_tokens // tm,),
        in_specs=[row_spec(D_MODEL), row_spec(1), full(invf), full(w), full(gq), full(wuq),
                  full(gkv), full(wk), full(wv)],
        out_specs=[t_spec(slabs), row_spec(MEM_WIDTH), row_spec(slabs),
                   t_spec(MLA_HEADS * VT_ROWS)],
        out_shape=[jax.ShapeDtypeStruct((nb, slabs, seq), BF16),
                   jax.ShapeDtypeStruct((t_tokens, MEM_WIDTH), BF16),
                   jax.ShapeDtypeStruct((t_tokens, slabs), BF16),
                   jax.ShapeDtypeStruct((nb, MLA_HEADS * VT_ROWS, seq), BF16)],
        compiler_params=_cparams(1),
        name="mla_proj",
    )(x2d, pos2d, invf, w, gq, wuq, gkv, wk, wv)


def _softmax_chunk(h, s, vh, m_ref, l_ref, acc_ref):
    m_prev = m_ref[h]
    m_new = jnp.maximum(m_prev, jnp.max(s, axis=1, keepdims=True))
    alpha = jnp.exp2(m_prev - m_new)
    p = jnp.exp2(s - m_new)
    l_ref[h] = alpha * l_ref[h] + jnp.sum(p, axis=1, keepdims=True)
    acc_ref[h] = acc_ref[h] * alpha + jnp.dot(p.astype(BF16), vh, preferred_element_type=F32)
    m_ref[h] = m_new


def _init_state(m_ref, l_ref, acc_ref):
    m_ref[...] = jnp.full(m_ref.shape, NEG_INF, F32)
    l_ref[...] = jnp.zeros(l_ref.shape, F32)
    acc_ref[...] = jnp.zeros(acc_ref.shape, F32)


def _head_operands(h, q, kc, vc, q_slab):
    lane = lax.broadcasted_iota(jnp.int32, (1, LANES), 1)
    mine = (lane < HEAD_DIM) if h == 0 else (lane >= HEAD_DIM)
    vh = jnp.where(mine, vc, jnp.zeros_like(vc))
    if q_slab:
        sl = slice(h * MLA_SLAB, (h + 1) * MLA_SLAB)
        return q[:, sl], kc[:, sl], vh
    return q, jnp.where(mine, kc, jnp.zeros_like(kc)), vh


def _finish(o_ref, l_ref, acc_ref):
    lane = lax.broadcasted_iota(jnp.int32, (1, LANES), 1)
    o0 = acc_ref[0] / l_ref[0]
    o1 = acc_ref[1] / l_ref[1]
    o_ref[0] = jnp.where(lane < HEAD_DIM, o0, o1).astype(o_ref.dtype)


SKIP_LOG2 = -160.0
NORM_SLACK = 1.02


def _causal_attn_kernel(*refs, tq, n_tiles, fox, slab):
    if fox:
        (cf_ref, cl_ref, kn_ref, qt_ref, k_ref, vt_ref, cq_ref, ck_ref,
         o_ref, m_ref, acc_ref, s_ref) = refs
    else:
        qt_ref, k_ref, vt_ref, o_ref, m_ref, acc_ref, s_ref = refs
    b = pl.program_id(0)
    pair = pl.program_id(1)
    i = pl.program_id(2)
    m_ref[...] = jnp.full(m_ref.shape, NEG_INF, F32)
    acc_ref[...] = jnp.zeros(acc_ref.shape, F32)
    qt = qt_ref[0]
    if slab:
        qth = [qt[:MLA_SLAB], qt[MLA_SLAB:]]
    else:
        rowi = lax.broadcasted_iota(jnp.int32, qt.shape, 0)
        zero = jnp.zeros_like(qt)
        qth = [jnp.where(rowi < HEAD_DIM, qt, zero), jnp.where(rowi >= HEAD_DIM, qt, zero)]
    if fox:
        lane = lax.broadcasted_iota(jnp.int32, (1, LANES), 1)
        cq = [cq_ref[0, pl.ds(2 * pair + h, 1), :] for h in range(2)]

    def logits_to(slot, j):
        off = pl.multiple_of(j * tq, tq)
        kc = k_ref[0, pl.ds(off, tq), :]
        for h in range(2):
            kh = kc[:, h * MLA_SLAB:(h + 1) * MLA_SLAB] if slab else kc
            s = jnp.dot(kh, qth[h], preferred_element_type=F32)
            if fox:
                ck_blk = ck_ref[0, pl.ds(off, tq), :]
                ck = jnp.sum(jnp.where(lane == 2 * pair + h, ck_blk, 0.0), axis=1, keepdims=True)
                s = s + cq[h] - ck
            s_ref[slot, h] = s

    def softmax_pv(slot, j, causal_mask):
        off = pl.multiple_of(j * tq, tq)
        for h in range(2):
            s = s_ref[slot, h]
            if causal_mask is not None:
                s = jnp.where(causal_mask, s, NEG_INF)
            m_prev = m_ref[h]
            m_new = jnp.maximum(m_prev, jnp.max(s, axis=0, keepdims=True))
            alpha = jnp.exp2(m_prev - m_new)
            p = jnp.exp2(s - m_new).astype(BF16)
            vth = vt_ref[0, h * VT_ROWS:(h + 1) * VT_ROWS, pl.ds(off, tq)]
            acc_ref[h] = acc_ref[h] * alpha + jnp.dot(vth, p, preferred_element_type=F32)
            m_ref[h] = m_new

    if fox:
        base = b * n_tiles
        qn = []
        for h in range(2):
            qf = qth[h].astype(F32)
            qn.append(jnp.sqrt(jnp.max(jnp.sum(qf * qf, axis=0, keepdims=True))) * NORM_SLACK)

        def first_live(j, j0):
            ub = [qn[h] * (kn_ref[base + j, 2 * pair + h] + kn_ref[base + i, 2 * pair + h])
                  + cf_ref[base + i, 2 * pair + h] - cl_ref[base + j, 2 * pair + h] for h in range(2)]
            return jnp.where(jnp.maximum(ub[0], ub[1]) > SKIP_LOG2, jnp.minimum(j0, j), j0)

        j0 = lax.fori_loop(0, i, first_live, i)
    else:
        j0 = 0
    n_before = i - j0
    odd = (n_before % 2) == 1

    @pl.when(odd)
    def _():
        logits_to(1, j0)

    @pl.when(jnp.logical_not(odd))
    def _():
        logits_to(0, j0)

    @pl.when(odd)
    def _():
        logits_to(0, j0 + 1)
        softmax_pv(1, j0, None)

    def body(t, carry):
        j = j0 + (n_before % 2) + 2 * t
        logits_to(1, j + 1)
        softmax_pv(0, j, None)
        logits_to(0, j + 2)
        softmax_pv(1, j + 1, None)
        return carry

    lax.fori_loop(0, n_before // 2, body, 0)
    key = lax.broadcasted_iota(jnp.int32, (tq, tq), 0)
    qry = lax.broadcasted_iota(jnp.int32, (tq, tq), 1)
    softmax_pv(0, i, key <= qry)
    out_t = jnp.concatenate(
        [acc_ref[h, :HEAD_DIM] * (1.0 / acc_ref[h, HEAD_DIM:HEAD_DIM + 1]) for h in range(2)],
        axis=0)
    o_ref[0] = out_t.T.astype(o_ref.dtype)


def _causal_attention(qt, k, vt, fox_args, *, slab):
    nb, seq, _ = k.shape
    tq = ATT_TQ
    n_tiles = seq // tq
    n_pairs = FOX_HEADS // 2
    rows = 2 * MLA_SLAB if slab else LANES
    fox = fox_args is not None
    in_specs = [pl.BlockSpec((1, rows, tq), lambda b, p, i: (b, p, i)),
                pl.BlockSpec((1, seq, rows), lambda b, p, i: (b, 0, p)),
                pl.BlockSpec((1, 2 * VT_ROWS, seq), lambda b, p, i: (b, p, 0))]
    args = [qt, k, vt]
    if fox:
        cum, cumt, cfirst, clast, knorm = fox_args
        smem = pl.BlockSpec(memory_space=pltpu.SMEM)
        in_specs = [smem, smem, smem] + in_specs + [
            pl.BlockSpec((1, 16, tq), lambda b, p, i: (b, 0, i)),
            pl.BlockSpec((1, seq, LANES), lambda b, p, i: (b, 0, 0))]
        args = [cfirst, clast, knorm] + args + [cumt, cum]
    return pl.pallas_call(
        functools.partial(_causal_attn_kernel, tq=tq, n_tiles=n_tiles, fox=fox, slab=slab),
        grid=(nb, n_pairs, n_tiles),
        in_specs=in_specs,
        out_specs=pl.BlockSpec((1, tq, LANES), lambda b, p, i: (b, i, p)),
        out_shape=jax.ShapeDtypeStruct((nb, seq, n_pairs * LANES), BF16),
        scratch_shapes=[pltpu.VMEM((2, 1, tq), F32), pltpu.VMEM((2, VT_ROWS, tq), F32),
                        pltpu.VMEM((2, 2, tq, tq), F32)],
        compiler_params=_cparams(3),
        name="fox_attention" if fox else "mla_attention",
    )(*args)


def _mem_attn_kernel(q_ref, k_ref, v_ref, o_ref, m_ref, l_ref, acc_ref):
    _init_state(m_ref, l_ref, acc_ref)
    q = q_ref[0]
    kc = k_ref[0]
    vc = v_ref[0]
    for h in range(2):
        qh, kh, vh = _head_operands(h, q, kc, vc, False)
        s = lax.dot_general(qh, kh, (((1,), (1,)), ((), ())), preferred_element_type=F32)
        _softmax_chunk(h, s, vh, m_ref, l_ref, acc_ref)
    _finish(o_ref, l_ref, acc_ref)


def _memory_attention(mq, mk, mv):
    nb, seq, _ = mq.shape
    tq = ATT_TQ
    n_pairs = MEM_HEADS // 2
    return pl.pallas_call(
        _mem_attn_kernel,
        grid=(nb, n_pairs, seq // tq),
        in_specs=[pl.BlockSpec((1, tq, LANES), lambda b, p, i: (b, i, p)),
                  pl.BlockSpec((1, N_MEM, LANES), lambda b, p, i: (b, 0, p)),
                  pl.BlockSpec((1, N_MEM, LANES), lambda b, p, i: (b, 0, p))],
        out_specs=pl.BlockSpec((1, tq, LANES), lambda b, p, i: (b, i, p)),
        out_shape=jax.ShapeDtypeStruct((nb, seq, MEM_WIDTH), BF16),
        scratch_shapes=[pltpu.VMEM((2, tq, 1), F32), pltpu.VMEM((2, tq, 1), F32),
                        pltpu.VMEM((2, tq, LANES), F32)],
        compiler_params=_cparams(3),
        name="memory_attention",
    )(mq, mk, mv)


def _post_attn_kernel(x_ref, att_ref, memo_ref, wa_ref, wm_ref, g_ref, b_ref, wr_ref, br_ref,
                      x1_ref, x1p_ref, idx_ref, rank_ref, gate_ref, cnt_ref, carry_ref):
    t = pl.program_id(0)

    @pl.when(t == 0)
    def _():
        carry_ref[...] = jnp.zeros_like(carry_ref)

    tm = x_ref.shape[0]
    mix = (jnp.dot(att_ref[...], wa_ref[...], preferred_element_type=F32)
           + jnp.dot(memo_ref[...], wm_ref[...], preferred_element_type=F32))
    x1 = _layer_norm(DEEPNORM_ALPHA * x_ref[...] + mix, g_ref[...], b_ref[...])
    x1_ref[...] = x1
    x1b = x1.astype(BF16)
    bits = pltpu.bitcast(x1b.astype(F32), jnp.uint32)
    half = D_MODEL // 2
    x1p_ref[...] = (bits[:, :half] >> 16) | bits[:, half:]
    logits = jnp.dot(x1b, wr_ref[...], preferred_element_type=F32) + br_ref[...]
    lane = lax.broadcasted_iota(jnp.int32, (tm, LANES), 1)
    work = jnp.where(lane < N_EXPERTS, logits, -jnp.inf)
    idxs, vals = [], []
    onehot = jnp.zeros((tm, LANES), F32)
    for _ in range(TOP_K):
        best = jnp.max(work, axis=1, keepdims=True)
        where_best = jnp.min(jnp.where(work == best, lane, LANES), axis=1, keepdims=True)
        hit = lane == where_best
        onehot = jnp.where(hit, 1.0, onehot)
        work = jnp.where(hit, -jnp.inf, work)
        idxs.append(where_best)
        vals.append(best)
    exps = [jnp.exp(v - vals[0]) for v in vals]
    denom = exps[0] + exps[1] + exps[2] + exps[3]
    row = lax.broadcasted_iota(jnp.int32, (tm, tm), 0)
    col = lax.broadcasted_iota(jnp.int32, (tm, tm), 1)
    strict = jnp.where(row > col, 1.0, 0.0).astype(BF16)
    before = jnp.dot(strict, onehot.astype(BF16), preferred_element_type=F32) + carry_ref[...]
    idx_out = jnp.zeros((tm, LANES), jnp.int32)
    rank_out = jnp.zeros((tm, LANES), jnp.int32)
    gate_out = jnp.zeros((tm, LANES), F32)
    for r in range(TOP_K):
        rank_r = jnp.sum(jnp.where(lane == idxs[r], before, 0.0), axis=1, keepdims=True)
        idx_out = jnp.where(lane == r, idxs[r], idx_out)
        rank_out = jnp.where(lane == r, rank_r.astype(jnp.int32), rank_out)
        gate_out = jnp.where(lane == r, exps[r] / denom, gate_out)
    idx_ref[...] = idx_out
    rank_ref[...] = rank_out
    gate_ref[...] = gate_out
    total = carry_ref[...] + jnp.sum(onehot, axis=0, keepdims=True)
    carry_ref[...] = total
    cnt_ref[...] = jnp.broadcast_to(total, cnt_ref.shape)


def _post_attn(x2d, att, memo, wa, wm, g, b, wr, br):
    t_tokens = x2d.shape[0]
    tm = PROJ_TM
    row_spec = lambda width: pl.BlockSpec((tm, width), lambda i: (i, 0))
    full = lambda a: pl.BlockSpec(a.shape, lambda i: (0, 0))
    return pl.pallas_call(
        _post_attn_kernel,
        grid=(t_tokens // tm,),
        in_specs=[row_spec(D_MODEL), row_spec(att.shape[1]), row_spec(MEM_WIDTH),
                  full(wa), full(wm), full(g), full(b), full(wr), full(br)],
        out_specs=[row_spec(D_MODEL), row_spec(D_MODEL // 2), row_spec(LANES), row_spec(LANES),
                   row_spec(LANES), pl.BlockSpec((8, LANES), lambda i: (0, 0))],
        out_shape=[jax.ShapeDtypeStruct((t_tokens, D_MODEL), F32),
                   jax.ShapeDtypeStruct((t_tokens, D_MODEL // 2), jnp.uint32),
                   jax.ShapeDtypeStruct((t_tokens, LANES), jnp.int32),
                   jax.ShapeDtypeStruct((t_tokens, LANES), jnp.int32),
                   jax.ShapeDtypeStruct((t_tokens, LANES), F32),
                   jax.ShapeDtypeStruct((8, LANES), F32)],
        scratch_shapes=[pltpu.VMEM((1, LANES), F32)],
        compiler_params=_cparams(1),
        name="outproj_ln_router",
    )(x2d, att, memo, wa, wm, g, b, wr, br)


def _expert_kernel(be_ref, nu_ref, x_ref, wgu_ref, bgu_ref, wdn_ref, bdn_ref, o_ref,
                   wgu_bf, wdn_bf, *, n_chunks):
    blk = pl.program_id(0)

    @pl.when((blk == 0) | (be_ref[blk] != be_ref[jnp.maximum(blk - 1, 0)]))
    def _():
        wgu_bf[...] = wgu_ref[0, 0].astype(BF16)
        wdn_bf[...] = wdn_ref[0, 0].astype(BF16)

    @pl.when(blk < nu_ref[0])
    def _():
        words = x_ref[...]
        x = jnp.concatenate(
            [pltpu.bitcast(words << 16, F32), pltpu.bitcast(words & jnp.uint32(0xFFFF0000), F32)],
            axis=1).astype(BF16)
        cw = D_EXPERT // n_chunks
        acc = jnp.zeros(o_ref.shape, F32)
        for c in range(n_chunks):
            gs = slice(c * cw, (c + 1) * cw)
            us = slice(D_EXPERT + c * cw, D_EXPERT + (c + 1) * cw)
            g = jnp.dot(x, wgu_bf[:, gs], preferred_element_type=F32) + bgu_ref[0, 0, :, gs]
            u = jnp.dot(x, wgu_bf[:, us], preferred_element_type=F32) + bgu_ref[0, 0, :, us]
            g = jnp.minimum(g, SWIGLU_LIMIT)
            u = jnp.clip(u, -SWIGLU_LIMIT, SWIGLU_LIMIT)
            hmid = (u + 1.0) * (g * jax.nn.sigmoid(SWIGLU_ALPHA * g))
            acc = acc + jnp.dot(hmid.astype(BF16), wdn_bf[gs, :], preferred_element_type=F32)
        o_ref[...] = acc + bdn_ref[0, 0]

    @pl.when(blk >= nu_ref[0])
    def _():
        o_ref[...] = jnp.zeros_like(o_ref)


def _experts(block_expert, n_used, xb, layer, wgu, bgu, wdn, bdn):
    p_rows = xb.shape[0]
    tm = MOE_TM
    grid_spec = pltpu.PrefetchScalarGridSpec(
        num_scalar_prefetch=2,
        grid=(p_rows // tm,),
        in_specs=[pl.BlockSpec((tm, D_MODEL // 2), lambda i, be, nu: (i, 0)),
                  pl.BlockSpec((1, 1, D_MODEL, 2 * D_EXPERT), lambda i, be, nu: (layer, be[i], 0, 0)),
                  pl.BlockSpec((1, 1, 1, 2 * D_EXPERT), lambda i, be, nu: (layer, be[i], 0, 0)),
                  pl.BlockSpec((1, 1, D_EXPERT, D_MODEL), lambda i, be, nu: (layer, be[i], 0, 0)),
                  pl.BlockSpec((1, 1, 1, D_MODEL), lambda i, be, nu: (layer, be[i], 0, 0))],
        out_specs=pl.BlockSpec((tm, D_MODEL), lambda i, be, nu: (i, 0)),
        scratch_shapes=[pltpu.VMEM((D_MODEL, 2 * D_EXPERT), BF16),
                        pltpu.VMEM((D_EXPERT, D_MODEL), BF16)],
    )
    return pl.pallas_call(
        functools.partial(_expert_kernel, n_chunks=4),
        grid_spec=grid_spec,
        out_shape=jax.ShapeDtypeStruct((p_rows, D_MODEL), F32),
        compiler_params=pltpu.CompilerParams(dimension_semantics=("arbitrary",),
                                             vmem_limit_bytes=EXPERT_VMEM_LIMIT),
        name="experts",
    )(block_expert, n_used, xb, wgu, bgu, wdn, bdn)


def _sc_gather_rows(table, idx):
    n_idx = idx.shape[0]
    width = table.shape[1]
    per_worker = n_idx // SC_WORKERS
    n_chunks = per_worker // SC_CHUNK
    mesh = plsc.VectorSubcoreMesh(core_axis_name="c", subcore_axis_name="s",
                                  num_cores=SC_CORES, num_subcores=SC_SUBCORES)

    def body(table_hbm, idx_hbm, out_hbm, idx_v, rows_v, sem):
        wid = lax.axis_index("s") * SC_CORES + lax.axis_index("c")
        base = wid * per_worker
        pltpu.sync_copy(idx_hbm.at[pl.ds(base, per_worker)], idx_v)

        @pl.loop(0, n_chunks)
        def _(c):
            off = pl.multiple_of(c * SC_CHUNK, SC_CHUNK)
            pltpu.async_copy(table_hbm.at[idx_v.at[pl.ds(off, SC_CHUNK)]], rows_v, sem).wait()
            pltpu.sync_copy(rows_v, out_hbm.at[pl.ds(base + off, SC_CHUNK)])

    return pl.kernel(
        body,
        out_type=jax.ShapeDtypeStruct((n_idx, width), table.dtype),
        mesh=mesh,
        scratch_types=[pltpu.VMEM((per_worker,), jnp.int32),
                       pltpu.VMEM((SC_CHUNK, width), table.dtype),
                       pltpu.SemaphoreType.DMA],
        name="sc_gather_rows",
    )(table, idx)


def _sc_dispatch_rows(table, dest, p_rows):
    n_tok, width = table.shape
    n_pairs = dest.shape[0]
    per_worker = p_rows // SC_WORKERS
    n_chunks = per_worker // SC_CHUNK
    n_scan = n_pairs // SC_SCAN
    mesh = plsc.VectorSubcoreMesh(core_axis_name="c", subcore_axis_name="s",
                                  num_cores=SC_CORES, num_subcores=SC_SUBCORES)

    def body(table_hbm, dest_hbm, out_hbm, tok_v, dest_v, rows_v, sem):
        wid = lax.axis_index("s") * SC_CORES + lax.axis_index("c")
        base = wid * per_worker
        lane = lax.iota(jnp.int32, SC_LANES)

        @pl.loop(0, per_worker // SC_LANES)
        def _(i):
            off = pl.multiple_of(i * SC_LANES, SC_LANES)
            tok_v[pl.ds(off, SC_LANES)] = lax.rem(base + off + lane, n_tok)

        @pl.loop(0, n_scan)
        def _(g):
            goff = pl.multiple_of(g * SC_SCAN, SC_SCAN)
            pltpu.sync_copy(dest_hbm.at[pl.ds(goff, SC_SCAN)], dest_v)

            @pl.loop(0, SC_SCAN // SC_LANES)
            def _(i):
                off = pl.multiple_of(i * SC_LANES, SC_LANES)
                local = dest_v[pl.ds(off, SC_LANES)] - base
                mine = (local >= 0) & (local < per_worker)
                pair = goff + off + lane
                plsc.store_scatter(tok_v, [jnp.where(mine, local, 0)], pair // TOP_K, mask=mine)

        @pl.loop(0, n_chunks)
        def _(c):
            off = pl.multiple_of(c * SC_CHUNK, SC_CHUNK)
            pltpu.async_copy(table_hbm.at[tok_v.at[pl.ds(off, SC_CHUNK)]], rows_v, sem).wait()
            pltpu.sync_copy(rows_v, out_hbm.at[pl.ds(base + off, SC_CHUNK)])

    return pl.kernel(
        body,
        out_type=jax.ShapeDtypeStruct((p_rows, width), table.dtype),
        mesh=mesh,
        scratch_types=[pltpu.VMEM((per_worker,), jnp.int32),
                       pltpu.VMEM((SC_SCAN,), jnp.int32),
                       pltpu.VMEM((SC_CHUNK, width), table.dtype),
                       pltpu.SemaphoreType.DMA],
        compiler_params=pltpu.CompilerParams(needs_layout_passes=False),
        name="sc_dispatch_rows",
    )(table, dest)


def _combine_kernel(x1_ref, yg_ref, gate_ref, g_ref, b_ref, o_ref):
    gates = gate_ref[...]
    ffn = yg_ref[0] * gates[:, 0:1]
    for r in range(1, TOP_K):
        ffn = ffn + yg_ref[r] * gates[:, r:r + 1]
    o_ref[...] = _layer_norm(DEEPNORM_ALPHA * x1_ref[...] + ffn, g_ref[...], b_ref[...])


def _combine(x1, yg, gates, g, b):
    t_tokens = x1.shape[0]
    tm = 256
    return pl.pallas_call(
        _combine_kernel,
        grid=(t_tokens // tm,),
        in_specs=[pl.BlockSpec((tm, D_MODEL), lambda i: (i, 0)),
                  pl.BlockSpec((TOP_K, tm, D_MODEL), lambda i: (0, i, 0)),
                  pl.BlockSpec((tm, LANES), lambda i: (i, 0)),
                  pl.BlockSpec((1, D_MODEL), lambda i: (0, 0)),
                  pl.BlockSpec((1, D_MODEL), lambda i: (0, 0))],
        out_specs=pl.BlockSpec((tm, D_MODEL), lambda i: (i, 0)),
        out_shape=jax.ShapeDtypeStruct((t_tokens, D_MODEL), F32),
        compiler_params=_cparams(1),
        name="combine_ln",
    )(x1, yg, gates, g, b)


def _moe(x1, x1p, idx, rank, gates, counts, layer, wgu, bgu, wdn, bdn, g, b):
    t_tokens = x1.shape[0]
    tm = MOE_TM
    counts = counts[0, :N_EXPERTS].astype(jnp.int32)
    padded = ((counts + tm - 1) // tm) * tm
    pend = jnp.cumsum(padded)
    pstart = pend - padded
    idx4 = idx[:, :TOP_K]
    dest = (pstart[idx4] + rank[:, :TOP_K]).reshape(-1)
    n_blocks = (t_tokens * TOP_K) // tm + N_EXPERTS
    p_rows = n_blocks * tm
    block_start = jnp.arange(n_blocks, dtype=jnp.int32) * tm
    block_expert = jnp.minimum(
        jnp.sum((pend[None, :] <= block_start[:, None]).astype(jnp.int32), axis=1), N_EXPERTS - 1)
    n_used = (pend[-1:] // tm).astype(jnp.int32)
    xb = _sc_dispatch_rows(x1p, dest, p_rows)
    y = _experts(block_expert, n_used, xb, layer, wgu, bgu, wdn, bdn)
    dest_by_choice = dest.reshape(t_tokens, TOP_K).T.reshape(-1)
    yg = _sc_gather_rows(y, dest_by_choice).reshape(TOP_K, t_tokens, D_MODEL)
    return _combine(x1, yg, gates, g, b)


def _row(v, width=None):
    v = v.astype(F32).reshape(1, -1)
    if width is not None and v.shape[1] < width:
        v = jnp.pad(v, ((0, 0), (0, width - v.shape[1])))
    return v


def _pad_cols(w, width):
    return jnp.pad(w, ((0, 0), (0, width - w.shape[1])))


def kernel(x, mem, positions, a_w_in, a_b_f, a_w_out, b_w_in, b_g_q, b_w_uq, b_w_out,
           kv_w_dkv, kv_g, kv_w_ukv, mem_w_kv, ln_g, ln_b,
           moe_w_r, moe_b_r, moe_w_gu, moe_b_gu, moe_w_dn, moe_b_dn):
    nb, seq, d = x.shape
    t_tokens = nb * seq
    n_a = a_w_in.shape[0]
    x2d = x.reshape(t_tokens, d)
    mem2d = mem.reshape(nb * N_MEM, d)
    pos2d = positions.reshape(t_tokens, 1)
    half = QK_ROPE // 2
    inv_freq = ROPE_THETA ** (-jnp.arange(half, dtype=F32) * 2.0 / QK_ROPE)
    invf = jnp.zeros((1, LANES), F32)
    invf = invf.at[0, QK_NOPE:QK_NOPE + half].set(inv_freq)
    invf = invf.at[0, QK_NOPE + half:QK_NOPE + QK_ROPE].set(inv_freq)

    shared_kv = None
    for l in range(DEPTH):
        mkv = _matmul(mem2d, mem_w_kv[l].astype(BF16), BF16, tm=nb * N_MEM)
        mk = mkv[:, :MEM_WIDTH].reshape(nb, N_MEM, MEM_WIDTH)
        mv = mkv[:, MEM_WIDTH:].reshape(nb, N_MEM, MEM_WIDTH)
        if l < n_a:
            w_in = a_w_in[l]
            w = jnp.concatenate([w_in[:, :3 * FOX_WIDTH],
                                 _pad_cols(w_in[:, 3 * FOX_WIDTH:3 * FOX_WIDTH + FOX_HEADS], LANES),
                                 w_in[:, 3 * FOX_WIDTH + FOX_HEADS:]], axis=1).astype(BF16)
            qt, k, vt, mq, cum, cumt, cfirst, clast, knorm = _fox_proj(
                x2d, w, _row(a_b_f[l], LANES), seq)
            stats = [s.reshape(-1, LANES) for s in (cfirst, clast, knorm)]
            att = _causal_attention(qt, k.reshape(nb, seq, -1), vt,
                                    [cum.reshape(nb, seq, LANES), cumt] + stats, slab=False)
            w_out = a_w_out[l]
        else:
            bl = l - n_a
            w = jnp.concatenate([b_w_in[bl], kv_w_dkv[:, :KV_LORA],
                                 jnp.zeros((d, QK_NOPE), F32), kv_w_dkv[:, KV_LORA:],
                                 jnp.zeros((d, LANES - QK_NOPE - QK_ROPE), F32)], axis=1).astype(BF16)
            wuq = b_w_uq[bl].reshape(Q_LORA, MLA_HEADS, QK_NOPE + QK_ROPE)
            wuq = jnp.pad(wuq, ((0, 0), (0, 0), (0, MLA_SLAB - QK_NOPE - QK_ROPE)))
            wuq = wuq.reshape(Q_LORA, MLA_HEADS * MLA_SLAB).astype(BF16)
            wukv = kv_w_ukv.reshape(KV_LORA, MLA_HEADS, QK_NOPE + V_DIM)
            wk = jnp.pad(wukv[:, :, :QK_NOPE], ((0, 0), (0, 0), (0, MLA_SLAB - QK_NOPE)))
            wk = wk.reshape(KV_LORA, MLA_HEADS * MLA_SLAB).astype(BF16)
            wv = wukv[:, :, QK_NOPE:].reshape(KV_LORA, MLA_V_WIDTH).astype(BF16)
            qt, mq, k_new, vt_new = _mla_proj(x2d, pos2d, invf, w, _row(b_g_q[bl]), wuq,
                                              _row(kv_g), wk, wv, seq)
            if shared_kv is None:
                shared_kv = (k_new.reshape(nb, seq, -1), vt_new)
            att = _causal_attention(qt, shared_kv[0], shared_kv[1], None, slab=True)
            w_out = b_w_out[bl]
        memo = _memory_attention(mq.reshape(nb, seq, MEM_WIDTH), mk, mv)
        n_att = w_out.shape[0] - MEM_WIDTH
        x1, x1p, idx, rank, gates, counts = _post_attn(
            x2d, att.reshape(t_tokens, -1), memo.reshape(t_tokens, MEM_WIDTH),
            w_out[:n_att].astype(BF16), w_out[n_att:].astype(BF16),
            _row(ln_g[l, 0]), _row(ln_b[l, 0]),
            _pad_cols(moe_w_r[l], LANES).astype(BF16), _row(moe_b_r[l], LANES))
        x2d = _moe(x1, x1p, idx, rank, gates, counts, l,
                   moe_w_gu, moe_b_gu.reshape(DEPTH, N_EXPERTS, 1, -1),
                   moe_w_dn, moe_b_dn.reshape(DEPTH, N_EXPERTS, 1, -1),
                   _row(ln_g[l, 1]), _row(ln_b[l, 1]))
    return x2d.reshape(nb, seq, d)
```

```python
import functools
import math

import jax
import jax.numpy as jnp
from jax import lax
from jax.experimental import pallas as pl
from jax.experimental.pallas import tpu as pltpu
from jax.experimental.pallas import tpu_sc as plsc

F32 = jnp.float32
BF16 = jnp.bfloat16

D_MODEL = 1024
DEPTH = 2
N_MEM = 256
HEAD_DIM = 64
FOX_HEADS = 12
MEM_HEADS = 4
MLA_HEADS = 12
Q_LORA = 384
KV_LORA = 256
QK_NOPE = 64
QK_ROPE = 32
V_DIM = 64
ROPE_THETA = 10000.0
N_EXPERTS = 32
TOP_K = 4
D_EXPERT = D_MODEL
SWIGLU_LIMIT = 7.0
SWIGLU_ALPHA = 1.702
LN_EPS = 1e-5
RMS_EPS = 1e-6
NEG_INF = -1e30
DEEPNORM_ALPHA = (2 * DEPTH) ** 0.25
FOX_WIDTH = FOX_HEADS * HEAD_DIM
MEM_WIDTH = MEM_HEADS * HEAD_DIM
MLA_V_WIDTH = MLA_HEADS * V_DIM

LANES = 128
LOG2E = math.log2(math.e)
VMEM_LIMIT = 48 * 1024 * 1024
EXPERT_VMEM_LIMIT = 56 * 1024 * 1024

PROJ_TM = 512
ATT_TQ = 512
MOE_TM = 512
MLA_SLAB = LANES
VT_ROWS = HEAD_DIM + 16
SC_CORES = 2
SC_SUBCORES = 16
SC_WORKERS = SC_CORES * SC_SUBCORES
SC_CHUNK = 32
SC_LANES = 16
SC_SCAN = 8192


def _cparams(n_axes):
    return pltpu.CompilerParams(dimension_semantics=("arbitrary",) * n_axes,
                                vmem_limit_bytes=VMEM_LIMIT)


def _split3(x):
    hi = x.astype(BF16)
    r1 = x - hi.astype(F32)
    mid = r1.astype(BF16)
    lo = (r1 - mid.astype(F32)).astype(BF16)
    return hi, mid, lo


def _layer_norm(y, g, b):
    mu = jnp.mean(y, axis=-1, keepdims=True)
    yc = y - mu
    var = jnp.mean(yc * yc, axis=-1, keepdims=True)
    return yc * lax.rsqrt(var + LN_EPS) * g + b


def _rms_norm(y, g):
    return y * lax.rsqrt(jnp.mean(y * y, axis=-1, keepdims=True) + RMS_EPS) * g


def _store_transposed(dst_ref, val):
    for s in range(val.shape[1] // LANES):
        sl = slice(s * LANES, (s + 1) * LANES)
        dst_ref[0, sl, :] = val[:, sl].T.astype(dst_ref.dtype)


def _store_values_transposed(vt_ref, val):
    tm = val.shape[0]
    ones = jnp.ones((VT_ROWS - HEAD_DIM, tm), vt_ref.dtype)
    for s in range(val.shape[1] // LANES):
        pair_t = val[:, s * LANES:(s + 1) * LANES].T.astype(vt_ref.dtype)
        for h in range(2):
            r0 = (2 * s + h) * VT_ROWS
            vt_ref[0, r0:r0 + HEAD_DIM, :] = pair_t[h * HEAD_DIM:(h + 1) * HEAD_DIM]
            vt_ref[0, r0 + HEAD_DIM:r0 + VT_ROWS, :] = ones


def _fox_proj_kernel(x_ref, w_ref, bf_ref, seg_ref, qt_ref, k_ref, vt_ref, mqt_ref, cum_ref,
                     cumt_ref, cfirst_ref, clast_ref, knorm_ref, carry_ref, *, tiles_per_batch):
    t = pl.program_id(0)

    @pl.when(t % tiles_per_batch == 0)
    def _():
        carry_ref[...] = jnp.zeros_like(carry_ref)

    tm = x_ref.shape[0]
    proj = jnp.dot(x_ref[...].astype(BF16), w_ref[...], preferred_element_type=F32)
    qscale = HEAD_DIM ** -0.5 * LOG2E
    _store_transposed(qt_ref, proj[:, :FOX_WIDTH] * qscale)
    kb = proj[:, FOX_WIDTH:2 * FOX_WIDTH].astype(BF16)
    k_ref[...] = kb
    _store_values_transposed(vt_ref, proj[:, 2 * FOX_WIDTH:3 * FOX_WIDTH])
    kf = kb.astype(F32)
    ksq = jnp.dot((kf * kf).astype(BF16), seg_ref[...], preferred_element_type=F32)
    knorm_ref[0] = jnp.sqrt(jnp.max(ksq, axis=0, keepdims=True))
    f = proj[:, 3 * FOX_WIDTH:3 * FOX_WIDTH + LANES] + bf_ref[...]
    _store_transposed(mqt_ref, proj[:, 3 * FOX_WIDTH + LANES:] * qscale)
    log_f = jnp.minimum(f, 0.0) - jnp.log1p(jnp.exp(-jnp.abs(f)))
    row = lax.broadcasted_iota(jnp.int32, (tm, tm), 0)
    col = lax.broadcasted_iota(jnp.int32, (tm, tm), 1)
    tri = jnp.where(row >= col, 1.0, 0.0).astype(BF16)
    hi, mid, lo = _split3(log_f)
    cum = (jnp.dot(tri, hi, preferred_element_type=F32)
           + jnp.dot(tri, mid, preferred_element_type=F32)
           + jnp.dot(tri, lo, preferred_element_type=F32)) + carry_ref[...]
    carry_ref[...] = cum[tm - 1:tm, :]
    cum2 = cum * LOG2E
    cum_ref[...] = cum2
    cumt_ref[0] = cum2.T[:16, :]
    cfirst_ref[0] = cum2[0:1, :]
    clast_ref[0] = cum2[tm - 1:tm, :]


def _fox_proj(x2d, w, bf, seq):
    t_tokens = x2d.shape[0]
    tm = PROJ_TM
    nb = t_tokens // seq
    n = w.shape[1]
    tiles_per_batch = seq // tm
    n_tiles = t_tokens // tm
    seg = (jnp.arange(FOX_WIDTH)[:, None] // HEAD_DIM == jnp.arange(LANES)[None, :]).astype(BF16)
    row_spec = lambda width: pl.BlockSpec((tm, width), lambda i: (i, 0))
    t_spec = lambda rows: pl.BlockSpec(
        (1, rows, tm), lambda i: (i // tiles_per_batch, 0, i % tiles_per_batch))
    stat_spec = pl.BlockSpec((1, 1, LANES), lambda i: (i, 0, 0))
    stat_shape = jax.ShapeDtypeStruct((n_tiles, 1, LANES), F32)
    return pl.pallas_call(
        functools.partial(_fox_proj_kernel, tiles_per_batch=tiles_per_batch),
        grid=(n_tiles,),
        in_specs=[row_spec(D_MODEL),
                  pl.BlockSpec((D_MODEL, n), lambda i: (0, 0)),
                  pl.BlockSpec((1, LANES), lambda i: (0, 0)),
                  pl.BlockSpec((FOX_WIDTH, LANES), lambda i: (0, 0))],
        out_specs=[t_spec(FOX_WIDTH), row_spec(FOX_WIDTH), t_spec(FOX_HEADS * VT_ROWS),
                   t_spec(MEM_WIDTH), row_spec(LANES), t_spec(16),
                   stat_spec, stat_spec, stat_spec],
        out_shape=[jax.ShapeDtypeStruct((nb, FOX_WIDTH, seq), BF16),
                   jax.ShapeDtypeStruct((t_tokens, FOX_WIDTH), BF16),
                   jax.ShapeDtypeStruct((nb, FOX_HEADS * VT_ROWS, seq), BF16),
                   jax.ShapeDtypeStruct((nb, MEM_WIDTH, seq), BF16),
                   jax.ShapeDtypeStruct((t_tokens, LANES), F32),
                   jax.ShapeDtypeStruct((nb, 16, seq), F32),
                   stat_shape, stat_shape, stat_shape],
        scratch_shapes=[pltpu.VMEM((1, LANES), F32)],
        compiler_params=_cparams(1),
        name="fox_proj",
    )(x2d, w, bf, seg)


def _rope_tables(pos_ref, invf_ref):
    ang = pos_ref[...].astype(F32) * invf_ref[...]
    cos = jnp.cos(ang)
    sin = jnp.sin(ang)
    lane = lax.broadcasted_iota(jnp.int32, ang.shape, 1)
    half = QK_ROPE // 2
    in_x1 = (lane >= QK_NOPE) & (lane < QK_NOPE + half)
    in_x2 = (lane >= QK_NOPE + half) & (lane < QK_NOPE + QK_ROPE)
    c_tab = jnp.where(lane < QK_NOPE, 1.0, jnp.where(in_x1 | in_x2, cos, 0.0))
    s_up = jnp.where(in_x2, sin, 0.0)
    s_dn = jnp.where(in_x1, -sin, 0.0)
    return c_tab, s_up, s_dn


def _rope_slab(slab, tables):
    c_tab, s_up, s_dn = tables
    half = QK_ROPE // 2
    up = pltpu.roll(slab, half, 1)
    dn = pltpu.roll(slab, LANES - half, 1)
    return slab * c_tab + up * s_up + dn * s_dn


def _mla_proj_kernel(x_ref, pos_ref, invf_ref, w_ref, gq_ref, wuq_ref, gkv_ref, wk_ref, wv_ref,
                     qt_ref, mqt_ref, k_ref, vt_ref):
    proj = jnp.dot(x_ref[...].astype(BF16), w_ref[...], preferred_element_type=F32)
    tables = _rope_tables(pos_ref, invf_ref)
    c_q = _rms_norm(proj[:, :Q_LORA], gq_ref[...])
    _store_transposed(mqt_ref, proj[:, Q_LORA:Q_LORA + MEM_WIDTH] * (HEAD_DIM ** -0.5 * LOG2E))
    kv_off = Q_LORA + MEM_WIDTH
    c_kv = _rms_norm(proj[:, kv_off:kv_off + KV_LORA], gkv_ref[...])
    kr = _rope_slab(proj[:, kv_off + KV_LORA:], tables)
    q = jnp.dot(c_q.astype(BF16), wuq_ref[...], preferred_element_type=F32)
    kn = jnp.dot(c_kv.astype(BF16), wk_ref[...], preferred_element_type=F32)
    qscale = (QK_NOPE + QK_ROPE) ** -0.5 * LOG2E
    for h in range(MLA_HEADS):
        sl = slice(h * MLA_SLAB, (h + 1) * MLA_SLAB)
        qt_ref[0, sl, :] = (_rope_slab(q[:, sl], tables) * qscale).T.astype(BF16)
        k_ref[:, sl] = (kn[:, sl] + kr).astype(BF16)
    _store_values_transposed(vt_ref, jnp.dot(c_kv.astype(BF16), wv_ref[...],
                                             preferred_element_type=F32))


def _mla_proj(x2d, pos2d, invf, w, gq, wuq, gkv, wk, wv, seq):
    t_tokens = x2d.shape[0]
    tm = PROJ_TM
    nb = t_tokens // seq
    tiles_per_batch = seq // tm
    row_spec = lambda width: pl.BlockSpec((tm, width), lambda i: (i, 0))
    t_spec = lambda rows: pl.BlockSpec(
        (1, rows, tm), lambda i: (i // tiles_per_batch, 0, i % tiles_per_batch))
    full = lambda a: pl.BlockSpec(a.shape, lambda i: (0, 0))
    slabs = MLA_HEADS * MLA_SLAB
    return pl.pallas_call(
        _mla_proj_kernel,
        grid=(t_tokens // tm,),
        in_specs=[row_spec(D_MODEL), row_spec(1), full(invf), full(w), full(gq), full(wuq),
                  full(gkv), full(wk), full(wv)],
        out_specs=[t_spec(slabs), t_spec(MEM_WIDTH), row_spec(slabs),
                   t_spec(MLA_HEADS * VT_ROWS)],
        out_shape=[jax.ShapeDtypeStruct((nb, slabs, seq), BF16),
                   jax.ShapeDtypeStruct((nb, MEM_WIDTH, seq), BF16),
                   jax.ShapeDtypeStruct((t_tokens, slabs), BF16),
                   jax.ShapeDtypeStruct((nb, MLA_HEADS * VT_ROWS, seq), BF16)],
        compiler_params=_cparams(1),
        name="mla_proj",
    )(x2d, pos2d, invf, w, gq, wuq, gkv, wk, wv)


SKIP_LOG2 = -160.0
NORM_SLACK = 1.02


def _causal_attn_kernel(*refs, tq, n_tiles, fox, slab, long_body):
    if fox:
        (cf_ref, cl_ref, kn_ref, qt_ref, k_ref, vt_ref, cq_ref, ck_ref,
         o_ref, m_ref, acc_ref, s_ref) = refs
    else:
        qt_ref, k_ref, vt_ref, o_ref, m_ref, acc_ref, s_ref = refs
    b = pl.program_id(0)
    pair = pl.program_id(1)
    i = pl.program_id(2)
    m_ref[...] = jnp.full(m_ref.shape, NEG_INF, F32)
    acc_ref[...] = jnp.zeros(acc_ref.shape, F32)
    qt = qt_ref[0]
    if slab:
        qth = [qt[:MLA_SLAB], qt[MLA_SLAB:]]
    else:
        rowi = lax.broadcasted_iota(jnp.int32, qt.shape, 0)
        zero = jnp.zeros_like(qt)
        qth = [jnp.where(rowi < HEAD_DIM, qt, zero), jnp.where(rowi >= HEAD_DIM, qt, zero)]
    if fox:
        lane = lax.broadcasted_iota(jnp.int32, (1, LANES), 1)
        cq = [cq_ref[0, pl.ds(2 * pair + h, 1), :] for h in range(2)]

    def logits_to(slot, j):
        off = pl.multiple_of(j * tq, tq)
        kc = k_ref[0, pl.ds(off, tq), :]
        for h in range(2):
            kh = kc[:, h * MLA_SLAB:(h + 1) * MLA_SLAB] if slab else kc
            s = jnp.dot(kh, qth[h], preferred_element_type=F32)
            if fox:
                ck_blk = ck_ref[0, pl.ds(off, tq), :]
                ck = jnp.sum(jnp.where(lane == 2 * pair + h, ck_blk, 0.0), axis=1, keepdims=True)
                s = s + cq[h] - ck
            s_ref[slot, h] = s

    def softmax_pv(slot, j, causal_mask):
        off = pl.multiple_of(j * tq, tq)
        for h in range(2):
            s = s_ref[slot, h]
            if causal_mask is not None:
                s = jnp.where(causal_mask, s, NEG_INF)
            m_prev = m_ref[h]
            m_new = jnp.maximum(m_prev, jnp.max(s, axis=0, keepdims=True))
            alpha = jnp.exp2(m_prev - m_new)
            p = jnp.exp2(s - m_new).astype(BF16)
            vth = vt_ref[0, h * VT_ROWS:(h + 1) * VT_ROWS, pl.ds(off, tq)]
            acc_ref[h] = acc_ref[h] * alpha + jnp.dot(vth, p, preferred_element_type=F32)
            m_ref[h] = m_new

    if fox:
        base = b * n_tiles
        qn = []
        for h in range(2):
            qf = qth[h].astype(F32)
            qn.append(jnp.sqrt(jnp.max(jnp.sum(qf * qf, axis=0, keepdims=True))) * NORM_SLACK)

        def first_live(j, j0):
            ub = [qn[h] * (kn_ref[base + j, 2 * pair + h] + kn_ref[base + i, 2 * pair + h])
                  + cf_ref[base + i, 2 * pair + h] - cl_ref[base + j, 2 * pair + h] for h in range(2)]
            return jnp.where(jnp.maximum(ub[0], ub[1]) > SKIP_LOG2, jnp.minimum(j0, j), j0)

        j0 = lax.fori_loop(0, i, first_live, i)
    else:
        j0 = 0
    n_before = i - j0
    odd = (n_before % 2) == 1

    @pl.when(odd)
    def _():
        logits_to(1, j0)

    @pl.when(jnp.logical_not(odd))
    def _():
        logits_to(0, j0)

    @pl.when(odd)
    def _():
        logits_to(0, j0 + 1)
        softmax_pv(1, j0, None)

    def two_steps(j):
        logits_to(1, j + 1)
        softmax_pv(0, j, None)
        logits_to(0, j + 2)
        softmax_pv(1, j + 1, None)

    j_even = j0 + (n_before % 2)
    n_two = n_before // 2
    if long_body:
        @pl.when((n_two % 2) == 1)
        def _():
            two_steps(j_even)

        j_quad = j_even + 2 * (n_two % 2)

        def body(t, carry):
            two_steps(j_quad + 4 * t)
            two_steps(j_quad + 4 * t + 2)
            return carry

        lax.fori_loop(0, n_two // 2, body, 0)
    else:
        def body(t, carry):
            two_steps(j_even + 2 * t)
            return carry

        lax.fori_loop(0, n_two, body, 0)
    key = lax.broadcasted_iota(jnp.int32, (tq, tq), 0)
    qry = lax.broadcasted_iota(jnp.int32, (tq, tq), 1)
    softmax_pv(0, i, key <= qry)
    out_t = jnp.concatenate(
        [acc_ref[h, :HEAD_DIM] * (1.0 / acc_ref[h, HEAD_DIM:HEAD_DIM + 1]) for h in range(2)],
        axis=0)
    o_ref[0] = out_t.T.astype(o_ref.dtype)


def _causal_attention(qt, k, vt, fox_args, *, slab):
    nb, seq, _ = k.shape
    tq = ATT_TQ
    n_tiles = seq // tq
    n_pairs = FOX_HEADS // 2
    rows = 2 * MLA_SLAB if slab else LANES
    fox = fox_args is not None
    in_specs = [pl.BlockSpec((1, rows, tq), lambda b, p, i: (b, p, i)),
                pl.BlockSpec((1, seq, rows), lambda b, p, i: (b, 0, p)),
                pl.BlockSpec((1, 2 * VT_ROWS, seq), lambda b, p, i: (b, p, 0))]
    args = [qt, k, vt]
    if fox:
        cum, cumt, cfirst, clast, knorm = fox_args
        smem = pl.BlockSpec(memory_space=pltpu.SMEM)
        in_specs = [smem, smem, smem] + in_specs + [
            pl.BlockSpec((1, 16, tq), lambda b, p, i: (b, 0, i)),
            pl.BlockSpec((1, seq, LANES), lambda b, p, i: (b, 0, 0))]
        args = [cfirst, clast, knorm] + args + [cumt, cum]
    return pl.pallas_call(
        functools.partial(_causal_attn_kernel, tq=tq, n_tiles=n_tiles, fox=fox, slab=slab,
                          long_body=not fox),
        grid=(nb, n_pairs, n_tiles),
        in_specs=in_specs,
        out_specs=pl.BlockSpec((1, tq, LANES), lambda b, p, i: (b, i, p)),
        out_shape=jax.ShapeDtypeStruct((nb, seq, n_pairs * LANES), BF16),
        scratch_shapes=[pltpu.VMEM((2, 1, tq), F32), pltpu.VMEM((2, VT_ROWS, tq), F32),
                        pltpu.VMEM((2, 2, tq, tq), F32)],
        compiler_params=_cparams(3),
        name="fox_attention" if fox else "mla_attention",
    )(*args)


def _mem_proj_kernel(mem_ref, w_ref, mk_ref, mvt_ref):
    mkv = jnp.dot(mem_ref[...].astype(BF16), w_ref[...], preferred_element_type=F32)
    mk_ref[0] = mkv[:, :MEM_WIDTH].astype(BF16)
    _store_values_transposed(mvt_ref, mkv[:, MEM_WIDTH:])


def _mem_proj(mem2d, w, nb):
    return pl.pallas_call(
        _mem_proj_kernel,
        grid=(nb,),
        in_specs=[pl.BlockSpec((N_MEM, D_MODEL), lambda b: (b, 0)),
                  pl.BlockSpec(w.shape, lambda b: (0, 0))],
        out_specs=[pl.BlockSpec((1, N_MEM, MEM_WIDTH), lambda b: (b, 0, 0)),
                   pl.BlockSpec((1, MEM_HEADS * VT_ROWS, N_MEM), lambda b: (b, 0, 0))],
        out_shape=[jax.ShapeDtypeStruct((nb, N_MEM, MEM_WIDTH), BF16),
                   jax.ShapeDtypeStruct((nb, MEM_HEADS * VT_ROWS, N_MEM), BF16)],
        compiler_params=_cparams(1),
        name="mem_proj",
    )(mem2d, w)


def _mem_attn_kernel(qt_ref, k_ref, vt_ref, o_ref):
    qt = qt_ref[0]
    kc = k_ref[0]
    for pair in range(MEM_HEADS // 2):
        qt_p = qt[pair * LANES:(pair + 1) * LANES]
        k_p = kc[:, pair * LANES:(pair + 1) * LANES]
        rowi = lax.broadcasted_iota(jnp.int32, qt_p.shape, 0)
        zero = jnp.zeros_like(qt_p)
        outs = []
        for h in range(2):
            mine = (rowi < HEAD_DIM) if h == 0 else (rowi >= HEAD_DIM)
            s = jnp.dot(k_p, jnp.where(mine, qt_p, zero), preferred_element_type=F32)
            p = jnp.exp2(s - jnp.max(s, axis=0, keepdims=True)).astype(BF16)
            r0 = (2 * pair + h) * VT_ROWS
            acc = jnp.dot(vt_ref[0, r0:r0 + VT_ROWS, :], p, preferred_element_type=F32)
            outs.append(acc[:HEAD_DIM] * (1.0 / acc[HEAD_DIM:HEAD_DIM + 1]))
        o_ref[0, :, pair * LANES:(pair + 1) * LANES] = (
            jnp.concatenate(outs, axis=0).T.astype(o_ref.dtype))


def _memory_attention(mqt, mk, mvt):
    nb, _, seq = mqt.shape
    tq = ATT_TQ
    return pl.pallas_call(
        _mem_attn_kernel,
        grid=(nb, seq // tq),
        in_specs=[pl.BlockSpec((1, MEM_WIDTH, tq), lambda b, i: (b, 0, i)),
                  pl.BlockSpec((1, N_MEM, MEM_WIDTH), lambda b, i: (b, 0, 0)),
                  pl.BlockSpec((1, MEM_HEADS * VT_ROWS, N_MEM), lambda b, i: (b, 0, 0))],
        out_specs=pl.BlockSpec((1, tq, MEM_WIDTH), lambda b, i: (b, i, 0)),
        out_shape=jax.ShapeDtypeStruct((nb, seq, MEM_WIDTH), BF16),
        compiler_params=_cparams(2),
        name="memory_attention",
    )(mqt, mk, mvt)


def _post_attn_kernel(x_ref, att_ref, memo_ref, wa_ref, wm_ref, g_ref, b_ref, wr_ref, br_ref,
                      x1_ref, x1p_ref, idx_ref, rank_ref, gate_ref, cnt_ref, carry_ref):
    t = pl.program_id(0)

    @pl.when(t == 0)
    def _():
        carry_ref[...] = jnp.zeros_like(carry_ref)

    tm = x_ref.shape[0]
    mix = (jnp.dot(att_ref[...], wa_ref[...], preferred_element_type=F32)
           + jnp.dot(memo_ref[...], wm_ref[...], preferred_element_type=F32))
    x1 = _layer_norm(DEEPNORM_ALPHA * x_ref[...] + mix, g_ref[...], b_ref[...])
    x1_ref[...] = x1
    x1b = x1.astype(BF16)
    bits = pltpu.bitcast(x1b.astype(F32), jnp.uint32)
    half = D_MODEL // 2
    x1p_ref[...] = (bits[:, :half] >> 16) | bits[:, half:]
    logits = jnp.dot(x1b, wr_ref[...], preferred_element_type=F32) + br_ref[...]
    lane = lax.broadcasted_iota(jnp.int32, (tm, LANES), 1)
    work = jnp.where(lane < N_EXPERTS, logits, -jnp.inf)
    idxs, vals = [], []
    onehot = jnp.zeros((tm, LANES), F32)
    for _ in range(TOP_K):
        best = jnp.max(work, axis=1, keepdims=True)
        where_best = jnp.min(jnp.where(work == best, lane, LANES), axis=1, keepdims=True)
        hit = lane == where_best
        onehot = jnp.where(hit, 1.0, onehot)
        work = jnp.where(hit, -jnp.inf, work)
        idxs.append(where_best)
        vals.append(best)
    exps = [jnp.exp(v - vals[0]) for v in vals]
    denom = exps[0] + exps[1] + exps[2] + exps[3]
    row = lax.broadcasted_iota(jnp.int32, (tm, tm), 0)
    col = lax.broadcasted_iota(jnp.int32, (tm, tm), 1)
    strict = jnp.where(row > col, 1.0, 0.0).astype(BF16)
    before = jnp.dot(strict, onehot.astype(BF16), preferred_element_type=F32) + carry_ref[...]
    idx_out = jnp.zeros((tm, LANES), jnp.int32)
    rank_out = jnp.zeros((tm, LANES), jnp.int32)
    gate_out = jnp.zeros((tm, LANES), F32)
    for r in range(TOP_K):
        rank_r = jnp.sum(jnp.where(lane == idxs[r], before, 0.0), axis=1, keepdims=True)
        idx_out = jnp.where(lane == r, idxs[r], idx_out)
        rank_out = jnp.where(lane == r, rank_r.astype(jnp.int32), rank_out)
        gate_out = jnp.where(lane == r, exps[r] / denom, gate_out)
    idx_ref[...] = idx_out
    rank_ref[...] = rank_out
    gate_ref[...] = gate_out
    total = carry_ref[...] + jnp.sum(onehot, axis=0, keepdims=True)
    carry_ref[...] = total
    cnt_ref[...] = jnp.broadcast_to(total, cnt_ref.shape)


def _post_attn(x2d, att, memo, wa, wm, g, b, wr, br):
    t_tokens = x2d.shape[0]
    tm = PROJ_TM
    row_spec = lambda width: pl.BlockSpec((tm, width), lambda i: (i, 0))
    full = lambda a: pl.BlockSpec(a.shape, lambda i: (0, 0))
    return pl.pallas_call(
        _post_attn_kernel,
        grid=(t_tokens // tm,),
        in_specs=[row_spec(D_MODEL), row_spec(att.shape[1]), row_spec(MEM_WIDTH),
                  full(wa), full(wm), full(g), full(b), full(wr), full(br)],
        out_specs=[row_spec(D_MODEL), row_spec(D_MODEL // 2), row_spec(LANES), row_spec(LANES),
                   row_spec(LANES), pl.BlockSpec((8, LANES), lambda i: (0, 0))],
        out_shape=[jax.ShapeDtypeStruct((t_tokens, D_MODEL), F32),
                   jax.ShapeDtypeStruct((t_tokens, D_MODEL // 2), jnp.uint32),
                   jax.ShapeDtypeStruct((t_tokens, LANES), jnp.int32),
                   jax.ShapeDtypeStruct((t_tokens, LANES), jnp.int32),
                   jax.ShapeDtypeStruct((t_tokens, LANES), F32),
                   jax.ShapeDtypeStruct((8, LANES), F32)],
        scratch_shapes=[pltpu.VMEM((1, LANES), F32)],
        compiler_params=_cparams(1),
        name="outproj_ln_router",
    )(x2d, att, memo, wa, wm, g, b, wr, br)


def _expert_kernel(be_ref, nu_ref, x_ref, wgu_ref, bgu_ref, wdn_ref, bdn_ref, o_ref,
                   wgu_bf, wdn_bf, *, n_chunks):
    blk = pl.program_id(0)

    @pl.when((blk == 0) | (be_ref[blk] != be_ref[jnp.maximum(blk - 1, 0)]))
    def _():
        wgu_bf[...] = wgu_ref[0, 0].astype(BF16)
        wdn_bf[...] = wdn_ref[0, 0].astype(BF16)

    @pl.when(blk < nu_ref[0])
    def _():
        words = x_ref[...]
        x = jnp.concatenate(
            [pltpu.bitcast(words << 16, F32), pltpu.bitcast(words & jnp.uint32(0xFFFF0000), F32)],
            axis=1).astype(BF16)
        cw = D_EXPERT // n_chunks
        acc = jnp.zeros(o_ref.shape, F32)
        for c in range(n_chunks):
            gs = slice(c * cw, (c + 1) * cw)
            us = slice(D_EXPERT + c * cw, D_EXPERT + (c + 1) * cw)
            g = jnp.dot(x, wgu_bf[:, gs], preferred_element_type=F32) + bgu_ref[0, 0, :, gs]
            u = jnp.dot(x, wgu_bf[:, us], preferred_element_type=F32) + bgu_ref[0, 0, :, us]
            g = jnp.minimum(g, SWIGLU_LIMIT)
            u = jnp.clip(u, -SWIGLU_LIMIT, SWIGLU_LIMIT)
            hmid = (u + 1.0) * (g * jax.nn.sigmoid(SWIGLU_ALPHA * g))
            acc = acc + jnp.dot(hmid.astype(BF16), wdn_bf[gs, :], preferred_element_type=F32)
        o_ref[...] = acc + bdn_ref[0, 0]

    @pl.when(blk >= nu_ref[0])
    def _():
        o_ref[...] = jnp.zeros_like(o_ref)


def _experts(block_expert, n_used, xb, layer, wgu, bgu, wdn, bdn):
    p_rows = xb.shape[0]
    tm = MOE_TM
    grid_spec = pltpu.PrefetchScalarGridSpec(
        num_scalar_prefetch=2,
        grid=(p_rows // tm,),
        in_specs=[pl.BlockSpec((tm, D_MODEL // 2), lambda i, be, nu: (i, 0)),
                  pl.BlockSpec((1, 1, D_MODEL, 2 * D_EXPERT), lambda i, be, nu: (layer, be[i], 0, 0)),
                  pl.BlockSpec((1, 1, 1, 2 * D_EXPERT), lambda i, be, nu: (layer, be[i], 0, 0)),
                  pl.BlockSpec((1, 1, D_EXPERT, D_MODEL), lambda i, be, nu: (layer, be[i], 0, 0)),
                  pl.BlockSpec((1, 1, 1, D_MODEL), lambda i, be, nu: (layer, be[i], 0, 0))],
        out_specs=pl.BlockSpec((tm, D_MODEL), lambda i, be, nu: (i, 0)),
        scratch_shapes=[pltpu.VMEM((D_MODEL, 2 * D_EXPERT), BF16),
                        pltpu.VMEM((D_EXPERT, D_MODEL), BF16)],
    )
    return pl.pallas_call(
        functools.partial(_expert_kernel, n_chunks=4),
        grid_spec=grid_spec,
        out_shape=jax.ShapeDtypeStruct((p_rows, D_MODEL), F32),
        compiler_params=pltpu.CompilerParams(dimension_semantics=("arbitrary",),
                                             vmem_limit_bytes=EXPERT_VMEM_LIMIT),
        name="experts",
    )(block_expert, n_used, xb, wgu, bgu, wdn, bdn)


def _sc_gather_rows(table, idx):
    n_idx = idx.shape[0]
    width = table.shape[1]
    per_worker = n_idx // SC_WORKERS
    n_chunks = per_worker // SC_CHUNK
    mesh = plsc.VectorSubcoreMesh(core_axis_name="c", subcore_axis_name="s",
                                  num_cores=SC_CORES, num_subcores=SC_SUBCORES)

    def body(table_hbm, idx_hbm, out_hbm, idx_v, rows_v, sem):
        wid = lax.axis_index("s") * SC_CORES + lax.axis_index("c")
        base = wid * per_worker
        pltpu.sync_copy(idx_hbm.at[pl.ds(base, per_worker)], idx_v)

        @pl.loop(0, n_chunks)
        def _(c):
            off = pl.multiple_of(c * SC_CHUNK, SC_CHUNK)
            pltpu.async_copy(table_hbm.at[idx_v.at[pl.ds(off, SC_CHUNK)]], rows_v, sem).wait()
            pltpu.sync_copy(rows_v, out_hbm.at[pl.ds(base + off, SC_CHUNK)])

    return pl.kernel(
        body,
        out_type=jax.ShapeDtypeStruct((n_idx, width), table.dtype),
        mesh=mesh,
        scratch_types=[pltpu.VMEM((per_worker,), jnp.int32),
                       pltpu.VMEM((SC_CHUNK, width), table.dtype),
                       pltpu.SemaphoreType.DMA],
        name="sc_gather_rows",
    )(table, idx)


def _sc_dispatch_rows(table, dest, p_rows):
    n_tok, width = table.shape
    n_pairs = dest.shape[0]
    per_worker = p_rows // SC_WORKERS
    n_chunks = per_worker // SC_CHUNK
    n_scan = n_pairs // SC_SCAN
    mesh = plsc.VectorSubcoreMesh(core_axis_name="c", subcore_axis_name="s",
                                  num_cores=SC_CORES, num_subcores=SC_SUBCORES)

    def body(table_hbm, dest_hbm, out_hbm, tok_v, dest_v, rows_v, sem):
        wid = lax.axis_index("s") * SC_CORES + lax.axis_index("c")
        base = wid * per_worker
        lane = lax.iota(jnp.int32, SC_LANES)

        @pl.loop(0, per_worker // SC_LANES)
        def _(i):
            off = pl.multiple_of(i * SC_LANES, SC_LANES)
            tok_v[pl.ds(off, SC_LANES)] = lax.rem(base + off + lane, n_tok)

        @pl.loop(0, n_scan)
        def _(g):
            goff = pl.multiple_of(g * SC_SCAN, SC_SCAN)
            pltpu.sync_copy(dest_hbm.at[pl.ds(goff, SC_SCAN)], dest_v)

            @pl.loop(0, SC_SCAN // SC_LANES)
            def _(i):
                off = pl.multiple_of(i * SC_LANES, SC_LANES)
                local = dest_v[pl.ds(off, SC_LANES)] - base
                mine = (local >= 0) & (local < per_worker)
                pair = goff + off + lane
                plsc.store_scatter(tok_v, [jnp.where(mine, local, 0)], pair // TOP_K, mask=mine)

        @pl.loop(0, n_chunks)
        def _(c):
            off = pl.multiple_of(c * SC_CHUNK, SC_CHUNK)
            pltpu.async_copy(table_hbm.at[tok_v.at[pl.ds(off, SC_CHUNK)]], rows_v, sem).wait()
            pltpu.sync_copy(rows_v, out_hbm.at[pl.ds(base + off, SC_CHUNK)])

    return pl.kernel(
        body,
        out_type=jax.ShapeDtypeStruct((p_rows, width), table.dtype),
        mesh=mesh,
        scratch_types=[pltpu.VMEM((per_worker,), jnp.int32),
                       pltpu.VMEM((SC_SCAN,), jnp.int32),
                       pltpu.VMEM((SC_CHUNK, width), table.dtype),
                       pltpu.SemaphoreType.DMA],
        compiler_params=pltpu.CompilerParams(needs_layout_passes=False),
        name="sc_dispatch_rows",
    )(table, dest)


def _combine_kernel(x1_ref, yg_ref, gate_ref, g_ref, b_ref, o_ref):
    gates = gate_ref[...]
    ffn = yg_ref[0] * gates[:, 0:1]
    for r in range(1, TOP_K):
        ffn = ffn + yg_ref[r] * gates[:, r:r + 1]
    o_ref[...] = _layer_norm(DEEPNORM_ALPHA * x1_ref[...] + ffn, g_ref[...], b_ref[...])


def _combine(x1, yg, gates, g, b):
    t_tokens = x1.shape[0]
    tm = 256
    return pl.pallas_call(
        _combine_kernel,
        grid=(t_tokens // tm,),
        in_specs=[pl.BlockSpec((tm, D_MODEL), lambda i: (i, 0)),
                  pl.BlockSpec((TOP_K, tm, D_MODEL), lambda i: (0, i, 0)),
                  pl.BlockSpec((tm, LANES), lambda i: (i, 0)),
                  pl.BlockSpec((1, D_MODEL), lambda i: (0, 0)),
                  pl.BlockSpec((1, D_MODEL), lambda i: (0, 0))],
        out_specs=pl.BlockSpec((tm, D_MODEL), lambda i: (i, 0)),
        out_shape=jax.ShapeDtypeStruct((t_tokens, D_MODEL), F32),
        compiler_params=_cparams(1),
        name="combine_ln",
    )(x1, yg, gates, g, b)


def _moe(x1, x1p, idx, rank, gates, counts, layer, wgu, bgu, wdn, bdn, g, b):
    t_tokens = x1.shape[0]
    tm = MOE_TM
    counts = counts[0, :N_EXPERTS].astype(jnp.int32)
    padded = ((counts + tm - 1) // tm) * tm
    pend = jnp.cumsum(padded)
    pstart = pend - padded
    idx4 = idx[:, :TOP_K]
    dest = (pstart[idx4] + rank[:, :TOP_K]).reshape(-1)
    n_blocks = (t_tokens * TOP_K) // tm + N_EXPERTS
    p_rows = n_blocks * tm
    block_start = jnp.arange(n_blocks, dtype=jnp.int32) * tm
    block_expert = jnp.minimum(
        jnp.sum((pend[None, :] <= block_start[:, None]).astype(jnp.int32), axis=1), N_EXPERTS - 1)
    n_used = (pend[-1:] // tm).astype(jnp.int32)
    xb = _sc_dispatch_rows(x1p, dest, p_rows)
    y = _experts(block_expert, n_used, xb, layer, wgu, bgu, wdn, bdn)
    dest_by_choice = dest.reshape(t_tokens, TOP_K).T.reshape(-1)
    yg = _sc_gather_rows(y, dest_by_choice).reshape(TOP_K, t_tokens, D_MODEL)
    return _combine(x1, yg, gates, g, b)


def _row(v, width=None):
    v = v.astype(F32).reshape(1, -1)
    if width is not None and v.shape[1] < width:
        v = jnp.pad(v, ((0, 0), (0, width - v.shape[1])))
    return v


def _pad_cols(w, width):
    return jnp.pad(w, ((0, 0), (0, width - w.shape[1])))


def kernel(x, mem, positions, a_w_in, a_b_f, a_w_out, b_w_in, b_g_q, b_w_uq, b_w_out,
           kv_w_dkv, kv_g, kv_w_ukv, mem_w_kv, ln_g, ln_b,
           moe_w_r, moe_b_r, moe_w_gu, moe_b_gu, moe_w_dn, moe_b_dn):
    nb, seq, d = x.shape
    t_tokens = nb * seq
    n_a = a_w_in.shape[0]
    x2d = x.reshape(t_tokens, d)
    mem2d = mem.reshape(nb * N_MEM, d)
    pos2d = positions.reshape(t_tokens, 1)
    half = QK_ROPE // 2
    inv_freq = ROPE_THETA ** (-jnp.arange(half, dtype=F32) * 2.0 / QK_ROPE)
    invf = jnp.zeros((1, LANES), F32)
    invf = invf.at[0, QK_NOPE:QK_NOPE + half].set(inv_freq)
    invf = invf.at[0, QK_NOPE + half:QK_NOPE + QK_ROPE].set(inv_freq)

    shared_kv = None
    for l in range(DEPTH):
        mk, mvt = _mem_proj(mem2d, mem_w_kv[l].astype(BF16), nb)
        if l < n_a:
            w_in = a_w_in[l]
            w = jnp.concatenate([w_in[:, :3 * FOX_WIDTH],
                                 _pad_cols(w_in[:, 3 * FOX_WIDTH:3 * FOX_WIDTH + FOX_HEADS], LANES),
                                 w_in[:, 3 * FOX_WIDTH + FOX_HEADS:]], axis=1).astype(BF16)
            qt, k, vt, mqt, cum, cumt, cfirst, clast, knorm = _fox_proj(
                x2d, w, _row(a_b_f[l], LANES), seq)
            stats = [s.reshape(-1, LANES) for s in (cfirst, clast, knorm)]
            att = _causal_attention(qt, k.reshape(nb, seq, -1), vt,
                                    [cum.reshape(nb, seq, LANES), cumt] + stats, slab=False)
            w_out = a_w_out[l]
        else:
            bl = l - n_a
            w = jnp.concatenate([b_w_in[bl], kv_w_dkv[:, :KV_LORA],
                                 jnp.zeros((d, QK_NOPE), F32), kv_w_dkv[:, KV_LORA:],
                                 jnp.zeros((d, LANES - QK_NOPE - QK_ROPE), F32)], axis=1).astype(BF16)
            wuq = b_w_uq[bl].reshape(Q_LORA, MLA_HEADS, QK_NOPE + QK_ROPE)
            wuq = jnp.pad(wuq, ((0, 0), (0, 0), (0, MLA_SLAB - QK_NOPE - QK_ROPE)))
            wuq = wuq.reshape(Q_LORA, MLA_HEADS * MLA_SLAB).astype(BF16)
            wukv = kv_w_ukv.reshape(KV_LORA, MLA_HEADS, QK_NOPE + V_DIM)
            wk = jnp.pad(wukv[:, :, :QK_NOPE], ((0, 0), (0, 0), (0, MLA_SLAB - QK_NOPE)))
            wk = wk.reshape(KV_LORA, MLA_HEADS * MLA_SLAB).astype(BF16)
            wv = wukv[:, :, QK_NOPE:].reshape(KV_LORA, MLA_V_WIDTH).astype(BF16)
            qt, mqt, k_new, vt_new = _mla_proj(x2d, pos2d, invf, w, _row(b_g_q[bl]), wuq,
                                              _row(kv_g), wk, wv, seq)
            if shared_kv is None:
                shared_kv = (k_new.reshape(nb, seq, -1), vt_new)
            att = _causal_attention(qt, shared_kv[0], shared_kv[1], None, slab=True)
            w_out = b_w_out[bl]
        memo = _memory_attention(mqt, mk, mvt)
        n_att = w_out.shape[0] - MEM_WIDTH
        x1, x1p, idx, rank, gates, counts = _post_attn(
            x2d, att.reshape(t_tokens, -1), memo.reshape(t_tokens, MEM_WIDTH),
            w_out[:n_att].astype(BF16), w_out[n_att:].astype(BF16),
            _row(ln_g[l, 0]), _row(ln_b[l, 0]),
            _pad_cols(moe_w_r[l], LANES).astype(BF16), _row(moe_b_r[l], LANES))
        x2d = _moe(x1, x1p, idx, rank, gates, counts, l,
                   moe_w_gu, moe_b_gu.reshape(DEPTH, N_EXPERTS, 1, -1),
                   moe_w_dn, moe_b_dn.reshape(DEPTH, N_EXPERTS, 1, -1),
                   _row(ln_g[l, 1]), _row(ln_b[l, 1]))
    return x2d.reshape(nb, seq, d)
```

```python
import functools
import math

import jax
import jax.numpy as jnp
from jax import lax
from jax.experimental import pallas as pl
from jax.experimental.pallas import tpu as pltpu
from jax.experimental.pallas import tpu_sc as plsc

F32 = jnp.float32
BF16 = jnp.bfloat16

D_MODEL = 1024
DEPTH = 2
N_MEM = 256
HEAD_DIM = 64
FOX_HEADS = 12
MEM_HEADS = 4
MLA_HEADS = 12
Q_LORA = 384
KV_LORA = 256
QK_NOPE = 64
QK_ROPE = 32
V_DIM = 64
ROPE_THETA = 10000.0
N_EXPERTS = 32
TOP_K = 4
D_EXPERT = D_MODEL
SWIGLU_LIMIT = 7.0
SWIGLU_ALPHA = 1.702
LN_EPS = 1e-5
RMS_EPS = 1e-6
NEG_INF = -1e30
DEEPNORM_ALPHA = (2 * DEPTH) ** 0.25
FOX_WIDTH = FOX_HEADS * HEAD_DIM
MEM_WIDTH = MEM_HEADS * HEAD_DIM
MLA_V_WIDTH = MLA_HEADS * V_DIM

LANES = 128
LOG2E = math.log2(math.e)
VMEM_LIMIT = 48 * 1024 * 1024
EXPERT_VMEM_LIMIT = 56 * 1024 * 1024

PROJ_TM = 512
ATT_TQ = 512
FOX_TQ = 256
MOE_TM = 512
MLA_SLAB = LANES
VT_ROWS = HEAD_DIM + 16
SC_CORES = 2
SC_SUBCORES = 16
SC_WORKERS = SC_CORES * SC_SUBCORES
SC_CHUNK = 32
SC_LANES = 16
SC_SCAN = 8192


def _cparams(n_axes):
    return pltpu.CompilerParams(dimension_semantics=("arbitrary",) * n_axes,
                                vmem_limit_bytes=VMEM_LIMIT)


def _split3(x):
    hi = x.astype(BF16)
    r1 = x - hi.astype(F32)
    mid = r1.astype(BF16)
    lo = (r1 - mid.astype(F32)).astype(BF16)
    return hi, mid, lo


def _layer_norm(y, g, b):
    mu = jnp.mean(y, axis=-1, keepdims=True)
    yc = y - mu
    var = jnp.mean(yc * yc, axis=-1, keepdims=True)
    return yc * lax.rsqrt(var + LN_EPS) * g + b


def _rms_norm(y, g):
    return y * lax.rsqrt(jnp.mean(y * y, axis=-1, keepdims=True) + RMS_EPS) * g


def _store_transposed(dst_ref, val):
    for s in range(val.shape[1] // LANES):
        sl = slice(s * LANES, (s + 1) * LANES)
        dst_ref[0, sl, :] = val[:, sl].T.astype(dst_ref.dtype)


def _store_values_transposed(vt_ref, val):
    tm = val.shape[0]
    ones = jnp.ones((VT_ROWS - HEAD_DIM, tm), vt_ref.dtype)
    for s in range(val.shape[1] // LANES):
        pair_t = val[:, s * LANES:(s + 1) * LANES].T.astype(vt_ref.dtype)
        for h in range(2):
            r0 = (2 * s + h) * VT_ROWS
            vt_ref[0, r0:r0 + HEAD_DIM, :] = pair_t[h * HEAD_DIM:(h + 1) * HEAD_DIM]
            vt_ref[0, r0 + HEAD_DIM:r0 + VT_ROWS, :] = ones


def _fox_proj_kernel(x_ref, w_ref, bf_ref, seg_ref, qt_ref, k_ref, vt_ref, mqt_ref, cum_ref,
                     cumt_ref, cfirst_ref, clast_ref, knorm_ref, carry_ref, knmax_ref, *,
                     tiles_per_batch):
    t = pl.program_id(0)

    @pl.when(t % tiles_per_batch == 0)
    def _():
        carry_ref[...] = jnp.zeros_like(carry_ref)
        knmax_ref[...] = jnp.zeros_like(knmax_ref)

    tm = x_ref.shape[0]
    proj = jnp.dot(x_ref[...].astype(BF16), w_ref[...], preferred_element_type=F32)
    qscale = HEAD_DIM ** -0.5 * LOG2E
    _store_transposed(qt_ref, proj[:, :FOX_WIDTH] * qscale)
    kb = proj[:, FOX_WIDTH:2 * FOX_WIDTH].astype(BF16)
    k_ref[...] = kb
    _store_values_transposed(vt_ref, proj[:, 2 * FOX_WIDTH:3 * FOX_WIDTH])
    kf = kb.astype(F32)
    ksq = jnp.dot((kf * kf).astype(BF16), seg_ref[...], preferred_element_type=F32)
    n_sub = tm // FOX_TQ
    for sub in range(n_sub):
        tile_max = jnp.sqrt(jnp.max(ksq[sub * FOX_TQ:(sub + 1) * FOX_TQ], axis=0, keepdims=True))
        knmax_ref[...] = jnp.maximum(knmax_ref[...], tile_max)
        knorm_ref[sub] = knmax_ref[...]
    f = proj[:, 3 * FOX_WIDTH:3 * FOX_WIDTH + LANES] + bf_ref[...]
    _store_transposed(mqt_ref, proj[:, 3 * FOX_WIDTH + LANES:] * qscale)
    log_f = jnp.minimum(f, 0.0) - jnp.log1p(jnp.exp(-jnp.abs(f)))
    row = lax.broadcasted_iota(jnp.int32, (tm, tm), 0)
    col = lax.broadcasted_iota(jnp.int32, (tm, tm), 1)
    tri = jnp.where(row >= col, 1.0, 0.0).astype(BF16)
    hi, mid, lo = _split3(log_f)
    cum = (jnp.dot(tri, hi, preferred_element_type=F32)
           + jnp.dot(tri, mid, preferred_element_type=F32)
           + jnp.dot(tri, lo, preferred_element_type=F32)) + carry_ref[...]
    carry_ref[...] = cum[tm - 1:tm, :]
    cum2 = cum * LOG2E
    cum_ref[...] = cum2
    cumt_ref[0] = cum2.T[:16, :]
    for sub in range(n_sub):
        cfirst_ref[sub] = cum2[sub * FOX_TQ:sub * FOX_TQ + 1, :]
        clast_ref[sub] = cum2[(sub + 1) * FOX_TQ - 1:(sub + 1) * FOX_TQ, :]


def _fox_proj(x2d, w, bf, seq):
    t_tokens = x2d.shape[0]
    tm = PROJ_TM
    nb = t_tokens // seq
    n = w.shape[1]
    tiles_per_batch = seq // tm
    n_tiles = t_tokens // tm
    seg = (jnp.arange(FOX_WIDTH)[:, None] // HEAD_DIM == jnp.arange(LANES)[None, :]).astype(BF16)
    row_spec = lambda width: pl.BlockSpec((tm, width), lambda i: (i, 0))
    t_spec = lambda rows: pl.BlockSpec(
        (1, rows, tm), lambda i: (i // tiles_per_batch, 0, i % tiles_per_batch))
    n_sub = tm // FOX_TQ
    stat_spec = pl.BlockSpec((n_sub, 1, LANES), lambda i: (i, 0, 0))
    stat_shape = jax.ShapeDtypeStruct((n_tiles * n_sub, 1, LANES), F32)
    return pl.pallas_call(
        functools.partial(_fox_proj_kernel, tiles_per_batch=tiles_per_batch),
        grid=(n_tiles,),
        in_specs=[row_spec(D_MODEL),
                  pl.BlockSpec((D_MODEL, n), lambda i: (0, 0)),
                  pl.BlockSpec((1, LANES), lambda i: (0, 0)),
                  pl.BlockSpec((FOX_WIDTH, LANES), lambda i: (0, 0))],
        out_specs=[t_spec(FOX_WIDTH), row_spec(FOX_WIDTH), t_spec(FOX_HEADS * VT_ROWS),
                   t_spec(MEM_WIDTH), row_spec(LANES), t_spec(16),
                   stat_spec, stat_spec, stat_spec],
        out_shape=[jax.ShapeDtypeStruct((nb, FOX_WIDTH, seq), BF16),
                   jax.ShapeDtypeStruct((t_tokens, FOX_WIDTH), BF16),
                   jax.ShapeDtypeStruct((nb, FOX_HEADS * VT_ROWS, seq), BF16),
                   jax.ShapeDtypeStruct((nb, MEM_WIDTH, seq), BF16),
                   jax.ShapeDtypeStruct((t_tokens, LANES), F32),
                   jax.ShapeDtypeStruct((nb, 16, seq), F32),
                   stat_shape, stat_shape, stat_shape],
        scratch_shapes=[pltpu.VMEM((1, LANES), F32), pltpu.VMEM((1, LANES), F32)],
        compiler_params=_cparams(1),
        name="fox_proj",
    )(x2d, w, bf, seg)


def _rope_tables(pos_ref, invf_ref):
    ang = pos_ref[...].astype(F32) * invf_ref[...]
    cos = jnp.cos(ang)
    sin = jnp.sin(ang)
    lane = lax.broadcasted_iota(jnp.int32, ang.shape, 1)
    half = QK_ROPE // 2
    in_x1 = (lane >= QK_NOPE) & (lane < QK_NOPE + half)
    in_x2 = (lane >= QK_NOPE + half) & (lane < QK_NOPE + QK_ROPE)
    c_tab = jnp.where(lane < QK_NOPE, 1.0, jnp.where(in_x1 | in_x2, cos, 0.0))
    s_up = jnp.where(in_x2, sin, 0.0)
    s_dn = jnp.where(in_x1, -sin, 0.0)
    return c_tab, s_up, s_dn


def _rope_slab(slab, tables):
    c_tab, s_up, s_dn = tables
    half = QK_ROPE // 2
    up = pltpu.roll(slab, half, 1)
    dn = pltpu.roll(slab, LANES - half, 1)
    return slab * c_tab + up * s_up + dn * s_dn


def _mla_proj_kernel(x_ref, pos_ref, invf_ref, w_ref, gq_ref, wuq_ref, gkv_ref, wk_ref, wv_ref,
                     qt_ref, mqt_ref, k_ref, vt_ref):
    proj = jnp.dot(x_ref[...].astype(BF16), w_ref[...], preferred_element_type=F32)
    tables = _rope_tables(pos_ref, invf_ref)
    c_q = _rms_norm(proj[:, :Q_LORA], gq_ref[...])
    _store_transposed(mqt_ref, proj[:, Q_LORA:Q_LORA + MEM_WIDTH] * (HEAD_DIM ** -0.5 * LOG2E))
    kv_off = Q_LORA + MEM_WIDTH
    c_kv = _rms_norm(proj[:, kv_off:kv_off + KV_LORA], gkv_ref[...])
    kr = _rope_slab(proj[:, kv_off + KV_LORA:], tables)
    q = jnp.dot(c_q.astype(BF16), wuq_ref[...], preferred_element_type=F32)
    kn = jnp.dot(c_kv.astype(BF16), wk_ref[...], preferred_element_type=F32)
    qscale = (QK_NOPE + QK_ROPE) ** -0.5 * LOG2E
    for h in range(MLA_HEADS):
        sl = slice(h * MLA_SLAB, (h + 1) * MLA_SLAB)
        qt_ref[0, sl, :] = (_rope_slab(q[:, sl], tables) * qscale).T.astype(BF16)
        k_ref[:, sl] = (kn[:, sl] + kr).astype(BF16)
    _store_values_transposed(vt_ref, jnp.dot(c_kv.astype(BF16), wv_ref[...],
                                             preferred_element_type=F32))


def _mla_proj(x2d, pos2d, invf, w, gq, wuq, gkv, wk, wv, seq):
    t_tokens = x2d.shape[0]
    tm = PROJ_TM
    nb = t_tokens // seq
    tiles_per_batch = seq // tm
    row_spec = lambda width: pl.BlockSpec((tm, width), lambda i: (i, 0))
    t_spec = lambda rows: pl.BlockSpec(
        (1, rows, tm), lambda i: (i // tiles_per_batch, 0, i % tiles_per_batch))
    full = lambda a: pl.BlockSpec(a.shape, lambda i: (0, 0))
    slabs = MLA_HEADS * MLA_SLAB
    return pl.pallas_call(
        _mla_proj_kernel,
        grid=(t_tokens // tm,),
        in_specs=[row_spec(D_MODEL), row_spec(1), full(invf), full(w), full(gq), full(wuq),
                  full(gkv), full(wk), full(wv)],
        out_specs=[t_spec(slabs), t_spec(MEM_WIDTH), row_spec(slabs),
                   t_spec(MLA_HEADS * VT_ROWS)],
        out_shape=[jax.ShapeDtypeStruct((nb, slabs, seq), BF16),
                   jax.ShapeDtypeStruct((nb, MEM_WIDTH, seq), BF16),
                   jax.ShapeDtypeStruct((t_tokens, slabs), BF16),
                   jax.ShapeDtypeStruct((nb, MLA_HEADS * VT_ROWS, seq), BF16)],
        compiler_params=_cparams(1),
        name="mla_proj",
    )(x2d, pos2d, invf, w, gq, wuq, gkv, wk, wv)


SKIP_LOG2 = -160.0
NORM_SLACK = 1.02


def _causal_attn_kernel(*refs, tq, n_tiles, fox, slab, long_body):
    if fox:
        (cf_ref, cl_ref, kn_ref, qt_ref, k_ref, vt_ref, cq_ref, ck_ref,
         o_ref, m_ref, acc_ref, s_ref) = refs
    else:
        qt_ref, k_ref, vt_ref, o_ref, m_ref, acc_ref, s_ref = refs
    b = pl.program_id(0)
    pair = pl.program_id(1)
    i = pl.program_id(2)
    m_ref[...] = jnp.full(m_ref.shape, NEG_INF, F32)
    acc_ref[...] = jnp.zeros(acc_ref.shape, F32)
    qt = qt_ref[0]
    if slab:
        qth = [qt[:MLA_SLAB], qt[MLA_SLAB:]]
    else:
        rowi = lax.broadcasted_iota(jnp.int32, qt.shape, 0)
        zero = jnp.zeros_like(qt)
        qth = [jnp.where(rowi < HEAD_DIM, qt, zero), jnp.where(rowi >= HEAD_DIM, qt, zero)]
    if fox:
        lane = lax.broadcasted_iota(jnp.int32, (1, LANES), 1)
        cq = [cq_ref[0, pl.ds(2 * pair + h, 1), :] for h in range(2)]

    def logits_to(slot, j):
        off = pl.multiple_of(j * tq, tq)
        kc = k_ref[0, pl.ds(off, tq), :]
        for h in range(2):
            kh = kc[:, h * MLA_SLAB:(h + 1) * MLA_SLAB] if slab else kc
            s = jnp.dot(kh, qth[h], preferred_element_type=F32)
            if fox:
                ck_blk = ck_ref[0, pl.ds(off, tq), :]
                ck = jnp.sum(jnp.where(lane == 2 * pair + h, ck_blk, 0.0), axis=1, keepdims=True)
                s = s + cq[h] - ck
            s_ref[slot, h] = s

    def softmax_pv(slot, j, causal_mask):
        off = pl.multiple_of(j * tq, tq)
        for h in range(2):
            s = s_ref[slot, h]
            if causal_mask is not None:
                s = jnp.where(causal_mask, s, NEG_INF)
            m_prev = m_ref[h]
            m_new = jnp.maximum(m_prev, jnp.max(s, axis=0, keepdims=True))
            alpha = jnp.exp2(m_prev - m_new)
            p = jnp.exp2(s - m_new).astype(BF16)
            vth = vt_ref[0, h * VT_ROWS:(h + 1) * VT_ROWS, pl.ds(off, tq)]
            acc_ref[h] = acc_ref[h] * alpha + jnp.dot(vth, p, preferred_element_type=F32)
            m_ref[h] = m_new

    if fox:
        base = b * n_tiles
        qn = []
        for h in range(2):
            qf = qth[h].astype(F32)
            qn.append(jnp.sqrt(jnp.max(jnp.sum(qf * qf, axis=0, keepdims=True))) * NORM_SLACK)

        top = [2.0 * qn[h] * kn_ref[base + i, 2 * pair + h] + cf_ref[base + i, 2 * pair + h]
               for h in range(2)]

        def live(j):
            ub = [top[h] - cl_ref[base + jnp.maximum(j, 0), 2 * pair + h] for h in range(2)]
            return (j >= 0) & (jnp.maximum(ub[0], ub[1]) > SKIP_LOG2)

        j0 = lax.while_loop(live, lambda j: j - 1, i - 1) + 1
    else:
        j0 = 0
    n_before = i - j0
    odd = (n_before % 2) == 1

    @pl.when(odd)
    def _():
        logits_to(1, j0)

    @pl.when(jnp.logical_not(odd))
    def _():
        logits_to(0, j0)

    @pl.when(odd)
    def _():
        logits_to(0, j0 + 1)
        softmax_pv(1, j0, None)

    def two_steps(j):
        logits_to(1, j + 1)
        softmax_pv(0, j, None)
        logits_to(0, j + 2)
        softmax_pv(1, j + 1, None)

    j_even = j0 + (n_before % 2)
    n_two = n_before // 2
    if long_body:
        @pl.when((n_two % 2) == 1)
        def _():
            two_steps(j_even)

        j_quad = j_even + 2 * (n_two % 2)

        def body(t, carry):
            two_steps(j_quad + 4 * t)
            two_steps(j_quad + 4 * t + 2)
            return carry

        lax.fori_loop(0, n_two // 2, body, 0)
    else:
        def body(t, carry):
            two_steps(j_even + 2 * t)
            return carry

        lax.fori_loop(0, n_two, body, 0)
    key = lax.broadcasted_iota(jnp.int32, (tq, tq), 0)
    qry = lax.broadcasted_iota(jnp.int32, (tq, tq), 1)
    softmax_pv(0, i, key <= qry)
    out_t = jnp.concatenate(
        [acc_ref[h, :HEAD_DIM] * (1.0 / acc_ref[h, HEAD_DIM:HEAD_DIM + 1]) for h in range(2)],
        axis=0)
    o_ref[0] = out_t.T.astype(o_ref.dtype)


def _causal_attention(qt, k, vt, fox_args, *, slab):
    nb, seq, _ = k.shape
    fox = fox_args is not None
    tq = FOX_TQ if fox else ATT_TQ
    n_tiles = seq // tq
    n_pairs = FOX_HEADS // 2
    rows = 2 * MLA_SLAB if slab else LANES
    in_specs = [pl.BlockSpec((1, rows, tq), lambda b, p, i: (b, p, i)),
                pl.BlockSpec((1, seq, rows), lambda b, p, i: (b, 0, p)),
                pl.BlockSpec((1, 2 * VT_ROWS, seq), lambda b, p, i: (b, p, 0))]
    args = [qt, k, vt]
    if fox:
        cum, cumt, cfirst, clast, knorm = fox_args
        smem = pl.BlockSpec(memory_space=pltpu.SMEM)
        in_specs = [smem, smem, smem] + in_specs + [
            pl.BlockSpec((1, 16, tq), lambda b, p, i: (b, 0, i)),
            pl.BlockSpec((1, seq, LANES), lambda b, p, i: (b, 0, 0))]
        args = [cfirst, clast, knorm] + args + [cumt, cum]
    return pl.pallas_call(
        functools.partial(_causal_attn_kernel, tq=tq, n_tiles=n_tiles, fox=fox, slab=slab,
                          long_body=not fox),
        grid=(nb, n_pairs, n_tiles),
        in_specs=in_specs,
        out_specs=pl.BlockSpec((1, tq, LANES), lambda b, p, i: (b, i, p)),
        out_shape=jax.ShapeDtypeStruct((nb, seq, n_pairs * LANES), BF16),
        scratch_shapes=[pltpu.VMEM((2, 1, tq), F32), pltpu.VMEM((2, VT_ROWS, tq), F32),
                        pltpu.VMEM((2, 2, tq, tq), F32)],
        compiler_params=_cparams(3),
        name="fox_attention" if fox else "mla_attention",
    )(*args)


def _mem_proj_kernel(mem_ref, w_ref, mk_ref, mvt_ref):
    mkv = jnp.dot(mem_ref[...].astype(BF16), w_ref[...], preferred_element_type=F32)
    mk_ref[0] = mkv[:, :MEM_WIDTH].astype(BF16)
    _store_values_transposed(mvt_ref, mkv[:, MEM_WIDTH:])


def _mem_proj(mem2d, w, nb):
    return pl.pallas_call(
        _mem_proj_kernel,
        grid=(nb,),
        in_specs=[pl.BlockSpec((N_MEM, D_MODEL), lambda b: (b, 0)),
                  pl.BlockSpec(w.shape, lambda b: (0, 0))],
        out_specs=[pl.BlockSpec((1, N_MEM, MEM_WIDTH), lambda b: (b, 0, 0)),
                   pl.BlockSpec((1, MEM_HEADS * VT_ROWS, N_MEM), lambda b: (b, 0, 0))],
        out_shape=[jax.ShapeDtypeStruct((nb, N_MEM, MEM_WIDTH), BF16),
                   jax.ShapeDtypeStruct((nb, MEM_HEADS * VT_ROWS, N_MEM), BF16)],
        compiler_params=_cparams(1),
        name="mem_proj",
    )(mem2d, w)


def _mem_attn_kernel(qt_ref, k_ref, vt_ref, o_ref):
    qt = qt_ref[0]
    kc = k_ref[0]
    for pair in range(MEM_HEADS // 2):
        qt_p = qt[pair * LANES:(pair + 1) * LANES]
        k_p = kc[:, pair * LANES:(pair + 1) * LANES]
        rowi = lax.broadcasted_iota(jnp.int32, qt_p.shape, 0)
        zero = jnp.zeros_like(qt_p)
        outs = []
        for h in range(2):
            mine = (rowi < HEAD_DIM) if h == 0 else (rowi >= HEAD_DIM)
            s = jnp.dot(k_p, jnp.where(mine, qt_p, zero), preferred_element_type=F32)
            p = jnp.exp2(s - jnp.max(s, axis=0, keepdims=True)).astype(BF16)
            r0 = (2 * pair + h) * VT_ROWS
            acc = jnp.dot(vt_ref[0, r0:r0 + VT_ROWS, :], p, preferred_element_type=F32)
            outs.append(acc[:HEAD_DIM] * (1.0 / acc[HEAD_DIM:HEAD_DIM + 1]))
        o_ref[0, :, pair * LANES:(pair + 1) * LANES] = (
            jnp.concatenate(outs, axis=0).T.astype(o_ref.dtype))


def _memory_attention(mqt, mk, mvt):
    nb, _, seq = mqt.shape
    tq = ATT_TQ
    return pl.pallas_call(
        _mem_attn_kernel,
        grid=(nb, seq // tq),
        in_specs=[pl.BlockSpec((1, MEM_WIDTH, tq), lambda b, i: (b, 0, i)),
                  pl.BlockSpec((1, N_MEM, MEM_WIDTH), lambda b, i: (b, 0, 0)),
                  pl.BlockSpec((1, MEM_HEADS * VT_ROWS, N_MEM), lambda b, i: (b, 0, 0))],
        out_specs=pl.BlockSpec((1, tq, MEM_WIDTH), lambda b, i: (b, i, 0)),
        out_shape=jax.ShapeDtypeStruct((nb, seq, MEM_WIDTH), BF16),
        compiler_params=_cparams(2),
        name="memory_attention",
    )(mqt, mk, mvt)


def _post_attn_kernel(x_ref, att_ref, memo_ref, wa_ref, wm_ref, g_ref, b_ref, wr_ref, br_ref,
                      x1_ref, x1p_ref, idx_ref, rank_ref, gate_ref, cnt_ref, carry_ref):
    t = pl.program_id(0)

    @pl.when(t == 0)
    def _():
        carry_ref[...] = jnp.zeros_like(carry_ref)

    tm = x_ref.shape[0]
    mix = (jnp.dot(att_ref[...], wa_ref[...], preferred_element_type=F32)
           + jnp.dot(memo_ref[...], wm_ref[...], preferred_element_type=F32))
    x1 = _layer_norm(DEEPNORM_ALPHA * x_ref[...] + mix, g_ref[...], b_ref[...])
    x1_ref[...] = x1
    x1b = x1.astype(BF16)
    bits = pltpu.bitcast(x1b.astype(F32), jnp.uint32)
    half = D_MODEL // 2
    x1p_ref[...] = (bits[:, :half] >> 16) | bits[:, half:]
    logits = jnp.dot(x1b, wr_ref[...], preferred_element_type=F32) + br_ref[...]
    lane = lax.broadcasted_iota(jnp.int32, (tm, LANES), 1)
    work = jnp.where(lane < N_EXPERTS, logits, -jnp.inf)
    idxs, vals = [], []
    onehot = jnp.zeros((tm, LANES), F32)
    for _ in range(TOP_K):
        best = jnp.max(work, axis=1, keepdims=True)
        where_best = jnp.min(jnp.where(work == best, lane, LANES), axis=1, keepdims=True)
        hit = lane == where_best
        onehot = jnp.where(hit, 1.0, onehot)
        work = jnp.where(hit, -jnp.inf, work)
        idxs.append(where_best)
        vals.append(best)
    exps = [jnp.exp(v - vals[0]) for v in vals]
    denom = exps[0] + exps[1] + exps[2] + exps[3]
    row = lax.broadcasted_iota(jnp.int32, (tm, tm), 0)
    col = lax.broadcasted_iota(jnp.int32, (tm, tm), 1)
    strict = jnp.where(row > col, 1.0, 0.0).astype(BF16)
    before = jnp.dot(strict, onehot.astype(BF16), preferred_element_type=F32) + carry_ref[...]
    idx_out = jnp.zeros((tm, LANES), jnp.int32)
    rank_out = jnp.zeros((tm, LANES), jnp.int32)
    gate_out = jnp.zeros((tm, LANES), F32)
    for r in range(TOP_K):
        rank_r = jnp.sum(jnp.where(lane == idxs[r], before, 0.0), axis=1, keepdims=True)
        idx_out = jnp.where(lane == r, idxs[r], idx_out)
        rank_out = jnp.where(lane == r, rank_r.astype(jnp.int32), rank_out)
        gate_out = jnp.where(lane == r, exps[r] / denom, gate_out)
    idx_ref[...] = idx_out
    rank_ref[...] = rank_out
    gate_ref[...] = gate_out
    total = carry_ref[...] + jnp.sum(onehot, axis=0, keepdims=True)
    carry_ref[...] = total
    cnt_ref[...] = jnp.broadcast_to(total, cnt_ref.shape)


def _post_attn(x2d, att, memo, wa, wm, g, b, wr, br):
    t_tokens = x2d.shape[0]
    tm = PROJ_TM
    row_spec = lambda width: pl.BlockSpec((tm, width), lambda i: (i, 0))
    full = lambda a: pl.BlockSpec(a.shape, lambda i: (0, 0))
    return pl.pallas_call(
        _post_attn_kernel,
        grid=(t_tokens // tm,),
        in_specs=[row_spec(D_MODEL), row_spec(att.shape[1]), row_spec(MEM_WIDTH),
                  full(wa), full(wm), full(g), full(b), full(wr), full(br)],
        out_specs=[row_spec(D_MODEL), row_spec(D_MODEL // 2), row_spec(LANES), row_spec(LANES),
                   row_spec(LANES), pl.BlockSpec((8, LANES), lambda i: (0, 0))],
        out_shape=[jax.ShapeDtypeStruct((t_tokens, D_MODEL), F32),
                   jax.ShapeDtypeStruct((t_tokens, D_MODEL // 2), jnp.uint32),
                   jax.ShapeDtypeStruct((t_tokens, LANES), jnp.int32),
                   jax.ShapeDtypeStruct((t_tokens, LANES), jnp.int32),
                   jax.ShapeDtypeStruct((t_tokens, LANES), F32),
                   jax.ShapeDtypeStruct((8, LANES), F32)],
        scratch_shapes=[pltpu.VMEM((1, LANES), F32)],
        compiler_params=_cparams(1),
        name="outproj_ln_router",
    )(x2d, att, memo, wa, wm, g, b, wr, br)


def _expert_kernel(be_ref, nu_ref, x_ref, wgu_ref, bgu_ref, wdn_ref, bdn_ref, o_ref,
                   wgu_bf, wdn_bf, h_ref, *, n_chunks):
    blk = pl.program_id(0)

    @pl.when((blk == 0) | (be_ref[blk] != be_ref[jnp.maximum(blk - 1, 0)]))
    def _():
        wgu_bf[...] = wgu_ref[0, 0].astype(BF16)
        wdn_bf[...] = wdn_ref[0, 0].astype(BF16)

    @pl.when(blk < nu_ref[0])
    def _():
        words = x_ref[...]
        x = jnp.concatenate(
            [pltpu.bitcast(words << 16, F32), pltpu.bitcast(words & jnp.uint32(0xFFFF0000), F32)],
            axis=1).astype(BF16)
        cw = D_EXPERT // n_chunks
        for c in range(n_chunks):
            gs = slice(c * cw, (c + 1) * cw)
            us = slice(D_EXPERT + c * cw, D_EXPERT + (c + 1) * cw)
            g = jnp.dot(x, wgu_bf[:, gs], preferred_element_type=F32) + bgu_ref[0, 0, :, gs]
            u = jnp.dot(x, wgu_bf[:, us], preferred_element_type=F32) + bgu_ref[0, 0, :, us]
            g = jnp.minimum(g, SWIGLU_LIMIT)
            u = jnp.clip(u, -SWIGLU_LIMIT, SWIGLU_LIMIT)
            h_ref[:, gs] = ((u + 1.0) * (g * jax.nn.sigmoid(SWIGLU_ALPHA * g))).astype(BF16)
        o_ref[...] = jnp.dot(h_ref[...], wdn_bf[...], preferred_element_type=F32) + bdn_ref[0, 0]

    @pl.when(blk >= nu_ref[0])
    def _():
        o_ref[...] = jnp.zeros_like(o_ref)


def _experts(block_expert, n_used, xb, layer, wgu, bgu, wdn, bdn):
    p_rows = xb.shape[0]
    tm = MOE_TM
    grid_spec = pltpu.PrefetchScalarGridSpec(
        num_scalar_prefetch=2,
        grid=(p_rows // tm,),
        in_specs=[pl.BlockSpec((tm, D_MODEL // 2), lambda i, be, nu: (i, 0)),
                  pl.BlockSpec((1, 1, D_MODEL, 2 * D_EXPERT), lambda i, be, nu: (layer, be[i], 0, 0)),
                  pl.BlockSpec((1, 1, 1, 2 * D_EXPERT), lambda i, be, nu: (layer, be[i], 0, 0)),
                  pl.BlockSpec((1, 1, D_EXPERT, D_MODEL), lambda i, be, nu: (layer, be[i], 0, 0)),
                  pl.BlockSpec((1, 1, 1, D_MODEL), lambda i, be, nu: (layer, be[i], 0, 0))],
        out_specs=pl.BlockSpec((tm, D_MODEL), lambda i, be, nu: (i, 0)),
        scratch_shapes=[pltpu.VMEM((D_MODEL, 2 * D_EXPERT), BF16),
                        pltpu.VMEM((D_EXPERT, D_MODEL), BF16),
                        pltpu.VMEM((tm, D_EXPERT), BF16)],
    )
    return pl.pallas_call(
        functools.partial(_expert_kernel, n_chunks=4),
        grid_spec=grid_spec,
        out_shape=jax.ShapeDtypeStruct((p_rows, D_MODEL), F32),
        compiler_params=pltpu.CompilerParams(dimension_semantics=("arbitrary",),
                                             vmem_limit_bytes=EXPERT_VMEM_LIMIT),
        name="experts",
    )(block_expert, n_used, xb, wgu, bgu, wdn, bdn)


def _sc_move_rows(table_hbm, idx_v, out_hbm, base, n_chunks, rows_v, gsem, wsem):
    def gather(c, slot):
        off = pl.multiple_of(c * SC_CHUNK, SC_CHUNK)
        return pltpu.make_async_copy(table_hbm.at[idx_v.at[pl.ds(off, SC_CHUNK)]],
                                     rows_v.at[slot], gsem.at[slot])

    def put(c, slot):
        off = pl.multiple_of(c * SC_CHUNK, SC_CHUNK)
        return pltpu.make_async_copy(rows_v.at[slot], out_hbm.at[pl.ds(base + off, SC_CHUNK)],
                                     wsem.at[slot])

    gather(0, 0).start()
    gather(1, 1).start()

    @pl.loop(0, n_chunks, step=2)
    def _(c):
        for slot in range(2):
            gather(c + slot, slot).wait()
            put(c + slot, slot).start()
        for slot in range(2):
            put(c + slot, slot).wait()

            @pl.when(c + 2 + slot < n_chunks)
            def _():
                gather(c + 2 + slot, slot).start()


def _sc_gather_rows(table, idx):
    n_idx = idx.shape[0]
    width = table.shape[1]
    per_worker = n_idx // SC_WORKERS
    n_chunks = per_worker // SC_CHUNK
    mesh = plsc.VectorSubcoreMesh(core_axis_name="c", subcore_axis_name="s",
                                  num_cores=SC_CORES, num_subcores=SC_SUBCORES)

    def body(table_hbm, idx_hbm, out_hbm, idx_v, rows_v, gsem, wsem):
        wid = lax.axis_index("s") * SC_CORES + lax.axis_index("c")
        base = wid * per_worker
        pltpu.sync_copy(idx_hbm.at[pl.ds(base, per_worker)], idx_v)
        _sc_move_rows(table_hbm, idx_v, out_hbm, base, n_chunks, rows_v, gsem, wsem)

    return pl.kernel(
        body,
        out_type=jax.ShapeDtypeStruct((n_idx, width), table.dtype),
        mesh=mesh,
        scratch_types=[pltpu.VMEM((per_worker,), jnp.int32),
                       pltpu.VMEM((2, SC_CHUNK, width), table.dtype),
                       pltpu.SemaphoreType.DMA((2,)),
                       pltpu.SemaphoreType.DMA((2,))],
        name="sc_gather_rows",
    )(table, idx)


def _sc_dispatch_rows(table, dest, p_rows):
    n_tok, width = table.shape
    n_pairs = dest.shape[0]
    per_worker = p_rows // SC_WORKERS
    n_chunks = per_worker // SC_CHUNK
    n_scan = n_pairs // SC_SCAN
    mesh = plsc.VectorSubcoreMesh(core_axis_name="c", subcore_axis_name="s",
                                  num_cores=SC_CORES, num_subcores=SC_SUBCORES)

    def body(table_hbm, dest_hbm, out_hbm, tok_v, dest_v, rows_v, gsem, wsem):
        wid = lax.axis_index("s") * SC_CORES + lax.axis_index("c")
        base = wid * per_worker
        lane = lax.iota(jnp.int32, SC_LANES)

        @pl.loop(0, per_worker // SC_LANES)
        def _(i):
            off = pl.multiple_of(i * SC_LANES, SC_LANES)
            tok_v[pl.ds(off, SC_LANES)] = lax.rem(base + off + lane, n_tok)

        @pl.loop(0, n_scan)
        def _(g):
            goff = pl.multiple_of(g * SC_SCAN, SC_SCAN)
            pltpu.sync_copy(dest_hbm.at[pl.ds(goff, SC_SCAN)], dest_v)

            @pl.loop(0, SC_SCAN // SC_LANES)
            def _(i):
                off = pl.multiple_of(i * SC_LANES, SC_LANES)
                local = dest_v[pl.ds(off, SC_LANES)] - base
                mine = (local >= 0) & (local < per_worker)
                pair = goff + off + lane
                plsc.store_scatter(tok_v, [jnp.where(mine, local, 0)], pair // TOP_K, mask=mine)

        _sc_move_rows(table_hbm, tok_v, out_hbm, base, n_chunks, rows_v, gsem, wsem)

    return pl.kernel(
        body,
        out_type=jax.ShapeDtypeStruct((p_rows, width), table.dtype),
        mesh=mesh,
        scratch_types=[pltpu.VMEM((per_worker,), jnp.int32),
                       pltpu.VMEM((SC_SCAN,), jnp.int32),
                       pltpu.VMEM((2, SC_CHUNK, width), table.dtype),
                       pltpu.SemaphoreType.DMA((2,)),
                       pltpu.SemaphoreType.DMA((2,))],
        compiler_params=pltpu.CompilerParams(needs_layout_passes=False),
        name="sc_dispatch_rows",
    )(table, dest)


def _combine_kernel(x1_ref, yg_ref, gate_ref, g_ref, b_ref, o_ref):
    gates = gate_ref[...]
    ffn = yg_ref[0] * gates[:, 0:1]
    for r in range(1, TOP_K):
        ffn = ffn + yg_ref[r] * gates[:, r:r + 1]
    o_ref[...] = _layer_norm(DEEPNORM_ALPHA * x1_ref[...] + ffn, g_ref[...], b_ref[...])


def _combine(x1, yg, gates, g, b):
    t_tokens = x1.shape[0]
    tm = 256
    return pl.pallas_call(
        _combine_kernel,
        grid=(t_tokens // tm,),
        in_specs=[pl.BlockSpec((tm, D_MODEL), lambda i: (i, 0)),
                  pl.BlockSpec((TOP_K, tm, D_MODEL), lambda i: (0, i, 0)),
                  pl.BlockSpec((tm, LANES), lambda i: (i, 0)),
                  pl.BlockSpec((1, D_MODEL), lambda i: (0, 0)),
                  pl.BlockSpec((1, D_MODEL), lambda i: (0, 0))],
        out_specs=pl.BlockSpec((tm, D_MODEL), lambda i: (i, 0)),
        out_shape=jax.ShapeDtypeStruct((t_tokens, D_MODEL), F32),
        compiler_params=_cparams(1),
        name="combine_ln",
    )(x1, yg, gates, g, b)


def _moe(x1, x1p, idx, rank, gates, counts, layer, wgu, bgu, wdn, bdn, g, b):
    t_tokens = x1.shape[0]
    tm = MOE_TM
    counts = counts[0, :N_EXPERTS].astype(jnp.int32)
    padded = ((counts + tm - 1) // tm) * tm
    pend = jnp.cumsum(padded)
    pstart = pend - padded
    idx4 = idx[:, :TOP_K]
    dest = (pstart[idx4] + rank[:, :TOP_K]).reshape(-1)
    n_blocks = (t_tokens * TOP_K) // tm + N_EXPERTS
    p_rows = n_blocks * tm
    block_start = jnp.arange(n_blocks, dtype=jnp.int32) * tm
    block_expert = jnp.minimum(
        jnp.sum((pend[None, :] <= block_start[:, None]).astype(jnp.int32), axis=1), N_EXPERTS - 1)
    n_used = (pend[-1:] // tm).astype(jnp.int32)
    xb = _sc_dispatch_rows(x1p, dest, p_rows)
    y = _experts(block_expert, n_used, xb, layer, wgu, bgu, wdn, bdn)
    dest_by_choice = dest.reshape(t_tokens, TOP_K).T.reshape(-1)
    yg = _sc_gather_rows(y, dest_by_choice).reshape(TOP_K, t_tokens, D_MODEL)
    return _combine(x1, yg, gates, g, b)


def _row(v, width=None):
    v = v.astype(F32).reshape(1, -1)
    if width is not None and v.shape[1] < width:
        v = jnp.pad(v, ((0, 0), (0, width - v.shape[1])))
    return v


def _pad_cols(w, width):
    return jnp.pad(w, ((0, 0), (0, width - w.shape[1])))


def kernel(x, mem, positions, a_w_in, a_b_f, a_w_out, b_w_in, b_g_q, b_w_uq, b_w_out,
           kv_w_dkv, kv_g, kv_w_ukv, mem_w_kv, ln_g, ln_b,
           moe_w_r, moe_b_r, moe_w_gu, moe_b_gu, moe_w_dn, moe_b_dn):
    nb, seq, d = x.shape
    t_tokens = nb * seq
    n_a = a_w_in.shape[0]
    x2d = x.reshape(t_tokens, d)
    mem2d = mem.reshape(nb * N_MEM, d)
    pos2d = positions.reshape(t_tokens, 1)
    half = QK_ROPE // 2
    inv_freq = ROPE_THETA ** (-jnp.arange(half, dtype=F32) * 2.0 / QK_ROPE)
    invf = jnp.zeros((1, LANES), F32)
    invf = invf.at[0, QK_NOPE:QK_NOPE + half].set(inv_freq)
    invf = invf.at[0, QK_NOPE + half:QK_NOPE + QK_ROPE].set(inv_freq)

    shared_kv = None
    for l in range(DEPTH):
        mk, mvt = _mem_proj(mem2d, mem_w_kv[l].astype(BF16), nb)
        if l < n_a:
            w_in = a_w_in[l]
            w = jnp.concatenate([w_in[:, :3 * FOX_WIDTH],
                                 _pad_cols(w_in[:, 3 * FOX_WIDTH:3 * FOX_WIDTH + FOX_HEADS], LANES),
                                 w_in[:, 3 * FOX_WIDTH + FOX_HEADS:]], axis=1).astype(BF16)
            qt, k, vt, mqt, cum, cumt, cfirst, clast, knorm = _fox_proj(
                x2d, w, _row(a_b_f[l], LANES), seq)
            stats = [s.reshape(-1, LANES) for s in (cfirst, clast, knorm)]
            att = _causal_attention(qt, k.reshape(nb, seq, -1), vt,
                                    [cum.reshape(nb, seq, LANES), cumt] + stats, slab=False)
            w_out = a_w_out[l]
        else:
            bl = l - n_a
            w = jnp.concatenate([b_w_in[bl], kv_w_dkv[:, :KV_LORA],
                                 jnp.zeros((d, QK_NOPE), F32), kv_w_dkv[:, KV_LORA:],
                                 jnp.zeros((d, LANES - QK_NOPE - QK_ROPE), F32)], axis=1).astype(BF16)
            wuq = b_w_uq[bl].reshape(Q_LORA, MLA_HEADS, QK_NOPE + QK_ROPE)
            wuq = jnp.pad(wuq, ((0, 0), (0, 0), (0, MLA_SLAB - QK_NOPE - QK_ROPE)))
            wuq = wuq.reshape(Q_LORA, MLA_HEADS * MLA_SLAB).astype(BF16)
            wukv = kv_w_ukv.reshape(KV_LORA, MLA_HEADS, QK_NOPE + V_DIM)
            wk = jnp.pad(wukv[:, :, :QK_NOPE], ((0, 0), (0, 0), (0, MLA_SLAB - QK_NOPE)))
            wk = wk.reshape(KV_LORA, MLA_HEADS * MLA_SLAB).astype(BF16)
            wv = wukv[:, :, QK_NOPE:].reshape(KV_LORA, MLA_V_WIDTH).astype(BF16)
            qt, mqt, k_new, vt_new = _mla_proj(x2d, pos2d, invf, w, _row(b_g_q[bl]), wuq,
                                              _row(kv_g), wk, wv, seq)
            if shared_kv is None:
                shared_kv = (k_new.reshape(nb, seq, -1), vt_new)
            att = _causal_attention(qt, shared_kv[0], shared_kv[1], None, slab=True)
            w_out = b_w_out[bl]
        memo = _memory_attention(mqt, mk, mvt)
        n_att = w_out.shape[0] - MEM_WIDTH
        x1, x1p, idx, rank, gates, counts = _post_attn(
            x2d, att.reshape(t_tokens, -1), memo.reshape(t_tokens, MEM_WIDTH),
            w_out[:n_att].astype(BF16), w_out[n_att:].astype(BF16),
            _row(ln_g[l, 0]), _row(ln_b[l, 0]),
            _pad_cols(moe_w_r[l], LANES).astype(BF16), _row(moe_b_r[l], LANES))
        x2d = _moe(x1, x1p, idx, rank, gates, counts, l,
                   moe_w_gu, moe_b_gu.reshape(DEPTH, N_EXPERTS, 1, -1),
                   moe_w_dn, moe_b_dn.reshape(DEPTH, N_EXPERTS, 1, -1),
                   _row(ln_g[l, 1]), _row(ln_b[l, 1]))
    return x2d.reshape(nb, seq, d)
```

```python
import functools
import math

import jax
import jax.numpy as jnp
from jax import lax
from jax.experimental import pallas as pl
from jax.experimental.pallas import tpu as pltpu
from jax.experimental.pallas import tpu_sc as plsc

F32 = jnp.float32
BF16 = jnp.bfloat16

D_MODEL = 1024
DEPTH = 2
N_MEM = 256
HEAD_DIM = 64
FOX_HEADS = 12
MEM_HEADS = 4
MLA_HEADS = 12
Q_LORA = 384
KV_LORA = 256
QK_NOPE = 64
QK_ROPE = 32
V_DIM = 64
ROPE_THETA = 10000.0
N_EXPERTS = 32
TOP_K = 4
D_EXPERT = D_MODEL
SWIGLU_LIMIT = 7.0
SWIGLU_ALPHA = 1.702
LN_EPS = 1e-5
RMS_EPS = 1e-6
NEG_INF = -1e30
DEEPNORM_ALPHA = (2 * DEPTH) ** 0.25
FOX_WIDTH = FOX_HEADS * HEAD_DIM
MEM_WIDTH = MEM_HEADS * HEAD_DIM
MLA_V_WIDTH = MLA_HEADS * V_DIM

LANES = 128
LOG2E = math.log2(math.e)
VMEM_LIMIT = 48 * 1024 * 1024
EXPERT_VMEM_LIMIT = 56 * 1024 * 1024

PROJ_TM = 512
ATT_TQ = 512
FOX_TQ = 512
MOE_TM = 512
MLA_SLAB = LANES
VT_ROWS = HEAD_DIM + 16
SC_CORES = 2
SC_SUBCORES = 16
SC_WORKERS = SC_CORES * SC_SUBCORES
SC_CHUNK = 32
SC_LANES = 16
SC_SCAN = 8192


def _cparams(n_axes):
    return pltpu.CompilerParams(dimension_semantics=("arbitrary",) * n_axes,
                                vmem_limit_bytes=VMEM_LIMIT)


def _split3(x):
    hi = x.astype(BF16)
    r1 = x - hi.astype(F32)
    mid = r1.astype(BF16)
    lo = (r1 - mid.astype(F32)).astype(BF16)
    return hi, mid, lo


def _pack_bf16_pairs(v):
    bits = pltpu.bitcast(v.astype(F32), jnp.uint32)
    half = v.shape[1] // 2
    return (bits[:, :half] >> 16) | bits[:, half:]


def _unpack_bf16_pairs(words):
    return jnp.concatenate([pltpu.bitcast(words << 16, F32),
                            pltpu.bitcast(words & jnp.uint32(0xFFFF0000), F32)], axis=1)


def _layer_norm(y, g, b):
    mu = jnp.mean(y, axis=-1, keepdims=True)
    yc = y - mu
    var = jnp.mean(yc * yc, axis=-1, keepdims=True)
    return yc * lax.rsqrt(var + LN_EPS) * g + b


def _rms_norm(y, g):
    return y * lax.rsqrt(jnp.mean(y * y, axis=-1, keepdims=True) + RMS_EPS) * g


def _store_transposed(dst_ref, val):
    for s in range(val.shape[1] // LANES):
        sl = slice(s * LANES, (s + 1) * LANES)
        dst_ref[0, sl, :] = val[:, sl].T.astype(dst_ref.dtype)


def _store_values_transposed(vt_ref, val):
    tm = val.shape[0]
    ones = jnp.ones((VT_ROWS - HEAD_DIM, tm), vt_ref.dtype)
    for s in range(val.shape[1] // LANES):
        pair_t = val[:, s * LANES:(s + 1) * LANES].T.astype(vt_ref.dtype)
        for h in range(2):
            r0 = (2 * s + h) * VT_ROWS
            vt_ref[0, r0:r0 + HEAD_DIM, :] = pair_t[h * HEAD_DIM:(h + 1) * HEAD_DIM]
            vt_ref[0, r0 + HEAD_DIM:r0 + VT_ROWS, :] = ones


def _fox_proj_kernel(x_ref, w_ref, bf_ref, seg_ref, qt_ref, k_ref, vt_ref, mqt_ref, cum_ref,
                     cumt_ref, cfirst_ref, clast_ref, knorm_ref, carry_ref, knmax_ref, *,
                     tiles_per_batch):
    t = pl.program_id(0)

    @pl.when(t % tiles_per_batch == 0)
    def _():
        carry_ref[...] = jnp.zeros_like(carry_ref)
        knmax_ref[...] = jnp.zeros_like(knmax_ref)

    tm = x_ref.shape[0]
    proj = jnp.dot(x_ref[...].astype(BF16), w_ref[...], preferred_element_type=F32)
    qscale = HEAD_DIM ** -0.5 * LOG2E
    _store_transposed(qt_ref, proj[:, :FOX_WIDTH] * qscale)
    kb = proj[:, FOX_WIDTH:2 * FOX_WIDTH].astype(BF16)
    k_ref[...] = kb
    _store_values_transposed(vt_ref, proj[:, 2 * FOX_WIDTH:3 * FOX_WIDTH])
    kf = kb.astype(F32)
    ksq = jnp.dot((kf * kf).astype(BF16), seg_ref[...], preferred_element_type=F32)
    n_sub = tm // FOX_TQ
    for sub in range(n_sub):
        tile_max = jnp.sqrt(jnp.max(ksq[sub * FOX_TQ:(sub + 1) * FOX_TQ], axis=0, keepdims=True))
        knmax_ref[...] = jnp.maximum(knmax_ref[...], tile_max)
        knorm_ref[sub] = knmax_ref[...]
    f = proj[:, 3 * FOX_WIDTH:3 * FOX_WIDTH + LANES] + bf_ref[...]
    _store_transposed(mqt_ref, proj[:, 3 * FOX_WIDTH + LANES:] * qscale)
    log_f = jnp.minimum(f, 0.0) - jnp.log1p(jnp.exp(-jnp.abs(f)))
    row = lax.broadcasted_iota(jnp.int32, (tm, tm), 0)
    col = lax.broadcasted_iota(jnp.int32, (tm, tm), 1)
    tri = jnp.where(row >= col, 1.0, 0.0).astype(BF16)
    hi, mid, lo = _split3(log_f)
    cum = (jnp.dot(tri, hi, preferred_element_type=F32)
           + jnp.dot(tri, mid, preferred_element_type=F32)
           + jnp.dot(tri, lo, preferred_element_type=F32)) + carry_ref[...]
    carry_ref[...] = cum[tm - 1:tm, :]
    cum2 = cum * LOG2E
    cum_ref[...] = cum2
    cumt_ref[0] = cum2.T[:16, :]
    for sub in range(n_sub):
        cfirst_ref[sub] = cum2[sub * FOX_TQ:sub * FOX_TQ + 1, :]
        clast_ref[sub] = cum2[(sub + 1) * FOX_TQ - 1:(sub + 1) * FOX_TQ, :]


def _fox_proj(x2d, w, bf, seq):
    t_tokens = x2d.shape[0]
    tm = PROJ_TM
    nb = t_tokens // seq
    n = w.shape[1]
    tiles_per_batch = seq // tm
    n_tiles = t_tokens // tm
    seg = (jnp.arange(FOX_WIDTH)[:, None] // HEAD_DIM == jnp.arange(LANES)[None, :]).astype(BF16)
    row_spec = lambda width: pl.BlockSpec((tm, width), lambda i: (i, 0))
    t_spec = lambda rows: pl.BlockSpec(
        (1, rows, tm), lambda i: (i // tiles_per_batch, 0, i % tiles_per_batch))
    n_sub = tm // FOX_TQ
    stat_spec = pl.BlockSpec((n_sub, 1, LANES), lambda i: (i, 0, 0))
    stat_shape = jax.ShapeDtypeStruct((n_tiles * n_sub, 1, LANES), F32)
    return pl.pallas_call(
        functools.partial(_fox_proj_kernel, tiles_per_batch=tiles_per_batch),
        grid=(n_tiles,),
        in_specs=[row_spec(D_MODEL),
                  pl.BlockSpec((D_MODEL, n), lambda i: (0, 0)),
                  pl.BlockSpec((1, LANES), lambda i: (0, 0)),
                  pl.BlockSpec((FOX_WIDTH, LANES), lambda i: (0, 0))],
        out_specs=[t_spec(FOX_WIDTH), row_spec(FOX_WIDTH), t_spec(FOX_HEADS * VT_ROWS),
                   t_spec(MEM_WIDTH), row_spec(LANES), t_spec(16),
                   stat_spec, stat_spec, stat_spec],
        out_shape=[jax.ShapeDtypeStruct((nb, FOX_WIDTH, seq), BF16),
                   jax.ShapeDtypeStruct((t_tokens, FOX_WIDTH), BF16),
                   jax.ShapeDtypeStruct((nb, FOX_HEADS * VT_ROWS, seq), BF16),
                   jax.ShapeDtypeStruct((nb, MEM_WIDTH, seq), BF16),
                   jax.ShapeDtypeStruct((t_tokens, LANES), F32),
                   jax.ShapeDtypeStruct((nb, 16, seq), F32),
                   stat_shape, stat_shape, stat_shape],
        scratch_shapes=[pltpu.VMEM((1, LANES), F32), pltpu.VMEM((1, LANES), F32)],
        compiler_params=_cparams(1),
        name="fox_proj",
    )(x2d, w, bf, seg)


def _rope_tables(pos_ref, invf_ref):
    ang = pos_ref[...].astype(F32) * invf_ref[...]
    cos = jnp.cos(ang)
    sin = jnp.sin(ang)
    lane = lax.broadcasted_iota(jnp.int32, ang.shape, 1)
    half = QK_ROPE // 2
    in_x1 = (lane >= QK_NOPE) & (lane < QK_NOPE + half)
    in_x2 = (lane >= QK_NOPE + half) & (lane < QK_NOPE + QK_ROPE)
    c_tab = jnp.where(lane < QK_NOPE, 1.0, jnp.where(in_x1 | in_x2, cos, 0.0))
    s_up = jnp.where(in_x2, sin, 0.0)
    s_dn = jnp.where(in_x1, -sin, 0.0)
    return c_tab, s_up, s_dn


def _rope_slab(slab, tables):
    c_tab, s_up, s_dn = tables
    half = QK_ROPE // 2
    up = pltpu.roll(slab, half, 1)
    dn = pltpu.roll(slab, LANES - half, 1)
    return slab * c_tab + up * s_up + dn * s_dn


def _mla_proj_kernel(x_ref, pos_ref, invf_ref, w_ref, gq_ref, wuq_ref, gkv_ref, wk_ref, wv_ref,
                     qt_ref, mqt_ref, k_ref, vt_ref):
    proj = jnp.dot(x_ref[...].astype(BF16), w_ref[...], preferred_element_type=F32)
    tables = _rope_tables(pos_ref, invf_ref)
    c_q = _rms_norm(proj[:, :Q_LORA], gq_ref[...])
    _store_transposed(mqt_ref, proj[:, Q_LORA:Q_LORA + MEM_WIDTH] * (HEAD_DIM ** -0.5 * LOG2E))
    kv_off = Q_LORA + MEM_WIDTH
    c_kv = _rms_norm(proj[:, kv_off:kv_off + KV_LORA], gkv_ref[...])
    kr = _rope_slab(proj[:, kv_off + KV_LORA:], tables)
    q = jnp.dot(c_q.astype(BF16), wuq_ref[...], preferred_element_type=F32)
    kn = jnp.dot(c_kv.astype(BF16), wk_ref[...], preferred_element_type=F32)
    qscale = (QK_NOPE + QK_ROPE) ** -0.5 * LOG2E
    for h in range(MLA_HEADS):
        sl = slice(h * MLA_SLAB, (h + 1) * MLA_SLAB)
        qt_ref[0, sl, :] = (_rope_slab(q[:, sl], tables) * qscale).T.astype(BF16)
        k_ref[:, sl] = (kn[:, sl] + kr).astype(BF16)
    _store_values_transposed(vt_ref, jnp.dot(c_kv.astype(BF16), wv_ref[...],
                                             preferred_element_type=F32))


def _mla_proj(x2d, pos2d, invf, w, gq, wuq, gkv, wk, wv, seq):
    t_tokens = x2d.shape[0]
    tm = PROJ_TM
    nb = t_tokens // seq
    tiles_per_batch = seq // tm
    row_spec = lambda width: pl.BlockSpec((tm, width), lambda i: (i, 0))
    t_spec = lambda rows: pl.BlockSpec(
        (1, rows, tm), lambda i: (i // tiles_per_batch, 0, i % tiles_per_batch))
    full = lambda a: pl.BlockSpec(a.shape, lambda i: (0, 0))
    slabs = MLA_HEADS * MLA_SLAB
    return pl.pallas_call(
        _mla_proj_kernel,
        grid=(t_tokens // tm,),
        in_specs=[row_spec(D_MODEL), row_spec(1), full(invf), full(w), full(gq), full(wuq),
                  full(gkv), full(wk), full(wv)],
        out_specs=[t_spec(slabs), t_spec(MEM_WIDTH), row_spec(slabs),
                   t_spec(MLA_HEADS * VT_ROWS)],
        out_shape=[jax.ShapeDtypeStruct((nb, slabs, seq), BF16),
                   jax.ShapeDtypeStruct((nb, MEM_WIDTH, seq), BF16),
                   jax.ShapeDtypeStruct((t_tokens, slabs), BF16),
                   jax.ShapeDtypeStruct((nb, MLA_HEADS * VT_ROWS, seq), BF16)],
        compiler_params=_cparams(1),
        name="mla_proj",
    )(x2d, pos2d, invf, w, gq, wuq, gkv, wk, wv)


SKIP_LOG2 = -160.0
NORM_SLACK = 1.02


def _causal_attn_kernel(*refs, tq, n_tiles, fox, slab, long_body):
    if fox:
        (cf_ref, cl_ref, kn_ref, qt_ref, k_ref, vt_ref, cq_ref, ck_ref,
         o_ref, m_ref, acc_ref, s_ref) = refs
    else:
        qt_ref, k_ref, vt_ref, o_ref, m_ref, acc_ref, s_ref = refs
    b = pl.program_id(0)
    pair = pl.program_id(1)
    i = pl.program_id(2)
    m_ref[...] = jnp.full(m_ref.shape, NEG_INF, F32)
    acc_ref[...] = jnp.zeros(acc_ref.shape, F32)
    qt = qt_ref[0]
    if slab:
        qth = [qt[:MLA_SLAB], qt[MLA_SLAB:]]
    else:
        rowi = lax.broadcasted_iota(jnp.int32, qt.shape, 0)
        zero = jnp.zeros_like(qt)
        qth = [jnp.where(rowi < HEAD_DIM, qt, zero), jnp.where(rowi >= HEAD_DIM, qt, zero)]
    if fox:
        lane = lax.broadcasted_iota(jnp.int32, (1, LANES), 1)
        cq = [cq_ref[0, pl.ds(2 * pair + h, 1), :] for h in range(2)]

    def logits_to(slot, j):
        off = pl.multiple_of(j * tq, tq)
        kc = k_ref[0, pl.ds(off, tq), :]
        for h in range(2):
            kh = kc[:, h * MLA_SLAB:(h + 1) * MLA_SLAB] if slab else kc
            s = jnp.dot(kh, qth[h], preferred_element_type=F32)
            if fox:
                ck_blk = ck_ref[0, pl.ds(off, tq), :]
                ck = jnp.sum(jnp.where(lane == 2 * pair + h, ck_blk, 0.0), axis=1, keepdims=True)
                s = s + cq[h] - ck
            s_ref[slot, h] = s

    def softmax_pv(slot, j, causal_mask):
        off = pl.multiple_of(j * tq, tq)
        for h in range(2):
            s = s_ref[slot, h]
            if causal_mask is not None:
                s = jnp.where(causal_mask, s, NEG_INF)
            m_prev = m_ref[h]
            m_new = jnp.maximum(m_prev, jnp.max(s, axis=0, keepdims=True))
            alpha = jnp.exp2(m_prev - m_new)
            p = jnp.exp2(s - m_new).astype(BF16)
            vth = vt_ref[0, h * VT_ROWS:(h + 1) * VT_ROWS, pl.ds(off, tq)]
            acc_ref[h] = acc_ref[h] * alpha + jnp.dot(vth, p, preferred_element_type=F32)
            m_ref[h] = m_new

    if fox:
        base = b * n_tiles
        qn = []
        for h in range(2):
            qf = qth[h].astype(F32)
            qn.append(jnp.sqrt(jnp.max(jnp.sum(qf * qf, axis=0, keepdims=True))) * NORM_SLACK)

        top = [2.0 * qn[h] * kn_ref[base + i, 2 * pair + h] + cf_ref[base + i, 2 * pair + h]
               for h in range(2)]

        def live(j):
            ub = [top[h] - cl_ref[base + jnp.maximum(j, 0), 2 * pair + h] for h in range(2)]
            return (j >= 0) & (jnp.maximum(ub[0], ub[1]) > SKIP_LOG2)

        j0 = lax.while_loop(live, lambda j: j - 1, i - 1) + 1
    else:
        j0 = 0
    n_before = i - j0
    odd = (n_before % 2) == 1

    @pl.when(odd)
    def _():
        logits_to(1, j0)
        logits_to(0, j0 + 1)
        softmax_pv(1, j0, None)

    @pl.when(jnp.logical_not(odd))
    def _():
        logits_to(0, j0)

    def two_steps(j):
        logits_to(1, j + 1)
        softmax_pv(0, j, None)
        logits_to(0, j + 2)
        softmax_pv(1, j + 1, None)

    j_even = j0 + (n_before % 2)
    n_two = n_before // 2
    if long_body:
        @pl.when((n_two % 2) == 1)
        def _():
            two_steps(j_even)

        j_quad = j_even + 2 * (n_two % 2)

        def body(t, carry):
            two_steps(j_quad + 4 * t)
            two_steps(j_quad + 4 * t + 2)
            return carry

        lax.fori_loop(0, n_two // 2, body, 0)
    else:
        def body(t, carry):
            two_steps(j_even + 2 * t)
            return carry

        lax.fori_loop(0, n_two, body, 0)
    key = lax.broadcasted_iota(jnp.int32, (tq, tq), 0)
    qry = lax.broadcasted_iota(jnp.int32, (tq, tq), 1)
    softmax_pv(0, i, key <= qry)
    out_t = jnp.concatenate(
        [acc_ref[h, :HEAD_DIM] * (1.0 / acc_ref[h, HEAD_DIM:HEAD_DIM + 1]) for h in range(2)],
        axis=0)
    o_ref[0] = out_t.T.astype(o_ref.dtype)


def _causal_attention(qt, k, vt, fox_args, *, slab):
    nb, seq, _ = k.shape
    fox = fox_args is not None
    tq = FOX_TQ if fox else ATT_TQ
    n_tiles = seq // tq
    n_pairs = FOX_HEADS // 2
    rows = 2 * MLA_SLAB if slab else LANES
    in_specs = [pl.BlockSpec((1, rows, tq), lambda b, p, i: (b, p, i)),
                pl.BlockSpec((1, seq, rows), lambda b, p, i: (b, 0, p)),
                pl.BlockSpec((1, 2 * VT_ROWS, seq), lambda b, p, i: (b, p, 0))]
    args = [qt, k, vt]
    if fox:
        cum, cumt, cfirst, clast, knorm = fox_args
        smem = pl.BlockSpec(memory_space=pltpu.SMEM)
        in_specs = [smem, smem, smem] + in_specs + [
            pl.BlockSpec((1, 16, tq), lambda b, p, i: (b, 0, i)),
            pl.BlockSpec((1, seq, LANES), lambda b, p, i: (b, 0, 0))]
        args = [cfirst, clast, knorm] + args + [cumt, cum]
    return pl.pallas_call(
        functools.partial(_causal_attn_kernel, tq=tq, n_tiles=n_tiles, fox=fox, slab=slab,
                          long_body=not fox),
        grid=(nb, n_pairs, n_tiles),
        in_specs=in_specs,
        out_specs=pl.BlockSpec((1, tq, LANES), lambda b, p, i: (b, i, p)),
        out_shape=jax.ShapeDtypeStruct((nb, seq, n_pairs * LANES), BF16),
        scratch_shapes=[pltpu.VMEM((2, 1, tq), F32), pltpu.VMEM((2, VT_ROWS, tq), F32),
                        pltpu.VMEM((2, 2, tq, tq), F32)],
        compiler_params=_cparams(3),
        name="fox_attention" if fox else "mla_attention",
    )(*args)


def _mem_proj_kernel(mem_ref, w_ref, mk_ref, mvt_ref):
    mkv = jnp.dot(mem_ref[...].astype(BF16), w_ref[...], preferred_element_type=F32)
    mk_ref[0] = mkv[:, :MEM_WIDTH].astype(BF16)
    _store_values_transposed(mvt_ref, mkv[:, MEM_WIDTH:])


def _mem_proj(mem2d, w, nb):
    return pl.pallas_call(
        _mem_proj_kernel,
        grid=(nb,),
        in_specs=[pl.BlockSpec((N_MEM, D_MODEL), lambda b: (b, 0)),
                  pl.BlockSpec(w.shape, lambda b: (0, 0))],
        out_specs=[pl.BlockSpec((1, N_MEM, MEM_WIDTH), lambda b: (b, 0, 0)),
                   pl.BlockSpec((1, MEM_HEADS * VT_ROWS, N_MEM), lambda b: (b, 0, 0))],
        out_shape=[jax.ShapeDtypeStruct((nb, N_MEM, MEM_WIDTH), BF16),
                   jax.ShapeDtypeStruct((nb, MEM_HEADS * VT_ROWS, N_MEM), BF16)],
        compiler_params=_cparams(1),
        name="mem_proj",
    )(mem2d, w)


def _mem_attn_kernel(qt_ref, k_ref, vt_ref, o_ref):
    qt = qt_ref[0]
    kc = k_ref[0]
    for pair in range(MEM_HEADS // 2):
        qt_p = qt[pair * LANES:(pair + 1) * LANES]
        k_p = kc[:, pair * LANES:(pair + 1) * LANES]
        rowi = lax.broadcasted_iota(jnp.int32, qt_p.shape, 0)
        zero = jnp.zeros_like(qt_p)
        outs = []
        for h in range(2):
            mine = (rowi < HEAD_DIM) if h == 0 else (rowi >= HEAD_DIM)
            s = jnp.dot(k_p, jnp.where(mine, qt_p, zero), preferred_element_type=F32)
            p = jnp.exp2(s - jnp.max(s, axis=0, keepdims=True)).astype(BF16)
            r0 = (2 * pair + h) * VT_ROWS
            acc = jnp.dot(vt_ref[0, r0:r0 + VT_ROWS, :], p, preferred_element_type=F32)
            outs.append(acc[:HEAD_DIM] * (1.0 / acc[HEAD_DIM:HEAD_DIM + 1]))
        o_ref[0, :, pair * LANES:(pair + 1) * LANES] = (
            jnp.concatenate(outs, axis=0).T.astype(o_ref.dtype))


def _memory_attention(mqt, mk, mvt):
    nb, _, seq = mqt.shape
    tq = ATT_TQ
    return pl.pallas_call(
        _mem_attn_kernel,
        grid=(nb, seq // tq),
        in_specs=[pl.BlockSpec((1, MEM_WIDTH, tq), lambda b, i: (b, 0, i)),
                  pl.BlockSpec((1, N_MEM, MEM_WIDTH), lambda b, i: (b, 0, 0)),
                  pl.BlockSpec((1, MEM_HEADS * VT_ROWS, N_MEM), lambda b, i: (b, 0, 0))],
        out_specs=pl.BlockSpec((1, tq, MEM_WIDTH), lambda b, i: (b, i, 0)),
        out_shape=jax.ShapeDtypeStruct((nb, seq, MEM_WIDTH), BF16),
        compiler_params=_cparams(2),
        name="memory_attention",
    )(mqt, mk, mvt)


def _post_attn_kernel(x_ref, att_ref, memo_ref, wa_ref, wm_ref, g_ref, b_ref, wr_ref, br_ref,
                      x1_ref, x1p_ref, idx_ref, rank_ref, gate_ref, cnt_ref, carry_ref):
    t = pl.program_id(0)

    @pl.when(t == 0)
    def _():
        carry_ref[...] = jnp.zeros_like(carry_ref)

    tm = x_ref.shape[0]
    mix = (jnp.dot(att_ref[...], wa_ref[...], preferred_element_type=F32)
           + jnp.dot(memo_ref[...], wm_ref[...], preferred_element_type=F32))
    x1 = _layer_norm(DEEPNORM_ALPHA * x_ref[...] + mix, g_ref[...], b_ref[...])
    x1_ref[...] = x1
    x1b = x1.astype(BF16)
    x1p_ref[...] = _pack_bf16_pairs(x1b)
    logits = jnp.dot(x1b, wr_ref[...], preferred_element_type=F32) + br_ref[...]
    lane = lax.broadcasted_iota(jnp.int32, (tm, LANES), 1)
    work = jnp.where(lane < N_EXPERTS, logits, -jnp.inf)
    idxs, vals = [], []
    onehot = jnp.zeros((tm, LANES), F32)
    for _ in range(TOP_K):
        best = jnp.max(work, axis=1, keepdims=True)
        where_best = jnp.min(jnp.where(work == best, lane, LANES), axis=1, keepdims=True)
        hit = lane == where_best
        onehot = jnp.where(hit, 1.0, onehot)
        work = jnp.where(hit, -jnp.inf, work)
        idxs.append(where_best)
        vals.append(best)
    exps = [jnp.exp(v - vals[0]) for v in vals]
    denom = exps[0] + exps[1] + exps[2] + exps[3]
    row = lax.broadcasted_iota(jnp.int32, (tm, tm), 0)
    col = lax.broadcasted_iota(jnp.int32, (tm, tm), 1)
    strict = jnp.where(row > col, 1.0, 0.0).astype(BF16)
    before = jnp.dot(strict, onehot.astype(BF16), preferred_element_type=F32) + carry_ref[...]
    idx_out = jnp.zeros((tm, LANES), jnp.int32)
    rank_out = jnp.zeros((tm, LANES), jnp.int32)
    gate_out = jnp.zeros((tm, LANES), F32)
    for r in range(TOP_K):
        rank_r = jnp.sum(jnp.where(lane == idxs[r], before, 0.0), axis=1, keepdims=True)
        idx_out = jnp.where(lane == r, idxs[r], idx_out)
        rank_out = jnp.where(lane == r, rank_r.astype(jnp.int32), rank_out)
        gate_out = jnp.where(lane == r, exps[r] / denom, gate_out)
    idx_ref[...] = idx_out
    rank_ref[...] = rank_out
    gate_ref[...] = gate_out
    total = carry_ref[...] + jnp.sum(onehot, axis=0, keepdims=True)
    carry_ref[...] = total
    cnt_ref[...] = jnp.broadcast_to(total, cnt_ref.shape)


def _post_attn(x2d, att, memo, wa, wm, g, b, wr, br):
    t_tokens = x2d.shape[0]
    tm = PROJ_TM
    row_spec = lambda width: pl.BlockSpec((tm, width), lambda i: (i, 0))
    full = lambda a: pl.BlockSpec(a.shape, lambda i: (0, 0))
    return pl.pallas_call(
        _post_attn_kernel,
        grid=(t_tokens // tm,),
        in_specs=[row_spec(D_MODEL), row_spec(att.shape[1]), row_spec(MEM_WIDTH),
                  full(wa), full(wm), full(g), full(b), full(wr), full(br)],
        out_specs=[row_spec(D_MODEL), row_spec(D_MODEL // 2), row_spec(LANES), row_spec(LANES),
                   row_spec(LANES), pl.BlockSpec((8, LANES), lambda i: (0, 0))],
        out_shape=[jax.ShapeDtypeStruct((t_tokens, D_MODEL), F32),
                   jax.ShapeDtypeStruct((t_tokens, D_MODEL // 2), jnp.uint32),
                   jax.ShapeDtypeStruct((t_tokens, LANES), jnp.int32),
                   jax.ShapeDtypeStruct((t_tokens, LANES), jnp.int32),
                   jax.ShapeDtypeStruct((t_tokens, LANES), F32),
                   jax.ShapeDtypeStruct((8, LANES), F32)],
        scratch_shapes=[pltpu.VMEM((1, LANES), F32)],
        compiler_params=_cparams(1),
        name="outproj_ln_router",
    )(x2d, att, memo, wa, wm, g, b, wr, br)


def _expert_kernel(be_ref, nu_ref, x_ref, wgu_ref, bgu_ref, wdn_ref, bdn_ref, o_ref,
                   wgu_bf, wdn_bf, h_ref, *, n_chunks):
    blk = pl.program_id(0)

    @pl.when((blk == 0) | (be_ref[blk] != be_ref[jnp.maximum(blk - 1, 0)]))
    def _():
        wgu_bf[...] = wgu_ref[0, 0].astype(BF16)
        wdn_bf[...] = wdn_ref[0, 0].astype(BF16)

    @pl.when(blk < nu_ref[0])
    def _():
        x = _unpack_bf16_pairs(x_ref[...]).astype(BF16)
        cw = D_EXPERT // n_chunks
        for c in range(n_chunks):
            gs = slice(c * cw, (c + 1) * cw)
            us = slice(D_EXPERT + c * cw, D_EXPERT + (c + 1) * cw)
            g = jnp.dot(x, wgu_bf[:, gs], preferred_element_type=F32) + bgu_ref[0, 0, :, gs]
            u = jnp.dot(x, wgu_bf[:, us], preferred_element_type=F32) + bgu_ref[0, 0, :, us]
            g = jnp.minimum(g, SWIGLU_LIMIT)
            u = jnp.clip(u, -SWIGLU_LIMIT, SWIGLU_LIMIT)
            h_ref[:, gs] = ((u + 1.0) * (g * jax.nn.sigmoid(SWIGLU_ALPHA * g))).astype(BF16)
        y = jnp.dot(h_ref[...], wdn_bf[...], preferred_element_type=F32) + bdn_ref[0, 0]
        o_ref[...] = _pack_bf16_pairs(y.astype(BF16))

    @pl.when(blk >= nu_ref[0])
    def _():
        o_ref[...] = jnp.zeros_like(o_ref)


def _experts(block_expert, n_used, xb, layer, wgu, bgu, wdn, bdn):
    p_rows = xb.shape[0]
    tm = MOE_TM
    grid_spec = pltpu.PrefetchScalarGridSpec(
        num_scalar_prefetch=2,
        grid=(p_rows // tm,),
        in_specs=[pl.BlockSpec((tm, D_MODEL // 2), lambda i, be, nu: (i, 0)),
                  pl.BlockSpec((1, 1, D_MODEL, 2 * D_EXPERT), lambda i, be, nu: (layer, be[i], 0, 0)),
                  pl.BlockSpec((1, 1, 1, 2 * D_EXPERT), lambda i, be, nu: (layer, be[i], 0, 0)),
                  pl.BlockSpec((1, 1, D_EXPERT, D_MODEL), lambda i, be, nu: (layer, be[i], 0, 0)),
                  pl.BlockSpec((1, 1, 1, D_MODEL), lambda i, be, nu: (layer, be[i], 0, 0))],
        out_specs=pl.BlockSpec((tm, D_MODEL // 2), lambda i, be, nu: (i, 0)),
        scratch_shapes=[pltpu.VMEM((D_MODEL, 2 * D_EXPERT), BF16),
                        pltpu.VMEM((D_EXPERT, D_MODEL), BF16),
                        pltpu.VMEM((tm, D_EXPERT), BF16)],
    )
    return pl.pallas_call(
        functools.partial(_expert_kernel, n_chunks=4),
        grid_spec=grid_spec,
        out_shape=jax.ShapeDtypeStruct((p_rows, D_MODEL // 2), jnp.uint32),
        compiler_params=pltpu.CompilerParams(dimension_semantics=("arbitrary",),
                                             vmem_limit_bytes=EXPERT_VMEM_LIMIT),
        name="experts",
    )(block_expert, n_used, xb, wgu, bgu, wdn, bdn)


def _sc_move_rows(table_hbm, idx_v, out_hbm, base, n_chunks, rows_v, gsem, wsem):
    def gather(c, slot):
        off = pl.multiple_of(c * SC_CHUNK, SC_CHUNK)
        return pltpu.make_async_copy(table_hbm.at[idx_v.at[pl.ds(off, SC_CHUNK)]],
                                     rows_v.at[slot], gsem.at[slot])

    def put(c, slot):
        off = pl.multiple_of(c * SC_CHUNK, SC_CHUNK)
        return pltpu.make_async_copy(rows_v.at[slot], out_hbm.at[pl.ds(base + off, SC_CHUNK)],
                                     wsem.at[slot])

    gather(0, 0).start()
    gather(1, 1).start()

    @pl.loop(0, n_chunks, step=2)
    def _(c):
        for slot in range(2):
            gather(c + slot, slot).wait()
            put(c + slot, slot).start()
        for slot in range(2):
            put(c + slot, slot).wait()

            @pl.when(c + 2 + slot < n_chunks)
            def _():
                gather(c + 2 + slot, slot).start()


def _sc_gather_rows(table, idx):
    n_idx = idx.shape[0]
    width = table.shape[1]
    per_worker = n_idx // SC_WORKERS
    n_chunks = per_worker // SC_CHUNK
    mesh = plsc.VectorSubcoreMesh(core_axis_name="c", subcore_axis_name="s",
                                  num_cores=SC_CORES, num_subcores=SC_SUBCORES)

    def body(table_hbm, idx_hbm, out_hbm, idx_v, rows_v, gsem, wsem):
        wid = lax.axis_index("s") * SC_CORES + lax.axis_index("c")
        base = wid * per_worker
        pltpu.sync_copy(idx_hbm.at[pl.ds(base, per_worker)], idx_v)
        _sc_move_rows(table_hbm, idx_v, out_hbm, base, n_chunks, rows_v, gsem, wsem)

    return pl.kernel(
        body,
        out_type=jax.ShapeDtypeStruct((n_idx, width), table.dtype),
        mesh=mesh,
        scratch_types=[pltpu.VMEM((per_worker,), jnp.int32),
                       pltpu.VMEM((2, SC_CHUNK, width), table.dtype),
                       pltpu.SemaphoreType.DMA((2,)),
                       pltpu.SemaphoreType.DMA((2,))],
        name="sc_gather_rows",
    )(table, idx)


def _sc_dispatch_rows(table, dest, p_rows):
    n_tok, width = table.shape
    n_pairs = dest.shape[0]
    per_worker = p_rows // SC_WORKERS
    n_chunks = per_worker // SC_CHUNK
    n_scan = n_pairs // SC_SCAN
    mesh = plsc.VectorSubcoreMesh(core_axis_name="c", subcore_axis_name="s",
                                  num_cores=SC_CORES, num_subcores=SC_SUBCORES)

    def body(table_hbm, dest_hbm, out_hbm, tok_v, dest_v, rows_v, gsem, wsem):
        wid = lax.axis_index("s") * SC_CORES + lax.axis_index("c")
        base = wid * per_worker
        lane = lax.iota(jnp.int32, SC_LANES)

        @pl.loop(0, per_worker // SC_LANES)
        def _(i):
            off = pl.multiple_of(i * SC_LANES, SC_LANES)
            tok_v[pl.ds(off, SC_LANES)] = lax.rem(base + off + lane, n_tok)

        @pl.loop(0, n_scan)
        def _(g):
            goff = pl.multiple_of(g * SC_SCAN, SC_SCAN)
            pltpu.sync_copy(dest_hbm.at[pl.ds(goff, SC_SCAN)], dest_v)

            @pl.loop(0, SC_SCAN // SC_LANES)
            def _(i):
                off = pl.multiple_of(i * SC_LANES, SC_LANES)
                local = dest_v[pl.ds(off, SC_LANES)] - base
                mine = (local >= 0) & (local < per_worker)
                pair = goff + off + lane
                plsc.store_scatter(tok_v, [jnp.where(mine, local, 0)], pair // TOP_K, mask=mine)

        _sc_move_rows(table_hbm, tok_v, out_hbm, base, n_chunks, rows_v, gsem, wsem)

    return pl.kernel(
        body,
        out_type=jax.ShapeDtypeStruct((p_rows, width), table.dtype),
        mesh=mesh,
        scratch_types=[pltpu.VMEM((per_worker,), jnp.int32),
                       pltpu.VMEM((SC_SCAN,), jnp.int32),
                       pltpu.VMEM((2, SC_CHUNK, width), table.dtype),
                       pltpu.SemaphoreType.DMA((2,)),
                       pltpu.SemaphoreType.DMA((2,))],
        compiler_params=pltpu.CompilerParams(needs_layout_passes=False),
        name="sc_dispatch_rows",
    )(table, dest)


def _combine_kernel(x1_ref, yg_ref, gate_ref, g_ref, b_ref, o_ref):
    gates = gate_ref[...]
    ffn = _unpack_bf16_pairs(yg_ref[0]) * gates[:, 0:1]
    for r in range(1, TOP_K):
        ffn = ffn + _unpack_bf16_pairs(yg_ref[r]) * gates[:, r:r + 1]
    o_ref[...] = _layer_norm(DEEPNORM_ALPHA * x1_ref[...] + ffn, g_ref[...], b_ref[...])


def _combine(x1, yg, gates, g, b):
    t_tokens = x1.shape[0]
    tm = 256
    return pl.pallas_call(
        _combine_kernel,
        grid=(t_tokens // tm,),
        in_specs=[pl.BlockSpec((tm, D_MODEL), lambda i: (i, 0)),
                  pl.BlockSpec((TOP_K, tm, D_MODEL // 2), lambda i: (0, i, 0)),
                  pl.BlockSpec((tm, LANES), lambda i: (i, 0)),
                  pl.BlockSpec((1, D_MODEL), lambda i: (0, 0)),
                  pl.BlockSpec((1, D_MODEL), lambda i: (0, 0))],
        out_specs=pl.BlockSpec((tm, D_MODEL), lambda i: (i, 0)),
        out_shape=jax.ShapeDtypeStruct((t_tokens, D_MODEL), F32),
        compiler_params=_cparams(1),
        name="combine_ln",
    )(x1, yg, gates, g, b)


def _moe(x1, x1p, idx, rank, gates, counts, layer, wgu, bgu, wdn, bdn, g, b):
    t_tokens = x1.shape[0]
    tm = MOE_TM
    counts = counts[0, :N_EXPERTS].astype(jnp.int32)
    padded = ((counts + tm - 1) // tm) * tm
    pend = jnp.cumsum(padded)
    pstart = pend - padded
    idx4 = idx[:, :TOP_K]
    dest = (pstart[idx4] + rank[:, :TOP_K]).reshape(-1)
    n_blocks = (t_tokens * TOP_K) // tm + N_EXPERTS
    p_rows = n_blocks * tm
    block_start = jnp.arange(n_blocks, dtype=jnp.int32) * tm
    block_expert = jnp.minimum(
        jnp.sum((pend[None, :] <= block_start[:, None]).astype(jnp.int32), axis=1), N_EXPERTS - 1)
    n_used = (pend[-1:] // tm).astype(jnp.int32)
    xb = _sc_dispatch_rows(x1p, dest, p_rows)
    y = _experts(block_expert, n_used, xb, layer, wgu, bgu, wdn, bdn)
    dest_by_choice = dest.reshape(t_tokens, TOP_K).T.reshape(-1)
    yg = _sc_gather_rows(y, dest_by_choice).reshape(TOP_K, t_tokens, D_MODEL // 2)
    return _combine(x1, yg, gates, g, b)


def _row(v, width=None):
    v = v.astype(F32).reshape(1, -1)
    if width is not None and v.shape[1] < width:
        v = jnp.pad(v, ((0, 0), (0, width - v.shape[1])))
    return v


def _pad_cols(w, width):
    return jnp.pad(w, ((0, 0), (0, width - w.shape[1])))


def kernel(x, mem, positions, a_w_in, a_b_f, a_w_out, b_w_in, b_g_q, b_w_uq, b_w_out,
           kv_w_dkv, kv_g, kv_w_ukv, mem_w_kv, ln_g, ln_b,
           moe_w_r, moe_b_r, moe_w_gu, moe_b_gu, moe_w_dn, moe_b_dn):
    nb, seq, d = x.shape
    t_tokens = nb * seq
    n_a = a_w_in.shape[0]
    x2d = x.reshape(t_tokens, d)
    mem2d = mem.reshape(nb * N_MEM, d)
    pos2d = positions.reshape(t_tokens, 1)
    half = QK_ROPE // 2
    inv_freq = ROPE_THETA ** (-jnp.arange(half, dtype=F32) * 2.0 / QK_ROPE)
    invf = jnp.zeros((1, LANES), F32)
    invf = invf.at[0, QK_NOPE:QK_NOPE + half].set(inv_freq)
    invf = invf.at[0, QK_NOPE + half:QK_NOPE + QK_ROPE].set(inv_freq)

    shared_kv = None
    for l in range(DEPTH):
        mk, mvt = _mem_proj(mem2d, mem_w_kv[l].astype(BF16), nb)
        if l < n_a:
            w_in = a_w_in[l]
            w = jnp.concatenate([w_in[:, :3 * FOX_WIDTH],
                                 _pad_cols(w_in[:, 3 * FOX_WIDTH:3 * FOX_WIDTH + FOX_HEADS], LANES),
                                 w_in[:, 3 * FOX_WIDTH + FOX_HEADS:]], axis=1).astype(BF16)
            qt, k, vt, mqt, cum, cumt, cfirst, clast, knorm = _fox_proj(
                x2d, w, _row(a_b_f[l], LANES), seq)
            stats = [s.reshape(-1, LANES) for s in (cfirst, clast, knorm)]
            att = _causal_attention(qt, k.reshape(nb, seq, -1), vt,
                                    [cum.reshape(nb, seq, LANES), cumt] + stats, slab=False)
            w_out = a_w_out[l]
        else:
            bl = l - n_a
            w = jnp.concatenate([b_w_in[bl], kv_w_dkv[:, :KV_LORA],
                                 jnp.zeros((d, QK_NOPE), F32), kv_w_dkv[:, KV_LORA:],
                                 jnp.zeros((d, LANES - QK_NOPE - QK_ROPE), F32)], axis=1).astype(BF16)
            wuq = b_w_uq[bl].reshape(Q_LORA, MLA_HEADS, QK_NOPE + QK_ROPE)
            wuq = jnp.pad(wuq, ((0, 0), (0, 0), (0, MLA_SLAB - QK_NOPE - QK_ROPE)))
            wuq = wuq.reshape(Q_LORA, MLA_HEADS * MLA_SLAB).astype(BF16)
            wukv = kv_w_ukv.reshape(KV_LORA, MLA_HEADS, QK_NOPE + V_DIM)
            wk = jnp.pad(wukv[:, :, :QK_NOPE], ((0, 0), (0, 0), (0, MLA_SLAB - QK_NOPE)))
            wk = wk.reshape(KV_LORA, MLA_HEADS * MLA_SLAB).astype(BF16)
            wv = wukv[:, :, QK_NOPE:].reshape(KV_LORA, MLA_V_WIDTH).astype(BF16)
            qt, mqt, k_new, vt_new = _mla_proj(x2d, pos2d, invf, w, _row(b_g_q[bl]), wuq,
                                              _row(kv_g), wk, wv, seq)
            if shared_kv is None:
                shared_kv = (k_new.reshape(nb, seq, -1), vt_new)
            att = _causal_attention(qt, shared_kv[0], shared_kv[1], None, slab=True)
            w_out = b_w_out[bl]
        memo = _memory_attention(mqt, mk, mvt)
        n_att = w_out.shape[0] - MEM_WIDTH
        x1, x1p, idx, rank, gates, counts = _post_attn(
            x2d, att.reshape(t_tokens, -1), memo.reshape(t_tokens, MEM_WIDTH),
            w_out[:n_att].astype(BF16), w_out[n_att:].astype(BF16),
            _row(ln_g[l, 0]), _row(ln_b[l, 0]),
            _pad_cols(moe_w_r[l], LANES).astype(BF16), _row(moe_b_r[l], LANES))
        x2d = _moe(x1, x1p, idx, rank, gates, counts, l,
                   moe_w_gu, moe_b_gu.reshape(DEPTH, N_EXPERTS, 1, -1),
                   moe_w_dn, moe_b_dn.reshape(DEPTH, N_EXPERTS, 1, -1),
                   _row(ln_g[l, 1]), _row(ln_b[l, 1]))
    return x2d.reshape(nb, seq, d)
```

```python
import functools
import math

import jax
import jax.numpy as jnp
from jax import lax
from jax.experimental import pallas as pl
from jax.experimental.pallas import tpu as pltpu
from jax.experimental.pallas import tpu_sc as plsc

F32 = jnp.float32
BF16 = jnp.bfloat16

D_MODEL = 1024
DEPTH = 2
N_MEM = 256
HEAD_DIM = 64
FOX_HEADS = 12
MEM_HEADS = 4
MLA_HEADS = 12
Q_LORA = 384
KV_LORA = 256
QK_NOPE = 64
QK_ROPE = 32
V_DIM = 64
ROPE_THETA = 10000.0
N_EXPERTS = 32
TOP_K = 4
D_EXPERT = D_MODEL
SWIGLU_LIMIT = 7.0
SWIGLU_ALPHA = 1.702
LN_EPS = 1e-5
RMS_EPS = 1e-6
NEG_INF = -1e30
DEEPNORM_ALPHA = (2 * DEPTH) ** 0.25
FOX_WIDTH = FOX_HEADS * HEAD_DIM
MEM_WIDTH = MEM_HEADS * HEAD_DIM
MLA_V_WIDTH = MLA_HEADS * V_DIM

LANES = 128
LOG2E = math.log2(math.e)
VMEM_LIMIT = 48 * 1024 * 1024
EXPERT_VMEM_LIMIT = 56 * 1024 * 1024

PROJ_TM = 512
ATT_TQ = 512
FOX_TQ = 512
MOE_TM = 512
COMBINE_TM = 256
MLA_SLAB = LANES
VT_ROWS = HEAD_DIM + 16
SC_CORES = 2
SC_SUBCORES = 16
SC_WORKERS = SC_CORES * SC_SUBCORES
SC_CHUNK = 32
SC_LANES = 16
SC_SCAN = 8192


def _cparams(n_axes):
    return pltpu.CompilerParams(dimension_semantics=("arbitrary",) * n_axes,
                                vmem_limit_bytes=VMEM_LIMIT)


def _split3(x):
    hi = x.astype(BF16)
    r1 = x - hi.astype(F32)
    mid = r1.astype(BF16)
    lo = (r1 - mid.astype(F32)).astype(BF16)
    return hi, mid, lo


def _pack_bf16_pairs(v):
    bits = pltpu.bitcast(v.astype(F32), jnp.uint32)
    half = v.shape[1] // 2
    return (bits[:, :half] >> 16) | bits[:, half:]


def _unpack_bf16_pairs(words):
    return jnp.concatenate([pltpu.bitcast(words << 16, F32),
                            pltpu.bitcast(words & jnp.uint32(0xFFFF0000), F32)], axis=1)


def _layer_norm(y, g, b):
    mu = jnp.mean(y, axis=-1, keepdims=True)
    yc = y - mu
    var = jnp.mean(yc * yc, axis=-1, keepdims=True)
    return yc * lax.rsqrt(var + LN_EPS) * g + b


def _rms_norm(y, g):
    return y * lax.rsqrt(jnp.mean(y * y, axis=-1, keepdims=True) + RMS_EPS) * g


def _store_transposed(dst_ref, val):
    for s in range(val.shape[1] // LANES):
        sl = slice(s * LANES, (s + 1) * LANES)
        dst_ref[0, sl, :] = val[:, sl].T.astype(dst_ref.dtype)


def _store_values_transposed(vt_ref, val):
    tm = val.shape[0]
    ones = jnp.ones((VT_ROWS - HEAD_DIM, tm), vt_ref.dtype)
    for s in range(val.shape[1] // LANES):
        pair_t = val[:, s * LANES:(s + 1) * LANES].T.astype(vt_ref.dtype)
        for h in range(2):
            r0 = (2 * s + h) * VT_ROWS
            vt_ref[0, r0:r0 + HEAD_DIM, :] = pair_t[h * HEAD_DIM:(h + 1) * HEAD_DIM]
            vt_ref[0, r0 + HEAD_DIM:r0 + VT_ROWS, :] = ones


def _fox_proj_kernel(x_ref, w_ref, bf_ref, seg_ref, qt_ref, k_ref, vt_ref, mqt_ref, cum_ref,
                     cumt_ref, cfirst_ref, clast_ref, knorm_ref, carry_ref, knmax_ref, *,
                     tiles_per_batch):
    t = pl.program_id(0)

    @pl.when(t % tiles_per_batch == 0)
    def _():
        carry_ref[...] = jnp.zeros_like(carry_ref)
        knmax_ref[...] = jnp.zeros_like(knmax_ref)

    tm = x_ref.shape[0]
    proj = jnp.dot(x_ref[...].astype(BF16), w_ref[...], preferred_element_type=F32)
    qscale = HEAD_DIM ** -0.5 * LOG2E
    _store_transposed(qt_ref, proj[:, :FOX_WIDTH] * qscale)
    kb = proj[:, FOX_WIDTH:2 * FOX_WIDTH].astype(BF16)
    k_ref[...] = kb
    _store_values_transposed(vt_ref, proj[:, 2 * FOX_WIDTH:3 * FOX_WIDTH])
    kf = kb.astype(F32)
    ksq = jnp.dot((kf * kf).astype(BF16), seg_ref[...], preferred_element_type=F32)
    n_sub = tm // FOX_TQ
    for sub in range(n_sub):
        tile_max = jnp.sqrt(jnp.max(ksq[sub * FOX_TQ:(sub + 1) * FOX_TQ], axis=0, keepdims=True))
        knmax_ref[...] = jnp.maximum(knmax_ref[...], tile_max)
        knorm_ref[sub] = knmax_ref[...]
    f = proj[:, 3 * FOX_WIDTH:3 * FOX_WIDTH + LANES] + bf_ref[...]
    _store_transposed(mqt_ref, proj[:, 3 * FOX_WIDTH + LANES:] * qscale)
    log_f = jnp.minimum(f, 0.0) - jnp.log1p(jnp.exp(-jnp.abs(f)))
    row = lax.broadcasted_iota(jnp.int32, (tm, tm), 0)
    col = lax.broadcasted_iota(jnp.int32, (tm, tm), 1)
    tri = jnp.where(row >= col, 1.0, 0.0).astype(BF16)
    hi, mid, lo = _split3(log_f)
    cum = (jnp.dot(tri, hi, preferred_element_type=F32)
           + jnp.dot(tri, mid, preferred_element_type=F32)
           + jnp.dot(tri, lo, preferred_element_type=F32)) + carry_ref[...]
    carry_ref[...] = cum[tm - 1:tm, :]
    cum2 = cum * LOG2E
    cum_ref[...] = cum2
    cumt_ref[0] = cum2.T[:16, :]
    for sub in range(n_sub):
        cfirst_ref[sub] = cum2[sub * FOX_TQ:sub * FOX_TQ + 1, :]
        clast_ref[sub] = cum2[(sub + 1) * FOX_TQ - 1:(sub + 1) * FOX_TQ, :]


def _fox_proj(x2d, w, bf, seq):
    t_tokens = x2d.shape[0]
    tm = PROJ_TM
    nb = t_tokens // seq
    n = w.shape[1]
    tiles_per_batch = seq // tm
    n_tiles = t_tokens // tm
    seg = (jnp.arange(FOX_WIDTH)[:, None] // HEAD_DIM == jnp.arange(LANES)[None, :]).astype(BF16)
    row_spec = lambda width: pl.BlockSpec((tm, width), lambda i: (i, 0))
    t_spec = lambda rows: pl.BlockSpec(
        (1, rows, tm), lambda i: (i // tiles_per_batch, 0, i % tiles_per_batch))
    n_sub = tm // FOX_TQ
    stat_spec = pl.BlockSpec((n_sub, 1, LANES), lambda i: (i, 0, 0))
    stat_shape = jax.ShapeDtypeStruct((n_tiles * n_sub, 1, LANES), F32)
    return pl.pallas_call(
        functools.partial(_fox_proj_kernel, tiles_per_batch=tiles_per_batch),
        grid=(n_tiles,),
        in_specs=[row_spec(D_MODEL),
                  pl.BlockSpec((D_MODEL, n), lambda i: (0, 0)),
                  pl.BlockSpec((1, LANES), lambda i: (0, 0)),
                  pl.BlockSpec((FOX_WIDTH, LANES), lambda i: (0, 0))],
        out_specs=[t_spec(FOX_WIDTH), row_spec(FOX_WIDTH), t_spec(FOX_HEADS * VT_ROWS),
                   t_spec(MEM_WIDTH), row_spec(LANES), t_spec(16),
                   stat_spec, stat_spec, stat_spec],
        out_shape=[jax.ShapeDtypeStruct((nb, FOX_WIDTH, seq), BF16),
                   jax.ShapeDtypeStruct((t_tokens, FOX_WIDTH), BF16),
                   jax.ShapeDtypeStruct((nb, FOX_HEADS * VT_ROWS, seq), BF16),
                   jax.ShapeDtypeStruct((nb, MEM_WIDTH, seq), BF16),
                   jax.ShapeDtypeStruct((t_tokens, LANES), F32),
                   jax.ShapeDtypeStruct((nb, 16, seq), F32),
                   stat_shape, stat_shape, stat_shape],
        scratch_shapes=[pltpu.VMEM((1, LANES), F32), pltpu.VMEM((1, LANES), F32)],
        compiler_params=_cparams(1),
        name="fox_proj",
    )(x2d, w, bf, seg)


def _rope_tables(pos_ref, invf_ref):
    ang = pos_ref[...].astype(F32) * invf_ref[...]
    cos = jnp.cos(ang)
    sin = jnp.sin(ang)
    lane = lax.broadcasted_iota(jnp.int32, ang.shape, 1)
    half = QK_ROPE // 2
    in_x1 = (lane >= QK_NOPE) & (lane < QK_NOPE + half)
    in_x2 = (lane >= QK_NOPE + half) & (lane < QK_NOPE + QK_ROPE)
    c_tab = jnp.where(lane < QK_NOPE, 1.0, jnp.where(in_x1 | in_x2, cos, 0.0))
    s_up = jnp.where(in_x2, sin, 0.0)
    s_dn = jnp.where(in_x1, -sin, 0.0)
    return c_tab, s_up, s_dn


def _rope_slab(slab, tables):
    c_tab, s_up, s_dn = tables
    half = QK_ROPE // 2
    up = pltpu.roll(slab, half, 1)
    dn = pltpu.roll(slab, LANES - half, 1)
    return slab * c_tab + up * s_up + dn * s_dn


def _mla_proj_kernel(x_ref, pos_ref, invf_ref, w_ref, gq_ref, wuq_ref, gkv_ref, wk_ref, wv_ref,
                     qt_ref, mqt_ref, k_ref, vt_ref):
    proj = jnp.dot(x_ref[...].astype(BF16), w_ref[...], preferred_element_type=F32)
    tables = _rope_tables(pos_ref, invf_ref)
    c_q = _rms_norm(proj[:, :Q_LORA], gq_ref[...])
    _store_transposed(mqt_ref, proj[:, Q_LORA:Q_LORA + MEM_WIDTH] * (HEAD_DIM ** -0.5 * LOG2E))
    kv_off = Q_LORA + MEM_WIDTH
    c_kv = _rms_norm(proj[:, kv_off:kv_off + KV_LORA], gkv_ref[...])
    kr = _rope_slab(proj[:, kv_off + KV_LORA:], tables)
    q = jnp.dot(c_q.astype(BF16), wuq_ref[...], preferred_element_type=F32)
    kn = jnp.dot(c_kv.astype(BF16), wk_ref[...], preferred_element_type=F32)
    qscale = (QK_NOPE + QK_ROPE) ** -0.5 * LOG2E
    for h in range(MLA_HEADS):
        sl = slice(h * MLA_SLAB, (h + 1) * MLA_SLAB)
        qt_ref[0, sl, :] = (_rope_slab(q[:, sl], tables) * qscale).T.astype(BF16)
        k_ref[:, sl] = (kn[:, sl] + kr).astype(BF16)
    _store_values_transposed(vt_ref, jnp.dot(c_kv.astype(BF16), wv_ref[...],
                                             preferred_element_type=F32))


def _mla_proj(x2d, pos2d, invf, w, gq, wuq, gkv, wk, wv, seq):
    t_tokens = x2d.shape[0]
    tm = PROJ_TM
    nb = t_tokens // seq
    tiles_per_batch = seq // tm
    row_spec = lambda width: pl.BlockSpec((tm, width), lambda i: (i, 0))
    t_spec = lambda rows: pl.BlockSpec(
        (1, rows, tm), lambda i: (i // tiles_per_batch, 0, i % tiles_per_batch))
    full = lambda a: pl.BlockSpec(a.shape, lambda i: (0, 0))
    slabs = MLA_HEADS * MLA_SLAB
    return pl.pallas_call(
        _mla_proj_kernel,
        grid=(t_tokens // tm,),
        in_specs=[row_spec(D_MODEL), row_spec(1), full(invf), full(w), full(gq), full(wuq),
                  full(gkv), full(wk), full(wv)],
        out_specs=[t_spec(slabs), t_spec(MEM_WIDTH), row_spec(slabs),
                   t_spec(MLA_HEADS * VT_ROWS)],
        out_shape=[jax.ShapeDtypeStruct((nb, slabs, seq), BF16),
                   jax.ShapeDtypeStruct((nb, MEM_WIDTH, seq), BF16),
                   jax.ShapeDtypeStruct((t_tokens, slabs), BF16),
                   jax.ShapeDtypeStruct((nb, MLA_HEADS * VT_ROWS, seq), BF16)],
        compiler_params=_cparams(1),
        name="mla_proj",
    )(x2d, pos2d, invf, w, gq, wuq, gkv, wk, wv)


SKIP_LOG2 = -160.0
NORM_SLACK = 1.02


def _causal_attn_kernel(*refs, tq, n_tiles, fox, slab, long_body):
    if fox:
        (cf_ref, cl_ref, kn_ref, qt_ref, k_ref, vt_ref, cq_ref, ck_ref,
         o_ref, m_ref, acc_ref, s_ref) = refs
    else:
        qt_ref, k_ref, vt_ref, o_ref, m_ref, acc_ref, s_ref = refs
    b = pl.program_id(0)
    pair = pl.program_id(1)
    i = pl.program_id(2)
    m_ref[...] = jnp.full(m_ref.shape, NEG_INF, F32)
    acc_ref[...] = jnp.zeros(acc_ref.shape, F32)
    qt = qt_ref[0]
    if slab:
        qth = [qt[:MLA_SLAB], qt[MLA_SLAB:]]
    else:
        rowi = lax.broadcasted_iota(jnp.int32, qt.shape, 0)
        zero = jnp.zeros_like(qt)
        qth = [jnp.where(rowi < HEAD_DIM, qt, zero), jnp.where(rowi >= HEAD_DIM, qt, zero)]
    if fox:
        lane = lax.broadcasted_iota(jnp.int32, (1, LANES), 1)
        cq = [cq_ref[0, pl.ds(2 * pair + h, 1), :] for h in range(2)]

    def logits_to(slot, j):
        off = pl.multiple_of(j * tq, tq)
        kc = k_ref[0, pl.ds(off, tq), :]
        for h in range(2):
            kh = kc[:, h * MLA_SLAB:(h + 1) * MLA_SLAB] if slab else kc
            s = jnp.dot(kh, qth[h], preferred_element_type=F32)
            if fox:
                ck_blk = ck_ref[0, pl.ds(off, tq), :]
                ck = jnp.sum(jnp.where(lane == 2 * pair + h, ck_blk, 0.0), axis=1, keepdims=True)
                s = s + cq[h] - ck
            s_ref[slot, h] = s

    def softmax_pv(slot, j, causal_mask):
        off = pl.multiple_of(j * tq, tq)
        for h in range(2):
            s = s_ref[slot, h]
            if causal_mask is not None:
                s = jnp.where(causal_mask, s, NEG_INF)
            m_prev = m_ref[h]
            m_new = jnp.maximum(m_prev, jnp.max(s, axis=0, keepdims=True))
            alpha = jnp.exp2(m_prev - m_new)
            p = jnp.exp2(s - m_new).astype(BF16)
            vth = vt_ref[0, h * VT_ROWS:(h + 1) * VT_ROWS, pl.ds(off, tq)]
            acc_ref[h] = acc_ref[h] * alpha + jnp.dot(vth, p, preferred_element_type=F32)
            m_ref[h] = m_new

    if fox:
        base = b * n_tiles
        qn = []
        for h in range(2):
            qf = qth[h].astype(F32)
            qn.append(jnp.sqrt(jnp.max(jnp.sum(qf * qf, axis=0, keepdims=True))) * NORM_SLACK)

        top = [2.0 * qn[h] * kn_ref[base + i, 2 * pair + h] + cf_ref[base + i, 2 * pair + h]
               for h in range(2)]

        def live(j):
            ub = [top[h] - cl_ref[base + jnp.maximum(j, 0), 2 * pair + h] for h in range(2)]
            return (j >= 0) & (jnp.maximum(ub[0], ub[1]) > SKIP_LOG2)

        j0 = lax.while_loop(live, lambda j: j - 1, i - 1) + 1
    else:
        j0 = 0
    n_before = i - j0
    odd = (n_before % 2) == 1

    @pl.when(odd)
    def _():
        logits_to(1, j0)
        logits_to(0, j0 + 1)
        softmax_pv(1, j0, None)

    @pl.when(jnp.logical_not(odd))
    def _():
        logits_to(0, j0)

    def two_steps(j):
        logits_to(1, j + 1)
        softmax_pv(0, j, None)
        logits_to(0, j + 2)
        softmax_pv(1, j + 1, None)

    j_even = j0 + (n_before % 2)
    n_two = n_before // 2
    if long_body:
        @pl.when((n_two % 2) == 1)
        def _():
            two_steps(j_even)

        j_quad = j_even + 2 * (n_two % 2)

        def body(t, carry):
            two_steps(j_quad + 4 * t)
            two_steps(j_quad + 4 * t + 2)
            return carry

        lax.fori_loop(0, n_two // 2, body, 0)
    else:
        def body(t, carry):
            two_steps(j_even + 2 * t)
            return carry

        lax.fori_loop(0, n_two, body, 0)
    key = lax.broadcasted_iota(jnp.int32, (tq, tq), 0)
    qry = lax.broadcasted_iota(jnp.int32, (tq, tq), 1)
    softmax_pv(0, i, key <= qry)
    out_t = jnp.concatenate(
        [acc_ref[h, :HEAD_DIM] * (1.0 / acc_ref[h, HEAD_DIM:HEAD_DIM + 1]) for h in range(2)],
        axis=0)
    o_ref[0] = out_t.T.astype(o_ref.dtype)


def _causal_attention(qt, k, vt, fox_args, *, slab):
    nb, seq, _ = k.shape
    fox = fox_args is not None
    tq = FOX_TQ if fox else ATT_TQ
    n_tiles = seq // tq
    n_pairs = FOX_HEADS // 2
    rows = 2 * MLA_SLAB if slab else LANES
    in_specs = [pl.BlockSpec((1, rows, tq), lambda b, p, i: (b, p, i)),
                pl.BlockSpec((1, seq, rows), lambda b, p, i: (b, 0, p)),
                pl.BlockSpec((1, 2 * VT_ROWS, seq), lambda b, p, i: (b, p, 0))]
    args = [qt, k, vt]
    if fox:
        cum, cumt, cfirst, clast, knorm = fox_args
        smem = pl.BlockSpec(memory_space=pltpu.SMEM)
        in_specs = [smem, smem, smem] + in_specs + [
            pl.BlockSpec((1, 16, tq), lambda b, p, i: (b, 0, i)),
            pl.BlockSpec((1, seq, LANES), lambda b, p, i: (b, 0, 0))]
        args = [cfirst, clast, knorm] + args + [cumt, cum]
    return pl.pallas_call(
        functools.partial(_causal_attn_kernel, tq=tq, n_tiles=n_tiles, fox=fox, slab=slab,
                          long_body=not fox),
        grid=(nb, n_pairs, n_tiles),
        in_specs=in_specs,
        out_specs=pl.BlockSpec((1, tq, LANES), lambda b, p, i: (b, i, p)),
        out_shape=jax.ShapeDtypeStruct((nb, seq, n_pairs * LANES), BF16),
        scratch_shapes=[pltpu.VMEM((2, 1, tq), F32), pltpu.VMEM((2, VT_ROWS, tq), F32),
                        pltpu.VMEM((2, 2, tq, tq), F32)],
        compiler_params=_cparams(3),
        name="fox_attention" if fox else "mla_attention",
    )(*args)


def _mem_proj_kernel(mem_ref, w_ref, mk_ref, mvt_ref):
    mkv = jnp.dot(mem_ref[...].astype(BF16), w_ref[...], preferred_element_type=F32)
    mk_ref[0] = mkv[:, :MEM_WIDTH].astype(BF16)
    _store_values_transposed(mvt_ref, mkv[:, MEM_WIDTH:])


def _mem_proj(mem2d, w, nb):
    return pl.pallas_call(
        _mem_proj_kernel,
        grid=(nb,),
        in_specs=[pl.BlockSpec((N_MEM, D_MODEL), lambda b: (b, 0)),
                  pl.BlockSpec(w.shape, lambda b: (0, 0))],
        out_specs=[pl.BlockSpec((1, N_MEM, MEM_WIDTH), lambda b: (b, 0, 0)),
                   pl.BlockSpec((1, MEM_HEADS * VT_ROWS, N_MEM), lambda b: (b, 0, 0))],
        out_shape=[jax.ShapeDtypeStruct((nb, N_MEM, MEM_WIDTH), BF16),
                   jax.ShapeDtypeStruct((nb, MEM_HEADS * VT_ROWS, N_MEM), BF16)],
        compiler_params=_cparams(1),
        name="mem_proj",
    )(mem2d, w)


def _mem_attn_kernel(qt_ref, k_ref, vt_ref, o_ref):
    qt = qt_ref[0]
    kc = k_ref[0]
    for pair in range(MEM_HEADS // 2):
        qt_p = qt[pair * LANES:(pair + 1) * LANES]
        k_p = kc[:, pair * LANES:(pair + 1) * LANES]
        rowi = lax.broadcasted_iota(jnp.int32, qt_p.shape, 0)
        zero = jnp.zeros_like(qt_p)
        outs = []
        for h in range(2):
            mine = (rowi < HEAD_DIM) if h == 0 else (rowi >= HEAD_DIM)
            s = jnp.dot(k_p, jnp.where(mine, qt_p, zero), preferred_element_type=F32)
            p = jnp.exp2(s - jnp.max(s, axis=0, keepdims=True)).astype(BF16)
            r0 = (2 * pair + h) * VT_ROWS
            acc = jnp.dot(vt_ref[0, r0:r0 + VT_ROWS, :], p, preferred_element_type=F32)
            outs.append(acc[:HEAD_DIM] * (1.0 / acc[HEAD_DIM:HEAD_DIM + 1]))
        o_ref[0, :, pair * LANES:(pair + 1) * LANES] = (
            jnp.concatenate(outs, axis=0).T.astype(o_ref.dtype))


def _memory_attention(mqt, mk, mvt):
    nb, _, seq = mqt.shape
    tq = ATT_TQ
    return pl.pallas_call(
        _mem_attn_kernel,
        grid=(nb, seq // tq),
        in_specs=[pl.BlockSpec((1, MEM_WIDTH, tq), lambda b, i: (b, 0, i)),
                  pl.BlockSpec((1, N_MEM, MEM_WIDTH), lambda b, i: (b, 0, 0)),
                  pl.BlockSpec((1, MEM_HEADS * VT_ROWS, N_MEM), lambda b, i: (b, 0, 0))],
        out_specs=pl.BlockSpec((1, tq, MEM_WIDTH), lambda b, i: (b, i, 0)),
        out_shape=jax.ShapeDtypeStruct((nb, seq, MEM_WIDTH), BF16),
        compiler_params=_cparams(2),
        name="memory_attention",
    )(mqt, mk, mvt)


def _post_attn_kernel(x_ref, att_ref, memo_ref, wa_ref, wm_ref, g_ref, b_ref, wr_ref, br_ref,
                      x1_ref, x1p_ref, idx_ref, rank_ref, gate_ref, cnt_ref, carry_ref):
    t = pl.program_id(0)

    @pl.when(t == 0)
    def _():
        carry_ref[...] = jnp.zeros_like(carry_ref)

    tm = x_ref.shape[0]
    mix = (jnp.dot(att_ref[...], wa_ref[...], preferred_element_type=F32)
           + jnp.dot(memo_ref[...], wm_ref[...], preferred_element_type=F32))
    x1 = _layer_norm(DEEPNORM_ALPHA * x_ref[...] + mix, g_ref[...], b_ref[...])
    x1_ref[...] = x1
    x1b = x1.astype(BF16)
    x1p_ref[...] = _pack_bf16_pairs(x1b)
    logits = jnp.dot(x1b, wr_ref[...], preferred_element_type=F32) + br_ref[...]
    lane = lax.broadcasted_iota(jnp.int32, (tm, LANES), 1)
    work = jnp.where(lane < N_EXPERTS, logits, -jnp.inf)
    idxs, vals = [], []
    onehot = jnp.zeros((tm, LANES), F32)
    for _ in range(TOP_K):
        best = jnp.max(work, axis=1, keepdims=True)
        where_best = jnp.min(jnp.where(work == best, lane, LANES), axis=1, keepdims=True)
        hit = lane == where_best
        onehot = jnp.where(hit, 1.0, onehot)
        work = jnp.where(hit, -jnp.inf, work)
        idxs.append(where_best)
        vals.append(best)
    exps = [jnp.exp(v - vals[0]) for v in vals]
    denom = exps[0] + exps[1] + exps[2] + exps[3]
    row = lax.broadcasted_iota(jnp.int32, (tm, tm), 0)
    col = lax.broadcasted_iota(jnp.int32, (tm, tm), 1)
    strict = jnp.where(row > col, 1.0, 0.0).astype(BF16)
    before = jnp.dot(strict, onehot.astype(BF16), preferred_element_type=F32) + carry_ref[...]
    idx_out = jnp.zeros((tm, LANES), jnp.int32)
    rank_out = jnp.zeros((tm, LANES), jnp.int32)
    gate_out = jnp.zeros((tm, LANES), F32)
    for r in range(TOP_K):
        rank_r = jnp.sum(jnp.where(lane == idxs[r], before, 0.0), axis=1, keepdims=True)
        idx_out = jnp.where(lane == r, idxs[r], idx_out)
        rank_out = jnp.where(lane == r, rank_r.astype(jnp.int32), rank_out)
        gate_out = jnp.where(lane == r, exps[r] / denom, gate_out)
    idx_ref[...] = idx_out
    rank_ref[...] = rank_out
    gate_ref[...] = gate_out
    total = carry_ref[...] + jnp.sum(onehot, axis=0, keepdims=True)
    carry_ref[...] = total
    cnt_ref[...] = jnp.broadcast_to(total, cnt_ref.shape)


def _post_attn(x2d, att, memo, wa, wm, g, b, wr, br):
    t_tokens = x2d.shape[0]
    tm = PROJ_TM
    row_spec = lambda width: pl.BlockSpec((tm, width), lambda i: (i, 0))
    full = lambda a: pl.BlockSpec(a.shape, lambda i: (0, 0))
    return pl.pallas_call(
        _post_attn_kernel,
        grid=(t_tokens // tm,),
        in_specs=[row_spec(D_MODEL), row_spec(att.shape[1]), row_spec(MEM_WIDTH),
                  full(wa), full(wm), full(g), full(b), full(wr), full(br)],
        out_specs=[row_spec(D_MODEL), row_spec(D_MODEL // 2), row_spec(LANES), row_spec(LANES),
                   row_spec(LANES), pl.BlockSpec((8, LANES), lambda i: (0, 0))],
        out_shape=[jax.ShapeDtypeStruct((t_tokens, D_MODEL), F32),
                   jax.ShapeDtypeStruct((t_tokens, D_MODEL // 2), jnp.uint32),
                   jax.ShapeDtypeStruct((t_tokens, LANES), jnp.int32),
                   jax.ShapeDtypeStruct((t_tokens, LANES), jnp.int32),
                   jax.ShapeDtypeStruct((t_tokens, LANES), F32),
                   jax.ShapeDtypeStruct((8, LANES), F32)],
        scratch_shapes=[pltpu.VMEM((1, LANES), F32)],
        compiler_params=_cparams(1),
        name="outproj_ln_router",
    )(x2d, att, memo, wa, wm, g, b, wr, br)


def _expert_kernel(be_ref, nu_ref, x_ref, wgu_ref, bgu_ref, wdn_ref, bdn_ref, *rest,
                   n_chunks, blk_lo, has_prev):
    o_ref, wgu_bf, wdn_bf, h_ref = rest[1:] if has_prev else rest
    step = pl.program_id(0)
    blk = step + blk_lo

    @pl.when((step == 0) | (be_ref[blk] != be_ref[jnp.maximum(blk - 1, 0)]))
    def _():
        wgu_bf[...] = wgu_ref[0, 0].astype(BF16)
        wdn_bf[...] = wdn_ref[0, 0].astype(BF16)

    @pl.when(blk < nu_ref[0])
    def _():
        x = _unpack_bf16_pairs(x_ref[...]).astype(BF16)
        cw = D_EXPERT // n_chunks
        for c in range(n_chunks):
            gs = slice(c * cw, (c + 1) * cw)
            us = slice(D_EXPERT + c * cw, D_EXPERT + (c + 1) * cw)
            g = jnp.dot(x, wgu_bf[:, gs], preferred_element_type=F32) + bgu_ref[0, 0, :, gs]
            u = jnp.dot(x, wgu_bf[:, us], preferred_element_type=F32) + bgu_ref[0, 0, :, us]
            g = jnp.minimum(g, SWIGLU_LIMIT)
            u = jnp.clip(u, -SWIGLU_LIMIT, SWIGLU_LIMIT)
            h_ref[:, gs] = ((u + 1.0) * (g * jax.nn.sigmoid(SWIGLU_ALPHA * g))).astype(BF16)
        y = jnp.dot(h_ref[...], wdn_bf[...], preferred_element_type=F32) + bdn_ref[0, 0]
        o_ref[...] = _pack_bf16_pairs(y.astype(BF16))

    @pl.when(blk >= nu_ref[0])
    def _():
        o_ref[...] = jnp.zeros_like(o_ref)


def _experts(block_expert, n_used, xb, blk_lo, p_rows, y_prev, layer, wgu, bgu, wdn, bdn):
    tm = MOE_TM
    has_prev = y_prev is not None
    w_map = lambda i, be, nu: (layer, be[i + blk_lo], 0, 0)
    in_specs = [pl.BlockSpec((tm, D_MODEL // 2), lambda i, be, nu: (i, 0)),
                pl.BlockSpec((1, 1, D_MODEL, 2 * D_EXPERT), w_map),
                pl.BlockSpec((1, 1, 1, 2 * D_EXPERT), w_map),
                pl.BlockSpec((1, 1, D_EXPERT, D_MODEL), w_map),
                pl.BlockSpec((1, 1, 1, D_MODEL), w_map)]
    args = [block_expert, n_used, xb, wgu, bgu, wdn, bdn]
    if has_prev:
        in_specs.append(pl.BlockSpec(memory_space=pl.ANY))
        args.append(y_prev)
    grid_spec = pltpu.PrefetchScalarGridSpec(
        num_scalar_prefetch=2,
        grid=(xb.shape[0] // tm,),
        in_specs=in_specs,
        out_specs=pl.BlockSpec((tm, D_MODEL // 2), lambda i, be, nu: (i + blk_lo, 0)),
        scratch_shapes=[pltpu.VMEM((D_MODEL, 2 * D_EXPERT), BF16),
                        pltpu.VMEM((D_EXPERT, D_MODEL), BF16),
                        pltpu.VMEM((tm, D_EXPERT), BF16)],
    )
    return pl.pallas_call(
        functools.partial(_expert_kernel, n_chunks=4, blk_lo=blk_lo, has_prev=has_prev),
        grid_spec=grid_spec,
        out_shape=jax.ShapeDtypeStruct((p_rows, D_MODEL // 2), jnp.uint32),
        input_output_aliases={len(args) - 1: 0} if has_prev else {},
        compiler_params=pltpu.CompilerParams(dimension_semantics=("arbitrary",),
                                             vmem_limit_bytes=EXPERT_VMEM_LIMIT),
        name="experts",
    )(*args)


def _sc_move_rows(table_hbm, idx_v, out_hbm, base, n_chunks, rows_v, gsem, wsem):
    def gather(c, slot):
        off = pl.multiple_of(c * SC_CHUNK, SC_CHUNK)
        return pltpu.make_async_copy(table_hbm.at[idx_v.at[pl.ds(off, SC_CHUNK)]],
                                     rows_v.at[slot], gsem.at[slot])

    def put(c, slot):
        off = pl.multiple_of(c * SC_CHUNK, SC_CHUNK)
        return pltpu.make_async_copy(rows_v.at[slot], out_hbm.at[pl.ds(base + off, SC_CHUNK)],
                                     wsem.at[slot])

    gather(0, 0).start()
    gather(1, 1).start()

    @pl.loop(0, n_chunks, step=2)
    def _(c):
        for slot in range(2):
            gather(c + slot, slot).wait()
            put(c + slot, slot).start()
        for slot in range(2):
            put(c + slot, slot).wait()

            @pl.when(c + 2 + slot < n_chunks)
            def _():
                gather(c + 2 + slot, slot).start()


def _sc_gather_rows(table, idx):
    n_idx = idx.shape[0]
    width = table.shape[1]
    per_worker = n_idx // SC_WORKERS
    n_chunks = per_worker // SC_CHUNK
    mesh = plsc.VectorSubcoreMesh(core_axis_name="c", subcore_axis_name="s",
                                  num_cores=SC_CORES, num_subcores=SC_SUBCORES)

    def body(table_hbm, idx_hbm, out_hbm, idx_v, rows_v, gsem, wsem):
        wid = lax.axis_index("s") * SC_CORES + lax.axis_index("c")
        base = wid * per_worker
        pltpu.sync_copy(idx_hbm.at[pl.ds(base, per_worker)], idx_v)
        _sc_move_rows(table_hbm, idx_v, out_hbm, base, n_chunks, rows_v, gsem, wsem)

    return pl.kernel(
        body,
        out_type=jax.ShapeDtypeStruct((n_idx, width), table.dtype),
        mesh=mesh,
        scratch_types=[pltpu.VMEM((per_worker,), jnp.int32),
                       pltpu.VMEM((2, SC_CHUNK, width), table.dtype),
                       pltpu.SemaphoreType.DMA((2,)),
                       pltpu.SemaphoreType.DMA((2,))],
        name="sc_gather_rows",
    )(table, idx)


def _sc_dispatch_rows(table, dest, slot_lo, n_slots):
    n_tok, width = table.shape
    n_pairs = dest.shape[0]
    per_worker = n_slots // SC_WORKERS
    n_chunks = per_worker // SC_CHUNK
    n_scan = n_pairs // SC_SCAN
    mesh = plsc.VectorSubcoreMesh(core_axis_name="c", subcore_axis_name="s",
                                  num_cores=SC_CORES, num_subcores=SC_SUBCORES)

    def body(table_hbm, dest_hbm, out_hbm, tok_v, dest_v, rows_v, gsem, wsem):
        wid = lax.axis_index("s") * SC_CORES + lax.axis_index("c")
        base = wid * per_worker
        first = slot_lo + base
        lane = lax.iota(jnp.int32, SC_LANES)

        @pl.loop(0, per_worker // SC_LANES)
        def _(i):
            off = pl.multiple_of(i * SC_LANES, SC_LANES)
            tok_v[pl.ds(off, SC_LANES)] = lax.rem(first + off + lane, n_tok)

        @pl.loop(0, n_scan)
        def _(g):
            goff = pl.multiple_of(g * SC_SCAN, SC_SCAN)
            pltpu.sync_copy(dest_hbm.at[pl.ds(goff, SC_SCAN)], dest_v)

            @pl.loop(0, SC_SCAN // SC_LANES)
            def _(i):
                off = pl.multiple_of(i * SC_LANES, SC_LANES)
                local = dest_v[pl.ds(off, SC_LANES)] - first
                mine = (local >= 0) & (local < per_worker)
                pair = goff + off + lane
                plsc.store_scatter(tok_v, [jnp.where(mine, local, 0)], pair // TOP_K, mask=mine)

        _sc_move_rows(table_hbm, tok_v, out_hbm, base, n_chunks, rows_v, gsem, wsem)

    return pl.kernel(
        body,
        out_type=jax.ShapeDtypeStruct((n_slots, width), table.dtype),
        mesh=mesh,
        scratch_types=[pltpu.VMEM((per_worker,), jnp.int32),
                       pltpu.VMEM((SC_SCAN,), jnp.int32),
                       pltpu.VMEM((2, SC_CHUNK, width), table.dtype),
                       pltpu.SemaphoreType.DMA((2,)),
                       pltpu.SemaphoreType.DMA((2,))],
        compiler_params=pltpu.CompilerParams(needs_layout_passes=False),
        name="sc_dispatch_rows",
    )(table, dest)


def _combine_kernel(x1_ref, yg_ref, gate_ref, g_ref, b_ref, *rest):
    o_ref = rest[-1]
    gates = gate_ref[...]
    ffn = _unpack_bf16_pairs(yg_ref[0]) * gates[:, 0:1]
    for r in range(1, TOP_K):
        ffn = ffn + _unpack_bf16_pairs(yg_ref[r]) * gates[:, r:r + 1]
    o_ref[...] = _layer_norm(DEEPNORM_ALPHA * x1_ref[...] + ffn, g_ref[...], b_ref[...])


def _combine(x1, yg, gates, g, b, tok_lo, out_prev):
    t_tokens = x1.shape[0]
    tm = COMBINE_TM
    tile_lo = tok_lo // tm
    in_specs = [pl.BlockSpec((tm, D_MODEL), lambda i: (i + tile_lo, 0)),
                pl.BlockSpec((TOP_K, tm, D_MODEL // 2), lambda i: (0, i, 0)),
                pl.BlockSpec((tm, LANES), lambda i: (i + tile_lo, 0)),
                pl.BlockSpec((1, D_MODEL), lambda i: (0, 0)),
                pl.BlockSpec((1, D_MODEL), lambda i: (0, 0))]
    args = [x1, yg, gates, g, b]
    if out_prev is not None:
        in_specs.append(pl.BlockSpec(memory_space=pl.ANY))
        args.append(out_prev)
    return pl.pallas_call(
        _combine_kernel,
        grid=(yg.shape[1] // tm,),
        in_specs=in_specs,
        out_specs=pl.BlockSpec((tm, D_MODEL), lambda i: (i + tile_lo, 0)),
        out_shape=jax.ShapeDtypeStruct((t_tokens, D_MODEL), F32),
        input_output_aliases={len(args) - 1: 0} if out_prev is not None else {},
        compiler_params=_cparams(1),
        name="combine_ln",
    )(*args)


def _moe(x1, x1p, idx, rank, gates, counts, layer, wgu, bgu, wdn, bdn, g, b):
    t_tokens = x1.shape[0]
    tm = MOE_TM
    counts = counts[0, :N_EXPERTS].astype(jnp.int32)
    padded = ((counts + tm - 1) // tm) * tm
    pend = jnp.cumsum(padded)
    pstart = pend - padded
    idx4 = idx[:, :TOP_K]
    dest = (pstart[idx4] + rank[:, :TOP_K]).reshape(-1)
    n_blocks = (t_tokens * TOP_K) // tm + N_EXPERTS
    p_rows = n_blocks * tm
    block_start = jnp.arange(n_blocks, dtype=jnp.int32) * tm
    block_expert = jnp.minimum(
        jnp.sum((pend[None, :] <= block_start[:, None]).astype(jnp.int32), axis=1), N_EXPERTS - 1)
    n_used = (pend[-1:] // tm).astype(jnp.int32)
    half_rows = p_rows // 2
    y = None
    for part in range(2):
        xb = _sc_dispatch_rows(x1p, dest, part * half_rows, half_rows)
        y = _experts(block_expert, n_used, xb, part * (n_blocks // 2), p_rows, y,
                     layer, wgu, bgu, wdn, bdn)
    dest2 = dest.reshape(t_tokens, TOP_K)
    half_tok = t_tokens // 2
    out = None
    for part in range(2):
        by_choice = dest2[part * half_tok:(part + 1) * half_tok].T.reshape(-1)
        yg = _sc_gather_rows(y, by_choice).reshape(TOP_K, half_tok, D_MODEL // 2)
        out = _combine(x1, yg, gates, g, b, part * half_tok, out)
    return out


def _row(v, width=None):
    v = v.astype(F32).reshape(1, -1)
    if width is not None and v.shape[1] < width:
        v = jnp.pad(v, ((0, 0), (0, width - v.shape[1])))
    return v


def _pad_cols(w, width):
    return jnp.pad(w, ((0, 0), (0, width - w.shape[1])))


def kernel(x, mem, positions, a_w_in, a_b_f, a_w_out, b_w_in, b_g_q, b_w_uq, b_w_out,
           kv_w_dkv, kv_g, kv_w_ukv, mem_w_kv, ln_g, ln_b,
           moe_w_r, moe_b_r, moe_w_gu, moe_b_gu, moe_w_dn, moe_b_dn):
    nb, seq, d = x.shape
    t_tokens = nb * seq
    n_a = a_w_in.shape[0]
    x2d = x.reshape(t_tokens, d)
    mem2d = mem.reshape(nb * N_MEM, d)
    pos2d = positions.reshape(t_tokens, 1)
    half = QK_ROPE // 2
    inv_freq = ROPE_THETA ** (-jnp.arange(half, dtype=F32) * 2.0 / QK_ROPE)
    invf = jnp.zeros((1, LANES), F32)
    invf = invf.at[0, QK_NOPE:QK_NOPE + half].set(inv_freq)
    invf = invf.at[0, QK_NOPE + half:QK_NOPE + QK_ROPE].set(inv_freq)

    shared_kv = None
    for l in range(DEPTH):
        mk, mvt = _mem_proj(mem2d, mem_w_kv[l].astype(BF16), nb)
        if l < n_a:
            w_in = a_w_in[l]
            w = jnp.concatenate([w_in[:, :3 * FOX_WIDTH],
                                 _pad_cols(w_in[:, 3 * FOX_WIDTH:3 * FOX_WIDTH + FOX_HEADS], LANES),
                                 w_in[:, 3 * FOX_WIDTH + FOX_HEADS:]], axis=1).astype(BF16)
            qt, k, vt, mqt, cum, cumt, cfirst, clast, knorm = _fox_proj(
                x2d, w, _row(a_b_f[l], LANES), seq)
            stats = [s.reshape(-1, LANES) for s in (cfirst, clast, knorm)]
            att = _causal_attention(qt, k.reshape(nb, seq, -1), vt,
                                    [cum.reshape(nb, seq, LANES), cumt] + stats, slab=False)
            w_out = a_w_out[l]
        else:
            bl = l - n_a
            w = jnp.concatenate([b_w_in[bl], kv_w_dkv[:, :KV_LORA],
                                 jnp.zeros((d, QK_NOPE), F32), kv_w_dkv[:, KV_LORA:],
                                 jnp.zeros((d, LANES - QK_NOPE - QK_ROPE), F32)], axis=1).astype(BF16)
            wuq = b_w_uq[bl].reshape(Q_LORA, MLA_HEADS, QK_NOPE + QK_ROPE)
            wuq = jnp.pad(wuq, ((0, 0), (0, 0), (0, MLA_SLAB - QK_NOPE - QK_ROPE)))
            wuq = wuq.reshape(Q_LORA, MLA_HEADS * MLA_SLAB).astype(BF16)
            wukv = kv_w_ukv.reshape(KV_LORA, MLA_HEADS, QK_NOPE + V_DIM)
            wk = jnp.pad(wukv[:, :, :QK_NOPE], ((0, 0), (0, 0), (0, MLA_SLAB - QK_NOPE)))
            wk = wk.reshape(KV_LORA, MLA_HEADS * MLA_SLAB).astype(BF16)
            wv = wukv[:, :, QK_NOPE:].reshape(KV_LORA, MLA_V_WIDTH).astype(BF16)
            qt, mqt, k_new, vt_new = _mla_proj(x2d, pos2d, invf, w, _row(b_g_q[bl]), wuq,
                                              _row(kv_g), wk, wv, seq)
            if shared_kv is None:
                shared_kv = (k_new.reshape(nb, seq, -1), vt_new)
            att = _causal_attention(qt, shared_kv[0], shared_kv[1], None, slab=True)
            w_out = b_w_out[bl]
        memo = _memory_attention(mqt, mk, mvt)
        n_att = w_out.shape[0] - MEM_WIDTH
        x1, x1p, idx, rank, gates, counts = _post_attn(
            x2d, att.reshape(t_tokens, -1), memo.reshape(t_tokens, MEM_WIDTH),
            w_out[:n_att].astype(BF16), w_out[n_att:].astype(BF16),
            _row(ln_g[l, 0]), _row(ln_b[l, 0]),
            _pad_cols(moe_w_r[l], LANES).astype(BF16), _row(moe_b_r[l], LANES))
        x2d = _moe(x1, x1p, idx, rank, gates, counts, l,
                   moe_w_gu, moe_b_gu.reshape(DEPTH, N_EXPERTS, 1, -1),
                   moe_w_dn, moe_b_dn.reshape(DEPTH, N_EXPERTS, 1, -1),
                   _row(ln_g[l, 1]), _row(ln_b[l, 1]))
    return x2d.reshape(nb, seq, d)
```

```python
import functools
import math

import jax
import jax.numpy as jnp
from jax import lax
from jax.experimental import pallas as pl
from jax.experimental.pallas import tpu as pltpu
from jax.experimental.pallas import tpu_sc as plsc

F32 = jnp.float32
BF16 = jnp.bfloat16

D_MODEL = 1024
DEPTH = 2
N_MEM = 256
HEAD_DIM = 64
FOX_HEADS = 12
MEM_HEADS = 4
MLA_HEADS = 12
Q_LORA = 384
KV_LORA = 256
QK_NOPE = 64
QK_ROPE = 32
V_DIM = 64
ROPE_THETA = 10000.0
N_EXPERTS = 32
TOP_K = 4
D_EXPERT = D_MODEL
SWIGLU_LIMIT = 7.0
SWIGLU_ALPHA = 1.702
LN_EPS = 1e-5
RMS_EPS = 1e-6
NEG_INF = -1e30
DEEPNORM_ALPHA = (2 * DEPTH) ** 0.25
FOX_WIDTH = FOX_HEADS * HEAD_DIM
MEM_WIDTH = MEM_HEADS * HEAD_DIM
MLA_V_WIDTH = MLA_HEADS * V_DIM

LANES = 128
LOG2E = math.log2(math.e)
VMEM_LIMIT = 48 * 1024 * 1024
EXPERT_VMEM_LIMIT = 48 * 1024 * 1024

PROJ_TM = 512
ATT_TQ = 512
FOX_TQ = 512
MOE_TM = 512
COMBINE_TM = 256
MLA_SLAB = LANES
VT_ROWS = HEAD_DIM + 16
SC_CORES = 2
SC_SUBCORES = 16
SC_WORKERS = SC_CORES * SC_SUBCORES
SC_CHUNK = 32
SC_LANES = 16
SC_SCAN = 8192


def _cparams(n_axes):
    return pltpu.CompilerParams(dimension_semantics=("arbitrary",) * n_axes,
                                vmem_limit_bytes=VMEM_LIMIT)


def _split3(x):
    hi = x.astype(BF16)
    r1 = x - hi.astype(F32)
    mid = r1.astype(BF16)
    lo = (r1 - mid.astype(F32)).astype(BF16)
    return hi, mid, lo


def _pack_bf16_pairs(v):
    bits = pltpu.bitcast(v.astype(F32), jnp.uint32)
    half = v.shape[1] // 2
    return (bits[:, :half] >> 16) | bits[:, half:]


def _unpack_bf16_pairs(words):
    return jnp.concatenate([pltpu.bitcast(words << 16, F32),
                            pltpu.bitcast(words & jnp.uint32(0xFFFF0000), F32)], axis=1)


def _layer_norm(y, g, b):
    mu = jnp.mean(y, axis=-1, keepdims=True)
    yc = y - mu
    var = jnp.mean(yc * yc, axis=-1, keepdims=True)
    return yc * lax.rsqrt(var + LN_EPS) * g + b


def _rms_norm(y, g):
    return y * lax.rsqrt(jnp.mean(y * y, axis=-1, keepdims=True) + RMS_EPS) * g


def _store_transposed(dst_ref, val):
    for s in range(val.shape[1] // LANES):
        sl = slice(s * LANES, (s + 1) * LANES)
        dst_ref[0, sl, :] = val[:, sl].T.astype(dst_ref.dtype)


def _store_values_transposed(vt_ref, val):
    tm = val.shape[0]
    ones = jnp.ones((VT_ROWS - HEAD_DIM, tm), vt_ref.dtype)
    for s in range(val.shape[1] // LANES):
        pair_t = val[:, s * LANES:(s + 1) * LANES].T.astype(vt_ref.dtype)
        for h in range(2):
            r0 = (2 * s + h) * VT_ROWS
            vt_ref[0, r0:r0 + HEAD_DIM, :] = pair_t[h * HEAD_DIM:(h + 1) * HEAD_DIM]
            vt_ref[0, r0 + HEAD_DIM:r0 + VT_ROWS, :] = ones


def _fox_proj_kernel(x_ref, w_ref, bf_ref, seg_ref, qt_ref, k_ref, vt_ref, mqt_ref, cum_ref,
                     cumt_ref, cfirst_ref, clast_ref, knorm_ref, carry_ref, knmax_ref, *,
                     tiles_per_batch):
    t = pl.program_id(0)

    @pl.when(t % tiles_per_batch == 0)
    def _():
        carry_ref[...] = jnp.zeros_like(carry_ref)
        knmax_ref[...] = jnp.zeros_like(knmax_ref)

    tm = x_ref.shape[0]
    proj = jnp.dot(x_ref[...].astype(BF16), w_ref[...], preferred_element_type=F32)
    qscale = HEAD_DIM ** -0.5 * LOG2E
    _store_transposed(qt_ref, proj[:, :FOX_WIDTH] * qscale)
    kb = proj[:, FOX_WIDTH:2 * FOX_WIDTH].astype(BF16)
    k_ref[...] = kb
    _store_values_transposed(vt_ref, proj[:, 2 * FOX_WIDTH:3 * FOX_WIDTH])
    kf = kb.astype(F32)
    ksq = jnp.dot((kf * kf).astype(BF16), seg_ref[...], preferred_element_type=F32)
    n_sub = tm // FOX_TQ
    for sub in range(n_sub):
        tile_max = jnp.sqrt(jnp.max(ksq[sub * FOX_TQ:(sub + 1) * FOX_TQ], axis=0, keepdims=True))
        knmax_ref[...] = jnp.maximum(knmax_ref[...], tile_max)
        knorm_ref[sub] = knmax_ref[...]
    f = proj[:, 3 * FOX_WIDTH:3 * FOX_WIDTH + LANES] + bf_ref[...]
    _store_transposed(mqt_ref, proj[:, 3 * FOX_WIDTH + LANES:] * qscale)
    log_f = jnp.minimum(f, 0.0) - jnp.log1p(jnp.exp(-jnp.abs(f)))
    row = lax.broadcasted_iota(jnp.int32, (tm, tm), 0)
    col = lax.broadcasted_iota(jnp.int32, (tm, tm), 1)
    tri = jnp.where(row >= col, 1.0, 0.0).astype(BF16)
    hi, mid, lo = _split3(log_f)
    cum = (jnp.dot(tri, hi, preferred_element_type=F32)
           + jnp.dot(tri, mid, preferred_element_type=F32)
           + jnp.dot(tri, lo, preferred_element_type=F32)) + carry_ref[...]
    carry_ref[...] = cum[tm - 1:tm, :]
    cum2 = cum * LOG2E
    cum_ref[...] = cum2
    cumt_ref[0] = cum2.T[:16, :]
    for sub in range(n_sub):
        cfirst_ref[sub] = cum2[sub * FOX_TQ:sub * FOX_TQ + 1, :]
        clast_ref[sub] = cum2[(sub + 1) * FOX_TQ - 1:(sub + 1) * FOX_TQ, :]


def _fox_proj(x2d, w, bf, seq):
    t_tokens = x2d.shape[0]
    tm = PROJ_TM
    nb = t_tokens // seq
    n = w.shape[1]
    tiles_per_batch = seq // tm
    n_tiles = t_tokens // tm
    seg = (jnp.arange(FOX_WIDTH)[:, None] // HEAD_DIM == jnp.arange(LANES)[None, :]).astype(BF16)
    row_spec = lambda width: pl.BlockSpec((tm, width), lambda i: (i, 0))
    t_spec = lambda rows: pl.BlockSpec(
        (1, rows, tm), lambda i: (i // tiles_per_batch, 0, i % tiles_per_batch))
    n_sub = tm // FOX_TQ
    stat_spec = pl.BlockSpec((n_sub, 1, LANES), lambda i: (i, 0, 0))
    stat_shape = jax.ShapeDtypeStruct((n_tiles * n_sub, 1, LANES), F32)
    return pl.pallas_call(
        functools.partial(_fox_proj_kernel, tiles_per_batch=tiles_per_batch),
        grid=(n_tiles,),
        in_specs=[row_spec(D_MODEL),
                  pl.BlockSpec((D_MODEL, n), lambda i: (0, 0)),
                  pl.BlockSpec((1, LANES), lambda i: (0, 0)),
                  pl.BlockSpec((FOX_WIDTH, LANES), lambda i: (0, 0))],
        out_specs=[t_spec(FOX_WIDTH), row_spec(FOX_WIDTH), t_spec(FOX_HEADS * VT_ROWS),
                   t_spec(MEM_WIDTH), row_spec(LANES), t_spec(16),
                   stat_spec, stat_spec, stat_spec],
        out_shape=[jax.ShapeDtypeStruct((nb, FOX_WIDTH, seq), BF16),
                   jax.ShapeDtypeStruct((t_tokens, FOX_WIDTH), BF16),
                   jax.ShapeDtypeStruct((nb, FOX_HEADS * VT_ROWS, seq), BF16),
                   jax.ShapeDtypeStruct((nb, MEM_WIDTH, seq), BF16),
                   jax.ShapeDtypeStruct((t_tokens, LANES), F32),
                   jax.ShapeDtypeStruct((nb, 16, seq), F32),
                   stat_shape, stat_shape, stat_shape],
        scratch_shapes=[pltpu.VMEM((1, LANES), F32), pltpu.VMEM((1, LANES), F32)],
        compiler_params=_cparams(1),
        name="fox_proj",
    )(x2d, w, bf, seg)


def _rope_tables(pos_ref, invf_ref):
    ang = pos_ref[...].astype(F32) * invf_ref[...]
    cos = jnp.cos(ang)
    sin = jnp.sin(ang)
    lane = lax.broadcasted_iota(jnp.int32, ang.shape, 1)
    half = QK_ROPE // 2
    in_x1 = (lane >= QK_NOPE) & (lane < QK_NOPE + half)
    in_x2 = (lane >= QK_NOPE + half) & (lane < QK_NOPE + QK_ROPE)
    c_tab = jnp.where(lane < QK_NOPE, 1.0, jnp.where(in_x1 | in_x2, cos, 0.0))
    s_up = jnp.where(in_x2, sin, 0.0)
    s_dn = jnp.where(in_x1, -sin, 0.0)
    return c_tab, s_up, s_dn


def _rope_slab(slab, tables):
    c_tab, s_up, s_dn = tables
    half = QK_ROPE // 2
    up = pltpu.roll(slab, half, 1)
    dn = pltpu.roll(slab, LANES - half, 1)
    return slab * c_tab + up * s_up + dn * s_dn


def _mla_proj_kernel(x_ref, pos_ref, invf_ref, w_ref, gq_ref, wuq_ref, gkv_ref, wk_ref, wv_ref,
                     qt_ref, mqt_ref, k_ref, vt_ref):
    proj = jnp.dot(x_ref[...].astype(BF16), w_ref[...], preferred_element_type=F32)
    tables = _rope_tables(pos_ref, invf_ref)
    c_q = _rms_norm(proj[:, :Q_LORA], gq_ref[...])
    _store_transposed(mqt_ref, proj[:, Q_LORA:Q_LORA + MEM_WIDTH] * (HEAD_DIM ** -0.5 * LOG2E))
    kv_off = Q_LORA + MEM_WIDTH
    c_kv = _rms_norm(proj[:, kv_off:kv_off + KV_LORA], gkv_ref[...])
    kr = _rope_slab(proj[:, kv_off + KV_LORA:], tables)
    q = jnp.dot(c_q.astype(BF16), wuq_ref[...], preferred_element_type=F32)
    kn = jnp.dot(c_kv.astype(BF16), wk_ref[...], preferred_element_type=F32)
    qscale = (QK_NOPE + QK_ROPE) ** -0.5 * LOG2E
    for h in range(MLA_HEADS):
        sl = slice(h * MLA_SLAB, (h + 1) * MLA_SLAB)
        qt_ref[0, sl, :] = (_rope_slab(q[:, sl], tables) * qscale).T.astype(BF16)
        k_ref[:, sl] = (kn[:, sl] + kr).astype(BF16)
    _store_values_transposed(vt_ref, jnp.dot(c_kv.astype(BF16), wv_ref[...],
                                             preferred_element_type=F32))


def _mla_proj(x2d, pos2d, invf, w, gq, wuq, gkv, wk, wv, seq):
    t_tokens = x2d.shape[0]
    tm = PROJ_TM
    nb = t_tokens // seq
    tiles_per_batch = seq // tm
    row_spec = lambda width: pl.BlockSpec((tm, width), lambda i: (i, 0))
    t_spec = lambda rows: pl.BlockSpec(
        (1, rows, tm), lambda i: (i // tiles_per_batch, 0, i % tiles_per_batch))
    full = lambda a: pl.BlockSpec(a.shape, lambda i: (0, 0))
    slabs = MLA_HEADS * MLA_SLAB
    return pl.pallas_call(
        _mla_proj_kernel,
        grid=(t_tokens // tm,),
        in_specs=[row_spec(D_MODEL), row_spec(1), full(invf), full(w), full(gq), full(wuq),
                  full(gkv), full(wk), full(wv)],
        out_specs=[t_spec(slabs), t_spec(MEM_WIDTH), row_spec(slabs),
                   t_spec(MLA_HEADS * VT_ROWS)],
        out_shape=[jax.ShapeDtypeStruct((nb, slabs, seq), BF16),
                   jax.ShapeDtypeStruct((nb, MEM_WIDTH, seq), BF16),
                   jax.ShapeDtypeStruct((t_tokens, slabs), BF16),
                   jax.ShapeDtypeStruct((nb, MLA_HEADS * VT_ROWS, seq), BF16)],
        compiler_params=_cparams(1),
        name="mla_proj",
    )(x2d, pos2d, invf, w, gq, wuq, gkv, wk, wv)


SKIP_LOG2 = -160.0
NORM_SLACK = 1.02


def _causal_attn_kernel(*refs, tq, n_tiles, fox, slab, long_body):
    if fox:
        (cf_ref, cl_ref, kn_ref, qt_ref, k_ref, vt_ref, cq_ref, ck_ref,
         o_ref, m_ref, acc_ref, s_ref) = refs
    else:
        qt_ref, k_ref, vt_ref, o_ref, m_ref, acc_ref, s_ref = refs
    b = pl.program_id(0)
    pair = pl.program_id(1)
    i = pl.program_id(2)
    m_ref[...] = jnp.full(m_ref.shape, NEG_INF, F32)
    acc_ref[...] = jnp.zeros(acc_ref.shape, F32)
    qt = qt_ref[0]
    if slab:
        qth = [qt[:MLA_SLAB], qt[MLA_SLAB:]]
    else:
        rowi = lax.broadcasted_iota(jnp.int32, qt.shape, 0)
        zero = jnp.zeros_like(qt)
        qth = [jnp.where(rowi < HEAD_DIM, qt, zero), jnp.where(rowi >= HEAD_DIM, qt, zero)]
    if fox:
        lane = lax.broadcasted_iota(jnp.int32, (1, LANES), 1)
        cq = [cq_ref[0, pl.ds(2 * pair + h, 1), :] for h in range(2)]

    def logits_to(slot, j):
        off = pl.multiple_of(j * tq, tq)
        kc = k_ref[0, pl.ds(off, tq), :]
        for h in range(2):
            kh = kc[:, h * MLA_SLAB:(h + 1) * MLA_SLAB] if slab else kc
            s = jnp.dot(kh, qth[h], preferred_element_type=F32)
            if fox:
                ck_blk = ck_ref[0, pl.ds(off, tq), :]
                ck = jnp.sum(jnp.where(lane == 2 * pair + h, ck_blk, 0.0), axis=1, keepdims=True)
                s = s + cq[h] - ck
            s_ref[slot, h] = s

    def softmax_pv(slot, j, causal_mask):
        off = pl.multiple_of(j * tq, tq)
        for h in range(2):
            s = s_ref[slot, h]
            if causal_mask is not None:
                s = jnp.where(causal_mask, s, NEG_INF)
            m_prev = m_ref[h]
            m_new = jnp.maximum(m_prev, jnp.max(s, axis=0, keepdims=True))
            alpha = jnp.exp2(m_prev - m_new)
            p = jnp.exp2(s - m_new).astype(BF16)
            vth = vt_ref[0, h * VT_ROWS:(h + 1) * VT_ROWS, pl.ds(off, tq)]
            acc_ref[h] = acc_ref[h] * alpha + jnp.dot(vth, p, preferred_element_type=F32)
            m_ref[h] = m_new

    if fox:
        base = b * n_tiles
        qn = []
        for h in range(2):
            qf = qth[h].astype(F32)
            qn.append(jnp.sqrt(jnp.max(jnp.sum(qf * qf, axis=0, keepdims=True))) * NORM_SLACK)

        top = [2.0 * qn[h] * kn_ref[base + i, 2 * pair + h] + cf_ref[base + i, 2 * pair + h]
               for h in range(2)]

        def live(j):
            ub = [top[h] - cl_ref[base + jnp.maximum(j, 0), 2 * pair + h] for h in range(2)]
            return (j >= 0) & (jnp.maximum(ub[0], ub[1]) > SKIP_LOG2)

        j0 = lax.while_loop(live, lambda j: j - 1, i - 1) + 1
    else:
        j0 = 0
    n_before = i - j0
    odd = (n_before % 2) == 1

    @pl.when(odd)
    def _():
        logits_to(1, j0)
        logits_to(0, j0 + 1)
        softmax_pv(1, j0, None)

    @pl.when(jnp.logical_not(odd))
    def _():
        logits_to(0, j0)

    def two_steps(j):
        logits_to(1, j + 1)
        softmax_pv(0, j, None)
        logits_to(0, j + 2)
        softmax_pv(1, j + 1, None)

    j_even = j0 + (n_before % 2)
    n_two = n_before // 2
    if long_body:
        @pl.when((n_two % 2) == 1)
        def _():
            two_steps(j_even)

        j_quad = j_even + 2 * (n_two % 2)

        def body(t, carry):
            two_steps(j_quad + 4 * t)
            two_steps(j_quad + 4 * t + 2)
            return carry

        lax.fori_loop(0, n_two // 2, body, 0)
    else:
        def body(t, carry):
            two_steps(j_even + 2 * t)
            return carry

        lax.fori_loop(0, n_two, body, 0)
    key = lax.broadcasted_iota(jnp.int32, (tq, tq), 0)
    qry = lax.broadcasted_iota(jnp.int32, (tq, tq), 1)
    softmax_pv(0, i, key <= qry)
    out_t = jnp.concatenate(
        [acc_ref[h, :HEAD_DIM] * (1.0 / acc_ref[h, HEAD_DIM:HEAD_DIM + 1]) for h in range(2)],
        axis=0)
    o_ref[0] = out_t.T.astype(o_ref.dtype)


def _causal_attention(qt, k, vt, fox_args, *, slab):
    nb, seq, _ = k.shape
    fox = fox_args is not None
    tq = FOX_TQ if fox else ATT_TQ
    n_tiles = seq // tq
    n_pairs = FOX_HEADS // 2
    rows = 2 * MLA_SLAB if slab else LANES
    in_specs = [pl.BlockSpec((1, rows, tq), lambda b, p, i: (b, p, i)),
                pl.BlockSpec((1, seq, rows), lambda b, p, i: (b, 0, p)),
                pl.BlockSpec((1, 2 * VT_ROWS, seq), lambda b, p, i: (b, p, 0))]
    args = [qt, k, vt]
    if fox:
        cum, cumt, cfirst, clast, knorm = fox_args
        smem = pl.BlockSpec(memory_space=pltpu.SMEM)
        in_specs = [smem, smem, smem] + in_specs + [
            pl.BlockSpec((1, 16, tq), lambda b, p, i: (b, 0, i)),
            pl.BlockSpec((1, seq, LANES), lambda b, p, i: (b, 0, 0))]
        args = [cfirst, clast, knorm] + args + [cumt, cum]
    return pl.pallas_call(
        functools.partial(_causal_attn_kernel, tq=tq, n_tiles=n_tiles, fox=fox, slab=slab,
                          long_body=not fox),
        grid=(nb, n_pairs, n_tiles),
        in_specs=in_specs,
        out_specs=pl.BlockSpec((1, tq, LANES), lambda b, p, i: (b, i, p)),
        out_shape=jax.ShapeDtypeStruct((nb, seq, n_pairs * LANES), BF16),
        scratch_shapes=[pltpu.VMEM((2, 1, tq), F32), pltpu.VMEM((2, VT_ROWS, tq), F32),
                        pltpu.VMEM((2, 2, tq, tq), F32)],
        compiler_params=_cparams(3),
        name="fox_attention" if fox else "mla_attention",
    )(*args)


def _mem_proj_kernel(mem_ref, w_ref, mk_ref, mvt_ref):
    mkv = jnp.dot(mem_ref[...].astype(BF16), w_ref[...], preferred_element_type=F32)
    mk_ref[0] = mkv[:, :MEM_WIDTH].astype(BF16)
    _store_values_transposed(mvt_ref, mkv[:, MEM_WIDTH:])


def _mem_proj(mem2d, w, nb):
    return pl.pallas_call(
        _mem_proj_kernel,
        grid=(nb,),
        in_specs=[pl.BlockSpec((N_MEM, D_MODEL), lambda b: (b, 0)),
                  pl.BlockSpec(w.shape, lambda b: (0, 0))],
        out_specs=[pl.BlockSpec((1, N_MEM, MEM_WIDTH), lambda b: (b, 0, 0)),
                   pl.BlockSpec((1, MEM_HEADS * VT_ROWS, N_MEM), lambda b: (b, 0, 0))],
        out_shape=[jax.ShapeDtypeStruct((nb, N_MEM, MEM_WIDTH), BF16),
                   jax.ShapeDtypeStruct((nb, MEM_HEADS * VT_ROWS, N_MEM), BF16)],
        compiler_params=_cparams(1),
        name="mem_proj",
    )(mem2d, w)


def _mem_attn_kernel(qt_ref, k_ref, vt_ref, o_ref):
    qt = qt_ref[0]
    kc = k_ref[0]
    for pair in range(MEM_HEADS // 2):
        qt_p = qt[pair * LANES:(pair + 1) * LANES]
        k_p = kc[:, pair * LANES:(pair + 1) * LANES]
        rowi = lax.broadcasted_iota(jnp.int32, qt_p.shape, 0)
        zero = jnp.zeros_like(qt_p)
        outs = []
        for h in range(2):
            mine = (rowi < HEAD_DIM) if h == 0 else (rowi >= HEAD_DIM)
            s = jnp.dot(k_p, jnp.where(mine, qt_p, zero), preferred_element_type=F32)
            p = jnp.exp2(s - jnp.max(s, axis=0, keepdims=True)).astype(BF16)
            r0 = (2 * pair + h) * VT_ROWS
            acc = jnp.dot(vt_ref[0, r0:r0 + VT_ROWS, :], p, preferred_element_type=F32)
            outs.append(acc[:HEAD_DIM] * (1.0 / acc[HEAD_DIM:HEAD_DIM + 1]))
        o_ref[0, :, pair * LANES:(pair + 1) * LANES] = (
            jnp.concatenate(outs, axis=0).T.astype(o_ref.dtype))


def _memory_attention(mqt, mk, mvt):
    nb, _, seq = mqt.shape
    tq = ATT_TQ
    return pl.pallas_call(
        _mem_attn_kernel,
        grid=(nb, seq // tq),
        in_specs=[pl.BlockSpec((1, MEM_WIDTH, tq), lambda b, i: (b, 0, i)),
                  pl.BlockSpec((1, N_MEM, MEM_WIDTH), lambda b, i: (b, 0, 0)),
                  pl.BlockSpec((1, MEM_HEADS * VT_ROWS, N_MEM), lambda b, i: (b, 0, 0))],
        out_specs=pl.BlockSpec((1, tq, MEM_WIDTH), lambda b, i: (b, i, 0)),
        out_shape=jax.ShapeDtypeStruct((nb, seq, MEM_WIDTH), BF16),
        compiler_params=_cparams(2),
        name="memory_attention",
    )(mqt, mk, mvt)


def _post_attn_kernel(x_ref, att_ref, memo_ref, wa_ref, wm_ref, g_ref, b_ref, wr_ref, br_ref,
                      x1_ref, x1p_ref, idx_ref, rank_ref, gate_ref, cnt_ref, carry_ref):
    t = pl.program_id(0)

    @pl.when(t == 0)
    def _():
        carry_ref[...] = jnp.zeros_like(carry_ref)

    tm = x_ref.shape[0]
    mix = (jnp.dot(att_ref[...], wa_ref[...], preferred_element_type=F32)
           + jnp.dot(memo_ref[...], wm_ref[...], preferred_element_type=F32))
    x1 = _layer_norm(DEEPNORM_ALPHA * x_ref[...] + mix, g_ref[...], b_ref[...])
    x1_ref[...] = x1
    x1b = x1.astype(BF16)
    x1p_ref[...] = _pack_bf16_pairs(x1b)
    logits = jnp.dot(x1b, wr_ref[...], preferred_element_type=F32) + br_ref[...]
    lane = lax.broadcasted_iota(jnp.int32, (tm, LANES), 1)
    work = jnp.where(lane < N_EXPERTS, logits, -jnp.inf)
    idxs, vals = [], []
    onehot = jnp.zeros((tm, LANES), F32)
    for _ in range(TOP_K):
        best = jnp.max(work, axis=1, keepdims=True)
        where_best = jnp.min(jnp.where(work == best, lane, LANES), axis=1, keepdims=True)
        hit = lane == where_best
        onehot = jnp.where(hit, 1.0, onehot)
        work = jnp.where(hit, -jnp.inf, work)
        idxs.append(where_best)
        vals.append(best)
    exps = [jnp.exp(v - vals[0]) for v in vals]
    denom = exps[0] + exps[1] + exps[2] + exps[3]
    row = lax.broadcasted_iota(jnp.int32, (tm, tm), 0)
    col = lax.broadcasted_iota(jnp.int32, (tm, tm), 1)
    strict = jnp.where(row > col, 1.0, 0.0).astype(BF16)
    before = jnp.dot(strict, onehot.astype(BF16), preferred_element_type=F32) + carry_ref[...]
    idx_out = jnp.zeros((tm, LANES), jnp.int32)
    rank_out = jnp.zeros((tm, LANES), jnp.int32)
    gate_out = jnp.zeros((tm, LANES), F32)
    for r in range(TOP_K):
        rank_r = jnp.sum(jnp.where(lane == idxs[r], before, 0.0), axis=1, keepdims=True)
        idx_out = jnp.where(lane == r, idxs[r], idx_out)
        rank_out = jnp.where(lane == r, rank_r.astype(jnp.int32), rank_out)
        gate_out = jnp.where(lane == r, exps[r] / denom, gate_out)
    idx_ref[...] = idx_out
    rank_ref[...] = rank_out
    gate_ref[...] = gate_out
    total = carry_ref[...] + jnp.sum(onehot, axis=0, keepdims=True)
    carry_ref[...] = total
    cnt_ref[...] = jnp.broadcast_to(total, cnt_ref.shape)


def _post_attn(x2d, att, memo, wa, wm, g, b, wr, br):
    t_tokens = x2d.shape[0]
    tm = PROJ_TM
    row_spec = lambda width: pl.BlockSpec((tm, width), lambda i: (i, 0))
    full = lambda a: pl.BlockSpec(a.shape, lambda i: (0, 0))
    return pl.pallas_call(
        _post_attn_kernel,
        grid=(t_tokens // tm,),
        in_specs=[row_spec(D_MODEL), row_spec(att.shape[1]), row_spec(MEM_WIDTH),
                  full(wa), full(wm), full(g), full(b), full(wr), full(br)],
        out_specs=[row_spec(D_MODEL), row_spec(D_MODEL // 2), row_spec(LANES), row_spec(LANES),
                   row_spec(LANES), pl.BlockSpec((8, LANES), lambda i: (0, 0))],
        out_shape=[jax.ShapeDtypeStruct((t_tokens, D_MODEL), F32),
                   jax.ShapeDtypeStruct((t_tokens, D_MODEL // 2), jnp.uint32),
                   jax.ShapeDtypeStruct((t_tokens, LANES), jnp.int32),
                   jax.ShapeDtypeStruct((t_tokens, LANES), jnp.int32),
                   jax.ShapeDtypeStruct((t_tokens, LANES), F32),
                   jax.ShapeDtypeStruct((8, LANES), F32)],
        scratch_shapes=[pltpu.VMEM((1, LANES), F32)],
        compiler_params=_cparams(1),
        name="outproj_ln_router",
    )(x2d, att, memo, wa, wm, g, b, wr, br)


def _expert_kernel(be_ref, nu_ref, x_ref, wgu_hbm, bgu_ref, wdn_hbm, bdn_ref, *rest,
                   n_chunks, blk_lo, n_blocks, layer, has_prev):
    o_ref, wgu_st, wdn_st, wgu_bf, wdn_bf, h_ref, sem = rest[1:] if has_prev else rest
    step = pl.program_id(0)
    blk = step + blk_lo
    live_end = jnp.minimum(blk_lo + pl.num_programs(0), nu_ref[0])
    expert = be_ref[blk]

    def fetch(e):
        return (pltpu.make_async_copy(wgu_hbm.at[layer, e], wgu_st, sem.at[0]),
                pltpu.make_async_copy(wdn_hbm.at[layer, e], wdn_st, sem.at[1]))

    @pl.when(((step == 0) | (expert != be_ref[jnp.maximum(blk - 1, 0)])) & (blk < live_end))
    def _():
        @pl.when(step == 0)
        def _():
            for copy in fetch(expert):
                copy.start()

        for copy in fetch(expert):
            copy.wait()
        wgu_bf[...] = wgu_st[...].astype(BF16)
        wdn_bf[...] = wdn_st[...].astype(BF16)
        nxt = lax.while_loop(
            lambda j: (j < live_end) & (be_ref[jnp.minimum(j, n_blocks - 1)] == expert),
            lambda j: j + 1, blk + 1)

        @pl.when(nxt < live_end)
        def _():
            for copy in fetch(be_ref[jnp.minimum(nxt, n_blocks - 1)]):
                copy.start()

    @pl.when(blk < nu_ref[0])
    def _():
        x = _unpack_bf16_pairs(x_ref[...]).astype(BF16)
        cw = D_EXPERT // n_chunks
        for c in range(n_chunks):
            gs = slice(c * cw, (c + 1) * cw)
            us = slice(D_EXPERT + c * cw, D_EXPERT + (c + 1) * cw)
            g = jnp.dot(x, wgu_bf[:, gs], preferred_element_type=F32) + bgu_ref[0, 0, :, gs]
            u = jnp.dot(x, wgu_bf[:, us], preferred_element_type=F32) + bgu_ref[0, 0, :, us]
            g = jnp.minimum(g, SWIGLU_LIMIT)
            u = jnp.clip(u, -SWIGLU_LIMIT, SWIGLU_LIMIT)
            h_ref[:, gs] = ((u + 1.0) * (g * jax.nn.sigmoid(SWIGLU_ALPHA * g))).astype(BF16)
        y = jnp.dot(h_ref[...], wdn_bf[...], preferred_element_type=F32) + bdn_ref[0, 0]
        o_ref[...] = _pack_bf16_pairs(y.astype(BF16))

    @pl.when(blk >= nu_ref[0])
    def _():
        o_ref[...] = jnp.zeros_like(o_ref)


def _experts(block_expert, n_used, xb, blk_lo, p_rows, y_prev, layer, wgu, bgu, wdn, bdn):
    tm = MOE_TM
    has_prev = y_prev is not None
    b_map = lambda i, be, nu: (layer, be[i + blk_lo], 0, 0)
    in_specs = [pl.BlockSpec((tm, D_MODEL // 2), lambda i, be, nu: (i, 0)),
                pl.BlockSpec(memory_space=pl.ANY),
                pl.BlockSpec((1, 1, 1, 2 * D_EXPERT), b_map),
                pl.BlockSpec(memory_space=pl.ANY),
                pl.BlockSpec((1, 1, 1, D_MODEL), b_map)]
    args = [block_expert, n_used, xb, wgu, bgu, wdn, bdn]
    if has_prev:
        in_specs.append(pl.BlockSpec(memory_space=pl.ANY))
        args.append(y_prev)
    grid_spec = pltpu.PrefetchScalarGridSpec(
        num_scalar_prefetch=2,
        grid=(xb.shape[0] // tm,),
        in_specs=in_specs,
        out_specs=pl.BlockSpec((tm, D_MODEL // 2), lambda i, be, nu: (i + blk_lo, 0)),
        scratch_shapes=[pltpu.VMEM((D_MODEL, 2 * D_EXPERT), F32),
                        pltpu.VMEM((D_EXPERT, D_MODEL), F32),
                        pltpu.VMEM((D_MODEL, 2 * D_EXPERT), BF16),
                        pltpu.VMEM((D_EXPERT, D_MODEL), BF16),
                        pltpu.VMEM((tm, D_EXPERT), BF16),
                        pltpu.SemaphoreType.DMA((2,))],
    )
    return pl.pallas_call(
        functools.partial(_expert_kernel, n_chunks=4, blk_lo=blk_lo, n_blocks=p_rows // tm,
                          layer=layer, has_prev=has_prev),
        grid_spec=grid_spec,
        out_shape=jax.ShapeDtypeStruct((p_rows, D_MODEL // 2), jnp.uint32),
        input_output_aliases={len(args) - 1: 0} if has_prev else {},
        compiler_params=pltpu.CompilerParams(dimension_semantics=("arbitrary",),
                                             vmem_limit_bytes=EXPERT_VMEM_LIMIT),
        name="experts",
    )(*args)


def _sc_move_rows(table_hbm, idx_v, out_hbm, base, n_chunks, rows_v, gsem, wsem):
    def gather(c, slot):
        off = pl.multiple_of(c * SC_CHUNK, SC_CHUNK)
        return pltpu.make_async_copy(table_hbm.at[idx_v.at[pl.ds(off, SC_CHUNK)]],
                                     rows_v.at[slot], gsem.at[slot])

    def put(c, slot):
        off = pl.multiple_of(c * SC_CHUNK, SC_CHUNK)
        return pltpu.make_async_copy(rows_v.at[slot], out_hbm.at[pl.ds(base + off, SC_CHUNK)],
                                     wsem.at[slot])

    gather(0, 0).start()
    gather(1, 1).start()

    @pl.loop(0, n_chunks, step=2)
    def _(c):
        for slot in range(2):
            gather(c + slot, slot).wait()
            put(c + slot, slot).start()
        for slot in range(2):
            put(c + slot, slot).wait()

            @pl.when(c + 2 + slot < n_chunks)
            def _():
                gather(c + 2 + slot, slot).start()


def _sc_gather_rows(table, idx):
    n_idx = idx.shape[0]
    width = table.shape[1]
    per_worker = n_idx // SC_WORKERS
    n_chunks = per_worker // SC_CHUNK
    mesh = plsc.VectorSubcoreMesh(core_axis_name="c", subcore_axis_name="s",
                                  num_cores=SC_CORES, num_subcores=SC_SUBCORES)

    def body(table_hbm, idx_hbm, out_hbm, idx_v, rows_v, gsem, wsem):
        wid = lax.axis_index("s") * SC_CORES + lax.axis_index("c")
        base = wid * per_worker
        pltpu.sync_copy(idx_hbm.at[pl.ds(base, per_worker)], idx_v)
        _sc_move_rows(table_hbm, idx_v, out_hbm, base, n_chunks, rows_v, gsem, wsem)

    return pl.kernel(
        body,
        out_type=jax.ShapeDtypeStruct((n_idx, width), table.dtype),
        mesh=mesh,
        scratch_types=[pltpu.VMEM((per_worker,), jnp.int32),
                       pltpu.VMEM((2, SC_CHUNK, width), table.dtype),
                       pltpu.SemaphoreType.DMA((2,)),
                       pltpu.SemaphoreType.DMA((2,))],
        name="sc_gather_rows",
    )(table, idx)


def _sc_dispatch_rows(table, dest, slot_lo, n_slots):
    n_tok, width = table.shape
    n_pairs = dest.shape[0]
    per_worker = n_slots // SC_WORKERS
    n_chunks = per_worker // SC_CHUNK
    n_scan = n_pairs // SC_SCAN
    mesh = plsc.VectorSubcoreMesh(core_axis_name="c", subcore_axis_name="s",
                                  num_cores=SC_CORES, num_subcores=SC_SUBCORES)

    def body(table_hbm, dest_hbm, out_hbm, tok_v, dest_v, rows_v, gsem, wsem):
        wid = lax.axis_index("s") * SC_CORES + lax.axis_index("c")
        base = wid * per_worker
        first = slot_lo + base
        lane = lax.iota(jnp.int32, SC_LANES)

        @pl.loop(0, per_worker // SC_LANES)
        def _(i):
            off = pl.multiple_of(i * SC_LANES, SC_LANES)
            tok_v[pl.ds(off, SC_LANES)] = lax.rem(first + off + lane, n_tok)

        @pl.loop(0, n_scan)
        def _(g):
            goff = pl.multiple_of(g * SC_SCAN, SC_SCAN)
            pltpu.sync_copy(dest_hbm.at[pl.ds(goff, SC_SCAN)], dest_v)

            @pl.loop(0, SC_SCAN // SC_LANES)
            def _(i):
                off = pl.multiple_of(i * SC_LANES, SC_LANES)
                local = dest_v[pl.ds(off, SC_LANES)] - first
                mine = (local >= 0) & (local < per_worker)
                pair = goff + off + lane
                plsc.store_scatter(tok_v, [jnp.where(mine, local, 0)], pair // TOP_K, mask=mine)

        _sc_move_rows(table_hbm, tok_v, out_hbm, base, n_chunks, rows_v, gsem, wsem)

    return pl.kernel(
        body,
        out_type=jax.ShapeDtypeStruct((n_slots, width), table.dtype),
        mesh=mesh,
        scratch_types=[pltpu.VMEM((per_worker,), jnp.int32),
                       pltpu.VMEM((SC_SCAN,), jnp.int32),
                       pltpu.VMEM((2, SC_CHUNK, width), table.dtype),
                       pltpu.SemaphoreType.DMA((2,)),
                       pltpu.SemaphoreType.DMA((2,))],
        compiler_params=pltpu.CompilerParams(needs_layout_passes=False),
        name="sc_dispatch_rows",
    )(table, dest)


def _combine_kernel(x1_ref, yg_ref, gate_ref, g_ref, b_ref, *rest):
    o_ref = rest[-1]
    gates = gate_ref[...]
    ffn = _unpack_bf16_pairs(yg_ref[0]) * gates[:, 0:1]
    for r in range(1, TOP_K):
        ffn = ffn + _unpack_bf16_pairs(yg_ref[r]) * gates[:, r:r + 1]
    o_ref[...] = _layer_norm(DEEPNORM_ALPHA * x1_ref[...] + ffn, g_ref[...], b_ref[...])


def _combine(x1, yg, gates, g, b, tok_lo, out_prev):
    t_tokens = x1.shape[0]
    tm = COMBINE_TM
    tile_lo = tok_lo // tm
    in_specs = [pl.BlockSpec((tm, D_MODEL), lambda i: (i + tile_lo, 0)),
                pl.BlockSpec((TOP_K, tm, D_MODEL // 2), lambda i: (0, i, 0)),
                pl.BlockSpec((tm, LANES), lambda i: (i + tile_lo, 0)),
                pl.BlockSpec((1, D_MODEL), lambda i: (0, 0)),
                pl.BlockSpec((1, D_MODEL), lambda i: (0, 0))]
    args = [x1, yg, gates, g, b]
    if out_prev is not None:
        in_specs.append(pl.BlockSpec(memory_space=pl.ANY))
        args.append(out_prev)
    return pl.pallas_call(
        _combine_kernel,
        grid=(yg.shape[1] // tm,),
        in_specs=in_specs,
        out_specs=pl.BlockSpec((tm, D_MODEL), lambda i: (i + tile_lo, 0)),
        out_shape=jax.ShapeDtypeStruct((t_tokens, D_MODEL), F32),
        input_output_aliases={len(args) - 1: 0} if out_prev is not None else {},
        compiler_params=_cparams(1),
        name="combine_ln",
    )(*args)


def _moe(x1, x1p, idx, rank, gates, counts, layer, wgu, bgu, wdn, bdn, g, b):
    t_tokens = x1.shape[0]
    tm = MOE_TM
    counts = counts[0, :N_EXPERTS].astype(jnp.int32)
    padded = ((counts + tm - 1) // tm) * tm
    pend = jnp.cumsum(padded)
    pstart = pend - padded
    idx4 = idx[:, :TOP_K]
    dest = (pstart[idx4] + rank[:, :TOP_K]).reshape(-1)
    n_blocks = (t_tokens * TOP_K) // tm + N_EXPERTS
    p_rows = n_blocks * tm
    block_start = jnp.arange(n_blocks, dtype=jnp.int32) * tm
    block_expert = jnp.minimum(
        jnp.sum((pend[None, :] <= block_start[:, None]).astype(jnp.int32), axis=1), N_EXPERTS - 1)
    n_used = (pend[-1:] // tm).astype(jnp.int32)
    half_rows = p_rows // 2
    y = None
    for part in range(2):
        xb = _sc_dispatch_rows(x1p, dest, part * half_rows, half_rows)
        y = _experts(block_expert, n_used, xb, part * (n_blocks // 2), p_rows, y,
                     layer, wgu, bgu, wdn, bdn)
    dest2 = dest.reshape(t_tokens, TOP_K)
    half_tok = t_tokens // 2
    out = None
    for part in range(2):
        by_choice = dest2[part * half_tok:(part + 1) * half_tok].T.reshape(-1)
        yg = _sc_gather_rows(y, by_choice).reshape(TOP_K, half_tok, D_MODEL // 2)
        out = _combine(x1, yg, gates, g, b, part * half_tok, out)
    return out


def _row(v, width=None):
    v = v.astype(F32).reshape(1, -1)
    if width is not None and v.shape[1] < width:
        v = jnp.pad(v, ((0, 0), (0, width - v.shape[1])))
    return v


def _pad_cols(w, width):
    return jnp.pad(w, ((0, 0), (0, width - w.shape[1])))


def kernel(x, mem, positions, a_w_in, a_b_f, a_w_out, b_w_in, b_g_q, b_w_uq, b_w_out,
           kv_w_dkv, kv_g, kv_w_ukv, mem_w_kv, ln_g, ln_b,
           moe_w_r, moe_b_r, moe_w_gu, moe_b_gu, moe_w_dn, moe_b_dn):
    nb, seq, d = x.shape
    t_tokens = nb * seq
    n_a = a_w_in.shape[0]
    x2d = x.reshape(t_tokens, d)
    mem2d = mem.reshape(nb * N_MEM, d)
    pos2d = positions.reshape(t_tokens, 1)
    half = QK_ROPE // 2
    inv_freq = ROPE_THETA ** (-jnp.arange(half, dtype=F32) * 2.0 / QK_ROPE)
    invf = jnp.zeros((1, LANES), F32)
    invf = invf.at[0, QK_NOPE:QK_NOPE + half].set(inv_freq)
    invf = invf.at[0, QK_NOPE + half:QK_NOPE + QK_ROPE].set(inv_freq)

    shared_kv = None
    for l in range(DEPTH):
        mk, mvt = _mem_proj(mem2d, mem_w_kv[l].astype(BF16), nb)
        if l < n_a:
            w_in = a_w_in[l]
            w = jnp.concatenate([w_in[:, :3 * FOX_WIDTH],
                                 _pad_cols(w_in[:, 3 * FOX_WIDTH:3 * FOX_WIDTH + FOX_HEADS], LANES),
                                 w_in[:, 3 * FOX_WIDTH + FOX_HEADS:]], axis=1).astype(BF16)
            qt, k, vt, mqt, cum, cumt, cfirst, clast, knorm = _fox_proj(
                x2d, w, _row(a_b_f[l], LANES), seq)
            stats = [s.reshape(-1, LANES) for s in (cfirst, clast, knorm)]
            att = _causal_attention(qt, k.reshape(nb, seq, -1), vt,
                                    [cum.reshape(nb, seq, LANES), cumt] + stats, slab=False)
            w_out = a_w_out[l]
        else:
            bl = l - n_a
            w = jnp.concatenate([b_w_in[bl], kv_w_dkv[:, :KV_LORA],
                                 jnp.zeros((d, QK_NOPE), F32), kv_w_dkv[:, KV_LORA:],
                                 jnp.zeros((d, LANES - QK_NOPE - QK_ROPE), F32)], axis=1).astype(BF16)
            wuq = b_w_uq[bl].reshape(Q_LORA, MLA_HEADS, QK_NOPE + QK_ROPE)
            wuq = jnp.pad(wuq, ((0, 0), (0, 0), (0, MLA_SLAB - QK_NOPE - QK_ROPE)))
            wuq = wuq.reshape(Q_LORA, MLA_HEADS * MLA_SLAB).astype(BF16)
            wukv = kv_w_ukv.reshape(KV_LORA, MLA_HEADS, QK_NOPE + V_DIM)
            wk = jnp.pad(wukv[:, :, :QK_NOPE], ((0, 0), (0, 0), (0, MLA_SLAB - QK_NOPE)))
            wk = wk.reshape(KV_LORA, MLA_HEADS * MLA_SLAB).astype(BF16)
            wv = wukv[:, :, QK_NOPE:].reshape(KV_LORA, MLA_V_WIDTH).astype(BF16)
            qt, mqt, k_new, vt_new = _mla_proj(x2d, pos2d, invf, w, _row(b_g_q[bl]), wuq,
                                              _row(kv_g), wk, wv, seq)
            if shared_kv is None:
                shared_kv = (k_new.reshape(nb, seq, -1), vt_new)
            att = _causal_attention(qt, shared_kv[0], shared_kv[1], None, slab=True)
            w_out = b_w_out[bl]
        memo = _memory_attention(mqt, mk, mvt)
        n_att = w_out.shape[0] - MEM_WIDTH
        x1, x1p, idx, rank, gates, counts = _post_attn(
            x2d, att.reshape(t_tokens, -1), memo.reshape(t_tokens, MEM_WIDTH),
            w_out[:n_att].astype(BF16), w_out[n_att:].astype(BF16),
            _row(ln_g[l, 0]), _row(ln_b[l, 0]),
            _pad_cols(moe_w_r[l], LANES).astype(BF16), _row(moe_b_r[l], LANES))
        x2d = _moe(x1, x1p, idx, rank, gates, counts, l,
                   moe_w_gu, moe_b_gu.reshape(DEPTH, N_EXPERTS, 1, -1),
                   moe_w_dn, moe_b_dn.reshape(DEPTH, N_EXPERTS, 1, -1),
                   _row(ln_g[l, 1]), _row(ln_b[l, 1]))
    return x2d.reshape(nb, seq, d)
```

```python
import functools
import math

import jax
import jax.numpy as jnp
from jax import lax
from jax.experimental import pallas as pl
from jax.experimental.pallas import tpu as pltpu
from jax.experimental.pallas import tpu_sc as plsc

F32 = jnp.float32
BF16 = jnp.bfloat16

D_MODEL = 1024
DEPTH = 2
N_MEM = 256
HEAD_DIM = 64
FOX_HEADS = 12
MEM_HEADS = 4
MLA_HEADS = 12
Q_LORA = 384
KV_LORA = 256
QK_NOPE = 64
QK_ROPE = 32
V_DIM = 64
ROPE_THETA = 10000.0
N_EXPERTS = 32
TOP_K = 4
D_EXPERT = D_MODEL
SWIGLU_LIMIT = 7.0
SWIGLU_ALPHA = 1.702
LN_EPS = 1e-5
RMS_EPS = 1e-6
NEG_INF = -1e30
DEEPNORM_ALPHA = (2 * DEPTH) ** 0.25
FOX_WIDTH = FOX_HEADS * HEAD_DIM
MEM_WIDTH = MEM_HEADS * HEAD_DIM
MLA_V_WIDTH = MLA_HEADS * V_DIM

LANES = 128
LOG2E = math.log2(math.e)
VMEM_LIMIT = 48 * 1024 * 1024
EXPERT_VMEM_LIMIT = 48 * 1024 * 1024

PROJ_TM = 512
ATT_TQ = 512
FOX_TQ = 512
MOE_TM = 512
COMBINE_TM = 256
MOE_SPLIT = 4
MLA_SLAB = LANES
VT_ROWS = HEAD_DIM + 16
SC_CORES = 2
SC_SUBCORES = 16
SC_WORKERS = SC_CORES * SC_SUBCORES
SC_CHUNK = 32
SC_LANES = 16
SC_SCAN = 8192


def _cparams(n_axes):
    return pltpu.CompilerParams(dimension_semantics=("arbitrary",) * n_axes,
                                vmem_limit_bytes=VMEM_LIMIT)


def _split3(x):
    hi = x.astype(BF16)
    r1 = x - hi.astype(F32)
    mid = r1.astype(BF16)
    lo = (r1 - mid.astype(F32)).astype(BF16)
    return hi, mid, lo


def _pack_bf16_pairs(v):
    bits = pltpu.bitcast(v.astype(F32), jnp.uint32)
    half = v.shape[1] // 2
    return (bits[:, :half] >> 16) | bits[:, half:]


def _unpack_bf16_pairs(words):
    return jnp.concatenate([pltpu.bitcast(words << 16, F32),
                            pltpu.bitcast(words & jnp.uint32(0xFFFF0000), F32)], axis=1)


def _layer_norm(y, g, b):
    mu = jnp.mean(y, axis=-1, keepdims=True)
    yc = y - mu
    var = jnp.mean(yc * yc, axis=-1, keepdims=True)
    return yc * lax.rsqrt(var + LN_EPS) * g + b


def _rms_norm(y, g):
    return y * lax.rsqrt(jnp.mean(y * y, axis=-1, keepdims=True) + RMS_EPS) * g


def _store_transposed(dst_ref, val):
    for s in range(val.shape[1] // LANES):
        sl = slice(s * LANES, (s + 1) * LANES)
        dst_ref[0, sl, :] = val[:, sl].T.astype(dst_ref.dtype)


def _store_values_transposed(vt_ref, val):
    tm = val.shape[0]
    ones = jnp.ones((VT_ROWS - HEAD_DIM, tm), vt_ref.dtype)
    for s in range(val.shape[1] // LANES):
        pair_t = val[:, s * LANES:(s + 1) * LANES].T.astype(vt_ref.dtype)
        for h in range(2):
            r0 = (2 * s + h) * VT_ROWS
            vt_ref[0, r0:r0 + HEAD_DIM, :] = pair_t[h * HEAD_DIM:(h + 1) * HEAD_DIM]
            vt_ref[0, r0 + HEAD_DIM:r0 + VT_ROWS, :] = ones


def _fox_proj_kernel(x_ref, w_ref, bf_ref, seg_ref, qt_ref, k_ref, vt_ref, mqt_ref, cum_ref,
                     cumt_ref, cfirst_ref, clast_ref, knorm_ref, carry_ref, knmax_ref, *,
                     tiles_per_batch):
    t = pl.program_id(0)

    @pl.when(t % tiles_per_batch == 0)
    def _():
        carry_ref[...] = jnp.zeros_like(carry_ref)
        knmax_ref[...] = jnp.zeros_like(knmax_ref)

    tm = x_ref.shape[0]
    proj = jnp.dot(x_ref[...].astype(BF16), w_ref[...], preferred_element_type=F32)
    qscale = HEAD_DIM ** -0.5 * LOG2E
    _store_transposed(qt_ref, proj[:, :FOX_WIDTH] * qscale)
    kb = proj[:, FOX_WIDTH:2 * FOX_WIDTH].astype(BF16)
    k_ref[...] = kb
    _store_values_transposed(vt_ref, proj[:, 2 * FOX_WIDTH:3 * FOX_WIDTH])
    kf = kb.astype(F32)
    ksq = jnp.dot((kf * kf).astype(BF16), seg_ref[...], preferred_element_type=F32)
    n_sub = tm // FOX_TQ
    for sub in range(n_sub):
        tile_max = jnp.sqrt(jnp.max(ksq[sub * FOX_TQ:(sub + 1) * FOX_TQ], axis=0, keepdims=True))
        knmax_ref[...] = jnp.maximum(knmax_ref[...], tile_max)
        knorm_ref[sub] = knmax_ref[...]
    f = proj[:, 3 * FOX_WIDTH:3 * FOX_WIDTH + LANES] + bf_ref[...]
    _store_transposed(mqt_ref, proj[:, 3 * FOX_WIDTH + LANES:] * qscale)
    log_f = jnp.minimum(f, 0.0) - jnp.log1p(jnp.exp(-jnp.abs(f)))
    row = lax.broadcasted_iota(jnp.int32, (tm, tm), 0)
    col = lax.broadcasted_iota(jnp.int32, (tm, tm), 1)
    tri = jnp.where(row >= col, 1.0, 0.0).astype(BF16)
    hi, mid, lo = _split3(log_f)
    cum = (jnp.dot(tri, hi, preferred_element_type=F32)
           + jnp.dot(tri, mid, preferred_element_type=F32)
           + jnp.dot(tri, lo, preferred_element_type=F32)) + carry_ref[...]
    carry_ref[...] = cum[tm - 1:tm, :]
    cum2 = cum * LOG2E
    cum_ref[...] = cum2
    cumt_ref[0] = cum2.T[:16, :]
    for sub in range(n_sub):
        cfirst_ref[sub] = cum2[sub * FOX_TQ:sub * FOX_TQ + 1, :]
        clast_ref[sub] = cum2[(sub + 1) * FOX_TQ - 1:(sub + 1) * FOX_TQ, :]


def _fox_proj(x2d, w, bf, seq):
    t_tokens = x2d.shape[0]
    tm = PROJ_TM
    nb = t_tokens // seq
    n = w.shape[1]
    tiles_per_batch = seq // tm
    n_tiles = t_tokens // tm
    seg = (jnp.arange(FOX_WIDTH)[:, None] // HEAD_DIM == jnp.arange(LANES)[None, :]).astype(BF16)
    row_spec = lambda width: pl.BlockSpec((tm, width), lambda i: (i, 0))
    t_spec = lambda rows: pl.BlockSpec(
        (1, rows, tm), lambda i: (i // tiles_per_batch, 0, i % tiles_per_batch))
    n_sub = tm // FOX_TQ
    stat_spec = pl.BlockSpec((n_sub, 1, LANES), lambda i: (i, 0, 0))
    stat_shape = jax.ShapeDtypeStruct((n_tiles * n_sub, 1, LANES), F32)
    return pl.pallas_call(
        functools.partial(_fox_proj_kernel, tiles_per_batch=tiles_per_batch),
        grid=(n_tiles,),
        in_specs=[row_spec(D_MODEL),
                  pl.BlockSpec((D_MODEL, n), lambda i: (0, 0)),
                  pl.BlockSpec((1, LANES), lambda i: (0, 0)),
                  pl.BlockSpec((FOX_WIDTH, LANES), lambda i: (0, 0))],
        out_specs=[t_spec(FOX_WIDTH), row_spec(FOX_WIDTH), t_spec(FOX_HEADS * VT_ROWS),
                   t_spec(MEM_WIDTH), row_spec(LANES), t_spec(16),
                   stat_spec, stat_spec, stat_spec],
        out_shape=[jax.ShapeDtypeStruct((nb, FOX_WIDTH, seq), BF16),
                   jax.ShapeDtypeStruct((t_tokens, FOX_WIDTH), BF16),
                   jax.ShapeDtypeStruct((nb, FOX_HEADS * VT_ROWS, seq), BF16),
                   jax.ShapeDtypeStruct((nb, MEM_WIDTH, seq), BF16),
                   jax.ShapeDtypeStruct((t_tokens, LANES), F32),
                   jax.ShapeDtypeStruct((nb, 16, seq), F32),
                   stat_shape, stat_shape, stat_shape],
        scratch_shapes=[pltpu.VMEM((1, LANES), F32), pltpu.VMEM((1, LANES), F32)],
        compiler_params=_cparams(1),
        name="fox_proj",
    )(x2d, w, bf, seg)


def _rope_tables(pos_ref, invf_ref):
    ang = pos_ref[...].astype(F32) * invf_ref[...]
    cos = jnp.cos(ang)
    sin = jnp.sin(ang)
    lane = lax.broadcasted_iota(jnp.int32, ang.shape, 1)
    half = QK_ROPE // 2
    in_x1 = (lane >= QK_NOPE) & (lane < QK_NOPE + half)
    in_x2 = (lane >= QK_NOPE + half) & (lane < QK_NOPE + QK_ROPE)
    c_tab = jnp.where(lane < QK_NOPE, 1.0, jnp.where(in_x1 | in_x2, cos, 0.0))
    s_up = jnp.where(in_x2, sin, 0.0)
    s_dn = jnp.where(in_x1, -sin, 0.0)
    return c_tab, s_up, s_dn


def _rope_slab(slab, tables):
    c_tab, s_up, s_dn = tables
    half = QK_ROPE // 2
    up = pltpu.roll(slab, half, 1)
    dn = pltpu.roll(slab, LANES - half, 1)
    return slab * c_tab + up * s_up + dn * s_dn


def _mla_proj_kernel(x_ref, pos_ref, invf_ref, w_ref, gq_ref, wuq_ref, gkv_ref, wk_ref, wv_ref,
                     qt_ref, mqt_ref, k_ref, vt_ref):
    proj = jnp.dot(x_ref[...].astype(BF16), w_ref[...], preferred_element_type=F32)
    tables = _rope_tables(pos_ref, invf_ref)
    c_q = _rms_norm(proj[:, :Q_LORA], gq_ref[...])
    _store_transposed(mqt_ref, proj[:, Q_LORA:Q_LORA + MEM_WIDTH] * (HEAD_DIM ** -0.5 * LOG2E))
    kv_off = Q_LORA + MEM_WIDTH
    c_kv = _rms_norm(proj[:, kv_off:kv_off + KV_LORA], gkv_ref[...])
    kr = _rope_slab(proj[:, kv_off + KV_LORA:], tables)
    q = jnp.dot(c_q.astype(BF16), wuq_ref[...], preferred_element_type=F32)
    kn = jnp.dot(c_kv.astype(BF16), wk_ref[...], preferred_element_type=F32)
    qscale = (QK_NOPE + QK_ROPE) ** -0.5 * LOG2E
    for h in range(MLA_HEADS):
        sl = slice(h * MLA_SLAB, (h + 1) * MLA_SLAB)
        qt_ref[0, sl, :] = (_rope_slab(q[:, sl], tables) * qscale).T.astype(BF16)
        k_ref[:, sl] = (kn[:, sl] + kr).astype(BF16)
    _store_values_transposed(vt_ref, jnp.dot(c_kv.astype(BF16), wv_ref[...],
                                             preferred_element_type=F32))


def _mla_proj(x2d, pos2d, invf, w, gq, wuq, gkv, wk, wv, seq):
    t_tokens = x2d.shape[0]
    tm = PROJ_TM
    nb = t_tokens // seq
    tiles_per_batch = seq // tm
    row_spec = lambda width: pl.BlockSpec((tm, width), lambda i: (i, 0))
    t_spec = lambda rows: pl.BlockSpec(
        (1, rows, tm), lambda i: (i // tiles_per_batch, 0, i % tiles_per_batch))
    full = lambda a: pl.BlockSpec(a.shape, lambda i: (0, 0))
    slabs = MLA_HEADS * MLA_SLAB
    return pl.pallas_call(
        _mla_proj_kernel,
        grid=(t_tokens // tm,),
        in_specs=[row_spec(D_MODEL), row_spec(1), full(invf), full(w), full(gq), full(wuq),
                  full(gkv), full(wk), full(wv)],
        out_specs=[t_spec(slabs), t_spec(MEM_WIDTH), row_spec(slabs),
                   t_spec(MLA_HEADS * VT_ROWS)],
        out_shape=[jax.ShapeDtypeStruct((nb, slabs, seq), BF16),
                   jax.ShapeDtypeStruct((nb, MEM_WIDTH, seq), BF16),
                   jax.ShapeDtypeStruct((t_tokens, slabs), BF16),
                   jax.ShapeDtypeStruct((nb, MLA_HEADS * VT_ROWS, seq), BF16)],
        compiler_params=_cparams(1),
        name="mla_proj",
    )(x2d, pos2d, invf, w, gq, wuq, gkv, wk, wv)


SKIP_LOG2 = -160.0
NORM_SLACK = 1.02


def _causal_attn_kernel(*refs, tq, n_tiles, fox, slab, long_body):
    if fox:
        (cf_ref, cl_ref, kn_ref, qt_ref, k_ref, vt_ref, cq_ref, ck_ref,
         o_ref, m_ref, acc_ref, s_ref) = refs
    else:
        qt_ref, k_ref, vt_ref, o_ref, m_ref, acc_ref, s_ref = refs
    b = pl.program_id(0)
    pair = pl.program_id(1)
    i = pl.program_id(2)
    m_ref[...] = jnp.full(m_ref.shape, NEG_INF, F32)
    acc_ref[...] = jnp.zeros(acc_ref.shape, F32)
    qt = qt_ref[0]
    if slab:
        qth = [qt[:MLA_SLAB], qt[MLA_SLAB:]]
    else:
        rowi = lax.broadcasted_iota(jnp.int32, qt.shape, 0)
        zero = jnp.zeros_like(qt)
        qth = [jnp.where(rowi < HEAD_DIM, qt, zero), jnp.where(rowi >= HEAD_DIM, qt, zero)]
    if fox:
        lane = lax.broadcasted_iota(jnp.int32, (1, LANES), 1)
        cq = [cq_ref[0, pl.ds(2 * pair + h, 1), :] for h in range(2)]

    def logits_to(slot, j):
        off = pl.multiple_of(j * tq, tq)
        kc = k_ref[0, pl.ds(off, tq), :]
        for h in range(2):
            kh = kc[:, h * MLA_SLAB:(h + 1) * MLA_SLAB] if slab else kc
            s = jnp.dot(kh, qth[h], preferred_element_type=F32)
            if fox:
                ck_blk = ck_ref[0, pl.ds(off, tq), :]
                ck = jnp.sum(jnp.where(lane == 2 * pair + h, ck_blk, 0.0), axis=1, keepdims=True)
                s = s + cq[h] - ck
            s_ref[slot, h] = s

    def softmax_pv(slot, j, causal_mask):
        off = pl.multiple_of(j * tq, tq)
        for h in range(2):
            s = s_ref[slot, h]
            if causal_mask is not None:
                s = jnp.where(causal_mask, s, NEG_INF)
            m_prev = m_ref[h]
            m_new = jnp.maximum(m_prev, jnp.max(s, axis=0, keepdims=True))
            alpha = jnp.exp2(m_prev - m_new)
            p = jnp.exp2(s - m_new).astype(BF16)
            vth = vt_ref[0, h * VT_ROWS:(h + 1) * VT_ROWS, pl.ds(off, tq)]
            acc_ref[h] = acc_ref[h] * alpha + jnp.dot(vth, p, preferred_element_type=F32)
            m_ref[h] = m_new

    if fox:
        base = b * n_tiles
        qn = []
        for h in range(2):
            qf = qth[h].astype(F32)
            qn.append(jnp.sqrt(jnp.max(jnp.sum(qf * qf, axis=0, keepdims=True))) * NORM_SLACK)

        top = [2.0 * qn[h] * kn_ref[base + i, 2 * pair + h] + cf_ref[base + i, 2 * pair + h]
               for h in range(2)]

        def live(j):
            ub = [top[h] - cl_ref[base + jnp.maximum(j, 0), 2 * pair + h] for h in range(2)]
            return (j >= 0) & (jnp.maximum(ub[0], ub[1]) > SKIP_LOG2)

        j0 = lax.while_loop(live, lambda j: j - 1, i - 1) + 1
    else:
        j0 = 0
    n_before = i - j0
    odd = (n_before % 2) == 1

    @pl.when(odd)
    def _():
        logits_to(1, j0)
        logits_to(0, j0 + 1)
        softmax_pv(1, j0, None)

    @pl.when(jnp.logical_not(odd))
    def _():
        logits_to(0, j0)

    def two_steps(j):
        logits_to(1, j + 1)
        softmax_pv(0, j, None)
        logits_to(0, j + 2)
        softmax_pv(1, j + 1, None)

    j_even = j0 + (n_before % 2)
    n_two = n_before // 2
    if long_body:
        @pl.when((n_two % 2) == 1)
        def _():
            two_steps(j_even)

        j_quad = j_even + 2 * (n_two % 2)

        def body(t, carry):
            two_steps(j_quad + 4 * t)
            two_steps(j_quad + 4 * t + 2)
            return carry

        lax.fori_loop(0, n_two // 2, body, 0)
    else:
        def body(t, carry):
            two_steps(j_even + 2 * t)
            return carry

        lax.fori_loop(0, n_two, body, 0)
    key = lax.broadcasted_iota(jnp.int32, (tq, tq), 0)
    qry = lax.broadcasted_iota(jnp.int32, (tq, tq), 1)
    softmax_pv(0, i, key <= qry)
    out_t = jnp.concatenate(
        [acc_ref[h, :HEAD_DIM] * (1.0 / acc_ref[h, HEAD_DIM:HEAD_DIM + 1]) for h in range(2)],
        axis=0)
    o_ref[0] = out_t.T.astype(o_ref.dtype)


def _causal_attention(qt, k, vt, fox_args, *, slab):
    nb, seq, _ = k.shape
    fox = fox_args is not None
    tq = FOX_TQ if fox else ATT_TQ
    n_tiles = seq // tq
    n_pairs = FOX_HEADS // 2
    rows = 2 * MLA_SLAB if slab else LANES
    in_specs = [pl.BlockSpec((1, rows, tq), lambda b, p, i: (b, p, i)),
                pl.BlockSpec((1, seq, rows), lambda b, p, i: (b, 0, p)),
                pl.BlockSpec((1, 2 * VT_ROWS, seq), lambda b, p, i: (b, p, 0))]
    args = [qt, k, vt]
    if fox:
        cum, cumt, cfirst, clast, knorm = fox_args
        smem = pl.BlockSpec(memory_space=pltpu.SMEM)
        in_specs = [smem, smem, smem] + in_specs + [
            pl.BlockSpec((1, 16, tq), lambda b, p, i: (b, 0, i)),
            pl.BlockSpec((1, seq, LANES), lambda b, p, i: (b, 0, 0))]
        args = [cfirst, clast, knorm] + args + [cumt, cum]
    return pl.pallas_call(
        functools.partial(_causal_attn_kernel, tq=tq, n_tiles=n_tiles, fox=fox, slab=slab,
                          long_body=not fox),
        grid=(nb, n_pairs, n_tiles),
        in_specs=in_specs,
        out_specs=pl.BlockSpec((1, tq, LANES), lambda b, p, i: (b, i, p)),
        out_shape=jax.ShapeDtypeStruct((nb, seq, n_pairs * LANES), BF16),
        scratch_shapes=[pltpu.VMEM((2, 1, tq), F32), pltpu.VMEM((2, VT_ROWS, tq), F32),
                        pltpu.VMEM((2, 2, tq, tq), F32)],
        compiler_params=_cparams(3),
        name="fox_attention" if fox else "mla_attention",
    )(*args)


def _mem_proj_kernel(mem_ref, w_ref, mk_ref, mvt_ref):
    mkv = jnp.dot(mem_ref[...].astype(BF16), w_ref[...], preferred_element_type=F32)
    mk_ref[0] = mkv[:, :MEM_WIDTH].astype(BF16)
    _store_values_transposed(mvt_ref, mkv[:, MEM_WIDTH:])


def _mem_proj(mem2d, w, nb):
    return pl.pallas_call(
        _mem_proj_kernel,
        grid=(nb,),
        in_specs=[pl.BlockSpec((N_MEM, D_MODEL), lambda b: (b, 0)),
                  pl.BlockSpec(w.shape, lambda b: (0, 0))],
        out_specs=[pl.BlockSpec((1, N_MEM, MEM_WIDTH), lambda b: (b, 0, 0)),
                   pl.BlockSpec((1, MEM_HEADS * VT_ROWS, N_MEM), lambda b: (b, 0, 0))],
        out_shape=[jax.ShapeDtypeStruct((nb, N_MEM, MEM_WIDTH), BF16),
                   jax.ShapeDtypeStruct((nb, MEM_HEADS * VT_ROWS, N_MEM), BF16)],
        compiler_params=_cparams(1),
        name="mem_proj",
    )(mem2d, w)


def _mem_attn_kernel(qt_ref, k_ref, vt_ref, o_ref):
    qt = qt_ref[0]
    kc = k_ref[0]
    for pair in range(MEM_HEADS // 2):
        qt_p = qt[pair * LANES:(pair + 1) * LANES]
        k_p = kc[:, pair * LANES:(pair + 1) * LANES]
        rowi = lax.broadcasted_iota(jnp.int32, qt_p.shape, 0)
        zero = jnp.zeros_like(qt_p)
        outs = []
        for h in range(2):
            mine = (rowi < HEAD_DIM) if h == 0 else (rowi >= HEAD_DIM)
            s = jnp.dot(k_p, jnp.where(mine, qt_p, zero), preferred_element_type=F32)
            p = jnp.exp2(s - jnp.max(s, axis=0, keepdims=True)).astype(BF16)
            r0 = (2 * pair + h) * VT_ROWS
            acc = jnp.dot(vt_ref[0, r0:r0 + VT_ROWS, :], p, preferred_element_type=F32)
            outs.append(acc[:HEAD_DIM] * (1.0 / acc[HEAD_DIM:HEAD_DIM + 1]))
        o_ref[0, :, pair * LANES:(pair + 1) * LANES] = (
            jnp.concatenate(outs, axis=0).T.astype(o_ref.dtype))


def _memory_attention(mqt, mk, mvt):
    nb, _, seq = mqt.shape
    tq = ATT_TQ
    return pl.pallas_call(
        _mem_attn_kernel,
        grid=(nb, seq // tq),
        in_specs=[pl.BlockSpec((1, MEM_WIDTH, tq), lambda b, i: (b, 0, i)),
                  pl.BlockSpec((1, N_MEM, MEM_WIDTH), lambda b, i: (b, 0, 0)),
                  pl.BlockSpec((1, MEM_HEADS * VT_ROWS, N_MEM), lambda b, i: (b, 0, 0))],
        out_specs=pl.BlockSpec((1, tq, MEM_WIDTH), lambda b, i: (b, i, 0)),
        out_shape=jax.ShapeDtypeStruct((nb, seq, MEM_WIDTH), BF16),
        compiler_params=_cparams(2),
        name="memory_attention",
    )(mqt, mk, mvt)


def _post_attn_kernel(x_ref, att_ref, memo_ref, wa_ref, wm_ref, g_ref, b_ref, wr_ref, br_ref,
                      x1_ref, x1p_ref, idx_ref, rank_ref, gate_ref, cnt_ref, carry_ref):
    t = pl.program_id(0)

    @pl.when(t == 0)
    def _():
        carry_ref[...] = jnp.zeros_like(carry_ref)

    tm = x_ref.shape[0]
    mix = (jnp.dot(att_ref[...], wa_ref[...], preferred_element_type=F32)
           + jnp.dot(memo_ref[...], wm_ref[...], preferred_element_type=F32))
    x1 = _layer_norm(DEEPNORM_ALPHA * x_ref[...] + mix, g_ref[...], b_ref[...])
    x1_ref[...] = x1
    x1b = x1.astype(BF16)
    x1p_ref[...] = _pack_bf16_pairs(x1b)
    logits = jnp.dot(x1b, wr_ref[...], preferred_element_type=F32) + br_ref[...]
    lane = lax.broadcasted_iota(jnp.int32, (tm, LANES), 1)
    work = jnp.where(lane < N_EXPERTS, logits, -jnp.inf)
    idxs, vals = [], []
    onehot = jnp.zeros((tm, LANES), F32)
    for _ in range(TOP_K):
        best = jnp.max(work, axis=1, keepdims=True)
        where_best = jnp.argmax(work, axis=1, keepdims=True).astype(jnp.int32)
        hit = lane == where_best
        onehot = jnp.where(hit, 1.0, onehot)
        work = jnp.where(hit, -jnp.inf, work)
        idxs.append(where_best)
        vals.append(best)
    exps = [jnp.exp(v - vals[0]) for v in vals]
    denom = exps[0] + exps[1] + exps[2] + exps[3]
    row = lax.broadcasted_iota(jnp.int32, (tm, tm), 0)
    col = lax.broadcasted_iota(jnp.int32, (tm, tm), 1)
    strict = jnp.where(row > col, 1.0, 0.0).astype(BF16)
    before = jnp.dot(strict, onehot.astype(BF16), preferred_element_type=F32) + carry_ref[...]
    idx_out = jnp.zeros((tm, LANES), jnp.int32)
    rank_out = jnp.zeros((tm, LANES), jnp.int32)
    gate_out = jnp.zeros((tm, LANES), F32)
    for r in range(TOP_K):
        rank_r = jnp.sum(jnp.where(lane == idxs[r], before, 0.0), axis=1, keepdims=True)
        idx_out = jnp.where(lane == r, idxs[r], idx_out)
        rank_out = jnp.where(lane == r, rank_r.astype(jnp.int32), rank_out)
        gate_out = jnp.where(lane == r, exps[r] / denom, gate_out)
    idx_ref[...] = idx_out
    rank_ref[...] = rank_out
    gate_ref[...] = gate_out
    total = carry_ref[...] + jnp.sum(onehot, axis=0, keepdims=True)
    carry_ref[...] = total
    cnt_ref[...] = jnp.broadcast_to(total, cnt_ref.shape)


def _post_attn(x2d, att, memo, wa, wm, g, b, wr, br):
    t_tokens = x2d.shape[0]
    tm = PROJ_TM
    row_spec = lambda width: pl.BlockSpec((tm, width), lambda i: (i, 0))
    full = lambda a: pl.BlockSpec(a.shape, lambda i: (0, 0))
    return pl.pallas_call(
        _post_attn_kernel,
        grid=(t_tokens // tm,),
        in_specs=[row_spec(D_MODEL), row_spec(att.shape[1]), row_spec(MEM_WIDTH),
                  full(wa), full(wm), full(g), full(b), full(wr), full(br)],
        out_specs=[row_spec(D_MODEL), row_spec(D_MODEL // 2), row_spec(LANES), row_spec(LANES),
                   row_spec(LANES), pl.BlockSpec((8, LANES), lambda i: (0, 0))],
        out_shape=[jax.ShapeDtypeStruct((t_tokens, D_MODEL), F32),
                   jax.ShapeDtypeStruct((t_tokens, D_MODEL // 2), jnp.uint32),
                   jax.ShapeDtypeStruct((t_tokens, LANES), jnp.int32),
                   jax.ShapeDtypeStruct((t_tokens, LANES), jnp.int32),
                   jax.ShapeDtypeStruct((t_tokens, LANES), F32),
                   jax.ShapeDtypeStruct((8, LANES), F32)],
        scratch_shapes=[pltpu.VMEM((1, LANES), F32)],
        compiler_params=_cparams(1),
        name="outproj_ln_router",
    )(x2d, att, memo, wa, wm, g, b, wr, br)


def _expert_kernel(be_ref, nu_ref, br_ref, x_ref, wgu_hbm, bgu_ref, wdn_hbm, bdn_ref, *rest,
                   n_chunks, blk_lo, n_blocks, layer, has_prev):
    o_ref, wgu_st, wdn_st, wgu_bf, wdn_bf, h_ref, sem = rest[1:] if has_prev else rest
    step = pl.program_id(0)
    blk = step + blk_lo
    live_end = jnp.minimum(blk_lo + pl.num_programs(0), nu_ref[0])
    expert = be_ref[blk]

    def fetch(e):
        return (pltpu.make_async_copy(wgu_hbm.at[layer, e], wgu_st, sem.at[0]),
                pltpu.make_async_copy(wdn_hbm.at[layer, e], wdn_st, sem.at[1]))

    @pl.when(((step == 0) | (expert != be_ref[jnp.maximum(blk - 1, 0)])) & (blk < live_end))
    def _():
        @pl.when(step == 0)
        def _():
            for copy in fetch(expert):
                copy.start()

        for copy in fetch(expert):
            copy.wait()
        wgu_bf[...] = wgu_st[...].astype(BF16)
        wdn_bf[...] = wdn_st[...].astype(BF16)
        nxt = lax.while_loop(
            lambda j: (j < live_end) & (be_ref[jnp.minimum(j, n_blocks - 1)] == expert),
            lambda j: j + 1, blk + 1)

        @pl.when(nxt < live_end)
        def _():
            for copy in fetch(be_ref[jnp.minimum(nxt, n_blocks - 1)]):
                copy.start()

    def mlp(rows):
        x = _unpack_bf16_pairs(x_ref[:rows]).astype(BF16)
        cw = D_EXPERT // n_chunks
        for c in range(n_chunks):
            gs = slice(c * cw, (c + 1) * cw)
            us = slice(D_EXPERT + c * cw, D_EXPERT + (c + 1) * cw)
            g = jnp.dot(x, wgu_bf[:, gs], preferred_element_type=F32) + bgu_ref[0, 0, :, gs]
            u = jnp.dot(x, wgu_bf[:, us], preferred_element_type=F32) + bgu_ref[0, 0, :, us]
            g = jnp.minimum(g, SWIGLU_LIMIT)
            u = jnp.clip(u, -SWIGLU_LIMIT, SWIGLU_LIMIT)
            h_ref[:rows, gs] = ((u + 1.0) * (g * jax.nn.sigmoid(SWIGLU_ALPHA * g))).astype(BF16)
        y = jnp.dot(h_ref[:rows], wdn_bf[...], preferred_element_type=F32) + bdn_ref[0, 0]
        return _pack_bf16_pairs(y.astype(BF16))

    tm = o_ref.shape[0]
    live_rows = br_ref[blk]

    @pl.when(live_rows > tm // 2)
    def _():
        o_ref[...] = mlp(tm)

    @pl.when((live_rows > 0) & (live_rows <= tm // 2))
    def _():
        o_ref[:tm // 2] = mlp(tm // 2)
        o_ref[tm // 2:] = jnp.zeros((tm - tm // 2, o_ref.shape[1]), o_ref.dtype)

    @pl.when(live_rows == 0)
    def _():
        o_ref[...] = jnp.zeros_like(o_ref)


def _experts(block_expert, n_used, block_rows, xb, blk_lo, p_rows, y_prev, layer, wgu, bgu, wdn, bdn):
    tm = MOE_TM
    has_prev = y_prev is not None
    b_map = lambda i, be, nu, br: (layer, be[i + blk_lo], 0, 0)
    in_specs = [pl.BlockSpec((tm, D_MODEL // 2), lambda i, be, nu, br: (i, 0)),
                pl.BlockSpec(memory_space=pl.ANY),
                pl.BlockSpec((1, 1, 1, 2 * D_EXPERT), b_map),
                pl.BlockSpec(memory_space=pl.ANY),
                pl.BlockSpec((1, 1, 1, D_MODEL), b_map)]
    args = [block_expert, n_used, block_rows, xb, wgu, bgu, wdn, bdn]
    if has_prev:
        in_specs.append(pl.BlockSpec(memory_space=pl.ANY))
        args.append(y_prev)
    grid_spec = pltpu.PrefetchScalarGridSpec(
        num_scalar_prefetch=3,
        grid=(xb.shape[0] // tm,),
        in_specs=in_specs,
        out_specs=pl.BlockSpec((tm, D_MODEL // 2), lambda i, be, nu, br: (i + blk_lo, 0)),
        scratch_shapes=[pltpu.VMEM((D_MODEL, 2 * D_EXPERT), F32),
                        pltpu.VMEM((D_EXPERT, D_MODEL), F32),
                        pltpu.VMEM((D_MODEL, 2 * D_EXPERT), BF16),
                        pltpu.VMEM((D_EXPERT, D_MODEL), BF16),
                        pltpu.VMEM((tm, D_EXPERT), BF16),
                        pltpu.SemaphoreType.DMA((2,))],
    )
    return pl.pallas_call(
        functools.partial(_expert_kernel, n_chunks=4, blk_lo=blk_lo, n_blocks=p_rows // tm,
                          layer=layer, has_prev=has_prev),
        grid_spec=grid_spec,
        out_shape=jax.ShapeDtypeStruct((p_rows, D_MODEL // 2), jnp.uint32),
        input_output_aliases={len(args) - 1: 0} if has_prev else {},
        compiler_params=pltpu.CompilerParams(dimension_semantics=("arbitrary",),
                                             vmem_limit_bytes=EXPERT_VMEM_LIMIT),
        name="experts",
    )(*args)


def _sc_move_rows(table_hbm, idx_v, out_hbm, base, n_chunks, rows_v, gsem, wsem):
    def gather(c, slot):
        off = pl.multiple_of(c * SC_CHUNK, SC_CHUNK)
        return pltpu.make_async_copy(table_hbm.at[idx_v.at[pl.ds(off, SC_CHUNK)]],
                                     rows_v.at[slot], gsem.at[slot])

    def put(c, slot):
        off = pl.multiple_of(c * SC_CHUNK, SC_CHUNK)
        return pltpu.make_async_copy(rows_v.at[slot], out_hbm.at[pl.ds(base + off, SC_CHUNK)],
                                     wsem.at[slot])

    gather(0, 0).start()
    gather(1, 1).start()

    @pl.loop(0, n_chunks, step=2)
    def _(c):
        for slot in range(2):
            gather(c + slot, slot).wait()
            put(c + slot, slot).start()
        for slot in range(2):
            put(c + slot, slot).wait()

            @pl.when(c + 2 + slot < n_chunks)
            def _():
                gather(c + 2 + slot, slot).start()


def _sc_gather_rows(table, idx):
    n_idx = idx.shape[0]
    width = table.shape[1]
    per_worker = n_idx // SC_WORKERS
    n_chunks = per_worker // SC_CHUNK
    mesh = plsc.VectorSubcoreMesh(core_axis_name="c", subcore_axis_name="s",
                                  num_cores=SC_CORES, num_subcores=SC_SUBCORES)

    def body(table_hbm, idx_hbm, out_hbm, idx_v, rows_v, gsem, wsem):
        wid = lax.axis_index("s") * SC_CORES + lax.axis_index("c")
        base = wid * per_worker
        pltpu.sync_copy(idx_hbm.at[pl.ds(base, per_worker)], idx_v)
        _sc_move_rows(table_hbm, idx_v, out_hbm, base, n_chunks, rows_v, gsem, wsem)

    return pl.kernel(
        body,
        out_type=jax.ShapeDtypeStruct((n_idx, width), table.dtype),
        mesh=mesh,
        scratch_types=[pltpu.VMEM((per_worker,), jnp.int32),
                       pltpu.VMEM((2, SC_CHUNK, width), table.dtype),
                       pltpu.SemaphoreType.DMA((2,)),
                       pltpu.SemaphoreType.DMA((2,))],
        name="sc_gather_rows",
    )(table, idx)


def _sc_dispatch_rows(table, dest, slot_lo, n_slots):
    n_tok, width = table.shape
    n_pairs = dest.shape[0]
    per_worker = n_slots // SC_WORKERS
    n_chunks = per_worker // SC_CHUNK
    n_scan = n_pairs // SC_SCAN
    mesh = plsc.VectorSubcoreMesh(core_axis_name="c", subcore_axis_name="s",
                                  num_cores=SC_CORES, num_subcores=SC_SUBCORES)

    def body(table_hbm, dest_hbm, out_hbm, tok_v, dest_v, rows_v, gsem, wsem):
        wid = lax.axis_index("s") * SC_CORES + lax.axis_index("c")
        base = wid * per_worker
        first = slot_lo + base
        lane = lax.iota(jnp.int32, SC_LANES)

        @pl.loop(0, per_worker // SC_LANES)
        def _(i):
            off = pl.multiple_of(i * SC_LANES, SC_LANES)
            tok_v[pl.ds(off, SC_LANES)] = lax.rem(first + off + lane, n_tok)

        @pl.loop(0, n_scan)
        def _(g):
            goff = pl.multiple_of(g * SC_SCAN, SC_SCAN)
            pltpu.sync_copy(dest_hbm.at[pl.ds(goff, SC_SCAN)], dest_v)

            @pl.loop(0, SC_SCAN // SC_LANES)
            def _(i):
                off = pl.multiple_of(i * SC_LANES, SC_LANES)
                local = dest_v[pl.ds(off, SC_LANES)] - first
                mine = (local >= 0) & (local < per_worker)
                pair = goff + off + lane
                plsc.store_scatter(tok_v, [jnp.where(mine, local, 0)], pair // TOP_K, mask=mine)

        _sc_move_rows(table_hbm, tok_v, out_hbm, base, n_chunks, rows_v, gsem, wsem)

    return pl.kernel(
        body,
        out_type=jax.ShapeDtypeStruct((n_slots, width), table.dtype),
        mesh=mesh,
        scratch_types=[pltpu.VMEM((per_worker,), jnp.int32),
                       pltpu.VMEM((SC_SCAN,), jnp.int32),
                       pltpu.VMEM((2, SC_CHUNK, width), table.dtype),
                       pltpu.SemaphoreType.DMA((2,)),
                       pltpu.SemaphoreType.DMA((2,))],
        compiler_params=pltpu.CompilerParams(needs_layout_passes=False),
        name="sc_dispatch_rows",
    )(table, dest)


def _combine_kernel(x1_ref, yg_ref, gate_ref, g_ref, b_ref, *rest):
    o_ref = rest[-1]
    gates = gate_ref[...]
    ffn = _unpack_bf16_pairs(yg_ref[0]) * gates[:, 0:1]
    for r in range(1, TOP_K):
        ffn = ffn + _unpack_bf16_pairs(yg_ref[r]) * gates[:, r:r + 1]
    o_ref[...] = _layer_norm(DEEPNORM_ALPHA * x1_ref[...] + ffn, g_ref[...], b_ref[...])


def _combine(x1, yg, gates, g, b, tok_lo, out_prev):
    t_tokens = x1.shape[0]
    tm = COMBINE_TM
    tile_lo = tok_lo // tm
    in_specs = [pl.BlockSpec((tm, D_MODEL), lambda i: (i + tile_lo, 0)),
                pl.BlockSpec((TOP_K, tm, D_MODEL // 2), lambda i: (0, i, 0)),
                pl.BlockSpec((tm, LANES), lambda i: (i + tile_lo, 0)),
                pl.BlockSpec((1, D_MODEL), lambda i: (0, 0)),
                pl.BlockSpec((1, D_MODEL), lambda i: (0, 0))]
    args = [x1, yg, gates, g, b]
    if out_prev is not None:
        in_specs.append(pl.BlockSpec(memory_space=pl.ANY))
        args.append(out_prev)
    return pl.pallas_call(
        _combine_kernel,
        grid=(yg.shape[1] // tm,),
        in_specs=in_specs,
        out_specs=pl.BlockSpec((tm, D_MODEL), lambda i: (i + tile_lo, 0)),
        out_shape=jax.ShapeDtypeStruct((t_tokens, D_MODEL), F32),
        input_output_aliases={len(args) - 1: 0} if out_prev is not None else {},
        compiler_params=_cparams(1),
        name="combine_ln",
    )(*args)


def _moe(x1, x1p, idx, rank, gates, counts, layer, wgu, bgu, wdn, bdn, g, b):
    t_tokens = x1.shape[0]
    tm = MOE_TM
    counts = counts[0, :N_EXPERTS].astype(jnp.int32)
    padded = ((counts + tm - 1) // tm) * tm
    pend = jnp.cumsum(padded)
    pstart = pend - padded
    idx4 = idx[:, :TOP_K]
    dest = (pstart[idx4] + rank[:, :TOP_K]).reshape(-1)
    n_blocks = (t_tokens * TOP_K) // tm + N_EXPERTS
    p_rows = n_blocks * tm
    block_start = jnp.arange(n_blocks, dtype=jnp.int32) * tm
    block_expert = jnp.minimum(
        jnp.sum((pend[None, :] <= block_start[:, None]).astype(jnp.int32), axis=1), N_EXPERTS - 1)
    n_used = (pend[-1:] // tm).astype(jnp.int32)
    block_rows = jnp.clip(counts[block_expert] - (block_start - pstart[block_expert]), 0, tm)
    y = None
    for lo, hi in ((0, n_blocks // MOE_SPLIT), (n_blocks // MOE_SPLIT, n_blocks)):
        xb = _sc_dispatch_rows(x1p, dest, lo * tm, (hi - lo) * tm)
        y = _experts(block_expert, n_used, block_rows, xb, lo, p_rows, y, layer, wgu, bgu, wdn, bdn)
    dest2 = dest.reshape(t_tokens, TOP_K)
    out = None
    for lo, hi in ((0, t_tokens // MOE_SPLIT), (t_tokens // MOE_SPLIT, t_tokens)):
        by_choice = dest2[lo:hi].T.reshape(-1)
        yg = _sc_gather_rows(y, by_choice).reshape(TOP_K, hi - lo, D_MODEL // 2)
        out = _combine(x1, yg, gates, g, b, lo, out)
    return out


def _row(v, width=None):
    v = v.astype(F32).reshape(1, -1)
    if width is not None and v.shape[1] < width:
        v = jnp.pad(v, ((0, 0), (0, width - v.shape[1])))
    return v


def _pad_cols(w, width):
    return jnp.pad(w, ((0, 0), (0, width - w.shape[1])))


def kernel(x, mem, positions, a_w_in, a_b_f, a_w_out, b_w_in, b_g_q, b_w_uq, b_w_out,
           kv_w_dkv, kv_g, kv_w_ukv, mem_w_kv, ln_g, ln_b,
           moe_w_r, moe_b_r, moe_w_gu, moe_b_gu, moe_w_dn, moe_b_dn):
    nb, seq, d = x.shape
    t_tokens = nb * seq
    n_a = a_w_in.shape[0]
    x2d = x.reshape(t_tokens, d)
    mem2d = mem.reshape(nb * N_MEM, d)
    pos2d = positions.reshape(t_tokens, 1)
    half = QK_ROPE // 2
    inv_freq = ROPE_THETA ** (-jnp.arange(half, dtype=F32) * 2.0 / QK_ROPE)
    invf = jnp.zeros((1, LANES), F32)
    invf = invf.at[0, QK_NOPE:QK_NOPE + half].set(inv_freq)
    invf = invf.at[0, QK_NOPE + half:QK_NOPE + QK_ROPE].set(inv_freq)

    shared_kv = None
    for l in range(DEPTH):
        mk, mvt = _mem_proj(mem2d, mem_w_kv[l].astype(BF16), nb)
        if l < n_a:
            w_in = a_w_in[l]
            w = jnp.concatenate([w_in[:, :3 * FOX_WIDTH],
                                 _pad_cols(w_in[:, 3 * FOX_WIDTH:3 * FOX_WIDTH + FOX_HEADS], LANES),
                                 w_in[:, 3 * FOX_WIDTH + FOX_HEADS:]], axis=1).astype(BF16)
            qt, k, vt, mqt, cum, cumt, cfirst, clast, knorm = _fox_proj(
                x2d, w, _row(a_b_f[l], LANES), seq)
            stats = [s.reshape(-1, LANES) for s in (cfirst, clast, knorm)]
            att = _causal_attention(qt, k.reshape(nb, seq, -1), vt,
                                    [cum.reshape(nb, seq, LANES), cumt] + stats, slab=False)
            w_out = a_w_out[l]
        else:
            bl = l - n_a
            w = jnp.concatenate([b_w_in[bl], kv_w_dkv[:, :KV_LORA],
                                 jnp.zeros((d, QK_NOPE), F32), kv_w_dkv[:, KV_LORA:],
                                 jnp.zeros((d, LANES - QK_NOPE - QK_ROPE), F32)], axis=1).astype(BF16)
            wuq = b_w_uq[bl].reshape(Q_LORA, MLA_HEADS, QK_NOPE + QK_ROPE)
            wuq = jnp.pad(wuq, ((0, 0), (0, 0), (0, MLA_SLAB - QK_NOPE - QK_ROPE)))
            wuq = wuq.reshape(Q_LORA, MLA_HEADS * MLA_SLAB).astype(BF16)
            wukv = kv_w_ukv.reshape(KV_LORA, MLA_HEADS, QK_NOPE + V_DIM)
            wk = jnp.pad(wukv[:, :, :QK_NOPE], ((0, 0), (0, 0), (0, MLA_SLAB - QK_NOPE)))
            wk = wk.reshape(KV_LORA, MLA_HEADS * MLA_SLAB).astype(BF16)
            wv = wukv[:, :, QK_NOPE:].reshape(KV_LORA, MLA_V_WIDTH).astype(BF16)
            qt, mqt, k_new, vt_new = _mla_proj(x2d, pos2d, invf, w, _row(b_g_q[bl]), wuq,
                                              _row(kv_g), wk, wv, seq)
            if shared_kv is None:
                shared_kv = (k_new.reshape(nb, seq, -1), vt_new)
            att = _causal_attention(qt, shared_kv[0], shared_kv[1], None, slab=True)
            w_out = b_w_out[bl]
        memo = _memory_attention(mqt, mk, mvt)
        n_att = w_out.shape[0] - MEM_WIDTH
        x1, x1p, idx, rank, gates, counts = _post_attn(
            x2d, att.reshape(t_tokens, -1), memo.reshape(t_tokens, MEM_WIDTH),
            w_out[:n_att].astype(BF16), w_out[n_att:].astype(BF16),
            _row(ln_g[l, 0]), _row(ln_b[l, 0]),
            _pad_cols(moe_w_r[l], LANES).astype(BF16), _row(moe_b_r[l], LANES))
        x2d = _moe(x1, x1p, idx, rank, gates, counts, l,
                   moe_w_gu, moe_b_gu.reshape(DEPTH, N_EXPERTS, 1, -1),
                   moe_w_dn, moe_b_dn.reshape(DEPTH, N_EXPERTS, 1, -1),
                   _row(ln_g[l, 1]), _row(ln_b[l, 1]))
    return x2d.reshape(nb, seq, d)
```

```python
import functools
import math

import jax
import jax.numpy as jnp
from jax import lax
from jax.experimental import pallas as pl
from jax.experimental.pallas import tpu as pltpu
from jax.experimental.pallas import tpu_sc as plsc

F32 = jnp.float32
BF16 = jnp.bfloat16

D_MODEL = 1024
DEPTH = 2
N_MEM = 256
HEAD_DIM = 64
FOX_HEADS = 12
MEM_HEADS = 4
MLA_HEADS = 12
Q_LORA = 384
KV_LORA = 256
QK_NOPE = 64
QK_ROPE = 32
V_DIM = 64
ROPE_THETA = 10000.0
N_EXPERTS = 32
TOP_K = 4
D_EXPERT = D_MODEL
SWIGLU_LIMIT = 7.0
SWIGLU_ALPHA = 1.702
LN_EPS = 1e-5
RMS_EPS = 1e-6
NEG_INF = -1e30
DEEPNORM_ALPHA = (2 * DEPTH) ** 0.25
FOX_WIDTH = FOX_HEADS * HEAD_DIM
MEM_WIDTH = MEM_HEADS * HEAD_DIM
MLA_V_WIDTH = MLA_HEADS * V_DIM

LANES = 128
LOG2E = math.log2(math.e)
VMEM_LIMIT = 48 * 1024 * 1024
EXPERT_VMEM_LIMIT = 48 * 1024 * 1024

PROJ_TM = 512
ATT_TQ = 512
FOX_TQ = 512
MOE_TM = 512
COMBINE_TM = 256
MOE_SPLIT = 4
MLA_SLAB = LANES
VT_ROWS = HEAD_DIM + 16
SC_CORES = 2
SC_SUBCORES = 16
SC_WORKERS = SC_CORES * SC_SUBCORES
SC_CHUNK = 32
SC_LANES = 16
SC_SCAN = 8192


def _cparams(n_axes):
    return pltpu.CompilerParams(dimension_semantics=("arbitrary",) * n_axes,
                                vmem_limit_bytes=VMEM_LIMIT)


def _split3(x):
    hi = x.astype(BF16)
    r1 = x - hi.astype(F32)
    mid = r1.astype(BF16)
    lo = (r1 - mid.astype(F32)).astype(BF16)
    return hi, mid, lo


def _pack_bf16_pairs(v):
    bits = pltpu.bitcast(v.astype(F32), jnp.uint32)
    half = v.shape[1] // 2
    return (bits[:, :half] >> 16) | bits[:, half:]


def _unpack_bf16_pairs(words):
    return jnp.concatenate([pltpu.bitcast(words << 16, F32),
                            pltpu.bitcast(words & jnp.uint32(0xFFFF0000), F32)], axis=1)


def _layer_norm(y, g, b):
    mu = jnp.mean(y, axis=-1, keepdims=True)
    yc = y - mu
    var = jnp.mean(yc * yc, axis=-1, keepdims=True)
    return yc * lax.rsqrt(var + LN_EPS) * g + b


def _rms_norm(y, g):
    return y * lax.rsqrt(jnp.mean(y * y, axis=-1, keepdims=True) + RMS_EPS) * g


def _store_transposed(dst_ref, val):
    for s in range(val.shape[1] // LANES):
        sl = slice(s * LANES, (s + 1) * LANES)
        dst_ref[0, sl, :] = val[:, sl].T.astype(dst_ref.dtype)


def _store_values_transposed(vt_ref, val):
    tm = val.shape[0]
    ones = jnp.ones((VT_ROWS - HEAD_DIM, tm), vt_ref.dtype)
    for s in range(val.shape[1] // LANES):
        pair_t = val[:, s * LANES:(s + 1) * LANES].T.astype(vt_ref.dtype)
        for h in range(2):
            r0 = (2 * s + h) * VT_ROWS
            vt_ref[0, r0:r0 + HEAD_DIM, :] = pair_t[h * HEAD_DIM:(h + 1) * HEAD_DIM]
            vt_ref[0, r0 + HEAD_DIM:r0 + VT_ROWS, :] = ones


def _fox_proj_kernel(x_ref, w_ref, bf_ref, seg_ref, qt_ref, k_ref, vt_ref, mqt_ref, cum_ref,
                     cumt_ref, cfirst_ref, clast_ref, knorm_ref, carry_ref, knmax_ref, *,
                     tiles_per_batch):
    t = pl.program_id(0)

    @pl.when(t % tiles_per_batch == 0)
    def _():
        carry_ref[...] = jnp.zeros_like(carry_ref)
        knmax_ref[...] = jnp.zeros_like(knmax_ref)

    tm = x_ref.shape[0]
    proj = jnp.dot(x_ref[...].astype(BF16), w_ref[...], preferred_element_type=F32)
    qscale = HEAD_DIM ** -0.5 * LOG2E
    _store_transposed(qt_ref, proj[:, :FOX_WIDTH] * qscale)
    kb = proj[:, FOX_WIDTH:2 * FOX_WIDTH].astype(BF16)
    k_ref[...] = kb
    _store_values_transposed(vt_ref, proj[:, 2 * FOX_WIDTH:3 * FOX_WIDTH])
    kf = kb.astype(F32)
    ksq = jnp.dot((kf * kf).astype(BF16), seg_ref[...], preferred_element_type=F32)
    n_sub = tm // FOX_TQ
    for sub in range(n_sub):
        tile_max = jnp.sqrt(jnp.max(ksq[sub * FOX_TQ:(sub + 1) * FOX_TQ], axis=0, keepdims=True))
        knmax_ref[...] = jnp.maximum(knmax_ref[...], tile_max)
        knorm_ref[sub] = knmax_ref[...]
    f = proj[:, 3 * FOX_WIDTH:3 * FOX_WIDTH + LANES] + bf_ref[...]
    _store_transposed(mqt_ref, proj[:, 3 * FOX_WIDTH + LANES:] * qscale)
    log_f = jnp.minimum(f, 0.0) - jnp.log1p(jnp.exp(-jnp.abs(f)))
    row = lax.broadcasted_iota(jnp.int32, (tm, tm), 0)
    col = lax.broadcasted_iota(jnp.int32, (tm, tm), 1)
    tri = jnp.where(row >= col, 1.0, 0.0).astype(BF16)
    hi, mid, lo = _split3(log_f)
    cum = (jnp.dot(tri, hi, preferred_element_type=F32)
           + jnp.dot(tri, mid, preferred_element_type=F32)
           + jnp.dot(tri, lo, preferred_element_type=F32)) + carry_ref[...]
    carry_ref[...] = cum[tm - 1:tm, :]
    cum2 = cum * LOG2E
    cum_ref[...] = cum2
    cumt_ref[0] = cum2.T[:16, :]
    for sub in range(n_sub):
        cfirst_ref[sub] = cum2[sub * FOX_TQ:sub * FOX_TQ + 1, :]
        clast_ref[sub] = cum2[(sub + 1) * FOX_TQ - 1:(sub + 1) * FOX_TQ, :]


def _fox_proj(x2d, w, bf, seq):
    t_tokens = x2d.shape[0]
    tm = PROJ_TM
    nb = t_tokens // seq
    n = w.shape[1]
    tiles_per_batch = seq // tm
    n_tiles = t_tokens // tm
    seg = (jnp.arange(FOX_WIDTH)[:, None] // HEAD_DIM == jnp.arange(LANES)[None, :]).astype(BF16)
    row_spec = lambda width: pl.BlockSpec((tm, width), lambda i: (i, 0))
    t_spec = lambda rows: pl.BlockSpec(
        (1, rows, tm), lambda i: (i // tiles_per_batch, 0, i % tiles_per_batch))
    n_sub = tm // FOX_TQ
    stat_spec = pl.BlockSpec((n_sub, 1, LANES), lambda i: (i, 0, 0))
    stat_shape = jax.ShapeDtypeStruct((n_tiles * n_sub, 1, LANES), F32)
    return pl.pallas_call(
        functools.partial(_fox_proj_kernel, tiles_per_batch=tiles_per_batch),
        grid=(n_tiles,),
        in_specs=[row_spec(D_MODEL),
                  pl.BlockSpec((D_MODEL, n), lambda i: (0, 0)),
                  pl.BlockSpec((1, LANES), lambda i: (0, 0)),
                  pl.BlockSpec((FOX_WIDTH, LANES), lambda i: (0, 0))],
        out_specs=[t_spec(FOX_WIDTH), row_spec(FOX_WIDTH), t_spec(FOX_HEADS * VT_ROWS),
                   t_spec(MEM_WIDTH), row_spec(LANES), t_spec(16),
                   stat_spec, stat_spec, stat_spec],
        out_shape=[jax.ShapeDtypeStruct((nb, FOX_WIDTH, seq), BF16),
                   jax.ShapeDtypeStruct((t_tokens, FOX_WIDTH), BF16),
                   jax.ShapeDtypeStruct((nb, FOX_HEADS * VT_ROWS, seq), BF16),
                   jax.ShapeDtypeStruct((nb, MEM_WIDTH, seq), BF16),
                   jax.ShapeDtypeStruct((t_tokens, LANES), F32),
                   jax.ShapeDtypeStruct((nb, 16, seq), F32),
                   stat_shape, stat_shape, stat_shape],
        scratch_shapes=[pltpu.VMEM((1, LANES), F32), pltpu.VMEM((1, LANES), F32)],
        compiler_params=_cparams(1),
        name="fox_proj",
    )(x2d, w, bf, seg)


def _rope_tables(pos_ref, invf_ref):
    ang = pos_ref[...].astype(F32) * invf_ref[...]
    cos = jnp.cos(ang)
    sin = jnp.sin(ang)
    lane = lax.broadcasted_iota(jnp.int32, ang.shape, 1)
    half = QK_ROPE // 2
    in_x1 = (lane >= QK_NOPE) & (lane < QK_NOPE + half)
    in_x2 = (lane >= QK_NOPE + half) & (lane < QK_NOPE + QK_ROPE)
    c_tab = jnp.where(lane < QK_NOPE, 1.0, jnp.where(in_x1 | in_x2, cos, 0.0))
    s_up = jnp.where(in_x2, sin, 0.0)
    s_dn = jnp.where(in_x1, -sin, 0.0)
    return c_tab, s_up, s_dn


def _rope_slab(slab, tables):
    c_tab, s_up, s_dn = tables
    half = QK_ROPE // 2
    up = pltpu.roll(slab, half, 1)
    dn = pltpu.roll(slab, LANES - half, 1)
    return slab * c_tab + up * s_up + dn * s_dn


def _mla_proj_kernel(x_ref, pos_ref, invf_ref, w_ref, gq_ref, wuq_ref, gkv_ref, wk_ref, wv_ref,
                     qt_ref, mqt_ref, k_ref, vt_ref):
    proj = jnp.dot(x_ref[...].astype(BF16), w_ref[...], preferred_element_type=F32)
    tables = _rope_tables(pos_ref, invf_ref)
    c_q = _rms_norm(proj[:, :Q_LORA], gq_ref[...])
    _store_transposed(mqt_ref, proj[:, Q_LORA:Q_LORA + MEM_WIDTH] * (HEAD_DIM ** -0.5 * LOG2E))
    kv_off = Q_LORA + MEM_WIDTH
    c_kv = _rms_norm(proj[:, kv_off:kv_off + KV_LORA], gkv_ref[...])
    kr = _rope_slab(proj[:, kv_off + KV_LORA:], tables)
    q = jnp.dot(c_q.astype(BF16), wuq_ref[...], preferred_element_type=F32)
    kn = jnp.dot(c_kv.astype(BF16), wk_ref[...], preferred_element_type=F32)
    qscale = (QK_NOPE + QK_ROPE) ** -0.5 * LOG2E
    for h in range(MLA_HEADS):
        sl = slice(h * MLA_SLAB, (h + 1) * MLA_SLAB)
        qt_ref[0, sl, :] = (_rope_slab(q[:, sl], tables) * qscale).T.astype(BF16)
        k_ref[:, sl] = (kn[:, sl] + kr).astype(BF16)
    _store_values_transposed(vt_ref, jnp.dot(c_kv.astype(BF16), wv_ref[...],
                                             preferred_element_type=F32))


def _mla_proj(x2d, pos2d, invf, w, gq, wuq, gkv, wk, wv, seq):
    t_tokens = x2d.shape[0]
    tm = PROJ_TM
    nb = t_tokens // seq
    tiles_per_batch = seq // tm
    row_spec = lambda width: pl.BlockSpec((tm, width), lambda i: (i, 0))
    t_spec = lambda rows: pl.BlockSpec(
        (1, rows, tm), lambda i: (i // tiles_per_batch, 0, i % tiles_per_batch))
    full = lambda a: pl.BlockSpec(a.shape, lambda i: (0, 0))
    slabs = MLA_HEADS * MLA_SLAB
    return pl.pallas_call(
        _mla_proj_kernel,
        grid=(t_tokens // tm,),
        in_specs=[row_spec(D_MODEL), row_spec(1), full(invf), full(w), full(gq), full(wuq),
                  full(gkv), full(wk), full(wv)],
        out_specs=[t_spec(slabs), t_spec(MEM_WIDTH), row_spec(slabs),
                   t_spec(MLA_HEADS * VT_ROWS)],
        out_shape=[jax.ShapeDtypeStruct((nb, slabs, seq), BF16),
                   jax.ShapeDtypeStruct((nb, MEM_WIDTH, seq), BF16),
                   jax.ShapeDtypeStruct((t_tokens, slabs), BF16),
                   jax.ShapeDtypeStruct((nb, MLA_HEADS * VT_ROWS, seq), BF16)],
        compiler_params=_cparams(1),
        name="mla_proj",
    )(x2d, pos2d, invf, w, gq, wuq, gkv, wk, wv)


SKIP_LOG2 = -160.0
NORM_SLACK = 1.02


def _causal_attn_kernel(*refs, tq, n_tiles, fox, slab, long_body):
    if fox:
        (cf_ref, cl_ref, kn_ref, qt_ref, k_ref, vt_ref, cq_ref, ck_ref,
         o_ref, m_ref, acc_ref, s_ref, cmax_ref) = refs
    else:
        qt_ref, k_ref, vt_ref, o_ref, m_ref, acc_ref, s_ref, cmax_ref = refs
    b = pl.program_id(0)
    pair = pl.program_id(1)
    i = pl.program_id(2)
    m_ref[...] = jnp.full(m_ref.shape, NEG_INF, F32)
    acc_ref[...] = jnp.zeros(acc_ref.shape, F32)
    qt = qt_ref[0]
    if slab:
        qth = [qt[:MLA_SLAB], qt[MLA_SLAB:]]
    else:
        rowi = lax.broadcasted_iota(jnp.int32, qt.shape, 0)
        zero = jnp.zeros_like(qt)
        qth = [jnp.where(rowi < HEAD_DIM, qt, zero), jnp.where(rowi >= HEAD_DIM, qt, zero)]
    if fox:
        lane = lax.broadcasted_iota(jnp.int32, (1, LANES), 1)
        cq = [cq_ref[0, pl.ds(2 * pair + h, 1), :] for h in range(2)]

    def logits_to(slot, j):
        off = pl.multiple_of(j * tq, tq)
        kc = k_ref[0, pl.ds(off, tq), :]
        for h in range(2):
            kh = kc[:, h * MLA_SLAB:(h + 1) * MLA_SLAB] if slab else kc
            s = jnp.dot(kh, qth[h], preferred_element_type=F32)
            if fox:
                ck_blk = ck_ref[0, pl.ds(off, tq), :]
                ck = jnp.sum(jnp.where(lane == 2 * pair + h, ck_blk, 0.0), axis=1, keepdims=True)
                s = s + cq[h] - ck
            s_ref[slot, h] = s
            cmax_ref[slot, h] = jnp.max(s, axis=0, keepdims=True)

    def softmax_pv(slot, j, causal_mask):
        off = pl.multiple_of(j * tq, tq)
        for h in range(2):
            s = s_ref[slot, h]
            if causal_mask is not None:
                s = jnp.where(causal_mask, s, NEG_INF)
                chunk_max = jnp.max(s, axis=0, keepdims=True)
            else:
                chunk_max = cmax_ref[slot, h]
            m_prev = m_ref[h]
            m_new = jnp.maximum(m_prev, chunk_max)
            alpha = jnp.exp2(m_prev - m_new)
            p = jnp.exp2(s - m_new).astype(BF16)
            vth = vt_ref[0, h * VT_ROWS:(h + 1) * VT_ROWS, pl.ds(off, tq)]
            acc_ref[h] = acc_ref[h] * alpha + jnp.dot(vth, p, preferred_element_type=F32)
            m_ref[h] = m_new

    if fox:
        base = b * n_tiles
        qn = []
        for h in range(2):
            qf = qth[h].astype(F32)
            qn.append(jnp.sqrt(jnp.max(jnp.sum(qf * qf, axis=0, keepdims=True))) * NORM_SLACK)

        top = [2.0 * qn[h] * kn_ref[base + i, 2 * pair + h] + cf_ref[base + i, 2 * pair + h]
               for h in range(2)]

        def live(j):
            ub = [top[h] - cl_ref[base + jnp.maximum(j, 0), 2 * pair + h] for h in range(2)]
            return (j >= 0) & (jnp.maximum(ub[0], ub[1]) > SKIP_LOG2)

        j0 = lax.while_loop(live, lambda j: j - 1, i - 1) + 1
    else:
        j0 = 0
    n_before = i - j0
    odd = (n_before % 2) == 1

    @pl.when(odd)
    def _():
        logits_to(1, j0)
        logits_to(0, j0 + 1)
        softmax_pv(1, j0, None)

    @pl.when(jnp.logical_not(odd))
    def _():
        logits_to(0, j0)

    def two_steps(j):
        logits_to(1, j + 1)
        softmax_pv(0, j, None)
        logits_to(0, j + 2)
        softmax_pv(1, j + 1, None)

    j_even = j0 + (n_before % 2)
    n_two = n_before // 2
    if long_body:
        @pl.when((n_two % 2) == 1)
        def _():
            two_steps(j_even)

        j_quad = j_even + 2 * (n_two % 2)

        def body(t, carry):
            two_steps(j_quad + 4 * t)
            two_steps(j_quad + 4 * t + 2)
            return carry

        lax.fori_loop(0, n_two // 2, body, 0)
    else:
        def body(t, carry):
            two_steps(j_even + 2 * t)
            return carry

        lax.fori_loop(0, n_two, body, 0)
    key = lax.broadcasted_iota(jnp.int32, (tq, tq), 0)
    qry = lax.broadcasted_iota(jnp.int32, (tq, tq), 1)
    softmax_pv(0, i, key <= qry)
    out_t = jnp.concatenate(
        [acc_ref[h, :HEAD_DIM] * (1.0 / acc_ref[h, HEAD_DIM:HEAD_DIM + 1]) for h in range(2)],
        axis=0)
    o_ref[0] = out_t.T.astype(o_ref.dtype)


def _causal_attention(qt, k, vt, fox_args, *, slab):
    nb, seq, _ = k.shape
    fox = fox_args is not None
    tq = FOX_TQ if fox else ATT_TQ
    n_tiles = seq // tq
    n_pairs = FOX_HEADS // 2
    rows = 2 * MLA_SLAB if slab else LANES
    in_specs = [pl.BlockSpec((1, rows, tq), lambda b, p, i: (b, p, i)),
                pl.BlockSpec((1, seq, rows), lambda b, p, i: (b, 0, p)),
                pl.BlockSpec((1, 2 * VT_ROWS, seq), lambda b, p, i: (b, p, 0))]
    args = [qt, k, vt]
    if fox:
        cum, cumt, cfirst, clast, knorm = fox_args
        smem = pl.BlockSpec(memory_space=pltpu.SMEM)
        in_specs = [smem, smem, smem] + in_specs + [
            pl.BlockSpec((1, 16, tq), lambda b, p, i: (b, 0, i)),
            pl.BlockSpec((1, seq, LANES), lambda b, p, i: (b, 0, 0))]
        args = [cfirst, clast, knorm] + args + [cumt, cum]
    return pl.pallas_call(
        functools.partial(_causal_attn_kernel, tq=tq, n_tiles=n_tiles, fox=fox, slab=slab,
                          long_body=not fox),
        grid=(nb, n_pairs, n_tiles),
        in_specs=in_specs,
        out_specs=pl.BlockSpec((1, tq, LANES), lambda b, p, i: (b, i, p)),
        out_shape=jax.ShapeDtypeStruct((nb, seq, n_pairs * LANES), BF16),
        scratch_shapes=[pltpu.VMEM((2, 1, tq), F32), pltpu.VMEM((2, VT_ROWS, tq), F32),
                        pltpu.VMEM((2, 2, tq, tq), F32), pltpu.VMEM((2, 2, 1, tq), F32)],
        compiler_params=_cparams(3),
        name="fox_attention" if fox else "mla_attention",
    )(*args)


def _mem_proj_kernel(mem_ref, w_ref, mk_ref, mvt_ref):
    mkv = jnp.dot(mem_ref[...].astype(BF16), w_ref[...], preferred_element_type=F32)
    mk_ref[0] = mkv[:, :MEM_WIDTH].astype(BF16)
    _store_values_transposed(mvt_ref, mkv[:, MEM_WIDTH:])


def _mem_proj(mem2d, w, nb):
    return pl.pallas_call(
        _mem_proj_kernel,
        grid=(nb,),
        in_specs=[pl.BlockSpec((N_MEM, D_MODEL), lambda b: (b, 0)),
                  pl.BlockSpec(w.shape, lambda b: (0, 0))],
        out_specs=[pl.BlockSpec((1, N_MEM, MEM_WIDTH), lambda b: (b, 0, 0)),
                   pl.BlockSpec((1, MEM_HEADS * VT_ROWS, N_MEM), lambda b: (b, 0, 0))],
        out_shape=[jax.ShapeDtypeStruct((nb, N_MEM, MEM_WIDTH), BF16),
                   jax.ShapeDtypeStruct((nb, MEM_HEADS * VT_ROWS, N_MEM), BF16)],
        compiler_params=_cparams(1),
        name="mem_proj",
    )(mem2d, w)


def _mem_attn_kernel(qt_ref, k_ref, vt_ref, o_ref):
    qt = qt_ref[0]
    kc = k_ref[0]
    for pair in range(MEM_HEADS // 2):
        qt_p = qt[pair * LANES:(pair + 1) * LANES]
        k_p = kc[:, pair * LANES:(pair + 1) * LANES]
        rowi = lax.broadcasted_iota(jnp.int32, qt_p.shape, 0)
        zero = jnp.zeros_like(qt_p)
        outs = []
        for h in range(2):
            mine = (rowi < HEAD_DIM) if h == 0 else (rowi >= HEAD_DIM)
            s = jnp.dot(k_p, jnp.where(mine, qt_p, zero), preferred_element_type=F32)
            p = jnp.exp2(s - jnp.max(s, axis=0, keepdims=True)).astype(BF16)
            r0 = (2 * pair + h) * VT_ROWS
            acc = jnp.dot(vt_ref[0, r0:r0 + VT_ROWS, :], p, preferred_element_type=F32)
            outs.append(acc[:HEAD_DIM] * (1.0 / acc[HEAD_DIM:HEAD_DIM + 1]))
        o_ref[0, :, pair * LANES:(pair + 1) * LANES] = (
            jnp.concatenate(outs, axis=0).T.astype(o_ref.dtype))


def _memory_attention(mqt, mk, mvt):
    nb, _, seq = mqt.shape
    tq = ATT_TQ
    return pl.pallas_call(
        _mem_attn_kernel,
        grid=(nb, seq // tq),
        in_specs=[pl.BlockSpec((1, MEM_WIDTH, tq), lambda b, i: (b, 0, i)),
                  pl.BlockSpec((1, N_MEM, MEM_WIDTH), lambda b, i: (b, 0, 0)),
                  pl.BlockSpec((1, MEM_HEADS * VT_ROWS, N_MEM), lambda b, i: (b, 0, 0))],
        out_specs=pl.BlockSpec((1, tq, MEM_WIDTH), lambda b, i: (b, i, 0)),
        out_shape=jax.ShapeDtypeStruct((nb, seq, MEM_WIDTH), BF16),
        compiler_params=_cparams(2),
        name="memory_attention",
    )(mqt, mk, mvt)


def _post_attn_kernel(x_ref, att_ref, memo_ref, wa_ref, wm_ref, g_ref, b_ref, wr_ref, br_ref,
                      x1_ref, x1p_ref, idx_ref, rank_ref, gate_ref, cnt_ref, carry_ref):
    t = pl.program_id(0)

    @pl.when(t == 0)
    def _():
        carry_ref[...] = jnp.zeros_like(carry_ref)

    tm = x_ref.shape[0]
    mix = (jnp.dot(att_ref[...], wa_ref[...], preferred_element_type=F32)
           + jnp.dot(memo_ref[...], wm_ref[...], preferred_element_type=F32))
    x1 = _layer_norm(DEEPNORM_ALPHA * x_ref[...] + mix, g_ref[...], b_ref[...])
    x1_ref[...] = x1
    x1b = x1.astype(BF16)
    x1p_ref[...] = _pack_bf16_pairs(x1b)
    logits = jnp.dot(x1b, wr_ref[...], preferred_element_type=F32) + br_ref[...]
    lane = lax.broadcasted_iota(jnp.int32, (tm, LANES), 1)
    work = jnp.where(lane < N_EXPERTS, logits, -jnp.inf)
    idxs, vals = [], []
    onehot = jnp.zeros((tm, LANES), F32)
    for _ in range(TOP_K):
        best = jnp.max(work, axis=1, keepdims=True)
        where_best = jnp.argmax(work, axis=1, keepdims=True).astype(jnp.int32)
        hit = lane == where_best
        onehot = jnp.where(hit, 1.0, onehot)
        work = jnp.where(hit, -jnp.inf, work)
        idxs.append(where_best)
        vals.append(best)
    exps = [jnp.exp(v - vals[0]) for v in vals]
    denom = exps[0] + exps[1] + exps[2] + exps[3]
    row = lax.broadcasted_iota(jnp.int32, (tm, tm), 0)
    col = lax.broadcasted_iota(jnp.int32, (tm, tm), 1)
    strict = jnp.where(row > col, 1.0, 0.0).astype(BF16)
    before = jnp.dot(strict, onehot.astype(BF16), preferred_element_type=F32) + carry_ref[...]
    idx_out = jnp.zeros((tm, LANES), F32)
    rank_out = jnp.zeros((tm, LANES), F32)
    gate_out = jnp.zeros((tm, LANES), F32)
    for r in range(TOP_K):
        rank_r = jnp.sum(jnp.where(lane == idxs[r], before, 0.0), axis=1, keepdims=True)
        idx_out = jnp.where(lane == r, idxs[r].astype(F32), idx_out)
        rank_out = jnp.where(lane == r, rank_r, rank_out)
        gate_out = jnp.where(lane == r, exps[r] / denom, gate_out)
    idx_ref[...] = idx_out.T[:8].astype(jnp.int32)
    rank_ref[...] = rank_out.T[:8].astype(jnp.int32)
    gate_ref[...] = gate_out
    total = carry_ref[...] + jnp.sum(onehot, axis=0, keepdims=True)
    carry_ref[...] = total
    cnt_ref[...] = jnp.broadcast_to(total, cnt_ref.shape)


def _post_attn(x2d, att, memo, wa, wm, g, b, wr, br):
    t_tokens = x2d.shape[0]
    tm = PROJ_TM
    row_spec = lambda width: pl.BlockSpec((tm, width), lambda i: (i, 0))
    full = lambda a: pl.BlockSpec(a.shape, lambda i: (0, 0))
    return pl.pallas_call(
        _post_attn_kernel,
        grid=(t_tokens // tm,),
        in_specs=[row_spec(D_MODEL), row_spec(att.shape[1]), row_spec(MEM_WIDTH),
                  full(wa), full(wm), full(g), full(b), full(wr), full(br)],
        out_specs=[row_spec(D_MODEL), row_spec(D_MODEL // 2),
                   pl.BlockSpec((8, tm), lambda i: (0, i)), pl.BlockSpec((8, tm), lambda i: (0, i)),
                   row_spec(LANES), pl.BlockSpec((8, LANES), lambda i: (0, 0))],
        out_shape=[jax.ShapeDtypeStruct((t_tokens, D_MODEL), F32),
                   jax.ShapeDtypeStruct((t_tokens, D_MODEL // 2), jnp.uint32),
                   jax.ShapeDtypeStruct((8, t_tokens), jnp.int32),
                   jax.ShapeDtypeStruct((8, t_tokens), jnp.int32),
                   jax.ShapeDtypeStruct((t_tokens, LANES), F32),
                   jax.ShapeDtypeStruct((8, LANES), F32)],
        scratch_shapes=[pltpu.VMEM((1, LANES), F32)],
        compiler_params=_cparams(1),
        name="outproj_ln_router",
    )(x2d, att, memo, wa, wm, g, b, wr, br)


def _expert_kernel(be_ref, nu_ref, br_ref, x_ref, wgu_hbm, bgu_ref, wdn_hbm, bdn_ref, *rest,
                   n_chunks, blk_lo, n_blocks, layer, has_prev):
    o_ref, wgu_st, wdn_st, wgu_bf, wdn_bf, h_ref, sem = rest[1:] if has_prev else rest
    step = pl.program_id(0)
    blk = step + blk_lo
    live_end = jnp.minimum(blk_lo + pl.num_programs(0), nu_ref[0])
    expert = be_ref[blk]

    def fetch(e):
        return (pltpu.make_async_copy(wgu_hbm.at[layer, e], wgu_st, sem.at[0]),
                pltpu.make_async_copy(wdn_hbm.at[layer, e], wdn_st, sem.at[1]))

    @pl.when(((step == 0) | (expert != be_ref[jnp.maximum(blk - 1, 0)])) & (blk < live_end))
    def _():
        @pl.when(step == 0)
        def _():
            for copy in fetch(expert):
                copy.start()

        for copy in fetch(expert):
            copy.wait()
        wgu_bf[...] = wgu_st[...].astype(BF16)
        wdn_bf[...] = wdn_st[...].astype(BF16)
        nxt = lax.while_loop(
            lambda j: (j < live_end) & (be_ref[jnp.minimum(j, n_blocks - 1)] == expert),
            lambda j: j + 1, blk + 1)

        @pl.when(nxt < live_end)
        def _():
            for copy in fetch(be_ref[jnp.minimum(nxt, n_blocks - 1)]):
                copy.start()

    def mlp(rows):
        x = _unpack_bf16_pairs(x_ref[:rows]).astype(BF16)
        cw = D_EXPERT // n_chunks
        for c in range(n_chunks):
            gs = slice(c * cw, (c + 1) * cw)
            us = slice(D_EXPERT + c * cw, D_EXPERT + (c + 1) * cw)
            g = jnp.dot(x, wgu_bf[:, gs], preferred_element_type=F32) + bgu_ref[0, 0, :, gs]
            u = jnp.dot(x, wgu_bf[:, us], preferred_element_type=F32) + bgu_ref[0, 0, :, us]
            g = jnp.minimum(g, SWIGLU_LIMIT)
            u = jnp.clip(u, -SWIGLU_LIMIT, SWIGLU_LIMIT)
            h_ref[:rows, gs] = ((u + 1.0) * (g * jax.nn.sigmoid(SWIGLU_ALPHA * g))).astype(BF16)
        y = jnp.dot(h_ref[:rows], wdn_bf[...], preferred_element_type=F32) + bdn_ref[0, 0]
        return _pack_bf16_pairs(y.astype(BF16))

    tm = o_ref.shape[0]
    live_rows = br_ref[blk]

    @pl.when(live_rows > tm // 2)
    def _():
        o_ref[...] = mlp(tm)

    @pl.when((live_rows > 0) & (live_rows <= tm // 2))
    def _():
        o_ref[:tm // 2] = mlp(tm // 2)
        o_ref[tm // 2:] = jnp.zeros((tm - tm // 2, o_ref.shape[1]), o_ref.dtype)

    @pl.when(live_rows == 0)
    def _():
        o_ref[...] = jnp.zeros_like(o_ref)


def _experts(block_expert, n_used, block_rows, xb, blk_lo, p_rows, y_prev, layer, wgu, bgu, wdn, bdn):
    tm = MOE_TM
    has_prev = y_prev is not None
    b_map = lambda i, be, nu, br: (layer, be[i + blk_lo], 0, 0)
    in_specs = [pl.BlockSpec((tm, D_MODEL // 2), lambda i, be, nu, br: (i, 0)),
                pl.BlockSpec(memory_space=pl.ANY),
                pl.BlockSpec((1, 1, 1, 2 * D_EXPERT), b_map),
                pl.BlockSpec(memory_space=pl.ANY),
                pl.BlockSpec((1, 1, 1, D_MODEL), b_map)]
    args = [block_expert, n_used, block_rows, xb, wgu, bgu, wdn, bdn]
    if has_prev:
        in_specs.append(pl.BlockSpec(memory_space=pl.ANY))
        args.append(y_prev)
    grid_spec = pltpu.PrefetchScalarGridSpec(
        num_scalar_prefetch=3,
        grid=(xb.shape[0] // tm,),
        in_specs=in_specs,
        out_specs=pl.BlockSpec((tm, D_MODEL // 2), lambda i, be, nu, br: (i + blk_lo, 0)),
        scratch_shapes=[pltpu.VMEM((D_MODEL, 2 * D_EXPERT), F32),
                        pltpu.VMEM((D_EXPERT, D_MODEL), F32),
                        pltpu.VMEM((D_MODEL, 2 * D_EXPERT), BF16),
                        pltpu.VMEM((D_EXPERT, D_MODEL), BF16),
                        pltpu.VMEM((tm, D_EXPERT), BF16),
                        pltpu.SemaphoreType.DMA((2,))],
    )
    return pl.pallas_call(
        functools.partial(_expert_kernel, n_chunks=4, blk_lo=blk_lo, n_blocks=p_rows // tm,
                          layer=layer, has_prev=has_prev),
        grid_spec=grid_spec,
        out_shape=jax.ShapeDtypeStruct((p_rows, D_MODEL // 2), jnp.uint32),
        input_output_aliases={len(args) - 1: 0} if has_prev else {},
        compiler_params=pltpu.CompilerParams(dimension_semantics=("arbitrary",),
                                             vmem_limit_bytes=EXPERT_VMEM_LIMIT),
        name="experts",
    )(*args)


def _sc_move_rows(table_hbm, idx_v, out_hbm, base, n_chunks, rows_v, gsem, wsem):
    def gather(c, slot):
        off = pl.multiple_of(c * SC_CHUNK, SC_CHUNK)
        return pltpu.make_async_copy(table_hbm.at[idx_v.at[pl.ds(off, SC_CHUNK)]],
                                     rows_v.at[slot], gsem.at[slot])

    def put(c, slot):
        off = pl.multiple_of(c * SC_CHUNK, SC_CHUNK)
        return pltpu.make_async_copy(rows_v.at[slot], out_hbm.at[pl.ds(base + off, SC_CHUNK)],
                                     wsem.at[slot])

    gather(0, 0).start()
    gather(1, 1).start()

    @pl.loop(0, n_chunks, step=2)
    def _(c):
        for slot in range(2):
            gather(c + slot, slot).wait()
            put(c + slot, slot).start()
        for slot in range(2):
            put(c + slot, slot).wait()

            @pl.when(c + 2 + slot < n_chunks)
            def _():
                gather(c + 2 + slot, slot).start()


def _sc_gather_rows(table, idx):
    n_idx = idx.shape[0]
    width = table.shape[1]
    per_worker = n_idx // SC_WORKERS
    n_chunks = per_worker // SC_CHUNK
    mesh = plsc.VectorSubcoreMesh(core_axis_name="c", subcore_axis_name="s",
                                  num_cores=SC_CORES, num_subcores=SC_SUBCORES)

    def body(table_hbm, idx_hbm, out_hbm, idx_v, rows_v, gsem, wsem):
        wid = lax.axis_index("s") * SC_CORES + lax.axis_index("c")
        base = wid * per_worker
        pltpu.sync_copy(idx_hbm.at[pl.ds(base, per_worker)], idx_v)
        _sc_move_rows(table_hbm, idx_v, out_hbm, base, n_chunks, rows_v, gsem, wsem)

    return pl.kernel(
        body,
        out_type=jax.ShapeDtypeStruct((n_idx, width), table.dtype),
        mesh=mesh,
        scratch_types=[pltpu.VMEM((per_worker,), jnp.int32),
                       pltpu.VMEM((2, SC_CHUNK, width), table.dtype),
                       pltpu.SemaphoreType.DMA((2,)),
                       pltpu.SemaphoreType.DMA((2,))],
        name="sc_gather_rows",
    )(table, idx)


def _sc_dispatch_rows(table, dest, slot_lo, n_slots):
    n_tok, width = table.shape
    n_pairs = dest.shape[0]
    per_worker = n_slots // SC_WORKERS
    n_chunks = per_worker // SC_CHUNK
    n_scan = n_pairs // SC_SCAN
    mesh = plsc.VectorSubcoreMesh(core_axis_name="c", subcore_axis_name="s",
                                  num_cores=SC_CORES, num_subcores=SC_SUBCORES)

    def body(table_hbm, dest_hbm, out_hbm, tok_v, dest_v, rows_v, gsem, wsem):
        wid = lax.axis_index("s") * SC_CORES + lax.axis_index("c")
        base = wid * per_worker
        first = slot_lo + base
        lane = lax.iota(jnp.int32, SC_LANES)

        @pl.loop(0, per_worker // SC_LANES)
        def _(i):
            off = pl.multiple_of(i * SC_LANES, SC_LANES)
            tok_v[pl.ds(off, SC_LANES)] = lax.rem(first + off + lane, n_tok)

        @pl.loop(0, n_scan)
        def _(g):
            goff = pl.multiple_of(g * SC_SCAN, SC_SCAN)
            pltpu.sync_copy(dest_hbm.at[pl.ds(goff, SC_SCAN)], dest_v)

            @pl.loop(0, SC_SCAN // SC_LANES)
            def _(i):
                off = pl.multiple_of(i * SC_LANES, SC_LANES)
                local = dest_v[pl.ds(off, SC_LANES)] - first
                mine = (local >= 0) & (local < per_worker)
                pair = goff + off + lane
                plsc.store_scatter(tok_v, [jnp.where(mine, local, 0)], lax.rem(pair, n_tok), mask=mine)

        _sc_move_rows(table_hbm, tok_v, out_hbm, base, n_chunks, rows_v, gsem, wsem)

    return pl.kernel(
        body,
        out_type=jax.ShapeDtypeStruct((n_slots, width), table.dtype),
        mesh=mesh,
        scratch_types=[pltpu.VMEM((per_worker,), jnp.int32),
                       pltpu.VMEM((SC_SCAN,), jnp.int32),
                       pltpu.VMEM((2, SC_CHUNK, width), table.dtype),
                       pltpu.SemaphoreType.DMA((2,)),
                       pltpu.SemaphoreType.DMA((2,))],
        compiler_params=pltpu.CompilerParams(needs_layout_passes=False),
        name="sc_dispatch_rows",
    )(table, dest)


def _combine_kernel(x1_ref, yg_ref, gate_ref, g_ref, b_ref, *rest):
    o_ref = rest[-1]
    gates = gate_ref[...]
    ffn = _unpack_bf16_pairs(yg_ref[0]) * gates[:, 0:1]
    for r in range(1, TOP_K):
        ffn = ffn + _unpack_bf16_pairs(yg_ref[r]) * gates[:, r:r + 1]
    o_ref[...] = _layer_norm(DEEPNORM_ALPHA * x1_ref[...] + ffn, g_ref[...], b_ref[...])


def _combine(x1, yg, gates, g, b, tok_lo, out_prev):
    t_tokens = x1.shape[0]
    tm = COMBINE_TM
    tile_lo = tok_lo // tm
    in_specs = [pl.BlockSpec((tm, D_MODEL), lambda i: (i + tile_lo, 0)),
                pl.BlockSpec((TOP_K, tm, D_MODEL // 2), lambda i: (0, i, 0)),
                pl.BlockSpec((tm, LANES), lambda i: (i + tile_lo, 0)),
                pl.BlockSpec((1, D_MODEL), lambda i: (0, 0)),
                pl.BlockSpec((1, D_MODEL), lambda i: (0, 0))]
    args = [x1, yg, gates, g, b]
    if out_prev is not None:
        in_specs.append(pl.BlockSpec(memory_space=pl.ANY))
        args.append(out_prev)
    return pl.pallas_call(
        _combine_kernel,
        grid=(yg.shape[1] // tm,),
        in_specs=in_specs,
        out_specs=pl.BlockSpec((tm, D_MODEL), lambda i: (i + tile_lo, 0)),
        out_shape=jax.ShapeDtypeStruct((t_tokens, D_MODEL), F32),
        input_output_aliases={len(args) - 1: 0} if out_prev is not None else {},
        compiler_params=_cparams(1),
        name="combine_ln",
    )(*args)


def _moe(x1, x1p, idx, rank, gates, counts, layer, wgu, bgu, wdn, bdn, g, b):
    t_tokens = x1.shape[0]
    tm = MOE_TM
    counts = counts[0, :N_EXPERTS].astype(jnp.int32)
    padded = ((counts + tm - 1) // tm) * tm
    pend = jnp.cumsum(padded)
    pstart = pend - padded
    dest = pstart[idx[:TOP_K]] + rank[:TOP_K]
    n_blocks = (t_tokens * TOP_K) // tm + N_EXPERTS
    p_rows = n_blocks * tm
    block_start = jnp.arange(n_blocks, dtype=jnp.int32) * tm
    block_expert = jnp.minimum(
        jnp.sum((pend[None, :] <= block_start[:, None]).astype(jnp.int32), axis=1), N_EXPERTS - 1)
    n_used = (pend[-1:] // tm).astype(jnp.int32)
    block_rows = jnp.clip(counts[block_expert] - (block_start - pstart[block_expert]), 0, tm)
    y = None
    for lo, hi in ((0, n_blocks // MOE_SPLIT), (n_blocks // MOE_SPLIT, n_blocks)):
        xb = _sc_dispatch_rows(x1p, dest.reshape(-1), lo * tm, (hi - lo) * tm)
        y = _experts(block_expert, n_used, block_rows, xb, lo, p_rows, y, layer, wgu, bgu, wdn, bdn)
    out = None
    for lo, hi in ((0, t_tokens // MOE_SPLIT), (t_tokens // MOE_SPLIT, t_tokens)):
        yg = _sc_gather_rows(y, dest[:, lo:hi].reshape(-1)).reshape(TOP_K, hi - lo, D_MODEL // 2)
        out = _combine(x1, yg, gates, g, b, lo, out)
    return out


def _row(v, width=None):
    v = v.astype(F32).reshape(1, -1)
    if width is not None and v.shape[1] < width:
        v = jnp.pad(v, ((0, 0), (0, width - v.shape[1])))
    return v


def _pad_cols(w, width):
    return jnp.pad(w, ((0, 0), (0, width - w.shape[1])))


def kernel(x, mem, positions, a_w_in, a_b_f, a_w_out, b_w_in, b_g_q, b_w_uq, b_w_out,
           kv_w_dkv, kv_g, kv_w_ukv, mem_w_kv, ln_g, ln_b,
           moe_w_r, moe_b_r, moe_w_gu, moe_b_gu, moe_w_dn, moe_b_dn):
    nb, seq, d = x.shape
    t_tokens = nb * seq
    n_a = a_w_in.shape[0]
    x2d = x.reshape(t_tokens, d)
    mem2d = mem.reshape(nb * N_MEM, d)
    pos2d = positions.reshape(t_tokens, 1)
    half = QK_ROPE // 2
    inv_freq = ROPE_THETA ** (-jnp.arange(half, dtype=F32) * 2.0 / QK_ROPE)
    invf = jnp.zeros((1, LANES), F32)
    invf = invf.at[0, QK_NOPE:QK_NOPE + half].set(inv_freq)
    invf = invf.at[0, QK_NOPE + half:QK_NOPE + QK_ROPE].set(inv_freq)

    shared_kv = None
    for l in range(DEPTH):
        mk, mvt = _mem_proj(mem2d, mem_w_kv[l].astype(BF16), nb)
        if l < n_a:
            w_in = a_w_in[l]
            w = jnp.concatenate([w_in[:, :3 * FOX_WIDTH],
                                 _pad_cols(w_in[:, 3 * FOX_WIDTH:3 * FOX_WIDTH + FOX_HEADS], LANES),
                                 w_in[:, 3 * FOX_WIDTH + FOX_HEADS:]], axis=1).astype(BF16)
            qt, k, vt, mqt, cum, cumt, cfirst, clast, knorm = _fox_proj(
                x2d, w, _row(a_b_f[l], LANES), seq)
            stats = [s.reshape(-1, LANES) for s in (cfirst, clast, knorm)]
            att = _causal_attention(qt, k.reshape(nb, seq, -1), vt,
                                    [cum.reshape(nb, seq, LANES), cumt] + stats, slab=False)
            w_out = a_w_out[l]
        else:
            bl = l - n_a
            w = jnp.concatenate([b_w_in[bl], kv_w_dkv[:, :KV_LORA],
                                 jnp.zeros((d, QK_NOPE), F32), kv_w_dkv[:, KV_LORA:],
                                 jnp.zeros((d, LANES - QK_NOPE - QK_ROPE), F32)], axis=1).astype(BF16)
            wuq = b_w_uq[bl].reshape(Q_LORA, MLA_HEADS, QK_NOPE + QK_ROPE)
            wuq = jnp.pad(wuq, ((0, 0), (0, 0), (0, MLA_SLAB - QK_NOPE - QK_ROPE)))
            wuq = wuq.reshape(Q_LORA, MLA_HEADS * MLA_SLAB).astype(BF16)
            wukv = kv_w_ukv.reshape(KV_LORA, MLA_HEADS, QK_NOPE + V_DIM)
            wk = jnp.pad(wukv[:, :, :QK_NOPE], ((0, 0), (0, 0), (0, MLA_SLAB - QK_NOPE)))
            wk = wk.reshape(KV_LORA, MLA_HEADS * MLA_SLAB).astype(BF16)
            wv = wukv[:, :, QK_NOPE:].reshape(KV_LORA, MLA_V_WIDTH).astype(BF16)
            qt, mqt, k_new, vt_new = _mla_proj(x2d, pos2d, invf, w, _row(b_g_q[bl]), wuq,
                                              _row(kv_g), wk, wv, seq)
            if shared_kv is None:
                shared_kv = (k_new.reshape(nb, seq, -1), vt_new)
            att = _causal_attention(qt, shared_kv[0], shared_kv[1], None, slab=True)
            w_out = b_w_out[bl]
        memo = _memory_attention(mqt, mk, mvt)
        n_att = w_out.shape[0] - MEM_WIDTH
        x1, x1p, idx, rank, gates, counts = _post_attn(
            x2d, att.reshape(t_tokens, -1), memo.reshape(t_tokens, MEM_WIDTH),
            w_out[:n_att].astype(BF16), w_out[n_att:].astype(BF16),
            _row(ln_g[l, 0]), _row(ln_b[l, 0]),
            _pad_cols(moe_w_r[l], LANES).astype(BF16), _row(moe_b_r[l], LANES))
        x2d = _moe(x1, x1p, idx, rank, gates, counts, l,
                   moe_w_gu, moe_b_gu.reshape(DEPTH, N_EXPERTS, 1, -1),
                   moe_w_dn, moe_b_dn.reshape(DEPTH, N_EXPERTS, 1, -1),
                   _row(ln_g[l, 1]), _row(ln_b[l, 1]))
    return x2d.reshape(nb, seq, d)
```

```python
import functools
import math

import jax
import jax.numpy as jnp
from jax import lax
from jax.experimental import pallas as pl
from jax.experimental.pallas import tpu as pltpu
from jax.experimental.pallas import tpu_sc as plsc

F32 = jnp.float32
BF16 = jnp.bfloat16

D_MODEL = 1024
DEPTH = 2
N_MEM = 256
HEAD_DIM = 64
FOX_HEADS = 12
MEM_HEADS = 4
MLA_HEADS = 12
Q_LORA = 384
KV_LORA = 256
QK_NOPE = 64
QK_ROPE = 32
V_DIM = 64
ROPE_THETA = 10000.0
N_EXPERTS = 32
TOP_K = 4
D_EXPERT = D_MODEL
SWIGLU_LIMIT = 7.0
SWIGLU_ALPHA = 1.702
LN_EPS = 1e-5
RMS_EPS = 1e-6
NEG_INF = -1e30
DEEPNORM_ALPHA = (2 * DEPTH) ** 0.25
FOX_WIDTH = FOX_HEADS * HEAD_DIM
MEM_WIDTH = MEM_HEADS * HEAD_DIM
MLA_V_WIDTH = MLA_HEADS * V_DIM

LANES = 128
LOG2E = math.log2(math.e)
VMEM_LIMIT = 48 * 1024 * 1024
EXPERT_VMEM_LIMIT = 48 * 1024 * 1024

PROJ_TM = 512
ATT_TQ = 512
FOX_TQ = 512
MOE_TM = 512
COMBINE_TM = 256
MOE_SPLIT = 4
MLA_SLAB = LANES
VT_ROWS = HEAD_DIM + 16
SC_CORES = 2
SC_SUBCORES = 16
SC_WORKERS = SC_CORES * SC_SUBCORES
SC_CHUNK = 32
SC_LANES = 16
SC_SCAN = 8192


def _cparams(n_axes):
    return pltpu.CompilerParams(dimension_semantics=("arbitrary",) * n_axes,
                                vmem_limit_bytes=VMEM_LIMIT)


def _split3(x):
    hi = x.astype(BF16)
    r1 = x - hi.astype(F32)
    mid = r1.astype(BF16)
    lo = (r1 - mid.astype(F32)).astype(BF16)
    return hi, mid, lo


def _pack_bf16_pairs(v):
    bits = pltpu.bitcast(v.astype(F32), jnp.uint32)
    half = v.shape[1] // 2
    return (bits[:, :half] >> 16) | bits[:, half:]


def _unpack_bf16_pairs(words):
    return jnp.concatenate([pltpu.bitcast(words << 16, F32),
                            pltpu.bitcast(words & jnp.uint32(0xFFFF0000), F32)], axis=1)


def _layer_norm(y, g, b):
    mu = jnp.mean(y, axis=-1, keepdims=True)
    yc = y - mu
    var = jnp.mean(yc * yc, axis=-1, keepdims=True)
    return yc * lax.rsqrt(var + LN_EPS) * g + b


def _rms_norm(y, g):
    return y * lax.rsqrt(jnp.mean(y * y, axis=-1, keepdims=True) + RMS_EPS) * g


def _store_transposed(dst_ref, val):
    for s in range(val.shape[1] // LANES):
        sl = slice(s * LANES, (s + 1) * LANES)
        dst_ref[0, sl, :] = val[:, sl].T.astype(dst_ref.dtype)


def _store_values_transposed(vt_ref, val):
    tm = val.shape[0]
    ones = jnp.ones((VT_ROWS - HEAD_DIM, tm), vt_ref.dtype)
    for s in range(val.shape[1] // LANES):
        pair_t = val[:, s * LANES:(s + 1) * LANES].T.astype(vt_ref.dtype)
        for h in range(2):
            r0 = (2 * s + h) * VT_ROWS
            vt_ref[0, r0:r0 + HEAD_DIM, :] = pair_t[h * HEAD_DIM:(h + 1) * HEAD_DIM]
            vt_ref[0, r0 + HEAD_DIM:r0 + VT_ROWS, :] = ones


def _fox_proj_kernel(x_ref, w_ref, bf_ref, seg_ref, qt_ref, k_ref, vt_ref, mqt_ref, cum_ref,
                     cumt_ref, cfirst_ref, clast_ref, knorm_ref, carry_ref, knmax_ref, *,
                     tiles_per_batch):
    t = pl.program_id(0)

    @pl.when(t % tiles_per_batch == 0)
    def _():
        carry_ref[...] = jnp.zeros_like(carry_ref)
        knmax_ref[...] = jnp.zeros_like(knmax_ref)

    tm = x_ref.shape[0]
    proj = jnp.dot(x_ref[...].astype(BF16), w_ref[...], preferred_element_type=F32)
    qscale = HEAD_DIM ** -0.5 * LOG2E
    _store_transposed(qt_ref, proj[:, :FOX_WIDTH] * qscale)
    kb = proj[:, FOX_WIDTH:2 * FOX_WIDTH].astype(BF16)
    k_ref[...] = kb
    _store_values_transposed(vt_ref, proj[:, 2 * FOX_WIDTH:3 * FOX_WIDTH])
    kf = kb.astype(F32)
    ksq = jnp.dot((kf * kf).astype(BF16), seg_ref[...], preferred_element_type=F32)
    n_sub = tm // FOX_TQ
    for sub in range(n_sub):
        tile_max = jnp.sqrt(jnp.max(ksq[sub * FOX_TQ:(sub + 1) * FOX_TQ], axis=0, keepdims=True))
        knmax_ref[...] = jnp.maximum(knmax_ref[...], tile_max)
        knorm_ref[sub] = knmax_ref[...]
    f = proj[:, 3 * FOX_WIDTH:3 * FOX_WIDTH + LANES] + bf_ref[...]
    _store_transposed(mqt_ref, proj[:, 3 * FOX_WIDTH + LANES:] * qscale)
    log_f = jnp.minimum(f, 0.0) - jnp.log1p(jnp.exp(-jnp.abs(f)))
    row = lax.broadcasted_iota(jnp.int32, (tm, tm), 0)
    col = lax.broadcasted_iota(jnp.int32, (tm, tm), 1)
    tri = jnp.where(row >= col, 1.0, 0.0).astype(BF16)
    hi, mid, lo = _split3(log_f)
    cum = (jnp.dot(tri, hi, preferred_element_type=F32)
           + jnp.dot(tri, mid, preferred_element_type=F32)
           + jnp.dot(tri, lo, preferred_element_type=F32)) + carry_ref[...]
    carry_ref[...] = cum[tm - 1:tm, :]
    cum2 = cum * LOG2E
    cum_ref[...] = cum2
    cumt_ref[0] = cum2.T[:16, :]
    for sub in range(n_sub):
        cfirst_ref[sub] = cum2[sub * FOX_TQ:sub * FOX_TQ + 1, :]
        clast_ref[sub] = cum2[(sub + 1) * FOX_TQ - 1:(sub + 1) * FOX_TQ, :]


def _fox_proj(x2d, w, bf, seq):
    t_tokens = x2d.shape[0]
    tm = PROJ_TM
    nb = t_tokens // seq
    n = w.shape[1]
    tiles_per_batch = seq // tm
    n_tiles = t_tokens // tm
    seg = (jnp.arange(FOX_WIDTH)[:, None] // HEAD_DIM == jnp.arange(LANES)[None, :]).astype(BF16)
    row_spec = lambda width: pl.BlockSpec((tm, width), lambda i: (i, 0))
    t_spec = lambda rows: pl.BlockSpec(
        (1, rows, tm), lambda i: (i // tiles_per_batch, 0, i % tiles_per_batch))
    n_sub = tm // FOX_TQ
    stat_spec = pl.BlockSpec((n_sub, 1, LANES), lambda i: (i, 0, 0))
    stat_shape = jax.ShapeDtypeStruct((n_tiles * n_sub, 1, LANES), F32)
    return pl.pallas_call(
        functools.partial(_fox_proj_kernel, tiles_per_batch=tiles_per_batch),
        grid=(n_tiles,),
        in_specs=[row_spec(D_MODEL),
                  pl.BlockSpec((D_MODEL, n), lambda i: (0, 0)),
                  pl.BlockSpec((1, LANES), lambda i: (0, 0)),
                  pl.BlockSpec((FOX_WIDTH, LANES), lambda i: (0, 0))],
        out_specs=[t_spec(FOX_WIDTH), row_spec(FOX_WIDTH), t_spec(FOX_HEADS * VT_ROWS),
                   t_spec(MEM_WIDTH), row_spec(LANES), t_spec(16),
                   stat_spec, stat_spec, stat_spec],
        out_shape=[jax.ShapeDtypeStruct((nb, FOX_WIDTH, seq), BF16),
                   jax.ShapeDtypeStruct((t_tokens, FOX_WIDTH), BF16),
                   jax.ShapeDtypeStruct((nb, FOX_HEADS * VT_ROWS, seq), BF16),
                   jax.ShapeDtypeStruct((nb, MEM_WIDTH, seq), BF16),
                   jax.ShapeDtypeStruct((t_tokens, LANES), F32),
                   jax.ShapeDtypeStruct((nb, 16, seq), F32),
                   stat_shape, stat_shape, stat_shape],
        scratch_shapes=[pltpu.VMEM((1, LANES), F32), pltpu.VMEM((1, LANES), F32)],
        compiler_params=_cparams(1),
        name="fox_proj",
    )(x2d, w, bf, seg)


def _rope_tables(pos_ref, invf_ref):
    ang = pos_ref[...].astype(F32) * invf_ref[...]
    cos = jnp.cos(ang)
    sin = jnp.sin(ang)
    lane = lax.broadcasted_iota(jnp.int32, ang.shape, 1)
    half = QK_ROPE // 2
    in_x1 = (lane >= QK_NOPE) & (lane < QK_NOPE + half)
    in_x2 = (lane >= QK_NOPE + half) & (lane < QK_NOPE + QK_ROPE)
    c_tab = jnp.where(lane < QK_NOPE, 1.0, jnp.where(in_x1 | in_x2, cos, 0.0))
    s_up = jnp.where(in_x2, sin, 0.0)
    s_dn = jnp.where(in_x1, -sin, 0.0)
    return c_tab, s_up, s_dn


def _rope_slab(slab, tables):
    c_tab, s_up, s_dn = tables
    half = QK_ROPE // 2
    up = pltpu.roll(slab, half, 1)
    dn = pltpu.roll(slab, LANES - half, 1)
    return slab * c_tab + up * s_up + dn * s_dn


def _mla_proj_kernel(x_ref, pos_ref, invf_ref, w_ref, gq_ref, wuq_ref, gkv_ref, wk_ref, wv_ref,
                     qt_ref, mqt_ref, k_ref, vt_ref):
    proj = jnp.dot(x_ref[...].astype(BF16), w_ref[...], preferred_element_type=F32)
    tables = _rope_tables(pos_ref, invf_ref)
    c_q = _rms_norm(proj[:, :Q_LORA], gq_ref[...])
    _store_transposed(mqt_ref, proj[:, Q_LORA:Q_LORA + MEM_WIDTH] * (HEAD_DIM ** -0.5 * LOG2E))
    kv_off = Q_LORA + MEM_WIDTH
    c_kv = _rms_norm(proj[:, kv_off:kv_off + KV_LORA], gkv_ref[...])
    kr = _rope_slab(proj[:, kv_off + KV_LORA:], tables)
    q = jnp.dot(c_q.astype(BF16), wuq_ref[...], preferred_element_type=F32)
    kn = jnp.dot(c_kv.astype(BF16), wk_ref[...], preferred_element_type=F32)
    qscale = (QK_NOPE + QK_ROPE) ** -0.5 * LOG2E
    for h in range(MLA_HEADS):
        sl = slice(h * MLA_SLAB, (h + 1) * MLA_SLAB)
        qt_ref[0, sl, :] = (_rope_slab(q[:, sl], tables) * qscale).T.astype(BF16)
        k_ref[:, sl] = (kn[:, sl] + kr).astype(BF16)
    _store_values_transposed(vt_ref, jnp.dot(c_kv.astype(BF16), wv_ref[...],
                                             preferred_element_type=F32))


def _mla_proj(x2d, pos2d, invf, w, gq, wuq, gkv, wk, wv, seq):
    t_tokens = x2d.shape[0]
    tm = PROJ_TM
    nb = t_tokens // seq
    tiles_per_batch = seq // tm
    row_spec = lambda width: pl.BlockSpec((tm, width), lambda i: (i, 0))
    t_spec = lambda rows: pl.BlockSpec(
        (1, rows, tm), lambda i: (i // tiles_per_batch, 0, i % tiles_per_batch))
    full = lambda a: pl.BlockSpec(a.shape, lambda i: (0, 0))
    slabs = MLA_HEADS * MLA_SLAB
    return pl.pallas_call(
        _mla_proj_kernel,
        grid=(t_tokens // tm,),
        in_specs=[row_spec(D_MODEL), row_spec(1), full(invf), full(w), full(gq), full(wuq),
                  full(gkv), full(wk), full(wv)],
        out_specs=[t_spec(slabs), t_spec(MEM_WIDTH), row_spec(slabs),
                   t_spec(MLA_HEADS * VT_ROWS)],
        out_shape=[jax.ShapeDtypeStruct((nb, slabs, seq), BF16),
                   jax.ShapeDtypeStruct((nb, MEM_WIDTH, seq), BF16),
                   jax.ShapeDtypeStruct((t_tokens, slabs), BF16),
                   jax.ShapeDtypeStruct((nb, MLA_HEADS * VT_ROWS, seq), BF16)],
        compiler_params=_cparams(1),
        name="mla_proj",
    )(x2d, pos2d, invf, w, gq, wuq, gkv, wk, wv)


SKIP_LOG2 = -160.0
NORM_SLACK = 1.02


def _causal_attn_kernel(*refs, tq, n_tiles, fox, slab, long_body):
    if fox:
        (cf_ref, cl_ref, kn_ref, qt_ref, k_ref, vt_ref, cq_ref, ck_ref,
         o_ref, m_ref, acc_ref, s_ref, cmax_ref) = refs
    else:
        qt_ref, k_ref, vt_ref, o_ref, m_ref, acc_ref, s_ref, cmax_ref = refs
    b = pl.program_id(0)
    pair = pl.program_id(1)
    i = pl.program_id(2)
    m_ref[...] = jnp.full(m_ref.shape, NEG_INF, F32)
    acc_ref[...] = jnp.zeros(acc_ref.shape, F32)
    qt = qt_ref[0]
    if slab:
        qth = [qt[:MLA_SLAB], qt[MLA_SLAB:]]
    else:
        rowi = lax.broadcasted_iota(jnp.int32, qt.shape, 0)
        zero = jnp.zeros_like(qt)
        qth = [jnp.where(rowi < HEAD_DIM, qt, zero), jnp.where(rowi >= HEAD_DIM, qt, zero)]
    if fox:
        lane = lax.broadcasted_iota(jnp.int32, (1, LANES), 1)
        cq = [cq_ref[0, pl.ds(2 * pair + h, 1), :] for h in range(2)]

    def logits_to(slot, j):
        off = pl.multiple_of(j * tq, tq)
        kc = k_ref[0, pl.ds(off, tq), :]
        for h in range(2):
            kh = kc[:, h * MLA_SLAB:(h + 1) * MLA_SLAB] if slab else kc
            s = jnp.dot(kh, qth[h], preferred_element_type=F32)
            if fox:
                ck_blk = ck_ref[0, pl.ds(off, tq), :]
                ck = jnp.sum(jnp.where(lane == 2 * pair + h, ck_blk, 0.0), axis=1, keepdims=True)
                s = s + cq[h] - ck
            s_ref[slot, h] = s
            cmax_ref[slot, h] = jnp.max(s, axis=0, keepdims=True)

    def softmax_pv(slot, j, causal_mask):
        off = pl.multiple_of(j * tq, tq)
        for h in range(2):
            s = s_ref[slot, h]
            if causal_mask is not None:
                s = jnp.where(causal_mask, s, NEG_INF)
                chunk_max = jnp.max(s, axis=0, keepdims=True)
            else:
                chunk_max = cmax_ref[slot, h]
            m_prev = m_ref[h]
            m_new = jnp.maximum(m_prev, chunk_max)
            alpha = jnp.exp2(m_prev - m_new)
            p = jnp.exp2(s - m_new).astype(BF16)
            vth = vt_ref[0, h * VT_ROWS:(h + 1) * VT_ROWS, pl.ds(off, tq)]
            acc_ref[h] = acc_ref[h] * alpha + jnp.dot(vth, p, preferred_element_type=F32)
            m_ref[h] = m_new

    if fox:
        base = b * n_tiles
        qn = []
        for h in range(2):
            qf = qth[h].astype(F32)
            qn.append(jnp.sqrt(jnp.max(jnp.sum(qf * qf, axis=0, keepdims=True))) * NORM_SLACK)

        top = [2.0 * qn[h] * kn_ref[base + i, 2 * pair + h] + cf_ref[base + i, 2 * pair + h]
               for h in range(2)]

        def live(j):
            ub = [top[h] - cl_ref[base + jnp.maximum(j, 0), 2 * pair + h] for h in range(2)]
            return (j >= 0) & (jnp.maximum(ub[0], ub[1]) > SKIP_LOG2)

        j0 = lax.while_loop(live, lambda j: j - 1, i - 1) + 1
    else:
        j0 = 0
    n_before = i - j0
    odd = (n_before % 2) == 1

    @pl.when(odd)
    def _():
        logits_to(1, j0)
        logits_to(0, j0 + 1)
        softmax_pv(1, j0, None)

    @pl.when(jnp.logical_not(odd))
    def _():
        logits_to(0, j0)

    def two_steps(j):
        logits_to(1, j + 1)
        softmax_pv(0, j, None)
        logits_to(0, j + 2)
        softmax_pv(1, j + 1, None)

    j_even = j0 + (n_before % 2)
    n_two = n_before // 2
    if long_body:
        @pl.when((n_two % 2) == 1)
        def _():
            two_steps(j_even)

        j_quad = j_even + 2 * (n_two % 2)

        def body(t, carry):
            two_steps(j_quad + 4 * t)
            two_steps(j_quad + 4 * t + 2)
            return carry

        lax.fori_loop(0, n_two // 2, body, 0)
    else:
        def body(t, carry):
            two_steps(j_even + 2 * t)
            return carry

        lax.fori_loop(0, n_two, body, 0)
    key = lax.broadcasted_iota(jnp.int32, (tq, tq), 0)
    qry = lax.broadcasted_iota(jnp.int32, (tq, tq), 1)
    softmax_pv(0, i, key <= qry)
    out_t = jnp.concatenate(
        [acc_ref[h, :HEAD_DIM] * (1.0 / acc_ref[h, HEAD_DIM:HEAD_DIM + 1]) for h in range(2)],
        axis=0)
    o_ref[0] = out_t.T.astype(o_ref.dtype)


def _causal_attention(qt, k, vt, fox_args, *, slab):
    nb, seq, _ = k.shape
    fox = fox_args is not None
    tq = FOX_TQ if fox else ATT_TQ
    n_tiles = seq // tq
    n_pairs = FOX_HEADS // 2
    rows = 2 * MLA_SLAB if slab else LANES
    in_specs = [pl.BlockSpec((1, rows, tq), lambda b, p, i: (b, p, i)),
                pl.BlockSpec((1, seq, rows), lambda b, p, i: (b, 0, p)),
                pl.BlockSpec((1, 2 * VT_ROWS, seq), lambda b, p, i: (b, p, 0))]
    args = [qt, k, vt]
    if fox:
        cum, cumt, cfirst, clast, knorm = fox_args
        smem = pl.BlockSpec(memory_space=pltpu.SMEM)
        in_specs = [smem, smem, smem] + in_specs + [
            pl.BlockSpec((1, 16, tq), lambda b, p, i: (b, 0, i)),
            pl.BlockSpec((1, seq, LANES), lambda b, p, i: (b, 0, 0))]
        args = [cfirst, clast, knorm] + args + [cumt, cum]
    return pl.pallas_call(
        functools.partial(_causal_attn_kernel, tq=tq, n_tiles=n_tiles, fox=fox, slab=slab,
                          long_body=not fox),
        grid=(nb, n_pairs, n_tiles),
        in_specs=in_specs,
        out_specs=pl.BlockSpec((1, tq, LANES), lambda b, p, i: (b, i, p)),
        out_shape=jax.ShapeDtypeStruct((nb, seq, n_pairs * LANES), BF16),
        scratch_shapes=[pltpu.VMEM((2, 1, tq), F32), pltpu.VMEM((2, VT_ROWS, tq), F32),
                        pltpu.VMEM((2, 2, tq, tq), F32), pltpu.VMEM((2, 2, 1, tq), F32)],
        compiler_params=_cparams(3),
        name="fox_attention" if fox else "mla_attention",
    )(*args)


def _mem_proj_kernel(mem_ref, w_ref, mk_ref, mvt_ref):
    mkv = jnp.dot(mem_ref[...].astype(BF16), w_ref[...], preferred_element_type=F32)
    mk_ref[0] = mkv[:, :MEM_WIDTH].astype(BF16)
    _store_values_transposed(mvt_ref, mkv[:, MEM_WIDTH:])


def _mem_proj(mem2d, w, nb):
    return pl.pallas_call(
        _mem_proj_kernel,
        grid=(nb,),
        in_specs=[pl.BlockSpec((N_MEM, D_MODEL), lambda b: (b, 0)),
                  pl.BlockSpec(w.shape, lambda b: (0, 0))],
        out_specs=[pl.BlockSpec((1, N_MEM, MEM_WIDTH), lambda b: (b, 0, 0)),
                   pl.BlockSpec((1, MEM_HEADS * VT_ROWS, N_MEM), lambda b: (b, 0, 0))],
        out_shape=[jax.ShapeDtypeStruct((nb, N_MEM, MEM_WIDTH), BF16),
                   jax.ShapeDtypeStruct((nb, MEM_HEADS * VT_ROWS, N_MEM), BF16)],
        compiler_params=_cparams(1),
        name="mem_proj",
    )(mem2d, w)


def _mem_attn_kernel(qt_ref, k_ref, vt_ref, o_ref):
    qt = qt_ref[0]
    kc = k_ref[0]
    for pair in range(MEM_HEADS // 2):
        qt_p = qt[pair * LANES:(pair + 1) * LANES]
        k_p = kc[:, pair * LANES:(pair + 1) * LANES]
        rowi = lax.broadcasted_iota(jnp.int32, qt_p.shape, 0)
        zero = jnp.zeros_like(qt_p)
        outs = []
        for h in range(2):
            mine = (rowi < HEAD_DIM) if h == 0 else (rowi >= HEAD_DIM)
            s = jnp.dot(k_p, jnp.where(mine, qt_p, zero), preferred_element_type=F32)
            p = jnp.exp2(s - jnp.max(s, axis=0, keepdims=True)).astype(BF16)
            r0 = (2 * pair + h) * VT_ROWS
            acc = jnp.dot(vt_ref[0, r0:r0 + VT_ROWS, :], p, preferred_element_type=F32)
            outs.append(acc[:HEAD_DIM] * (1.0 / acc[HEAD_DIM:HEAD_DIM + 1]))
        o_ref[0, :, pair * LANES:(pair + 1) * LANES] = (
            jnp.concatenate(outs, axis=0).T.astype(o_ref.dtype))


def _memory_attention(mqt, mk, mvt):
    nb, _, seq = mqt.shape
    tq = ATT_TQ
    return pl.pallas_call(
        _mem_attn_kernel,
        grid=(nb, seq // tq),
        in_specs=[pl.BlockSpec((1, MEM_WIDTH, tq), lambda b, i: (b, 0, i)),
                  pl.BlockSpec((1, N_MEM, MEM_WIDTH), lambda b, i: (b, 0, 0)),
                  pl.BlockSpec((1, MEM_HEADS * VT_ROWS, N_MEM), lambda b, i: (b, 0, 0))],
        out_specs=pl.BlockSpec((1, tq, MEM_WIDTH), lambda b, i: (b, i, 0)),
        out_shape=jax.ShapeDtypeStruct((nb, seq, MEM_WIDTH), BF16),
        compiler_params=_cparams(2),
        name="memory_attention",
    )(mqt, mk, mvt)


def _post_attn_kernel(x_ref, att_ref, memo_ref, wa_ref, wm_ref, g_ref, b_ref, wr_ref, br_ref,
                      x1_ref, x1p_ref, idx_ref, rank_ref, gate_ref, cnt_ref, carry_ref):
    t = pl.program_id(0)

    @pl.when(t == 0)
    def _():
        carry_ref[...] = jnp.zeros_like(carry_ref)

    tm = x_ref.shape[0]
    mix = (jnp.dot(att_ref[...], wa_ref[...], preferred_element_type=F32)
           + jnp.dot(memo_ref[...], wm_ref[...], preferred_element_type=F32))
    x1 = _layer_norm(DEEPNORM_ALPHA * x_ref[...] + mix, g_ref[...], b_ref[...])
    x1_ref[...] = x1
    x1b = x1.astype(BF16)
    x1p_ref[...] = _pack_bf16_pairs(x1b)
    logits = jnp.dot(x1b, wr_ref[...], preferred_element_type=F32) + br_ref[...]
    lane = lax.broadcasted_iota(jnp.int32, (tm, LANES), 1)
    work = jnp.where(lane < N_EXPERTS, logits, -jnp.inf)
    idxs, vals = [], []
    onehot = jnp.zeros((tm, LANES), F32)
    for _ in range(TOP_K):
        best = jnp.max(work, axis=1, keepdims=True)
        where_best = jnp.argmax(work, axis=1, keepdims=True).astype(jnp.int32)
        hit = lane == where_best
        onehot = jnp.where(hit, 1.0, onehot)
        work = jnp.where(hit, -jnp.inf, work)
        idxs.append(where_best)
        vals.append(best)
    exps = [jnp.exp(v - vals[0]) for v in vals]
    denom = exps[0] + exps[1] + exps[2] + exps[3]
    row = lax.broadcasted_iota(jnp.int32, (tm, tm), 0)
    col = lax.broadcasted_iota(jnp.int32, (tm, tm), 1)
    strict = jnp.where(row > col, 1.0, 0.0).astype(BF16)
    before = jnp.dot(strict, onehot.astype(BF16), preferred_element_type=F32) + carry_ref[...]
    idx_out = jnp.zeros((tm, LANES), F32)
    rank_out = jnp.zeros((tm, LANES), F32)
    gate_out = jnp.zeros((tm, LANES), F32)
    for r in range(TOP_K):
        rank_r = jnp.sum(jnp.where(lane == idxs[r], before, 0.0), axis=1, keepdims=True)
        idx_out = jnp.where(lane == r, idxs[r].astype(F32), idx_out)
        rank_out = jnp.where(lane == r, rank_r, rank_out)
        gate_out = jnp.where(lane == r, exps[r] / denom, gate_out)
    idx_ref[...] = idx_out.T[:8].astype(jnp.int32)
    rank_ref[...] = rank_out.T[:8].astype(jnp.int32)
    gate_ref[...] = gate_out
    total = carry_ref[...] + jnp.sum(onehot, axis=0, keepdims=True)
    carry_ref[...] = total
    cnt_ref[...] = jnp.broadcast_to(total, cnt_ref.shape)


def _post_attn(x2d, att, memo, wa, wm, g, b, wr, br):
    t_tokens = x2d.shape[0]
    tm = PROJ_TM
    row_spec = lambda width: pl.BlockSpec((tm, width), lambda i: (i, 0))
    full = lambda a: pl.BlockSpec(a.shape, lambda i: (0, 0))
    return pl.pallas_call(
        _post_attn_kernel,
        grid=(t_tokens // tm,),
        in_specs=[row_spec(D_MODEL), row_spec(att.shape[1]), row_spec(MEM_WIDTH),
                  full(wa), full(wm), full(g), full(b), full(wr), full(br)],
        out_specs=[row_spec(D_MODEL), row_spec(D_MODEL // 2),
                   pl.BlockSpec((8, tm), lambda i: (0, i)), pl.BlockSpec((8, tm), lambda i: (0, i)),
                   row_spec(LANES), pl.BlockSpec((8, LANES), lambda i: (0, 0))],
        out_shape=[jax.ShapeDtypeStruct((t_tokens, D_MODEL), F32),
                   jax.ShapeDtypeStruct((t_tokens, D_MODEL // 2), jnp.uint32),
                   jax.ShapeDtypeStruct((8, t_tokens), jnp.int32),
                   jax.ShapeDtypeStruct((8, t_tokens), jnp.int32),
                   jax.ShapeDtypeStruct((t_tokens, LANES), F32),
                   jax.ShapeDtypeStruct((8, LANES), F32)],
        scratch_shapes=[pltpu.VMEM((1, LANES), F32)],
        compiler_params=_cparams(1),
        name="outproj_ln_router",
    )(x2d, att, memo, wa, wm, g, b, wr, br)


def _expert_kernel(be_ref, nu_ref, br_ref, x_ref, wgu_hbm, bgu_ref, wdn_hbm, bdn_ref, *rest,
                   n_chunks, blk_lo, n_blocks, layer, has_prev):
    o_ref, wgu_st, wdn_st, wgu_bf, wdn_bf, h_ref, sem = rest[1:] if has_prev else rest
    step = pl.program_id(0)
    blk = step + blk_lo
    live_end = jnp.minimum(blk_lo + pl.num_programs(0), nu_ref[0])
    expert = be_ref[blk]

    def fetch(e):
        return (pltpu.make_async_copy(wgu_hbm.at[layer, e], wgu_st, sem.at[0]),
                pltpu.make_async_copy(wdn_hbm.at[layer, e], wdn_st, sem.at[1]))

    @pl.when(((step == 0) | (expert != be_ref[jnp.maximum(blk - 1, 0)])) & (blk < live_end))
    def _():
        @pl.when(step == 0)
        def _():
            for copy in fetch(expert):
                copy.start()

        for copy in fetch(expert):
            copy.wait()
        wgu_bf[...] = wgu_st[...].astype(BF16)
        wdn_bf[...] = wdn_st[...].astype(BF16)
        nxt = lax.while_loop(
            lambda j: (j < live_end) & (be_ref[jnp.minimum(j, n_blocks - 1)] == expert),
            lambda j: j + 1, blk + 1)

        @pl.when(nxt < live_end)
        def _():
            for copy in fetch(be_ref[jnp.minimum(nxt, n_blocks - 1)]):
                copy.start()

    def mlp(rows):
        x = _unpack_bf16_pairs(x_ref[:rows]).astype(BF16)
        cw = D_EXPERT // n_chunks
        for c in range(n_chunks):
            gs = slice(c * cw, (c + 1) * cw)
            us = slice(D_EXPERT + c * cw, D_EXPERT + (c + 1) * cw)
            g = jnp.dot(x, wgu_bf[:, gs], preferred_element_type=F32) + bgu_ref[0, 0, :, gs]
            u = jnp.dot(x, wgu_bf[:, us], preferred_element_type=F32) + bgu_ref[0, 0, :, us]
            g = jnp.minimum(g, SWIGLU_LIMIT)
            u = jnp.clip(u, -SWIGLU_LIMIT, SWIGLU_LIMIT)
            h_ref[:rows, gs] = ((u + 1.0) * (g * jax.nn.sigmoid(SWIGLU_ALPHA * g))).astype(BF16)
        y = jnp.dot(h_ref[:rows], wdn_bf[...], preferred_element_type=F32) + bdn_ref[0, 0]
        return _pack_bf16_pairs(y.astype(BF16))

    tm = o_ref.shape[0]
    live_rows = br_ref[blk]

    @pl.when(live_rows > tm // 2)
    def _():
        o_ref[...] = mlp(tm)

    @pl.when((live_rows > 0) & (live_rows <= tm // 2))
    def _():
        o_ref[:tm // 2] = mlp(tm // 2)
        o_ref[tm // 2:] = jnp.zeros((tm - tm // 2, o_ref.shape[1]), o_ref.dtype)

    @pl.when(live_rows == 0)
    def _():
        o_ref[...] = jnp.zeros_like(o_ref)


def _experts(block_expert, n_used, block_rows, xb, blk_lo, p_rows, y_prev, layer, wgu, bgu, wdn, bdn):
    tm = MOE_TM
    has_prev = y_prev is not None
    b_map = lambda i, be, nu, br: (layer, be[i + blk_lo], 0, 0)
    in_specs = [pl.BlockSpec((tm, D_MODEL // 2), lambda i, be, nu, br: (i, 0)),
                pl.BlockSpec(memory_space=pl.ANY),
                pl.BlockSpec((1, 1, 1, 2 * D_EXPERT), b_map),
                pl.BlockSpec(memory_space=pl.ANY),
                pl.BlockSpec((1, 1, 1, D_MODEL), b_map)]
    args = [block_expert, n_used, block_rows, xb, wgu, bgu, wdn, bdn]
    if has_prev:
        in_specs.append(pl.BlockSpec(memory_space=pl.ANY))
        args.append(y_prev)
    grid_spec = pltpu.PrefetchScalarGridSpec(
        num_scalar_prefetch=3,
        grid=(xb.shape[0] // tm,),
        in_specs=in_specs,
        out_specs=pl.BlockSpec((tm, D_MODEL // 2), lambda i, be, nu, br: (i + blk_lo, 0)),
        scratch_shapes=[pltpu.VMEM((D_MODEL, 2 * D_EXPERT), F32),
                        pltpu.VMEM((D_EXPERT, D_MODEL), F32),
                        pltpu.VMEM((D_MODEL, 2 * D_EXPERT), BF16),
                        pltpu.VMEM((D_EXPERT, D_MODEL), BF16),
                        pltpu.VMEM((tm, D_EXPERT), BF16),
                        pltpu.SemaphoreType.DMA((2,))],
    )
    return pl.pallas_call(
        functools.partial(_expert_kernel, n_chunks=4, blk_lo=blk_lo, n_blocks=p_rows // tm,
                          layer=layer, has_prev=has_prev),
        grid_spec=grid_spec,
        out_shape=jax.ShapeDtypeStruct((p_rows, D_MODEL // 2), jnp.uint32),
        input_output_aliases={len(args) - 1: 0} if has_prev else {},
        compiler_params=pltpu.CompilerParams(dimension_semantics=("arbitrary",),
                                             vmem_limit_bytes=EXPERT_VMEM_LIMIT),
        name="experts",
    )(*args)


def _sc_move_rows(table_hbm, idx_v, out_hbm, base, n_chunks, rows_v, gsem, wsem):
    def gather(c, slot):
        off = pl.multiple_of(c * SC_CHUNK, SC_CHUNK)
        return pltpu.make_async_copy(table_hbm.at[idx_v.at[pl.ds(off, SC_CHUNK)]],
                                     rows_v.at[slot], gsem.at[slot])

    def put(c, slot):
        off = pl.multiple_of(c * SC_CHUNK, SC_CHUNK)
        return pltpu.make_async_copy(rows_v.at[slot], out_hbm.at[pl.ds(base + off, SC_CHUNK)],
                                     wsem.at[slot])

    gather(0, 0).start()
    gather(1, 1).start()

    @pl.loop(0, n_chunks, step=2)
    def _(c):
        for slot in range(2):
            gather(c + slot, slot).wait()
            put(c + slot, slot).start()
        for slot in range(2):
            put(c + slot, slot).wait()

            @pl.when(c + 2 + slot < n_chunks)
            def _():
                gather(c + 2 + slot, slot).start()


def _sc_gather_rows(table, idx):
    n_idx = idx.shape[0]
    width = table.shape[1]
    per_worker = n_idx // SC_WORKERS
    n_chunks = per_worker // SC_CHUNK
    mesh = plsc.VectorSubcoreMesh(core_axis_name="c", subcore_axis_name="s",
                                  num_cores=SC_CORES, num_subcores=SC_SUBCORES)

    def body(table_hbm, idx_hbm, out_hbm, idx_v, rows_v, gsem, wsem):
        wid = lax.axis_index("s") * SC_CORES + lax.axis_index("c")
        base = wid * per_worker
        pltpu.sync_copy(idx_hbm.at[pl.ds(base, per_worker)], idx_v)
        _sc_move_rows(table_hbm, idx_v, out_hbm, base, n_chunks, rows_v, gsem, wsem)

    return pl.kernel(
        body,
        out_type=jax.ShapeDtypeStruct((n_idx, width), table.dtype),
        mesh=mesh,
        scratch_types=[pltpu.VMEM((per_worker,), jnp.int32),
                       pltpu.VMEM((2, SC_CHUNK, width), table.dtype),
                       pltpu.SemaphoreType.DMA((2,)),
                       pltpu.SemaphoreType.DMA((2,))],
        name="sc_gather_rows",
    )(table, idx)


def _sc_dispatch_rows(table, dest, slot_lo, n_slots):
    n_tok, width = table.shape
    n_pairs = dest.shape[0]
    per_worker = n_slots // SC_WORKERS
    n_chunks = per_worker // SC_CHUNK
    n_scan = n_pairs // SC_SCAN
    mesh = plsc.VectorSubcoreMesh(core_axis_name="c", subcore_axis_name="s",
                                  num_cores=SC_CORES, num_subcores=SC_SUBCORES)

    def body(table_hbm, dest_hbm, out_hbm, tok_v, dest_v, rows_v, gsem, wsem):
        wid = lax.axis_index("s") * SC_CORES + lax.axis_index("c")
        base = wid * per_worker
        first = slot_lo + base
        lane = lax.iota(jnp.int32, SC_LANES)

        @pl.loop(0, per_worker // SC_LANES)
        def _(i):
            off = pl.multiple_of(i * SC_LANES, SC_LANES)
            tok_v[pl.ds(off, SC_LANES)] = lax.rem(first + off + lane, n_tok)

        @pl.loop(0, n_scan)
        def _(g):
            goff = pl.multiple_of(g * SC_SCAN, SC_SCAN)
            pltpu.sync_copy(dest_hbm.at[pl.ds(goff, SC_SCAN)], dest_v)

            @pl.loop(0, SC_SCAN // SC_LANES)
            def _(i):
                off = pl.multiple_of(i * SC_LANES, SC_LANES)
                local = dest_v[pl.ds(off, SC_LANES)] - first
                mine = (local >= 0) & (local < per_worker)
                pair = goff + off + lane
                plsc.store_scatter(tok_v, [jnp.where(mine, local, 0)], lax.rem(pair, n_tok), mask=mine)

        _sc_move_rows(table_hbm, tok_v, out_hbm, base, n_chunks, rows_v, gsem, wsem)

    return pl.kernel(
        body,
        out_type=jax.ShapeDtypeStruct((n_slots, width), table.dtype),
        mesh=mesh,
        scratch_types=[pltpu.VMEM((per_worker,), jnp.int32),
                       pltpu.VMEM((SC_SCAN,), jnp.int32),
                       pltpu.VMEM((2, SC_CHUNK, width), table.dtype),
                       pltpu.SemaphoreType.DMA((2,)),
                       pltpu.SemaphoreType.DMA((2,))],
        compiler_params=pltpu.CompilerParams(needs_layout_passes=False),
        name="sc_dispatch_rows",
    )(table, dest)


def _combine_kernel(x1_ref, yg_ref, gate_ref, g_ref, b_ref, *rest):
    o_ref = rest[-1]
    gates = gate_ref[...]
    ffn = _unpack_bf16_pairs(yg_ref[0]) * gates[:, 0:1]
    for r in range(1, TOP_K):
        ffn = ffn + _unpack_bf16_pairs(yg_ref[r]) * gates[:, r:r + 1]
    o_ref[...] = _layer_norm(DEEPNORM_ALPHA * x1_ref[...] + ffn, g_ref[...], b_ref[...])


def _combine(x1, yg, gates, g, b, tok_lo, out_prev):
    t_tokens = x1.shape[0]
    tm = COMBINE_TM
    tile_lo = tok_lo // tm
    in_specs = [pl.BlockSpec((tm, D_MODEL), lambda i: (i + tile_lo, 0)),
                pl.BlockSpec((TOP_K, tm, D_MODEL // 2), lambda i: (0, i, 0)),
                pl.BlockSpec((tm, LANES), lambda i: (i + tile_lo, 0)),
                pl.BlockSpec((1, D_MODEL), lambda i: (0, 0)),
                pl.BlockSpec((1, D_MODEL), lambda i: (0, 0))]
    args = [x1, yg, gates, g, b]
    if out_prev is not None:
        in_specs.append(pl.BlockSpec(memory_space=pl.ANY))
        args.append(out_prev)
    return pl.pallas_call(
        _combine_kernel,
        grid=(yg.shape[1] // tm,),
        in_specs=in_specs,
        out_specs=pl.BlockSpec((tm, D_MODEL), lambda i: (i + tile_lo, 0)),
        out_shape=jax.ShapeDtypeStruct((t_tokens, D_MODEL), F32),
        input_output_aliases={len(args) - 1: 0} if out_prev is not None else {},
        compiler_params=_cparams(1),
        name="combine_ln",
    )(*args)


def _moe(x1, x1p, idx, rank, gates, counts, layer, wgu, bgu, wdn, bdn, g, b):
    t_tokens = x1.shape[0]
    tm = MOE_TM
    counts = counts[0, :N_EXPERTS].astype(jnp.int32)
    padded = ((counts + tm - 1) // tm) * tm
    pend = jnp.cumsum(padded)
    pstart = pend - padded
    dest = rank[:TOP_K]
    for e in range(N_EXPERTS):
        dest = dest + jnp.where(idx[:TOP_K] == e, pstart[e], 0)
    n_blocks = (t_tokens * TOP_K) // tm + N_EXPERTS
    p_rows = n_blocks * tm
    block_start = jnp.arange(n_blocks, dtype=jnp.int32) * tm
    block_expert = jnp.minimum(
        jnp.sum((pend[None, :] <= block_start[:, None]).astype(jnp.int32), axis=1), N_EXPERTS - 1)
    n_used = (pend[-1:] // tm).astype(jnp.int32)
    block_rows = jnp.clip(counts[block_expert] - (block_start - pstart[block_expert]), 0, tm)
    y = None
    for lo, hi in ((0, n_blocks // MOE_SPLIT), (n_blocks // MOE_SPLIT, n_blocks)):
        xb = _sc_dispatch_rows(x1p, dest.reshape(-1), lo * tm, (hi - lo) * tm)
        y = _experts(block_expert, n_used, block_rows, xb, lo, p_rows, y, layer, wgu, bgu, wdn, bdn)
    out = None
    for lo, hi in ((0, t_tokens // MOE_SPLIT), (t_tokens // MOE_SPLIT, t_tokens)):
        yg = _sc_gather_rows(y, dest[:, lo:hi].reshape(-1)).reshape(TOP_K, hi - lo, D_MODEL // 2)
        out = _combine(x1, yg, gates, g, b, lo, out)
    return out


def _row(v, width=None):
    v = v.astype(F32).reshape(1, -1)
    if width is not None and v.shape[1] < width:
        v = jnp.pad(v, ((0, 0), (0, width - v.shape[1])))
    return v


def _pad_cols(w, width):
    return jnp.pad(w, ((0, 0), (0, width - w.shape[1])))


def kernel(x, mem, positions, a_w_in, a_b_f, a_w_out, b_w_in, b_g_q, b_w_uq, b_w_out,
           kv_w_dkv, kv_g, kv_w_ukv, mem_w_kv, ln_g, ln_b,
           moe_w_r, moe_b_r, moe_w_gu, moe_b_gu, moe_w_dn, moe_b_dn):
    nb, seq, d = x.shape
    t_tokens = nb * seq
    n_a = a_w_in.shape[0]
    x2d = x.reshape(t_tokens, d)
    mem2d = mem.reshape(nb * N_MEM, d)
    pos2d = positions.reshape(t_tokens, 1)
    half = QK_ROPE // 2
    inv_freq = ROPE_THETA ** (-jnp.arange(half, dtype=F32) * 2.0 / QK_ROPE)
    invf = jnp.zeros((1, LANES), F32)
    invf = invf.at[0, QK_NOPE:QK_NOPE + half].set(inv_freq)
    invf = invf.at[0, QK_NOPE + half:QK_NOPE + QK_ROPE].set(inv_freq)

    shared_kv = None
    for l in range(DEPTH):
        mk, mvt = _mem_proj(mem2d, mem_w_kv[l].astype(BF16), nb)
        if l < n_a:
            w_in = a_w_in[l]
            w = jnp.concatenate([w_in[:, :3 * FOX_WIDTH],
                                 _pad_cols(w_in[:, 3 * FOX_WIDTH:3 * FOX_WIDTH + FOX_HEADS], LANES),
                                 w_in[:, 3 * FOX_WIDTH + FOX_HEADS:]], axis=1).astype(BF16)
            qt, k, vt, mqt, cum, cumt, cfirst, clast, knorm = _fox_proj(
                x2d, w, _row(a_b_f[l], LANES), seq)
            stats = [s.reshape(-1, LANES) for s in (cfirst, clast, knorm)]
            att = _causal_attention(qt, k.reshape(nb, seq, -1), vt,
                                    [cum.reshape(nb, seq, LANES), cumt] + stats, slab=False)
            w_out = a_w_out[l]
        else:
            bl = l - n_a
            w = jnp.concatenate([b_w_in[bl], kv_w_dkv[:, :KV_LORA],
                                 jnp.zeros((d, QK_NOPE), F32), kv_w_dkv[:, KV_LORA:],
                                 jnp.zeros((d, LANES - QK_NOPE - QK_ROPE), F32)], axis=1).astype(BF16)
            wuq = b_w_uq[bl].reshape(Q_LORA, MLA_HEADS, QK_NOPE + QK_ROPE)
            wuq = jnp.pad(wuq, ((0, 0), (0, 0), (0, MLA_SLAB - QK_NOPE - QK_ROPE)))
            wuq = wuq.reshape(Q_LORA, MLA_HEADS * MLA_SLAB).astype(BF16)
            wukv = kv_w_ukv.reshape(KV_LORA, MLA_HEADS, QK_NOPE + V_DIM)
            wk = jnp.pad(wukv[:, :, :QK_NOPE], ((0, 0), (0, 0), (0, MLA_SLAB - QK_NOPE)))
            wk = wk.reshape(KV_LORA, MLA_HEADS * MLA_SLAB).astype(BF16)
            wv = wukv[:, :, QK_NOPE:].reshape(KV_LORA, MLA_V_WIDTH).astype(BF16)
            qt, mqt, k_new, vt_new = _mla_proj(x2d, pos2d, invf, w, _row(b_g_q[bl]), wuq,
                                              _row(kv_g), wk, wv, seq)
            if shared_kv is None:
                shared_kv = (k_new.reshape(nb, seq, -1), vt_new)
            att = _causal_attention(qt, shared_kv[0], shared_kv[1], None, slab=True)
            w_out = b_w_out[bl]
        memo = _memory_attention(mqt, mk, mvt)
        n_att = w_out.shape[0] - MEM_WIDTH
        x1, x1p, idx, rank, gates, counts = _post_attn(
            x2d, att.reshape(t_tokens, -1), memo.reshape(t_tokens, MEM_WIDTH),
            w_out[:n_att].astype(BF16), w_out[n_att:].astype(BF16),
            _row(ln_g[l, 0]), _row(ln_b[l, 0]),
            _pad_cols(moe_w_r[l], LANES).astype(BF16), _row(moe_b_r[l], LANES))
        x2d = _moe(x1, x1p, idx, rank, gates, counts, l,
                   moe_w_gu, moe_b_gu.reshape(DEPTH, N_EXPERTS, 1, -1),
                   moe_w_dn, moe_b_dn.reshape(DEPTH, N_EXPERTS, 1, -1),
                   _row(ln_g[l, 1]), _row(ln_b[l, 1]))
    return x2d.reshape(nb, seq, d)
```

```python
import functools
import math

import jax
import jax.numpy as jnp
from jax import lax
from jax.experimental import pallas as pl
from jax.experimental.pallas import tpu as pltpu
from jax.experimental.pallas import tpu_sc as plsc

F32 = jnp.float32
BF16 = jnp.bfloat16

D_MODEL = 1024
DEPTH = 2
N_MEM = 256
HEAD_DIM = 64
FOX_HEADS = 12
MEM_HEADS = 4
MLA_HEADS = 12
Q_LORA = 384
KV_LORA = 256
QK_NOPE = 64
QK_ROPE = 32
V_DIM = 64
ROPE_THETA = 10000.0
N_EXPERTS = 32
TOP_K = 4
D_EXPERT = D_MODEL
SWIGLU_LIMIT = 7.0
SWIGLU_ALPHA = 1.702
LN_EPS = 1e-5
RMS_EPS = 1e-6
NEG_INF = -1e30
DEEPNORM_ALPHA = (2 * DEPTH) ** 0.25
FOX_WIDTH = FOX_HEADS * HEAD_DIM
MEM_WIDTH = MEM_HEADS * HEAD_DIM
MLA_V_WIDTH = MLA_HEADS * V_DIM

LANES = 128
LOG2E = math.log2(math.e)
VMEM_LIMIT = 48 * 1024 * 1024
EXPERT_VMEM_LIMIT = 48 * 1024 * 1024

PROJ_TM = 512
ATT_TQ = 512
FOX_TQ = 512
MOE_TM = 512
COMBINE_TM = 512
MOE_SPLIT = 4
MLA_SLAB = LANES
VT_ROWS = HEAD_DIM + 16
SC_CORES = 2
SC_SUBCORES = 16
SC_WORKERS = SC_CORES * SC_SUBCORES
SC_CHUNK = 32
SC_LANES = 16
SC_SCAN = 8192


def _cparams(n_axes):
    return pltpu.CompilerParams(dimension_semantics=("arbitrary",) * n_axes,
                                vmem_limit_bytes=VMEM_LIMIT)


def _split3(x):
    hi = x.astype(BF16)
    r1 = x - hi.astype(F32)
    mid = r1.astype(BF16)
    lo = (r1 - mid.astype(F32)).astype(BF16)
    return hi, mid, lo


def _pack_bf16_pairs(v):
    bits = pltpu.bitcast(v.astype(F32), jnp.uint32)
    half = v.shape[1] // 2
    return (bits[:, :half] >> 16) | bits[:, half:]


def _unpack_bf16_pairs(words):
    return jnp.concatenate([pltpu.bitcast(words << 16, F32),
                            pltpu.bitcast(words & jnp.uint32(0xFFFF0000), F32)], axis=1)


def _layer_norm(y, g, b):
    mu = jnp.mean(y, axis=-1, keepdims=True)
    yc = y - mu
    var = jnp.mean(yc * yc, axis=-1, keepdims=True)
    return yc * lax.rsqrt(var + LN_EPS) * g + b


def _rms_norm(y, g):
    return y * lax.rsqrt(jnp.mean(y * y, axis=-1, keepdims=True) + RMS_EPS) * g


def _store_transposed(dst_ref, val):
    for s in range(val.shape[1] // LANES):
        sl = slice(s * LANES, (s + 1) * LANES)
        dst_ref[0, sl, :] = val[:, sl].T.astype(dst_ref.dtype)


def _store_values_transposed(vt_ref, val):
    tm = val.shape[0]
    ones = jnp.ones((VT_ROWS - HEAD_DIM, tm), vt_ref.dtype)
    for s in range(val.shape[1] // LANES):
        pair_t = val[:, s * LANES:(s + 1) * LANES].T.astype(vt_ref.dtype)
        for h in range(2):
            r0 = (2 * s + h) * VT_ROWS
            vt_ref[0, r0:r0 + HEAD_DIM, :] = pair_t[h * HEAD_DIM:(h + 1) * HEAD_DIM]
            vt_ref[0, r0 + HEAD_DIM:r0 + VT_ROWS, :] = ones


def _fox_proj_kernel(x_ref, w_ref, bf_ref, seg_ref, qt_ref, k_ref, vt_ref, mqt_ref, cum_ref,
                     cumt_ref, cfirst_ref, clast_ref, knorm_ref, carry_ref, knmax_ref, *,
                     tiles_per_batch):
    t = pl.program_id(0)

    @pl.when(t % tiles_per_batch == 0)
    def _():
        carry_ref[...] = jnp.zeros_like(carry_ref)
        knmax_ref[...] = jnp.zeros_like(knmax_ref)

    tm = x_ref.shape[0]
    proj = jnp.dot(x_ref[...].astype(BF16), w_ref[...], preferred_element_type=F32)
    qscale = HEAD_DIM ** -0.5 * LOG2E
    _store_transposed(qt_ref, proj[:, :FOX_WIDTH] * qscale)
    kb = proj[:, FOX_WIDTH:2 * FOX_WIDTH].astype(BF16)
    k_ref[...] = kb
    _store_values_transposed(vt_ref, proj[:, 2 * FOX_WIDTH:3 * FOX_WIDTH])
    kf = kb.astype(F32)
    ksq = jnp.dot((kf * kf).astype(BF16), seg_ref[...], preferred_element_type=F32)
    n_sub = tm // FOX_TQ
    for sub in range(n_sub):
        tile_max = jnp.sqrt(jnp.max(ksq[sub * FOX_TQ:(sub + 1) * FOX_TQ], axis=0, keepdims=True))
        knmax_ref[...] = jnp.maximum(knmax_ref[...], tile_max)
        knorm_ref[sub] = knmax_ref[...]
    f = proj[:, 3 * FOX_WIDTH:3 * FOX_WIDTH + LANES] + bf_ref[...]
    _store_transposed(mqt_ref, proj[:, 3 * FOX_WIDTH + LANES:] * qscale)
    log_f = jnp.minimum(f, 0.0) - jnp.log1p(jnp.exp(-jnp.abs(f)))
    row = lax.broadcasted_iota(jnp.int32, (tm, tm), 0)
    col = lax.broadcasted_iota(jnp.int32, (tm, tm), 1)
    tri = jnp.where(row >= col, 1.0, 0.0).astype(BF16)
    hi, mid, lo = _split3(log_f)
    cum = (jnp.dot(tri, hi, preferred_element_type=F32)
           + jnp.dot(tri, mid, preferred_element_type=F32)
           + jnp.dot(tri, lo, preferred_element_type=F32)) + carry_ref[...]
    carry_ref[...] = cum[tm - 1:tm, :]
    cum2 = cum * LOG2E
    cum_ref[...] = cum2
    cumt_ref[0] = cum2.T[:16, :]
    for sub in range(n_sub):
        cfirst_ref[sub] = cum2[sub * FOX_TQ:sub * FOX_TQ + 1, :]
        clast_ref[sub] = cum2[(sub + 1) * FOX_TQ - 1:(sub + 1) * FOX_TQ, :]


def _fox_proj(x2d, w, bf, seq):
    t_tokens = x2d.shape[0]
    tm = PROJ_TM
    nb = t_tokens // seq
    n = w.shape[1]
    tiles_per_batch = seq // tm
    n_tiles = t_tokens // tm
    seg = (jnp.arange(FOX_WIDTH)[:, None] // HEAD_DIM == jnp.arange(LANES)[None, :]).astype(BF16)
    row_spec = lambda width: pl.BlockSpec((tm, width), lambda i: (i, 0))
    t_spec = lambda rows: pl.BlockSpec(
        (1, rows, tm), lambda i: (i // tiles_per_batch, 0, i % tiles_per_batch))
    n_sub = tm // FOX_TQ
    stat_spec = pl.BlockSpec((n_sub, 1, LANES), lambda i: (i, 0, 0))
    stat_shape = jax.ShapeDtypeStruct((n_tiles * n_sub, 1, LANES), F32)
    return pl.pallas_call(
        functools.partial(_fox_proj_kernel, tiles_per_batch=tiles_per_batch),
        grid=(n_tiles,),
        in_specs=[row_spec(D_MODEL),
                  pl.BlockSpec((D_MODEL, n), lambda i: (0, 0)),
                  pl.BlockSpec((1, LANES), lambda i: (0, 0)),
                  pl.BlockSpec((FOX_WIDTH, LANES), lambda i: (0, 0))],
        out_specs=[t_spec(FOX_WIDTH), row_spec(FOX_WIDTH), t_spec(FOX_HEADS * VT_ROWS),
                   t_spec(MEM_WIDTH), row_spec(LANES), t_spec(16),
                   stat_spec, stat_spec, stat_spec],
        out_shape=[jax.ShapeDtypeStruct((nb, FOX_WIDTH, seq), BF16),
                   jax.ShapeDtypeStruct((t_tokens, FOX_WIDTH), BF16),
                   jax.ShapeDtypeStruct((nb, FOX_HEADS * VT_ROWS, seq), BF16),
                   jax.ShapeDtypeStruct((nb, MEM_WIDTH, seq), BF16),
                   jax.ShapeDtypeStruct((t_tokens, LANES), F32),
                   jax.ShapeDtypeStruct((nb, 16, seq), F32),
                   stat_shape, stat_shape, stat_shape],
        scratch_shapes=[pltpu.VMEM((1, LANES), F32), pltpu.VMEM((1, LANES), F32)],
        compiler_params=_cparams(1),
        name="fox_proj",
    )(x2d, w, bf, seg)


def _rope_slab_t(slab_t, cos_t, sin_t):
    half = QK_ROPE // 2
    x1 = slab_t[QK_NOPE:QK_NOPE + half]
    x2 = slab_t[QK_NOPE + half:QK_NOPE + QK_ROPE]
    return jnp.concatenate([slab_t[:QK_NOPE], x1 * cos_t - x2 * sin_t, x1 * sin_t + x2 * cos_t,
                            slab_t[QK_NOPE + QK_ROPE:]], axis=0)


def _mla_proj_kernel(x_ref, pos_ref, invf_ref, w_ref, gq_ref, wuq_ref, gkv_ref, wk_ref, wv_ref,
                     qt_ref, mqt_ref, k_ref, vt_ref):
    proj = jnp.dot(x_ref[...].astype(BF16), w_ref[...], preferred_element_type=F32)
    ang_t = invf_ref[...] * pos_ref[...].astype(F32)
    cos_t = jnp.cos(ang_t)
    sin_t = jnp.sin(ang_t)
    c_q = _rms_norm(proj[:, :Q_LORA], gq_ref[...])
    _store_transposed(mqt_ref, proj[:, Q_LORA:Q_LORA + MEM_WIDTH] * (HEAD_DIM ** -0.5 * LOG2E))
    kv_off = Q_LORA + MEM_WIDTH
    c_kv = _rms_norm(proj[:, kv_off:kv_off + KV_LORA], gkv_ref[...])
    kr = _rope_slab_t(proj[:, kv_off + KV_LORA:].T, cos_t, sin_t).T
    q = jnp.dot(c_q.astype(BF16), wuq_ref[...], preferred_element_type=F32)
    kn = jnp.dot(c_kv.astype(BF16), wk_ref[...], preferred_element_type=F32)
    qscale = (QK_NOPE + QK_ROPE) ** -0.5 * LOG2E
    for h in range(MLA_HEADS):
        sl = slice(h * MLA_SLAB, (h + 1) * MLA_SLAB)
        qt_ref[0, sl, :] = (_rope_slab_t(q[:, sl].T, cos_t, sin_t) * qscale).astype(BF16)
        k_ref[:, sl] = (kn[:, sl] + kr).astype(BF16)
    _store_values_transposed(vt_ref, jnp.dot(c_kv.astype(BF16), wv_ref[...],
                                             preferred_element_type=F32))


def _mla_proj(x2d, pos2d, invf, w, gq, wuq, gkv, wk, wv, seq):
    t_tokens = x2d.shape[0]
    tm = PROJ_TM
    nb = t_tokens // seq
    tiles_per_batch = seq // tm
    row_spec = lambda width: pl.BlockSpec((tm, width), lambda i: (i, 0))
    t_spec = lambda rows: pl.BlockSpec(
        (1, rows, tm), lambda i: (i // tiles_per_batch, 0, i % tiles_per_batch))
    full = lambda a: pl.BlockSpec(a.shape, lambda i: (0, 0))
    slabs = MLA_HEADS * MLA_SLAB
    return pl.pallas_call(
        _mla_proj_kernel,
        grid=(t_tokens // tm,),
        in_specs=[row_spec(D_MODEL), pl.BlockSpec((1, tm), lambda i: (0, i)), full(invf), full(w),
                  full(gq), full(wuq), full(gkv), full(wk), full(wv)],
        out_specs=[t_spec(slabs), t_spec(MEM_WIDTH), row_spec(slabs),
                   t_spec(MLA_HEADS * VT_ROWS)],
        out_shape=[jax.ShapeDtypeStruct((nb, slabs, seq), BF16),
                   jax.ShapeDtypeStruct((nb, MEM_WIDTH, seq), BF16),
                   jax.ShapeDtypeStruct((t_tokens, slabs), BF16),
                   jax.ShapeDtypeStruct((nb, MLA_HEADS * VT_ROWS, seq), BF16)],
        compiler_params=_cparams(1),
        name="mla_proj",
    )(x2d, pos2d, invf, w, gq, wuq, gkv, wk, wv)


SKIP_LOG2 = -160.0
NORM_SLACK = 1.02


def _causal_attn_kernel(*refs, tq, n_tiles, fox, slab, long_body):
    if fox:
        (cf_ref, cl_ref, kn_ref, qt_ref, k_ref, vt_ref, cq_ref, ck_ref,
         o_ref, m_ref, acc_ref, s_ref, cmax_ref) = refs
    else:
        qt_ref, k_ref, vt_ref, o_ref, m_ref, acc_ref, s_ref, cmax_ref = refs
    b = pl.program_id(0)
    pair = pl.program_id(1)
    i = pl.program_id(2)
    m_ref[...] = jnp.full(m_ref.shape, NEG_INF, F32)
    acc_ref[...] = jnp.zeros(acc_ref.shape, F32)
    qt = qt_ref[0]
    if slab:
        qth = [qt[:MLA_SLAB], qt[MLA_SLAB:]]
    else:
        rowi = lax.broadcasted_iota(jnp.int32, qt.shape, 0)
        zero = jnp.zeros_like(qt)
        qth = [jnp.where(rowi < HEAD_DIM, qt, zero), jnp.where(rowi >= HEAD_DIM, qt, zero)]
    if fox:
        lane = lax.broadcasted_iota(jnp.int32, (1, LANES), 1)
        cq = [cq_ref[0, pl.ds(2 * pair + h, 1), :] for h in range(2)]

    def logits_to(slot, j):
        off = pl.multiple_of(j * tq, tq)
        kc = k_ref[0, pl.ds(off, tq), :]
        for h in range(2):
            kh = kc[:, h * MLA_SLAB:(h + 1) * MLA_SLAB] if slab else kc
            s = jnp.dot(kh, qth[h], preferred_element_type=F32)
            if fox:
                ck_blk = ck_ref[0, pl.ds(off, tq), :]
                ck = jnp.sum(jnp.where(lane == 2 * pair + h, ck_blk, 0.0), axis=1, keepdims=True)
                s = s + cq[h] - ck
            s_ref[slot, h] = s
            cmax_ref[slot, h] = jnp.max(s, axis=0, keepdims=True)

    def softmax_pv(slot, j, causal_mask):
        off = pl.multiple_of(j * tq, tq)
        for h in range(2):
            s = s_ref[slot, h]
            if causal_mask is not None:
                s = jnp.where(causal_mask, s, NEG_INF)
                chunk_max = jnp.max(s, axis=0, keepdims=True)
            else:
                chunk_max = cmax_ref[slot, h]
            m_prev = m_ref[h]
            m_new = jnp.maximum(m_prev, chunk_max)
            alpha = jnp.exp2(m_prev - m_new)
            p = jnp.exp2(s - m_new).astype(BF16)
            vth = vt_ref[0, h * VT_ROWS:(h + 1) * VT_ROWS, pl.ds(off, tq)]
            acc_ref[h] = acc_ref[h] * alpha + jnp.dot(vth, p, preferred_element_type=F32)
            m_ref[h] = m_new

    if fox:
        base = b * n_tiles
        qn = []
        for h in range(2):
            qf = qth[h].astype(F32)
            qn.append(jnp.sqrt(jnp.max(jnp.sum(qf * qf, axis=0, keepdims=True))) * NORM_SLACK)

        top = [2.0 * qn[h] * kn_ref[base + i, 2 * pair + h] + cf_ref[base + i, 2 * pair + h]
               for h in range(2)]

        def live(j):
            ub = [top[h] - cl_ref[base + jnp.maximum(j, 0), 2 * pair + h] for h in range(2)]
            return (j >= 0) & (jnp.maximum(ub[0], ub[1]) > SKIP_LOG2)

        j0 = lax.while_loop(live, lambda j: j - 1, i - 1) + 1
    else:
        j0 = 0
    n_before = i - j0
    odd = (n_before % 2) == 1

    @pl.when(odd)
    def _():
        logits_to(1, j0)
        logits_to(0, j0 + 1)
        softmax_pv(1, j0, None)

    @pl.when(jnp.logical_not(odd))
    def _():
        logits_to(0, j0)

    def two_steps(j):
        logits_to(1, j + 1)
        softmax_pv(0, j, None)
        logits_to(0, j + 2)
        softmax_pv(1, j + 1, None)

    j_even = j0 + (n_before % 2)
    n_two = n_before // 2
    if long_body:
        @pl.when((n_two % 2) == 1)
        def _():
            two_steps(j_even)

        j_quad = j_even + 2 * (n_two % 2)

        def body(t, carry):
            two_steps(j_quad + 4 * t)
            two_steps(j_quad + 4 * t + 2)
            return carry

        lax.fori_loop(0, n_two // 2, body, 0)
    else:
        def body(t, carry):
            two_steps(j_even + 2 * t)
            return carry

        lax.fori_loop(0, n_two, body, 0)
    key = lax.broadcasted_iota(jnp.int32, (tq, tq), 0)
    qry = lax.broadcasted_iota(jnp.int32, (tq, tq), 1)
    softmax_pv(0, i, key <= qry)
    out_t = jnp.concatenate(
        [acc_ref[h, :HEAD_DIM] * (1.0 / acc_ref[h, HEAD_DIM:HEAD_DIM + 1]) for h in range(2)],
        axis=0)
    o_ref[0] = out_t.T.astype(o_ref.dtype)


def _causal_attention(qt, k, vt, fox_args, *, slab):
    nb, seq, _ = k.shape
    fox = fox_args is not None
    tq = FOX_TQ if fox else ATT_TQ
    n_tiles = seq // tq
    n_pairs = FOX_HEADS // 2
    rows = 2 * MLA_SLAB if slab else LANES
    in_specs = [pl.BlockSpec((1, rows, tq), lambda b, p, i: (b, p, i)),
                pl.BlockSpec((1, seq, rows), lambda b, p, i: (b, 0, p)),
                pl.BlockSpec((1, 2 * VT_ROWS, seq), lambda b, p, i: (b, p, 0))]
    args = [qt, k, vt]
    if fox:
        cum, cumt, cfirst, clast, knorm = fox_args
        smem = pl.BlockSpec(memory_space=pltpu.SMEM)
        in_specs = [smem, smem, smem] + in_specs + [
            pl.BlockSpec((1, 16, tq), lambda b, p, i: (b, 0, i)),
            pl.BlockSpec((1, seq, LANES), lambda b, p, i: (b, 0, 0))]
        args = [cfirst, clast, knorm] + args + [cumt, cum]
    return pl.pallas_call(
        functools.partial(_causal_attn_kernel, tq=tq, n_tiles=n_tiles, fox=fox, slab=slab,
                          long_body=not fox),
        grid=(nb, n_pairs, n_tiles),
        in_specs=in_specs,
        out_specs=pl.BlockSpec((1, tq, LANES), lambda b, p, i: (b, i, p)),
        out_shape=jax.ShapeDtypeStruct((nb, seq, n_pairs * LANES), BF16),
        scratch_shapes=[pltpu.VMEM((2, 1, tq), F32), pltpu.VMEM((2, VT_ROWS, tq), F32),
                        pltpu.VMEM((2, 2, tq, tq), F32), pltpu.VMEM((2, 2, 1, tq), F32)],
        compiler_params=_cparams(3),
        name="fox_attention" if fox else "mla_attention",
    )(*args)


def _mem_proj_kernel(mem_ref, w_ref, mk_ref, mvt_ref):
    mkv = jnp.dot(mem_ref[...].astype(BF16), w_ref[...], preferred_element_type=F32)
    mk_ref[0] = mkv[:, :MEM_WIDTH].astype(BF16)
    _store_values_transposed(mvt_ref, mkv[:, MEM_WIDTH:])


def _mem_proj(mem2d, w, nb):
    return pl.pallas_call(
        _mem_proj_kernel,
        grid=(nb,),
        in_specs=[pl.BlockSpec((N_MEM, D_MODEL), lambda b: (b, 0)),
                  pl.BlockSpec(w.shape, lambda b: (0, 0))],
        out_specs=[pl.BlockSpec((1, N_MEM, MEM_WIDTH), lambda b: (b, 0, 0)),
                   pl.BlockSpec((1, MEM_HEADS * VT_ROWS, N_MEM), lambda b: (b, 0, 0))],
        out_shape=[jax.ShapeDtypeStruct((nb, N_MEM, MEM_WIDTH), BF16),
                   jax.ShapeDtypeStruct((nb, MEM_HEADS * VT_ROWS, N_MEM), BF16)],
        compiler_params=_cparams(1),
        name="mem_proj",
    )(mem2d, w)


def _mem_attn_kernel(qt_ref, k_ref, vt_ref, o_ref):
    qt = qt_ref[0]
    kc = k_ref[0]
    for pair in range(MEM_HEADS // 2):
        qt_p = qt[pair * LANES:(pair + 1) * LANES]
        k_p = kc[:, pair * LANES:(pair + 1) * LANES]
        rowi = lax.broadcasted_iota(jnp.int32, qt_p.shape, 0)
        zero = jnp.zeros_like(qt_p)
        outs = []
        for h in range(2):
            mine = (rowi < HEAD_DIM) if h == 0 else (rowi >= HEAD_DIM)
            s = jnp.dot(k_p, jnp.where(mine, qt_p, zero), preferred_element_type=F32)
            p = jnp.exp2(s - jnp.max(s, axis=0, keepdims=True)).astype(BF16)
            r0 = (2 * pair + h) * VT_ROWS
            acc = jnp.dot(vt_ref[0, r0:r0 + VT_ROWS, :], p, preferred_element_type=F32)
            outs.append(acc[:HEAD_DIM] * (1.0 / acc[HEAD_DIM:HEAD_DIM + 1]))
        o_ref[0, :, pair * LANES:(pair + 1) * LANES] = (
            jnp.concatenate(outs, axis=0).T.astype(o_ref.dtype))


def _memory_attention(mqt, mk, mvt):
    nb, _, seq = mqt.shape
    tq = ATT_TQ
    return pl.pallas_call(
        _mem_attn_kernel,
        grid=(nb, seq // tq),
        in_specs=[pl.BlockSpec((1, MEM_WIDTH, tq), lambda b, i: (b, 0, i)),
                  pl.BlockSpec((1, N_MEM, MEM_WIDTH), lambda b, i: (b, 0, 0)),
                  pl.BlockSpec((1, MEM_HEADS * VT_ROWS, N_MEM), lambda b, i: (b, 0, 0))],
        out_specs=pl.BlockSpec((1, tq, MEM_WIDTH), lambda b, i: (b, i, 0)),
        out_shape=jax.ShapeDtypeStruct((nb, seq, MEM_WIDTH), BF16),
        compiler_params=_cparams(2),
        name="memory_attention",
    )(mqt, mk, mvt)


def _post_attn_kernel(x_ref, att_ref, memo_ref, wa_ref, wm_ref, g_ref, b_ref, wr_ref, br_ref,
                      x1_ref, x1p_ref, idx_ref, rank_ref, gate_ref, cnt_ref, carry_ref):
    t = pl.program_id(0)

    @pl.when(t == 0)
    def _():
        carry_ref[...] = jnp.zeros_like(carry_ref)

    tm = x_ref.shape[0]
    mix = (jnp.dot(att_ref[...], wa_ref[...], preferred_element_type=F32)
           + jnp.dot(memo_ref[...], wm_ref[...], preferred_element_type=F32))
    x1 = _layer_norm(DEEPNORM_ALPHA * x_ref[...] + mix, g_ref[...], b_ref[...])
    x1_ref[...] = x1
    x1b = x1.astype(BF16)
    x1p_ref[...] = _pack_bf16_pairs(x1b)
    logits = jnp.dot(x1b, wr_ref[...], preferred_element_type=F32) + br_ref[...]
    lane = lax.broadcasted_iota(jnp.int32, (tm, LANES), 1)
    work = jnp.where(lane < N_EXPERTS, logits, -jnp.inf)
    idxs, vals = [], []
    onehot = jnp.zeros((tm, LANES), F32)
    for _ in range(TOP_K):
        best = jnp.max(work, axis=1, keepdims=True)
        where_best = jnp.argmax(work, axis=1, keepdims=True).astype(jnp.int32)
        hit = lane == where_best
        onehot = jnp.where(hit, 1.0, onehot)
        work = jnp.where(hit, -jnp.inf, work)
        idxs.append(where_best)
        vals.append(best)
    exps = [jnp.exp(v - vals[0]) for v in vals]
    denom = exps[0] + exps[1] + exps[2] + exps[3]
    row = lax.broadcasted_iota(jnp.int32, (tm, tm), 0)
    col = lax.broadcasted_iota(jnp.int32, (tm, tm), 1)
    strict = jnp.where(row > col, 1.0, 0.0).astype(BF16)
    before = jnp.dot(strict, onehot.astype(BF16), preferred_element_type=F32) + carry_ref[...]
    idx_out = jnp.zeros((tm, LANES), F32)
    rank_out = jnp.zeros((tm, LANES), F32)
    gate_out = jnp.zeros((tm, LANES), F32)
    for r in range(TOP_K):
        rank_r = jnp.sum(jnp.where(lane == idxs[r], before, 0.0), axis=1, keepdims=True)
        idx_out = jnp.where(lane == r, idxs[r].astype(F32), idx_out)
        rank_out = jnp.where(lane == r, rank_r, rank_out)
        gate_out = jnp.where(lane == r, exps[r] / denom, gate_out)
    idx_ref[...] = idx_out.T[:8].astype(jnp.int32)
    rank_ref[...] = rank_out.T[:8].astype(jnp.int32)
    gate_ref[...] = gate_out
    total = carry_ref[...] + jnp.sum(onehot, axis=0, keepdims=True)
    carry_ref[...] = total
    cnt_ref[...] = jnp.broadcast_to(total, cnt_ref.shape)


def _post_attn(x2d, att, memo, wa, wm, g, b, wr, br):
    t_tokens = x2d.shape[0]
    tm = PROJ_TM
    row_spec = lambda width: pl.BlockSpec((tm, width), lambda i: (i, 0))
    full = lambda a: pl.BlockSpec(a.shape, lambda i: (0, 0))
    return pl.pallas_call(
        _post_attn_kernel,
        grid=(t_tokens // tm,),
        in_specs=[row_spec(D_MODEL), row_spec(att.shape[1]), row_spec(MEM_WIDTH),
                  full(wa), full(wm), full(g), full(b), full(wr), full(br)],
        out_specs=[row_spec(D_MODEL), row_spec(D_MODEL // 2),
                   pl.BlockSpec((8, tm), lambda i: (0, i)), pl.BlockSpec((8, tm), lambda i: (0, i)),
                   row_spec(LANES), pl.BlockSpec((8, LANES), lambda i: (0, 0))],
        out_shape=[jax.ShapeDtypeStruct((t_tokens, D_MODEL), F32),
                   jax.ShapeDtypeStruct((t_tokens, D_MODEL // 2), jnp.uint32),
                   jax.ShapeDtypeStruct((8, t_tokens), jnp.int32),
                   jax.ShapeDtypeStruct((8, t_tokens), jnp.int32),
                   jax.ShapeDtypeStruct((t_tokens, LANES), F32),
                   jax.ShapeDtypeStruct((8, LANES), F32)],
        scratch_shapes=[pltpu.VMEM((1, LANES), F32)],
        compiler_params=_cparams(1),
        name="outproj_ln_router",
    )(x2d, att, memo, wa, wm, g, b, wr, br)


def _expert_kernel(be_ref, nu_ref, br_ref, x_ref, wgu_hbm, bgu_ref, wdn_hbm, bdn_ref, *rest,
                   n_chunks, blk_lo, n_blocks, layer, has_prev):
    o_ref, wgu_st, wdn_st, wgu_bf, wdn_bf, h_ref, sem = rest[1:] if has_prev else rest
    step = pl.program_id(0)
    blk = step + blk_lo
    live_end = jnp.minimum(blk_lo + pl.num_programs(0), nu_ref[0])
    expert = be_ref[blk]

    def fetch(e):
        return (pltpu.make_async_copy(wgu_hbm.at[layer, e], wgu_st, sem.at[0]),
                pltpu.make_async_copy(wdn_hbm.at[layer, e], wdn_st, sem.at[1]))

    @pl.when(((step == 0) | (expert != be_ref[jnp.maximum(blk - 1, 0)])) & (blk < live_end))
    def _():
        @pl.when(step == 0)
        def _():
            for copy in fetch(expert):
                copy.start()

        for copy in fetch(expert):
            copy.wait()
        wgu_bf[...] = wgu_st[...].astype(BF16)
        wdn_bf[...] = wdn_st[...].astype(BF16)
        nxt = lax.while_loop(
            lambda j: (j < live_end) & (be_ref[jnp.minimum(j, n_blocks - 1)] == expert),
            lambda j: j + 1, blk + 1)

        @pl.when(nxt < live_end)
        def _():
            for copy in fetch(be_ref[jnp.minimum(nxt, n_blocks - 1)]):
                copy.start()

    def mlp(rows):
        x = _unpack_bf16_pairs(x_ref[:rows]).astype(BF16)
        cw = D_EXPERT // n_chunks
        for c in range(n_chunks):
            gs = slice(c * cw, (c + 1) * cw)
            us = slice(D_EXPERT + c * cw, D_EXPERT + (c + 1) * cw)
            g = jnp.dot(x, wgu_bf[:, gs], preferred_element_type=F32) + bgu_ref[0, 0, :, gs]
            u = jnp.dot(x, wgu_bf[:, us], preferred_element_type=F32) + bgu_ref[0, 0, :, us]
            g = jnp.minimum(g, SWIGLU_LIMIT)
            u = jnp.clip(u, -SWIGLU_LIMIT, SWIGLU_LIMIT)
            h_ref[:rows, gs] = ((u + 1.0) * (g * jax.nn.sigmoid(SWIGLU_ALPHA * g))).astype(BF16)
        y = jnp.dot(h_ref[:rows], wdn_bf[...], preferred_element_type=F32) + bdn_ref[0, 0]
        return _pack_bf16_pairs(y.astype(BF16))

    tm = o_ref.shape[0]
    live_rows = br_ref[blk]

    @pl.when(live_rows > tm // 2)
    def _():
        o_ref[...] = mlp(tm)

    @pl.when((live_rows > 0) & (live_rows <= tm // 2))
    def _():
        o_ref[:tm // 2] = mlp(tm // 2)
        o_ref[tm // 2:] = jnp.zeros((tm - tm // 2, o_ref.shape[1]), o_ref.dtype)

    @pl.when(live_rows == 0)
    def _():
        o_ref[...] = jnp.zeros_like(o_ref)


def _experts(block_expert, n_used, block_rows, xb, blk_lo, p_rows, y_prev, layer, wgu, bgu, wdn, bdn):
    tm = MOE_TM
    has_prev = y_prev is not None
    b_map = lambda i, be, nu, br: (layer, be[i + blk_lo], 0, 0)
    in_specs = [pl.BlockSpec((tm, D_MODEL // 2), lambda i, be, nu, br: (i, 0)),
                pl.BlockSpec(memory_space=pl.ANY),
                pl.BlockSpec((1, 1, 1, 2 * D_EXPERT), b_map),
                pl.BlockSpec(memory_space=pl.ANY),
                pl.BlockSpec((1, 1, 1, D_MODEL), b_map)]
    args = [block_expert, n_used, block_rows, xb, wgu, bgu, wdn, bdn]
    if has_prev:
        in_specs.append(pl.BlockSpec(memory_space=pl.ANY))
        args.append(y_prev)
    grid_spec = pltpu.PrefetchScalarGridSpec(
        num_scalar_prefetch=3,
        grid=(xb.shape[0] // tm,),
        in_specs=in_specs,
        out_specs=pl.BlockSpec((tm, D_MODEL // 2), lambda i, be, nu, br: (i + blk_lo, 0)),
        scratch_shapes=[pltpu.VMEM((D_MODEL, 2 * D_EXPERT), F32),
                        pltpu.VMEM((D_EXPERT, D_MODEL), F32),
                        pltpu.VMEM((D_MODEL, 2 * D_EXPERT), BF16),
                        pltpu.VMEM((D_EXPERT, D_MODEL), BF16),
                        pltpu.VMEM((tm, D_EXPERT), BF16),
                        pltpu.SemaphoreType.DMA((2,))],
    )
    return pl.pallas_call(
        functools.partial(_expert_kernel, n_chunks=4, blk_lo=blk_lo, n_blocks=p_rows // tm,
                          layer=layer, has_prev=has_prev),
        grid_spec=grid_spec,
        out_shape=jax.ShapeDtypeStruct((p_rows, D_MODEL // 2), jnp.uint32),
        input_output_aliases={len(args) - 1: 0} if has_prev else {},
        compiler_params=pltpu.CompilerParams(dimension_semantics=("arbitrary",),
                                             vmem_limit_bytes=EXPERT_VMEM_LIMIT),
        name="experts",
    )(*args)


def _sc_move_rows(table_hbm, idx_v, out_hbm, base, n_chunks, rows_v, gsem, wsem):
    def gather(c, slot):
        off = pl.multiple_of(c * SC_CHUNK, SC_CHUNK)
        return pltpu.make_async_copy(table_hbm.at[idx_v.at[pl.ds(off, SC_CHUNK)]],
                                     rows_v.at[slot], gsem.at[slot])

    def put(c, slot):
        off = pl.multiple_of(c * SC_CHUNK, SC_CHUNK)
        return pltpu.make_async_copy(rows_v.at[slot], out_hbm.at[pl.ds(base + off, SC_CHUNK)],
                                     wsem.at[slot])

    gather(0, 0).start()
    gather(1, 1).start()

    @pl.loop(0, n_chunks, step=2)
    def _(c):
        for slot in range(2):
            gather(c + slot, slot).wait()
            put(c + slot, slot).start()
        for slot in range(2):
            put(c + slot, slot).wait()

            @pl.when(c + 2 + slot < n_chunks)
            def _():
                gather(c + 2 + slot, slot).start()


def _sc_gather_rows(table, idx):
    n_idx = idx.shape[0]
    width = table.shape[1]
    per_worker = n_idx // SC_WORKERS
    n_chunks = per_worker // SC_CHUNK
    mesh = plsc.VectorSubcoreMesh(core_axis_name="c", subcore_axis_name="s",
                                  num_cores=SC_CORES, num_subcores=SC_SUBCORES)

    def body(table_hbm, idx_hbm, out_hbm, idx_v, rows_v, gsem, wsem):
        wid = lax.axis_index("s") * SC_CORES + lax.axis_index("c")
        base = wid * per_worker
        pltpu.sync_copy(idx_hbm.at[pl.ds(base, per_worker)], idx_v)
        _sc_move_rows(table_hbm, idx_v, out_hbm, base, n_chunks, rows_v, gsem, wsem)

    return pl.kernel(
        body,
        out_type=jax.ShapeDtypeStruct((n_idx, width), table.dtype),
        mesh=mesh,
        scratch_types=[pltpu.VMEM((per_worker,), jnp.int32),
                       pltpu.VMEM((2, SC_CHUNK, width), table.dtype),
                       pltpu.SemaphoreType.DMA((2,)),
                       pltpu.SemaphoreType.DMA((2,))],
        name="sc_gather_rows",
    )(table, idx)


def _sc_dispatch_rows(table, dest, slot_lo, n_slots):
    n_tok, width = table.shape
    n_pairs = dest.shape[0]
    per_worker = n_slots // SC_WORKERS
    n_chunks = per_worker // SC_CHUNK
    n_scan = n_pairs // SC_SCAN
    mesh = plsc.VectorSubcoreMesh(core_axis_name="c", subcore_axis_name="s",
                                  num_cores=SC_CORES, num_subcores=SC_SUBCORES)

    def body(table_hbm, dest_hbm, out_hbm, tok_v, dest_v, rows_v, gsem, wsem):
        wid = lax.axis_index("s") * SC_CORES + lax.axis_index("c")
        base = wid * per_worker
        first = slot_lo + base
        lane = lax.iota(jnp.int32, SC_LANES)

        @pl.loop(0, per_worker // SC_LANES)
        def _(i):
            off = pl.multiple_of(i * SC_LANES, SC_LANES)
            tok_v[pl.ds(off, SC_LANES)] = lax.rem(first + off + lane, n_tok)

        @pl.loop(0, n_scan)
        def _(g):
            goff = pl.multiple_of(g * SC_SCAN, SC_SCAN)
            pltpu.sync_copy(dest_hbm.at[pl.ds(goff, SC_SCAN)], dest_v)

            @pl.loop(0, SC_SCAN // SC_LANES)
            def _(i):
                off = pl.multiple_of(i * SC_LANES, SC_LANES)
                local = dest_v[pl.ds(off, SC_LANES)] - first
                mine = (local >= 0) & (local < per_worker)
                pair = goff + off + lane
                plsc.store_scatter(tok_v, [jnp.where(mine, local, 0)], lax.rem(pair, n_tok), mask=mine)

        _sc_move_rows(table_hbm, tok_v, out_hbm, base, n_chunks, rows_v, gsem, wsem)

    return pl.kernel(
        body,
        out_type=jax.ShapeDtypeStruct((n_slots, width), table.dtype),
        mesh=mesh,
        scratch_types=[pltpu.VMEM((per_worker,), jnp.int32),
                       pltpu.VMEM((SC_SCAN,), jnp.int32),
                       pltpu.VMEM((2, SC_CHUNK, width), table.dtype),
                       pltpu.SemaphoreType.DMA((2,)),
                       pltpu.SemaphoreType.DMA((2,))],
        compiler_params=pltpu.CompilerParams(needs_layout_passes=False),
        name="sc_dispatch_rows",
    )(table, dest)


def _combine_kernel(x1_ref, yg_ref, gate_ref, g_ref, b_ref, *rest):
    o_ref = rest[-1]
    gates = gate_ref[...]
    ffn = _unpack_bf16_pairs(yg_ref[0]) * gates[:, 0:1]
    for r in range(1, TOP_K):
        ffn = ffn + _unpack_bf16_pairs(yg_ref[r]) * gates[:, r:r + 1]
    o_ref[...] = _layer_norm(DEEPNORM_ALPHA * x1_ref[...] + ffn, g_ref[...], b_ref[...])


def _combine(x1, yg, gates, g, b, tok_lo, out_prev):
    t_tokens = x1.shape[0]
    tm = COMBINE_TM
    tile_lo = tok_lo // tm
    in_specs = [pl.BlockSpec((tm, D_MODEL), lambda i: (i + tile_lo, 0)),
                pl.BlockSpec((TOP_K, tm, D_MODEL // 2), lambda i: (0, i, 0)),
                pl.BlockSpec((tm, LANES), lambda i: (i + tile_lo, 0)),
                pl.BlockSpec((1, D_MODEL), lambda i: (0, 0)),
                pl.BlockSpec((1, D_MODEL), lambda i: (0, 0))]
    args = [x1, yg, gates, g, b]
    if out_prev is not None:
        in_specs.append(pl.BlockSpec(memory_space=pl.ANY))
        args.append(out_prev)
    return pl.pallas_call(
        _combine_kernel,
        grid=(yg.shape[1] // tm,),
        in_specs=in_specs,
        out_specs=pl.BlockSpec((tm, D_MODEL), lambda i: (i + tile_lo, 0)),
        out_shape=jax.ShapeDtypeStruct((t_tokens, D_MODEL), F32),
        input_output_aliases={len(args) - 1: 0} if out_prev is not None else {},
        compiler_params=_cparams(1),
        name="combine_ln",
    )(*args)


def _moe(x1, x1p, idx, rank, gates, counts, layer, wgu, bgu, wdn, bdn, g, b):
    t_tokens = x1.shape[0]
    tm = MOE_TM
    counts = counts[0, :N_EXPERTS].astype(jnp.int32)
    padded = ((counts + tm - 1) // tm) * tm
    pend = jnp.cumsum(padded)
    pstart = pend - padded
    dest = rank[:TOP_K]
    for e in range(N_EXPERTS):
        dest = dest + jnp.where(idx[:TOP_K] == e, pstart[e], 0)
    n_blocks = (t_tokens * TOP_K) // tm + N_EXPERTS
    p_rows = n_blocks * tm
    block_start = jnp.arange(n_blocks, dtype=jnp.int32) * tm
    block_expert = jnp.minimum(
        jnp.sum((pend[None, :] <= block_start[:, None]).astype(jnp.int32), axis=1), N_EXPERTS - 1)
    n_used = (pend[-1:] // tm).astype(jnp.int32)
    mine = block_expert[:, None] == jnp.arange(N_EXPERTS, dtype=jnp.int32)[None, :]
    live_end = jnp.sum(jnp.where(mine, (pstart + counts)[None, :], 0), axis=1)
    block_rows = jnp.clip(live_end - block_start, 0, tm)
    y = None
    for lo, hi in ((0, n_blocks // MOE_SPLIT), (n_blocks // MOE_SPLIT, n_blocks)):
        xb = _sc_dispatch_rows(x1p, dest.reshape(-1), lo * tm, (hi - lo) * tm)
        y = _experts(block_expert, n_used, block_rows, xb, lo, p_rows, y, layer, wgu, bgu, wdn, bdn)
    out = None
    for lo, hi in ((0, t_tokens // MOE_SPLIT), (t_tokens // MOE_SPLIT, t_tokens)):
        yg = _sc_gather_rows(y, dest[:, lo:hi].reshape(-1)).reshape(TOP_K, hi - lo, D_MODEL // 2)
        out = _combine(x1, yg, gates, g, b, lo, out)
    return out


def _row(v, width=None):
    v = v.astype(F32).reshape(1, -1)
    if width is not None and v.shape[1] < width:
        v = jnp.pad(v, ((0, 0), (0, width - v.shape[1])))
    return v


def _pad_cols(w, width):
    return jnp.pad(w, ((0, 0), (0, width - w.shape[1])))


def kernel(x, mem, positions, a_w_in, a_b_f, a_w_out, b_w_in, b_g_q, b_w_uq, b_w_out,
           kv_w_dkv, kv_g, kv_w_ukv, mem_w_kv, ln_g, ln_b,
           moe_w_r, moe_b_r, moe_w_gu, moe_b_gu, moe_w_dn, moe_b_dn):
    nb, seq, d = x.shape
    t_tokens = nb * seq
    n_a = a_w_in.shape[0]
    x2d = x.reshape(t_tokens, d)
    mem2d = mem.reshape(nb * N_MEM, d)
    pos2d = positions.reshape(1, t_tokens)
    half = QK_ROPE // 2
    invf = (ROPE_THETA ** (-jnp.arange(half, dtype=F32) * 2.0 / QK_ROPE)).reshape(half, 1)

    shared_kv = None
    for l in range(DEPTH):
        mk, mvt = _mem_proj(mem2d, mem_w_kv[l].astype(BF16), nb)
        if l < n_a:
            w_in = a_w_in[l]
            w = jnp.concatenate([w_in[:, :3 * FOX_WIDTH],
                                 _pad_cols(w_in[:, 3 * FOX_WIDTH:3 * FOX_WIDTH + FOX_HEADS], LANES),
                                 w_in[:, 3 * FOX_WIDTH + FOX_HEADS:]], axis=1).astype(BF16)
            qt, k, vt, mqt, cum, cumt, cfirst, clast, knorm = _fox_proj(
                x2d, w, _row(a_b_f[l], LANES), seq)
            stats = [s.reshape(-1, LANES) for s in (cfirst, clast, knorm)]
            att = _causal_attention(qt, k.reshape(nb, seq, -1), vt,
                                    [cum.reshape(nb, seq, LANES), cumt] + stats, slab=False)
            w_out = a_w_out[l]
        else:
            bl = l - n_a
            w = jnp.concatenate([b_w_in[bl], kv_w_dkv[:, :KV_LORA],
                                 jnp.zeros((d, QK_NOPE), F32), kv_w_dkv[:, KV_LORA:],
                                 jnp.zeros((d, LANES - QK_NOPE - QK_ROPE), F32)], axis=1).astype(BF16)
            wuq = b_w_uq[bl].reshape(Q_LORA, MLA_HEADS, QK_NOPE + QK_ROPE)
            wuq = jnp.pad(wuq, ((0, 0), (0, 0), (0, MLA_SLAB - QK_NOPE - QK_ROPE)))
            wuq = wuq.reshape(Q_LORA, MLA_HEADS * MLA_SLAB).astype(BF16)
            wukv = kv_w_ukv.reshape(KV_LORA, MLA_HEADS, QK_NOPE + V_DIM)
            wk = jnp.pad(wukv[:, :, :QK_NOPE], ((0, 0), (0, 0), (0, MLA_SLAB - QK_NOPE)))
            wk = wk.reshape(KV_LORA, MLA_HEADS * MLA_SLAB).astype(BF16)
            wv = wukv[:, :, QK_NOPE:].reshape(KV_LORA, MLA_V_WIDTH).astype(BF16)
            qt, mqt, k_new, vt_new = _mla_proj(x2d, pos2d, invf, w, _row(b_g_q[bl]), wuq,
                                              _row(kv_g), wk, wv, seq)
            if shared_kv is None:
                shared_kv = (k_new.reshape(nb, seq, -1), vt_new)
            att = _causal_attention(qt, shared_kv[0], shared_kv[1], None, slab=True)
            w_out = b_w_out[bl]
        memo = _memory_attention(mqt, mk, mvt)
        n_att = w_out.shape[0] - MEM_WIDTH
        x1, x1p, idx, rank, gates, counts = _post_attn(
            x2d, att.reshape(t_tokens, -1), memo.reshape(t_tokens, MEM_WIDTH),
            w_out[:n_att].astype(BF16), w_out[n_att:].astype(BF16),
            _row(ln_g[l, 0]), _row(ln_b[l, 0]),
            _pad_cols(moe_w_r[l], LANES).astype(BF16), _row(moe_b_r[l], LANES))
        x2d = _moe(x1, x1p, idx, rank, gates, counts, l,
                   moe_w_gu, moe_b_gu.reshape(DEPTH, N_EXPERTS, 1, -1),
                   moe_w_dn, moe_b_dn.reshape(DEPTH, N_EXPERTS, 1, -1),
                   _row(ln_g[l, 1]), _row(ln_b[l, 1]))
    return x2d.reshape(nb, seq, d)
```

```python
import functools
import math

import jax
import jax.numpy as jnp
from jax import lax
from jax.experimental import pallas as pl
from jax.experimental.pallas import tpu as pltpu
from jax.experimental.pallas import tpu_sc as plsc

F32 = jnp.float32
BF16 = jnp.bfloat16

D_MODEL = 1024
DEPTH = 2
N_MEM = 256
HEAD_DIM = 64
FOX_HEADS = 12
MEM_HEADS = 4
MLA_HEADS = 12
Q_LORA = 384
KV_LORA = 256
QK_NOPE = 64
QK_ROPE = 32
V_DIM = 64
ROPE_THETA = 10000.0
N_EXPERTS = 32
TOP_K = 4
D_EXPERT = D_MODEL
SWIGLU_LIMIT = 7.0
SWIGLU_ALPHA = 1.702
LN_EPS = 1e-5
RMS_EPS = 1e-6
NEG_INF = -1e30
DEEPNORM_ALPHA = (2 * DEPTH) ** 0.25
FOX_WIDTH = FOX_HEADS * HEAD_DIM
MEM_WIDTH = MEM_HEADS * HEAD_DIM
MLA_V_WIDTH = MLA_HEADS * V_DIM

LANES = 128
LOG2E = math.log2(math.e)
VMEM_LIMIT = 48 * 1024 * 1024
EXPERT_VMEM_LIMIT = 48 * 1024 * 1024

PROJ_TM = 512
ATT_TQ = 512
FOX_TQ = 512
MOE_TM = 512
COMBINE_TM = 512
MOE_SPLIT = 4
MLA_SLAB = LANES
VT_ROWS = HEAD_DIM + 16
SC_CORES = 2
SC_SUBCORES = 16
SC_WORKERS = SC_CORES * SC_SUBCORES
SC_CHUNK = 32
SC_LANES = 16
SC_SCAN = 8192


def _cparams(n_axes):
    return pltpu.CompilerParams(dimension_semantics=("arbitrary",) * n_axes,
                                vmem_limit_bytes=VMEM_LIMIT)


def _split3(x):
    hi = x.astype(BF16)
    r1 = x - hi.astype(F32)
    mid = r1.astype(BF16)
    lo = (r1 - mid.astype(F32)).astype(BF16)
    return hi, mid, lo


def _pack_bf16_pairs(v):
    bits = pltpu.bitcast(v.astype(F32), jnp.uint32)
    half = v.shape[1] // 2
    return (bits[:, :half] >> 16) | bits[:, half:]


def _unpack_bf16_pairs(words):
    return jnp.concatenate([pltpu.bitcast(words << 16, F32),
                            pltpu.bitcast(words & jnp.uint32(0xFFFF0000), F32)], axis=1)


def _layer_norm(y, g, b):
    mu = jnp.mean(y, axis=-1, keepdims=True)
    yc = y - mu
    var = jnp.mean(yc * yc, axis=-1, keepdims=True)
    return yc * lax.rsqrt(var + LN_EPS) * g + b


def _rms_norm(y, g):
    return y * lax.rsqrt(jnp.mean(y * y, axis=-1, keepdims=True) + RMS_EPS) * g


def _store_transposed(dst_ref, val):
    for s in range(val.shape[1] // LANES):
        sl = slice(s * LANES, (s + 1) * LANES)
        dst_ref[0, sl, :] = val[:, sl].T.astype(dst_ref.dtype)


def _store_values_transposed(vt_ref, val):
    tm = val.shape[0]
    ones = jnp.ones((VT_ROWS - HEAD_DIM, tm), vt_ref.dtype)
    for s in range(val.shape[1] // LANES):
        pair_t = val[:, s * LANES:(s + 1) * LANES].T.astype(vt_ref.dtype)
        for h in range(2):
            r0 = (2 * s + h) * VT_ROWS
            vt_ref[0, r0:r0 + HEAD_DIM, :] = pair_t[h * HEAD_DIM:(h + 1) * HEAD_DIM]
            vt_ref[0, r0 + HEAD_DIM:r0 + VT_ROWS, :] = ones


def _fox_proj_kernel(x_ref, w_ref, bf_ref, seg_ref, qt_ref, k_ref, vt_ref, mqt_ref, cum_ref,
                     cumt_ref, cfirst_ref, clast_ref, knorm_ref, carry_ref, knmax_ref, *,
                     tiles_per_batch):
    t = pl.program_id(0)

    @pl.when(t % tiles_per_batch == 0)
    def _():
        carry_ref[...] = jnp.zeros_like(carry_ref)
        knmax_ref[...] = jnp.zeros_like(knmax_ref)

    tm = x_ref.shape[0]
    proj = jnp.dot(x_ref[...].astype(BF16), w_ref[...], preferred_element_type=F32)
    qscale = HEAD_DIM ** -0.5 * LOG2E
    _store_transposed(qt_ref, proj[:, :FOX_WIDTH] * qscale)
    kb = proj[:, FOX_WIDTH:2 * FOX_WIDTH].astype(BF16)
    k_ref[...] = kb
    _store_values_transposed(vt_ref, proj[:, 2 * FOX_WIDTH:3 * FOX_WIDTH])
    kf = kb.astype(F32)
    ksq = jnp.dot((kf * kf).astype(BF16), seg_ref[...], preferred_element_type=F32)
    n_sub = tm // FOX_TQ
    for sub in range(n_sub):
        tile_max = jnp.sqrt(jnp.max(ksq[sub * FOX_TQ:(sub + 1) * FOX_TQ], axis=0, keepdims=True))
        knmax_ref[...] = jnp.maximum(knmax_ref[...], tile_max)
        knorm_ref[sub] = knmax_ref[...]
    f = proj[:, 3 * FOX_WIDTH:3 * FOX_WIDTH + LANES] + bf_ref[...]
    _store_transposed(mqt_ref, proj[:, 3 * FOX_WIDTH + LANES:] * qscale)
    log_f = jnp.minimum(f, 0.0) - jnp.log1p(jnp.exp(-jnp.abs(f)))
    row = lax.broadcasted_iota(jnp.int32, (tm, tm), 0)
    col = lax.broadcasted_iota(jnp.int32, (tm, tm), 1)
    tri = jnp.where(row >= col, 1.0, 0.0).astype(BF16)
    hi, mid, lo = _split3(log_f)
    cum = (jnp.dot(tri, hi, preferred_element_type=F32)
           + jnp.dot(tri, mid, preferred_element_type=F32)
           + jnp.dot(tri, lo, preferred_element_type=F32)) + carry_ref[...]
    carry_ref[...] = cum[tm - 1:tm, :]
    cum2 = cum * LOG2E
    cum_ref[...] = cum2
    cumt_ref[0] = cum2.T[:16, :]
    for sub in range(n_sub):
        cfirst_ref[sub] = cum2[sub * FOX_TQ:sub * FOX_TQ + 1, :]
        clast_ref[sub] = cum2[(sub + 1) * FOX_TQ - 1:(sub + 1) * FOX_TQ, :]


def _fox_proj(x2d, w, bf, seq):
    t_tokens = x2d.shape[0]
    tm = PROJ_TM
    nb = t_tokens // seq
    n = w.shape[1]
    tiles_per_batch = seq // tm
    n_tiles = t_tokens // tm
    seg = (jnp.arange(FOX_WIDTH)[:, None] // HEAD_DIM == jnp.arange(LANES)[None, :]).astype(BF16)
    row_spec = lambda width: pl.BlockSpec((tm, width), lambda i: (i, 0))
    t_spec = lambda rows: pl.BlockSpec(
        (1, rows, tm), lambda i: (i // tiles_per_batch, 0, i % tiles_per_batch))
    n_sub = tm // FOX_TQ
    stat_spec = pl.BlockSpec((n_sub, 1, LANES), lambda i: (i, 0, 0))
    stat_shape = jax.ShapeDtypeStruct((n_tiles * n_sub, 1, LANES), F32)
    return pl.pallas_call(
        functools.partial(_fox_proj_kernel, tiles_per_batch=tiles_per_batch),
        grid=(n_tiles,),
        in_specs=[row_spec(D_MODEL),
                  pl.BlockSpec((D_MODEL, n), lambda i: (0, 0)),
                  pl.BlockSpec((1, LANES), lambda i: (0, 0)),
                  pl.BlockSpec((FOX_WIDTH, LANES), lambda i: (0, 0))],
        out_specs=[t_spec(FOX_WIDTH), row_spec(FOX_WIDTH), t_spec(FOX_HEADS * VT_ROWS),
                   t_spec(MEM_WIDTH), row_spec(LANES), t_spec(16),
                   stat_spec, stat_spec, stat_spec],
        out_shape=[jax.ShapeDtypeStruct((nb, FOX_WIDTH, seq), BF16),
                   jax.ShapeDtypeStruct((t_tokens, FOX_WIDTH), BF16),
                   jax.ShapeDtypeStruct((nb, FOX_HEADS * VT_ROWS, seq), BF16),
                   jax.ShapeDtypeStruct((nb, MEM_WIDTH, seq), BF16),
                   jax.ShapeDtypeStruct((t_tokens, LANES), F32),
                   jax.ShapeDtypeStruct((nb, 16, seq), F32),
                   stat_shape, stat_shape, stat_shape],
        scratch_shapes=[pltpu.VMEM((1, LANES), F32), pltpu.VMEM((1, LANES), F32)],
        compiler_params=_cparams(1),
        name="fox_proj",
    )(x2d, w, bf, seg)


def _rope_slab_t(slab_t, cos_t, sin_t):
    half = QK_ROPE // 2
    x1 = slab_t[QK_NOPE:QK_NOPE + half]
    x2 = slab_t[QK_NOPE + half:QK_NOPE + QK_ROPE]
    return jnp.concatenate([slab_t[:QK_NOPE], x1 * cos_t - x2 * sin_t, x1 * sin_t + x2 * cos_t,
                            slab_t[QK_NOPE + QK_ROPE:]], axis=0)


def _mla_proj_kernel(x_ref, pos_ref, invf_ref, w_ref, gq_ref, wuq_ref, gkv_ref, wk_ref, wv_ref,
                     qt_ref, mqt_ref, k_ref, vt_ref):
    proj = jnp.dot(x_ref[...].astype(BF16), w_ref[...], preferred_element_type=F32)
    ang_t = invf_ref[...] * pos_ref[...].astype(F32)
    cos_t = jnp.cos(ang_t)
    sin_t = jnp.sin(ang_t)
    c_q = _rms_norm(proj[:, :Q_LORA], gq_ref[...])
    _store_transposed(mqt_ref, proj[:, Q_LORA:Q_LORA + MEM_WIDTH] * (HEAD_DIM ** -0.5 * LOG2E))
    kv_off = Q_LORA + MEM_WIDTH
    c_kv = _rms_norm(proj[:, kv_off:kv_off + KV_LORA], gkv_ref[...])
    kr = _rope_slab_t(proj[:, kv_off + KV_LORA:].T, cos_t, sin_t).T
    q = jnp.dot(c_q.astype(BF16), wuq_ref[...], preferred_element_type=F32)
    kn = jnp.dot(c_kv.astype(BF16), wk_ref[...], preferred_element_type=F32)
    qscale = (QK_NOPE + QK_ROPE) ** -0.5 * LOG2E
    for h in range(MLA_HEADS):
        sl = slice(h * MLA_SLAB, (h + 1) * MLA_SLAB)
        qt_ref[0, sl, :] = (_rope_slab_t(q[:, sl].T, cos_t, sin_t) * qscale).astype(BF16)
        k_ref[:, sl] = (kn[:, sl] + kr).astype(BF16)
    _store_values_transposed(vt_ref, jnp.dot(c_kv.astype(BF16), wv_ref[...],
                                             preferred_element_type=F32))


def _mla_proj(x2d, pos2d, invf, w, gq, wuq, gkv, wk, wv, seq):
    t_tokens = x2d.shape[0]
    tm = PROJ_TM
    nb = t_tokens // seq
    tiles_per_batch = seq // tm
    row_spec = lambda width: pl.BlockSpec((tm, width), lambda i: (i, 0))
    t_spec = lambda rows: pl.BlockSpec(
        (1, rows, tm), lambda i: (i // tiles_per_batch, 0, i % tiles_per_batch))
    full = lambda a: pl.BlockSpec(a.shape, lambda i: (0, 0))
    slabs = MLA_HEADS * MLA_SLAB
    return pl.pallas_call(
        _mla_proj_kernel,
        grid=(t_tokens // tm,),
        in_specs=[row_spec(D_MODEL), pl.BlockSpec((1, tm), lambda i: (0, i)), full(invf), full(w),
                  full(gq), full(wuq), full(gkv), full(wk), full(wv)],
        out_specs=[t_spec(slabs), t_spec(MEM_WIDTH), row_spec(slabs),
                   t_spec(MLA_HEADS * VT_ROWS)],
        out_shape=[jax.ShapeDtypeStruct((nb, slabs, seq), BF16),
                   jax.ShapeDtypeStruct((nb, MEM_WIDTH, seq), BF16),
                   jax.ShapeDtypeStruct((t_tokens, slabs), BF16),
                   jax.ShapeDtypeStruct((nb, MLA_HEADS * VT_ROWS, seq), BF16)],
        compiler_params=_cparams(1),
        name="mla_proj",
    )(x2d, pos2d, invf, w, gq, wuq, gkv, wk, wv)


SKIP_LOG2 = -160.0
NORM_SLACK = 1.02


def _causal_attn_kernel(*refs, tq, n_tiles, fox, slab, long_body):
    if fox:
        (cf_ref, cl_ref, kn_ref, qt_ref, k_ref, vt_ref, cq_ref, ck_ref,
         o_ref, m_ref, acc_ref, s_ref, cmax_ref) = refs
    else:
        qt_ref, k_ref, vt_ref, o_ref, m_ref, acc_ref, s_ref, cmax_ref = refs
    b = pl.program_id(0)
    pair = pl.program_id(1)
    i = pl.program_id(2)
    m_ref[...] = jnp.full(m_ref.shape, NEG_INF, F32)
    acc_ref[...] = jnp.zeros(acc_ref.shape, F32)
    qt = qt_ref[0]
    if slab:
        qth = [qt[:MLA_SLAB], qt[MLA_SLAB:]]
    else:
        rowi = lax.broadcasted_iota(jnp.int32, qt.shape, 0)
        zero = jnp.zeros_like(qt)
        qth = [jnp.where(rowi < HEAD_DIM, qt, zero), jnp.where(rowi >= HEAD_DIM, qt, zero)]
    if fox:
        lane = lax.broadcasted_iota(jnp.int32, (1, LANES), 1)
        cq = [cq_ref[0, pl.ds(2 * pair + h, 1), :] for h in range(2)]

    def logits_to(slot, j):
        off = pl.multiple_of(j * tq, tq)
        kc = k_ref[0, pl.ds(off, tq), :]
        for h in range(2):
            kh = kc[:, h * MLA_SLAB:(h + 1) * MLA_SLAB] if slab else kc
            s = jnp.dot(kh, qth[h], preferred_element_type=F32)
            if fox:
                ck_blk = ck_ref[0, pl.ds(off, tq), :]
                ck = jnp.sum(jnp.where(lane == 2 * pair + h, ck_blk, 0.0), axis=1, keepdims=True)
                s = s + cq[h] - ck
            s_ref[slot, h] = s
            cmax_ref[slot, h] = jnp.max(s, axis=0, keepdims=True)

    def softmax_pv(slot, j, causal_mask):
        off = pl.multiple_of(j * tq, tq)
        for h in range(2):
            s = s_ref[slot, h]
            if causal_mask is not None:
                s = jnp.where(causal_mask, s, NEG_INF)
                chunk_max = jnp.max(s, axis=0, keepdims=True)
            else:
                chunk_max = cmax_ref[slot, h]
            m_prev = m_ref[h]
            m_new = jnp.maximum(m_prev, chunk_max)
            alpha = jnp.exp2(m_prev - m_new)
            p = jnp.exp2(s - m_new).astype(BF16)
            vth = vt_ref[0, h * VT_ROWS:(h + 1) * VT_ROWS, pl.ds(off, tq)]
            acc_ref[h] = acc_ref[h] * alpha + jnp.dot(vth, p, preferred_element_type=F32)
            m_ref[h] = m_new

    if fox:
        base = b * n_tiles
        qn = []
        for h in range(2):
            qf = qth[h].astype(F32)
            qn.append(jnp.sqrt(jnp.max(jnp.sum(qf * qf, axis=0, keepdims=True))) * NORM_SLACK)

        top = [2.0 * qn[h] * kn_ref[base + i, 2 * pair + h] + cf_ref[base + i, 2 * pair + h]
               for h in range(2)]

        def live(j):
            ub = [top[h] - cl_ref[base + jnp.maximum(j, 0), 2 * pair + h] for h in range(2)]
            return (j >= 0) & (jnp.maximum(ub[0], ub[1]) > SKIP_LOG2)

        j0 = lax.while_loop(live, lambda j: j - 1, i - 1) + 1
    else:
        j0 = 0
    n_before = i - j0
    odd = (n_before % 2) == 1

    @pl.when(odd)
    def _():
        logits_to(1, j0)
        logits_to(0, j0 + 1)
        softmax_pv(1, j0, None)

    @pl.when(jnp.logical_not(odd))
    def _():
        logits_to(0, j0)

    def two_steps(j):
        logits_to(1, j + 1)
        softmax_pv(0, j, None)
        logits_to(0, j + 2)
        softmax_pv(1, j + 1, None)

    j_even = j0 + (n_before % 2)
    n_two = n_before // 2
    if long_body:
        @pl.when((n_two % 2) == 1)
        def _():
            two_steps(j_even)

        j_quad = j_even + 2 * (n_two % 2)

        def body(t, carry):
            two_steps(j_quad + 4 * t)
            two_steps(j_quad + 4 * t + 2)
            return carry

        lax.fori_loop(0, n_two // 2, body, 0)
    else:
        def body(t, carry):
            two_steps(j_even + 2 * t)
            return carry

        lax.fori_loop(0, n_two, body, 0)
    key = lax.broadcasted_iota(jnp.int32, (tq, tq), 0)
    qry = lax.broadcasted_iota(jnp.int32, (tq, tq), 1)
    softmax_pv(0, i, key <= qry)
    out_t = jnp.concatenate(
        [acc_ref[h, :HEAD_DIM] * (1.0 / acc_ref[h, HEAD_DIM:HEAD_DIM + 1]) for h in range(2)],
        axis=0)
    o_ref[0] = out_t.T.astype(o_ref.dtype)


def _causal_attention(qt, k, vt, fox_args, *, slab):
    nb, seq, _ = k.shape
    fox = fox_args is not None
    tq = FOX_TQ if fox else ATT_TQ
    n_tiles = seq // tq
    n_pairs = FOX_HEADS // 2
    rows = 2 * MLA_SLAB if slab else LANES
    in_specs = [pl.BlockSpec((1, rows, tq), lambda b, p, i: (b, p, i)),
                pl.BlockSpec((1, seq, rows), lambda b, p, i: (b, 0, p)),
                pl.BlockSpec((1, 2 * VT_ROWS, seq), lambda b, p, i: (b, p, 0))]
    args = [qt, k, vt]
    if fox:
        cum, cumt, cfirst, clast, knorm = fox_args
        smem = pl.BlockSpec(memory_space=pltpu.SMEM)
        in_specs = [smem, smem, smem] + in_specs + [
            pl.BlockSpec((1, 16, tq), lambda b, p, i: (b, 0, i)),
            pl.BlockSpec((1, seq, LANES), lambda b, p, i: (b, 0, 0))]
        args = [cfirst, clast, knorm] + args + [cumt, cum]
    return pl.pallas_call(
        functools.partial(_causal_attn_kernel, tq=tq, n_tiles=n_tiles, fox=fox, slab=slab,
                          long_body=not fox),
        grid=(nb, n_pairs, n_tiles),
        in_specs=in_specs,
        out_specs=pl.BlockSpec((1, tq, LANES), lambda b, p, i: (b, i, p)),
        out_shape=jax.ShapeDtypeStruct((nb, seq, n_pairs * LANES), BF16),
        scratch_shapes=[pltpu.VMEM((2, 1, tq), F32), pltpu.VMEM((2, VT_ROWS, tq), F32),
                        pltpu.VMEM((2, 2, tq, tq), F32), pltpu.VMEM((2, 2, 1, tq), F32)],
        compiler_params=_cparams(3),
        name="fox_attention" if fox else "mla_attention",
    )(*args)


def _mem_proj_kernel(mem_ref, w_ref, mk_ref, mvt_ref):
    mkv = jnp.dot(mem_ref[...].astype(BF16), w_ref[...], preferred_element_type=F32)
    mk_ref[0] = mkv[:, :MEM_WIDTH].astype(BF16)
    _store_values_transposed(mvt_ref, mkv[:, MEM_WIDTH:])


def _mem_proj(mem2d, w, nb):
    return pl.pallas_call(
        _mem_proj_kernel,
        grid=(nb,),
        in_specs=[pl.BlockSpec((N_MEM, D_MODEL), lambda b: (b, 0)),
                  pl.BlockSpec(w.shape, lambda b: (0, 0))],
        out_specs=[pl.BlockSpec((1, N_MEM, MEM_WIDTH), lambda b: (b, 0, 0)),
                   pl.BlockSpec((1, MEM_HEADS * VT_ROWS, N_MEM), lambda b: (b, 0, 0))],
        out_shape=[jax.ShapeDtypeStruct((nb, N_MEM, MEM_WIDTH), BF16),
                   jax.ShapeDtypeStruct((nb, MEM_HEADS * VT_ROWS, N_MEM), BF16)],
        compiler_params=_cparams(1),
        name="mem_proj",
    )(mem2d, w)


def _mem_attn_kernel(qt_ref, k_ref, vt_ref, o_ref):
    qt = qt_ref[0]
    kc = k_ref[0]
    for pair in range(MEM_HEADS // 2):
        qt_p = qt[pair * LANES:(pair + 1) * LANES]
        k_p = kc[:, pair * LANES:(pair + 1) * LANES]
        rowi = lax.broadcasted_iota(jnp.int32, qt_p.shape, 0)
        zero = jnp.zeros_like(qt_p)
        outs = []
        for h in range(2):
            mine = (rowi < HEAD_DIM) if h == 0 else (rowi >= HEAD_DIM)
            s = jnp.dot(k_p, jnp.where(mine, qt_p, zero), preferred_element_type=F32)
            p = jnp.exp2(s - jnp.max(s, axis=0, keepdims=True)).astype(BF16)
            r0 = (2 * pair + h) * VT_ROWS
            acc = jnp.dot(vt_ref[0, r0:r0 + VT_ROWS, :], p, preferred_element_type=F32)
            outs.append(acc[:HEAD_DIM] * (1.0 / acc[HEAD_DIM:HEAD_DIM + 1]))
        o_ref[0, :, pair * LANES:(pair + 1) * LANES] = (
            jnp.concatenate(outs, axis=0).T.astype(o_ref.dtype))


def _memory_attention(mqt, mk, mvt):
    nb, _, seq = mqt.shape
    tq = ATT_TQ
    return pl.pallas_call(
        _mem_attn_kernel,
        grid=(nb, seq // tq),
        in_specs=[pl.BlockSpec((1, MEM_WIDTH, tq), lambda b, i: (b, 0, i)),
                  pl.BlockSpec((1, N_MEM, MEM_WIDTH), lambda b, i: (b, 0, 0)),
                  pl.BlockSpec((1, MEM_HEADS * VT_ROWS, N_MEM), lambda b, i: (b, 0, 0))],
        out_specs=pl.BlockSpec((1, tq, MEM_WIDTH), lambda b, i: (b, i, 0)),
        out_shape=jax.ShapeDtypeStruct((nb, seq, MEM_WIDTH), BF16),
        compiler_params=_cparams(2),
        name="memory_attention",
    )(mqt, mk, mvt)


def _post_attn_kernel(x_ref, att_ref, memo_ref, wa_ref, wm_ref, g_ref, b_ref, wr_ref, br_ref,
                      x1_ref, x1p_ref, idx_ref, rank_ref, gate_ref, cnt_ref, carry_ref):
    t = pl.program_id(0)

    @pl.when(t == 0)
    def _():
        carry_ref[...] = jnp.zeros_like(carry_ref)

    tm = x_ref.shape[0]
    mix = (jnp.dot(att_ref[...], wa_ref[...], preferred_element_type=F32)
           + jnp.dot(memo_ref[...], wm_ref[...], preferred_element_type=F32))
    x1 = _layer_norm(DEEPNORM_ALPHA * x_ref[...] + mix, g_ref[...], b_ref[...])
    x1_ref[...] = x1
    x1b = x1.astype(BF16)
    x1p_ref[...] = _pack_bf16_pairs(x1b)
    logits = jnp.dot(x1b, wr_ref[...], preferred_element_type=F32) + br_ref[...]
    lane = lax.broadcasted_iota(jnp.int32, (tm, LANES), 1)
    work = jnp.where(lane < N_EXPERTS, logits, -jnp.inf)
    idxs, vals = [], []
    onehot = jnp.zeros((tm, LANES), F32)
    for _ in range(TOP_K):
        best = jnp.max(work, axis=1, keepdims=True)
        where_best = jnp.argmax(work, axis=1, keepdims=True).astype(jnp.int32)
        hit = lane == where_best
        onehot = jnp.where(hit, 1.0, onehot)
        work = jnp.where(hit, -jnp.inf, work)
        idxs.append(where_best)
        vals.append(best)
    exps = [jnp.exp(v - vals[0]) for v in vals]
    denom = exps[0] + exps[1] + exps[2] + exps[3]
    row = lax.broadcasted_iota(jnp.int32, (tm, tm), 0)
    col = lax.broadcasted_iota(jnp.int32, (tm, tm), 1)
    strict = jnp.where(row > col, 1.0, 0.0).astype(BF16)
    before = jnp.dot(strict, onehot.astype(BF16), preferred_element_type=F32) + carry_ref[...]
    idx_out = jnp.zeros((tm, LANES), F32)
    rank_out = jnp.zeros((tm, LANES), F32)
    gate_out = jnp.zeros((tm, LANES), F32)
    for r in range(TOP_K):
        rank_r = jnp.sum(jnp.where(lane == idxs[r], before, 0.0), axis=1, keepdims=True)
        idx_out = jnp.where(lane == r, idxs[r].astype(F32), idx_out)
        rank_out = jnp.where(lane == r, rank_r, rank_out)
        gate_out = jnp.where(lane == r, exps[r] / denom, gate_out)
    idx_ref[...] = idx_out.T[:8].astype(jnp.int32)
    rank_ref[...] = rank_out.T[:8].astype(jnp.int32)
    gate_ref[...] = gate_out
    total = carry_ref[...] + jnp.sum(onehot, axis=0, keepdims=True)
    carry_ref[...] = total
    cnt_ref[...] = jnp.broadcast_to(total, cnt_ref.shape)


def _post_attn(x2d, att, memo, wa, wm, g, b, wr, br):
    t_tokens = x2d.shape[0]
    tm = PROJ_TM
    row_spec = lambda width: pl.BlockSpec((tm, width), lambda i: (i, 0))
    full = lambda a: pl.BlockSpec(a.shape, lambda i: (0, 0))
    return pl.pallas_call(
        _post_attn_kernel,
        grid=(t_tokens // tm,),
        in_specs=[row_spec(D_MODEL), row_spec(att.shape[1]), row_spec(MEM_WIDTH),
                  full(wa), full(wm), full(g), full(b), full(wr), full(br)],
        out_specs=[row_spec(D_MODEL), row_spec(D_MODEL // 2),
                   pl.BlockSpec((8, tm), lambda i: (0, i)), pl.BlockSpec((8, tm), lambda i: (0, i)),
                   row_spec(LANES), pl.BlockSpec((8, LANES), lambda i: (0, 0))],
        out_shape=[jax.ShapeDtypeStruct((t_tokens, D_MODEL), F32),
                   jax.ShapeDtypeStruct((t_tokens, D_MODEL // 2), jnp.uint32),
                   jax.ShapeDtypeStruct((8, t_tokens), jnp.int32),
                   jax.ShapeDtypeStruct((8, t_tokens), jnp.int32),
                   jax.ShapeDtypeStruct((t_tokens, LANES), F32),
                   jax.ShapeDtypeStruct((8, LANES), F32)],
        scratch_shapes=[pltpu.VMEM((1, LANES), F32)],
        compiler_params=_cparams(1),
        name="outproj_ln_router",
    )(x2d, att, memo, wa, wm, g, b, wr, br)


def _expert_kernel(be_ref, nu_ref, br_ref, x_ref, wgu_hbm, bgu_ref, wdn_hbm, bdn_ref, *rest,
                   n_chunks, blk_lo, n_blocks, layer, has_prev):
    o_ref, wgu_st, wdn_st, wgu_bf, wdn_bf, h_ref, sem = rest[1:] if has_prev else rest
    step = pl.program_id(0)
    blk = step + blk_lo
    live_end = jnp.minimum(blk_lo + pl.num_programs(0), nu_ref[0])
    expert = be_ref[blk]

    def fetch(e):
        return (pltpu.make_async_copy(wgu_hbm.at[layer, e], wgu_st, sem.at[0]),
                pltpu.make_async_copy(wdn_hbm.at[layer, e], wdn_st, sem.at[1]))

    @pl.when(((step == 0) | (expert != be_ref[jnp.maximum(blk - 1, 0)])) & (blk < live_end))
    def _():
        @pl.when(step == 0)
        def _():
            for copy in fetch(expert):
                copy.start()

        for copy in fetch(expert):
            copy.wait()
        wgu_bf[...] = wgu_st[...].astype(BF16)
        wdn_bf[...] = wdn_st[...].astype(BF16)
        nxt = lax.while_loop(
            lambda j: (j < live_end) & (be_ref[jnp.minimum(j, n_blocks - 1)] == expert),
            lambda j: j + 1, blk + 1)

        @pl.when(nxt < live_end)
        def _():
            for copy in fetch(be_ref[jnp.minimum(nxt, n_blocks - 1)]):
                copy.start()

    def mlp(rows):
        x = _unpack_bf16_pairs(x_ref[:rows]).astype(BF16)
        cw = D_EXPERT // n_chunks
        for c in range(n_chunks):
            gs = slice(c * cw, (c + 1) * cw)
            us = slice(D_EXPERT + c * cw, D_EXPERT + (c + 1) * cw)
            g = jnp.dot(x, wgu_bf[:, gs], preferred_element_type=F32) + bgu_ref[0, 0, :, gs]
            u = jnp.dot(x, wgu_bf[:, us], preferred_element_type=F32) + bgu_ref[0, 0, :, us]
            g = jnp.minimum(g, SWIGLU_LIMIT)
            u = jnp.clip(u, -SWIGLU_LIMIT, SWIGLU_LIMIT)
            h_ref[:rows, gs] = ((u + 1.0) * (g * jax.nn.sigmoid(SWIGLU_ALPHA * g))).astype(BF16)
        y = jnp.dot(h_ref[:rows], wdn_bf[...], preferred_element_type=F32) + bdn_ref[0, 0]
        return _pack_bf16_pairs(y.astype(BF16))

    tm = o_ref.shape[0]
    live_rows = br_ref[blk]

    @pl.when(live_rows > tm // 2)
    def _():
        o_ref[...] = mlp(tm)

    @pl.when((live_rows > 0) & (live_rows <= tm // 2))
    def _():
        o_ref[:tm // 2] = mlp(tm // 2)
        o_ref[tm // 2:] = jnp.zeros((tm - tm // 2, o_ref.shape[1]), o_ref.dtype)

    @pl.when(live_rows == 0)
    def _():
        o_ref[...] = jnp.zeros_like(o_ref)


def _experts(block_expert, n_used, block_rows, xb, blk_lo, p_rows, y_prev, layer, wgu, bgu, wdn, bdn):
    tm = MOE_TM
    has_prev = y_prev is not None
    b_map = lambda i, be, nu, br: (layer, be[i + blk_lo], 0, 0)
    in_specs = [pl.BlockSpec((tm, D_MODEL // 2), lambda i, be, nu, br: (i, 0)),
                pl.BlockSpec(memory_space=pl.ANY),
                pl.BlockSpec((1, 1, 1, 2 * D_EXPERT), b_map),
                pl.BlockSpec(memory_space=pl.ANY),
                pl.BlockSpec((1, 1, 1, D_MODEL), b_map)]
    args = [block_expert, n_used, block_rows, xb, wgu, bgu, wdn, bdn]
    if has_prev:
        in_specs.append(pl.BlockSpec(memory_space=pl.ANY))
        args.append(y_prev)
    grid_spec = pltpu.PrefetchScalarGridSpec(
        num_scalar_prefetch=3,
        grid=(xb.shape[0] // tm,),
        in_specs=in_specs,
        out_specs=pl.BlockSpec((tm, D_MODEL // 2), lambda i, be, nu, br: (i + blk_lo, 0)),
        scratch_shapes=[pltpu.VMEM((D_MODEL, 2 * D_EXPERT), F32),
                        pltpu.VMEM((D_EXPERT, D_MODEL), F32),
                        pltpu.VMEM((D_MODEL, 2 * D_EXPERT), BF16),
                        pltpu.VMEM((D_EXPERT, D_MODEL), BF16),
                        pltpu.VMEM((tm, D_EXPERT), BF16),
                        pltpu.SemaphoreType.DMA((2,))],
    )
    return pl.pallas_call(
        functools.partial(_expert_kernel, n_chunks=4, blk_lo=blk_lo, n_blocks=p_rows // tm,
                          layer=layer, has_prev=has_prev),
        grid_spec=grid_spec,
        out_shape=jax.ShapeDtypeStruct((p_rows, D_MODEL // 2), jnp.uint32),
        input_output_aliases={len(args) - 1: 0} if has_prev else {},
        compiler_params=pltpu.CompilerParams(dimension_semantics=("arbitrary",),
                                             vmem_limit_bytes=EXPERT_VMEM_LIMIT),
        name="experts",
    )(*args)


def _sc_move_rows(table_hbm, idx_v, out_hbm, base, n_chunks, rows_v, gsem, wsem):
    def gather(c, slot):
        off = pl.multiple_of(c * SC_CHUNK, SC_CHUNK)
        return pltpu.make_async_copy(table_hbm.at[idx_v.at[pl.ds(off, SC_CHUNK)]],
                                     rows_v.at[slot], gsem.at[slot])

    def put(c, slot):
        off = pl.multiple_of(c * SC_CHUNK, SC_CHUNK)
        return pltpu.make_async_copy(rows_v.at[slot], out_hbm.at[pl.ds(base + off, SC_CHUNK)],
                                     wsem.at[slot])

    gather(0, 0).start()
    gather(1, 1).start()

    @pl.loop(0, n_chunks, step=2)
    def _(c):
        for slot in range(2):
            gather(c + slot, slot).wait()
            put(c + slot, slot).start()
        for slot in range(2):
            put(c + slot, slot).wait()

            @pl.when(c + 2 + slot < n_chunks)
            def _():
                gather(c + 2 + slot, slot).start()


def _sc_gather_rows(table, idx):
    n_idx = idx.shape[0]
    width = table.shape[1]
    per_worker = n_idx // SC_WORKERS
    n_chunks = per_worker // SC_CHUNK
    mesh = plsc.VectorSubcoreMesh(core_axis_name="c", subcore_axis_name="s",
                                  num_cores=SC_CORES, num_subcores=SC_SUBCORES)

    def body(table_hbm, idx_hbm, out_hbm, idx_v, rows_v, gsem, wsem):
        wid = lax.axis_index("s") * SC_CORES + lax.axis_index("c")
        base = wid * per_worker
        pltpu.sync_copy(idx_hbm.at[pl.ds(base, per_worker)], idx_v)
        _sc_move_rows(table_hbm, idx_v, out_hbm, base, n_chunks, rows_v, gsem, wsem)

    return pl.kernel(
        body,
        out_type=jax.ShapeDtypeStruct((n_idx, width), table.dtype),
        mesh=mesh,
        scratch_types=[pltpu.VMEM((per_worker,), jnp.int32),
                       pltpu.VMEM((2, SC_CHUNK, width), table.dtype),
                       pltpu.SemaphoreType.DMA((2,)),
                       pltpu.SemaphoreType.DMA((2,))],
        name="sc_gather_rows",
    )(table, idx)


def _sc_dispatch_rows(table, dest, slot_lo, n_slots):
    n_tok, width = table.shape
    n_pairs = dest.shape[0]
    per_worker = n_slots // SC_WORKERS
    n_chunks = per_worker // SC_CHUNK
    n_scan = n_pairs // SC_SCAN
    mesh = plsc.VectorSubcoreMesh(core_axis_name="c", subcore_axis_name="s",
                                  num_cores=SC_CORES, num_subcores=SC_SUBCORES)

    def body(table_hbm, dest_hbm, out_hbm, tok_v, dest_v, rows_v, gsem, wsem):
        wid = lax.axis_index("s") * SC_CORES + lax.axis_index("c")
        base = wid * per_worker
        first = slot_lo + base
        lane = lax.iota(jnp.int32, SC_LANES)

        @pl.loop(0, per_worker // SC_LANES)
        def _(i):
            off = pl.multiple_of(i * SC_LANES, SC_LANES)
            tok_v[pl.ds(off, SC_LANES)] = lax.rem(first + off + lane, n_tok)

        @pl.loop(0, n_scan)
        def _(g):
            goff = pl.multiple_of(g * SC_SCAN, SC_SCAN)
            pltpu.sync_copy(dest_hbm.at[pl.ds(goff, SC_SCAN)], dest_v)

            @plsc.parallel_loop(0, SC_SCAN // SC_LANES, unroll=8)
            def _(i):
                off = pl.multiple_of(i * SC_LANES, SC_LANES)
                local = dest_v[pl.ds(off, SC_LANES)] - first
                mine = (local >= 0) & (local < per_worker)
                pair = goff + off + lane
                plsc.store_scatter(tok_v, [jnp.where(mine, local, 0)], lax.rem(pair, n_tok), mask=mine)

        _sc_move_rows(table_hbm, tok_v, out_hbm, base, n_chunks, rows_v, gsem, wsem)

    return pl.kernel(
        body,
        out_type=jax.ShapeDtypeStruct((n_slots, width), table.dtype),
        mesh=mesh,
        scratch_types=[pltpu.VMEM((per_worker,), jnp.int32),
                       pltpu.VMEM((SC_SCAN,), jnp.int32),
                       pltpu.VMEM((2, SC_CHUNK, width), table.dtype),
                       pltpu.SemaphoreType.DMA((2,)),
                       pltpu.SemaphoreType.DMA((2,))],
        compiler_params=pltpu.CompilerParams(needs_layout_passes=False),
        name="sc_dispatch_rows",
    )(table, dest)


def _combine_kernel(x1_ref, yg_ref, gate_ref, g_ref, b_ref, *rest):
    o_ref = rest[-1]
    gates = gate_ref[...]
    ffn = _unpack_bf16_pairs(yg_ref[0]) * gates[:, 0:1]
    for r in range(1, TOP_K):
        ffn = ffn + _unpack_bf16_pairs(yg_ref[r]) * gates[:, r:r + 1]
    o_ref[...] = _layer_norm(DEEPNORM_ALPHA * x1_ref[...] + ffn, g_ref[...], b_ref[...])


def _combine(x1, yg, gates, g, b, tok_lo, out_prev):
    t_tokens = x1.shape[0]
    tm = COMBINE_TM
    tile_lo = tok_lo // tm
    in_specs = [pl.BlockSpec((tm, D_MODEL), lambda i: (i + tile_lo, 0)),
                pl.BlockSpec((TOP_K, tm, D_MODEL // 2), lambda i: (0, i, 0)),
                pl.BlockSpec((tm, LANES), lambda i: (i + tile_lo, 0)),
                pl.BlockSpec((1, D_MODEL), lambda i: (0, 0)),
                pl.BlockSpec((1, D_MODEL), lambda i: (0, 0))]
    args = [x1, yg, gates, g, b]
    if out_prev is not None:
        in_specs.append(pl.BlockSpec(memory_space=pl.ANY))
        args.append(out_prev)
    return pl.pallas_call(
        _combine_kernel,
        grid=(yg.shape[1] // tm,),
        in_specs=in_specs,
        out_specs=pl.BlockSpec((tm, D_MODEL), lambda i: (i + tile_lo, 0)),
        out_shape=jax.ShapeDtypeStruct((t_tokens, D_MODEL), F32),
        input_output_aliases={len(args) - 1: 0} if out_prev is not None else {},
        compiler_params=_cparams(1),
        name="combine_ln",
    )(*args)


def _moe(x1, x1p, idx, rank, gates, counts, layer, wgu, bgu, wdn, bdn, g, b):
    t_tokens = x1.shape[0]
    tm = MOE_TM
    counts = counts[0, :N_EXPERTS].astype(jnp.int32)
    padded = ((counts + tm - 1) // tm) * tm
    pend = jnp.cumsum(padded)
    pstart = pend - padded
    dest = rank[:TOP_K]
    for e in range(N_EXPERTS):
        dest = dest + jnp.where(idx[:TOP_K] == e, pstart[e], 0)
    n_blocks = (t_tokens * TOP_K) // tm + N_EXPERTS
    p_rows = n_blocks * tm
    block_start = jnp.arange(n_blocks, dtype=jnp.int32) * tm
    block_expert = jnp.minimum(
        jnp.sum((pend[None, :] <= block_start[:, None]).astype(jnp.int32), axis=1), N_EXPERTS - 1)
    n_used = (pend[-1:] // tm).astype(jnp.int32)
    mine = block_expert[:, None] == jnp.arange(N_EXPERTS, dtype=jnp.int32)[None, :]
    live_end = jnp.sum(jnp.where(mine, (pstart + counts)[None, :], 0), axis=1)
    block_rows = jnp.clip(live_end - block_start, 0, tm)
    y = None
    for lo, hi in ((0, n_blocks // MOE_SPLIT), (n_blocks // MOE_SPLIT, n_blocks)):
        xb = _sc_dispatch_rows(x1p, dest.reshape(-1), lo * tm, (hi - lo) * tm)
        y = _experts(block_expert, n_used, block_rows, xb, lo, p_rows, y, layer, wgu, bgu, wdn, bdn)
    out = None
    for lo, hi in ((0, t_tokens // MOE_SPLIT), (t_tokens // MOE_SPLIT, t_tokens)):
        yg = _sc_gather_rows(y, dest[:, lo:hi].reshape(-1)).reshape(TOP_K, hi - lo, D_MODEL // 2)
        out = _combine(x1, yg, gates, g, b, lo, out)
    return out


def _row(v, width=None):
    v = v.astype(F32).reshape(1, -1)
    if width is not None and v.shape[1] < width:
        v = jnp.pad(v, ((0, 0), (0, width - v.shape[1])))
    return v


def _pad_cols(w, width):
    return jnp.pad(w, ((0, 0), (0, width - w.shape[1])))


def kernel(x, mem, positions, a_w_in, a_b_f, a_w_out, b_w_in, b_g_q, b_w_uq, b_w_out,
           kv_w_dkv, kv_g, kv_w_ukv, mem_w_kv, ln_g, ln_b,
           moe_w_r, moe_b_r, moe_w_gu, moe_b_gu, moe_w_dn, moe_b_dn):
    nb, seq, d = x.shape
    t_tokens = nb * seq
    n_a = a_w_in.shape[0]
    x2d = x.reshape(t_tokens, d)
    mem2d = mem.reshape(nb * N_MEM, d)
    pos2d = positions.reshape(1, t_tokens)
    half = QK_ROPE // 2
    invf = (ROPE_THETA ** (-jnp.arange(half, dtype=F32) * 2.0 / QK_ROPE)).reshape(half, 1)

    shared_kv = None
    for l in range(DEPTH):
        mk, mvt = _mem_proj(mem2d, mem_w_kv[l].astype(BF16), nb)
        if l < n_a:
            w_in = a_w_in[l]
            w = jnp.concatenate([w_in[:, :3 * FOX_WIDTH],
                                 _pad_cols(w_in[:, 3 * FOX_WIDTH:3 * FOX_WIDTH + FOX_HEADS], LANES),
                                 w_in[:, 3 * FOX_WIDTH + FOX_HEADS:]], axis=1).astype(BF16)
            qt, k, vt, mqt, cum, cumt, cfirst, clast, knorm = _fox_proj(
                x2d, w, _row(a_b_f[l], LANES), seq)
            stats = [s.reshape(-1, LANES) for s in (cfirst, clast, knorm)]
            att = _causal_attention(qt, k.reshape(nb, seq, -1), vt,
                                    [cum.reshape(nb, seq, LANES), cumt] + stats, slab=False)
            w_out = a_w_out[l]
        else:
            bl = l - n_a
            w = jnp.concatenate([b_w_in[bl], kv_w_dkv[:, :KV_LORA],
                                 jnp.zeros((d, QK_NOPE), F32), kv_w_dkv[:, KV_LORA:],
                                 jnp.zeros((d, LANES - QK_NOPE - QK_ROPE), F32)], axis=1).astype(BF16)
            wuq = b_w_uq[bl].reshape(Q_LORA, MLA_HEADS, QK_NOPE + QK_ROPE)
            wuq = jnp.pad(wuq, ((0, 0), (0, 0), (0, MLA_SLAB - QK_NOPE - QK_ROPE)))
            wuq = wuq.reshape(Q_LORA, MLA_HEADS * MLA_SLAB).astype(BF16)
            wukv = kv_w_ukv.reshape(KV_LORA, MLA_HEADS, QK_NOPE + V_DIM)
            wk = jnp.pad(wukv[:, :, :QK_NOPE], ((0, 0), (0, 0), (0, MLA_SLAB - QK_NOPE)))
            wk = wk.reshape(KV_LORA, MLA_HEADS * MLA_SLAB).astype(BF16)
            wv = wukv[:, :, QK_NOPE:].reshape(KV_LORA, MLA_V_WIDTH).astype(BF16)
            qt, mqt, k_new, vt_new = _mla_proj(x2d, pos2d, invf, w, _row(b_g_q[bl]), wuq,
                                              _row(kv_g), wk, wv, seq)
            if shared_kv is None:
                shared_kv = (k_new.reshape(nb, seq, -1), vt_new)
            att = _causal_attention(qt, shared_kv[0], shared_kv[1], None, slab=True)
            w_out = b_w_out[bl]
        memo = _memory_attention(mqt, mk, mvt)
        n_att = w_out.shape[0] - MEM_WIDTH
        x1, x1p, idx, rank, gates, counts = _post_attn(
            x2d, att.reshape(t_tokens, -1), memo.reshape(t_tokens, MEM_WIDTH),
            w_out[:n_att].astype(BF16), w_out[n_att:].astype(BF16),
            _row(ln_g[l, 0]), _row(ln_b[l, 0]),
            _pad_cols(moe_w_r[l], LANES).astype(BF16), _row(moe_b_r[l], LANES))
        x2d = _moe(x1, x1p, idx, rank, gates, counts, l,
                   moe_w_gu, moe_b_gu.reshape(DEPTH, N_EXPERTS, 1, -1),
                   moe_w_dn, moe_b_dn.reshape(DEPTH, N_EXPERTS, 1, -1),
                   _row(ln_g[l, 1]), _row(ln_b[l, 1]))
    return x2d.reshape(nb, seq, d)
```

```python
import functools
import math

import jax
import jax.numpy as jnp
from jax import lax
from jax.experimental import pallas as pl
from jax.experimental.pallas import tpu as pltpu
from jax.experimental.pallas import tpu_sc as plsc

F32 = jnp.float32
BF16 = jnp.bfloat16

D_MODEL = 1024
DEPTH = 2
N_MEM = 256
HEAD_DIM = 64
FOX_HEADS = 12
MEM_HEADS = 4
MLA_HEADS = 12
Q_LORA = 384
KV_LORA = 256
QK_NOPE = 64
QK_ROPE = 32
V_DIM = 64
ROPE_THETA = 10000.0
N_EXPERTS = 32
TOP_K = 4
D_EXPERT = D_MODEL
SWIGLU_LIMIT = 7.0
SWIGLU_ALPHA = 1.702
LN_EPS = 1e-5
RMS_EPS = 1e-6
NEG_INF = -1e30
DEEPNORM_ALPHA = (2 * DEPTH) ** 0.25
FOX_WIDTH = FOX_HEADS * HEAD_DIM
MEM_WIDTH = MEM_HEADS * HEAD_DIM
MLA_V_WIDTH = MLA_HEADS * V_DIM

LANES = 128
LOG2E = math.log2(math.e)
VMEM_LIMIT = 48 * 1024 * 1024

PROJ_TM = 512
ATT_TQ = 512
FOX_TQ = 512
MOE_TM = 512
COMBINE_TM = 512
MOE_SPLIT = 4
MLA_SLAB = LANES
VT_ROWS = HEAD_DIM + 16
SC_CORES = 2
SC_SUBCORES = 16
SC_WORKERS = SC_CORES * SC_SUBCORES
SC_CHUNK = 32
SC_LANES = 16
SC_SCAN = 8192


def _cparams(n_axes):
    return pltpu.CompilerParams(dimension_semantics=("arbitrary",) * n_axes,
                                vmem_limit_bytes=VMEM_LIMIT)


def _split3(x):
    hi = x.astype(BF16)
    r1 = x - hi.astype(F32)
    mid = r1.astype(BF16)
    lo = (r1 - mid.astype(F32)).astype(BF16)
    return hi, mid, lo


def _pack_bf16_pairs(v):
    bits = pltpu.bitcast(v.astype(F32), jnp.uint32)
    half = v.shape[1] // 2
    return (bits[:, :half] >> 16) | bits[:, half:]


def _unpack_bf16_pairs(words):
    return jnp.concatenate([pltpu.bitcast(words << 16, F32),
                            pltpu.bitcast(words & jnp.uint32(0xFFFF0000), F32)], axis=1)


def _layer_norm(y, g, b):
    mu = jnp.mean(y, axis=-1, keepdims=True)
    yc = y - mu
    var = jnp.mean(yc * yc, axis=-1, keepdims=True)
    return yc * lax.rsqrt(var + LN_EPS) * g + b


def _rms_norm(y, g):
    return y * lax.rsqrt(jnp.mean(y * y, axis=-1, keepdims=True) + RMS_EPS) * g


def _store_transposed(dst_ref, val):
    for s in range(val.shape[1] // LANES):
        sl = slice(s * LANES, (s + 1) * LANES)
        dst_ref[0, sl, :] = val[:, sl].T.astype(dst_ref.dtype)


def _store_values_transposed(vt_ref, val):
    tm = val.shape[0]
    ones = jnp.ones((VT_ROWS - HEAD_DIM, tm), vt_ref.dtype)
    for s in range(val.shape[1] // LANES):
        pair_t = val[:, s * LANES:(s + 1) * LANES].T.astype(vt_ref.dtype)
        for h in range(2):
            r0 = (2 * s + h) * VT_ROWS
            vt_ref[0, r0:r0 + HEAD_DIM, :] = pair_t[h * HEAD_DIM:(h + 1) * HEAD_DIM]
            vt_ref[0, r0 + HEAD_DIM:r0 + VT_ROWS, :] = ones


def _fox_proj_kernel(x_ref, w_ref, bf_ref, seg_ref, qt_ref, k_ref, vt_ref, mqt_ref, cum_ref,
                     cumt_ref, cfirst_ref, clast_ref, knorm_ref, carry_ref, knmax_ref, *,
                     tiles_per_batch):
    t = pl.program_id(0)

    @pl.when(t % tiles_per_batch == 0)
    def _():
        carry_ref[...] = jnp.zeros_like(carry_ref)
        knmax_ref[...] = jnp.zeros_like(knmax_ref)

    tm = x_ref.shape[0]
    proj = jnp.dot(x_ref[...].astype(BF16), w_ref[...], preferred_element_type=F32)
    qscale = HEAD_DIM ** -0.5 * LOG2E
    _store_transposed(qt_ref, proj[:, :FOX_WIDTH] * qscale)
    kb = proj[:, FOX_WIDTH:2 * FOX_WIDTH].astype(BF16)
    k_ref[...] = kb
    _store_values_transposed(vt_ref, proj[:, 2 * FOX_WIDTH:3 * FOX_WIDTH])
    kf = kb.astype(F32)
    ksq = jnp.dot((kf * kf).astype(BF16), seg_ref[...], preferred_element_type=F32)
    n_sub = tm // FOX_TQ
    for sub in range(n_sub):
        tile_max = jnp.sqrt(jnp.max(ksq[sub * FOX_TQ:(sub + 1) * FOX_TQ], axis=0, keepdims=True))
        knmax_ref[...] = jnp.maximum(knmax_ref[...], tile_max)
        knorm_ref[sub] = knmax_ref[...]
    f = proj[:, 3 * FOX_WIDTH:3 * FOX_WIDTH + LANES] + bf_ref[...]
    _store_transposed(mqt_ref, proj[:, 3 * FOX_WIDTH + LANES:] * qscale)
    log_f = jnp.minimum(f, 0.0) - jnp.log1p(jnp.exp(-jnp.abs(f)))
    row = lax.broadcasted_iota(jnp.int32, (tm, tm), 0)
    col = lax.broadcasted_iota(jnp.int32, (tm, tm), 1)
    tri = jnp.where(row >= col, 1.0, 0.0).astype(BF16)
    hi, mid, lo = _split3(log_f)
    cum = (jnp.dot(tri, hi, preferred_element_type=F32)
           + jnp.dot(tri, mid, preferred_element_type=F32)
           + jnp.dot(tri, lo, preferred_element_type=F32)) + carry_ref[...]
    carry_ref[...] = cum[tm - 1:tm, :]
    cum2 = cum * LOG2E
    cum_ref[...] = cum2
    cumt_ref[0] = cum2.T[:16, :]
    for sub in range(n_sub):
        cfirst_ref[sub] = cum2[sub * FOX_TQ:sub * FOX_TQ + 1, :]
        clast_ref[sub] = cum2[(sub + 1) * FOX_TQ - 1:(sub + 1) * FOX_TQ, :]


def _fox_proj(x2d, w, bf, seq):
    t_tokens = x2d.shape[0]
    tm = PROJ_TM
    nb = t_tokens // seq
    n = w.shape[1]
    tiles_per_batch = seq // tm
    n_tiles = t_tokens // tm
    seg = (jnp.arange(FOX_WIDTH)[:, None] // HEAD_DIM == jnp.arange(LANES)[None, :]).astype(BF16)
    row_spec = lambda width: pl.BlockSpec((tm, width), lambda i: (i, 0))
    t_spec = lambda rows: pl.BlockSpec(
        (1, rows, tm), lambda i: (i // tiles_per_batch, 0, i % tiles_per_batch))
    n_sub = tm // FOX_TQ
    stat_spec = pl.BlockSpec((n_sub, 1, LANES), lambda i: (i, 0, 0))
    stat_shape = jax.ShapeDtypeStruct((n_tiles * n_sub, 1, LANES), F32)
    return pl.pallas_call(
        functools.partial(_fox_proj_kernel, tiles_per_batch=tiles_per_batch),
        grid=(n_tiles,),
        in_specs=[row_spec(D_MODEL),
                  pl.BlockSpec((D_MODEL, n), lambda i: (0, 0)),
                  pl.BlockSpec((1, LANES), lambda i: (0, 0)),
                  pl.BlockSpec((FOX_WIDTH, LANES), lambda i: (0, 0))],
        out_specs=[t_spec(FOX_WIDTH), row_spec(FOX_WIDTH), t_spec(FOX_HEADS * VT_ROWS),
                   t_spec(MEM_WIDTH), row_spec(LANES), t_spec(16),
                   stat_spec, stat_spec, stat_spec],
        out_shape=[jax.ShapeDtypeStruct((nb, FOX_WIDTH, seq), BF16),
                   jax.ShapeDtypeStruct((t_tokens, FOX_WIDTH), BF16),
                   jax.ShapeDtypeStruct((nb, FOX_HEADS * VT_ROWS, seq), BF16),
                   jax.ShapeDtypeStruct((nb, MEM_WIDTH, seq), BF16),
                   jax.ShapeDtypeStruct((t_tokens, LANES), F32),
                   jax.ShapeDtypeStruct((nb, 16, seq), F32),
                   stat_shape, stat_shape, stat_shape],
        scratch_shapes=[pltpu.VMEM((1, LANES), F32), pltpu.VMEM((1, LANES), F32)],
        compiler_params=_cparams(1),
        name="fox_proj",
    )(x2d, w, bf, seg)


def _rope_slab_t(slab_t, cos_t, sin_t):
    half = QK_ROPE // 2
    x1 = slab_t[QK_NOPE:QK_NOPE + half]
    x2 = slab_t[QK_NOPE + half:QK_NOPE + QK_ROPE]
    return jnp.concatenate([slab_t[:QK_NOPE], x1 * cos_t - x2 * sin_t, x1 * sin_t + x2 * cos_t,
                            slab_t[QK_NOPE + QK_ROPE:]], axis=0)


def _mla_proj_kernel(x_ref, pos_ref, invf_ref, w_ref, gq_ref, wuq_ref, gkv_ref, wk_ref, wv_ref,
                     qt_ref, mqt_ref, k_ref, vt_ref):
    proj = jnp.dot(x_ref[...].astype(BF16), w_ref[...], preferred_element_type=F32)
    ang_t = invf_ref[...] * pos_ref[...].astype(F32)
    cos_t = jnp.cos(ang_t)
    sin_t = jnp.sin(ang_t)
    c_q = _rms_norm(proj[:, :Q_LORA], gq_ref[...])
    _store_transposed(mqt_ref, proj[:, Q_LORA:Q_LORA + MEM_WIDTH] * (HEAD_DIM ** -0.5 * LOG2E))
    kv_off = Q_LORA + MEM_WIDTH
    c_kv = _rms_norm(proj[:, kv_off:kv_off + KV_LORA], gkv_ref[...])
    kr = _rope_slab_t(proj[:, kv_off + KV_LORA:].T, cos_t, sin_t).T
    q = jnp.dot(c_q.astype(BF16), wuq_ref[...], preferred_element_type=F32)
    kn = jnp.dot(c_kv.astype(BF16), wk_ref[...], preferred_element_type=F32)
    qscale = (QK_NOPE + QK_ROPE) ** -0.5 * LOG2E
    for h in range(MLA_HEADS):
        sl = slice(h * MLA_SLAB, (h + 1) * MLA_SLAB)
        qt_ref[0, sl, :] = (_rope_slab_t(q[:, sl].T, cos_t, sin_t) * qscale).astype(BF16)
        k_ref[:, sl] = (kn[:, sl] + kr).astype(BF16)
    _store_values_transposed(vt_ref, jnp.dot(c_kv.astype(BF16), wv_ref[...],
                                             preferred_element_type=F32))


def _mla_proj(x2d, pos2d, invf, w, gq, wuq, gkv, wk, wv, seq):
    t_tokens = x2d.shape[0]
    tm = PROJ_TM
    nb = t_tokens // seq
    tiles_per_batch = seq // tm
    row_spec = lambda width: pl.BlockSpec((tm, width), lambda i: (i, 0))
    t_spec = lambda rows: pl.BlockSpec(
        (1, rows, tm), lambda i: (i // tiles_per_batch, 0, i % tiles_per_batch))
    full = lambda a: pl.BlockSpec(a.shape, lambda i: (0, 0))
    slabs = MLA_HEADS * MLA_SLAB
    return pl.pallas_call(
        _mla_proj_kernel,
        grid=(t_tokens // tm,),
        in_specs=[row_spec(D_MODEL), pl.BlockSpec((1, tm), lambda i: (0, i)), full(invf), full(w),
                  full(gq), full(wuq), full(gkv), full(wk), full(wv)],
        out_specs=[t_spec(slabs), t_spec(MEM_WIDTH), row_spec(slabs),
                   t_spec(MLA_HEADS * VT_ROWS)],
        out_shape=[jax.ShapeDtypeStruct((nb, slabs, seq), BF16),
                   jax.ShapeDtypeStruct((nb, MEM_WIDTH, seq), BF16),
                   jax.ShapeDtypeStruct((t_tokens, slabs), BF16),
                   jax.ShapeDtypeStruct((nb, MLA_HEADS * VT_ROWS, seq), BF16)],
        compiler_params=_cparams(1),
        name="mla_proj",
    )(x2d, pos2d, invf, w, gq, wuq, gkv, wk, wv)


SKIP_LOG2 = -160.0
NORM_SLACK = 1.02


def _causal_attn_kernel(*refs, tq, n_tiles, fox, slab, long_body):
    if fox:
        (cf_ref, cl_ref, kn_ref, qt_ref, k_ref, vt_ref, cq_ref, ck_ref,
         o_ref, m_ref, acc_ref, s_ref, cmax_ref) = refs
    else:
        qt_ref, k_ref, vt_ref, o_ref, m_ref, acc_ref, s_ref, cmax_ref = refs
    b = pl.program_id(0)
    pair = pl.program_id(1)
    i = pl.program_id(2)
    m_ref[...] = jnp.full(m_ref.shape, NEG_INF, F32)
    acc_ref[...] = jnp.zeros(acc_ref.shape, F32)
    qt = qt_ref[0]
    if slab:
        qth = [qt[:MLA_SLAB], qt[MLA_SLAB:]]
    else:
        rowi = lax.broadcasted_iota(jnp.int32, qt.shape, 0)
        zero = jnp.zeros_like(qt)
        qth = [jnp.where(rowi < HEAD_DIM, qt, zero), jnp.where(rowi >= HEAD_DIM, qt, zero)]
    if fox:
        lane = lax.broadcasted_iota(jnp.int32, (1, LANES), 1)
        cq = [cq_ref[0, pl.ds(2 * pair + h, 1), :] for h in range(2)]

    def logits_to(slot, j):
        off = pl.multiple_of(j * tq, tq)
        kc = k_ref[0, pl.ds(off, tq), :]
        for h in range(2):
            kh = kc[:, h * MLA_SLAB:(h + 1) * MLA_SLAB] if slab else kc
            s = jnp.dot(kh, qth[h], preferred_element_type=F32)
            if fox:
                ck_blk = ck_ref[0, pl.ds(off, tq), :]
                ck = jnp.sum(jnp.where(lane == 2 * pair + h, ck_blk, 0.0), axis=1, keepdims=True)
                s = s + cq[h] - ck
            s_ref[slot, h] = s
            cmax_ref[slot, h] = jnp.max(s, axis=0, keepdims=True)

    def softmax_pv(slot, j, causal_mask):
        off = pl.multiple_of(j * tq, tq)
        for h in range(2):
            s = s_ref[slot, h]
            if causal_mask is not None:
                s = jnp.where(causal_mask, s, NEG_INF)
                chunk_max = jnp.max(s, axis=0, keepdims=True)
            else:
                chunk_max = cmax_ref[slot, h]
            m_prev = m_ref[h]
            m_new = jnp.maximum(m_prev, chunk_max)
            alpha = jnp.exp2(m_prev - m_new)
            p = jnp.exp2(s - m_new).astype(BF16)
            vth = vt_ref[0, h * VT_ROWS:(h + 1) * VT_ROWS, pl.ds(off, tq)]
            acc_ref[h] = acc_ref[h] * alpha + jnp.dot(vth, p, preferred_element_type=F32)
            m_ref[h] = m_new

    if fox:
        base = b * n_tiles
        qn = []
        for h in range(2):
            qf = qth[h].astype(F32)
            qn.append(jnp.sqrt(jnp.max(jnp.sum(qf * qf, axis=0, keepdims=True))) * NORM_SLACK)

        top = [2.0 * qn[h] * kn_ref[base + i, 2 * pair + h] + cf_ref[base + i, 2 * pair + h]
               for h in range(2)]

        def live(j):
            ub = [top[h] - cl_ref[base + jnp.maximum(j, 0), 2 * pair + h] for h in range(2)]
            return (j >= 0) & (jnp.maximum(ub[0], ub[1]) > SKIP_LOG2)

        j0 = lax.while_loop(live, lambda j: j - 1, i - 1) + 1
    else:
        j0 = 0
    n_before = i - j0

    def causal_mask():
        key = lax.broadcasted_iota(jnp.int32, (tq, tq), 0)
        qry = lax.broadcasted_iota(jnp.int32, (tq, tq), 1)
        return key <= qry

    def pipelined():
        odd = (n_before % 2) == 1

        @pl.when(odd)
        def _():
            logits_to(1, j0)
            logits_to(0, j0 + 1)
            softmax_pv(1, j0, None)

        @pl.when(jnp.logical_not(odd))
        def _():
            logits_to(0, j0)

        def two_steps(j):
            logits_to(1, j + 1)
            softmax_pv(0, j, None)
            logits_to(0, j + 2)
            softmax_pv(1, j + 1, None)

        j_even = j0 + (n_before % 2)
        n_two = n_before // 2
        if long_body:
            @pl.when((n_two % 2) == 1)
            def _():
                two_steps(j_even)

            j_quad = j_even + 2 * (n_two % 2)

            def body(t, carry):
                two_steps(j_quad + 4 * t)
                two_steps(j_quad + 4 * t + 2)
                return carry

            lax.fori_loop(0, n_two // 2, body, 0)
        else:
            def body(t, carry):
                two_steps(j_even + 2 * t)
                return carry

            lax.fori_loop(0, n_two, body, 0)
        softmax_pv(0, i, causal_mask())

    if fox:
        @pl.when(n_before == 1)
        def _():
            logits_to(1, j0)
            logits_to(0, i)
            softmax_pv(1, j0, None)
            softmax_pv(0, i, causal_mask())

        @pl.when(n_before != 1)
        def _():
            pipelined()
    else:
        pipelined()
    out_t = jnp.concatenate(
        [acc_ref[h, :HEAD_DIM] * (1.0 / acc_ref[h, HEAD_DIM:HEAD_DIM + 1]) for h in range(2)],
        axis=0)
    o_ref[0] = out_t.T.astype(o_ref.dtype)


def _causal_attention(qt, k, vt, fox_args, *, slab):
    nb, seq, _ = k.shape
    fox = fox_args is not None
    tq = FOX_TQ if fox else ATT_TQ
    n_tiles = seq // tq
    n_pairs = FOX_HEADS // 2
    rows = 2 * MLA_SLAB if slab else LANES
    in_specs = [pl.BlockSpec((1, rows, tq), lambda b, p, i: (b, p, i)),
                pl.BlockSpec((1, seq, rows), lambda b, p, i: (b, 0, p)),
                pl.BlockSpec((1, 2 * VT_ROWS, seq), lambda b, p, i: (b, p, 0))]
    args = [qt, k, vt]
    if fox:
        cum, cumt, cfirst, clast, knorm = fox_args
        smem = pl.BlockSpec(memory_space=pltpu.SMEM)
        in_specs = [smem, smem, smem] + in_specs + [
            pl.BlockSpec((1, 16, tq), lambda b, p, i: (b, 0, i)),
            pl.BlockSpec((1, seq, LANES), lambda b, p, i: (b, 0, 0))]
        args = [cfirst, clast, knorm] + args + [cumt, cum]
    return pl.pallas_call(
        functools.partial(_causal_attn_kernel, tq=tq, n_tiles=n_tiles, fox=fox, slab=slab,
                          long_body=not fox),
        grid=(nb, n_pairs, n_tiles),
        in_specs=in_specs,
        out_specs=pl.BlockSpec((1, tq, LANES), lambda b, p, i: (b, i, p)),
        out_shape=jax.ShapeDtypeStruct((nb, seq, n_pairs * LANES), BF16),
        scratch_shapes=[pltpu.VMEM((2, 1, tq), F32), pltpu.VMEM((2, VT_ROWS, tq), F32),
                        pltpu.VMEM((2, 2, tq, tq), F32), pltpu.VMEM((2, 2, 1, tq), F32)],
        compiler_params=_cparams(3),
        name="fox_attention" if fox else "mla_attention",
    )(*args)


def _mem_proj_kernel(mem_ref, w_ref, mk_ref, mvt_ref):
    mkv = jnp.dot(mem_ref[...].astype(BF16), w_ref[...], preferred_element_type=F32)
    mk_ref[0] = mkv[:, :MEM_WIDTH].astype(BF16)
    _store_values_transposed(mvt_ref, mkv[:, MEM_WIDTH:])


def _mem_proj(mem2d, w, nb):
    return pl.pallas_call(
        _mem_proj_kernel,
        grid=(nb,),
        in_specs=[pl.BlockSpec((N_MEM, D_MODEL), lambda b: (b, 0)),
                  pl.BlockSpec(w.shape, lambda b: (0, 0))],
        out_specs=[pl.BlockSpec((1, N_MEM, MEM_WIDTH), lambda b: (b, 0, 0)),
                   pl.BlockSpec((1, MEM_HEADS * VT_ROWS, N_MEM), lambda b: (b, 0, 0))],
        out_shape=[jax.ShapeDtypeStruct((nb, N_MEM, MEM_WIDTH), BF16),
                   jax.ShapeDtypeStruct((nb, MEM_HEADS * VT_ROWS, N_MEM), BF16)],
        compiler_params=_cparams(1),
        name="mem_proj",
    )(mem2d, w)


def _mem_attn_kernel(qt_ref, k_ref, vt_ref, o_ref):
    qt = qt_ref[0]
    kc = k_ref[0]
    for pair in range(MEM_HEADS // 2):
        qt_p = qt[pair * LANES:(pair + 1) * LANES]
        k_p = kc[:, pair * LANES:(pair + 1) * LANES]
        rowi = lax.broadcasted_iota(jnp.int32, qt_p.shape, 0)
        zero = jnp.zeros_like(qt_p)
        outs = []
        for h in range(2):
            mine = (rowi < HEAD_DIM) if h == 0 else (rowi >= HEAD_DIM)
            s = jnp.dot(k_p, jnp.where(mine, qt_p, zero), preferred_element_type=F32)
            p = jnp.exp2(s - jnp.max(s, axis=0, keepdims=True)).astype(BF16)
            r0 = (2 * pair + h) * VT_ROWS
            acc = jnp.dot(vt_ref[0, r0:r0 + VT_ROWS, :], p, preferred_element_type=F32)
            outs.append(acc[:HEAD_DIM] * (1.0 / acc[HEAD_DIM:HEAD_DIM + 1]))
        o_ref[0, :, pair * LANES:(pair + 1) * LANES] = (
            jnp.concatenate(outs, axis=0).T.astype(o_ref.dtype))


def _memory_attention(mqt, mk, mvt):
    nb, _, seq = mqt.shape
    tq = ATT_TQ
    return pl.pallas_call(
        _mem_attn_kernel,
        grid=(nb, seq // tq),
        in_specs=[pl.BlockSpec((1, MEM_WIDTH, tq), lambda b, i: (b, 0, i)),
                  pl.BlockSpec((1, N_MEM, MEM_WIDTH), lambda b, i: (b, 0, 0)),
                  pl.BlockSpec((1, MEM_HEADS * VT_ROWS, N_MEM), lambda b, i: (b, 0, 0))],
        out_specs=pl.BlockSpec((1, tq, MEM_WIDTH), lambda b, i: (b, i, 0)),
        out_shape=jax.ShapeDtypeStruct((nb, seq, MEM_WIDTH), BF16),
        compiler_params=_cparams(2),
        name="memory_attention",
    )(mqt, mk, mvt)


def _post_attn_kernel(x_ref, att_ref, memo_ref, wa_ref, wm_ref, g_ref, b_ref, wr_ref, br_ref,
                      x1_ref, x1p_ref, idx_ref, rank_ref, gate_ref, cnt_ref, carry_ref):
    t = pl.program_id(0)

    @pl.when(t == 0)
    def _():
        carry_ref[...] = jnp.zeros_like(carry_ref)

    tm = x_ref.shape[0]
    mix = (jnp.dot(att_ref[...], wa_ref[...], preferred_element_type=F32)
           + jnp.dot(memo_ref[...], wm_ref[...], preferred_element_type=F32))
    x1 = _layer_norm(DEEPNORM_ALPHA * x_ref[...] + mix, g_ref[...], b_ref[...])
    x1_ref[...] = x1
    x1b = x1.astype(BF16)
    x1p_ref[...] = _pack_bf16_pairs(x1b)
    logits = jnp.dot(x1b, wr_ref[...], preferred_element_type=F32) + br_ref[...]
    lane = lax.broadcasted_iota(jnp.int32, (tm, LANES), 1)
    work = jnp.where(lane < N_EXPERTS, logits, -jnp.inf)
    idxs, vals = [], []
    onehot = jnp.zeros((tm, LANES), F32)
    for _ in range(TOP_K):
        best = jnp.max(work, axis=1, keepdims=True)
        where_best = jnp.argmax(work, axis=1, keepdims=True).astype(jnp.int32)
        hit = lane == where_best
        onehot = jnp.where(hit, 1.0, onehot)
        work = jnp.where(hit, -jnp.inf, work)
        idxs.append(where_best)
        vals.append(best)
    exps = [jnp.exp(v - vals[0]) for v in vals]
    denom = exps[0] + exps[1] + exps[2] + exps[3]
    row = lax.broadcasted_iota(jnp.int32, (tm, tm), 0)
    col = lax.broadcasted_iota(jnp.int32, (tm, tm), 1)
    strict = jnp.where(row > col, 1.0, 0.0).astype(BF16)
    before = jnp.dot(strict, onehot.astype(BF16), preferred_element_type=F32) + carry_ref[...]
    idx_out = jnp.zeros((tm, LANES), F32)
    rank_out = jnp.zeros((tm, LANES), F32)
    gate_out = jnp.zeros((tm, LANES), F32)
    for r in range(TOP_K):
        rank_r = jnp.sum(jnp.where(lane == idxs[r], before, 0.0), axis=1, keepdims=True)
        idx_out = jnp.where(lane == r, idxs[r].astype(F32), idx_out)
        rank_out = jnp.where(lane == r, rank_r, rank_out)
        gate_out = jnp.where(lane == r, exps[r] / denom, gate_out)
    idx_ref[...] = idx_out.T[:8].astype(jnp.int32)
    rank_ref[...] = rank_out.T[:8].astype(jnp.int32)
    gate_ref[...] = gate_out
    total = carry_ref[...] + jnp.sum(onehot, axis=0, keepdims=True)
    carry_ref[...] = total
    cnt_ref[...] = jnp.broadcast_to(total, cnt_ref.shape)


def _post_attn(x2d, att, memo, wa, wm, g, b, wr, br):
    t_tokens = x2d.shape[0]
    tm = PROJ_TM
    row_spec = lambda width: pl.BlockSpec((tm, width), lambda i: (i, 0))
    full = lambda a: pl.BlockSpec(a.shape, lambda i: (0, 0))
    return pl.pallas_call(
        _post_attn_kernel,
        grid=(t_tokens // tm,),
        in_specs=[row_spec(D_MODEL), row_spec(att.shape[1]), row_spec(MEM_WIDTH),
                  full(wa), full(wm), full(g), full(b), full(wr), full(br)],
        out_specs=[row_spec(D_MODEL), row_spec(D_MODEL // 2),
                   pl.BlockSpec((8, tm), lambda i: (0, i)), pl.BlockSpec((8, tm), lambda i: (0, i)),
                   row_spec(LANES), pl.BlockSpec((8, LANES), lambda i: (0, 0))],
        out_shape=[jax.ShapeDtypeStruct((t_tokens, D_MODEL), F32),
                   jax.ShapeDtypeStruct((t_tokens, D_MODEL // 2), jnp.uint32),
                   jax.ShapeDtypeStruct((8, t_tokens), jnp.int32),
                   jax.ShapeDtypeStruct((8, t_tokens), jnp.int32),
                   jax.ShapeDtypeStruct((t_tokens, LANES), F32),
                   jax.ShapeDtypeStruct((8, LANES), F32)],
        scratch_shapes=[pltpu.VMEM((1, LANES), F32)],
        compiler_params=_cparams(1),
        name="outproj_ln_router",
    )(x2d, att, memo, wa, wm, g, b, wr, br)


def _expert_kernel(be_ref, nu_ref, br_ref, x_ref, wgu_hbm, bgu_ref, wdn_hbm, bdn_ref, *rest,
                   n_chunks, blk_lo, n_blocks, layer, has_prev):
    o_ref, wgu_st, wdn_st, wgu_bf, wdn_bf, h_ref, sem = rest[1:] if has_prev else rest
    step = pl.program_id(0)
    blk = step + blk_lo
    live_end = jnp.minimum(blk_lo + pl.num_programs(0), nu_ref[0])
    expert = be_ref[blk]

    def fetch(e):
        return (pltpu.make_async_copy(wgu_hbm.at[layer, e], wgu_st, sem.at[0]),
                pltpu.make_async_copy(wdn_hbm.at[layer, e], wdn_st, sem.at[1]))

    @pl.when(((step == 0) | (expert != be_ref[jnp.maximum(blk - 1, 0)])) & (blk < live_end))
    def _():
        @pl.when(step == 0)
        def _():
            for copy in fetch(expert):
                copy.start()

        for copy in fetch(expert):
            copy.wait()
        wgu_bf[...] = wgu_st[...].astype(BF16)
        wdn_bf[...] = wdn_st[...].astype(BF16)
        nxt = lax.while_loop(
            lambda j: (j < live_end) & (be_ref[jnp.minimum(j, n_blocks - 1)] == expert),
            lambda j: j + 1, blk + 1)

        @pl.when(nxt < live_end)
        def _():
            for copy in fetch(be_ref[jnp.minimum(nxt, n_blocks - 1)]):
                copy.start()

    def mlp(rows):
        x = _unpack_bf16_pairs(x_ref[:rows]).astype(BF16)
        cw = D_EXPERT // n_chunks
        for c in range(n_chunks):
            gs = slice(c * cw, (c + 1) * cw)
            us = slice(D_EXPERT + c * cw, D_EXPERT + (c + 1) * cw)
            g = jnp.dot(x, wgu_bf[:, gs], preferred_element_type=F32) + bgu_ref[0, 0, :, gs]
            u = jnp.dot(x, wgu_bf[:, us], preferred_element_type=F32) + bgu_ref[0, 0, :, us]
            g = jnp.minimum(g, SWIGLU_LIMIT)
            u = jnp.clip(u, -SWIGLU_LIMIT, SWIGLU_LIMIT)
            h_ref[:rows, gs] = ((u + 1.0) * (g * jax.nn.sigmoid(SWIGLU_ALPHA * g))).astype(BF16)
        y = jnp.dot(h_ref[:rows], wdn_bf[...], preferred_element_type=F32) + bdn_ref[0, 0]
        return _pack_bf16_pairs(y.astype(BF16))

    tm = o_ref.shape[0]
    live_rows = br_ref[blk]

    @pl.when(live_rows > tm // 2)
    def _():
        o_ref[...] = mlp(tm)

    @pl.when((live_rows > 0) & (live_rows <= tm // 2))
    def _():
        o_ref[:tm // 2] = mlp(tm // 2)
        o_ref[tm // 2:] = jnp.zeros((tm - tm // 2, o_ref.shape[1]), o_ref.dtype)

    @pl.when(live_rows == 0)
    def _():
        o_ref[...] = jnp.zeros_like(o_ref)


def _experts(block_expert, n_used, block_rows, xb, blk_lo, p_rows, y_prev, layer, wgu, bgu, wdn, bdn):
    tm = MOE_TM
    has_prev = y_prev is not None
    b_map = lambda i, be, nu, br: (layer, be[i + blk_lo], 0, 0)
    in_specs = [pl.BlockSpec((tm, D_MODEL // 2), lambda i, be, nu, br: (i, 0)),
                pl.BlockSpec(memory_space=pl.ANY),
                pl.BlockSpec((1, 1, 1, 2 * D_EXPERT), b_map),
                pl.BlockSpec(memory_space=pl.ANY),
                pl.BlockSpec((1, 1, 1, D_MODEL), b_map)]
    args = [block_expert, n_used, block_rows, xb, wgu, bgu, wdn, bdn]
    if has_prev:
        in_specs.append(pl.BlockSpec(memory_space=pl.ANY))
        args.append(y_prev)
    grid_spec = pltpu.PrefetchScalarGridSpec(
        num_scalar_prefetch=3,
        grid=(xb.shape[0] // tm,),
        in_specs=in_specs,
        out_specs=pl.BlockSpec((tm, D_MODEL // 2), lambda i, be, nu, br: (i + blk_lo, 0)),
        scratch_shapes=[pltpu.VMEM((D_MODEL, 2 * D_EXPERT), F32),
                        pltpu.VMEM((D_EXPERT, D_MODEL), F32),
                        pltpu.VMEM((D_MODEL, 2 * D_EXPERT), BF16),
                        pltpu.VMEM((D_EXPERT, D_MODEL), BF16),
                        pltpu.VMEM((tm, D_EXPERT), BF16),
                        pltpu.SemaphoreType.DMA((2,))],
    )
    return pl.pallas_call(
        functools.partial(_expert_kernel, n_chunks=4, blk_lo=blk_lo, n_blocks=p_rows // tm,
                          layer=layer, has_prev=has_prev),
        grid_spec=grid_spec,
        out_shape=jax.ShapeDtypeStruct((p_rows, D_MODEL // 2), jnp.uint32),
        input_output_aliases={len(args) - 1: 0} if has_prev else {},
        compiler_params=_cparams(1),
        name="experts",
    )(*args)


def _sc_move_rows(table_hbm, idx_v, out_hbm, base, n_chunks, rows_v, gsem, wsem):
    def gather(c, slot):
        off = pl.multiple_of(c * SC_CHUNK, SC_CHUNK)
        return pltpu.make_async_copy(table_hbm.at[idx_v.at[pl.ds(off, SC_CHUNK)]],
                                     rows_v.at[slot], gsem.at[slot])

    def put(c, slot):
        off = pl.multiple_of(c * SC_CHUNK, SC_CHUNK)
        return pltpu.make_async_copy(rows_v.at[slot], out_hbm.at[pl.ds(base + off, SC_CHUNK)],
                                     wsem.at[slot])

    gather(0, 0).start()
    gather(1, 1).start()

    @pl.loop(0, n_chunks, step=2)
    def _(c):
        for slot in range(2):
            gather(c + slot, slot).wait()
            put(c + slot, slot).start()
        for slot in range(2):
            put(c + slot, slot).wait()

            @pl.when(c + 2 + slot < n_chunks)
            def _():
                gather(c + 2 + slot, slot).start()


def _sc_gather_rows(table, idx):
    n_idx = idx.shape[0]
    width = table.shape[1]
    per_worker = n_idx // SC_WORKERS
    n_chunks = per_worker // SC_CHUNK
    mesh = plsc.VectorSubcoreMesh(core_axis_name="c", subcore_axis_name="s",
                                  num_cores=SC_CORES, num_subcores=SC_SUBCORES)

    def body(table_hbm, idx_hbm, out_hbm, idx_v, rows_v, gsem, wsem):
        wid = lax.axis_index("s") * SC_CORES + lax.axis_index("c")
        base = wid * per_worker
        pltpu.sync_copy(idx_hbm.at[pl.ds(base, per_worker)], idx_v)
        _sc_move_rows(table_hbm, idx_v, out_hbm, base, n_chunks, rows_v, gsem, wsem)

    return pl.kernel(
        body,
        out_type=jax.ShapeDtypeStruct((n_idx, width), table.dtype),
        mesh=mesh,
        scratch_types=[pltpu.VMEM((per_worker,), jnp.int32),
                       pltpu.VMEM((2, SC_CHUNK, width), table.dtype),
                       pltpu.SemaphoreType.DMA((2,)),
                       pltpu.SemaphoreType.DMA((2,))],
        name="sc_gather_rows",
    )(table, idx)


def _sc_dispatch_rows(table, dest, slot_lo, n_slots):
    n_tok, width = table.shape
    n_pairs = dest.shape[0]
    per_worker = n_slots // SC_WORKERS
    n_chunks = per_worker // SC_CHUNK
    n_scan = n_pairs // SC_SCAN
    mesh = plsc.VectorSubcoreMesh(core_axis_name="c", subcore_axis_name="s",
                                  num_cores=SC_CORES, num_subcores=SC_SUBCORES)

    def body(table_hbm, dest_hbm, out_hbm, tok_v, dest_v, rows_v, gsem, wsem):
        wid = lax.axis_index("s") * SC_CORES + lax.axis_index("c")
        base = wid * per_worker
        first = slot_lo + base
        lane = lax.iota(jnp.int32, SC_LANES)

        @pl.loop(0, per_worker // SC_LANES)
        def _(i):
            off = pl.multiple_of(i * SC_LANES, SC_LANES)
            tok_v[pl.ds(off, SC_LANES)] = lax.rem(first + off + lane, n_tok)

        @pl.loop(0, n_scan)
        def _(g):
            goff = pl.multiple_of(g * SC_SCAN, SC_SCAN)
            pltpu.sync_copy(dest_hbm.at[pl.ds(goff, SC_SCAN)], dest_v)

            @plsc.parallel_loop(0, SC_SCAN // SC_LANES, unroll=8)
            def _(i):
                off = pl.multiple_of(i * SC_LANES, SC_LANES)
                local = dest_v[pl.ds(off, SC_LANES)] - first
                mine = (local >= 0) & (local < per_worker)
                pair = goff + off + lane
                plsc.store_scatter(tok_v, [jnp.where(mine, local, 0)], lax.rem(pair, n_tok), mask=mine)

        _sc_move_rows(table_hbm, tok_v, out_hbm, base, n_chunks, rows_v, gsem, wsem)

    return pl.kernel(
        body,
        out_type=jax.ShapeDtypeStruct((n_slots, width), table.dtype),
        mesh=mesh,
        scratch_types=[pltpu.VMEM((per_worker,), jnp.int32),
                       pltpu.VMEM((SC_SCAN,), jnp.int32),
                       pltpu.VMEM((2, SC_CHUNK, width), table.dtype),
                       pltpu.SemaphoreType.DMA((2,)),
                       pltpu.SemaphoreType.DMA((2,))],
        compiler_params=pltpu.CompilerParams(needs_layout_passes=False),
        name="sc_dispatch_rows",
    )(table, dest)


def _combine_kernel(x1_ref, yg_ref, gate_ref, g_ref, b_ref, *rest):
    o_ref = rest[-1]
    gates = gate_ref[...]
    ffn = _unpack_bf16_pairs(yg_ref[0]) * gates[:, 0:1]
    for r in range(1, TOP_K):
        ffn = ffn + _unpack_bf16_pairs(yg_ref[r]) * gates[:, r:r + 1]
    o_ref[...] = _layer_norm(DEEPNORM_ALPHA * x1_ref[...] + ffn, g_ref[...], b_ref[...])


def _combine(x1, yg, gates, g, b, tok_lo, out_prev):
    t_tokens = x1.shape[0]
    tm = COMBINE_TM
    tile_lo = tok_lo // tm
    in_specs = [pl.BlockSpec((tm, D_MODEL), lambda i: (i + tile_lo, 0)),
                pl.BlockSpec((TOP_K, tm, D_MODEL // 2), lambda i: (0, i, 0)),
                pl.BlockSpec((tm, LANES), lambda i: (i + tile_lo, 0)),
                pl.BlockSpec((1, D_MODEL), lambda i: (0, 0)),
                pl.BlockSpec((1, D_MODEL), lambda i: (0, 0))]
    args = [x1, yg, gates, g, b]
    if out_prev is not None:
        in_specs.append(pl.BlockSpec(memory_space=pl.ANY))
        args.append(out_prev)
    return pl.pallas_call(
        _combine_kernel,
        grid=(yg.shape[1] // tm,),
        in_specs=in_specs,
        out_specs=pl.BlockSpec((tm, D_MODEL), lambda i: (i + tile_lo, 0)),
        out_shape=jax.ShapeDtypeStruct((t_tokens, D_MODEL), F32),
        input_output_aliases={len(args) - 1: 0} if out_prev is not None else {},
        compiler_params=_cparams(1),
        name="combine_ln",
    )(*args)


def _moe(x1, x1p, idx, rank, gates, counts, layer, wgu, bgu, wdn, bdn, g, b):
    t_tokens = x1.shape[0]
    tm = MOE_TM
    counts = counts[0, :N_EXPERTS].astype(jnp.int32)
    padded = ((counts + tm - 1) // tm) * tm
    pend = jnp.cumsum(padded)
    pstart = pend - padded
    dest = rank[:TOP_K]
    for e in range(N_EXPERTS):
        dest = dest + jnp.where(idx[:TOP_K] == e, pstart[e], 0)
    n_blocks = (t_tokens * TOP_K) // tm + N_EXPERTS
    p_rows = n_blocks * tm
    block_start = jnp.arange(n_blocks, dtype=jnp.int32) * tm
    block_expert = jnp.minimum(
        jnp.sum((pend[None, :] <= block_start[:, None]).astype(jnp.int32), axis=1), N_EXPERTS - 1)
    n_used = (pend[-1:] // tm).astype(jnp.int32)
    mine = block_expert[:, None] == jnp.arange(N_EXPERTS, dtype=jnp.int32)[None, :]
    live_end = jnp.sum(jnp.where(mine, (pstart + counts)[None, :], 0), axis=1)
    block_rows = jnp.clip(live_end - block_start, 0, tm)
    y = None
    for lo, hi in ((0, n_blocks // MOE_SPLIT), (n_blocks // MOE_SPLIT, n_blocks)):
        xb = _sc_dispatch_rows(x1p, dest.reshape(-1), lo * tm, (hi - lo) * tm)
        y = _experts(block_expert, n_used, block_rows, xb, lo, p_rows, y, layer, wgu, bgu, wdn, bdn)
    out = None
    for lo, hi in ((0, t_tokens // MOE_SPLIT), (t_tokens // MOE_SPLIT, t_tokens)):
        yg = _sc_gather_rows(y, dest[:, lo:hi].reshape(-1)).reshape(TOP_K, hi - lo, D_MODEL // 2)
        out = _combine(x1, yg, gates, g, b, lo, out)
    return out


def _row(v, width=None):
    v = v.astype(F32).reshape(1, -1)
    if width is not None and v.shape[1] < width:
        v = jnp.pad(v, ((0, 0), (0, width - v.shape[1])))
    return v


def _pad_cols(w, width):
    return jnp.pad(w, ((0, 0), (0, width - w.shape[1])))


def kernel(x, mem, positions, a_w_in, a_b_f, a_w_out, b_w_in, b_g_q, b_w_uq, b_w_out,
           kv_w_dkv, kv_g, kv_w_ukv, mem_w_kv, ln_g, ln_b,
           moe_w_r, moe_b_r, moe_w_gu, moe_b_gu, moe_w_dn, moe_b_dn):
    nb, seq, d = x.shape
    t_tokens = nb * seq
    n_a = a_w_in.shape[0]
    x2d = x.reshape(t_tokens, d)
    mem2d = mem.reshape(nb * N_MEM, d)
    pos2d = positions.reshape(1, t_tokens)
    half = QK_ROPE // 2
    invf = (ROPE_THETA ** (-jnp.arange(half, dtype=F32) * 2.0 / QK_ROPE)).reshape(half, 1)

    shared_kv = None
    for l in range(DEPTH):
        mk, mvt = _mem_proj(mem2d, mem_w_kv[l].astype(BF16), nb)
        if l < n_a:
            w_in = a_w_in[l]
            w = jnp.concatenate([w_in[:, :3 * FOX_WIDTH],
                                 _pad_cols(w_in[:, 3 * FOX_WIDTH:3 * FOX_WIDTH + FOX_HEADS], LANES),
                                 w_in[:, 3 * FOX_WIDTH + FOX_HEADS:]], axis=1).astype(BF16)
            qt, k, vt, mqt, cum, cumt, cfirst, clast, knorm = _fox_proj(
                x2d, w, _row(a_b_f[l], LANES), seq)
            stats = [s.reshape(-1, LANES) for s in (cfirst, clast, knorm)]
            att = _causal_attention(qt, k.reshape(nb, seq, -1), vt,
                                    [cum.reshape(nb, seq, LANES), cumt] + stats, slab=False)
            w_out = a_w_out[l]
        else:
            bl = l - n_a
            w = jnp.concatenate([b_w_in[bl], kv_w_dkv[:, :KV_LORA],
                                 jnp.zeros((d, QK_NOPE), F32), kv_w_dkv[:, KV_LORA:],
                                 jnp.zeros((d, LANES - QK_NOPE - QK_ROPE), F32)], axis=1).astype(BF16)
            wuq = b_w_uq[bl].reshape(Q_LORA, MLA_HEADS, QK_NOPE + QK_ROPE)
            wuq = jnp.pad(wuq, ((0, 0), (0, 0), (0, MLA_SLAB - QK_NOPE - QK_ROPE)))
            wuq = wuq.reshape(Q_LORA, MLA_HEADS * MLA_SLAB).astype(BF16)
            wukv = kv_w_ukv.reshape(KV_LORA, MLA_HEADS, QK_NOPE + V_DIM)
            wk = jnp.pad(wukv[:, :, :QK_NOPE], ((0, 0), (0, 0), (0, MLA_SLAB - QK_NOPE)))
            wk = wk.reshape(KV_LORA, MLA_HEADS * MLA_SLAB).astype(BF16)
            wv = wukv[:, :, QK_NOPE:].reshape(KV_LORA, MLA_V_WIDTH).astype(BF16)
            qt, mqt, k_new, vt_new = _mla_proj(x2d, pos2d, invf, w, _row(b_g_q[bl]), wuq,
                                              _row(kv_g), wk, wv, seq)
            if shared_kv is None:
                shared_kv = (k_new.reshape(nb, seq, -1), vt_new)
            att = _causal_attention(qt, shared_kv[0], shared_kv[1], None, slab=True)
            w_out = b_w_out[bl]
        memo = _memory_attention(mqt, mk, mvt)
        n_att = w_out.shape[0] - MEM_WIDTH
        x1, x1p, idx, rank, gates, counts = _post_attn(
            x2d, att.reshape(t_tokens, -1), memo.reshape(t_tokens, MEM_WIDTH),
            w_out[:n_att].astype(BF16), w_out[n_att:].astype(BF16),
            _row(ln_g[l, 0]), _row(ln_b[l, 0]),
            _pad_cols(moe_w_r[l], LANES).astype(BF16), _row(moe_b_r[l], LANES))
        x2d = _moe(x1, x1p, idx, rank, gates, counts, l,
                   moe_w_gu, moe_b_gu.reshape(DEPTH, N_EXPERTS, 1, -1),
                   moe_w_dn, moe_b_dn.reshape(DEPTH, N_EXPERTS, 1, -1),
                   _row(ln_g[l, 1]), _row(ln_b[l, 1]))
    return x2d.reshape(nb, seq, d)
```

```python
import functools
import math

import jax
import jax.numpy as jnp
from jax import lax
from jax.experimental import pallas as pl
from jax.experimental.pallas import tpu as pltpu
from jax.experimental.pallas import tpu_sc as plsc

F32 = jnp.float32
BF16 = jnp.bfloat16

D_MODEL = 1024
DEPTH = 2
N_MEM = 256
HEAD_DIM = 64
FOX_HEADS = 12
MEM_HEADS = 4
MLA_HEADS = 12
Q_LORA = 384
KV_LORA = 256
QK_NOPE = 64
QK_ROPE = 32
V_DIM = 64
ROPE_THETA = 10000.0
N_EXPERTS = 32
TOP_K = 4
D_EXPERT = D_MODEL
SWIGLU_LIMIT = 7.0
SWIGLU_ALPHA = 1.702
LN_EPS = 1e-5
RMS_EPS = 1e-6
NEG_INF = -1e30
DEEPNORM_ALPHA = (2 * DEPTH) ** 0.25
FOX_WIDTH = FOX_HEADS * HEAD_DIM
MEM_WIDTH = MEM_HEADS * HEAD_DIM
MLA_V_WIDTH = MLA_HEADS * V_DIM

LANES = 128
LOG2E = math.log2(math.e)
VMEM_LIMIT = 48 * 1024 * 1024

PROJ_TM = 512
ATT_TQ = 512
FOX_TQ = 512
MOE_TM = 512
COMBINE_TM = 512
MOE_SPLIT = 4
MLA_SLAB = LANES
VT_ROWS = HEAD_DIM + 16
SC_CORES = 2
SC_SUBCORES = 16
SC_WORKERS = SC_CORES * SC_SUBCORES
SC_CHUNK = 64
SC_LANES = 16
SC_SCAN = 8192


def _cparams(n_axes):
    return pltpu.CompilerParams(dimension_semantics=("arbitrary",) * n_axes,
                                vmem_limit_bytes=VMEM_LIMIT)


def _split3(x):
    hi = x.astype(BF16)
    r1 = x - hi.astype(F32)
    mid = r1.astype(BF16)
    lo = (r1 - mid.astype(F32)).astype(BF16)
    return hi, mid, lo


def _pack_bf16_pairs(v):
    bits = pltpu.bitcast(v.astype(F32), jnp.uint32)
    half = v.shape[1] // 2
    return (bits[:, :half] >> 16) | bits[:, half:]


def _unpack_bf16_pairs(words):
    return jnp.concatenate([pltpu.bitcast(words << 16, F32),
                            pltpu.bitcast(words & jnp.uint32(0xFFFF0000), F32)], axis=1)


def _layer_norm(y, g, b):
    mu = jnp.mean(y, axis=-1, keepdims=True)
    yc = y - mu
    var = jnp.mean(yc * yc, axis=-1, keepdims=True)
    return yc * lax.rsqrt(var + LN_EPS) * g + b


def _rms_norm(y, g):
    return y * lax.rsqrt(jnp.mean(y * y, axis=-1, keepdims=True) + RMS_EPS) * g


def _store_transposed(dst_ref, val):
    for s in range(val.shape[1] // LANES):
        sl = slice(s * LANES, (s + 1) * LANES)
        dst_ref[0, sl, :] = val[:, sl].T.astype(dst_ref.dtype)


def _store_values_transposed(vt_ref, val):
    tm = val.shape[0]
    ones = jnp.ones((VT_ROWS - HEAD_DIM, tm), vt_ref.dtype)
    for s in range(val.shape[1] // LANES):
        pair_t = val[:, s * LANES:(s + 1) * LANES].T.astype(vt_ref.dtype)
        for h in range(2):
            r0 = (2 * s + h) * VT_ROWS
            vt_ref[0, r0:r0 + HEAD_DIM, :] = pair_t[h * HEAD_DIM:(h + 1) * HEAD_DIM]
            vt_ref[0, r0 + HEAD_DIM:r0 + VT_ROWS, :] = ones


def _fox_proj_kernel(x_ref, w_ref, bf_ref, seg_ref, qt_ref, k_ref, vt_ref, mqt_ref, cum_ref,
                     cumt_ref, cfirst_ref, clast_ref, knorm_ref, carry_ref, knmax_ref, *,
                     tiles_per_batch):
    t = pl.program_id(0)

    @pl.when(t % tiles_per_batch == 0)
    def _():
        carry_ref[...] = jnp.zeros_like(carry_ref)
        knmax_ref[...] = jnp.zeros_like(knmax_ref)

    tm = x_ref.shape[0]
    proj = jnp.dot(x_ref[...].astype(BF16), w_ref[...], preferred_element_type=F32)
    qscale = HEAD_DIM ** -0.5 * LOG2E
    _store_transposed(qt_ref, proj[:, :FOX_WIDTH] * qscale)
    kb = proj[:, FOX_WIDTH:2 * FOX_WIDTH].astype(BF16)
    k_ref[...] = kb
    _store_values_transposed(vt_ref, proj[:, 2 * FOX_WIDTH:3 * FOX_WIDTH])
    kf = kb.astype(F32)
    ksq = jnp.dot((kf * kf).astype(BF16), seg_ref[...], preferred_element_type=F32)
    n_sub = tm // FOX_TQ
    for sub in range(n_sub):
        tile_max = jnp.sqrt(jnp.max(ksq[sub * FOX_TQ:(sub + 1) * FOX_TQ], axis=0, keepdims=True))
        knmax_ref[...] = jnp.maximum(knmax_ref[...], tile_max)
        knorm_ref[sub] = knmax_ref[...]
    f = proj[:, 3 * FOX_WIDTH:3 * FOX_WIDTH + LANES] + bf_ref[...]
    _store_transposed(mqt_ref, proj[:, 3 * FOX_WIDTH + LANES:] * qscale)
    log_f = jnp.minimum(f, 0.0) - jnp.log1p(jnp.exp(-jnp.abs(f)))
    row = lax.broadcasted_iota(jnp.int32, (tm, tm), 0)
    col = lax.broadcasted_iota(jnp.int32, (tm, tm), 1)
    tri = jnp.where(row >= col, 1.0, 0.0).astype(BF16)
    hi, mid, lo = _split3(log_f)
    cum = (jnp.dot(tri, hi, preferred_element_type=F32)
           + jnp.dot(tri, mid, preferred_element_type=F32)
           + jnp.dot(tri, lo, preferred_element_type=F32)) + carry_ref[...]
    carry_ref[...] = cum[tm - 1:tm, :]
    cum2 = cum * LOG2E
    cum_ref[...] = cum2
    cumt_ref[0] = cum2.T[:16, :]
    for sub in range(n_sub):
        cfirst_ref[sub] = cum2[sub * FOX_TQ:sub * FOX_TQ + 1, :]
        clast_ref[sub] = cum2[(sub + 1) * FOX_TQ - 1:(sub + 1) * FOX_TQ, :]


def _fox_proj(x2d, w, bf, seq):
    t_tokens = x2d.shape[0]
    tm = PROJ_TM
    nb = t_tokens // seq
    n = w.shape[1]
    tiles_per_batch = seq // tm
    n_tiles = t_tokens // tm
    seg = (jnp.arange(FOX_WIDTH)[:, None] // HEAD_DIM == jnp.arange(LANES)[None, :]).astype(BF16)
    row_spec = lambda width: pl.BlockSpec((tm, width), lambda i: (i, 0))
    t_spec = lambda rows: pl.BlockSpec(
        (1, rows, tm), lambda i: (i // tiles_per_batch, 0, i % tiles_per_batch))
    n_sub = tm // FOX_TQ
    stat_spec = pl.BlockSpec((n_sub, 1, LANES), lambda i: (i, 0, 0))
    stat_shape = jax.ShapeDtypeStruct((n_tiles * n_sub, 1, LANES), F32)
    return pl.pallas_call(
        functools.partial(_fox_proj_kernel, tiles_per_batch=tiles_per_batch),
        grid=(n_tiles,),
        in_specs=[row_spec(D_MODEL),
                  pl.BlockSpec((D_MODEL, n), lambda i: (0, 0)),
                  pl.BlockSpec((1, LANES), lambda i: (0, 0)),
                  pl.BlockSpec((FOX_WIDTH, LANES), lambda i: (0, 0))],
        out_specs=[t_spec(FOX_WIDTH), row_spec(FOX_WIDTH), t_spec(FOX_HEADS * VT_ROWS),
                   t_spec(MEM_WIDTH), row_spec(LANES), t_spec(16),
                   stat_spec, stat_spec, stat_spec],
        out_shape=[jax.ShapeDtypeStruct((nb, FOX_WIDTH, seq), BF16),
                   jax.ShapeDtypeStruct((t_tokens, FOX_WIDTH), BF16),
                   jax.ShapeDtypeStruct((nb, FOX_HEADS * VT_ROWS, seq), BF16),
                   jax.ShapeDtypeStruct((nb, MEM_WIDTH, seq), BF16),
                   jax.ShapeDtypeStruct((t_tokens, LANES), F32),
                   jax.ShapeDtypeStruct((nb, 16, seq), F32),
                   stat_shape, stat_shape, stat_shape],
        scratch_shapes=[pltpu.VMEM((1, LANES), F32), pltpu.VMEM((1, LANES), F32)],
        compiler_params=_cparams(1),
        name="fox_proj",
    )(x2d, w, bf, seg)


def _rope_slab_t(slab_t, cos_t, sin_t):
    half = QK_ROPE // 2
    x1 = slab_t[QK_NOPE:QK_NOPE + half]
    x2 = slab_t[QK_NOPE + half:QK_NOPE + QK_ROPE]
    return jnp.concatenate([slab_t[:QK_NOPE], x1 * cos_t - x2 * sin_t, x1 * sin_t + x2 * cos_t,
                            slab_t[QK_NOPE + QK_ROPE:]], axis=0)


def _mla_proj_kernel(x_ref, pos_ref, invf_ref, w_ref, gq_ref, wuq_ref, gkv_ref, wk_ref, wv_ref,
                     qt_ref, mqt_ref, k_ref, vt_ref):
    proj = jnp.dot(x_ref[...].astype(BF16), w_ref[...], preferred_element_type=F32)
    ang_t = invf_ref[...] * pos_ref[...].astype(F32)
    cos_t = jnp.cos(ang_t)
    sin_t = jnp.sin(ang_t)
    c_q = _rms_norm(proj[:, :Q_LORA], gq_ref[...])
    _store_transposed(mqt_ref, proj[:, Q_LORA:Q_LORA + MEM_WIDTH] * (HEAD_DIM ** -0.5 * LOG2E))
    kv_off = Q_LORA + MEM_WIDTH
    c_kv = _rms_norm(proj[:, kv_off:kv_off + KV_LORA], gkv_ref[...])
    kr = _rope_slab_t(proj[:, kv_off + KV_LORA:].T, cos_t, sin_t).T
    q = jnp.dot(c_q.astype(BF16), wuq_ref[...], preferred_element_type=F32)
    kn = jnp.dot(c_kv.astype(BF16), wk_ref[...], preferred_element_type=F32)
    qscale = (QK_NOPE + QK_ROPE) ** -0.5 * LOG2E
    for h in range(MLA_HEADS):
        sl = slice(h * MLA_SLAB, (h + 1) * MLA_SLAB)
        qt_ref[0, sl, :] = (_rope_slab_t(q[:, sl].T, cos_t, sin_t) * qscale).astype(BF16)
        k_ref[:, sl] = (kn[:, sl] + kr).astype(BF16)
    _store_values_transposed(vt_ref, jnp.dot(c_kv.astype(BF16), wv_ref[...],
                                             preferred_element_type=F32))


def _mla_proj(x2d, pos2d, invf, w, gq, wuq, gkv, wk, wv, seq):
    t_tokens = x2d.shape[0]
    tm = PROJ_TM
    nb = t_tokens // seq
    tiles_per_batch = seq // tm
    row_spec = lambda width: pl.BlockSpec((tm, width), lambda i: (i, 0))
    t_spec = lambda rows: pl.BlockSpec(
        (1, rows, tm), lambda i: (i // tiles_per_batch, 0, i % tiles_per_batch))
    full = lambda a: pl.BlockSpec(a.shape, lambda i: (0, 0))
    slabs = MLA_HEADS * MLA_SLAB
    return pl.pallas_call(
        _mla_proj_kernel,
        grid=(t_tokens // tm,),
        in_specs=[row_spec(D_MODEL), pl.BlockSpec((1, tm), lambda i: (0, i)), full(invf), full(w),
                  full(gq), full(wuq), full(gkv), full(wk), full(wv)],
        out_specs=[t_spec(slabs), t_spec(MEM_WIDTH), row_spec(slabs),
                   t_spec(MLA_HEADS * VT_ROWS)],
        out_shape=[jax.ShapeDtypeStruct((nb, slabs, seq), BF16),
                   jax.ShapeDtypeStruct((nb, MEM_WIDTH, seq), BF16),
                   jax.ShapeDtypeStruct((t_tokens, slabs), BF16),
                   jax.ShapeDtypeStruct((nb, MLA_HEADS * VT_ROWS, seq), BF16)],
        compiler_params=_cparams(1),
        name="mla_proj",
    )(x2d, pos2d, invf, w, gq, wuq, gkv, wk, wv)


SKIP_LOG2 = -160.0
NORM_SLACK = 1.02


def _causal_attn_kernel(*refs, tq, n_tiles, fox, slab, long_body):
    if fox:
        (cf_ref, cl_ref, kn_ref, qt_ref, k_ref, vt_ref, cq_ref, ck_ref,
         o_ref, m_ref, acc_ref, s_ref, cmax_ref) = refs
    else:
        qt_ref, k_ref, vt_ref, o_ref, m_ref, acc_ref, s_ref, cmax_ref = refs
    b = pl.program_id(0)
    pair = pl.program_id(1)
    i = pl.program_id(2)
    m_ref[...] = jnp.full(m_ref.shape, NEG_INF, F32)
    acc_ref[...] = jnp.zeros(acc_ref.shape, F32)
    qt = qt_ref[0]
    if slab:
        qth = [qt[:MLA_SLAB], qt[MLA_SLAB:]]
    else:
        rowi = lax.broadcasted_iota(jnp.int32, qt.shape, 0)
        zero = jnp.zeros_like(qt)
        qth = [jnp.where(rowi < HEAD_DIM, qt, zero), jnp.where(rowi >= HEAD_DIM, qt, zero)]
    if fox:
        lane = lax.broadcasted_iota(jnp.int32, (1, LANES), 1)
        cq = [cq_ref[0, pl.ds(2 * pair + h, 1), :] for h in range(2)]

    def logits_to(slot, j):
        off = pl.multiple_of(j * tq, tq)
        kc = k_ref[0, pl.ds(off, tq), :]
        for h in range(2):
            kh = kc[:, h * MLA_SLAB:(h + 1) * MLA_SLAB] if slab else kc
            s = jnp.dot(kh, qth[h], preferred_element_type=F32)
            if fox:
                ck_blk = ck_ref[0, pl.ds(off, tq), :]
                ck = jnp.sum(jnp.where(lane == 2 * pair + h, ck_blk, 0.0), axis=1, keepdims=True)
                s = s + cq[h] - ck
            s_ref[slot, h] = s
            cmax_ref[slot, h] = jnp.max(s, axis=0, keepdims=True)

    def softmax_pv(slot, j, causal_mask):
        off = pl.multiple_of(j * tq, tq)
        for h in range(2):
            s = s_ref[slot, h]
            if causal_mask is not None:
                s = jnp.where(causal_mask, s, NEG_INF)
                chunk_max = jnp.max(s, axis=0, keepdims=True)
            else:
                chunk_max = cmax_ref[slot, h]
            m_prev = m_ref[h]
            m_new = jnp.maximum(m_prev, chunk_max)
            alpha = jnp.exp2(m_prev - m_new)
            p = jnp.exp2(s - m_new).astype(BF16)
            vth = vt_ref[0, h * VT_ROWS:(h + 1) * VT_ROWS, pl.ds(off, tq)]
            acc_ref[h] = acc_ref[h] * alpha + jnp.dot(vth, p, preferred_element_type=F32)
            m_ref[h] = m_new

    if fox:
        base = b * n_tiles
        qn = []
        for h in range(2):
            qf = qth[h].astype(F32)
            qn.append(jnp.sqrt(jnp.max(jnp.sum(qf * qf, axis=0, keepdims=True))) * NORM_SLACK)

        top = [2.0 * qn[h] * kn_ref[base + i, 2 * pair + h] + cf_ref[base + i, 2 * pair + h]
               for h in range(2)]

        def live(j):
            ub = [top[h] - cl_ref[base + jnp.maximum(j, 0), 2 * pair + h] for h in range(2)]
            return (j >= 0) & (jnp.maximum(ub[0], ub[1]) > SKIP_LOG2)

        j0 = lax.while_loop(live, lambda j: j - 1, i - 1) + 1
    else:
        j0 = 0
    n_before = i - j0

    def causal_mask():
        key = lax.broadcasted_iota(jnp.int32, (tq, tq), 0)
        qry = lax.broadcasted_iota(jnp.int32, (tq, tq), 1)
        return key <= qry

    def pipelined():
        odd = (n_before % 2) == 1

        @pl.when(odd)
        def _():
            logits_to(1, j0)
            logits_to(0, j0 + 1)
            softmax_pv(1, j0, None)

        @pl.when(jnp.logical_not(odd))
        def _():
            logits_to(0, j0)

        def two_steps(j):
            logits_to(1, j + 1)
            softmax_pv(0, j, None)
            logits_to(0, j + 2)
            softmax_pv(1, j + 1, None)

        j_even = j0 + (n_before % 2)
        n_two = n_before // 2
        if long_body:
            @pl.when((n_two % 2) == 1)
            def _():
                two_steps(j_even)

            j_quad = j_even + 2 * (n_two % 2)

            def body(t, carry):
                two_steps(j_quad + 4 * t)
                two_steps(j_quad + 4 * t + 2)
                return carry

            lax.fori_loop(0, n_two // 2, body, 0)
        else:
            def body(t, carry):
                two_steps(j_even + 2 * t)
                return carry

            lax.fori_loop(0, n_two, body, 0)
        softmax_pv(0, i, causal_mask())

    if fox:
        @pl.when(n_before == 1)
        def _():
            logits_to(1, j0)
            logits_to(0, i)
            softmax_pv(1, j0, None)
            softmax_pv(0, i, causal_mask())

        @pl.when(n_before != 1)
        def _():
            pipelined()
    else:
        pipelined()
    out_t = jnp.concatenate(
        [acc_ref[h, :HEAD_DIM] * (1.0 / acc_ref[h, HEAD_DIM:HEAD_DIM + 1]) for h in range(2)],
        axis=0)
    o_ref[0] = out_t.T.astype(o_ref.dtype)


def _causal_attention(qt, k, vt, fox_args, *, slab):
    nb, seq, _ = k.shape
    fox = fox_args is not None
    tq = FOX_TQ if fox else ATT_TQ
    n_tiles = seq // tq
    n_pairs = FOX_HEADS // 2
    rows = 2 * MLA_SLAB if slab else LANES
    in_specs = [pl.BlockSpec((1, rows, tq), lambda b, p, i: (b, p, i)),
                pl.BlockSpec((1, seq, rows), lambda b, p, i: (b, 0, p)),
                pl.BlockSpec((1, 2 * VT_ROWS, seq), lambda b, p, i: (b, p, 0))]
    args = [qt, k, vt]
    if fox:
        cum, cumt, cfirst, clast, knorm = fox_args
        smem = pl.BlockSpec(memory_space=pltpu.SMEM)
        in_specs = [smem, smem, smem] + in_specs + [
            pl.BlockSpec((1, 16, tq), lambda b, p, i: (b, 0, i)),
            pl.BlockSpec((1, seq, LANES), lambda b, p, i: (b, 0, 0))]
        args = [cfirst, clast, knorm] + args + [cumt, cum]
    return pl.pallas_call(
        functools.partial(_causal_attn_kernel, tq=tq, n_tiles=n_tiles, fox=fox, slab=slab,
                          long_body=not fox),
        grid=(nb, n_pairs, n_tiles),
        in_specs=in_specs,
        out_specs=pl.BlockSpec((1, tq, LANES), lambda b, p, i: (b, i, p)),
        out_shape=jax.ShapeDtypeStruct((nb, seq, n_pairs * LANES), BF16),
        scratch_shapes=[pltpu.VMEM((2, 1, tq), F32), pltpu.VMEM((2, VT_ROWS, tq), F32),
                        pltpu.VMEM((2, 2, tq, tq), F32), pltpu.VMEM((2, 2, 1, tq), F32)],
        compiler_params=_cparams(3),
        name="fox_attention" if fox else "mla_attention",
    )(*args)


def _mem_proj_kernel(mem_ref, w_ref, mk_ref, mvt_ref):
    mkv = jnp.dot(mem_ref[...].astype(BF16), w_ref[...], preferred_element_type=F32)
    mk_ref[0] = mkv[:, :MEM_WIDTH].astype(BF16)
    _store_values_transposed(mvt_ref, mkv[:, MEM_WIDTH:])


def _mem_proj(mem2d, w, nb):
    return pl.pallas_call(
        _mem_proj_kernel,
        grid=(nb,),
        in_specs=[pl.BlockSpec((N_MEM, D_MODEL), lambda b: (b, 0)),
                  pl.BlockSpec(w.shape, lambda b: (0, 0))],
        out_specs=[pl.BlockSpec((1, N_MEM, MEM_WIDTH), lambda b: (b, 0, 0)),
                   pl.BlockSpec((1, MEM_HEADS * VT_ROWS, N_MEM), lambda b: (b, 0, 0))],
        out_shape=[jax.ShapeDtypeStruct((nb, N_MEM, MEM_WIDTH), BF16),
                   jax.ShapeDtypeStruct((nb, MEM_HEADS * VT_ROWS, N_MEM), BF16)],
        compiler_params=_cparams(1),
        name="mem_proj",
    )(mem2d, w)


def _mem_attn_kernel(qt_ref, k_ref, vt_ref, o_ref):
    qt = qt_ref[0]
    kc = k_ref[0]
    for pair in range(MEM_HEADS // 2):
        qt_p = qt[pair * LANES:(pair + 1) * LANES]
        k_p = kc[:, pair * LANES:(pair + 1) * LANES]
        rowi = lax.broadcasted_iota(jnp.int32, qt_p.shape, 0)
        zero = jnp.zeros_like(qt_p)
        outs = []
        for h in range(2):
            mine = (rowi < HEAD_DIM) if h == 0 else (rowi >= HEAD_DIM)
            s = jnp.dot(k_p, jnp.where(mine, qt_p, zero), preferred_element_type=F32)
            p = jnp.exp2(s - jnp.max(s, axis=0, keepdims=True)).astype(BF16)
            r0 = (2 * pair + h) * VT_ROWS
            acc = jnp.dot(vt_ref[0, r0:r0 + VT_ROWS, :], p, preferred_element_type=F32)
            outs.append(acc[:HEAD_DIM] * (1.0 / acc[HEAD_DIM:HEAD_DIM + 1]))
        o_ref[0, :, pair * LANES:(pair + 1) * LANES] = (
            jnp.concatenate(outs, axis=0).T.astype(o_ref.dtype))


def _memory_attention(mqt, mk, mvt):
    nb, _, seq = mqt.shape
    tq = ATT_TQ
    return pl.pallas_call(
        _mem_attn_kernel,
        grid=(nb, seq // tq),
        in_specs=[pl.BlockSpec((1, MEM_WIDTH, tq), lambda b, i: (b, 0, i)),
                  pl.BlockSpec((1, N_MEM, MEM_WIDTH), lambda b, i: (b, 0, 0)),
                  pl.BlockSpec((1, MEM_HEADS * VT_ROWS, N_MEM), lambda b, i: (b, 0, 0))],
        out_specs=pl.BlockSpec((1, tq, MEM_WIDTH), lambda b, i: (b, i, 0)),
        out_shape=jax.ShapeDtypeStruct((nb, seq, MEM_WIDTH), BF16),
        compiler_params=_cparams(2),
        name="memory_attention",
    )(mqt, mk, mvt)


def _post_attn_kernel(x_ref, att_ref, memo_ref, wa_ref, wm_ref, g_ref, b_ref, wr_ref, br_ref,
                      x1_ref, x1p_ref, idx_ref, rank_ref, gate_ref, cnt_ref, carry_ref):
    t = pl.program_id(0)

    @pl.when(t == 0)
    def _():
        carry_ref[...] = jnp.zeros_like(carry_ref)

    tm = x_ref.shape[0]
    mix = (jnp.dot(att_ref[...], wa_ref[...], preferred_element_type=F32)
           + jnp.dot(memo_ref[...], wm_ref[...], preferred_element_type=F32))
    x1 = _layer_norm(DEEPNORM_ALPHA * x_ref[...] + mix, g_ref[...], b_ref[...])
    x1_ref[...] = x1
    x1b = x1.astype(BF16)
    x1p_ref[...] = _pack_bf16_pairs(x1b)
    logits = jnp.dot(x1b, wr_ref[...], preferred_element_type=F32) + br_ref[...]
    lane = lax.broadcasted_iota(jnp.int32, (tm, LANES), 1)
    work = jnp.where(lane < N_EXPERTS, logits, -jnp.inf)
    idxs, vals = [], []
    onehot = jnp.zeros((tm, LANES), F32)
    for _ in range(TOP_K):
        best = jnp.max(work, axis=1, keepdims=True)
        where_best = jnp.argmax(work, axis=1, keepdims=True).astype(jnp.int32)
        hit = lane == where_best
        onehot = jnp.where(hit, 1.0, onehot)
        work = jnp.where(hit, -jnp.inf, work)
        idxs.append(where_best)
        vals.append(best)
    exps = [jnp.exp(v - vals[0]) for v in vals]
    denom = exps[0] + exps[1] + exps[2] + exps[3]
    row = lax.broadcasted_iota(jnp.int32, (tm, tm), 0)
    col = lax.broadcasted_iota(jnp.int32, (tm, tm), 1)
    strict = jnp.where(row > col, 1.0, 0.0).astype(BF16)
    before = jnp.dot(strict, onehot.astype(BF16), preferred_element_type=F32) + carry_ref[...]
    idx_out = jnp.zeros((tm, LANES), F32)
    rank_out = jnp.zeros((tm, LANES), F32)
    gate_out = jnp.zeros((tm, LANES), F32)
    for r in range(TOP_K):
        rank_r = jnp.sum(jnp.where(lane == idxs[r], before, 0.0), axis=1, keepdims=True)
        idx_out = jnp.where(lane == r, idxs[r].astype(F32), idx_out)
        rank_out = jnp.where(lane == r, rank_r, rank_out)
        gate_out = jnp.where(lane == r, exps[r] / denom, gate_out)
    idx_ref[...] = idx_out.T[:8].astype(jnp.int32)
    rank_ref[...] = rank_out.T[:8].astype(jnp.int32)
    gate_ref[...] = gate_out
    total = carry_ref[...] + jnp.sum(onehot, axis=0, keepdims=True)
    carry_ref[...] = total
    cnt_ref[...] = jnp.broadcast_to(total, cnt_ref.shape)


def _post_attn(x2d, att, memo, wa, wm, g, b, wr, br):
    t_tokens = x2d.shape[0]
    tm = PROJ_TM
    row_spec = lambda width: pl.BlockSpec((tm, width), lambda i: (i, 0))
    full = lambda a: pl.BlockSpec(a.shape, lambda i: (0, 0))
    return pl.pallas_call(
        _post_attn_kernel,
        grid=(t_tokens // tm,),
        in_specs=[row_spec(D_MODEL), row_spec(att.shape[1]), row_spec(MEM_WIDTH),
                  full(wa), full(wm), full(g), full(b), full(wr), full(br)],
        out_specs=[row_spec(D_MODEL), row_spec(D_MODEL // 2),
                   pl.BlockSpec((8, tm), lambda i: (0, i)), pl.BlockSpec((8, tm), lambda i: (0, i)),
                   row_spec(LANES), pl.BlockSpec((8, LANES), lambda i: (0, 0))],
        out_shape=[jax.ShapeDtypeStruct((t_tokens, D_MODEL), F32),
                   jax.ShapeDtypeStruct((t_tokens, D_MODEL // 2), jnp.uint32),
                   jax.ShapeDtypeStruct((8, t_tokens), jnp.int32),
                   jax.ShapeDtypeStruct((8, t_tokens), jnp.int32),
                   jax.ShapeDtypeStruct((t_tokens, LANES), F32),
                   jax.ShapeDtypeStruct((8, LANES), F32)],
        scratch_shapes=[pltpu.VMEM((1, LANES), F32)],
        compiler_params=_cparams(1),
        name="outproj_ln_router",
    )(x2d, att, memo, wa, wm, g, b, wr, br)


def _expert_kernel(be_ref, nu_ref, x_ref, wgu_hbm, bgu_ref, wdn_hbm, bdn_ref, *rest,
                   n_chunks, blk_lo, n_blocks, layer, has_prev):
    o_ref, wgu_st, wdn_st, wgu_bf, wdn_bf, h_ref, sem = rest[1:] if has_prev else rest
    step = pl.program_id(0)
    blk = step + blk_lo
    live_end = jnp.minimum(blk_lo + pl.num_programs(0), nu_ref[0])
    expert = be_ref[blk]

    def fetch(e):
        return (pltpu.make_async_copy(wgu_hbm.at[layer, e], wgu_st, sem.at[0]),
                pltpu.make_async_copy(wdn_hbm.at[layer, e], wdn_st, sem.at[1]))

    @pl.when(((step == 0) | (expert != be_ref[jnp.maximum(blk - 1, 0)])) & (blk < live_end))
    def _():
        @pl.when(step == 0)
        def _():
            for copy in fetch(expert):
                copy.start()

        for copy in fetch(expert):
            copy.wait()
        wgu_bf[...] = wgu_st[...].astype(BF16)
        wdn_bf[...] = wdn_st[...].astype(BF16)
        nxt = lax.while_loop(
            lambda j: (j < live_end) & (be_ref[jnp.minimum(j, n_blocks - 1)] == expert),
            lambda j: j + 1, blk + 1)

        @pl.when(nxt < live_end)
        def _():
            for copy in fetch(be_ref[jnp.minimum(nxt, n_blocks - 1)]):
                copy.start()

    def mlp(rows):
        x = _unpack_bf16_pairs(x_ref[:rows]).astype(BF16)
        cw = D_EXPERT // n_chunks
        for c in range(n_chunks):
            gs = slice(c * cw, (c + 1) * cw)
            us = slice(D_EXPERT + c * cw, D_EXPERT + (c + 1) * cw)
            g = jnp.dot(x, wgu_bf[:, gs], preferred_element_type=F32) + bgu_ref[0, 0, :, gs]
            u = jnp.dot(x, wgu_bf[:, us], preferred_element_type=F32) + bgu_ref[0, 0, :, us]
            g = jnp.minimum(g, SWIGLU_LIMIT)
            u = jnp.clip(u, -SWIGLU_LIMIT, SWIGLU_LIMIT)
            h_ref[:rows, gs] = ((u + 1.0) * (g * jax.nn.sigmoid(SWIGLU_ALPHA * g))).astype(BF16)
        y = jnp.dot(h_ref[:rows], wdn_bf[...], preferred_element_type=F32) + bdn_ref[0, 0]
        return _pack_bf16_pairs(y.astype(BF16))

    @pl.when(blk < nu_ref[0])
    def _():
        o_ref[...] = mlp(o_ref.shape[0])

    @pl.when(blk >= nu_ref[0])
    def _():
        o_ref[...] = jnp.zeros_like(o_ref)


def _experts(block_expert, n_used, xb, blk_lo, p_rows, y_prev, layer, wgu, bgu, wdn, bdn):
    tm = MOE_TM
    has_prev = y_prev is not None
    b_map = lambda i, be, nu: (layer, be[i + blk_lo], 0, 0)
    in_specs = [pl.BlockSpec((tm, D_MODEL // 2), lambda i, be, nu: (i, 0)),
                pl.BlockSpec(memory_space=pl.ANY),
                pl.BlockSpec((1, 1, 1, 2 * D_EXPERT), b_map),
                pl.BlockSpec(memory_space=pl.ANY),
                pl.BlockSpec((1, 1, 1, D_MODEL), b_map)]
    args = [block_expert, n_used, xb, wgu, bgu, wdn, bdn]
    if has_prev:
        in_specs.append(pl.BlockSpec(memory_space=pl.ANY))
        args.append(y_prev)
    grid_spec = pltpu.PrefetchScalarGridSpec(
        num_scalar_prefetch=2,
        grid=(xb.shape[0] // tm,),
        in_specs=in_specs,
        out_specs=pl.BlockSpec((tm, D_MODEL // 2), lambda i, be, nu: (i + blk_lo, 0)),
        scratch_shapes=[pltpu.VMEM((D_MODEL, 2 * D_EXPERT), F32),
                        pltpu.VMEM((D_EXPERT, D_MODEL), F32),
                        pltpu.VMEM((D_MODEL, 2 * D_EXPERT), BF16),
                        pltpu.VMEM((D_EXPERT, D_MODEL), BF16),
                        pltpu.VMEM((tm, D_EXPERT), BF16),
                        pltpu.SemaphoreType.DMA((2,))],
    )
    return pl.pallas_call(
        functools.partial(_expert_kernel, n_chunks=4, blk_lo=blk_lo, n_blocks=p_rows // tm,
                          layer=layer, has_prev=has_prev),
        grid_spec=grid_spec,
        out_shape=jax.ShapeDtypeStruct((p_rows, D_MODEL // 2), jnp.uint32),
        input_output_aliases={len(args) - 1: 0} if has_prev else {},
        compiler_params=_cparams(1),
        name="experts",
    )(*args)


def _sc_move_rows(table_hbm, idx_v, out_hbm, base, n_chunks, rows_v, gsem, wsem):
    def gather(c, slot):
        off = pl.multiple_of(c * SC_CHUNK, SC_CHUNK)
        return pltpu.make_async_copy(table_hbm.at[idx_v.at[pl.ds(off, SC_CHUNK)]],
                                     rows_v.at[slot], gsem.at[slot])

    def put(c, slot):
        off = pl.multiple_of(c * SC_CHUNK, SC_CHUNK)
        return pltpu.make_async_copy(rows_v.at[slot], out_hbm.at[pl.ds(base + off, SC_CHUNK)],
                                     wsem.at[slot])

    gather(0, 0).start()
    gather(1, 1).start()

    @pl.loop(0, n_chunks, step=2)
    def _(c):
        for slot in range(2):
            gather(c + slot, slot).wait()
            put(c + slot, slot).start()
        for slot in range(2):
            put(c + slot, slot).wait()

            @pl.when(c + 2 + slot < n_chunks)
            def _():
                gather(c + 2 + slot, slot).start()


def _sc_gather_rows(table, idx):
    n_idx = idx.shape[0]
    width = table.shape[1]
    per_worker = n_idx // SC_WORKERS
    n_chunks = per_worker // SC_CHUNK
    assert per_worker * SC_WORKERS == n_idx and n_chunks * SC_CHUNK == per_worker and n_chunks % 2 == 0
    mesh = plsc.VectorSubcoreMesh(core_axis_name="c", subcore_axis_name="s",
                                  num_cores=SC_CORES, num_subcores=SC_SUBCORES)

    def body(table_hbm, idx_hbm, out_hbm, idx_v, rows_v, gsem, wsem):
        wid = lax.axis_index("s") * SC_CORES + lax.axis_index("c")
        base = wid * per_worker
        pltpu.sync_copy(idx_hbm.at[pl.ds(base, per_worker)], idx_v)
        _sc_move_rows(table_hbm, idx_v, out_hbm, base, n_chunks, rows_v, gsem, wsem)

    return pl.kernel(
        body,
        out_type=jax.ShapeDtypeStruct((n_idx, width), table.dtype),
        mesh=mesh,
        scratch_types=[pltpu.VMEM((per_worker,), jnp.int32),
                       pltpu.VMEM((2, SC_CHUNK, width), table.dtype),
                       pltpu.SemaphoreType.DMA((2,)),
                       pltpu.SemaphoreType.DMA((2,))],
        name="sc_gather_rows",
    )(table, idx)


def _sc_dispatch_rows(table, dest, slot_lo, n_slots):
    n_tok, width = table.shape
    n_pairs = dest.shape[0]
    per_worker = n_slots // SC_WORKERS
    n_chunks = per_worker // SC_CHUNK
    assert per_worker * SC_WORKERS == n_slots and n_chunks * SC_CHUNK == per_worker and n_chunks % 2 == 0
    n_scan = n_pairs // SC_SCAN
    assert n_scan * SC_SCAN == n_pairs
    mesh = plsc.VectorSubcoreMesh(core_axis_name="c", subcore_axis_name="s",
                                  num_cores=SC_CORES, num_subcores=SC_SUBCORES)

    def body(table_hbm, dest_hbm, out_hbm, tok_v, dest_v, rows_v, gsem, wsem):
        wid = lax.axis_index("s") * SC_CORES + lax.axis_index("c")
        base = wid * per_worker
        first = slot_lo + base
        lane = lax.iota(jnp.int32, SC_LANES)

        @pl.loop(0, per_worker // SC_LANES)
        def _(i):
            off = pl.multiple_of(i * SC_LANES, SC_LANES)
            tok_v[pl.ds(off, SC_LANES)] = lax.rem(first + off + lane, n_tok)

        @pl.loop(0, n_scan)
        def _(g):
            goff = pl.multiple_of(g * SC_SCAN, SC_SCAN)
            pltpu.sync_copy(dest_hbm.at[pl.ds(goff, SC_SCAN)], dest_v)

            @plsc.parallel_loop(0, SC_SCAN // SC_LANES, unroll=8)
            def _(i):
                off = pl.multiple_of(i * SC_LANES, SC_LANES)
                local = dest_v[pl.ds(off, SC_LANES)] - first
                mine = (local >= 0) & (local < per_worker)
                pair = goff + off + lane
                plsc.store_scatter(tok_v, [jnp.where(mine, local, 0)], lax.rem(pair, n_tok), mask=mine)

        _sc_move_rows(table_hbm, tok_v, out_hbm, base, n_chunks, rows_v, gsem, wsem)

    return pl.kernel(
        body,
        out_type=jax.ShapeDtypeStruct((n_slots, width), table.dtype),
        mesh=mesh,
        scratch_types=[pltpu.VMEM((per_worker,), jnp.int32),
                       pltpu.VMEM((SC_SCAN,), jnp.int32),
                       pltpu.VMEM((2, SC_CHUNK, width), table.dtype),
                       pltpu.SemaphoreType.DMA((2,)),
                       pltpu.SemaphoreType.DMA((2,))],
        compiler_params=pltpu.CompilerParams(needs_layout_passes=False),
        name="sc_dispatch_rows",
    )(table, dest)


def _combine_kernel(x1_ref, yg_ref, gate_ref, g_ref, b_ref, *rest):
    o_ref = rest[-1]
    gates = gate_ref[...]
    ffn = _unpack_bf16_pairs(yg_ref[0]) * gates[:, 0:1]
    for r in range(1, TOP_K):
        ffn = ffn + _unpack_bf16_pairs(yg_ref[r]) * gates[:, r:r + 1]
    o_ref[...] = _layer_norm(DEEPNORM_ALPHA * x1_ref[...] + ffn, g_ref[...], b_ref[...])


def _combine(x1, yg, gates, g, b, tok_lo, out_prev):
    t_tokens = x1.shape[0]
    tm = COMBINE_TM
    tile_lo = tok_lo // tm
    in_specs = [pl.BlockSpec((tm, D_MODEL), lambda i: (i + tile_lo, 0)),
                pl.BlockSpec((TOP_K, tm, D_MODEL // 2), lambda i: (0, i, 0)),
                pl.BlockSpec((tm, LANES), lambda i: (i + tile_lo, 0)),
                pl.BlockSpec((1, D_MODEL), lambda i: (0, 0)),
                pl.BlockSpec((1, D_MODEL), lambda i: (0, 0))]
    args = [x1, yg, gates, g, b]
    if out_prev is not None:
        in_specs.append(pl.BlockSpec(memory_space=pl.ANY))
        args.append(out_prev)
    return pl.pallas_call(
        _combine_kernel,
        grid=(yg.shape[1] // tm,),
        in_specs=in_specs,
        out_specs=pl.BlockSpec((tm, D_MODEL), lambda i: (i + tile_lo, 0)),
        out_shape=jax.ShapeDtypeStruct((t_tokens, D_MODEL), F32),
        input_output_aliases={len(args) - 1: 0} if out_prev is not None else {},
        compiler_params=_cparams(1),
        name="combine_ln",
    )(*args)


def _moe(x1, x1p, idx, rank, gates, counts, layer, wgu, bgu, wdn, bdn, g, b):
    t_tokens = x1.shape[0]
    tm = MOE_TM
    counts = counts[0, :N_EXPERTS].astype(jnp.int32)
    padded = ((counts + tm - 1) // tm) * tm
    pend = jnp.cumsum(padded)
    pstart = pend - padded
    dest = rank[:TOP_K]
    for e in range(N_EXPERTS):
        dest = dest + jnp.where(idx[:TOP_K] == e, pstart[e], 0)
    n_blocks = (t_tokens * TOP_K) // tm + N_EXPERTS
    p_rows = n_blocks * tm
    block_start = jnp.arange(n_blocks, dtype=jnp.int32) * tm
    block_expert = jnp.minimum(
        jnp.sum((pend[None, :] <= block_start[:, None]).astype(jnp.int32), axis=1), N_EXPERTS - 1)
    n_used = (pend[-1:] // tm).astype(jnp.int32)
    y = None
    for lo, hi in ((0, n_blocks // MOE_SPLIT), (n_blocks // MOE_SPLIT, n_blocks)):
        xb = _sc_dispatch_rows(x1p, dest.reshape(-1), lo * tm, (hi - lo) * tm)
        y = _experts(block_expert, n_used, xb, lo, p_rows, y, layer, wgu, bgu, wdn, bdn)
    out = None
    for lo, hi in ((0, t_tokens // MOE_SPLIT), (t_tokens // MOE_SPLIT, t_tokens)):
        yg = _sc_gather_rows(y, dest[:, lo:hi].reshape(-1)).reshape(TOP_K, hi - lo, D_MODEL // 2)
        out = _combine(x1, yg, gates, g, b, lo, out)
    return out


def _row(v, width=None):
    v = v.astype(F32).reshape(1, -1)
    if width is not None and v.shape[1] < width:
        v = jnp.pad(v, ((0, 0), (0, width - v.shape[1])))
    return v


def _pad_cols(w, width):
    return jnp.pad(w, ((0, 0), (0, width - w.shape[1])))


def kernel(x, mem, positions, a_w_in, a_b_f, a_w_out, b_w_in, b_g_q, b_w_uq, b_w_out,
           kv_w_dkv, kv_g, kv_w_ukv, mem_w_kv, ln_g, ln_b,
           moe_w_r, moe_b_r, moe_w_gu, moe_b_gu, moe_w_dn, moe_b_dn):
    nb, seq, d = x.shape
    t_tokens = nb * seq
    n_a = a_w_in.shape[0]
    x2d = x.reshape(t_tokens, d)
    mem2d = mem.reshape(nb * N_MEM, d)
    pos2d = positions.reshape(1, t_tokens)
    half = QK_ROPE // 2
    invf = (ROPE_THETA ** (-jnp.arange(half, dtype=F32) * 2.0 / QK_ROPE)).reshape(half, 1)

    shared_kv = None
    for l in range(DEPTH):
        mk, mvt = _mem_proj(mem2d, mem_w_kv[l].astype(BF16), nb)
        if l < n_a:
            w_in = a_w_in[l]
            w = jnp.concatenate([w_in[:, :3 * FOX_WIDTH],
                                 _pad_cols(w_in[:, 3 * FOX_WIDTH:3 * FOX_WIDTH + FOX_HEADS], LANES),
                                 w_in[:, 3 * FOX_WIDTH + FOX_HEADS:]], axis=1).astype(BF16)
            qt, k, vt, mqt, cum, cumt, cfirst, clast, knorm = _fox_proj(
                x2d, w, _row(a_b_f[l], LANES), seq)
            stats = [s.reshape(-1, LANES) for s in (cfirst, clast, knorm)]
            att = _causal_attention(qt, k.reshape(nb, seq, -1), vt,
                                    [cum.reshape(nb, seq, LANES), cumt] + stats, slab=False)
            w_out = a_w_out[l]
        else:
            bl = l - n_a
            w = jnp.concatenate([b_w_in[bl], kv_w_dkv[:, :KV_LORA],
                                 jnp.zeros((d, QK_NOPE), F32), kv_w_dkv[:, KV_LORA:],
                                 jnp.zeros((d, LANES - QK_NOPE - QK_ROPE), F32)], axis=1).astype(BF16)
            wuq = b_w_uq[bl].reshape(Q_LORA, MLA_HEADS, QK_NOPE + QK_ROPE)
            wuq = jnp.pad(wuq, ((0, 0), (0, 0), (0, MLA_SLAB - QK_NOPE - QK_ROPE)))
            wuq = wuq.reshape(Q_LORA, MLA_HEADS * MLA_SLAB).astype(BF16)
            wukv = kv_w_ukv.reshape(KV_LORA, MLA_HEADS, QK_NOPE + V_DIM)
            wk = jnp.pad(wukv[:, :, :QK_NOPE], ((0, 0), (0, 0), (0, MLA_SLAB - QK_NOPE)))
            wk = wk.reshape(KV_LORA, MLA_HEADS * MLA_SLAB).astype(BF16)
            wv = wukv[:, :, QK_NOPE:].reshape(KV_LORA, MLA_V_WIDTH).astype(BF16)
            qt, mqt, k_new, vt_new = _mla_proj(x2d, pos2d, invf, w, _row(b_g_q[bl]), wuq,
                                              _row(kv_g), wk, wv, seq)
            if shared_kv is None:
                shared_kv = (k_new.reshape(nb, seq, -1), vt_new)
            att = _causal_attention(qt, shared_kv[0], shared_kv[1], None, slab=True)
            w_out = b_w_out[bl]
        memo = _memory_attention(mqt, mk, mvt)
        n_att = w_out.shape[0] - MEM_WIDTH
        x1, x1p, idx, rank, gates, counts = _post_attn(
            x2d, att.reshape(t_tokens, -1), memo.reshape(t_tokens, MEM_WIDTH),
            w_out[:n_att].astype(BF16), w_out[n_att:].astype(BF16),
            _row(ln_g[l, 0]), _row(ln_b[l, 0]),
            _pad_cols(moe_w_r[l], LANES).astype(BF16), _row(moe_b_r[l], LANES))
        x2d = _moe(x1, x1p, idx, rank, gates, counts, l,
                   moe_w_gu, moe_b_gu.reshape(DEPTH, N_EXPERTS, 1, -1),
                   moe_w_dn, moe_b_dn.reshape(DEPTH, N_EXPERTS, 1, -1),
                   _row(ln_g[l, 1]), _row(ln_b[l, 1]))
    return x2d.reshape(nb, seq, d)
```

```python
import functools
import math

import jax
import jax.numpy as jnp
from jax import lax
from jax.experimental import pallas as pl
from jax.experimental.pallas import tpu as pltpu
from jax.experimental.pallas import tpu_sc as plsc

F32 = jnp.float32
BF16 = jnp.bfloat16

D_MODEL = 1024
DEPTH = 2
N_MEM = 256
HEAD_DIM = 64
FOX_HEADS = 12
MEM_HEADS = 4
MLA_HEADS = 12
Q_LORA = 384
KV_LORA = 256
QK_NOPE = 64
QK_ROPE = 32
V_DIM = 64
ROPE_THETA = 10000.0
N_EXPERTS = 32
TOP_K = 4
D_EXPERT = D_MODEL
SWIGLU_LIMIT = 7.0
SWIGLU_ALPHA = 1.702
LN_EPS = 1e-5
RMS_EPS = 1e-6
NEG_INF = -1e30
DEEPNORM_ALPHA = (2 * DEPTH) ** 0.25
FOX_WIDTH = FOX_HEADS * HEAD_DIM
MEM_WIDTH = MEM_HEADS * HEAD_DIM
MLA_V_WIDTH = MLA_HEADS * V_DIM

LANES = 128
LOG2E = math.log2(math.e)
VMEM_LIMIT = 48 * 1024 * 1024

PROJ_TM = 512
ATT_TQ = 512
FOX_TQ = 512
MOE_TM = 512
COMBINE_TM = 512
MOE_SPLIT = 4
MLA_SLAB = LANES
VT_ROWS = HEAD_DIM + 16
SC_CORES = 2
SC_SUBCORES = 16
SC_WORKERS = SC_CORES * SC_SUBCORES
SC_CHUNK = 64
SC_LANES = 16
SC_SCAN = 8192


def _cparams(n_axes):
    return pltpu.CompilerParams(dimension_semantics=("arbitrary",) * n_axes,
                                vmem_limit_bytes=VMEM_LIMIT)


def _split3(x):
    hi = x.astype(BF16)
    r1 = x - hi.astype(F32)
    mid = r1.astype(BF16)
    lo = (r1 - mid.astype(F32)).astype(BF16)
    return hi, mid, lo


def _pack_bf16_pairs(v):
    bits = pltpu.bitcast(v.astype(F32), jnp.uint32)
    half = v.shape[1] // 2
    return (bits[:, :half] >> 16) | bits[:, half:]


def _unpack_bf16_pairs(words):
    return jnp.concatenate([pltpu.bitcast(words << 16, F32),
                            pltpu.bitcast(words & jnp.uint32(0xFFFF0000), F32)], axis=1)


def _layer_norm(y, g, b):
    mu = jnp.mean(y, axis=-1, keepdims=True)
    yc = y - mu
    var = jnp.mean(yc * yc, axis=-1, keepdims=True)
    return yc * lax.rsqrt(var + LN_EPS) * g + b


def _rms_norm(y, g):
    return y * lax.rsqrt(jnp.mean(y * y, axis=-1, keepdims=True) + RMS_EPS) * g


def _store_transposed(dst_ref, val):
    for s in range(val.shape[1] // LANES):
        sl = slice(s * LANES, (s + 1) * LANES)
        dst_ref[0, sl, :] = val[:, sl].T.astype(dst_ref.dtype)


def _store_values_transposed(vt_ref, val):
    tm = val.shape[0]
    ones = jnp.ones((VT_ROWS - HEAD_DIM, tm), vt_ref.dtype)
    for s in range(val.shape[1] // LANES):
        pair_t = val[:, s * LANES:(s + 1) * LANES].T.astype(vt_ref.dtype)
        for h in range(2):
            r0 = (2 * s + h) * VT_ROWS
            vt_ref[0, r0:r0 + HEAD_DIM, :] = pair_t[h * HEAD_DIM:(h + 1) * HEAD_DIM]
            vt_ref[0, r0 + HEAD_DIM:r0 + VT_ROWS, :] = ones


def _fox_proj_kernel(x_ref, w_ref, bf_ref, seg_ref, qt_ref, k_ref, vt_ref, mqt_ref, cum_ref,
                     cumt_ref, cfirst_ref, clast_ref, knorm_ref, carry_ref, knmax_ref, *,
                     tiles_per_batch):
    t = pl.program_id(0)

    @pl.when(t % tiles_per_batch == 0)
    def _():
        carry_ref[...] = jnp.zeros_like(carry_ref)
        knmax_ref[...] = jnp.zeros_like(knmax_ref)

    tm = x_ref.shape[0]
    proj = jnp.dot(x_ref[...].astype(BF16), w_ref[...], preferred_element_type=F32)
    qscale = HEAD_DIM ** -0.5 * LOG2E
    _store_transposed(qt_ref, proj[:, :FOX_WIDTH] * qscale)
    kb = proj[:, FOX_WIDTH:2 * FOX_WIDTH].astype(BF16)
    k_ref[...] = kb
    _store_values_transposed(vt_ref, proj[:, 2 * FOX_WIDTH:3 * FOX_WIDTH])
    kf = kb.astype(F32)
    ksq = jnp.dot((kf * kf).astype(BF16), seg_ref[...], preferred_element_type=F32)
    n_sub = tm // FOX_TQ
    for sub in range(n_sub):
        tile_max = jnp.sqrt(jnp.max(ksq[sub * FOX_TQ:(sub + 1) * FOX_TQ], axis=0, keepdims=True))
        knmax_ref[...] = jnp.maximum(knmax_ref[...], tile_max)
        knorm_ref[sub] = knmax_ref[...]
    f = proj[:, 3 * FOX_WIDTH:3 * FOX_WIDTH + LANES] + bf_ref[...]
    _store_transposed(mqt_ref, proj[:, 3 * FOX_WIDTH + LANES:] * qscale)
    log_f = jnp.minimum(f, 0.0) - jnp.log1p(jnp.exp(-jnp.abs(f)))
    row = lax.broadcasted_iota(jnp.int32, (tm, tm), 0)
    col = lax.broadcasted_iota(jnp.int32, (tm, tm), 1)
    tri = jnp.where(row >= col, 1.0, 0.0).astype(BF16)
    hi, mid, lo = _split3(log_f)
    cum = (jnp.dot(tri, hi, preferred_element_type=F32)
           + jnp.dot(tri, mid, preferred_element_type=F32)
           + jnp.dot(tri, lo, preferred_element_type=F32)) + carry_ref[...]
    carry_ref[...] = cum[tm - 1:tm, :]
    cum2 = cum * LOG2E
    cum_ref[...] = cum2
    cumt_ref[0] = cum2.T[:16, :]
    for sub in range(n_sub):
        cfirst_ref[sub] = cum2[sub * FOX_TQ:sub * FOX_TQ + 1, :]
        clast_ref[sub] = cum2[(sub + 1) * FOX_TQ - 1:(sub + 1) * FOX_TQ, :]


def _fox_proj(x2d, w, bf, seq):
    t_tokens = x2d.shape[0]
    tm = PROJ_TM
    nb = t_tokens // seq
    n = w.shape[1]
    tiles_per_batch = seq // tm
    n_tiles = t_tokens // tm
    seg = (jnp.arange(FOX_WIDTH)[:, None] // HEAD_DIM == jnp.arange(LANES)[None, :]).astype(BF16)
    row_spec = lambda width: pl.BlockSpec((tm, width), lambda i: (i, 0))
    t_spec = lambda rows: pl.BlockSpec(
        (1, rows, tm), lambda i: (i // tiles_per_batch, 0, i % tiles_per_batch))
    n_sub = tm // FOX_TQ
    stat_spec = pl.BlockSpec((n_sub, 1, LANES), lambda i: (i, 0, 0))
    stat_shape = jax.ShapeDtypeStruct((n_tiles * n_sub, 1, LANES), F32)
    return pl.pallas_call(
        functools.partial(_fox_proj_kernel, tiles_per_batch=tiles_per_batch),
        grid=(n_tiles,),
        in_specs=[row_spec(D_MODEL),
                  pl.BlockSpec((D_MODEL, n), lambda i: (0, 0)),
                  pl.BlockSpec((1, LANES), lambda i: (0, 0)),
                  pl.BlockSpec((FOX_WIDTH, LANES), lambda i: (0, 0))],
        out_specs=[t_spec(FOX_WIDTH), row_spec(FOX_WIDTH), t_spec(FOX_HEADS * VT_ROWS),
                   t_spec(MEM_WIDTH), row_spec(LANES), t_spec(16),
                   stat_spec, stat_spec, stat_spec],
        out_shape=[jax.ShapeDtypeStruct((nb, FOX_WIDTH, seq), BF16),
                   jax.ShapeDtypeStruct((t_tokens, FOX_WIDTH), BF16),
                   jax.ShapeDtypeStruct((nb, FOX_HEADS * VT_ROWS, seq), BF16),
                   jax.ShapeDtypeStruct((nb, MEM_WIDTH, seq), BF16),
                   jax.ShapeDtypeStruct((t_tokens, LANES), F32),
                   jax.ShapeDtypeStruct((nb, 16, seq), F32),
                   stat_shape, stat_shape, stat_shape],
        scratch_shapes=[pltpu.VMEM((1, LANES), F32), pltpu.VMEM((1, LANES), F32)],
        compiler_params=_cparams(1),
        name="fox_proj",
    )(x2d, w, bf, seg)


def _rope_slab_t(slab_t, cos_t, sin_t):
    half = QK_ROPE // 2
    x1 = slab_t[QK_NOPE:QK_NOPE + half]
    x2 = slab_t[QK_NOPE + half:QK_NOPE + QK_ROPE]
    return jnp.concatenate([slab_t[:QK_NOPE], x1 * cos_t - x2 * sin_t, x1 * sin_t + x2 * cos_t,
                            slab_t[QK_NOPE + QK_ROPE:]], axis=0)


def _mla_proj_kernel(x_ref, pos_ref, invf_ref, w_ref, gq_ref, wuq_ref, gkv_ref, wk_ref, wv_ref,
                     qt_ref, mqt_ref, k_ref, vt_ref):
    proj = jnp.dot(x_ref[...].astype(BF16), w_ref[...], preferred_element_type=F32)
    ang_t = invf_ref[...] * pos_ref[...].astype(F32)
    cos_t = jnp.cos(ang_t)
    sin_t = jnp.sin(ang_t)
    c_q = _rms_norm(proj[:, :Q_LORA], gq_ref[...])
    _store_transposed(mqt_ref, proj[:, Q_LORA:Q_LORA + MEM_WIDTH] * (HEAD_DIM ** -0.5 * LOG2E))
    kv_off = Q_LORA + MEM_WIDTH
    c_kv = _rms_norm(proj[:, kv_off:kv_off + KV_LORA], gkv_ref[...])
    kr = _rope_slab_t(proj[:, kv_off + KV_LORA:].T, cos_t, sin_t).T
    q = jnp.dot(c_q.astype(BF16), wuq_ref[...], preferred_element_type=F32)
    kn = jnp.dot(c_kv.astype(BF16), wk_ref[...], preferred_element_type=F32)
    qscale = (QK_NOPE + QK_ROPE) ** -0.5 * LOG2E
    for h in range(MLA_HEADS):
        sl = slice(h * MLA_SLAB, (h + 1) * MLA_SLAB)
        qt_ref[0, sl, :] = (_rope_slab_t(q[:, sl].T, cos_t, sin_t) * qscale).astype(BF16)
        k_ref[:, sl] = (kn[:, sl] + kr).astype(BF16)
    _store_values_transposed(vt_ref, jnp.dot(c_kv.astype(BF16), wv_ref[...],
                                             preferred_element_type=F32))


def _mla_proj(x2d, pos2d, invf, w, gq, wuq, gkv, wk, wv, seq):
    t_tokens = x2d.shape[0]
    tm = PROJ_TM
    nb = t_tokens // seq
    tiles_per_batch = seq // tm
    row_spec = lambda width: pl.BlockSpec((tm, width), lambda i: (i, 0))
    t_spec = lambda rows: pl.BlockSpec(
        (1, rows, tm), lambda i: (i // tiles_per_batch, 0, i % tiles_per_batch))
    full = lambda a: pl.BlockSpec(a.shape, lambda i: (0, 0))
    slabs = MLA_HEADS * MLA_SLAB
    return pl.pallas_call(
        _mla_proj_kernel,
        grid=(t_tokens // tm,),
        in_specs=[row_spec(D_MODEL), pl.BlockSpec((1, tm), lambda i: (0, i)), full(invf), full(w),
                  full(gq), full(wuq), full(gkv), full(wk), full(wv)],
        out_specs=[t_spec(slabs), t_spec(MEM_WIDTH), row_spec(slabs),
                   t_spec(MLA_HEADS * VT_ROWS)],
        out_shape=[jax.ShapeDtypeStruct((nb, slabs, seq), BF16),
                   jax.ShapeDtypeStruct((nb, MEM_WIDTH, seq), BF16),
                   jax.ShapeDtypeStruct((t_tokens, slabs), BF16),
                   jax.ShapeDtypeStruct((nb, MLA_HEADS * VT_ROWS, seq), BF16)],
        compiler_params=_cparams(1),
        name="mla_proj",
    )(x2d, pos2d, invf, w, gq, wuq, gkv, wk, wv)


SKIP_LOG2 = -160.0
NORM_SLACK = 1.02


def _causal_attn_kernel(*refs, tq, n_tiles, fox, slab, long_body):
    if fox:
        (cf_ref, cl_ref, kn_ref, qt_ref, k_ref, vt_ref, cq_ref, ck_ref,
         o_ref, m_ref, acc_ref, s_ref, cmax_ref) = refs
    else:
        qt_ref, qtn_ref, k_ref, vt_ref, o_ref, m_ref, acc_ref, s_ref, cmax_ref = refs
    b = pl.program_id(0)
    pair = pl.program_id(1)
    i = pl.program_id(2)
    m_ref[...] = jnp.full(m_ref.shape, NEG_INF, F32)
    acc_ref[...] = jnp.zeros(acc_ref.shape, F32)
    qt = qt_ref[0]
    if slab:
        qth = [qt[:MLA_SLAB], qt[MLA_SLAB:]]
    else:
        rowi = lax.broadcasted_iota(jnp.int32, qt.shape, 0)
        zero = jnp.zeros_like(qt)
        qth = [jnp.where(rowi < HEAD_DIM, qt, zero), jnp.where(rowi >= HEAD_DIM, qt, zero)]
    if fox:
        lane = lax.broadcasted_iota(jnp.int32, (1, LANES), 1)
        cq = [cq_ref[0, pl.ds(2 * pair + h, 1), :] for h in range(2)]

    def logits_to(slot, j, qs=None):
        qs = qth if qs is None else qs
        off = pl.multiple_of(j * tq, tq)
        kc = k_ref[0, pl.ds(off, tq), :]
        for h in range(2):
            kh = kc[:, h * MLA_SLAB:(h + 1) * MLA_SLAB] if slab else kc
            s = jnp.dot(kh, qs[h], preferred_element_type=F32)
            if fox:
                ck_blk = ck_ref[0, pl.ds(off, tq), :]
                ck = jnp.sum(jnp.where(lane == 2 * pair + h, ck_blk, 0.0), axis=1, keepdims=True)
                s = s + cq[h] - ck
            s_ref[slot, h] = s
            cmax_ref[slot, h] = jnp.max(s, axis=0, keepdims=True)

    def softmax_pv(slot, j, causal_mask):
        off = pl.multiple_of(j * tq, tq)
        for h in range(2):
            s = s_ref[slot, h]
            if causal_mask is not None:
                s = jnp.where(causal_mask, s, NEG_INF)
                chunk_max = jnp.max(s, axis=0, keepdims=True)
            else:
                chunk_max = cmax_ref[slot, h]
            m_prev = m_ref[h]
            m_new = jnp.maximum(m_prev, chunk_max)
            alpha = jnp.exp2(m_prev - m_new)
            p = jnp.exp2(s - m_new).astype(BF16)
            vth = vt_ref[0, h * VT_ROWS:(h + 1) * VT_ROWS, pl.ds(off, tq)]
            acc_ref[h] = acc_ref[h] * alpha + jnp.dot(vth, p, preferred_element_type=F32)
            m_ref[h] = m_new

    if fox:
        base = b * n_tiles
        qn = []
        for h in range(2):
            qf = qth[h].astype(F32)
            qn.append(jnp.sqrt(jnp.max(jnp.sum(qf * qf, axis=0, keepdims=True))) * NORM_SLACK)

        top = [2.0 * qn[h] * kn_ref[base + i, 2 * pair + h] + cf_ref[base + i, 2 * pair + h]
               for h in range(2)]

        def live(j):
            ub = [top[h] - cl_ref[base + jnp.maximum(j, 0), 2 * pair + h] for h in range(2)]
            return (j >= 0) & (jnp.maximum(ub[0], ub[1]) > SKIP_LOG2)

        j0 = lax.while_loop(live, lambda j: j - 1, i - 1) + 1
    else:
        j0 = 0
    n_before = i - j0

    def causal_mask():
        key = lax.broadcasted_iota(jnp.int32, (tq, tq), 0)
        qry = lax.broadcasted_iota(jnp.int32, (tq, tq), 1)
        return key <= qry

    def pipelined(handoff):
        first = 1 if handoff else j0
        n_mid = jnp.maximum(i - 1, 0) if handoff else n_before
        odd = (n_mid % 2) == 1
        if handoff:
            @pl.when(i == 0)
            def _():
                logits_to(0, 0)

            @pl.when(odd)
            def _():
                logits_to(1, 1)
                softmax_pv(2, 0, None)
                logits_to(0, 2)
                softmax_pv(1, 1, None)

            @pl.when(jnp.logical_not(odd) & (i > 0))
            def _():
                logits_to(0, 1)
                softmax_pv(2, 0, None)
        else:
            @pl.when(odd)
            def _():
                logits_to(1, j0)
                logits_to(0, j0 + 1)
                softmax_pv(1, j0, None)

            @pl.when(jnp.logical_not(odd))
            def _():
                logits_to(0, j0)

        def two_steps(j):
            logits_to(1, j + 1)
            softmax_pv(0, j, None)
            logits_to(0, j + 2)
            softmax_pv(1, j + 1, None)

        j_even = first + (n_mid % 2)
        n_two = n_mid // 2
        if long_body:
            @pl.when((n_two % 2) == 1)
            def _():
                two_steps(j_even)

            j_quad = j_even + 2 * (n_two % 2)

            def body(t, carry):
                two_steps(j_quad + 4 * t)
                two_steps(j_quad + 4 * t + 2)
                return carry

            lax.fori_loop(0, n_two // 2, body, 0)
        else:
            def body(t, carry):
                two_steps(j_even + 2 * t)
                return carry

            lax.fori_loop(0, n_two, body, 0)
        if handoff:
            qn = qtn_ref[0]
            logits_to(2, 0, [qn[:MLA_SLAB], qn[MLA_SLAB:]])
        softmax_pv(0, i, causal_mask())

    if fox:
        @pl.when(n_before == 1)
        def _():
            logits_to(1, j0)
            logits_to(0, i)
            softmax_pv(1, j0, None)
            softmax_pv(0, i, causal_mask())

        @pl.when(n_before != 1)
        def _():
            pipelined(False)
    else:
        pipelined(True)
    out_t = jnp.concatenate(
        [acc_ref[h, :HEAD_DIM] * (1.0 / acc_ref[h, HEAD_DIM:HEAD_DIM + 1]) for h in range(2)],
        axis=0)
    o_ref[0] = out_t.T.astype(o_ref.dtype)


def _causal_attention(qt, k, vt, fox_args, *, slab):
    nb, seq, _ = k.shape
    fox = fox_args is not None
    tq = FOX_TQ if fox else ATT_TQ
    n_tiles = seq // tq
    n_pairs = FOX_HEADS // 2
    rows = 2 * MLA_SLAB if slab else LANES
    n_slots = 2 if fox else 3
    in_specs = [pl.BlockSpec((1, rows, tq), lambda b, p, i: (b, p, i)),
                pl.BlockSpec((1, seq, rows), lambda b, p, i: (b, 0, p)),
                pl.BlockSpec((1, 2 * VT_ROWS, seq), lambda b, p, i: (b, p, 0))]
    args = [qt, k, vt]
    if not fox:
        in_specs.insert(1, pl.BlockSpec((1, rows, tq),
                                        lambda b, p, i: (b, p, jnp.minimum(i + 1, n_tiles - 1))))
        args.insert(1, qt)
    if fox:
        cum, cumt, cfirst, clast, knorm = fox_args
        smem = pl.BlockSpec(memory_space=pltpu.SMEM)
        in_specs = [smem, smem, smem] + in_specs + [
            pl.BlockSpec((1, 16, tq), lambda b, p, i: (b, 0, i)),
            pl.BlockSpec((1, seq, LANES), lambda b, p, i: (b, 0, 0))]
        args = [cfirst, clast, knorm] + args + [cumt, cum]
    return pl.pallas_call(
        functools.partial(_causal_attn_kernel, tq=tq, n_tiles=n_tiles, fox=fox, slab=slab,
                          long_body=not fox),
        grid=(nb, n_pairs, n_tiles),
        in_specs=in_specs,
        out_specs=pl.BlockSpec((1, tq, LANES), lambda b, p, i: (b, i, p)),
        out_shape=jax.ShapeDtypeStruct((nb, seq, n_pairs * LANES), BF16),
        scratch_shapes=[pltpu.VMEM((2, 1, tq), F32), pltpu.VMEM((2, VT_ROWS, tq), F32),
                        pltpu.VMEM((n_slots, 2, tq, tq), F32), pltpu.VMEM((n_slots, 2, 1, tq), F32)],
        compiler_params=_cparams(3),
        name="fox_attention" if fox else "mla_attention",
    )(*args)


def _mem_proj_kernel(mem_ref, w_ref, mk_ref, mvt_ref):
    mkv = jnp.dot(mem_ref[...].astype(BF16), w_ref[...], preferred_element_type=F32)
    mk_ref[0] = mkv[:, :MEM_WIDTH].astype(BF16)
    _store_values_transposed(mvt_ref, mkv[:, MEM_WIDTH:])


def _mem_proj(mem2d, w, nb):
    return pl.pallas_call(
        _mem_proj_kernel,
        grid=(nb,),
        in_specs=[pl.BlockSpec((N_MEM, D_MODEL), lambda b: (b, 0)),
                  pl.BlockSpec(w.shape, lambda b: (0, 0))],
        out_specs=[pl.BlockSpec((1, N_MEM, MEM_WIDTH), lambda b: (b, 0, 0)),
                   pl.BlockSpec((1, MEM_HEADS * VT_ROWS, N_MEM), lambda b: (b, 0, 0))],
        out_shape=[jax.ShapeDtypeStruct((nb, N_MEM, MEM_WIDTH), BF16),
                   jax.ShapeDtypeStruct((nb, MEM_HEADS * VT_ROWS, N_MEM), BF16)],
        compiler_params=_cparams(1),
        name="mem_proj",
    )(mem2d, w)


def _mem_attn_kernel(qt_ref, k_ref, vt_ref, o_ref):
    qt = qt_ref[0]
    kc = k_ref[0]
    for pair in range(MEM_HEADS // 2):
        qt_p = qt[pair * LANES:(pair + 1) * LANES]
        k_p = kc[:, pair * LANES:(pair + 1) * LANES]
        rowi = lax.broadcasted_iota(jnp.int32, qt_p.shape, 0)
        zero = jnp.zeros_like(qt_p)
        outs = []
        for h in range(2):
            mine = (rowi < HEAD_DIM) if h == 0 else (rowi >= HEAD_DIM)
            s = jnp.dot(k_p, jnp.where(mine, qt_p, zero), preferred_element_type=F32)
            p = jnp.exp2(s - jnp.max(s, axis=0, keepdims=True)).astype(BF16)
            r0 = (2 * pair + h) * VT_ROWS
            acc = jnp.dot(vt_ref[0, r0:r0 + VT_ROWS, :], p, preferred_element_type=F32)
            outs.append(acc[:HEAD_DIM] * (1.0 / acc[HEAD_DIM:HEAD_DIM + 1]))
        o_ref[0, :, pair * LANES:(pair + 1) * LANES] = (
            jnp.concatenate(outs, axis=0).T.astype(o_ref.dtype))


def _memory_attention(mqt, mk, mvt):
    nb, _, seq = mqt.shape
    tq = ATT_TQ
    return pl.pallas_call(
        _mem_attn_kernel,
        grid=(nb, seq // tq),
        in_specs=[pl.BlockSpec((1, MEM_WIDTH, tq), lambda b, i: (b, 0, i)),
                  pl.BlockSpec((1, N_MEM, MEM_WIDTH), lambda b, i: (b, 0, 0)),
                  pl.BlockSpec((1, MEM_HEADS * VT_ROWS, N_MEM), lambda b, i: (b, 0, 0))],
        out_specs=pl.BlockSpec((1, tq, MEM_WIDTH), lambda b, i: (b, i, 0)),
        out_shape=jax.ShapeDtypeStruct((nb, seq, MEM_WIDTH), BF16),
        compiler_params=_cparams(2),
        name="memory_attention",
    )(mqt, mk, mvt)


def _post_attn_kernel(x_ref, att_ref, memo_ref, wa_ref, wm_ref, g_ref, b_ref, wr_ref, br_ref,
                      x1_ref, x1p_ref, idx_ref, rank_ref, gate_ref, cnt_ref, carry_ref):
    t = pl.program_id(0)

    @pl.when(t == 0)
    def _():
        carry_ref[...] = jnp.zeros_like(carry_ref)

    tm = x_ref.shape[0]
    mix = (jnp.dot(att_ref[...], wa_ref[...], preferred_element_type=F32)
           + jnp.dot(memo_ref[...], wm_ref[...], preferred_element_type=F32))
    x1 = _layer_norm(DEEPNORM_ALPHA * x_ref[...] + mix, g_ref[...], b_ref[...])
    x1_ref[...] = x1
    x1b = x1.astype(BF16)
    x1p_ref[...] = _pack_bf16_pairs(x1b)
    logits = jnp.dot(x1b, wr_ref[...], preferred_element_type=F32) + br_ref[...]
    lane = lax.broadcasted_iota(jnp.int32, (tm, LANES), 1)
    work = jnp.where(lane < N_EXPERTS, logits, -jnp.inf)
    idxs, vals = [], []
    onehot = jnp.zeros((tm, LANES), F32)
    for _ in range(TOP_K):
        best = jnp.max(work, axis=1, keepdims=True)
        where_best = jnp.argmax(work, axis=1, keepdims=True).astype(jnp.int32)
        hit = lane == where_best
        onehot = jnp.where(hit, 1.0, onehot)
        work = jnp.where(hit, -jnp.inf, work)
        idxs.append(where_best)
        vals.append(best)
    exps = [jnp.exp(v - vals[0]) for v in vals]
    denom = exps[0] + exps[1] + exps[2] + exps[3]
    row = lax.broadcasted_iota(jnp.int32, (tm, tm), 0)
    col = lax.broadcasted_iota(jnp.int32, (tm, tm), 1)
    strict = jnp.where(row > col, 1.0, 0.0).astype(BF16)
    before = jnp.dot(strict, onehot.astype(BF16), preferred_element_type=F32) + carry_ref[...]
    idx_out = jnp.zeros((tm, LANES), F32)
    rank_out = jnp.zeros((tm, LANES), F32)
    gate_out = jnp.zeros((tm, LANES), F32)
    for r in range(TOP_K):
        rank_r = jnp.sum(jnp.where(lane == idxs[r], before, 0.0), axis=1, keepdims=True)
        idx_out = jnp.where(lane == r, idxs[r].astype(F32), idx_out)
        rank_out = jnp.where(lane == r, rank_r, rank_out)
        gate_out = jnp.where(lane == r, exps[r] / denom, gate_out)
    idx_ref[...] = idx_out.T[:8].astype(jnp.int32)
    rank_ref[...] = rank_out.T[:8].astype(jnp.int32)
    gate_ref[...] = gate_out
    total = carry_ref[...] + jnp.sum(onehot, axis=0, keepdims=True)
    carry_ref[...] = total
    cnt_ref[...] = jnp.broadcast_to(total, cnt_ref.shape)


def _post_attn(x2d, att, memo, wa, wm, g, b, wr, br):
    t_tokens = x2d.shape[0]
    tm = PROJ_TM
    row_spec = lambda width: pl.BlockSpec((tm, width), lambda i: (i, 0))
    full = lambda a: pl.BlockSpec(a.shape, lambda i: (0, 0))
    return pl.pallas_call(
        _post_attn_kernel,
        grid=(t_tokens // tm,),
        in_specs=[row_spec(D_MODEL), row_spec(att.shape[1]), row_spec(MEM_WIDTH),
                  full(wa), full(wm), full(g), full(b), full(wr), full(br)],
        out_specs=[row_spec(D_MODEL), row_spec(D_MODEL // 2),
                   pl.BlockSpec((8, tm), lambda i: (0, i)), pl.BlockSpec((8, tm), lambda i: (0, i)),
                   row_spec(LANES), pl.BlockSpec((8, LANES), lambda i: (0, 0))],
        out_shape=[jax.ShapeDtypeStruct((t_tokens, D_MODEL), F32),
                   jax.ShapeDtypeStruct((t_tokens, D_MODEL // 2), jnp.uint32),
                   jax.ShapeDtypeStruct((8, t_tokens), jnp.int32),
                   jax.ShapeDtypeStruct((8, t_tokens), jnp.int32),
                   jax.ShapeDtypeStruct((t_tokens, LANES), F32),
                   jax.ShapeDtypeStruct((8, LANES), F32)],
        scratch_shapes=[pltpu.VMEM((1, LANES), F32)],
        compiler_params=_cparams(1),
        name="outproj_ln_router",
    )(x2d, att, memo, wa, wm, g, b, wr, br)


def _expert_kernel(be_ref, nu_ref, x_ref, wgu_hbm, bgu_ref, wdn_hbm, bdn_ref, *rest,
                   n_chunks, blk_lo, n_blocks, layer, has_prev):
    o_ref, wgu_st, wdn_st, wgu_bf, wdn_bf, h_ref, sem = rest[1:] if has_prev else rest
    step = pl.program_id(0)
    blk = step + blk_lo
    live_end = jnp.minimum(blk_lo + pl.num_programs(0), nu_ref[0])
    expert = be_ref[blk]

    def fetch(e):
        return (pltpu.make_async_copy(wgu_hbm.at[layer, e], wgu_st, sem.at[0]),
                pltpu.make_async_copy(wdn_hbm.at[layer, e], wdn_st, sem.at[1]))

    @pl.when(((step == 0) | (expert != be_ref[jnp.maximum(blk - 1, 0)])) & (blk < live_end))
    def _():
        @pl.when(step == 0)
        def _():
            for copy in fetch(expert):
                copy.start()

        for copy in fetch(expert):
            copy.wait()
        wgu_bf[...] = wgu_st[...].astype(BF16)
        wdn_bf[...] = wdn_st[...].astype(BF16)
        nxt = lax.while_loop(
            lambda j: (j < live_end) & (be_ref[jnp.minimum(j, n_blocks - 1)] == expert),
            lambda j: j + 1, blk + 1)

        @pl.when(nxt < live_end)
        def _():
            for copy in fetch(be_ref[jnp.minimum(nxt, n_blocks - 1)]):
                copy.start()

    def mlp(rows):
        x = _unpack_bf16_pairs(x_ref[:rows]).astype(BF16)
        cw = D_EXPERT // n_chunks
        for c in range(n_chunks):
            gs = slice(c * cw, (c + 1) * cw)
            us = slice(D_EXPERT + c * cw, D_EXPERT + (c + 1) * cw)
            g = jnp.dot(x, wgu_bf[:, gs], preferred_element_type=F32) + bgu_ref[0, 0, :, gs]
            u = jnp.dot(x, wgu_bf[:, us], preferred_element_type=F32) + bgu_ref[0, 0, :, us]
            g = jnp.minimum(g, SWIGLU_LIMIT)
            u = jnp.clip(u, -SWIGLU_LIMIT, SWIGLU_LIMIT)
            h_ref[:rows, gs] = ((u + 1.0) * (g * jax.nn.sigmoid(SWIGLU_ALPHA * g))).astype(BF16)
        y = jnp.dot(h_ref[:rows], wdn_bf[...], preferred_element_type=F32) + bdn_ref[0, 0]
        return _pack_bf16_pairs(y.astype(BF16))

    @pl.when(blk < nu_ref[0])
    def _():
        o_ref[...] = mlp(o_ref.shape[0])

    @pl.when(blk >= nu_ref[0])
    def _():
        o_ref[...] = jnp.zeros_like(o_ref)


def _experts(block_expert, n_used, xb, blk_lo, p_rows, y_prev, layer, wgu, bgu, wdn, bdn):
    tm = MOE_TM
    has_prev = y_prev is not None
    b_map = lambda i, be, nu: (layer, be[i + blk_lo], 0, 0)
    in_specs = [pl.BlockSpec((tm, D_MODEL // 2), lambda i, be, nu: (i, 0)),
                pl.BlockSpec(memory_space=pl.ANY),
                pl.BlockSpec((1, 1, 1, 2 * D_EXPERT), b_map),
                pl.BlockSpec(memory_space=pl.ANY),
                pl.BlockSpec((1, 1, 1, D_MODEL), b_map)]
    args = [block_expert, n_used, xb, wgu, bgu, wdn, bdn]
    if has_prev:
        in_specs.append(pl.BlockSpec(memory_space=pl.ANY))
        args.append(y_prev)
    grid_spec = pltpu.PrefetchScalarGridSpec(
        num_scalar_prefetch=2,
        grid=(xb.shape[0] // tm,),
        in_specs=in_specs,
        out_specs=pl.BlockSpec((tm, D_MODEL // 2), lambda i, be, nu: (i + blk_lo, 0)),
        scratch_shapes=[pltpu.VMEM((D_MODEL, 2 * D_EXPERT), F32),
                        pltpu.VMEM((D_EXPERT, D_MODEL), F32),
                        pltpu.VMEM((D_MODEL, 2 * D_EXPERT), BF16),
                        pltpu.VMEM((D_EXPERT, D_MODEL), BF16),
                        pltpu.VMEM((tm, D_EXPERT), BF16),
                        pltpu.SemaphoreType.DMA((2,))],
    )
    return pl.pallas_call(
        functools.partial(_expert_kernel, n_chunks=4, blk_lo=blk_lo, n_blocks=p_rows // tm,
                          layer=layer, has_prev=has_prev),
        grid_spec=grid_spec,
        out_shape=jax.ShapeDtypeStruct((p_rows, D_MODEL // 2), jnp.uint32),
        input_output_aliases={len(args) - 1: 0} if has_prev else {},
        compiler_params=_cparams(1),
        name="experts",
    )(*args)


def _sc_move_rows(table_hbm, idx_v, out_hbm, base, n_chunks, rows_v, gsem, wsem):
    def gather(c, slot):
        off = pl.multiple_of(c * SC_CHUNK, SC_CHUNK)
        return pltpu.make_async_copy(table_hbm.at[idx_v.at[pl.ds(off, SC_CHUNK)]],
                                     rows_v.at[slot], gsem.at[slot])

    def put(c, slot):
        off = pl.multiple_of(c * SC_CHUNK, SC_CHUNK)
        return pltpu.make_async_copy(rows_v.at[slot], out_hbm.at[pl.ds(base + off, SC_CHUNK)],
                                     wsem.at[slot])

    gather(0, 0).start()
    gather(1, 1).start()

    @pl.loop(0, n_chunks, step=2)
    def _(c):
        for slot in range(2):
            gather(c + slot, slot).wait()
            put(c + slot, slot).start()
        for slot in range(2):
            put(c + slot, slot).wait()

            @pl.when(c + 2 + slot < n_chunks)
            def _():
                gather(c + 2 + slot, slot).start()


def _sc_gather_rows(table, idx):
    n_idx = idx.shape[0]
    width = table.shape[1]
    per_worker = n_idx // SC_WORKERS
    n_chunks = per_worker // SC_CHUNK
    assert per_worker * SC_WORKERS == n_idx and n_chunks * SC_CHUNK == per_worker and n_chunks % 2 == 0
    mesh = plsc.VectorSubcoreMesh(core_axis_name="c", subcore_axis_name="s",
                                  num_cores=SC_CORES, num_subcores=SC_SUBCORES)

    def body(table_hbm, idx_hbm, out_hbm, idx_v, rows_v, gsem, wsem):
        wid = lax.axis_index("s") * SC_CORES + lax.axis_index("c")
        base = wid * per_worker
        pltpu.sync_copy(idx_hbm.at[pl.ds(base, per_worker)], idx_v)
        _sc_move_rows(table_hbm, idx_v, out_hbm, base, n_chunks, rows_v, gsem, wsem)

    return pl.kernel(
        body,
        out_type=jax.ShapeDtypeStruct((n_idx, width), table.dtype),
        mesh=mesh,
        scratch_types=[pltpu.VMEM((per_worker,), jnp.int32),
                       pltpu.VMEM((2, SC_CHUNK, width), table.dtype),
                       pltpu.SemaphoreType.DMA((2,)),
                       pltpu.SemaphoreType.DMA((2,))],
        name="sc_gather_rows",
    )(table, idx)


def _sc_dispatch_rows(table, dest, slot_lo, n_slots):
    n_tok, width = table.shape
    n_pairs = dest.shape[0]
    per_worker = n_slots // SC_WORKERS
    n_chunks = per_worker // SC_CHUNK
    assert per_worker * SC_WORKERS == n_slots and n_chunks * SC_CHUNK == per_worker and n_chunks % 2 == 0
    n_scan = n_pairs // SC_SCAN
    assert n_scan * SC_SCAN == n_pairs
    mesh = plsc.VectorSubcoreMesh(core_axis_name="c", subcore_axis_name="s",
                                  num_cores=SC_CORES, num_subcores=SC_SUBCORES)

    def body(table_hbm, dest_hbm, out_hbm, tok_v, dest_v, rows_v, gsem, wsem):
        wid = lax.axis_index("s") * SC_CORES + lax.axis_index("c")
        base = wid * per_worker
        first = slot_lo + base
        lane = lax.iota(jnp.int32, SC_LANES)

        @pl.loop(0, per_worker // SC_LANES)
        def _(i):
            off = pl.multiple_of(i * SC_LANES, SC_LANES)
            tok_v[pl.ds(off, SC_LANES)] = lax.rem(first + off + lane, n_tok)

        @pl.loop(0, n_scan)
        def _(g):
            goff = pl.multiple_of(g * SC_SCAN, SC_SCAN)
            pltpu.sync_copy(dest_hbm.at[pl.ds(goff, SC_SCAN)], dest_v)

            @plsc.parallel_loop(0, SC_SCAN // SC_LANES, unroll=8)
            def _(i):
                off = pl.multiple_of(i * SC_LANES, SC_LANES)
                local = dest_v[pl.ds(off, SC_LANES)] - first
                mine = (local >= 0) & (local < per_worker)
                pair = goff + off + lane
                plsc.store_scatter(tok_v, [jnp.where(mine, local, 0)], lax.rem(pair, n_tok), mask=mine)

        _sc_move_rows(table_hbm, tok_v, out_hbm, base, n_chunks, rows_v, gsem, wsem)

    return pl.kernel(
        body,
        out_type=jax.ShapeDtypeStruct((n_slots, width), table.dtype),
        mesh=mesh,
        scratch_types=[pltpu.VMEM((per_worker,), jnp.int32),
                       pltpu.VMEM((SC_SCAN,), jnp.int32),
                       pltpu.VMEM((2, SC_CHUNK, width), table.dtype),
                       pltpu.SemaphoreType.DMA((2,)),
                       pltpu.SemaphoreType.DMA((2,))],
        compiler_params=pltpu.CompilerParams(needs_layout_passes=False),
        name="sc_dispatch_rows",
    )(table, dest)


def _combine_kernel(x1_ref, yg_ref, gate_ref, g_ref, b_ref, *rest):
    o_ref = rest[-1]
    gates = gate_ref[...]
    ffn = _unpack_bf16_pairs(yg_ref[0]) * gates[:, 0:1]
    for r in range(1, TOP_K):
        ffn = ffn + _unpack_bf16_pairs(yg_ref[r]) * gates[:, r:r + 1]
    o_ref[...] = _layer_norm(DEEPNORM_ALPHA * x1_ref[...] + ffn, g_ref[...], b_ref[...])


def _combine(x1, yg, gates, g, b, tok_lo, out_prev):
    t_tokens = x1.shape[0]
    tm = COMBINE_TM
    tile_lo = tok_lo // tm
    in_specs = [pl.BlockSpec((tm, D_MODEL), lambda i: (i + tile_lo, 0)),
                pl.BlockSpec((TOP_K, tm, D_MODEL // 2), lambda i: (0, i, 0)),
                pl.BlockSpec((tm, LANES), lambda i: (i + tile_lo, 0)),
                pl.BlockSpec((1, D_MODEL), lambda i: (0, 0)),
                pl.BlockSpec((1, D_MODEL), lambda i: (0, 0))]
    args = [x1, yg, gates, g, b]
    if out_prev is not None:
        in_specs.append(pl.BlockSpec(memory_space=pl.ANY))
        args.append(out_prev)
    return pl.pallas_call(
        _combine_kernel,
        grid=(yg.shape[1] // tm,),
        in_specs=in_specs,
        out_specs=pl.BlockSpec((tm, D_MODEL), lambda i: (i + tile_lo, 0)),
        out_shape=jax.ShapeDtypeStruct((t_tokens, D_MODEL), F32),
        input_output_aliases={len(args) - 1: 0} if out_prev is not None else {},
        compiler_params=_cparams(1),
        name="combine_ln",
    )(*args)


def _moe(x1, x1p, idx, rank, gates, counts, layer, wgu, bgu, wdn, bdn, g, b):
    t_tokens = x1.shape[0]
    tm = MOE_TM
    counts = counts[0, :N_EXPERTS].astype(jnp.int32)
    padded = ((counts + tm - 1) // tm) * tm
    pend = jnp.cumsum(padded)
    pstart = pend - padded
    dest = rank[:TOP_K]
    for e in range(N_EXPERTS):
        dest = dest + jnp.where(idx[:TOP_K] == e, pstart[e], 0)
    n_blocks = (t_tokens * TOP_K) // tm + N_EXPERTS
    p_rows = n_blocks * tm
    block_start = jnp.arange(n_blocks, dtype=jnp.int32) * tm
    block_expert = jnp.minimum(
        jnp.sum((pend[None, :] <= block_start[:, None]).astype(jnp.int32), axis=1), N_EXPERTS - 1)
    n_used = (pend[-1:] // tm).astype(jnp.int32)
    y = None
    for lo, hi in ((0, n_blocks // MOE_SPLIT), (n_blocks // MOE_SPLIT, n_blocks)):
        xb = _sc_dispatch_rows(x1p, dest.reshape(-1), lo * tm, (hi - lo) * tm)
        y = _experts(block_expert, n_used, xb, lo, p_rows, y, layer, wgu, bgu, wdn, bdn)
    out = None
    for lo, hi in ((0, t_tokens // MOE_SPLIT), (t_tokens // MOE_SPLIT, t_tokens)):
        yg = _sc_gather_rows(y, dest[:, lo:hi].reshape(-1)).reshape(TOP_K, hi - lo, D_MODEL // 2)
        out = _combine(x1, yg, gates, g, b, lo, out)
    return out


def _row(v, width=None):
    v = v.astype(F32).reshape(1, -1)
    if width is not None and v.shape[1] < width:
        v = jnp.pad(v, ((0, 0), (0, width - v.shape[1])))
    return v


def _pad_cols(w, width):
    return jnp.pad(w, ((0, 0), (0, width - w.shape[1])))


def kernel(x, mem, positions, a_w_in, a_b_f, a_w_out, b_w_in, b_g_q, b_w_uq, b_w_out,
           kv_w_dkv, kv_g, kv_w_ukv, mem_w_kv, ln_g, ln_b,
           moe_w_r, moe_b_r, moe_w_gu, moe_b_gu, moe_w_dn, moe_b_dn):
    nb, seq, d = x.shape
    t_tokens = nb * seq
    n_a = a_w_in.shape[0]
    x2d = x.reshape(t_tokens, d)
    mem2d = mem.reshape(nb * N_MEM, d)
    pos2d = positions.reshape(1, t_tokens)
    half = QK_ROPE // 2
    invf = (ROPE_THETA ** (-jnp.arange(half, dtype=F32) * 2.0 / QK_ROPE)).reshape(half, 1)

    shared_kv = None
    for l in range(DEPTH):
        mk, mvt = _mem_proj(mem2d, mem_w_kv[l].astype(BF16), nb)
        if l < n_a:
            w_in = a_w_in[l]
            w = jnp.concatenate([w_in[:, :3 * FOX_WIDTH],
                                 _pad_cols(w_in[:, 3 * FOX_WIDTH:3 * FOX_WIDTH + FOX_HEADS], LANES),
                                 w_in[:, 3 * FOX_WIDTH + FOX_HEADS:]], axis=1).astype(BF16)
            qt, k, vt, mqt, cum, cumt, cfirst, clast, knorm = _fox_proj(
                x2d, w, _row(a_b_f[l], LANES), seq)
            stats = [s.reshape(-1, LANES) for s in (cfirst, clast, knorm)]
            att = _causal_attention(qt, k.reshape(nb, seq, -1), vt,
                                    [cum.reshape(nb, seq, LANES), cumt] + stats, slab=False)
            w_out = a_w_out[l]
        else:
            bl = l - n_a
            w = jnp.concatenate([b_w_in[bl], kv_w_dkv[:, :KV_LORA],
                                 jnp.zeros((d, QK_NOPE), F32), kv_w_dkv[:, KV_LORA:],
                                 jnp.zeros((d, LANES - QK_NOPE - QK_ROPE), F32)], axis=1).astype(BF16)
            wuq = b_w_uq[bl].reshape(Q_LORA, MLA_HEADS, QK_NOPE + QK_ROPE)
            wuq = jnp.pad(wuq, ((0, 0), (0, 0), (0, MLA_SLAB - QK_NOPE - QK_ROPE)))
            wuq = wuq.reshape(Q_LORA, MLA_HEADS * MLA_SLAB).astype(BF16)
            wukv = kv_w_ukv.reshape(KV_LORA, MLA_HEADS, QK_NOPE + V_DIM)
            wk = jnp.pad(wukv[:, :, :QK_NOPE], ((0, 0), (0, 0), (0, MLA_SLAB - QK_NOPE)))
            wk = wk.reshape(KV_LORA, MLA_HEADS * MLA_SLAB).astype(BF16)
            wv = wukv[:, :, QK_NOPE:].reshape(KV_LORA, MLA_V_WIDTH).astype(BF16)
            qt, mqt, k_new, vt_new = _mla_proj(x2d, pos2d, invf, w, _row(b_g_q[bl]), wuq,
                                              _row(kv_g), wk, wv, seq)
            if shared_kv is None:
                shared_kv = (k_new.reshape(nb, seq, -1), vt_new)
            att = _causal_attention(qt, shared_kv[0], shared_kv[1], None, slab=True)
            w_out = b_w_out[bl]
        memo = _memory_attention(mqt, mk, mvt)
        n_att = w_out.shape[0] - MEM_WIDTH
        x1, x1p, idx, rank, gates, counts = _post_attn(
            x2d, att.reshape(t_tokens, -1), memo.reshape(t_tokens, MEM_WIDTH),
            w_out[:n_att].astype(BF16), w_out[n_att:].astype(BF16),
            _row(ln_g[l, 0]), _row(ln_b[l, 0]),
            _pad_cols(moe_w_r[l], LANES).astype(BF16), _row(moe_b_r[l], LANES))
        x2d = _moe(x1, x1p, idx, rank, gates, counts, l,
                   moe_w_gu, moe_b_gu.reshape(DEPTH, N_EXPERTS, 1, -1),
                   moe_w_dn, moe_b_dn.reshape(DEPTH, N_EXPERTS, 1, -1),
                   _row(ln_g[l, 1]), _row(ln_b[l, 1]))
    return x2d.reshape(nb, seq, d)
```

```python
import functools
import math

import jax
import jax.numpy as jnp
from jax import lax
from jax.experimental import pallas as pl
from jax.experimental.pallas import tpu as pltpu
from jax.experimental.pallas import tpu_sc as plsc

F32 = jnp.float32
BF16 = jnp.bfloat16

D_MODEL = 1024
DEPTH = 2
N_MEM = 256
HEAD_DIM = 64
FOX_HEADS = 12
MEM_HEADS = 4
MLA_HEADS = 12
Q_LORA = 384
KV_LORA = 256
QK_NOPE = 64
QK_ROPE = 32
V_DIM = 64
ROPE_THETA = 10000.0
N_EXPERTS = 32
TOP_K = 4
D_EXPERT = D_MODEL
SWIGLU_LIMIT = 7.0
SWIGLU_ALPHA = 1.702
LN_EPS = 1e-5
RMS_EPS = 1e-6
NEG_INF = -1e30
DEEPNORM_ALPHA = (2 * DEPTH) ** 0.25
FOX_WIDTH = FOX_HEADS * HEAD_DIM
MEM_WIDTH = MEM_HEADS * HEAD_DIM
MLA_V_WIDTH = MLA_HEADS * V_DIM

LANES = 128
LOG2E = math.log2(math.e)
VMEM_LIMIT = 48 * 1024 * 1024

PROJ_TM = 512
ATT_TQ = 512
FOX_TQ = 512
MOE_TM = 512
COMBINE_TM = 512
MOE_SPLIT = 4
MLA_SLAB = LANES
VT_ROWS = HEAD_DIM + 16
SC_CORES = 2
SC_SUBCORES = 16
SC_WORKERS = SC_CORES * SC_SUBCORES
SC_CHUNK = 32
SC_LANES = 16
SC_SCAN = 8192


def _cparams(n_axes):
    return pltpu.CompilerParams(dimension_semantics=("arbitrary",) * n_axes,
                                vmem_limit_bytes=VMEM_LIMIT)


def _split3(x):
    hi = x.astype(BF16)
    r1 = x - hi.astype(F32)
    mid = r1.astype(BF16)
    lo = (r1 - mid.astype(F32)).astype(BF16)
    return hi, mid, lo


def _pack_bf16_pairs(v):
    bits = pltpu.bitcast(v.astype(F32), jnp.uint32)
    half = v.shape[1] // 2
    return (bits[:, :half] >> 16) | bits[:, half:]


def _unpack_bf16_pairs(words):
    return jnp.concatenate([pltpu.bitcast(words << 16, F32),
                            pltpu.bitcast(words & jnp.uint32(0xFFFF0000), F32)], axis=1)


def _layer_norm(y, g, b):
    mu = jnp.mean(y, axis=-1, keepdims=True)
    yc = y - mu
    var = jnp.mean(yc * yc, axis=-1, keepdims=True)
    return yc * lax.rsqrt(var + LN_EPS) * g + b


def _rms_norm(y, g):
    return y * lax.rsqrt(jnp.mean(y * y, axis=-1, keepdims=True) + RMS_EPS) * g


def _store_transposed(dst_ref, val):
    for s in range(val.shape[1] // LANES):
        sl = slice(s * LANES, (s + 1) * LANES)
        dst_ref[0, sl, :] = val[:, sl].T.astype(dst_ref.dtype)


def _store_values_transposed(vt_ref, val):
    tm = val.shape[0]
    ones = jnp.ones((VT_ROWS - HEAD_DIM, tm), vt_ref.dtype)
    for s in range(val.shape[1] // LANES):
        pair_t = val[:, s * LANES:(s + 1) * LANES].T.astype(vt_ref.dtype)
        for h in range(2):
            r0 = (2 * s + h) * VT_ROWS
            vt_ref[0, r0:r0 + HEAD_DIM, :] = pair_t[h * HEAD_DIM:(h + 1) * HEAD_DIM]
            vt_ref[0, r0 + HEAD_DIM:r0 + VT_ROWS, :] = ones


def _fox_proj_kernel(x_ref, w_ref, bf_ref, seg_ref, qt_ref, k_ref, vt_ref, mqt_ref, cum_ref,
                     cumt_ref, cfirst_ref, clast_ref, knorm_ref, carry_ref, knmax_ref, *,
                     tiles_per_batch):
    t = pl.program_id(0)

    @pl.when(t % tiles_per_batch == 0)
    def _():
        carry_ref[...] = jnp.zeros_like(carry_ref)
        knmax_ref[...] = jnp.zeros_like(knmax_ref)

    tm = x_ref.shape[0]
    proj = jnp.dot(x_ref[...].astype(BF16), w_ref[...], preferred_element_type=F32)
    qscale = HEAD_DIM ** -0.5 * LOG2E
    _store_transposed(qt_ref, proj[:, :FOX_WIDTH] * qscale)
    kb = proj[:, FOX_WIDTH:2 * FOX_WIDTH].astype(BF16)
    k_ref[...] = kb
    _store_values_transposed(vt_ref, proj[:, 2 * FOX_WIDTH:3 * FOX_WIDTH])
    kf = kb.astype(F32)
    ksq = jnp.dot((kf * kf).astype(BF16), seg_ref[...], preferred_element_type=F32)
    n_sub = tm // FOX_TQ
    for sub in range(n_sub):
        tile_max = jnp.sqrt(jnp.max(ksq[sub * FOX_TQ:(sub + 1) * FOX_TQ], axis=0, keepdims=True))
        knmax_ref[...] = jnp.maximum(knmax_ref[...], tile_max)
        knorm_ref[sub] = knmax_ref[...]
    f = proj[:, 3 * FOX_WIDTH:3 * FOX_WIDTH + LANES] + bf_ref[...]
    _store_transposed(mqt_ref, proj[:, 3 * FOX_WIDTH + LANES:] * qscale)
    log_f = jnp.minimum(f, 0.0) - jnp.log1p(jnp.exp(-jnp.abs(f)))
    row = lax.broadcasted_iota(jnp.int32, (tm, tm), 0)
    col = lax.broadcasted_iota(jnp.int32, (tm, tm), 1)
    tri = jnp.where(row >= col, 1.0, 0.0).astype(BF16)
    hi, mid, lo = _split3(log_f)
    cum = (jnp.dot(tri, hi, preferred_element_type=F32)
           + jnp.dot(tri, mid, preferred_element_type=F32)
           + jnp.dot(tri, lo, preferred_element_type=F32)) + carry_ref[...]
    carry_ref[...] = cum[tm - 1:tm, :]
    cum2 = cum * LOG2E
    cum_ref[...] = cum2
    cumt_ref[0] = cum2.T[:16, :]
    for sub in range(n_sub):
        cfirst_ref[sub] = cum2[sub * FOX_TQ:sub * FOX_TQ + 1, :]
        clast_ref[sub] = cum2[(sub + 1) * FOX_TQ - 1:(sub + 1) * FOX_TQ, :]


def _fox_proj(x2d, w, bf, seq):
    t_tokens = x2d.shape[0]
    tm = PROJ_TM
    nb = t_tokens // seq
    n = w.shape[1]
    tiles_per_batch = seq // tm
    n_tiles = t_tokens // tm
    seg = (jnp.arange(FOX_WIDTH)[:, None] // HEAD_DIM == jnp.arange(LANES)[None, :]).astype(BF16)
    row_spec = lambda width: pl.BlockSpec((tm, width), lambda i: (i, 0))
    t_spec = lambda rows: pl.BlockSpec(
        (1, rows, tm), lambda i: (i // tiles_per_batch, 0, i % tiles_per_batch))
    n_sub = tm // FOX_TQ
    stat_spec = pl.BlockSpec((n_sub, 1, LANES), lambda i: (i, 0, 0))
    stat_shape = jax.ShapeDtypeStruct((n_tiles * n_sub, 1, LANES), F32)
    return pl.pallas_call(
        functools.partial(_fox_proj_kernel, tiles_per_batch=tiles_per_batch),
        grid=(n_tiles,),
        in_specs=[row_spec(D_MODEL),
                  pl.BlockSpec((D_MODEL, n), lambda i: (0, 0)),
                  pl.BlockSpec((1, LANES), lambda i: (0, 0)),
                  pl.BlockSpec((FOX_WIDTH, LANES), lambda i: (0, 0))],
        out_specs=[t_spec(FOX_WIDTH), row_spec(FOX_WIDTH), t_spec(FOX_HEADS * VT_ROWS),
                   t_spec(MEM_WIDTH), row_spec(LANES), t_spec(16),
                   stat_spec, stat_spec, stat_spec],
        out_shape=[jax.ShapeDtypeStruct((nb, FOX_WIDTH, seq), BF16),
                   jax.ShapeDtypeStruct((t_tokens, FOX_WIDTH), BF16),
                   jax.ShapeDtypeStruct((nb, FOX_HEADS * VT_ROWS, seq), BF16),
                   jax.ShapeDtypeStruct((nb, MEM_WIDTH, seq), BF16),
                   jax.ShapeDtypeStruct((t_tokens, LANES), F32),
                   jax.ShapeDtypeStruct((nb, 16, seq), F32),
                   stat_shape, stat_shape, stat_shape],
        scratch_shapes=[pltpu.VMEM((1, LANES), F32), pltpu.VMEM((1, LANES), F32)],
        compiler_params=_cparams(1),
        name="fox_proj",
    )(x2d, w, bf, seg)


def _rope_slab_t(slab_t, cos_t, sin_t):
    half = QK_ROPE // 2
    x1 = slab_t[QK_NOPE:QK_NOPE + half]
    x2 = slab_t[QK_NOPE + half:QK_NOPE + QK_ROPE]
    return jnp.concatenate([slab_t[:QK_NOPE], x1 * cos_t - x2 * sin_t, x1 * sin_t + x2 * cos_t,
                            slab_t[QK_NOPE + QK_ROPE:]], axis=0)


def _mla_proj_kernel(x_ref, pos_ref, invf_ref, w_ref, gq_ref, wuq_ref, gkv_ref, wk_ref, wv_ref,
                     qt_ref, mqt_ref, k_ref, vt_ref):
    proj = jnp.dot(x_ref[...].astype(BF16), w_ref[...], preferred_element_type=F32)
    ang_t = invf_ref[...] * pos_ref[...].astype(F32)
    cos_t = jnp.cos(ang_t)
    sin_t = jnp.sin(ang_t)
    c_q = _rms_norm(proj[:, :Q_LORA], gq_ref[...])
    _store_transposed(mqt_ref, proj[:, Q_LORA:Q_LORA + MEM_WIDTH] * (HEAD_DIM ** -0.5 * LOG2E))
    kv_off = Q_LORA + MEM_WIDTH
    c_kv = _rms_norm(proj[:, kv_off:kv_off + KV_LORA], gkv_ref[...])
    kr = _rope_slab_t(proj[:, kv_off + KV_LORA:].T, cos_t, sin_t).T
    q = jnp.dot(c_q.astype(BF16), wuq_ref[...], preferred_element_type=F32)
    kn = jnp.dot(c_kv.astype(BF16), wk_ref[...], preferred_element_type=F32)
    qscale = (QK_NOPE + QK_ROPE) ** -0.5 * LOG2E
    for h in range(MLA_HEADS):
        sl = slice(h * MLA_SLAB, (h + 1) * MLA_SLAB)
        qt_ref[0, sl, :] = (_rope_slab_t(q[:, sl].T, cos_t, sin_t) * qscale).astype(BF16)
        k_ref[:, sl] = (kn[:, sl] + kr).astype(BF16)
    _store_values_transposed(vt_ref, jnp.dot(c_kv.astype(BF16), wv_ref[...],
                                             preferred_element_type=F32))


def _mla_proj(x2d, pos2d, invf, w, gq, wuq, gkv, wk, wv, seq):
    t_tokens = x2d.shape[0]
    tm = PROJ_TM
    nb = t_tokens // seq
    tiles_per_batch = seq // tm
    row_spec = lambda width: pl.BlockSpec((tm, width), lambda i: (i, 0))
    t_spec = lambda rows: pl.BlockSpec(
        (1, rows, tm), lambda i: (i // tiles_per_batch, 0, i % tiles_per_batch))
    full = lambda a: pl.BlockSpec(a.shape, lambda i: (0, 0))
    slabs = MLA_HEADS * MLA_SLAB
    return pl.pallas_call(
        _mla_proj_kernel,
        grid=(t_tokens // tm,),
        in_specs=[row_spec(D_MODEL), pl.BlockSpec((1, tm), lambda i: (0, i)), full(invf), full(w),
                  full(gq), full(wuq), full(gkv), full(wk), full(wv)],
        out_specs=[t_spec(slabs), t_spec(MEM_WIDTH), row_spec(slabs),
                   t_spec(MLA_HEADS * VT_ROWS)],
        out_shape=[jax.ShapeDtypeStruct((nb, slabs, seq), BF16),
                   jax.ShapeDtypeStruct((nb, MEM_WIDTH, seq), BF16),
                   jax.ShapeDtypeStruct((t_tokens, slabs), BF16),
                   jax.ShapeDtypeStruct((nb, MLA_HEADS * VT_ROWS, seq), BF16)],
        compiler_params=_cparams(1),
        name="mla_proj",
    )(x2d, pos2d, invf, w, gq, wuq, gkv, wk, wv)


SKIP_LOG2 = -160.0
NORM_SLACK = 1.02


def _causal_attn_kernel(*refs, tq, n_tiles, fox, slab, long_body):
    if fox:
        (cf_ref, cl_ref, kn_ref, qt_ref, k_ref, vt_ref, cq_ref, ck_ref,
         o_ref, m_ref, acc_ref, s_ref, cmax_ref) = refs
    else:
        qt_ref, k_ref, vt_ref, o_ref, m_ref, acc_ref, s_ref, cmax_ref = refs
    b = pl.program_id(0)
    pair = pl.program_id(1)
    i = pl.program_id(2)
    m_ref[...] = jnp.full(m_ref.shape, NEG_INF, F32)
    acc_ref[...] = jnp.zeros(acc_ref.shape, F32)
    qt = qt_ref[0]
    if slab:
        qth = [qt[:MLA_SLAB], qt[MLA_SLAB:]]
    else:
        rowi = lax.broadcasted_iota(jnp.int32, qt.shape, 0)
        zero = jnp.zeros_like(qt)
        qth = [jnp.where(rowi < HEAD_DIM, qt, zero), jnp.where(rowi >= HEAD_DIM, qt, zero)]
    if fox:
        lane = lax.broadcasted_iota(jnp.int32, (1, LANES), 1)
        cq = [cq_ref[0, pl.ds(2 * pair + h, 1), :] for h in range(2)]

    def logits_to(slot, j):
        off = pl.multiple_of(j * tq, tq)
        kc = k_ref[0, pl.ds(off, tq), :]
        for h in range(2):
            kh = kc[:, h * MLA_SLAB:(h + 1) * MLA_SLAB] if slab else kc
            s = jnp.dot(kh, qth[h], preferred_element_type=F32)
            if fox:
                ck_blk = ck_ref[0, pl.ds(off, tq), :]
                ck = jnp.sum(jnp.where(lane == 2 * pair + h, ck_blk, 0.0), axis=1, keepdims=True)
                s = s + cq[h] - ck
            s_ref[slot, h] = s
            cmax_ref[slot, h] = jnp.max(s, axis=0, keepdims=True)

    def softmax_pv(slot, j, causal_mask):
        off = pl.multiple_of(j * tq, tq)
        for h in range(2):
            s = s_ref[slot, h]
            if causal_mask is not None:
                s = jnp.where(causal_mask, s, NEG_INF)
                chunk_max = jnp.max(s, axis=0, keepdims=True)
            else:
                chunk_max = cmax_ref[slot, h]
            m_prev = m_ref[h]
            m_new = jnp.maximum(m_prev, chunk_max)
            alpha = jnp.exp2(m_prev - m_new)
            p = jnp.exp2(s - m_new).astype(BF16)
            vth = vt_ref[0, h * VT_ROWS:(h + 1) * VT_ROWS, pl.ds(off, tq)]
            acc_ref[h] = acc_ref[h] * alpha + jnp.dot(vth, p, preferred_element_type=F32)
            m_ref[h] = m_new

    if fox:
        base = b * n_tiles
        qn = []
        for h in range(2):
            qf = qth[h].astype(F32)
            qn.append(jnp.sqrt(jnp.max(jnp.sum(qf * qf, axis=0, keepdims=True))) * NORM_SLACK)

        top = [2.0 * qn[h] * kn_ref[base + i, 2 * pair + h] + cf_ref[base + i, 2 * pair + h]
               for h in range(2)]

        def live(j):
            ub = [top[h] - cl_ref[base + jnp.maximum(j, 0), 2 * pair + h] for h in range(2)]
            return (j >= 0) & (jnp.maximum(ub[0], ub[1]) > SKIP_LOG2)

        j0 = lax.while_loop(live, lambda j: j - 1, i - 1) + 1
    else:
        j0 = 0
    n_before = i - j0

    def causal_mask():
        key = lax.broadcasted_iota(jnp.int32, (tq, tq), 0)
        qry = lax.broadcasted_iota(jnp.int32, (tq, tq), 1)
        return key <= qry

    def pipelined():
        odd = (n_before % 2) == 1

        @pl.when(odd)
        def _():
            logits_to(1, j0)
            logits_to(0, j0 + 1)
            softmax_pv(1, j0, None)

        @pl.when(jnp.logical_not(odd))
        def _():
            logits_to(0, j0)

        def two_steps(j):
            logits_to(1, j + 1)
            softmax_pv(0, j, None)
            logits_to(0, j + 2)
            softmax_pv(1, j + 1, None)

        j_even = j0 + (n_before % 2)
        n_two = n_before // 2
        if long_body:
            @pl.when((n_two % 2) == 1)
            def _():
                two_steps(j_even)

            j_quad = j_even + 2 * (n_two % 2)

            def body(t, carry):
                two_steps(j_quad + 4 * t)
                two_steps(j_quad + 4 * t + 2)
                return carry

            lax.fori_loop(0, n_two // 2, body, 0)
        else:
            def body(t, carry):
                two_steps(j_even + 2 * t)
                return carry

            lax.fori_loop(0, n_two, body, 0)
        softmax_pv(0, i, causal_mask())

    if fox:
        @pl.when(n_before == 1)
        def _():
            logits_to(1, j0)
            logits_to(0, i)
            softmax_pv(1, j0, None)
            softmax_pv(0, i, causal_mask())

        @pl.when(n_before != 1)
        def _():
            pipelined()
    else:
        pipelined()
    out_t = jnp.concatenate(
        [acc_ref[h, :HEAD_DIM] * (1.0 / acc_ref[h, HEAD_DIM:HEAD_DIM + 1]) for h in range(2)],
        axis=0)
    o_ref[0] = out_t.T.astype(o_ref.dtype)


def _causal_attention(qt, k, vt, fox_args, *, slab):
    nb, seq, _ = k.shape
    fox = fox_args is not None
    tq = FOX_TQ if fox else ATT_TQ
    n_tiles = seq // tq
    n_pairs = FOX_HEADS // 2
    rows = 2 * MLA_SLAB if slab else LANES
    in_specs = [pl.BlockSpec((1, rows, tq), lambda b, p, i: (b, p, i)),
                pl.BlockSpec((1, seq, rows), lambda b, p, i: (b, 0, p)),
                pl.BlockSpec((1, 2 * VT_ROWS, seq), lambda b, p, i: (b, p, 0))]
    args = [qt, k, vt]
    if fox:
        cum, cumt, cfirst, clast, knorm = fox_args
        smem = pl.BlockSpec(memory_space=pltpu.SMEM)
        in_specs = [smem, smem, smem] + in_specs + [
            pl.BlockSpec((1, 16, tq), lambda b, p, i: (b, 0, i)),
            pl.BlockSpec((1, seq, LANES), lambda b, p, i: (b, 0, 0))]
        args = [cfirst, clast, knorm] + args + [cumt, cum]
    return pl.pallas_call(
        functools.partial(_causal_attn_kernel, tq=tq, n_tiles=n_tiles, fox=fox, slab=slab,
                          long_body=not fox),
        grid=(nb, n_pairs, n_tiles),
        in_specs=in_specs,
        out_specs=pl.BlockSpec((1, tq, LANES), lambda b, p, i: (b, i, p)),
        out_shape=jax.ShapeDtypeStruct((nb, seq, n_pairs * LANES), BF16),
        scratch_shapes=[pltpu.VMEM((2, 1, tq), F32), pltpu.VMEM((2, VT_ROWS, tq), F32),
                        pltpu.VMEM((2, 2, tq, tq), F32), pltpu.VMEM((2, 2, 1, tq), F32)],
        compiler_params=_cparams(3),
        name="fox_attention" if fox else "mla_attention",
    )(*args)


def _mem_proj_kernel(mem_ref, w_ref, mk_ref, mvt_ref):
    mkv = jnp.dot(mem_ref[...].astype(BF16), w_ref[...], preferred_element_type=F32)
    mk_ref[0] = mkv[:, :MEM_WIDTH].astype(BF16)
    _store_values_transposed(mvt_ref, mkv[:, MEM_WIDTH:])


def _mem_proj(mem2d, w, nb):
    return pl.pallas_call(
        _mem_proj_kernel,
        grid=(nb,),
        in_specs=[pl.BlockSpec((N_MEM, D_MODEL), lambda b: (b, 0)),
                  pl.BlockSpec(w.shape, lambda b: (0, 0))],
        out_specs=[pl.BlockSpec((1, N_MEM, MEM_WIDTH), lambda b: (b, 0, 0)),
                   pl.BlockSpec((1, MEM_HEADS * VT_ROWS, N_MEM), lambda b: (b, 0, 0))],
        out_shape=[jax.ShapeDtypeStruct((nb, N_MEM, MEM_WIDTH), BF16),
                   jax.ShapeDtypeStruct((nb, MEM_HEADS * VT_ROWS, N_MEM), BF16)],
        compiler_params=_cparams(1),
        name="mem_proj",
    )(mem2d, w)


def _mem_attn_kernel(qt_ref, k_ref, vt_ref, o_ref):
    qt = qt_ref[0]
    kc = k_ref[0]
    for pair in range(MEM_HEADS // 2):
        qt_p = qt[pair * LANES:(pair + 1) * LANES]
        k_p = kc[:, pair * LANES:(pair + 1) * LANES]
        rowi = lax.broadcasted_iota(jnp.int32, qt_p.shape, 0)
        zero = jnp.zeros_like(qt_p)
        outs = []
        for h in range(2):
            mine = (rowi < HEAD_DIM) if h == 0 else (rowi >= HEAD_DIM)
            s = jnp.dot(k_p, jnp.where(mine, qt_p, zero), preferred_element_type=F32)
            p = jnp.exp2(s - jnp.max(s, axis=0, keepdims=True)).astype(BF16)
            r0 = (2 * pair + h) * VT_ROWS
            acc = jnp.dot(vt_ref[0, r0:r0 + VT_ROWS, :], p, preferred_element_type=F32)
            outs.append(acc[:HEAD_DIM] * (1.0 / acc[HEAD_DIM:HEAD_DIM + 1]))
        o_ref[0, :, pair * LANES:(pair + 1) * LANES] = (
            jnp.concatenate(outs, axis=0).T.astype(o_ref.dtype))


def _memory_attention(mqt, mk, mvt):
    nb, _, seq = mqt.shape
    tq = ATT_TQ
    return pl.pallas_call(
        _mem_attn_kernel,
        grid=(nb, seq // tq),
        in_specs=[pl.BlockSpec((1, MEM_WIDTH, tq), lambda b, i: (b, 0, i)),
                  pl.BlockSpec((1, N_MEM, MEM_WIDTH), lambda b, i: (b, 0, 0)),
                  pl.BlockSpec((1, MEM_HEADS * VT_ROWS, N_MEM), lambda b, i: (b, 0, 0))],
        out_specs=pl.BlockSpec((1, tq, MEM_WIDTH), lambda b, i: (b, i, 0)),
        out_shape=jax.ShapeDtypeStruct((nb, seq, MEM_WIDTH), BF16),
        compiler_params=_cparams(2),
        name="memory_attention",
    )(mqt, mk, mvt)


def _post_attn_kernel(x_ref, att_ref, memo_ref, wa_ref, wm_ref, g_ref, b_ref, wr_ref, br_ref,
                      x1_ref, x1p_ref, idx_ref, rank_ref, gate_ref, cnt_ref, carry_ref):
    t = pl.program_id(0)

    @pl.when(t == 0)
    def _():
        carry_ref[...] = jnp.zeros_like(carry_ref)

    tm = x_ref.shape[0]
    mix = (jnp.dot(att_ref[...], wa_ref[...], preferred_element_type=F32)
           + jnp.dot(memo_ref[...], wm_ref[...], preferred_element_type=F32))
    x1 = _layer_norm(DEEPNORM_ALPHA * x_ref[...] + mix, g_ref[...], b_ref[...])
    x1_ref[...] = x1
    x1b = x1.astype(BF16)
    x1p_ref[...] = _pack_bf16_pairs(x1b)
    logits = jnp.dot(x1b, wr_ref[...], preferred_element_type=F32) + br_ref[...]
    lane = lax.broadcasted_iota(jnp.int32, (tm, LANES), 1)
    work = jnp.where(lane < N_EXPERTS, logits, -jnp.inf)
    idxs, vals = [], []
    onehot = jnp.zeros((tm, LANES), F32)
    for _ in range(TOP_K):
        best = jnp.max(work, axis=1, keepdims=True)
        where_best = jnp.argmax(work, axis=1, keepdims=True).astype(jnp.int32)
        hit = lane == where_best
        onehot = jnp.where(hit, 1.0, onehot)
        work = jnp.where(hit, -jnp.inf, work)
        idxs.append(where_best)
        vals.append(best)
    exps = [jnp.exp(v - vals[0]) for v in vals]
    denom = exps[0] + exps[1] + exps[2] + exps[3]
    row = lax.broadcasted_iota(jnp.int32, (tm, tm), 0)
    col = lax.broadcasted_iota(jnp.int32, (tm, tm), 1)
    strict = jnp.where(row > col, 1.0, 0.0).astype(BF16)
    before = jnp.dot(strict, onehot.astype(BF16), preferred_element_type=F32) + carry_ref[...]
    idx_out = jnp.zeros((tm, LANES), F32)
    rank_out = jnp.zeros((tm, LANES), F32)
    gate_out = jnp.zeros((tm, LANES), F32)
    for r in range(TOP_K):
        rank_r = jnp.sum(jnp.where(lane == idxs[r], before, 0.0), axis=1, keepdims=True)
        idx_out = jnp.where(lane == r, idxs[r].astype(F32), idx_out)
        rank_out = jnp.where(lane == r, rank_r, rank_out)
        gate_out = jnp.where(lane == r, exps[r] / denom, gate_out)
    idx_ref[...] = idx_out.T[:8].astype(jnp.int32)
    rank_ref[...] = rank_out.T[:8].astype(jnp.int32)
    gate_ref[...] = gate_out
    total = carry_ref[...] + jnp.sum(onehot, axis=0, keepdims=True)
    carry_ref[...] = total
    cnt_ref[...] = jnp.broadcast_to(total, cnt_ref.shape)


def _post_attn(x2d, att, memo, wa, wm, g, b, wr, br):
    t_tokens = x2d.shape[0]
    tm = PROJ_TM
    row_spec = lambda width: pl.BlockSpec((tm, width), lambda i: (i, 0))
    full = lambda a: pl.BlockSpec(a.shape, lambda i: (0, 0))
    return pl.pallas_call(
        _post_attn_kernel,
        grid=(t_tokens // tm,),
        in_specs=[row_spec(D_MODEL), row_spec(att.shape[1]), row_spec(MEM_WIDTH),
                  full(wa), full(wm), full(g), full(b), full(wr), full(br)],
        out_specs=[row_spec(D_MODEL), row_spec(D_MODEL // 2),
                   pl.BlockSpec((8, tm), lambda i: (0, i)), pl.BlockSpec((8, tm), lambda i: (0, i)),
                   row_spec(LANES), pl.BlockSpec((8, LANES), lambda i: (0, 0))],
        out_shape=[jax.ShapeDtypeStruct((t_tokens, D_MODEL), F32),
                   jax.ShapeDtypeStruct((t_tokens, D_MODEL // 2), jnp.uint32),
                   jax.ShapeDtypeStruct((8, t_tokens), jnp.int32),
                   jax.ShapeDtypeStruct((8, t_tokens), jnp.int32),
                   jax.ShapeDtypeStruct((t_tokens, LANES), F32),
                   jax.ShapeDtypeStruct((8, LANES), F32)],
        scratch_shapes=[pltpu.VMEM((1, LANES), F32)],
        compiler_params=_cparams(1),
        name="outproj_ln_router",
    )(x2d, att, memo, wa, wm, g, b, wr, br)


def _slot_kernel(pstart_ref, idx_ref, rank_ref, o_ref):
    idx = idx_ref[...]
    dest = rank_ref[...]
    for e in range(N_EXPERTS):
        dest = dest + jnp.where(idx == e, pstart_ref[e], 0)
    o_ref[...] = dest


def _slot_table(idx, rank, pstart):
    return pl.pallas_call(
        _slot_kernel,
        in_specs=[pl.BlockSpec(memory_space=pltpu.SMEM), pl.BlockSpec(memory_space=pltpu.VMEM),
                  pl.BlockSpec(memory_space=pltpu.VMEM)],
        out_specs=pl.BlockSpec(memory_space=pltpu.VMEM),
        out_shape=jax.ShapeDtypeStruct(idx.shape, jnp.int32),
        name="slot_table",
    )(pstart, idx, rank)


def _expert_kernel(be_ref, nu_ref, br_ref, x_ref, wgu_hbm, bgu_ref, wdn_hbm, bdn_ref, *rest,
                   n_chunks, blk_lo, n_blocks, layer, has_prev):
    o_ref, wgu_st, wdn_st, wgu_bf, wdn_bf, h_ref, sem = rest[1:] if has_prev else rest
    step = pl.program_id(0)
    blk = step + blk_lo
    live_end = jnp.minimum(blk_lo + pl.num_programs(0), nu_ref[0])
    expert = be_ref[blk]

    def fetch(e):
        return (pltpu.make_async_copy(wgu_hbm.at[layer, e], wgu_st, sem.at[0]),
                pltpu.make_async_copy(wdn_hbm.at[layer, e], wdn_st, sem.at[1]))

    @pl.when(((step == 0) | (expert != be_ref[jnp.maximum(blk - 1, 0)])) & (blk < live_end))
    def _():
        @pl.when(step == 0)
        def _():
            for copy in fetch(expert):
                copy.start()

        for copy in fetch(expert):
            copy.wait()
        wgu_bf[...] = wgu_st[...].astype(BF16)
        wdn_bf[...] = wdn_st[...].astype(BF16)
        nxt = lax.while_loop(
            lambda j: (j < live_end) & (be_ref[jnp.minimum(j, n_blocks - 1)] == expert),
            lambda j: j + 1, blk + 1)

        @pl.when(nxt < live_end)
        def _():
            for copy in fetch(be_ref[jnp.minimum(nxt, n_blocks - 1)]):
                copy.start()

    def mlp(rows):
        x = _unpack_bf16_pairs(x_ref[:rows]).astype(BF16)
        cw = D_EXPERT // n_chunks
        for c in range(n_chunks):
            gs = slice(c * cw, (c + 1) * cw)
            us = slice(D_EXPERT + c * cw, D_EXPERT + (c + 1) * cw)
            g = jnp.dot(x, wgu_bf[:, gs], preferred_element_type=F32) + bgu_ref[0, 0, :, gs]
            u = jnp.dot(x, wgu_bf[:, us], preferred_element_type=F32) + bgu_ref[0, 0, :, us]
            g = jnp.minimum(g, SWIGLU_LIMIT)
            u = jnp.clip(u, -SWIGLU_LIMIT, SWIGLU_LIMIT)
            h_ref[:rows, gs] = ((u + 1.0) * (g * jax.nn.sigmoid(SWIGLU_ALPHA * g))).astype(BF16)
        y = jnp.dot(h_ref[:rows], wdn_bf[...], preferred_element_type=F32) + bdn_ref[0, 0]
        return _pack_bf16_pairs(y.astype(BF16))

    tm = o_ref.shape[0]
    live_rows = br_ref[blk]

    @pl.when(live_rows > tm // 2)
    def _():
        o_ref[...] = mlp(tm)

    @pl.when((live_rows > 0) & (live_rows <= tm // 2))
    def _():
        o_ref[:tm // 2] = mlp(tm // 2)
        o_ref[tm // 2:] = jnp.zeros((tm - tm // 2, o_ref.shape[1]), o_ref.dtype)

    @pl.when(live_rows == 0)
    def _():
        o_ref[...] = jnp.zeros_like(o_ref)


def _experts(block_expert, n_used, block_rows, xb, blk_lo, p_rows, y_prev, layer, wgu, bgu, wdn, bdn):
    tm = MOE_TM
    has_prev = y_prev is not None
    b_map = lambda i, be, nu, br: (layer, be[i + blk_lo], 0, 0)
    in_specs = [pl.BlockSpec((tm, D_MODEL // 2), lambda i, be, nu, br: (i, 0)),
                pl.BlockSpec(memory_space=pl.ANY),
                pl.BlockSpec((1, 1, 1, 2 * D_EXPERT), b_map),
                pl.BlockSpec(memory_space=pl.ANY),
                pl.BlockSpec((1, 1, 1, D_MODEL), b_map)]
    args = [block_expert, n_used, block_rows, xb, wgu, bgu, wdn, bdn]
    if has_prev:
        in_specs.append(pl.BlockSpec(memory_space=pl.ANY))
        args.append(y_prev)
    grid_spec = pltpu.PrefetchScalarGridSpec(
        num_scalar_prefetch=3,
        grid=(xb.shape[0] // tm,),
        in_specs=in_specs,
        out_specs=pl.BlockSpec((tm, D_MODEL // 2), lambda i, be, nu, br: (i + blk_lo, 0)),
        scratch_shapes=[pltpu.VMEM((D_MODEL, 2 * D_EXPERT), F32),
                        pltpu.VMEM((D_EXPERT, D_MODEL), F32),
                        pltpu.VMEM((D_MODEL, 2 * D_EXPERT), BF16),
                        pltpu.VMEM((D_EXPERT, D_MODEL), BF16),
                        pltpu.VMEM((tm, D_EXPERT), BF16),
                        pltpu.SemaphoreType.DMA((2,))],
    )
    return pl.pallas_call(
        functools.partial(_expert_kernel, n_chunks=4, blk_lo=blk_lo, n_blocks=p_rows // tm,
                          layer=layer, has_prev=has_prev),
        grid_spec=grid_spec,
        out_shape=jax.ShapeDtypeStruct((p_rows, D_MODEL // 2), jnp.uint32),
        input_output_aliases={len(args) - 1: 0} if has_prev else {},
        compiler_params=_cparams(1),
        name="experts",
    )(*args)


def _sc_move_rows(table_hbm, idx_v, out_hbm, base, n_chunks, rows_v, gsem, wsem):
    def gather(c, slot):
        off = pl.multiple_of(c * SC_CHUNK, SC_CHUNK)
        return pltpu.make_async_copy(table_hbm.at[idx_v.at[pl.ds(off, SC_CHUNK)]],
                                     rows_v.at[slot], gsem.at[slot])

    def put(c, slot):
        off = pl.multiple_of(c * SC_CHUNK, SC_CHUNK)
        return pltpu.make_async_copy(rows_v.at[slot], out_hbm.at[pl.ds(base + off, SC_CHUNK)],
                                     wsem.at[slot])

    gather(0, 0).start()
    gather(1, 1).start()

    @pl.loop(0, n_chunks, step=2)
    def _(c):
        for slot in range(2):
            gather(c + slot, slot).wait()
            put(c + slot, slot).start()
        for slot in range(2):
            put(c + slot, slot).wait()

            @pl.when(c + 2 + slot < n_chunks)
            def _():
                gather(c + 2 + slot, slot).start()


def _sc_gather_rows(table, idx):
    n_idx = idx.shape[0]
    width = table.shape[1]
    per_worker = n_idx // SC_WORKERS
    n_chunks = per_worker // SC_CHUNK
    mesh = plsc.VectorSubcoreMesh(core_axis_name="c", subcore_axis_name="s",
                                  num_cores=SC_CORES, num_subcores=SC_SUBCORES)

    def body(table_hbm, idx_hbm, out_hbm, idx_v, rows_v, gsem, wsem):
        wid = lax.axis_index("s") * SC_CORES + lax.axis_index("c")
        base = wid * per_worker
        pltpu.sync_copy(idx_hbm.at[pl.ds(base, per_worker)], idx_v)
        _sc_move_rows(table_hbm, idx_v, out_hbm, base, n_chunks, rows_v, gsem, wsem)

    return pl.kernel(
        body,
        out_type=jax.ShapeDtypeStruct((n_idx, width), table.dtype),
        mesh=mesh,
        scratch_types=[pltpu.VMEM((per_worker,), jnp.int32),
                       pltpu.VMEM((2, SC_CHUNK, width), table.dtype),
                       pltpu.SemaphoreType.DMA((2,)),
                       pltpu.SemaphoreType.DMA((2,))],
        name="sc_gather_rows",
    )(table, idx)


def _sc_dispatch_rows(table, dest, slot_lo, n_slots):
    n_tok, width = table.shape
    n_pairs = dest.shape[0]
    per_worker = n_slots // SC_WORKERS
    n_chunks = per_worker // SC_CHUNK
    n_scan = n_pairs // SC_SCAN
    mesh = plsc.VectorSubcoreMesh(core_axis_name="c", subcore_axis_name="s",
                                  num_cores=SC_CORES, num_subcores=SC_SUBCORES)

    def body(table_hbm, dest_hbm, out_hbm, tok_v, dest_v, rows_v, gsem, wsem):
        wid = lax.axis_index("s") * SC_CORES + lax.axis_index("c")
        base = wid * per_worker
        first = slot_lo + base
        lane = lax.iota(jnp.int32, SC_LANES)

        @pl.loop(0, per_worker // SC_LANES)
        def _(i):
            off = pl.multiple_of(i * SC_LANES, SC_LANES)
            tok_v[pl.ds(off, SC_LANES)] = lax.rem(first + off + lane, n_tok)

        @pl.loop(0, n_scan)
        def _(g):
            goff = pl.multiple_of(g * SC_SCAN, SC_SCAN)
            pltpu.sync_copy(dest_hbm.at[pl.ds(goff, SC_SCAN)], dest_v)

            @plsc.parallel_loop(0, SC_SCAN // SC_LANES, unroll=8)
            def _(i):
                off = pl.multiple_of(i * SC_LANES, SC_LANES)
                local = dest_v[pl.ds(off, SC_LANES)] - first
                mine = (local >= 0) & (local < per_worker)
                pair = goff + off + lane
                plsc.store_scatter(tok_v, [jnp.where(mine, local, 0)], lax.rem(pair, n_tok), mask=mine)

        _sc_move_rows(table_hbm, tok_v, out_hbm, base, n_chunks, rows_v, gsem, wsem)

    return pl.kernel(
        body,
        out_type=jax.ShapeDtypeStruct((n_slots, width), table.dtype),
        mesh=mesh,
        scratch_types=[pltpu.VMEM((per_worker,), jnp.int32),
                       pltpu.VMEM((SC_SCAN,), jnp.int32),
                       pltpu.VMEM((2, SC_CHUNK, width), table.dtype),
                       pltpu.SemaphoreType.DMA((2,)),
                       pltpu.SemaphoreType.DMA((2,))],
        compiler_params=pltpu.CompilerParams(needs_layout_passes=False),
        name="sc_dispatch_rows",
    )(table, dest)


def _combine_kernel(x1_ref, yg_ref, gate_ref, g_ref, b_ref, *rest):
    o_ref = rest[-1]
    gates = gate_ref[...]
    ffn = _unpack_bf16_pairs(yg_ref[0]) * gates[:, 0:1]
    for r in range(1, TOP_K):
        ffn = ffn + _unpack_bf16_pairs(yg_ref[r]) * gates[:, r:r + 1]
    o_ref[...] = _layer_norm(DEEPNORM_ALPHA * x1_ref[...] + ffn, g_ref[...], b_ref[...])


def _combine(x1, yg, gates, g, b, tok_lo, out_prev):
    t_tokens = x1.shape[0]
    tm = COMBINE_TM
    tile_lo = tok_lo // tm
    in_specs = [pl.BlockSpec((tm, D_MODEL), lambda i: (i + tile_lo, 0)),
                pl.BlockSpec((TOP_K, tm, D_MODEL // 2), lambda i: (0, i, 0)),
                pl.BlockSpec((tm, LANES), lambda i: (i + tile_lo, 0)),
                pl.BlockSpec((1, D_MODEL), lambda i: (0, 0)),
                pl.BlockSpec((1, D_MODEL), lambda i: (0, 0))]
    args = [x1, yg, gates, g, b]
    if out_prev is not None:
        in_specs.append(pl.BlockSpec(memory_space=pl.ANY))
        args.append(out_prev)
    return pl.pallas_call(
        _combine_kernel,
        grid=(yg.shape[1] // tm,),
        in_specs=in_specs,
        out_specs=pl.BlockSpec((tm, D_MODEL), lambda i: (i + tile_lo, 0)),
        out_shape=jax.ShapeDtypeStruct((t_tokens, D_MODEL), F32),
        input_output_aliases={len(args) - 1: 0} if out_prev is not None else {},
        compiler_params=_cparams(1),
        name="combine_ln",
    )(*args)


def _moe(x1, x1p, idx, rank, gates, counts, layer, wgu, bgu, wdn, bdn, g, b):
    t_tokens = x1.shape[0]
    tm = MOE_TM
    counts = counts[0, :N_EXPERTS].astype(jnp.int32)
    padded = ((counts + tm - 1) // tm) * tm
    pend = jnp.cumsum(padded)
    pstart = pend - padded
    dest = _slot_table(idx, rank, pstart)[:TOP_K]
    n_blocks = (t_tokens * TOP_K) // tm + N_EXPERTS
    p_rows = n_blocks * tm
    block_start = jnp.arange(n_blocks, dtype=jnp.int32) * tm
    block_expert = jnp.minimum(
        jnp.sum((pend[None, :] <= block_start[:, None]).astype(jnp.int32), axis=1), N_EXPERTS - 1)
    n_used = (pend[-1:] // tm).astype(jnp.int32)
    mine = block_expert[:, None] == jnp.arange(N_EXPERTS, dtype=jnp.int32)[None, :]
    live_end = jnp.sum(jnp.where(mine, (pstart + counts)[None, :], 0), axis=1)
    block_rows = jnp.clip(live_end - block_start, 0, tm)
    y = None
    for lo, hi in ((0, n_blocks // MOE_SPLIT), (n_blocks // MOE_SPLIT, n_blocks)):
        xb = _sc_dispatch_rows(x1p, dest.reshape(-1), lo * tm, (hi - lo) * tm)
        y = _experts(block_expert, n_used, block_rows, xb, lo, p_rows, y, layer, wgu, bgu, wdn, bdn)
    out = None
    for lo, hi in ((0, t_tokens // MOE_SPLIT), (t_tokens // MOE_SPLIT, t_tokens)):
        yg = _sc_gather_rows(y, dest[:, lo:hi].reshape(-1)).reshape(TOP_K, hi - lo, D_MODEL // 2)
        out = _combine(x1, yg, gates, g, b, lo, out)
    return out


def _row(v, width=None):
    v = v.astype(F32).reshape(1, -1)
    if width is not None and v.shape[1] < width:
        v = jnp.pad(v, ((0, 0), (0, width - v.shape[1])))
    return v


def _pad_cols(w, width):
    return jnp.pad(w, ((0, 0), (0, width - w.shape[1])))


def kernel(x, mem, positions, a_w_in, a_b_f, a_w_out, b_w_in, b_g_q, b_w_uq, b_w_out,
           kv_w_dkv, kv_g, kv_w_ukv, mem_w_kv, ln_g, ln_b,
           moe_w_r, moe_b_r, moe_w_gu, moe_b_gu, moe_w_dn, moe_b_dn):
    nb, seq, d = x.shape
    t_tokens = nb * seq
    n_a = a_w_in.shape[0]
    x2d = x.reshape(t_tokens, d)
    mem2d = mem.reshape(nb * N_MEM, d)
    pos2d = positions.reshape(1, t_tokens)
    half = QK_ROPE // 2
    invf = (ROPE_THETA ** (-jnp.arange(half, dtype=F32) * 2.0 / QK_ROPE)).reshape(half, 1)

    shared_kv = None
    for l in range(DEPTH):
        mk, mvt = _mem_proj(mem2d, mem_w_kv[l].astype(BF16), nb)
        if l < n_a:
            w_in = a_w_in[l]
            w = jnp.concatenate([w_in[:, :3 * FOX_WIDTH],
                                 _pad_cols(w_in[:, 3 * FOX_WIDTH:3 * FOX_WIDTH + FOX_HEADS], LANES),
                                 w_in[:, 3 * FOX_WIDTH + FOX_HEADS:]], axis=1).astype(BF16)
            qt, k, vt, mqt, cum, cumt, cfirst, clast, knorm = _fox_proj(
                x2d, w, _row(a_b_f[l], LANES), seq)
            stats = [s.reshape(-1, LANES) for s in (cfirst, clast, knorm)]
            att = _causal_attention(qt, k.reshape(nb, seq, -1), vt,
                                    [cum.reshape(nb, seq, LANES), cumt] + stats, slab=False)
            w_out = a_w_out[l]
        else:
            bl = l - n_a
            w = jnp.concatenate([b_w_in[bl], kv_w_dkv[:, :KV_LORA],
                                 jnp.zeros((d, QK_NOPE), F32), kv_w_dkv[:, KV_LORA:],
                                 jnp.zeros((d, LANES - QK_NOPE - QK_ROPE), F32)], axis=1).astype(BF16)
            wuq = b_w_uq[bl].reshape(Q_LORA, MLA_HEADS, QK_NOPE + QK_ROPE)
            wuq = jnp.pad(wuq, ((0, 0), (0, 0), (0, MLA_SLAB - QK_NOPE - QK_ROPE)))
            wuq = wuq.reshape(Q_LORA, MLA_HEADS * MLA_SLAB).astype(BF16)
            wukv = kv_w_ukv.reshape(KV_LORA, MLA_HEADS, QK_NOPE + V_DIM)
            wk = jnp.pad(wukv[:, :, :QK_NOPE], ((0, 0), (0, 0), (0, MLA_SLAB - QK_NOPE)))
            wk = wk.reshape(KV_LORA, MLA_HEADS * MLA_SLAB).astype(BF16)
            wv = wukv[:, :, QK_NOPE:].reshape(KV_LORA, MLA_V_WIDTH).astype(BF16)
            qt, mqt, k_new, vt_new = _mla_proj(x2d, pos2d, invf, w, _row(b_g_q[bl]), wuq,
                                              _row(kv_g), wk, wv, seq)
            if shared_kv is None:
                shared_kv = (k_new.reshape(nb, seq, -1), vt_new)
            att = _causal_attention(qt, shared_kv[0], shared_kv[1], None, slab=True)
            w_out = b_w_out[bl]
        memo = _memory_attention(mqt, mk, mvt)
        n_att = w_out.shape[0] - MEM_WIDTH
        x1, x1p, idx, rank, gates, counts = _post_attn(
            x2d, att.reshape(t_tokens, -1), memo.reshape(t_tokens, MEM_WIDTH),
            w_out[:n_att].astype(BF16), w_out[n_att:].astype(BF16),
            _row(ln_g[l, 0]), _row(ln_b[l, 0]),
            _pad_cols(moe_w_r[l], LANES).astype(BF16), _row(moe_b_r[l], LANES))
        x2d = _moe(x1, x1p, idx, rank, gates, counts, l,
                   moe_w_gu, moe_b_gu.reshape(DEPTH, N_EXPERTS, 1, -1),
                   moe_w_dn, moe_b_dn.reshape(DEPTH, N_EXPERTS, 1, -1),
                   _row(ln_g[l, 1]), _row(ln_b[l, 1]))
    return x2d.reshape(nb, seq, d)
```

```python
import functools
import math

import jax
import jax.numpy as jnp
from jax import lax
from jax.experimental import pallas as pl
from jax.experimental.pallas import tpu as pltpu
from jax.experimental.pallas import tpu_sc as plsc

F32 = jnp.float32
BF16 = jnp.bfloat16

D_MODEL = 1024
DEPTH = 2
N_MEM = 256
HEAD_DIM = 64
FOX_HEADS = 12
MEM_HEADS = 4
MLA_HEADS = 12
Q_LORA = 384
KV_LORA = 256
QK_NOPE = 64
QK_ROPE = 32
V_DIM = 64
ROPE_THETA = 10000.0
N_EXPERTS = 32
TOP_K = 4
D_EXPERT = D_MODEL
SWIGLU_LIMIT = 7.0
SWIGLU_ALPHA = 1.702
LN_EPS = 1e-5
RMS_EPS = 1e-6
NEG_INF = -1e30
DEEPNORM_ALPHA = (2 * DEPTH) ** 0.25
FOX_WIDTH = FOX_HEADS * HEAD_DIM
MEM_WIDTH = MEM_HEADS * HEAD_DIM
MLA_V_WIDTH = MLA_HEADS * V_DIM

LANES = 128
LOG2E = math.log2(math.e)
VMEM_LIMIT = 48 * 1024 * 1024

PROJ_TM = 512
ATT_TQ = 512
FOX_TK = 256
MOE_TM = 512
COMBINE_TM = 512
MOE_SPLIT = 4
MLA_SLAB = LANES
VT_ROWS = HEAD_DIM + 16
SC_CORES = 2
SC_SUBCORES = 16
SC_WORKERS = SC_CORES * SC_SUBCORES
SC_CHUNK = 32
SC_LANES = 16
SC_SCAN = 8192


def _cparams(n_axes):
    return pltpu.CompilerParams(dimension_semantics=("arbitrary",) * n_axes,
                                vmem_limit_bytes=VMEM_LIMIT)


def _split3(x):
    hi = x.astype(BF16)
    r1 = x - hi.astype(F32)
    mid = r1.astype(BF16)
    lo = (r1 - mid.astype(F32)).astype(BF16)
    return hi, mid, lo


def _pack_bf16_pairs(v):
    bits = pltpu.bitcast(v.astype(F32), jnp.uint32)
    half = v.shape[1] // 2
    return (bits[:, :half] >> 16) | bits[:, half:]


def _unpack_bf16_pairs(words):
    return jnp.concatenate([pltpu.bitcast(words << 16, F32),
                            pltpu.bitcast(words & jnp.uint32(0xFFFF0000), F32)], axis=1)


def _layer_norm(y, g, b):
    mu = jnp.mean(y, axis=-1, keepdims=True)
    yc = y - mu
    var = jnp.mean(yc * yc, axis=-1, keepdims=True)
    return yc * lax.rsqrt(var + LN_EPS) * g + b


def _rms_norm(y, g):
    return y * lax.rsqrt(jnp.mean(y * y, axis=-1, keepdims=True) + RMS_EPS) * g


def _store_transposed(dst_ref, val):
    for s in range(val.shape[1] // LANES):
        sl = slice(s * LANES, (s + 1) * LANES)
        dst_ref[0, sl, :] = val[:, sl].T.astype(dst_ref.dtype)


def _store_values_transposed(vt_ref, val):
    tm = val.shape[0]
    ones = jnp.ones((VT_ROWS - HEAD_DIM, tm), vt_ref.dtype)
    for s in range(val.shape[1] // LANES):
        pair_t = val[:, s * LANES:(s + 1) * LANES].T.astype(vt_ref.dtype)
        for h in range(2):
            r0 = (2 * s + h) * VT_ROWS
            vt_ref[0, r0:r0 + HEAD_DIM, :] = pair_t[h * HEAD_DIM:(h + 1) * HEAD_DIM]
            vt_ref[0, r0 + HEAD_DIM:r0 + VT_ROWS, :] = ones


def _fox_proj_kernel(x_ref, w_ref, bf_ref, seg_ref, qt_ref, k_ref, vt_ref, mqt_ref, cum_ref,
                     cumt_ref, cfirst_ref, clast_ref, knorm_ref, carry_ref, knmax_ref, *,
                     tiles_per_batch):
    t = pl.program_id(0)

    @pl.when(t % tiles_per_batch == 0)
    def _():
        carry_ref[...] = jnp.zeros_like(carry_ref)
        knmax_ref[...] = jnp.zeros_like(knmax_ref)

    tm = x_ref.shape[0]
    proj = jnp.dot(x_ref[...].astype(BF16), w_ref[...], preferred_element_type=F32)
    qscale = HEAD_DIM ** -0.5 * LOG2E
    _store_transposed(qt_ref, proj[:, :FOX_WIDTH] * qscale)
    kb = proj[:, FOX_WIDTH:2 * FOX_WIDTH].astype(BF16)
    k_ref[...] = kb
    _store_values_transposed(vt_ref, proj[:, 2 * FOX_WIDTH:3 * FOX_WIDTH])
    kf = kb.astype(F32)
    ksq = jnp.dot((kf * kf).astype(BF16), seg_ref[...], preferred_element_type=F32)
    n_sub = tm // FOX_TK
    for sub in range(n_sub):
        tile_max = jnp.sqrt(jnp.max(ksq[sub * FOX_TK:(sub + 1) * FOX_TK], axis=0, keepdims=True))
        knmax_ref[...] = jnp.maximum(knmax_ref[...], tile_max)
        knorm_ref[sub] = knmax_ref[...]
    f = proj[:, 3 * FOX_WIDTH:3 * FOX_WIDTH + LANES] + bf_ref[...]
    _store_transposed(mqt_ref, proj[:, 3 * FOX_WIDTH + LANES:] * qscale)
    log_f = jnp.minimum(f, 0.0) - jnp.log1p(jnp.exp(-jnp.abs(f)))
    row = lax.broadcasted_iota(jnp.int32, (tm, tm), 0)
    col = lax.broadcasted_iota(jnp.int32, (tm, tm), 1)
    tri = jnp.where(row >= col, 1.0, 0.0).astype(BF16)
    hi, mid, lo = _split3(log_f)
    cum = (jnp.dot(tri, hi, preferred_element_type=F32)
           + jnp.dot(tri, mid, preferred_element_type=F32)
           + jnp.dot(tri, lo, preferred_element_type=F32)) + carry_ref[...]
    carry_ref[...] = cum[tm - 1:tm, :]
    cum2 = cum * LOG2E
    cum_ref[...] = cum2
    cumt_ref[0] = cum2.T[:16, :]
    for sub in range(n_sub):
        cfirst_ref[sub] = cum2[sub * FOX_TK:sub * FOX_TK + 1, :]
        clast_ref[sub] = cum2[(sub + 1) * FOX_TK - 1:(sub + 1) * FOX_TK, :]


def _fox_proj(x2d, w, bf, seq):
    t_tokens = x2d.shape[0]
    tm = PROJ_TM
    nb = t_tokens // seq
    n = w.shape[1]
    tiles_per_batch = seq // tm
    n_tiles = t_tokens // tm
    seg = (jnp.arange(FOX_WIDTH)[:, None] // HEAD_DIM == jnp.arange(LANES)[None, :]).astype(BF16)
    row_spec = lambda width: pl.BlockSpec((tm, width), lambda i: (i, 0))
    t_spec = lambda rows: pl.BlockSpec(
        (1, rows, tm), lambda i: (i // tiles_per_batch, 0, i % tiles_per_batch))
    n_sub = tm // FOX_TK
    stat_spec = pl.BlockSpec((n_sub, 1, LANES), lambda i: (i, 0, 0))
    stat_shape = jax.ShapeDtypeStruct((n_tiles * n_sub, 1, LANES), F32)
    return pl.pallas_call(
        functools.partial(_fox_proj_kernel, tiles_per_batch=tiles_per_batch),
        grid=(n_tiles,),
        in_specs=[row_spec(D_MODEL),
                  pl.BlockSpec((D_MODEL, n), lambda i: (0, 0)),
                  pl.BlockSpec((1, LANES), lambda i: (0, 0)),
                  pl.BlockSpec((FOX_WIDTH, LANES), lambda i: (0, 0))],
        out_specs=[t_spec(FOX_WIDTH), row_spec(FOX_WIDTH), t_spec(FOX_HEADS * VT_ROWS),
                   t_spec(MEM_WIDTH), row_spec(LANES), t_spec(16),
                   stat_spec, stat_spec, stat_spec],
        out_shape=[jax.ShapeDtypeStruct((nb, FOX_WIDTH, seq), BF16),
                   jax.ShapeDtypeStruct((t_tokens, FOX_WIDTH), BF16),
                   jax.ShapeDtypeStruct((nb, FOX_HEADS * VT_ROWS, seq), BF16),
                   jax.ShapeDtypeStruct((nb, MEM_WIDTH, seq), BF16),
                   jax.ShapeDtypeStruct((t_tokens, LANES), F32),
                   jax.ShapeDtypeStruct((nb, 16, seq), F32),
                   stat_shape, stat_shape, stat_shape],
        scratch_shapes=[pltpu.VMEM((1, LANES), F32), pltpu.VMEM((1, LANES), F32)],
        compiler_params=_cparams(1),
        name="fox_proj",
    )(x2d, w, bf, seg)


def _rope_slab_t(slab_t, cos_t, sin_t):
    half = QK_ROPE // 2
    x1 = slab_t[QK_NOPE:QK_NOPE + half]
    x2 = slab_t[QK_NOPE + half:QK_NOPE + QK_ROPE]
    return jnp.concatenate([slab_t[:QK_NOPE], x1 * cos_t - x2 * sin_t, x1 * sin_t + x2 * cos_t,
                            slab_t[QK_NOPE + QK_ROPE:]], axis=0)


def _mla_proj_kernel(x_ref, pos_ref, invf_ref, w_ref, gq_ref, wuq_ref, gkv_ref, wk_ref, wv_ref,
                     qt_ref, mqt_ref, k_ref, vt_ref):
    proj = jnp.dot(x_ref[...].astype(BF16), w_ref[...], preferred_element_type=F32)
    ang_t = invf_ref[...] * pos_ref[...].astype(F32)
    cos_t = jnp.cos(ang_t)
    sin_t = jnp.sin(ang_t)
    c_q = _rms_norm(proj[:, :Q_LORA], gq_ref[...])
    _store_transposed(mqt_ref, proj[:, Q_LORA:Q_LORA + MEM_WIDTH] * (HEAD_DIM ** -0.5 * LOG2E))
    kv_off = Q_LORA + MEM_WIDTH
    c_kv = _rms_norm(proj[:, kv_off:kv_off + KV_LORA], gkv_ref[...])
    kr = _rope_slab_t(proj[:, kv_off + KV_LORA:].T, cos_t, sin_t).T
    q = jnp.dot(c_q.astype(BF16), wuq_ref[...], preferred_element_type=F32)
    kn = jnp.dot(c_kv.astype(BF16), wk_ref[...], preferred_element_type=F32)
    qscale = (QK_NOPE + QK_ROPE) ** -0.5 * LOG2E
    for h in range(MLA_HEADS):
        sl = slice(h * MLA_SLAB, (h + 1) * MLA_SLAB)
        qt_ref[0, sl, :] = (_rope_slab_t(q[:, sl].T, cos_t, sin_t) * qscale).astype(BF16)
        k_ref[:, sl] = (kn[:, sl] + kr).astype(BF16)
    _store_values_transposed(vt_ref, jnp.dot(c_kv.astype(BF16), wv_ref[...],
                                             preferred_element_type=F32))


def _mla_proj(x2d, pos2d, invf, w, gq, wuq, gkv, wk, wv, seq):
    t_tokens = x2d.shape[0]
    tm = PROJ_TM
    nb = t_tokens // seq
    tiles_per_batch = seq // tm
    row_spec = lambda width: pl.BlockSpec((tm, width), lambda i: (i, 0))
    t_spec = lambda rows: pl.BlockSpec(
        (1, rows, tm), lambda i: (i // tiles_per_batch, 0, i % tiles_per_batch))
    full = lambda a: pl.BlockSpec(a.shape, lambda i: (0, 0))
    slabs = MLA_HEADS * MLA_SLAB
    return pl.pallas_call(
        _mla_proj_kernel,
        grid=(t_tokens // tm,),
        in_specs=[row_spec(D_MODEL), pl.BlockSpec((1, tm), lambda i: (0, i)), full(invf), full(w),
                  full(gq), full(wuq), full(gkv), full(wk), full(wv)],
        out_specs=[t_spec(slabs), t_spec(MEM_WIDTH), row_spec(slabs),
                   t_spec(MLA_HEADS * VT_ROWS)],
        out_shape=[jax.ShapeDtypeStruct((nb, slabs, seq), BF16),
                   jax.ShapeDtypeStruct((nb, MEM_WIDTH, seq), BF16),
                   jax.ShapeDtypeStruct((t_tokens, slabs), BF16),
                   jax.ShapeDtypeStruct((nb, MLA_HEADS * VT_ROWS, seq), BF16)],
        compiler_params=_cparams(1),
        name="mla_proj",
    )(x2d, pos2d, invf, w, gq, wuq, gkv, wk, wv)


SKIP_LOG2 = -160.0
NORM_SLACK = 1.02


def _causal_attn_kernel(*refs, tq, tk, n_chunks, fox, slab, long_body):
    if fox:
        (cf_ref, cl_ref, kn_ref, qt_ref, k_ref, vt_ref, cq_ref, ck_ref,
         o_ref, m_ref, acc_ref, s_ref, cmax_ref) = refs
    else:
        qt_ref, k_ref, vt_ref, o_ref, m_ref, acc_ref, s_ref, cmax_ref = refs
    b = pl.program_id(0)
    pair = pl.program_id(1)
    i = pl.program_id(2)
    m_ref[...] = jnp.full(m_ref.shape, NEG_INF, F32)
    acc_ref[...] = jnp.zeros(acc_ref.shape, F32)
    qt = qt_ref[0]
    if slab:
        qth = [qt[:MLA_SLAB], qt[MLA_SLAB:]]
    else:
        rowi = lax.broadcasted_iota(jnp.int32, qt.shape, 0)
        zero = jnp.zeros_like(qt)
        qth = [jnp.where(rowi < HEAD_DIM, qt, zero), jnp.where(rowi >= HEAD_DIM, qt, zero)]
    if fox:
        lane = lax.broadcasted_iota(jnp.int32, (1, LANES), 1)
        cq = [cq_ref[0, pl.ds(2 * pair + h, 1), :] for h in range(2)]

    def logits_to(slot, j):
        off = pl.multiple_of(j * tk, tk)
        kc = k_ref[0, pl.ds(off, tk), :]
        for h in range(2):
            kh = kc[:, h * MLA_SLAB:(h + 1) * MLA_SLAB] if slab else kc
            s = jnp.dot(kh, qth[h], preferred_element_type=F32)
            if fox:
                ck_blk = ck_ref[0, pl.ds(off, tk), :]
                ck = jnp.sum(jnp.where(lane == 2 * pair + h, ck_blk, 0.0), axis=1, keepdims=True)
                s = s + cq[h] - ck
            s_ref[slot, h] = s
            cmax_ref[slot, h] = jnp.max(s, axis=0, keepdims=True)

    def softmax_pv(slot, j, causal_mask):
        off = pl.multiple_of(j * tk, tk)
        for h in range(2):
            s = s_ref[slot, h]
            if causal_mask is not None:
                s = jnp.where(causal_mask, s, NEG_INF)
                chunk_max = jnp.max(s, axis=0, keepdims=True)
            else:
                chunk_max = cmax_ref[slot, h]
            m_prev = m_ref[h]
            m_new = jnp.maximum(m_prev, chunk_max)
            alpha = jnp.exp2(m_prev - m_new)
            p = jnp.exp2(s - m_new).astype(BF16)
            vth = vt_ref[0, h * VT_ROWS:(h + 1) * VT_ROWS, pl.ds(off, tk)]
            acc_ref[h] = acc_ref[h] * alpha + jnp.dot(vth, p, preferred_element_type=F32)
            m_ref[h] = m_new

    per_tile = tq // tk
    d0 = per_tile * i
    if fox:
        base = b * n_chunks
        qn = []
        for h in range(2):
            qf = qth[h].astype(F32)
            qn.append(jnp.sqrt(jnp.max(jnp.sum(qf * qf, axis=0, keepdims=True))) * NORM_SLACK)

        top = [2.0 * qn[h] * kn_ref[base + d0 + per_tile - 1, 2 * pair + h]
               + cf_ref[base + d0, 2 * pair + h] for h in range(2)]

        def live(j):
            ub = [top[h] - cl_ref[base + jnp.maximum(j, 0), 2 * pair + h] for h in range(2)]
            return (j >= 0) & (jnp.maximum(ub[0], ub[1]) > SKIP_LOG2)

        j0 = lax.while_loop(live, lambda j: j - 1, d0 - 1) + 1
    else:
        j0 = 0
    n_before = d0 - j0

    def causal_mask(shift):
        key = lax.broadcasted_iota(jnp.int32, (tk, tq), 0)
        qry = lax.broadcasted_iota(jnp.int32, (tk, tq), 1)
        return key + shift <= qry

    def finish():
        if per_tile == 2:
            logits_to(1, d0 + 1)
        softmax_pv(0, d0, causal_mask(0))
        if per_tile == 2:
            softmax_pv(1, d0 + 1, causal_mask(tk))

    def pipelined():
        odd = (n_before % 2) == 1

        @pl.when(odd)
        def _():
            logits_to(1, j0)
            logits_to(0, j0 + 1)
            softmax_pv(1, j0, None)

        @pl.when(jnp.logical_not(odd))
        def _():
            logits_to(0, j0)

        def two_steps(j):
            logits_to(1, j + 1)
            softmax_pv(0, j, None)
            logits_to(0, j + 2)
            softmax_pv(1, j + 1, None)

        j_even = j0 + (n_before % 2)
        n_two = n_before // 2
        if long_body:
            @pl.when((n_two % 2) == 1)
            def _():
                two_steps(j_even)

            j_quad = j_even + 2 * (n_two % 2)

            def body(t, carry):
                two_steps(j_quad + 4 * t)
                two_steps(j_quad + 4 * t + 2)
                return carry

            lax.fori_loop(0, n_two // 2, body, 0)
        else:
            def body(t, carry):
                two_steps(j_even + 2 * t)
                return carry

            lax.fori_loop(0, n_two, body, 0)
        finish()

    if fox:
        @pl.when(n_before == 1)
        def _():
            logits_to(1, j0)
            logits_to(0, d0)
            softmax_pv(1, j0, None)
            finish()

        @pl.when(n_before != 1)
        def _():
            pipelined()
    else:
        pipelined()
    out_t = jnp.concatenate(
        [acc_ref[h, :HEAD_DIM] * (1.0 / acc_ref[h, HEAD_DIM:HEAD_DIM + 1]) for h in range(2)],
        axis=0)
    o_ref[0] = out_t.T.astype(o_ref.dtype)


def _causal_attention(qt, k, vt, fox_args, *, slab):
    nb, seq, _ = k.shape
    fox = fox_args is not None
    tq = ATT_TQ
    tk = FOX_TK if fox else ATT_TQ
    n_tiles = seq // tq
    n_pairs = FOX_HEADS // 2
    rows = 2 * MLA_SLAB if slab else LANES
    in_specs = [pl.BlockSpec((1, rows, tq), lambda b, p, i: (b, p, i)),
                pl.BlockSpec((1, seq, rows), lambda b, p, i: (b, 0, p)),
                pl.BlockSpec((1, 2 * VT_ROWS, seq), lambda b, p, i: (b, p, 0))]
    args = [qt, k, vt]
    if fox:
        cum, cumt, cfirst, clast, knorm = fox_args
        smem = pl.BlockSpec(memory_space=pltpu.SMEM)
        in_specs = [smem, smem, smem] + in_specs + [
            pl.BlockSpec((1, 16, tq), lambda b, p, i: (b, 0, i)),
            pl.BlockSpec((1, seq, LANES), lambda b, p, i: (b, 0, 0))]
        args = [cfirst, clast, knorm] + args + [cumt, cum]
    return pl.pallas_call(
        functools.partial(_causal_attn_kernel, tq=tq, tk=tk, n_chunks=seq // tk, fox=fox, slab=slab,
                          long_body=not fox),
        grid=(nb, n_pairs, n_tiles),
        in_specs=in_specs,
        out_specs=pl.BlockSpec((1, tq, LANES), lambda b, p, i: (b, i, p)),
        out_shape=jax.ShapeDtypeStruct((nb, seq, n_pairs * LANES), BF16),
        scratch_shapes=[pltpu.VMEM((2, 1, tq), F32), pltpu.VMEM((2, VT_ROWS, tq), F32),
                        pltpu.VMEM((2, 2, tk, tq), F32), pltpu.VMEM((2, 2, 1, tq), F32)],
        compiler_params=_cparams(3),
        name="fox_attention" if fox else "mla_attention",
    )(*args)


def _mem_proj_kernel(mem_ref, w_ref, mk_ref, mvt_ref):
    mkv = jnp.dot(mem_ref[...].astype(BF16), w_ref[...], preferred_element_type=F32)
    mk_ref[0] = mkv[:, :MEM_WIDTH].astype(BF16)
    _store_values_transposed(mvt_ref, mkv[:, MEM_WIDTH:])


def _mem_proj(mem2d, w, nb):
    return pl.pallas_call(
        _mem_proj_kernel,
        grid=(nb,),
        in_specs=[pl.BlockSpec((N_MEM, D_MODEL), lambda b: (b, 0)),
                  pl.BlockSpec(w.shape, lambda b: (0, 0))],
        out_specs=[pl.BlockSpec((1, N_MEM, MEM_WIDTH), lambda b: (b, 0, 0)),
                   pl.BlockSpec((1, MEM_HEADS * VT_ROWS, N_MEM), lambda b: (b, 0, 0))],
        out_shape=[jax.ShapeDtypeStruct((nb, N_MEM, MEM_WIDTH), BF16),
                   jax.ShapeDtypeStruct((nb, MEM_HEADS * VT_ROWS, N_MEM), BF16)],
        compiler_params=_cparams(1),
        name="mem_proj",
    )(mem2d, w)


def _mem_attn_kernel(qt_ref, k_ref, vt_ref, o_ref):
    qt = qt_ref[0]
    kc = k_ref[0]
    for pair in range(MEM_HEADS // 2):
        qt_p = qt[pair * LANES:(pair + 1) * LANES]
        k_p = kc[:, pair * LANES:(pair + 1) * LANES]
        rowi = lax.broadcasted_iota(jnp.int32, qt_p.shape, 0)
        zero = jnp.zeros_like(qt_p)
        outs = []
        for h in range(2):
            mine = (rowi < HEAD_DIM) if h == 0 else (rowi >= HEAD_DIM)
            s = jnp.dot(k_p, jnp.where(mine, qt_p, zero), preferred_element_type=F32)
            p = jnp.exp2(s - jnp.max(s, axis=0, keepdims=True)).astype(BF16)
            r0 = (2 * pair + h) * VT_ROWS
            acc = jnp.dot(vt_ref[0, r0:r0 + VT_ROWS, :], p, preferred_element_type=F32)
            outs.append(acc[:HEAD_DIM] * (1.0 / acc[HEAD_DIM:HEAD_DIM + 1]))
        o_ref[0, :, pair * LANES:(pair + 1) * LANES] = (
            jnp.concatenate(outs, axis=0).T.astype(o_ref.dtype))


def _memory_attention(mqt, mk, mvt):
    nb, _, seq = mqt.shape
    tq = ATT_TQ
    return pl.pallas_call(
        _mem_attn_kernel,
        grid=(nb, seq // tq),
        in_specs=[pl.BlockSpec((1, MEM_WIDTH, tq), lambda b, i: (b, 0, i)),
                  pl.BlockSpec((1, N_MEM, MEM_WIDTH), lambda b, i: (b, 0, 0)),
                  pl.BlockSpec((1, MEM_HEADS * VT_ROWS, N_MEM), lambda b, i: (b, 0, 0))],
        out_specs=pl.BlockSpec((1, tq, MEM_WIDTH), lambda b, i: (b, i, 0)),
        out_shape=jax.ShapeDtypeStruct((nb, seq, MEM_WIDTH), BF16),
        compiler_params=_cparams(2),
        name="memory_attention",
    )(mqt, mk, mvt)


def _post_attn_kernel(x_ref, att_ref, memo_ref, wa_ref, wm_ref, g_ref, b_ref, wr_ref, br_ref,
                      x1_ref, x1p_ref, idx_ref, rank_ref, gate_ref, cnt_ref, carry_ref):
    t = pl.program_id(0)

    @pl.when(t == 0)
    def _():
        carry_ref[...] = jnp.zeros_like(carry_ref)

    tm = x_ref.shape[0]
    mix = (jnp.dot(att_ref[...], wa_ref[...], preferred_element_type=F32)
           + jnp.dot(memo_ref[...], wm_ref[...], preferred_element_type=F32))
    x1 = _layer_norm(DEEPNORM_ALPHA * x_ref[...] + mix, g_ref[...], b_ref[...])
    x1_ref[...] = x1
    x1b = x1.astype(BF16)
    x1p_ref[...] = _pack_bf16_pairs(x1b)
    logits = jnp.dot(x1b, wr_ref[...], preferred_element_type=F32) + br_ref[...]
    lane = lax.broadcasted_iota(jnp.int32, (tm, LANES), 1)
    work = jnp.where(lane < N_EXPERTS, logits, -jnp.inf)
    idxs, vals = [], []
    onehot = jnp.zeros((tm, LANES), F32)
    for _ in range(TOP_K):
        best = jnp.max(work, axis=1, keepdims=True)
        where_best = jnp.argmax(work, axis=1, keepdims=True).astype(jnp.int32)
        hit = lane == where_best
        onehot = jnp.where(hit, 1.0, onehot)
        work = jnp.where(hit, -jnp.inf, work)
        idxs.append(where_best)
        vals.append(best)
    exps = [jnp.exp(v - vals[0]) for v in vals]
    denom = exps[0] + exps[1] + exps[2] + exps[3]
    row = lax.broadcasted_iota(jnp.int32, (tm, tm), 0)
    col = lax.broadcasted_iota(jnp.int32, (tm, tm), 1)
    strict = jnp.where(row > col, 1.0, 0.0).astype(BF16)
    before = jnp.dot(strict, onehot.astype(BF16), preferred_element_type=F32) + carry_ref[...]
    idx_out = jnp.zeros((tm, LANES), F32)
    rank_out = jnp.zeros((tm, LANES), F32)
    gate_out = jnp.zeros((tm, LANES), F32)
    for r in range(TOP_K):
        rank_r = jnp.sum(jnp.where(lane == idxs[r], before, 0.0), axis=1, keepdims=True)
        idx_out = jnp.where(lane == r, idxs[r].astype(F32), idx_out)
        rank_out = jnp.where(lane == r, rank_r, rank_out)
        gate_out = jnp.where(lane == r, exps[r] / denom, gate_out)
    idx_ref[...] = idx_out.T[:8].astype(jnp.int32)
    rank_ref[...] = rank_out.T[:8].astype(jnp.int32)
    gate_ref[...] = gate_out
    total = carry_ref[...] + jnp.sum(onehot, axis=0, keepdims=True)
    carry_ref[...] = total
    cnt_ref[...] = jnp.broadcast_to(total, cnt_ref.shape)


def _post_attn(x2d, att, memo, wa, wm, g, b, wr, br):
    t_tokens = x2d.shape[0]
    tm = PROJ_TM
    row_spec = lambda width: pl.BlockSpec((tm, width), lambda i: (i, 0))
    full = lambda a: pl.BlockSpec(a.shape, lambda i: (0, 0))
    return pl.pallas_call(
        _post_attn_kernel,
        grid=(t_tokens // tm,),
        in_specs=[row_spec(D_MODEL), row_spec(att.shape[1]), row_spec(MEM_WIDTH),
                  full(wa), full(wm), full(g), full(b), full(wr), full(br)],
        out_specs=[row_spec(D_MODEL), row_spec(D_MODEL // 2),
                   pl.BlockSpec((8, tm), lambda i: (0, i)), pl.BlockSpec((8, tm), lambda i: (0, i)),
                   row_spec(LANES), pl.BlockSpec((8, LANES), lambda i: (0, 0))],
        out_shape=[jax.ShapeDtypeStruct((t_tokens, D_MODEL), F32),
                   jax.ShapeDtypeStruct((t_tokens, D_MODEL // 2), jnp.uint32),
                   jax.ShapeDtypeStruct((8, t_tokens), jnp.int32),
                   jax.ShapeDtypeStruct((8, t_tokens), jnp.int32),
                   jax.ShapeDtypeStruct((t_tokens, LANES), F32),
                   jax.ShapeDtypeStruct((8, LANES), F32)],
        scratch_shapes=[pltpu.VMEM((1, LANES), F32)],
        compiler_params=_cparams(1),
        name="outproj_ln_router",
    )(x2d, att, memo, wa, wm, g, b, wr, br)


def _slot_kernel(pstart_ref, idx_ref, rank_ref, o_ref):
    idx = idx_ref[...]
    dest = rank_ref[...]
    for e in range(N_EXPERTS):
        dest = dest + jnp.where(idx == e, pstart_ref[e], 0)
    o_ref[...] = dest


def _slot_table(idx, rank, pstart):
    return pl.pallas_call(
        _slot_kernel,
        in_specs=[pl.BlockSpec(memory_space=pltpu.SMEM), pl.BlockSpec(memory_space=pltpu.VMEM),
                  pl.BlockSpec(memory_space=pltpu.VMEM)],
        out_specs=pl.BlockSpec(memory_space=pltpu.VMEM),
        out_shape=jax.ShapeDtypeStruct(idx.shape, jnp.int32),
        name="slot_table",
    )(pstart, idx, rank)


def _expert_kernel(be_ref, nu_ref, br_ref, x_ref, wgu_hbm, bgu_ref, wdn_hbm, bdn_ref, *rest,
                   n_chunks, blk_lo, n_blocks, layer, has_prev):
    o_ref, wgu_st, wdn_st, wgu_bf, wdn_bf, h_ref, sem = rest[1:] if has_prev else rest
    step = pl.program_id(0)
    blk = step + blk_lo
    live_end = jnp.minimum(blk_lo + pl.num_programs(0), nu_ref[0])
    expert = be_ref[blk]

    def fetch(e):
        return (pltpu.make_async_copy(wgu_hbm.at[layer, e], wgu_st, sem.at[0]),
                pltpu.make_async_copy(wdn_hbm.at[layer, e], wdn_st, sem.at[1]))

    @pl.when(((step == 0) | (expert != be_ref[jnp.maximum(blk - 1, 0)])) & (blk < live_end))
    def _():
        @pl.when(step == 0)
        def _():
            for copy in fetch(expert):
                copy.start()

        for copy in fetch(expert):
            copy.wait()
        wgu_bf[...] = wgu_st[...].astype(BF16)
        wdn_bf[...] = wdn_st[...].astype(BF16)
        nxt = lax.while_loop(
            lambda j: (j < live_end) & (be_ref[jnp.minimum(j, n_blocks - 1)] == expert),
            lambda j: j + 1, blk + 1)

        @pl.when(nxt < live_end)
        def _():
            for copy in fetch(be_ref[jnp.minimum(nxt, n_blocks - 1)]):
                copy.start()

    def mlp(rows):
        x = _unpack_bf16_pairs(x_ref[:rows]).astype(BF16)
        cw = D_EXPERT // n_chunks
        for c in range(n_chunks):
            gs = slice(c * cw, (c + 1) * cw)
            us = slice(D_EXPERT + c * cw, D_EXPERT + (c + 1) * cw)
            g = jnp.dot(x, wgu_bf[:, gs], preferred_element_type=F32) + bgu_ref[0, 0, :, gs]
            u = jnp.dot(x, wgu_bf[:, us], preferred_element_type=F32) + bgu_ref[0, 0, :, us]
            g = jnp.minimum(g, SWIGLU_LIMIT)
            u = jnp.clip(u, -SWIGLU_LIMIT, SWIGLU_LIMIT)
            h_ref[:rows, gs] = ((u + 1.0) * (g * jax.nn.sigmoid(SWIGLU_ALPHA * g))).astype(BF16)
        y = jnp.dot(h_ref[:rows], wdn_bf[...], preferred_element_type=F32) + bdn_ref[0, 0]
        return _pack_bf16_pairs(y.astype(BF16))

    tm = o_ref.shape[0]
    live_rows = br_ref[blk]

    @pl.when(live_rows > tm // 2)
    def _():
        o_ref[...] = mlp(tm)

    @pl.when((live_rows > 0) & (live_rows <= tm // 2))
    def _():
        o_ref[:tm // 2] = mlp(tm // 2)
        o_ref[tm // 2:] = jnp.zeros((tm - tm // 2, o_ref.shape[1]), o_ref.dtype)

    @pl.when(live_rows == 0)
    def _():
        o_ref[...] = jnp.zeros_like(o_ref)


def _experts(block_expert, n_used, block_rows, xb, blk_lo, p_rows, y_prev, layer, wgu, bgu, wdn, bdn):
    tm = MOE_TM
    has_prev = y_prev is not None
    b_map = lambda i, be, nu, br: (layer, be[i + blk_lo], 0, 0)
    in_specs = [pl.BlockSpec((tm, D_MODEL // 2), lambda i, be, nu, br: (i, 0)),
                pl.BlockSpec(memory_space=pl.ANY),
                pl.BlockSpec((1, 1, 1, 2 * D_EXPERT), b_map),
                pl.BlockSpec(memory_space=pl.ANY),
                pl.BlockSpec((1, 1, 1, D_MODEL), b_map)]
    args = [block_expert, n_used, block_rows, xb, wgu, bgu, wdn, bdn]
    if has_prev:
        in_specs.append(pl.BlockSpec(memory_space=pl.ANY))
        args.append(y_prev)
    grid_spec = pltpu.PrefetchScalarGridSpec(
        num_scalar_prefetch=3,
        grid=(xb.shape[0] // tm,),
        in_specs=in_specs,
        out_specs=pl.BlockSpec((tm, D_MODEL // 2), lambda i, be, nu, br: (i + blk_lo, 0)),
        scratch_shapes=[pltpu.VMEM((D_MODEL, 2 * D_EXPERT), F32),
                        pltpu.VMEM((D_EXPERT, D_MODEL), F32),
                        pltpu.VMEM((D_MODEL, 2 * D_EXPERT), BF16),
                        pltpu.VMEM((D_EXPERT, D_MODEL), BF16),
                        pltpu.VMEM((tm, D_EXPERT), BF16),
                        pltpu.SemaphoreType.DMA((2,))],
    )
    return pl.pallas_call(
        functools.partial(_expert_kernel, n_chunks=4, blk_lo=blk_lo, n_blocks=p_rows // tm,
                          layer=layer, has_prev=has_prev),
        grid_spec=grid_spec,
        out_shape=jax.ShapeDtypeStruct((p_rows, D_MODEL // 2), jnp.uint32),
        input_output_aliases={len(args) - 1: 0} if has_prev else {},
        compiler_params=_cparams(1),
        name="experts",
    )(*args)


def _sc_move_rows(table_hbm, idx_v, out_hbm, base, n_chunks, rows_v, gsem, wsem):
    def gather(c, slot):
        off = pl.multiple_of(c * SC_CHUNK, SC_CHUNK)
        return pltpu.make_async_copy(table_hbm.at[idx_v.at[pl.ds(off, SC_CHUNK)]],
                                     rows_v.at[slot], gsem.at[slot])

    def put(c, slot):
        off = pl.multiple_of(c * SC_CHUNK, SC_CHUNK)
        return pltpu.make_async_copy(rows_v.at[slot], out_hbm.at[pl.ds(base + off, SC_CHUNK)],
                                     wsem.at[slot])

    gather(0, 0).start()
    gather(1, 1).start()

    @pl.loop(0, n_chunks, step=2)
    def _(c):
        for slot in range(2):
            gather(c + slot, slot).wait()
            put(c + slot, slot).start()
        for slot in range(2):
            put(c + slot, slot).wait()

            @pl.when(c + 2 + slot < n_chunks)
            def _():
                gather(c + 2 + slot, slot).start()


def _sc_gather_rows(table, idx):
    n_idx = idx.shape[0]
    width = table.shape[1]
    per_worker = n_idx // SC_WORKERS
    n_chunks = per_worker // SC_CHUNK
    mesh = plsc.VectorSubcoreMesh(core_axis_name="c", subcore_axis_name="s",
                                  num_cores=SC_CORES, num_subcores=SC_SUBCORES)

    def body(table_hbm, idx_hbm, out_hbm, idx_v, rows_v, gsem, wsem):
        wid = lax.axis_index("s") * SC_CORES + lax.axis_index("c")
        base = wid * per_worker
        pltpu.sync_copy(idx_hbm.at[pl.ds(base, per_worker)], idx_v)
        _sc_move_rows(table_hbm, idx_v, out_hbm, base, n_chunks, rows_v, gsem, wsem)

    return pl.kernel(
        body,
        out_type=jax.ShapeDtypeStruct((n_idx, width), table.dtype),
        mesh=mesh,
        scratch_types=[pltpu.VMEM((per_worker,), jnp.int32),
                       pltpu.VMEM((2, SC_CHUNK, width), table.dtype),
                       pltpu.SemaphoreType.DMA((2,)),
                       pltpu.SemaphoreType.DMA((2,))],
        name="sc_gather_rows",
    )(table, idx)


def _sc_dispatch_rows(table, dest, slot_lo, n_slots):
    n_tok, width = table.shape
    n_pairs = dest.shape[0]
    per_worker = n_slots // SC_WORKERS
    n_chunks = per_worker // SC_CHUNK
    n_scan = n_pairs // SC_SCAN
    mesh = plsc.VectorSubcoreMesh(core_axis_name="c", subcore_axis_name="s",
                                  num_cores=SC_CORES, num_subcores=SC_SUBCORES)

    def body(table_hbm, dest_hbm, out_hbm, tok_v, dest_v, rows_v, gsem, wsem):
        wid = lax.axis_index("s") * SC_CORES + lax.axis_index("c")
        base = wid * per_worker
        first = slot_lo + base
        lane = lax.iota(jnp.int32, SC_LANES)

        @pl.loop(0, per_worker // SC_LANES)
        def _(i):
            off = pl.multiple_of(i * SC_LANES, SC_LANES)
            tok_v[pl.ds(off, SC_LANES)] = lax.rem(first + off + lane, n_tok)

        @pl.loop(0, n_scan)
        def _(g):
            goff = pl.multiple_of(g * SC_SCAN, SC_SCAN)
            pltpu.sync_copy(dest_hbm.at[pl.ds(goff, SC_SCAN)], dest_v)

            @plsc.parallel_loop(0, SC_SCAN // SC_LANES, unroll=8)
            def _(i):
                off = pl.multiple_of(i * SC_LANES, SC_LANES)
                local = dest_v[pl.ds(off, SC_LANES)] - first
                mine = (local >= 0) & (local < per_worker)
                pair = goff + off + lane
                plsc.store_scatter(tok_v, [jnp.where(mine, local, 0)], lax.rem(pair, n_tok), mask=mine)

        _sc_move_rows(table_hbm, tok_v, out_hbm, base, n_chunks, rows_v, gsem, wsem)

    return pl.kernel(
        body,
        out_type=jax.ShapeDtypeStruct((n_slots, width), table.dtype),
        mesh=mesh,
        scratch_types=[pltpu.VMEM((per_worker,), jnp.int32),
                       pltpu.VMEM((SC_SCAN,), jnp.int32),
                       pltpu.VMEM((2, SC_CHUNK, width), table.dtype),
                       pltpu.SemaphoreType.DMA((2,)),
                       pltpu.SemaphoreType.DMA((2,))],
        compiler_params=pltpu.CompilerParams(needs_layout_passes=False),
        name="sc_dispatch_rows",
    )(table, dest)


def _combine_kernel(x1_ref, yg_ref, gate_ref, g_ref, b_ref, *rest):
    o_ref = rest[-1]
    gates = gate_ref[...]
    ffn = _unpack_bf16_pairs(yg_ref[0]) * gates[:, 0:1]
    for r in range(1, TOP_K):
        ffn = ffn + _unpack_bf16_pairs(yg_ref[r]) * gates[:, r:r + 1]
    o_ref[...] = _layer_norm(DEEPNORM_ALPHA * x1_ref[...] + ffn, g_ref[...], b_ref[...])


def _combine(x1, yg, gates, g, b, tok_lo, out_prev):
    t_tokens = x1.shape[0]
    tm = COMBINE_TM
    tile_lo = tok_lo // tm
    in_specs = [pl.BlockSpec((tm, D_MODEL), lambda i: (i + tile_lo, 0)),
                pl.BlockSpec((TOP_K, tm, D_MODEL // 2), lambda i: (0, i, 0)),
                pl.BlockSpec((tm, LANES), lambda i: (i + tile_lo, 0)),
                pl.BlockSpec((1, D_MODEL), lambda i: (0, 0)),
                pl.BlockSpec((1, D_MODEL), lambda i: (0, 0))]
    args = [x1, yg, gates, g, b]
    if out_prev is not None:
        in_specs.append(pl.BlockSpec(memory_space=pl.ANY))
        args.append(out_prev)
    return pl.pallas_call(
        _combine_kernel,
        grid=(yg.shape[1] // tm,),
        in_specs=in_specs,
        out_specs=pl.BlockSpec((tm, D_MODEL), lambda i: (i + tile_lo, 0)),
        out_shape=jax.ShapeDtypeStruct((t_tokens, D_MODEL), F32),
        input_output_aliases={len(args) - 1: 0} if out_prev is not None else {},
        compiler_params=_cparams(1),
        name="combine_ln",
    )(*args)


def _moe(x1, x1p, idx, rank, gates, counts, layer, wgu, bgu, wdn, bdn, g, b):
    t_tokens = x1.shape[0]
    tm = MOE_TM
    counts = counts[0, :N_EXPERTS].astype(jnp.int32)
    padded = ((counts + tm - 1) // tm) * tm
    pend = jnp.cumsum(padded)
    pstart = pend - padded
    dest = _slot_table(idx, rank, pstart)[:TOP_K]
    n_blocks = (t_tokens * TOP_K) // tm + N_EXPERTS
    p_rows = n_blocks * tm
    block_start = jnp.arange(n_blocks, dtype=jnp.int32) * tm
    block_expert = jnp.minimum(
        jnp.sum((pend[None, :] <= block_start[:, None]).astype(jnp.int32), axis=1), N_EXPERTS - 1)
    n_used = (pend[-1:] // tm).astype(jnp.int32)
    mine = block_expert[:, None] == jnp.arange(N_EXPERTS, dtype=jnp.int32)[None, :]
    live_end = jnp.sum(jnp.where(mine, (pstart + counts)[None, :], 0), axis=1)
    block_rows = jnp.clip(live_end - block_start, 0, tm)
    y = None
    for lo, hi in ((0, n_blocks // MOE_SPLIT), (n_blocks // MOE_SPLIT, n_blocks)):
        xb = _sc_dispatch_rows(x1p, dest.reshape(-1), lo * tm, (hi - lo) * tm)
        y = _experts(block_expert, n_used, block_rows, xb, lo, p_rows, y, layer, wgu, bgu, wdn, bdn)
    out = None
    for lo, hi in ((0, t_tokens // MOE_SPLIT), (t_tokens // MOE_SPLIT, t_tokens)):
        yg = _sc_gather_rows(y, dest[:, lo:hi].reshape(-1)).reshape(TOP_K, hi - lo, D_MODEL // 2)
        out = _combine(x1, yg, gates, g, b, lo, out)
    return out


def _row(v, width=None):
    v = v.astype(F32).reshape(1, -1)
    if width is not None and v.shape[1] < width:
        v = jnp.pad(v, ((0, 0), (0, width - v.shape[1])))
    return v


def _pad_cols(w, width):
    return jnp.pad(w, ((0, 0), (0, width - w.shape[1])))


def kernel(x, mem, positions, a_w_in, a_b_f, a_w_out, b_w_in, b_g_q, b_w_uq, b_w_out,
           kv_w_dkv, kv_g, kv_w_ukv, mem_w_kv, ln_g, ln_b,
           moe_w_r, moe_b_r, moe_w_gu, moe_b_gu, moe_w_dn, moe_b_dn):
    nb, seq, d = x.shape
    t_tokens = nb * seq
    n_a = a_w_in.shape[0]
    x2d = x.reshape(t_tokens, d)
    mem2d = mem.reshape(nb * N_MEM, d)
    pos2d = positions.reshape(1, t_tokens)
    half = QK_ROPE // 2
    invf = (ROPE_THETA ** (-jnp.arange(half, dtype=F32) * 2.0 / QK_ROPE)).reshape(half, 1)

    shared_kv = None
    for l in range(DEPTH):
        mk, mvt = _mem_proj(mem2d, mem_w_kv[l].astype(BF16), nb)
        if l < n_a:
            w_in = a_w_in[l]
            w = jnp.concatenate([w_in[:, :3 * FOX_WIDTH],
                                 _pad_cols(w_in[:, 3 * FOX_WIDTH:3 * FOX_WIDTH + FOX_HEADS], LANES),
                                 w_in[:, 3 * FOX_WIDTH + FOX_HEADS:]], axis=1).astype(BF16)
            qt, k, vt, mqt, cum, cumt, cfirst, clast, knorm = _fox_proj(
                x2d, w, _row(a_b_f[l], LANES), seq)
            stats = [s.reshape(-1, LANES) for s in (cfirst, clast, knorm)]
            att = _causal_attention(qt, k.reshape(nb, seq, -1), vt,
                                    [cum.reshape(nb, seq, LANES), cumt] + stats, slab=False)
            w_out = a_w_out[l]
        else:
            bl = l - n_a
            w = jnp.concatenate([b_w_in[bl], kv_w_dkv[:, :KV_LORA],
                                 jnp.zeros((d, QK_NOPE), F32), kv_w_dkv[:, KV_LORA:],
                                 jnp.zeros((d, LANES - QK_NOPE - QK_ROPE), F32)], axis=1).astype(BF16)
            wuq = b_w_uq[bl].reshape(Q_LORA, MLA_HEADS, QK_NOPE + QK_ROPE)
            wuq = jnp.pad(wuq, ((0, 0), (0, 0), (0, MLA_SLAB - QK_NOPE - QK_ROPE)))
            wuq = wuq.reshape(Q_LORA, MLA_HEADS * MLA_SLAB).astype(BF16)
            wukv = kv_w_ukv.reshape(KV_LORA, MLA_HEADS, QK_NOPE + V_DIM)
            wk = jnp.pad(wukv[:, :, :QK_NOPE], ((0, 0), (0, 0), (0, MLA_SLAB - QK_NOPE)))
            wk = wk.reshape(KV_LORA, MLA_HEADS * MLA_SLAB).astype(BF16)
            wv = wukv[:, :, QK_NOPE:].reshape(KV_LORA, MLA_V_WIDTH).astype(BF16)
            qt, mqt, k_new, vt_new = _mla_proj(x2d, pos2d, invf, w, _row(b_g_q[bl]), wuq,
                                              _row(kv_g), wk, wv, seq)
            if shared_kv is None:
                shared_kv = (k_new.reshape(nb, seq, -1), vt_new)
            att = _causal_attention(qt, shared_kv[0], shared_kv[1], None, slab=True)
            w_out = b_w_out[bl]
        memo = _memory_attention(mqt, mk, mvt)
        n_att = w_out.shape[0] - MEM_WIDTH
        x1, x1p, idx, rank, gates, counts = _post_attn(
            x2d, att.reshape(t_tokens, -1), memo.reshape(t_tokens, MEM_WIDTH),
            w_out[:n_att].astype(BF16), w_out[n_att:].astype(BF16),
            _row(ln_g[l, 0]), _row(ln_b[l, 0]),
            _pad_cols(moe_w_r[l], LANES).astype(BF16), _row(moe_b_r[l], LANES))
        x2d = _moe(x1, x1p, idx, rank, gates, counts, l,
                   moe_w_gu, moe_b_gu.reshape(DEPTH, N_EXPERTS, 1, -1),
                   moe_w_dn, moe_b_dn.reshape(DEPTH, N_EXPERTS, 1, -1),
                   _row(ln_g[l, 1]), _row(ln_b[l, 1]))
    return x2d.reshape(nb, seq, d)
```

```python
import functools
import math

import jax
import jax.numpy as jnp
from jax import lax
from jax.experimental import pallas as pl
from jax.experimental.pallas import tpu as pltpu
from jax.experimental.pallas import tpu_sc as plsc

F32 = jnp.float32
BF16 = jnp.bfloat16

D_MODEL = 1024
DEPTH = 2
N_MEM = 256
HEAD_DIM = 64
FOX_HEADS = 12
MEM_HEADS = 4
MLA_HEADS = 12
Q_LORA = 384
KV_LORA = 256
QK_NOPE = 64
QK_ROPE = 32
V_DIM = 64
ROPE_THETA = 10000.0
N_EXPERTS = 32
TOP_K = 4
D_EXPERT = D_MODEL
SWIGLU_LIMIT = 7.0
SWIGLU_ALPHA = 1.702
LN_EPS = 1e-5
RMS_EPS = 1e-6
NEG_INF = -1e30
DEEPNORM_ALPHA = (2 * DEPTH) ** 0.25
FOX_WIDTH = FOX_HEADS * HEAD_DIM
MEM_WIDTH = MEM_HEADS * HEAD_DIM
MLA_V_WIDTH = MLA_HEADS * V_DIM

LANES = 128
LOG2E = math.log2(math.e)
VMEM_LIMIT = 48 * 1024 * 1024

PROJ_TM = 512
ATT_TQ = 512
FOX_TK = 256
MOE_TM = 512
COMBINE_TM = 512
MOE_SPLIT = 4
MLA_SLAB = LANES
VT_ROWS = HEAD_DIM + 16
SC_CORES = 2
SC_SUBCORES = 16
SC_WORKERS = SC_CORES * SC_SUBCORES
SC_CHUNK = 32
SC_LANES = 16
SC_SCAN = 8192


def _cparams(n_axes):
    return pltpu.CompilerParams(dimension_semantics=("arbitrary",) * n_axes,
                                vmem_limit_bytes=VMEM_LIMIT)


def _split3(x):
    hi = x.astype(BF16)
    r1 = x - hi.astype(F32)
    mid = r1.astype(BF16)
    lo = (r1 - mid.astype(F32)).astype(BF16)
    return hi, mid, lo


def _pack_bf16_pairs(v):
    bits = pltpu.bitcast(v.astype(F32), jnp.uint32)
    half = v.shape[1] // 2
    return (bits[:, :half] >> 16) | bits[:, half:]


def _unpack_bf16_pairs(words):
    return jnp.concatenate([pltpu.bitcast(words << 16, F32),
                            pltpu.bitcast(words & jnp.uint32(0xFFFF0000), F32)], axis=1)


def _layer_norm(y, g, b):
    mu = jnp.mean(y, axis=-1, keepdims=True)
    yc = y - mu
    var = jnp.mean(yc * yc, axis=-1, keepdims=True)
    return yc * lax.rsqrt(var + LN_EPS) * g + b


def _rms_norm(y, g):
    return y * lax.rsqrt(jnp.mean(y * y, axis=-1, keepdims=True) + RMS_EPS) * g


def _store_transposed(dst_ref, val):
    for s in range(val.shape[1] // LANES):
        sl = slice(s * LANES, (s + 1) * LANES)
        dst_ref[0, sl, :] = val[:, sl].T.astype(dst_ref.dtype)


def _store_values_transposed(vt_ref, val):
    tm = val.shape[0]
    ones = jnp.ones((VT_ROWS - HEAD_DIM, tm), vt_ref.dtype)
    for s in range(val.shape[1] // LANES):
        pair_t = val[:, s * LANES:(s + 1) * LANES].T.astype(vt_ref.dtype)
        for h in range(2):
            r0 = (2 * s + h) * VT_ROWS
            vt_ref[0, r0:r0 + HEAD_DIM, :] = pair_t[h * HEAD_DIM:(h + 1) * HEAD_DIM]
            vt_ref[0, r0 + HEAD_DIM:r0 + VT_ROWS, :] = ones


def _fox_proj_kernel(x_ref, w_ref, bf_ref, seg_ref, qt_ref, k_ref, vt_ref, mqt_ref, cum_ref,
                     cumt_ref, cfirst_ref, clast_ref, knorm_ref, carry_ref, knmax_ref, *,
                     tiles_per_batch):
    t = pl.program_id(0)

    @pl.when(t % tiles_per_batch == 0)
    def _():
        carry_ref[...] = jnp.zeros_like(carry_ref)
        knmax_ref[...] = jnp.zeros_like(knmax_ref)

    tm = x_ref.shape[0]
    proj = jnp.dot(x_ref[...].astype(BF16), w_ref[...], preferred_element_type=F32)
    qscale = HEAD_DIM ** -0.5 * LOG2E
    _store_transposed(qt_ref, proj[:, :FOX_WIDTH] * qscale)
    kb = proj[:, FOX_WIDTH:2 * FOX_WIDTH].astype(BF16)
    k_ref[...] = kb
    _store_values_transposed(vt_ref, proj[:, 2 * FOX_WIDTH:3 * FOX_WIDTH])
    kf = kb.astype(F32)
    ksq = jnp.dot((kf * kf).astype(BF16), seg_ref[...], preferred_element_type=F32)
    n_sub = tm // FOX_TK
    for sub in range(n_sub):
        tile_max = jnp.sqrt(jnp.max(ksq[sub * FOX_TK:(sub + 1) * FOX_TK], axis=0, keepdims=True))
        knmax_ref[...] = jnp.maximum(knmax_ref[...], tile_max)
        knorm_ref[sub] = knmax_ref[...]
    f = proj[:, 3 * FOX_WIDTH:3 * FOX_WIDTH + LANES] + bf_ref[...]
    _store_transposed(mqt_ref, proj[:, 3 * FOX_WIDTH + LANES:] * qscale)
    log_f = jnp.minimum(f, 0.0) - jnp.log1p(jnp.exp(-jnp.abs(f)))
    row = lax.broadcasted_iota(jnp.int32, (tm, tm), 0)
    col = lax.broadcasted_iota(jnp.int32, (tm, tm), 1)
    tri = jnp.where(row >= col, 1.0, 0.0).astype(BF16)
    hi, mid, lo = _split3(log_f)
    cum = (jnp.dot(tri, hi, preferred_element_type=F32)
           + jnp.dot(tri, mid, preferred_element_type=F32)
           + jnp.dot(tri, lo, preferred_element_type=F32)) + carry_ref[...]
    carry_ref[...] = cum[tm - 1:tm, :]
    cum2 = cum * LOG2E
    cum_ref[...] = cum2
    cumt_ref[0] = cum2.T[:16, :]
    for sub in range(n_sub):
        cfirst_ref[sub] = cum2[sub * FOX_TK:sub * FOX_TK + 1, :]
        clast_ref[sub] = cum2[(sub + 1) * FOX_TK - 1:(sub + 1) * FOX_TK, :]


def _fox_proj(x2d, w, bf, seq):
    t_tokens = x2d.shape[0]
    tm = PROJ_TM
    nb = t_tokens // seq
    n = w.shape[1]
    tiles_per_batch = seq // tm
    n_tiles = t_tokens // tm
    seg = (jnp.arange(FOX_WIDTH)[:, None] // HEAD_DIM == jnp.arange(LANES)[None, :]).astype(BF16)
    row_spec = lambda width: pl.BlockSpec((tm, width), lambda i: (i, 0))
    t_spec = lambda rows: pl.BlockSpec(
        (1, rows, tm), lambda i: (i // tiles_per_batch, 0, i % tiles_per_batch))
    n_sub = tm // FOX_TK
    stat_spec = pl.BlockSpec((n_sub, 1, LANES), lambda i: (i, 0, 0))
    stat_shape = jax.ShapeDtypeStruct((n_tiles * n_sub, 1, LANES), F32)
    return pl.pallas_call(
        functools.partial(_fox_proj_kernel, tiles_per_batch=tiles_per_batch),
        grid=(n_tiles,),
        in_specs=[row_spec(D_MODEL),
                  pl.BlockSpec((D_MODEL, n), lambda i: (0, 0)),
                  pl.BlockSpec((1, LANES), lambda i: (0, 0)),
                  pl.BlockSpec((FOX_WIDTH, LANES), lambda i: (0, 0))],
        out_specs=[t_spec(FOX_WIDTH), row_spec(FOX_WIDTH), t_spec(FOX_HEADS * VT_ROWS),
                   t_spec(MEM_WIDTH), row_spec(LANES), t_spec(16),
                   stat_spec, stat_spec, stat_spec],
        out_shape=[jax.ShapeDtypeStruct((nb, FOX_WIDTH, seq), BF16),
                   jax.ShapeDtypeStruct((t_tokens, FOX_WIDTH), BF16),
                   jax.ShapeDtypeStruct((nb, FOX_HEADS * VT_ROWS, seq), BF16),
                   jax.ShapeDtypeStruct((nb, MEM_WIDTH, seq), BF16),
                   jax.ShapeDtypeStruct((t_tokens, LANES), F32),
                   jax.ShapeDtypeStruct((nb, 16, seq), F32),
                   stat_shape, stat_shape, stat_shape],
        scratch_shapes=[pltpu.VMEM((1, LANES), F32), pltpu.VMEM((1, LANES), F32)],
        compiler_params=_cparams(1),
        name="fox_proj",
    )(x2d, w, bf, seg)


def _rope_slab_t(slab_t, cos_t, sin_t):
    half = QK_ROPE // 2
    x1 = slab_t[QK_NOPE:QK_NOPE + half]
    x2 = slab_t[QK_NOPE + half:QK_NOPE + QK_ROPE]
    return jnp.concatenate([slab_t[:QK_NOPE], x1 * cos_t - x2 * sin_t, x1 * sin_t + x2 * cos_t,
                            slab_t[QK_NOPE + QK_ROPE:]], axis=0)


def _mla_proj_kernel(x_ref, pos_ref, invf_ref, w_ref, gq_ref, wuq_ref, gkv_ref, wk_ref, wv_ref,
                     qt_ref, mqt_ref, k_ref, vt_ref):
    proj = jnp.dot(x_ref[...].astype(BF16), w_ref[...], preferred_element_type=F32)
    ang_t = invf_ref[...] * pos_ref[...].astype(F32)
    cos_t = jnp.cos(ang_t)
    sin_t = jnp.sin(ang_t)
    c_q = _rms_norm(proj[:, :Q_LORA], gq_ref[...])
    _store_transposed(mqt_ref, proj[:, Q_LORA:Q_LORA + MEM_WIDTH] * (HEAD_DIM ** -0.5 * LOG2E))
    kv_off = Q_LORA + MEM_WIDTH
    c_kv = _rms_norm(proj[:, kv_off:kv_off + KV_LORA], gkv_ref[...])
    kr = _rope_slab_t(proj[:, kv_off + KV_LORA:].T, cos_t, sin_t).T
    q = jnp.dot(c_q.astype(BF16), wuq_ref[...], preferred_element_type=F32)
    kn = jnp.dot(c_kv.astype(BF16), wk_ref[...], preferred_element_type=F32)
    qscale = (QK_NOPE + QK_ROPE) ** -0.5 * LOG2E
    for h in range(MLA_HEADS):
        sl = slice(h * MLA_SLAB, (h + 1) * MLA_SLAB)
        qt_ref[0, sl, :] = (_rope_slab_t(q[:, sl].T, cos_t, sin_t) * qscale).astype(BF16)
        k_ref[:, sl] = (kn[:, sl] + kr).astype(BF16)
    _store_values_transposed(vt_ref, jnp.dot(c_kv.astype(BF16), wv_ref[...],
                                             preferred_element_type=F32))


def _mla_proj(x2d, pos2d, invf, w, gq, wuq, gkv, wk, wv, seq):
    t_tokens = x2d.shape[0]
    tm = PROJ_TM
    nb = t_tokens // seq
    tiles_per_batch = seq // tm
    row_spec = lambda width: pl.BlockSpec((tm, width), lambda i: (i, 0))
    t_spec = lambda rows: pl.BlockSpec(
        (1, rows, tm), lambda i: (i // tiles_per_batch, 0, i % tiles_per_batch))
    full = lambda a: pl.BlockSpec(a.shape, lambda i: (0, 0))
    slabs = MLA_HEADS * MLA_SLAB
    return pl.pallas_call(
        _mla_proj_kernel,
        grid=(t_tokens // tm,),
        in_specs=[row_spec(D_MODEL), pl.BlockSpec((1, tm), lambda i: (0, i)), full(invf), full(w),
                  full(gq), full(wuq), full(gkv), full(wk), full(wv)],
        out_specs=[t_spec(slabs), t_spec(MEM_WIDTH), row_spec(slabs),
                   t_spec(MLA_HEADS * VT_ROWS)],
        out_shape=[jax.ShapeDtypeStruct((nb, slabs, seq), BF16),
                   jax.ShapeDtypeStruct((nb, MEM_WIDTH, seq), BF16),
                   jax.ShapeDtypeStruct((t_tokens, slabs), BF16),
                   jax.ShapeDtypeStruct((nb, MLA_HEADS * VT_ROWS, seq), BF16)],
        compiler_params=_cparams(1),
        name="mla_proj",
    )(x2d, pos2d, invf, w, gq, wuq, gkv, wk, wv)


SKIP_LOG2 = -160.0
NORM_SLACK = 1.02


def _causal_attn_kernel(*refs, tq, tk, n_chunks, fox, slab, long_body):
    if fox:
        (cf_ref, cl_ref, kn_ref, qt_ref, k_ref, vt_ref, cq_ref, ck_ref,
         o_ref, m_ref, acc_ref, s_ref, cmax_ref) = refs
    else:
        qt_ref, k_ref, vt_ref, o_ref, m_ref, acc_ref, s_ref, cmax_ref = refs
    b = pl.program_id(0)
    pair = pl.program_id(1)
    i = pl.program_id(2)
    m_ref[...] = jnp.full(m_ref.shape, NEG_INF, F32)
    acc_ref[...] = jnp.zeros(acc_ref.shape, F32)
    qt = qt_ref[0]
    if slab:
        qth = [qt[:MLA_SLAB], qt[MLA_SLAB:]]
    else:
        rowi = lax.broadcasted_iota(jnp.int32, qt.shape, 0)
        zero = jnp.zeros_like(qt)
        qth = [jnp.where(rowi < HEAD_DIM, qt, zero), jnp.where(rowi >= HEAD_DIM, qt, zero)]
    if fox:
        lane = lax.broadcasted_iota(jnp.int32, (1, LANES), 1)
        cq = [cq_ref[0, pl.ds(2 * pair + h, 1), :] for h in range(2)]

    def logits_to(slot, j):
        off = pl.multiple_of(j * tk, tk)
        kc = k_ref[0, pl.ds(off, tk), :]
        for h in range(2):
            kh = kc[:, h * MLA_SLAB:(h + 1) * MLA_SLAB] if slab else kc
            s = jnp.dot(kh, qth[h], preferred_element_type=F32)
            if fox:
                ck_blk = ck_ref[0, pl.ds(off, tk), :]
                ck = jnp.sum(jnp.where(lane == 2 * pair + h, ck_blk, 0.0), axis=1, keepdims=True)
                s = s + cq[h] - ck
            s_ref[slot, h] = s
            cmax_ref[slot, h] = jnp.max(s, axis=0, keepdims=True)

    def softmax_pv(slot, j, causal_mask):
        off = pl.multiple_of(j * tk, tk)
        for h in range(2):
            s = s_ref[slot, h]
            if causal_mask is not None:
                s = jnp.where(causal_mask, s, NEG_INF)
                chunk_max = jnp.max(s, axis=0, keepdims=True)
            else:
                chunk_max = cmax_ref[slot, h]
            m_prev = m_ref[h]
            m_new = jnp.maximum(m_prev, chunk_max)
            alpha = jnp.exp2(m_prev - m_new)
            p = jnp.exp2(s - m_new).astype(BF16)
            vth = vt_ref[0, h * VT_ROWS:(h + 1) * VT_ROWS, pl.ds(off, tk)]
            acc_ref[h] = acc_ref[h] * alpha + jnp.dot(vth, p, preferred_element_type=F32)
            m_ref[h] = m_new

    per_tile = tq // tk
    d0 = per_tile * i
    if fox:
        base = b * n_chunks
        qn = []
        for h in range(2):
            qf = qth[h].astype(F32)
            qn.append(jnp.sqrt(jnp.max(jnp.sum(qf * qf, axis=0, keepdims=True))) * NORM_SLACK)

        top = [2.0 * qn[h] * kn_ref[base + d0 + per_tile - 1, 2 * pair + h]
               + cf_ref[base + d0, 2 * pair + h] for h in range(2)]

        def live(j):
            ub = [top[h] - cl_ref[base + jnp.maximum(j, 0), 2 * pair + h] for h in range(2)]
            return (j >= 0) & (jnp.maximum(ub[0], ub[1]) > SKIP_LOG2)

        j0 = lax.while_loop(live, lambda j: j - 1, d0 - 1) + 1
    else:
        j0 = 0
    n_before = d0 - j0

    def causal_mask(shift):
        key = lax.broadcasted_iota(jnp.int32, (tk, tq), 0)
        qry = lax.broadcasted_iota(jnp.int32, (tk, tq), 1)
        return key + shift <= qry

    def finish():
        if per_tile == 2:
            logits_to(1, d0 + 1)
        softmax_pv(0, d0, causal_mask(0))
        if per_tile == 2:
            softmax_pv(1, d0 + 1, causal_mask(tk))

    def pipelined():
        odd = (n_before % 2) == 1

        @pl.when(odd)
        def _():
            logits_to(1, j0)
            logits_to(0, j0 + 1)
            softmax_pv(1, j0, None)

        @pl.when(jnp.logical_not(odd))
        def _():
            logits_to(0, j0)

        def two_steps(j):
            logits_to(1, j + 1)
            softmax_pv(0, j, None)
            logits_to(0, j + 2)
            softmax_pv(1, j + 1, None)

        j_even = j0 + (n_before % 2)
        n_two = n_before // 2
        if long_body:
            @pl.when((n_two % 2) == 1)
            def _():
                two_steps(j_even)

            j_quad = j_even + 2 * (n_two % 2)

            def body(t, carry):
                two_steps(j_quad + 4 * t)
                two_steps(j_quad + 4 * t + 2)
                return carry

            lax.fori_loop(0, n_two // 2, body, 0)
        else:
            def body(t, carry):
                two_steps(j_even + 2 * t)
                return carry

            lax.fori_loop(0, n_two, body, 0)
        finish()

    if fox:
        @pl.when(n_before == 1)
        def _():
            logits_to(1, j0)
            logits_to(0, d0)
            softmax_pv(1, j0, None)
            finish()

        @pl.when(n_before != 1)
        def _():
            pipelined()
    else:
        pipelined()
    out_t = jnp.concatenate(
        [acc_ref[h, :HEAD_DIM] * (1.0 / acc_ref[h, HEAD_DIM:HEAD_DIM + 1]) for h in range(2)],
        axis=0)
    o_ref[0] = out_t.T.astype(o_ref.dtype)


def _causal_attention(qt, k, vt, fox_args, *, slab):
    nb, seq, _ = k.shape
    fox = fox_args is not None
    tq = ATT_TQ
    tk = FOX_TK if fox else ATT_TQ
    n_tiles = seq // tq
    n_pairs = FOX_HEADS // 2
    rows = 2 * MLA_SLAB if slab else LANES
    in_specs = [pl.BlockSpec((1, rows, tq), lambda b, p, i: (b, p, i)),
                pl.BlockSpec((1, seq, rows), lambda b, p, i: (b, 0, p)),
                pl.BlockSpec((1, 2 * VT_ROWS, seq), lambda b, p, i: (b, p, 0))]
    args = [qt, k, vt]
    if fox:
        cum, cumt, cfirst, clast, knorm = fox_args
        smem = pl.BlockSpec(memory_space=pltpu.SMEM)
        in_specs = [smem, smem, smem] + in_specs + [
            pl.BlockSpec((1, 16, tq), lambda b, p, i: (b, 0, i)),
            pl.BlockSpec((1, seq, LANES), lambda b, p, i: (b, 0, 0))]
        args = [cfirst, clast, knorm] + args + [cumt, cum]
    return pl.pallas_call(
        functools.partial(_causal_attn_kernel, tq=tq, tk=tk, n_chunks=seq // tk, fox=fox, slab=slab,
                          long_body=not fox),
        grid=(nb, n_pairs, n_tiles),
        in_specs=in_specs,
        out_specs=pl.BlockSpec((1, tq, LANES), lambda b, p, i: (b, i, p)),
        out_shape=jax.ShapeDtypeStruct((nb, seq, n_pairs * LANES), BF16),
        scratch_shapes=[pltpu.VMEM((2, 1, tq), F32), pltpu.VMEM((2, VT_ROWS, tq), F32),
                        pltpu.VMEM((2, 2, tk, tq), F32), pltpu.VMEM((2, 2, 1, tq), F32)],
        compiler_params=_cparams(3),
        name="fox_attention" if fox else "mla_attention",
    )(*args)


def _mem_proj_kernel(mem_ref, w_ref, mk_ref, mvt_ref):
    mkv = jnp.dot(mem_ref[...].astype(BF16), w_ref[...], preferred_element_type=F32)
    mk_ref[0] = mkv[:, :MEM_WIDTH].astype(BF16)
    _store_values_transposed(mvt_ref, mkv[:, MEM_WIDTH:])


def _mem_proj(mem2d, w, nb):
    return pl.pallas_call(
        _mem_proj_kernel,
        grid=(nb,),
        in_specs=[pl.BlockSpec((N_MEM, D_MODEL), lambda b: (b, 0)),
                  pl.BlockSpec(w.shape, lambda b: (0, 0))],
        out_specs=[pl.BlockSpec((1, N_MEM, MEM_WIDTH), lambda b: (b, 0, 0)),
                   pl.BlockSpec((1, MEM_HEADS * VT_ROWS, N_MEM), lambda b: (b, 0, 0))],
        out_shape=[jax.ShapeDtypeStruct((nb, N_MEM, MEM_WIDTH), BF16),
                   jax.ShapeDtypeStruct((nb, MEM_HEADS * VT_ROWS, N_MEM), BF16)],
        compiler_params=_cparams(1),
        name="mem_proj",
    )(mem2d, w)


def _mem_attn_kernel(qt_ref, k_ref, vt_ref, o_ref):
    qt = qt_ref[0]
    kc = k_ref[0]
    for pair in range(MEM_HEADS // 2):
        qt_p = qt[pair * LANES:(pair + 1) * LANES]
        k_p = kc[:, pair * LANES:(pair + 1) * LANES]
        rowi = lax.broadcasted_iota(jnp.int32, qt_p.shape, 0)
        zero = jnp.zeros_like(qt_p)
        outs = []
        for h in range(2):
            mine = (rowi < HEAD_DIM) if h == 0 else (rowi >= HEAD_DIM)
            s = jnp.dot(k_p, jnp.where(mine, qt_p, zero), preferred_element_type=F32)
            p = jnp.exp2(s - jnp.max(s, axis=0, keepdims=True)).astype(BF16)
            r0 = (2 * pair + h) * VT_ROWS
            acc = jnp.dot(vt_ref[0, r0:r0 + VT_ROWS, :], p, preferred_element_type=F32)
            outs.append(acc[:HEAD_DIM] * (1.0 / acc[HEAD_DIM:HEAD_DIM + 1]))
        o_ref[0, :, pair * LANES:(pair + 1) * LANES] = (
            jnp.concatenate(outs, axis=0).T.astype(o_ref.dtype))


def _memory_attention(mqt, mk, mvt):
    nb, _, seq = mqt.shape
    tq = ATT_TQ
    return pl.pallas_call(
        _mem_attn_kernel,
        grid=(nb, seq // tq),
        in_specs=[pl.BlockSpec((1, MEM_WIDTH, tq), lambda b, i: (b, 0, i)),
                  pl.BlockSpec((1, N_MEM, MEM_WIDTH), lambda b, i: (b, 0, 0)),
                  pl.BlockSpec((1, MEM_HEADS * VT_ROWS, N_MEM), lambda b, i: (b, 0, 0))],
        out_specs=pl.BlockSpec((1, tq, MEM_WIDTH), lambda b, i: (b, i, 0)),
        out_shape=jax.ShapeDtypeStruct((nb, seq, MEM_WIDTH), BF16),
        compiler_params=_cparams(2),
        name="memory_attention",
    )(mqt, mk, mvt)


def _post_attn_kernel(x_ref, att_ref, memo_ref, wa_ref, wm_ref, g_ref, b_ref, wr_ref, br_ref,
                      x1_ref, x1p_ref, idx_ref, rank_ref, gate_ref, cnt_ref, carry_ref):
    t = pl.program_id(0)

    @pl.when(t == 0)
    def _():
        carry_ref[...] = jnp.zeros_like(carry_ref)

    tm = x_ref.shape[0]
    mix = (jnp.dot(att_ref[...], wa_ref[...], preferred_element_type=F32)
           + jnp.dot(memo_ref[...], wm_ref[...], preferred_element_type=F32))
    x1 = _layer_norm(DEEPNORM_ALPHA * x_ref[...] + mix, g_ref[...], b_ref[...])
    x1_ref[...] = x1
    x1b = x1.astype(BF16)
    x1p_ref[...] = _pack_bf16_pairs(x1b)
    logits = jnp.dot(x1b, wr_ref[...], preferred_element_type=F32) + br_ref[...]
    lane = lax.broadcasted_iota(jnp.int32, (tm, LANES), 1)
    work = jnp.where(lane < N_EXPERTS, logits, -jnp.inf)
    idxs, vals = [], []
    onehot = jnp.zeros((tm, LANES), F32)
    for _ in range(TOP_K):
        best = jnp.max(work, axis=1, keepdims=True)
        where_best = jnp.argmax(work, axis=1, keepdims=True).astype(jnp.int32)
        hit = lane == where_best
        onehot = jnp.where(hit, 1.0, onehot)
        work = jnp.where(hit, -jnp.inf, work)
        idxs.append(where_best)
        vals.append(best)
    exps = [jnp.exp(v - vals[0]) for v in vals]
    denom = exps[0] + exps[1] + exps[2] + exps[3]
    row = lax.broadcasted_iota(jnp.int32, (tm, tm), 0)
    col = lax.broadcasted_iota(jnp.int32, (tm, tm), 1)
    strict = jnp.where(row > col, 1.0, 0.0).astype(BF16)
    before = jnp.dot(strict, onehot.astype(BF16), preferred_element_type=F32) + carry_ref[...]
    idx_out = jnp.zeros((tm, LANES), F32)
    rank_out = jnp.zeros((tm, LANES), F32)
    gate_out = jnp.zeros((tm, LANES), F32)
    for r in range(TOP_K):
        rank_r = jnp.sum(jnp.where(lane == idxs[r], before, 0.0), axis=1, keepdims=True)
        idx_out = jnp.where(lane == r, idxs[r].astype(F32), idx_out)
        rank_out = jnp.where(lane == r, rank_r, rank_out)
        gate_out = jnp.where(lane == r, exps[r] / denom, gate_out)
    idx_ref[...] = idx_out.T[:8].astype(jnp.int32)
    rank_ref[...] = rank_out.T[:8].astype(jnp.int32)
    gate_ref[...] = gate_out
    total = carry_ref[...] + jnp.sum(onehot, axis=0, keepdims=True)
    carry_ref[...] = total
    cnt_ref[...] = jnp.broadcast_to(total, cnt_ref.shape)


def _post_attn(x2d, att, memo, wa, wm, g, b, wr, br):
    t_tokens = x2d.shape[0]
    tm = PROJ_TM
    row_spec = lambda width: pl.BlockSpec((tm, width), lambda i: (i, 0))
    full = lambda a: pl.BlockSpec(a.shape, lambda i: (0, 0))
    return pl.pallas_call(
        _post_attn_kernel,
        grid=(t_tokens // tm,),
        in_specs=[row_spec(D_MODEL), row_spec(att.shape[1]), row_spec(MEM_WIDTH),
                  full(wa), full(wm), full(g), full(b), full(wr), full(br)],
        out_specs=[row_spec(D_MODEL), row_spec(D_MODEL // 2),
                   pl.BlockSpec((8, tm), lambda i: (0, i)), pl.BlockSpec((8, tm), lambda i: (0, i)),
                   row_spec(LANES), pl.BlockSpec((8, LANES), lambda i: (0, 0))],
        out_shape=[jax.ShapeDtypeStruct((t_tokens, D_MODEL), F32),
                   jax.ShapeDtypeStruct((t_tokens, D_MODEL // 2), jnp.uint32),
                   jax.ShapeDtypeStruct((8, t_tokens), jnp.int32),
                   jax.ShapeDtypeStruct((8, t_tokens), jnp.int32),
                   jax.ShapeDtypeStruct((t_tokens, LANES), F32),
                   jax.ShapeDtypeStruct((8, LANES), F32)],
        scratch_shapes=[pltpu.VMEM((1, LANES), F32)],
        compiler_params=_cparams(1),
        name="outproj_ln_router",
    )(x2d, att, memo, wa, wm, g, b, wr, br)


def _slot_kernel(pstart_ref, idx_ref, rank_ref, o_ref):
    idx = idx_ref[...]
    dest = rank_ref[...]
    for e in range(N_EXPERTS):
        dest = dest + jnp.where(idx == e, pstart_ref[e], 0)
    o_ref[...] = dest


def _slot_table(idx, rank, pstart):
    return pl.pallas_call(
        _slot_kernel,
        in_specs=[pl.BlockSpec(memory_space=pltpu.SMEM), pl.BlockSpec(memory_space=pltpu.VMEM),
                  pl.BlockSpec(memory_space=pltpu.VMEM)],
        out_specs=pl.BlockSpec(memory_space=pltpu.VMEM),
        out_shape=jax.ShapeDtypeStruct(idx.shape, jnp.int32),
        name="slot_table",
    )(pstart, idx, rank)


def _expert_kernel(be_ref, nu_ref, br_ref, x_ref, wgu_hbm, bgu_ref, wdn_hbm, bdn_ref, *rest,
                   n_chunks, blk_lo, n_blocks, layer, has_prev):
    o_ref, wgu_st, wdn_st, wgu_bf, wdn_bf, h_ref, sem = rest[1:] if has_prev else rest
    step = pl.program_id(0)
    blk = step + blk_lo
    live_end = jnp.minimum(blk_lo + pl.num_programs(0), nu_ref[0])
    expert = be_ref[blk]

    def fetch(e):
        return (pltpu.make_async_copy(wgu_hbm.at[layer, e], wgu_st, sem.at[0]),
                pltpu.make_async_copy(wdn_hbm.at[layer, e], wdn_st, sem.at[1]))

    @pl.when(((step == 0) | (expert != be_ref[jnp.maximum(blk - 1, 0)])) & (blk < live_end))
    def _():
        @pl.when(step == 0)
        def _():
            for copy in fetch(expert):
                copy.start()

        for copy in fetch(expert):
            copy.wait()
        wgu_bf[...] = wgu_st[...].astype(BF16)
        wdn_bf[...] = wdn_st[...].astype(BF16)
        nxt = lax.while_loop(
            lambda j: (j < live_end) & (be_ref[jnp.minimum(j, n_blocks - 1)] == expert),
            lambda j: j + 1, blk + 1)

        @pl.when(nxt < live_end)
        def _():
            for copy in fetch(be_ref[jnp.minimum(nxt, n_blocks - 1)]):
                copy.start()

    def mlp(rows):
        x = _unpack_bf16_pairs(x_ref[:rows]).astype(BF16)
        cw = D_EXPERT // n_chunks
        for c in range(n_chunks):
            gs = slice(c * cw, (c + 1) * cw)
            us = slice(D_EXPERT + c * cw, D_EXPERT + (c + 1) * cw)
            g = jnp.dot(x, wgu_bf[:, gs], preferred_element_type=F32) + bgu_ref[0, 0, :, gs]
            u = jnp.dot(x, wgu_bf[:, us], preferred_element_type=F32) + bgu_ref[0, 0, :, us]
            g = jnp.minimum(g, SWIGLU_LIMIT)
            u = jnp.clip(u, -SWIGLU_LIMIT, SWIGLU_LIMIT)
            h_ref[:rows, gs] = ((u + 1.0) * (g * jax.nn.sigmoid(SWIGLU_ALPHA * g))).astype(BF16)
        y = jnp.dot(h_ref[:rows], wdn_bf[...], preferred_element_type=F32) + bdn_ref[0, 0]
        return _pack_bf16_pairs(y.astype(BF16))

    tm = o_ref.shape[0]
    live_rows = br_ref[blk]

    @pl.when(live_rows > tm // 2)
    def _():
        o_ref[...] = mlp(tm)

    @pl.when((live_rows > 0) & (live_rows <= tm // 2))
    def _():
        o_ref[:tm // 2] = mlp(tm // 2)
        o_ref[tm // 2:] = jnp.zeros((tm - tm // 2, o_ref.shape[1]), o_ref.dtype)

    @pl.when(live_rows == 0)
    def _():
        o_ref[...] = jnp.zeros_like(o_ref)


def _experts(block_expert, n_used, block_rows, xb, blk_lo, p_rows, y_prev, layer, wgu, bgu, wdn, bdn):
    tm = MOE_TM
    has_prev = y_prev is not None
    b_map = lambda i, be, nu, br: (layer, be[i + blk_lo], 0, 0)
    in_specs = [pl.BlockSpec((tm, D_MODEL // 2), lambda i, be, nu, br: (i, 0)),
                pl.BlockSpec(memory_space=pl.ANY),
                pl.BlockSpec((1, 1, 1, 2 * D_EXPERT), b_map),
                pl.BlockSpec(memory_space=pl.ANY),
                pl.BlockSpec((1, 1, 1, D_MODEL), b_map)]
    args = [block_expert, n_used, block_rows, xb, wgu, bgu, wdn, bdn]
    if has_prev:
        in_specs.append(pl.BlockSpec(memory_space=pl.ANY))
        args.append(y_prev)
    grid_spec = pltpu.PrefetchScalarGridSpec(
        num_scalar_prefetch=3,
        grid=(xb.shape[0] // tm,),
        in_specs=in_specs,
        out_specs=pl.BlockSpec((tm, D_MODEL // 2), lambda i, be, nu, br: (i + blk_lo, 0)),
        scratch_shapes=[pltpu.VMEM((D_MODEL, 2 * D_EXPERT), F32),
                        pltpu.VMEM((D_EXPERT, D_MODEL), F32),
                        pltpu.VMEM((D_MODEL, 2 * D_EXPERT), BF16),
                        pltpu.VMEM((D_EXPERT, D_MODEL), BF16),
                        pltpu.VMEM((tm, D_EXPERT), BF16),
                        pltpu.SemaphoreType.DMA((2,))],
    )
    return pl.pallas_call(
        functools.partial(_expert_kernel, n_chunks=4, blk_lo=blk_lo, n_blocks=p_rows // tm,
                          layer=layer, has_prev=has_prev),
        grid_spec=grid_spec,
        out_shape=jax.ShapeDtypeStruct((p_rows, D_MODEL // 2), jnp.uint32),
        input_output_aliases={len(args) - 1: 0} if has_prev else {},
        compiler_params=_cparams(1),
        name="experts",
    )(*args)


def _sc_move_rows(table_hbm, idx_v, out_hbm, base, n_chunks, rows_v, gsem, wsem):
    def gather(c, slot):
        off = pl.multiple_of(c * SC_CHUNK, SC_CHUNK)
        return pltpu.make_async_copy(table_hbm.at[idx_v.at[pl.ds(off, SC_CHUNK)]],
                                     rows_v.at[slot], gsem.at[slot])

    def put(c, slot):
        off = pl.multiple_of(c * SC_CHUNK, SC_CHUNK)
        return pltpu.make_async_copy(rows_v.at[slot], out_hbm.at[pl.ds(base + off, SC_CHUNK)],
                                     wsem.at[slot])

    gather(0, 0).start()
    gather(1, 1).start()

    @pl.loop(0, n_chunks, step=2)
    def _(c):
        for slot in range(2):
            gather(c + slot, slot).wait()
            put(c + slot, slot).start()
        for slot in range(2):
            put(c + slot, slot).wait()

            @pl.when(c + 2 + slot < n_chunks)
            def _():
                gather(c + 2 + slot, slot).start()


def _sc_gather_rows(table, idx):
    n_idx = idx.shape[0]
    width = table.shape[1]
    per_worker = n_idx // SC_WORKERS
    n_chunks = per_worker // SC_CHUNK
    assert per_worker * SC_WORKERS == n_idx and n_chunks * SC_CHUNK == per_worker and n_chunks % 2 == 0
    mesh = plsc.VectorSubcoreMesh(core_axis_name="c", subcore_axis_name="s",
                                  num_cores=SC_CORES, num_subcores=SC_SUBCORES)

    def body(table_hbm, idx_hbm, out_hbm, idx_v, rows_v, gsem, wsem):
        wid = lax.axis_index("s") * SC_CORES + lax.axis_index("c")
        base = wid * per_worker
        pltpu.sync_copy(idx_hbm.at[pl.ds(base, per_worker)], idx_v)
        _sc_move_rows(table_hbm, idx_v, out_hbm, base, n_chunks, rows_v, gsem, wsem)

    return pl.kernel(
        body,
        out_type=jax.ShapeDtypeStruct((n_idx, width), table.dtype),
        mesh=mesh,
        scratch_types=[pltpu.VMEM((per_worker,), jnp.int32),
                       pltpu.VMEM((2, SC_CHUNK, width), table.dtype),
                       pltpu.SemaphoreType.DMA((2,)),
                       pltpu.SemaphoreType.DMA((2,))],
        name="sc_gather_rows",
    )(table, idx)


def _sc_dispatch_rows(table, dest, slot_lo, n_slots):
    n_tok, width = table.shape
    n_pairs = dest.shape[0]
    per_worker = n_slots // SC_WORKERS
    n_chunks = per_worker // SC_CHUNK
    assert per_worker * SC_WORKERS == n_slots and n_chunks * SC_CHUNK == per_worker and n_chunks % 2 == 0
    n_scan = n_pairs // SC_SCAN
    assert n_scan * SC_SCAN == n_pairs
    mesh = plsc.VectorSubcoreMesh(core_axis_name="c", subcore_axis_name="s",
                                  num_cores=SC_CORES, num_subcores=SC_SUBCORES)

    def body(table_hbm, dest_hbm, out_hbm, tok_v, dest_v, rows_v, gsem, wsem):
        wid = lax.axis_index("s") * SC_CORES + lax.axis_index("c")
        base = wid * per_worker
        first = slot_lo + base
        lane = lax.iota(jnp.int32, SC_LANES)

        @pl.loop(0, per_worker // SC_LANES)
        def _(i):
            off = pl.multiple_of(i * SC_LANES, SC_LANES)
            tok_v[pl.ds(off, SC_LANES)] = lax.rem(first + off + lane, n_tok)

        @pl.loop(0, n_scan)
        def _(g):
            goff = pl.multiple_of(g * SC_SCAN, SC_SCAN)
            pltpu.sync_copy(dest_hbm.at[pl.ds(goff, SC_SCAN)], dest_v)

            @plsc.parallel_loop(0, SC_SCAN // SC_LANES, unroll=8)
            def _(i):
                off = pl.multiple_of(i * SC_LANES, SC_LANES)
                local = dest_v[pl.ds(off, SC_LANES)] - first
                mine = (local >= 0) & (local < per_worker)
                pair = goff + off + lane
                plsc.store_scatter(tok_v, [jnp.where(mine, local, 0)], lax.rem(pair, n_tok), mask=mine)

        _sc_move_rows(table_hbm, tok_v, out_hbm, base, n_chunks, rows_v, gsem, wsem)

    return pl.kernel(
        body,
        out_type=jax.ShapeDtypeStruct((n_slots, width), table.dtype),
        mesh=mesh,
        scratch_types=[pltpu.VMEM((per_worker,), jnp.int32),
                       pltpu.VMEM((SC_SCAN,), jnp.int32),
                       pltpu.VMEM((2, SC_CHUNK, width), table.dtype),
                       pltpu.SemaphoreType.DMA((2,)),
                       pltpu.SemaphoreType.DMA((2,))],
        compiler_params=pltpu.CompilerParams(needs_layout_passes=False),
        name="sc_dispatch_rows",
    )(table, dest)


def _combine_kernel(x1_ref, yg_ref, gate_ref, g_ref, b_ref, *rest):
    o_ref = rest[-1]
    gates = gate_ref[...]
    ffn = _unpack_bf16_pairs(yg_ref[0]) * gates[:, 0:1]
    for r in range(1, TOP_K):
        ffn = ffn + _unpack_bf16_pairs(yg_ref[r]) * gates[:, r:r + 1]
    o_ref[...] = _layer_norm(DEEPNORM_ALPHA * x1_ref[...] + ffn, g_ref[...], b_ref[...])


def _combine(x1, yg, gates, g, b, tok_lo, out_prev):
    t_tokens = x1.shape[0]
    tm = COMBINE_TM
    tile_lo = tok_lo // tm
    in_specs = [pl.BlockSpec((tm, D_MODEL), lambda i: (i + tile_lo, 0)),
                pl.BlockSpec((TOP_K, tm, D_MODEL // 2), lambda i: (0, i, 0)),
                pl.BlockSpec((tm, LANES), lambda i: (i + tile_lo, 0)),
                pl.BlockSpec((1, D_MODEL), lambda i: (0, 0)),
                pl.BlockSpec((1, D_MODEL), lambda i: (0, 0))]
    args = [x1, yg, gates, g, b]
    if out_prev is not None:
        in_specs.append(pl.BlockSpec(memory_space=pl.ANY))
        args.append(out_prev)
    return pl.pallas_call(
        _combine_kernel,
        grid=(yg.shape[1] // tm,),
        in_specs=in_specs,
        out_specs=pl.BlockSpec((tm, D_MODEL), lambda i: (i + tile_lo, 0)),
        out_shape=jax.ShapeDtypeStruct((t_tokens, D_MODEL), F32),
        input_output_aliases={len(args) - 1: 0} if out_prev is not None else {},
        compiler_params=_cparams(1),
        name="combine_ln",
    )(*args)


def _moe(x1, x1p, idx, rank, gates, counts, layer, wgu, bgu, wdn, bdn, g, b):
    t_tokens = x1.shape[0]
    tm = MOE_TM
    counts = counts[0, :N_EXPERTS].astype(jnp.int32)
    padded = ((counts + tm - 1) // tm) * tm
    pend = jnp.cumsum(padded)
    pstart = pend - padded
    dest = _slot_table(idx, rank, pstart)[:TOP_K]
    n_blocks = (t_tokens * TOP_K) // tm + N_EXPERTS
    p_rows = n_blocks * tm
    block_start = jnp.arange(n_blocks, dtype=jnp.int32) * tm
    block_expert = jnp.minimum(
        jnp.sum((pend[None, :] <= block_start[:, None]).astype(jnp.int32), axis=1), N_EXPERTS - 1)
    n_used = (pend[-1:] // tm).astype(jnp.int32)
    mine = block_expert[:, None] == jnp.arange(N_EXPERTS, dtype=jnp.int32)[None, :]
    live_end = jnp.sum(jnp.where(mine, (pstart + counts)[None, :], 0), axis=1)
    block_rows = jnp.clip(live_end - block_start, 0, tm)
    y = None
    for lo, hi in ((0, n_blocks // MOE_SPLIT), (n_blocks // MOE_SPLIT, n_blocks)):
        xb = _sc_dispatch_rows(x1p, dest.reshape(-1), lo * tm, (hi - lo) * tm)
        y = _experts(block_expert, n_used, block_rows, xb, lo, p_rows, y, layer, wgu, bgu, wdn, bdn)
    out = None
    for lo, hi in ((0, t_tokens // MOE_SPLIT), (t_tokens // MOE_SPLIT, t_tokens)):
        yg = _sc_gather_rows(y, dest[:, lo:hi].reshape(-1)).reshape(TOP_K, hi - lo, D_MODEL // 2)
        out = _combine(x1, yg, gates, g, b, lo, out)
    return out


def _row(v, width=None):
    v = v.astype(F32).reshape(1, -1)
    if width is not None and v.shape[1] < width:
        v = jnp.pad(v, ((0, 0), (0, width - v.shape[1])))
    return v


def _pad_cols(w, width):
    return jnp.pad(w, ((0, 0), (0, width - w.shape[1])))


def kernel(x, mem, positions, a_w_in, a_b_f, a_w_out, b_w_in, b_g_q, b_w_uq, b_w_out,
           kv_w_dkv, kv_g, kv_w_ukv, mem_w_kv, ln_g, ln_b,
           moe_w_r, moe_b_r, moe_w_gu, moe_b_gu, moe_w_dn, moe_b_dn):
    nb, seq, d = x.shape
    t_tokens = nb * seq
    n_a = a_w_in.shape[0]
    x2d = x.reshape(t_tokens, d)
    mem2d = mem.reshape(nb * N_MEM, d)
    pos2d = positions.reshape(1, t_tokens)
    half = QK_ROPE // 2
    invf = (ROPE_THETA ** (-jnp.arange(half, dtype=F32) * 2.0 / QK_ROPE)).reshape(half, 1)

    shared_kv = None
    for l in range(DEPTH):
        mk, mvt = _mem_proj(mem2d, mem_w_kv[l].astype(BF16), nb)
        if l < n_a:
            w_in = a_w_in[l]
            w = jnp.concatenate([w_in[:, :3 * FOX_WIDTH],
                                 _pad_cols(w_in[:, 3 * FOX_WIDTH:3 * FOX_WIDTH + FOX_HEADS], LANES),
                                 w_in[:, 3 * FOX_WIDTH + FOX_HEADS:]], axis=1).astype(BF16)
            qt, k, vt, mqt, cum, cumt, cfirst, clast, knorm = _fox_proj(
                x2d, w, _row(a_b_f[l], LANES), seq)
            stats = [s.reshape(-1, LANES) for s in (cfirst, clast, knorm)]
            att = _causal_attention(qt, k.reshape(nb, seq, -1), vt,
                                    [cum.reshape(nb, seq, LANES), cumt] + stats, slab=False)
            w_out = a_w_out[l]
        else:
            bl = l - n_a
            w = jnp.concatenate([b_w_in[bl], kv_w_dkv[:, :KV_LORA],
                                 jnp.zeros((d, QK_NOPE), F32), kv_w_dkv[:, KV_LORA:],
                                 jnp.zeros((d, LANES - QK_NOPE - QK_ROPE), F32)], axis=1).astype(BF16)
            wuq = b_w_uq[bl].reshape(Q_LORA, MLA_HEADS, QK_NOPE + QK_ROPE)
            wuq = jnp.pad(wuq, ((0, 0), (0, 0), (0, MLA_SLAB - QK_NOPE - QK_ROPE)))
            wuq = wuq.reshape(Q_LORA, MLA_HEADS * MLA_SLAB).astype(BF16)
            wukv = kv_w_ukv.reshape(KV_LORA, MLA_HEADS, QK_NOPE + V_DIM)
            wk = jnp.pad(wukv[:, :, :QK_NOPE], ((0, 0), (0, 0), (0, MLA_SLAB - QK_NOPE)))
            wk = wk.reshape(KV_LORA, MLA_HEADS * MLA_SLAB).astype(BF16)
            wv = wukv[:, :, QK_NOPE:].reshape(KV_LORA, MLA_V_WIDTH).astype(BF16)
            qt, mqt, k_new, vt_new = _mla_proj(x2d, pos2d, invf, w, _row(b_g_q[bl]), wuq,
                                              _row(kv_g), wk, wv, seq)
            if shared_kv is None:
                shared_kv = (k_new.reshape(nb, seq, -1), vt_new)
            att = _causal_attention(qt, shared_kv[0], shared_kv[1], None, slab=True)
            w_out = b_w_out[bl]
        memo = _memory_attention(mqt, mk, mvt)
        n_att = w_out.shape[0] - MEM_WIDTH
        x1, x1p, idx, rank, gates, counts = _post_attn(
            x2d, att.reshape(t_tokens, -1), memo.reshape(t_tokens, MEM_WIDTH),
            w_out[:n_att].astype(BF16), w_out[n_att:].astype(BF16),
            _row(ln_g[l, 0]), _row(ln_b[l, 0]),
            _pad_cols(moe_w_r[l], LANES).astype(BF16), _row(moe_b_r[l], LANES))
        x2d = _moe(x1, x1p, idx, rank, gates, counts, l,
                   moe_w_gu, moe_b_gu.reshape(DEPTH, N_EXPERTS, 1, -1),
                   moe_w_dn, moe_b_dn.reshape(DEPTH, N_EXPERTS, 1, -1),
                   _row(ln_g[l, 1]), _row(ln_b[l, 1]))
    return x2d.reshape(nb, seq, d)
```

```python
import functools
import math

import jax
import jax.numpy as jnp
from jax import lax
from jax.experimental import pallas as pl
from jax.experimental.pallas import tpu as pltpu
from jax.experimental.pallas import tpu_sc as plsc

F32 = jnp.float32
BF16 = jnp.bfloat16

D_MODEL = 1024
DEPTH = 2
N_MEM = 256
HEAD_DIM = 64
FOX_HEADS = 12
MEM_HEADS = 4
MLA_HEADS = 12
Q_LORA = 384
KV_LORA = 256
QK_NOPE = 64
QK_ROPE = 32
V_DIM = 64
ROPE_THETA = 10000.0
N_EXPERTS = 32
TOP_K = 4
D_EXPERT = D_MODEL
SWIGLU_LIMIT = 7.0
SWIGLU_ALPHA = 1.702
LN_EPS = 1e-5
RMS_EPS = 1e-6
NEG_INF = -1e30
DEEPNORM_ALPHA = (2 * DEPTH) ** 0.25
FOX_WIDTH = FOX_HEADS * HEAD_DIM
MEM_WIDTH = MEM_HEADS * HEAD_DIM
MLA_V_WIDTH = MLA_HEADS * V_DIM

LANES = 128
LOG2E = math.log2(math.e)
VMEM_LIMIT = 48 * 1024 * 1024

PROJ_TM = 512
ATT_TQ = 512
FOX_TK = 256
MOE_TM = 512
COMBINE_TM = 512
MOE_SPLIT = 4
EXPERT_SPLIT = 8
MLA_SLAB = LANES
VT_ROWS = HEAD_DIM + 16
SC_CORES = 2
SC_SUBCORES = 16
SC_WORKERS = SC_CORES * SC_SUBCORES
SC_CHUNK = 32
SC_LANES = 16
SC_SCAN = 8192


def _cparams(n_axes):
    return pltpu.CompilerParams(dimension_semantics=("arbitrary",) * n_axes,
                                vmem_limit_bytes=VMEM_LIMIT)


def _split3(x):
    hi = x.astype(BF16)
    r1 = x - hi.astype(F32)
    mid = r1.astype(BF16)
    lo = (r1 - mid.astype(F32)).astype(BF16)
    return hi, mid, lo


def _pack_bf16_pairs(v):
    bits = pltpu.bitcast(v.astype(F32), jnp.uint32)
    half = v.shape[1] // 2
    return (bits[:, :half] >> 16) | bits[:, half:]


def _unpack_bf16_pairs(words):
    return jnp.concatenate([pltpu.bitcast(words << 16, F32),
                            pltpu.bitcast(words & jnp.uint32(0xFFFF0000), F32)], axis=1)


def _layer_norm(y, g, b):
    mu = jnp.mean(y, axis=-1, keepdims=True)
    yc = y - mu
    var = jnp.mean(yc * yc, axis=-1, keepdims=True)
    return yc * lax.rsqrt(var + LN_EPS) * g + b


def _rms_norm(y, g):
    return y * lax.rsqrt(jnp.mean(y * y, axis=-1, keepdims=True) + RMS_EPS) * g


def _store_transposed(dst_ref, val):
    for s in range(val.shape[1] // LANES):
        sl = slice(s * LANES, (s + 1) * LANES)
        dst_ref[0, sl, :] = val[:, sl].T.astype(dst_ref.dtype)


def _store_values_transposed(vt_ref, val):
    tm = val.shape[0]
    ones = jnp.ones((VT_ROWS - HEAD_DIM, tm), vt_ref.dtype)
    for s in range(val.shape[1] // LANES):
        pair_t = val[:, s * LANES:(s + 1) * LANES].T.astype(vt_ref.dtype)
        for h in range(2):
            r0 = (2 * s + h) * VT_ROWS
            vt_ref[0, r0:r0 + HEAD_DIM, :] = pair_t[h * HEAD_DIM:(h + 1) * HEAD_DIM]
            vt_ref[0, r0 + HEAD_DIM:r0 + VT_ROWS, :] = ones


def _fox_proj_kernel(x_ref, w_ref, bf_ref, seg_ref, qt_ref, k_ref, vt_ref, mqt_ref, cum_ref,
                     cumt_ref, cfirst_ref, clast_ref, knorm_ref, carry_ref, knmax_ref, *,
                     tiles_per_batch):
    t = pl.program_id(0)

    @pl.when(t % tiles_per_batch == 0)
    def _():
        carry_ref[...] = jnp.zeros_like(carry_ref)
        knmax_ref[...] = jnp.zeros_like(knmax_ref)

    tm = x_ref.shape[0]
    proj = jnp.dot(x_ref[...].astype(BF16), w_ref[...], preferred_element_type=F32)
    qscale = HEAD_DIM ** -0.5 * LOG2E
    _store_transposed(qt_ref, proj[:, :FOX_WIDTH] * qscale)
    kb = proj[:, FOX_WIDTH:2 * FOX_WIDTH].astype(BF16)
    k_ref[...] = kb
    _store_values_transposed(vt_ref, proj[:, 2 * FOX_WIDTH:3 * FOX_WIDTH])
    kf = kb.astype(F32)
    ksq = jnp.dot((kf * kf).astype(BF16), seg_ref[...], preferred_element_type=F32)
    n_sub = tm // FOX_TK
    for sub in range(n_sub):
        tile_max = jnp.sqrt(jnp.max(ksq[sub * FOX_TK:(sub + 1) * FOX_TK], axis=0, keepdims=True))
        knmax_ref[...] = jnp.maximum(knmax_ref[...], tile_max)
        knorm_ref[sub] = knmax_ref[...]
    f = proj[:, 3 * FOX_WIDTH:3 * FOX_WIDTH + LANES] + bf_ref[...]
    _store_transposed(mqt_ref, proj[:, 3 * FOX_WIDTH + LANES:] * qscale)
    log_f = jnp.minimum(f, 0.0) - jnp.log1p(jnp.exp(-jnp.abs(f)))
    row = lax.broadcasted_iota(jnp.int32, (tm, tm), 0)
    col = lax.broadcasted_iota(jnp.int32, (tm, tm), 1)
    tri = jnp.where(row >= col, 1.0, 0.0).astype(BF16)
    hi, mid, lo = _split3(log_f)
    cum = (jnp.dot(tri, hi, preferred_element_type=F32)
           + jnp.dot(tri, mid, preferred_element_type=F32)
           + jnp.dot(tri, lo, preferred_element_type=F32)) + carry_ref[...]
    carry_ref[...] = cum[tm - 1:tm, :]
    cum2 = cum * LOG2E
    cum_ref[...] = cum2
    cumt_ref[0] = cum2.T[:16, :]
    for sub in range(n_sub):
        cfirst_ref[sub] = cum2[sub * FOX_TK:sub * FOX_TK + 1, :]
        clast_ref[sub] = cum2[(sub + 1) * FOX_TK - 1:(sub + 1) * FOX_TK, :]


def _fox_proj(x2d, w, bf, seq):
    t_tokens = x2d.shape[0]
    tm = PROJ_TM
    nb = t_tokens // seq
    n = w.shape[1]
    tiles_per_batch = seq // tm
    n_tiles = t_tokens // tm
    seg = (jnp.arange(FOX_WIDTH)[:, None] // HEAD_DIM == jnp.arange(LANES)[None, :]).astype(BF16)
    row_spec = lambda width: pl.BlockSpec((tm, width), lambda i: (i, 0))
    t_spec = lambda rows: pl.BlockSpec(
        (1, rows, tm), lambda i: (i // tiles_per_batch, 0, i % tiles_per_batch))
    n_sub = tm // FOX_TK
    stat_spec = pl.BlockSpec((n_sub, 1, LANES), lambda i: (i, 0, 0))
    stat_shape = jax.ShapeDtypeStruct((n_tiles * n_sub, 1, LANES), F32)
    return pl.pallas_call(
        functools.partial(_fox_proj_kernel, tiles_per_batch=tiles_per_batch),
        grid=(n_tiles,),
        in_specs=[row_spec(D_MODEL),
                  pl.BlockSpec((D_MODEL, n), lambda i: (0, 0)),
                  pl.BlockSpec((1, LANES), lambda i: (0, 0)),
                  pl.BlockSpec((FOX_WIDTH, LANES), lambda i: (0, 0))],
        out_specs=[t_spec(FOX_WIDTH), row_spec(FOX_WIDTH), t_spec(FOX_HEADS * VT_ROWS),
                   t_spec(MEM_WIDTH), row_spec(LANES), t_spec(16),
                   stat_spec, stat_spec, stat_spec],
        out_shape=[jax.ShapeDtypeStruct((nb, FOX_WIDTH, seq), BF16),
                   jax.ShapeDtypeStruct((t_tokens, FOX_WIDTH), BF16),
                   jax.ShapeDtypeStruct((nb, FOX_HEADS * VT_ROWS, seq), BF16),
                   jax.ShapeDtypeStruct((nb, MEM_WIDTH, seq), BF16),
                   jax.ShapeDtypeStruct((t_tokens, LANES), F32),
                   jax.ShapeDtypeStruct((nb, 16, seq), F32),
                   stat_shape, stat_shape, stat_shape],
        scratch_shapes=[pltpu.VMEM((1, LANES), F32), pltpu.VMEM((1, LANES), F32)],
        compiler_params=_cparams(1),
        name="fox_proj",
    )(x2d, w, bf, seg)


def _rope_slab_t(slab_t, cos_t, sin_t):
    half = QK_ROPE // 2
    x1 = slab_t[QK_NOPE:QK_NOPE + half]
    x2 = slab_t[QK_NOPE + half:QK_NOPE + QK_ROPE]
    return jnp.concatenate([slab_t[:QK_NOPE], x1 * cos_t - x2 * sin_t, x1 * sin_t + x2 * cos_t,
                            slab_t[QK_NOPE + QK_ROPE:]], axis=0)


def _mla_proj_kernel(x_ref, pos_ref, invf_ref, w_ref, gq_ref, wuq_ref, gkv_ref, wk_ref, wv_ref,
                     qt_ref, mqt_ref, k_ref, vt_ref):
    proj = jnp.dot(x_ref[...].astype(BF16), w_ref[...], preferred_element_type=F32)
    ang_t = invf_ref[...] * pos_ref[...].astype(F32)
    cos_t = jnp.cos(ang_t)
    sin_t = jnp.sin(ang_t)
    c_q = _rms_norm(proj[:, :Q_LORA], gq_ref[...])
    _store_transposed(mqt_ref, proj[:, Q_LORA:Q_LORA + MEM_WIDTH] * (HEAD_DIM ** -0.5 * LOG2E))
    kv_off = Q_LORA + MEM_WIDTH
    c_kv = _rms_norm(proj[:, kv_off:kv_off + KV_LORA], gkv_ref[...])
    kr = _rope_slab_t(proj[:, kv_off + KV_LORA:].T, cos_t, sin_t).T
    q = jnp.dot(c_q.astype(BF16), wuq_ref[...], preferred_element_type=F32)
    kn = jnp.dot(c_kv.astype(BF16), wk_ref[...], preferred_element_type=F32)
    qscale = (QK_NOPE + QK_ROPE) ** -0.5 * LOG2E
    for h in range(MLA_HEADS):
        sl = slice(h * MLA_SLAB, (h + 1) * MLA_SLAB)
        qt_ref[0, sl, :] = (_rope_slab_t(q[:, sl].T, cos_t, sin_t) * qscale).astype(BF16)
        k_ref[:, sl] = (kn[:, sl] + kr).astype(BF16)
    _store_values_transposed(vt_ref, jnp.dot(c_kv.astype(BF16), wv_ref[...],
                                             preferred_element_type=F32))


def _mla_proj(x2d, pos2d, invf, w, gq, wuq, gkv, wk, wv, seq):
    t_tokens = x2d.shape[0]
    tm = PROJ_TM
    nb = t_tokens // seq
    tiles_per_batch = seq // tm
    row_spec = lambda width: pl.BlockSpec((tm, width), lambda i: (i, 0))
    t_spec = lambda rows: pl.BlockSpec(
        (1, rows, tm), lambda i: (i // tiles_per_batch, 0, i % tiles_per_batch))
    full = lambda a: pl.BlockSpec(a.shape, lambda i: (0, 0))
    slabs = MLA_HEADS * MLA_SLAB
    return pl.pallas_call(
        _mla_proj_kernel,
        grid=(t_tokens // tm,),
        in_specs=[row_spec(D_MODEL), pl.BlockSpec((1, tm), lambda i: (0, i)), full(invf), full(w),
                  full(gq), full(wuq), full(gkv), full(wk), full(wv)],
        out_specs=[t_spec(slabs), t_spec(MEM_WIDTH), row_spec(slabs),
                   t_spec(MLA_HEADS * VT_ROWS)],
        out_shape=[jax.ShapeDtypeStruct((nb, slabs, seq), BF16),
                   jax.ShapeDtypeStruct((nb, MEM_WIDTH, seq), BF16),
                   jax.ShapeDtypeStruct((t_tokens, slabs), BF16),
                   jax.ShapeDtypeStruct((nb, MLA_HEADS * VT_ROWS, seq), BF16)],
        compiler_params=_cparams(1),
        name="mla_proj",
    )(x2d, pos2d, invf, w, gq, wuq, gkv, wk, wv)


SKIP_LOG2 = -160.0
NORM_SLACK = 1.02


def _causal_attn_kernel(*refs, tq, tk, n_chunks, fox, slab, long_body):
    if fox:
        (cf_ref, cl_ref, kn_ref, qt_ref, k_ref, vt_ref, cq_ref, ck_ref,
         o_ref, m_ref, acc_ref, s_ref, cmax_ref) = refs
    else:
        qt_ref, k_ref, vt_ref, o_ref, m_ref, acc_ref, s_ref, cmax_ref = refs
    b = pl.program_id(0)
    pair = pl.program_id(1)
    i = pl.program_id(2)
    m_ref[...] = jnp.full(m_ref.shape, NEG_INF, F32)
    acc_ref[...] = jnp.zeros(acc_ref.shape, F32)
    qt = qt_ref[0]
    if slab:
        qth = [qt[:MLA_SLAB], qt[MLA_SLAB:]]
    else:
        rowi = lax.broadcasted_iota(jnp.int32, qt.shape, 0)
        zero = jnp.zeros_like(qt)
        qth = [jnp.where(rowi < HEAD_DIM, qt, zero), jnp.where(rowi >= HEAD_DIM, qt, zero)]
    if fox:
        lane = lax.broadcasted_iota(jnp.int32, (1, LANES), 1)
        cq = [cq_ref[0, pl.ds(2 * pair + h, 1), :] for h in range(2)]

    def logits_to(slot, j):
        off = pl.multiple_of(j * tk, tk)
        kc = k_ref[0, pl.ds(off, tk), :]
        for h in range(2):
            kh = kc[:, h * MLA_SLAB:(h + 1) * MLA_SLAB] if slab else kc
            s = jnp.dot(kh, qth[h], preferred_element_type=F32)
            if fox:
                ck_blk = ck_ref[0, pl.ds(off, tk), :]
                ck = jnp.sum(jnp.where(lane == 2 * pair + h, ck_blk, 0.0), axis=1, keepdims=True)
                s = s + cq[h] - ck
            s_ref[slot, h] = s
            cmax_ref[slot, h] = jnp.max(s, axis=0, keepdims=True)

    def softmax_pv(slot, j, causal_mask):
        off = pl.multiple_of(j * tk, tk)
        for h in range(2):
            s = s_ref[slot, h]
            if causal_mask is not None:
                s = jnp.where(causal_mask, s, NEG_INF)
                chunk_max = jnp.max(s, axis=0, keepdims=True)
            else:
                chunk_max = cmax_ref[slot, h]
            m_prev = m_ref[h]
            m_new = jnp.maximum(m_prev, chunk_max)
            alpha = jnp.exp2(m_prev - m_new)
            p = jnp.exp2(s - m_new).astype(BF16)
            vth = vt_ref[0, h * VT_ROWS:(h + 1) * VT_ROWS, pl.ds(off, tk)]
            acc_ref[h] = acc_ref[h] * alpha + jnp.dot(vth, p, preferred_element_type=F32)
            m_ref[h] = m_new

    per_tile = tq // tk
    d0 = per_tile * i
    if fox:
        base = b * n_chunks
        qn = []
        for h in range(2):
            qf = qth[h].astype(F32)
            qn.append(jnp.sqrt(jnp.max(jnp.sum(qf * qf, axis=0, keepdims=True))) * NORM_SLACK)

        top = [2.0 * qn[h] * kn_ref[base + d0 + per_tile - 1, 2 * pair + h]
               + cf_ref[base + d0, 2 * pair + h] for h in range(2)]

        def live(j):
            ub = [top[h] - cl_ref[base + jnp.maximum(j, 0), 2 * pair + h] for h in range(2)]
            return (j >= 0) & (jnp.maximum(ub[0], ub[1]) > SKIP_LOG2)

        j0 = lax.while_loop(live, lambda j: j - 1, d0 - 1) + 1
    else:
        j0 = 0
    n_before = d0 - j0

    def causal_mask(shift):
        key = lax.broadcasted_iota(jnp.int32, (tk, tq), 0)
        qry = lax.broadcasted_iota(jnp.int32, (tk, tq), 1)
        return key + shift <= qry

    def finish():
        if per_tile == 2:
            logits_to(1, d0 + 1)
        softmax_pv(0, d0, causal_mask(0))
        if per_tile == 2:
            softmax_pv(1, d0 + 1, causal_mask(tk))

    def pipelined():
        odd = (n_before % 2) == 1

        @pl.when(odd)
        def _():
            logits_to(1, j0)
            logits_to(0, j0 + 1)
            softmax_pv(1, j0, None)

        @pl.when(jnp.logical_not(odd))
        def _():
            logits_to(0, j0)

        def two_steps(j):
            logits_to(1, j + 1)
            softmax_pv(0, j, None)
            logits_to(0, j + 2)
            softmax_pv(1, j + 1, None)

        j_even = j0 + (n_before % 2)
        n_two = n_before // 2
        if long_body:
            @pl.when((n_two % 2) == 1)
            def _():
                two_steps(j_even)

            j_quad = j_even + 2 * (n_two % 2)

            def body(t, carry):
                two_steps(j_quad + 4 * t)
                two_steps(j_quad + 4 * t + 2)
                return carry

            lax.fori_loop(0, n_two // 2, body, 0)
        else:
            def body(t, carry):
                two_steps(j_even + 2 * t)
                return carry

            lax.fori_loop(0, n_two, body, 0)
        finish()

    if fox:
        @pl.when(n_before == 1)
        def _():
            logits_to(1, j0)
            logits_to(0, d0)
            softmax_pv(1, j0, None)
            finish()

        @pl.when(n_before != 1)
        def _():
            pipelined()
    else:
        pipelined()
    out_t = jnp.concatenate(
        [acc_ref[h, :HEAD_DIM] * (1.0 / acc_ref[h, HEAD_DIM:HEAD_DIM + 1]) for h in range(2)],
        axis=0)
    o_ref[0] = out_t.T.astype(o_ref.dtype)


def _causal_attention(qt, k, vt, fox_args, *, slab):
    nb, seq, _ = k.shape
    fox = fox_args is not None
    tq = ATT_TQ
    tk = FOX_TK if fox else ATT_TQ
    n_tiles = seq // tq
    n_pairs = FOX_HEADS // 2
    rows = 2 * MLA_SLAB if slab else LANES
    in_specs = [pl.BlockSpec((1, rows, tq), lambda b, p, i: (b, p, i)),
                pl.BlockSpec((1, seq, rows), lambda b, p, i: (b, 0, p)),
                pl.BlockSpec((1, 2 * VT_ROWS, seq), lambda b, p, i: (b, p, 0))]
    args = [qt, k, vt]
    if fox:
        cum, cumt, cfirst, clast, knorm = fox_args
        smem = pl.BlockSpec(memory_space=pltpu.SMEM)
        in_specs = [smem, smem, smem] + in_specs + [
            pl.BlockSpec((1, 16, tq), lambda b, p, i: (b, 0, i)),
            pl.BlockSpec((1, seq, LANES), lambda b, p, i: (b, 0, 0))]
        args = [cfirst, clast, knorm] + args + [cumt, cum]
    return pl.pallas_call(
        functools.partial(_causal_attn_kernel, tq=tq, tk=tk, n_chunks=seq // tk, fox=fox, slab=slab,
                          long_body=not fox),
        grid=(nb, n_pairs, n_tiles),
        in_specs=in_specs,
        out_specs=pl.BlockSpec((1, tq, LANES), lambda b, p, i: (b, i, p)),
        out_shape=jax.ShapeDtypeStruct((nb, seq, n_pairs * LANES), BF16),
        scratch_shapes=[pltpu.VMEM((2, 1, tq), F32), pltpu.VMEM((2, VT_ROWS, tq), F32),
                        pltpu.VMEM((2, 2, tk, tq), F32), pltpu.VMEM((2, 2, 1, tq), F32)],
        compiler_params=_cparams(3),
        name="fox_attention" if fox else "mla_attention",
    )(*args)


def _mem_proj_kernel(mem_ref, w_ref, mk_ref, mvt_ref):
    mkv = jnp.dot(mem_ref[...].astype(BF16), w_ref[...], preferred_element_type=F32)
    mk_ref[0] = mkv[:, :MEM_WIDTH].astype(BF16)
    _store_values_transposed(mvt_ref, mkv[:, MEM_WIDTH:])


def _mem_proj(mem2d, w, nb):
    return pl.pallas_call(
        _mem_proj_kernel,
        grid=(nb,),
        in_specs=[pl.BlockSpec((N_MEM, D_MODEL), lambda b: (b, 0)),
                  pl.BlockSpec(w.shape, lambda b: (0, 0))],
        out_specs=[pl.BlockSpec((1, N_MEM, MEM_WIDTH), lambda b: (b, 0, 0)),
                   pl.BlockSpec((1, MEM_HEADS * VT_ROWS, N_MEM), lambda b: (b, 0, 0))],
        out_shape=[jax.ShapeDtypeStruct((nb, N_MEM, MEM_WIDTH), BF16),
                   jax.ShapeDtypeStruct((nb, MEM_HEADS * VT_ROWS, N_MEM), BF16)],
        compiler_params=_cparams(1),
        name="mem_proj",
    )(mem2d, w)


def _mem_attn_kernel(qt_ref, k_ref, vt_ref, o_ref):
    qt = qt_ref[0]
    kc = k_ref[0]
    for pair in range(MEM_HEADS // 2):
        qt_p = qt[pair * LANES:(pair + 1) * LANES]
        k_p = kc[:, pair * LANES:(pair + 1) * LANES]
        rowi = lax.broadcasted_iota(jnp.int32, qt_p.shape, 0)
        zero = jnp.zeros_like(qt_p)
        outs = []
        for h in range(2):
            mine = (rowi < HEAD_DIM) if h == 0 else (rowi >= HEAD_DIM)
            s = jnp.dot(k_p, jnp.where(mine, qt_p, zero), preferred_element_type=F32)
            p = jnp.exp2(s - jnp.max(s, axis=0, keepdims=True)).astype(BF16)
            r0 = (2 * pair + h) * VT_ROWS
            acc = jnp.dot(vt_ref[0, r0:r0 + VT_ROWS, :], p, preferred_element_type=F32)
            outs.append(acc[:HEAD_DIM] * (1.0 / acc[HEAD_DIM:HEAD_DIM + 1]))
        o_ref[0, :, pair * LANES:(pair + 1) * LANES] = (
            jnp.concatenate(outs, axis=0).T.astype(o_ref.dtype))


def _memory_attention(mqt, mk, mvt):
    nb, _, seq = mqt.shape
    tq = ATT_TQ
    return pl.pallas_call(
        _mem_attn_kernel,
        grid=(nb, seq // tq),
        in_specs=[pl.BlockSpec((1, MEM_WIDTH, tq), lambda b, i: (b, 0, i)),
                  pl.BlockSpec((1, N_MEM, MEM_WIDTH), lambda b, i: (b, 0, 0)),
                  pl.BlockSpec((1, MEM_HEADS * VT_ROWS, N_MEM), lambda b, i: (b, 0, 0))],
        out_specs=pl.BlockSpec((1, tq, MEM_WIDTH), lambda b, i: (b, i, 0)),
        out_shape=jax.ShapeDtypeStruct((nb, seq, MEM_WIDTH), BF16),
        compiler_params=_cparams(2),
        name="memory_attention",
    )(mqt, mk, mvt)


def _post_attn_kernel(x_ref, att_ref, memo_ref, wa_ref, wm_ref, g_ref, b_ref, wr_ref, br_ref,
                      x1_ref, x1p_ref, idx_ref, rank_ref, gate_ref, cnt_ref, carry_ref):
    t = pl.program_id(0)

    @pl.when(t == 0)
    def _():
        carry_ref[...] = jnp.zeros_like(carry_ref)

    tm = x_ref.shape[0]
    mix = (jnp.dot(att_ref[...], wa_ref[...], preferred_element_type=F32)
           + jnp.dot(memo_ref[...], wm_ref[...], preferred_element_type=F32))
    x1 = _layer_norm(DEEPNORM_ALPHA * x_ref[...] + mix, g_ref[...], b_ref[...])
    x1_ref[...] = x1
    x1b = x1.astype(BF16)
    x1p_ref[...] = _pack_bf16_pairs(x1b)
    logits = jnp.dot(x1b, wr_ref[...], preferred_element_type=F32) + br_ref[...]
    lane = lax.broadcasted_iota(jnp.int32, (tm, LANES), 1)
    work = jnp.where(lane < N_EXPERTS, logits, -jnp.inf)
    lane_f = lane.astype(F32)
    idxs, vals = [], []
    onehot = jnp.zeros((tm, LANES), F32)
    for _ in range(TOP_K):
        best = jnp.max(work, axis=1, keepdims=True)
        where_best = jnp.min(jnp.where(work == best, lane_f, float(LANES)), axis=1, keepdims=True)
        hit = lane_f == where_best
        onehot = jnp.where(hit, 1.0, onehot)
        work = jnp.where(hit, -jnp.inf, work)
        idxs.append(where_best)
        vals.append(best)
    exps = [jnp.exp(v - vals[0]) for v in vals]
    denom = exps[0] + exps[1] + exps[2] + exps[3]
    row = lax.broadcasted_iota(jnp.int32, (tm, tm), 0)
    col = lax.broadcasted_iota(jnp.int32, (tm, tm), 1)
    strict = jnp.where(row > col, 1.0, 0.0).astype(BF16)
    before = jnp.dot(strict, onehot.astype(BF16), preferred_element_type=F32) + carry_ref[...]
    idx_out = jnp.zeros((tm, LANES), F32)
    rank_out = jnp.zeros((tm, LANES), F32)
    gate_out = jnp.zeros((tm, LANES), F32)
    for r in range(TOP_K):
        rank_r = jnp.sum(jnp.where(lane_f == idxs[r], before, 0.0), axis=1, keepdims=True)
        idx_out = jnp.where(lane == r, idxs[r], idx_out)
        rank_out = jnp.where(lane == r, rank_r, rank_out)
        gate_out = jnp.where(lane == r, exps[r] / denom, gate_out)
    idx_ref[...] = idx_out.T[:8].astype(jnp.int32)
    rank_ref[...] = rank_out.T[:8].astype(jnp.int32)
    gate_ref[...] = gate_out
    total = carry_ref[...] + jnp.sum(onehot, axis=0, keepdims=True)
    carry_ref[...] = total
    cnt_ref[...] = jnp.broadcast_to(total, cnt_ref.shape)


def _post_attn(x2d, att, memo, wa, wm, g, b, wr, br):
    t_tokens = x2d.shape[0]
    tm = PROJ_TM
    row_spec = lambda width: pl.BlockSpec((tm, width), lambda i: (i, 0))
    full = lambda a: pl.BlockSpec(a.shape, lambda i: (0, 0))
    return pl.pallas_call(
        _post_attn_kernel,
        grid=(t_tokens // tm,),
        in_specs=[row_spec(D_MODEL), row_spec(att.shape[1]), row_spec(MEM_WIDTH),
                  full(wa), full(wm), full(g), full(b), full(wr), full(br)],
        out_specs=[row_spec(D_MODEL), row_spec(D_MODEL // 2),
                   pl.BlockSpec((8, tm), lambda i: (0, i)), pl.BlockSpec((8, tm), lambda i: (0, i)),
                   row_spec(LANES), pl.BlockSpec((8, LANES), lambda i: (0, 0))],
        out_shape=[jax.ShapeDtypeStruct((t_tokens, D_MODEL), F32),
                   jax.ShapeDtypeStruct((t_tokens, D_MODEL // 2), jnp.uint32),
                   jax.ShapeDtypeStruct((8, t_tokens), jnp.int32),
                   jax.ShapeDtypeStruct((8, t_tokens), jnp.int32),
                   jax.ShapeDtypeStruct((t_tokens, LANES), F32),
                   jax.ShapeDtypeStruct((8, LANES), F32)],
        scratch_shapes=[pltpu.VMEM((1, LANES), F32)],
        compiler_params=_cparams(1),
        name="outproj_ln_router",
    )(x2d, att, memo, wa, wm, g, b, wr, br)


def _slot_kernel(pstart_ref, idx_ref, rank_ref, o_ref):
    idx = idx_ref[...]
    dest = rank_ref[...]
    for e in range(N_EXPERTS):
        dest = dest + jnp.where(idx == e, pstart_ref[e], 0)
    o_ref[...] = dest


def _slot_table(idx, rank, pstart):
    return pl.pallas_call(
        _slot_kernel,
        in_specs=[pl.BlockSpec(memory_space=pltpu.SMEM), pl.BlockSpec(memory_space=pltpu.VMEM),
                  pl.BlockSpec(memory_space=pltpu.VMEM)],
        out_specs=pl.BlockSpec(memory_space=pltpu.VMEM),
        out_shape=jax.ShapeDtypeStruct(idx.shape, jnp.int32),
        name="slot_table",
    )(pstart, idx, rank)


def _expert_kernel(be_ref, nu_ref, br_ref, x_ref, wgu_hbm, bgu_ref, wdn_hbm, bdn_ref, *rest,
                   n_chunks, blk_lo, n_blocks, layer, has_prev):
    o_ref, wgu_st, wdn_st, wgu_bf, wdn_bf, h_ref, sem = rest[1:] if has_prev else rest
    step = pl.program_id(0)
    blk = step + blk_lo
    live_end = jnp.minimum(blk_lo + pl.num_programs(0), nu_ref[0])
    expert = be_ref[blk]

    def fetch(e):
        return (pltpu.make_async_copy(wgu_hbm.at[layer, e], wgu_st, sem.at[0]),
                pltpu.make_async_copy(wdn_hbm.at[layer, e], wdn_st, sem.at[1]))

    @pl.when(((step == 0) | (expert != be_ref[jnp.maximum(blk - 1, 0)])) & (blk < live_end))
    def _():
        @pl.when(step == 0)
        def _():
            for copy in fetch(expert):
                copy.start()

        for copy in fetch(expert):
            copy.wait()
        wgu_bf[...] = wgu_st[...].astype(BF16)
        wdn_bf[...] = wdn_st[...].astype(BF16)
        nxt = lax.while_loop(
            lambda j: (j < live_end) & (be_ref[jnp.minimum(j, n_blocks - 1)] == expert),
            lambda j: j + 1, blk + 1)

        @pl.when(nxt < live_end)
        def _():
            for copy in fetch(be_ref[jnp.minimum(nxt, n_blocks - 1)]):
                copy.start()

    def mlp(rows):
        x = _unpack_bf16_pairs(x_ref[:rows]).astype(BF16)
        cw = D_EXPERT // n_chunks
        for c in range(n_chunks):
            gs = slice(c * cw, (c + 1) * cw)
            us = slice(D_EXPERT + c * cw, D_EXPERT + (c + 1) * cw)
            g = jnp.dot(x, wgu_bf[:, gs], preferred_element_type=F32) + bgu_ref[0, 0, :, gs]
            u = jnp.dot(x, wgu_bf[:, us], preferred_element_type=F32) + bgu_ref[0, 0, :, us]
            g = jnp.minimum(g, SWIGLU_LIMIT)
            u = jnp.clip(u, -SWIGLU_LIMIT, SWIGLU_LIMIT)
            h_ref[:rows, gs] = ((u + 1.0) * (g * jax.nn.sigmoid(SWIGLU_ALPHA * g))).astype(BF16)
        y = jnp.dot(h_ref[:rows], wdn_bf[...], preferred_element_type=F32) + bdn_ref[0, 0]
        return _pack_bf16_pairs(y.astype(BF16))

    tm = o_ref.shape[0]
    live_rows = br_ref[blk]

    @pl.when(live_rows > tm // 2)
    def _():
        o_ref[...] = mlp(tm)

    @pl.when((live_rows > 0) & (live_rows <= tm // 2))
    def _():
        o_ref[:tm // 2] = mlp(tm // 2)
        o_ref[tm // 2:] = jnp.zeros((tm - tm // 2, o_ref.shape[1]), o_ref.dtype)

    @pl.when(live_rows == 0)
    def _():
        o_ref[...] = jnp.zeros_like(o_ref)


def _experts(block_expert, n_used, block_rows, xb, blk_lo, p_rows, y_prev, layer, wgu, bgu, wdn, bdn):
    tm = MOE_TM
    has_prev = y_prev is not None
    b_map = lambda i, be, nu, br: (layer, be[i + blk_lo], 0, 0)
    in_specs = [pl.BlockSpec((tm, D_MODEL // 2), lambda i, be, nu, br: (i, 0)),
                pl.BlockSpec(memory_space=pl.ANY),
                pl.BlockSpec((1, 1, 1, 2 * D_EXPERT), b_map),
                pl.BlockSpec(memory_space=pl.ANY),
                pl.BlockSpec((1, 1, 1, D_MODEL), b_map)]
    args = [block_expert, n_used, block_rows, xb, wgu, bgu, wdn, bdn]
    if has_prev:
        in_specs.append(pl.BlockSpec(memory_space=pl.ANY))
        args.append(y_prev)
    grid_spec = pltpu.PrefetchScalarGridSpec(
        num_scalar_prefetch=3,
        grid=(xb.shape[0] // tm,),
        in_specs=in_specs,
        out_specs=pl.BlockSpec((tm, D_MODEL // 2), lambda i, be, nu, br: (i + blk_lo, 0)),
        scratch_shapes=[pltpu.VMEM((D_MODEL, 2 * D_EXPERT), F32),
                        pltpu.VMEM((D_EXPERT, D_MODEL), F32),
                        pltpu.VMEM((D_MODEL, 2 * D_EXPERT), BF16),
                        pltpu.VMEM((D_EXPERT, D_MODEL), BF16),
                        pltpu.VMEM((tm, D_EXPERT), BF16),
                        pltpu.SemaphoreType.DMA((2,))],
    )
    return pl.pallas_call(
        functools.partial(_expert_kernel, n_chunks=4, blk_lo=blk_lo, n_blocks=p_rows // tm,
                          layer=layer, has_prev=has_prev),
        grid_spec=grid_spec,
        out_shape=jax.ShapeDtypeStruct((p_rows, D_MODEL // 2), jnp.uint32),
        input_output_aliases={len(args) - 1: 0} if has_prev else {},
        compiler_params=_cparams(1),
        name="experts",
    )(*args)


def _sc_move_rows(table_hbm, idx_v, out_hbm, base, n_chunks, rows_v, gsem, wsem):
    def gather(c, slot):
        off = pl.multiple_of(c * SC_CHUNK, SC_CHUNK)
        return pltpu.make_async_copy(table_hbm.at[idx_v.at[pl.ds(off, SC_CHUNK)]],
                                     rows_v.at[slot], gsem.at[slot])

    def put(c, slot):
        off = pl.multiple_of(c * SC_CHUNK, SC_CHUNK)
        return pltpu.make_async_copy(rows_v.at[slot], out_hbm.at[pl.ds(base + off, SC_CHUNK)],
                                     wsem.at[slot])

    gather(0, 0).start()
    gather(1, 1).start()

    @pl.loop(0, n_chunks, step=2)
    def _(c):
        for slot in range(2):
            gather(c + slot, slot).wait()
            put(c + slot, slot).start()
        for slot in range(2):
            put(c + slot, slot).wait()

            @pl.when(c + 2 + slot < n_chunks)
            def _():
                gather(c + 2 + slot, slot).start()


def _sc_gather_rows(table, idx):
    n_idx = idx.shape[0]
    width = table.shape[1]
    per_worker = n_idx // SC_WORKERS
    n_chunks = per_worker // SC_CHUNK
    assert per_worker * SC_WORKERS == n_idx and n_chunks * SC_CHUNK == per_worker and n_chunks % 2 == 0
    mesh = plsc.VectorSubcoreMesh(core_axis_name="c", subcore_axis_name="s",
                                  num_cores=SC_CORES, num_subcores=SC_SUBCORES)

    def body(table_hbm, idx_hbm, out_hbm, idx_v, rows_v, gsem, wsem):
        wid = lax.axis_index("s") * SC_CORES + lax.axis_index("c")
        base = wid * per_worker
        pltpu.sync_copy(idx_hbm.at[pl.ds(base, per_worker)], idx_v)
        _sc_move_rows(table_hbm, idx_v, out_hbm, base, n_chunks, rows_v, gsem, wsem)

    return pl.kernel(
        body,
        out_type=jax.ShapeDtypeStruct((n_idx, width), table.dtype),
        mesh=mesh,
        scratch_types=[pltpu.VMEM((per_worker,), jnp.int32),
                       pltpu.VMEM((2, SC_CHUNK, width), table.dtype),
                       pltpu.SemaphoreType.DMA((2,)),
                       pltpu.SemaphoreType.DMA((2,))],
        name="sc_gather_rows",
    )(table, idx)


def _sc_dispatch_rows(table, dest, slot_lo, n_slots):
    n_tok, width = table.shape
    n_pairs = dest.shape[0]
    per_worker = n_slots // SC_WORKERS
    n_chunks = per_worker // SC_CHUNK
    assert per_worker * SC_WORKERS == n_slots and n_chunks * SC_CHUNK == per_worker and n_chunks % 2 == 0
    n_scan = n_pairs // SC_SCAN
    assert n_scan * SC_SCAN == n_pairs
    mesh = plsc.VectorSubcoreMesh(core_axis_name="c", subcore_axis_name="s",
                                  num_cores=SC_CORES, num_subcores=SC_SUBCORES)

    def body(table_hbm, dest_hbm, out_hbm, tok_v, dest_v, rows_v, gsem, wsem):
        wid = lax.axis_index("s") * SC_CORES + lax.axis_index("c")
        base = wid * per_worker
        first = slot_lo + base
        lane = lax.iota(jnp.int32, SC_LANES)

        @pl.loop(0, per_worker // SC_LANES)
        def _(i):
            off = pl.multiple_of(i * SC_LANES, SC_LANES)
            tok_v[pl.ds(off, SC_LANES)] = lax.rem(first + off + lane, n_tok)

        @pl.loop(0, n_scan)
        def _(g):
            goff = pl.multiple_of(g * SC_SCAN, SC_SCAN)
            pltpu.sync_copy(dest_hbm.at[pl.ds(goff, SC_SCAN)], dest_v)

            @plsc.parallel_loop(0, SC_SCAN // SC_LANES, unroll=8)
            def _(i):
                off = pl.multiple_of(i * SC_LANES, SC_LANES)
                local = dest_v[pl.ds(off, SC_LANES)] - first
                mine = (local >= 0) & (local < per_worker)
                pair = goff + off + lane
                plsc.store_scatter(tok_v, [jnp.where(mine, local, 0)], lax.rem(pair, n_tok), mask=mine)

        _sc_move_rows(table_hbm, tok_v, out_hbm, base, n_chunks, rows_v, gsem, wsem)

    return pl.kernel(
        body,
        out_type=jax.ShapeDtypeStruct((n_slots, width), table.dtype),
        mesh=mesh,
        scratch_types=[pltpu.VMEM((per_worker,), jnp.int32),
                       pltpu.VMEM((SC_SCAN,), jnp.int32),
                       pltpu.VMEM((2, SC_CHUNK, width), table.dtype),
                       pltpu.SemaphoreType.DMA((2,)),
                       pltpu.SemaphoreType.DMA((2,))],
        compiler_params=pltpu.CompilerParams(needs_layout_passes=False),
        name="sc_dispatch_rows",
    )(table, dest)


def _combine_kernel(x1_ref, yg_ref, gate_ref, g_ref, b_ref, *rest):
    o_ref = rest[-1]
    gates = gate_ref[...]
    ffn = _unpack_bf16_pairs(yg_ref[0]) * gates[:, 0:1]
    for r in range(1, TOP_K):
        ffn = ffn + _unpack_bf16_pairs(yg_ref[r]) * gates[:, r:r + 1]
    o_ref[...] = _layer_norm(DEEPNORM_ALPHA * x1_ref[...] + ffn, g_ref[...], b_ref[...])


def _combine(x1, yg, gates, g, b, tok_lo, out_prev):
    t_tokens = x1.shape[0]
    tm = COMBINE_TM
    tile_lo = tok_lo // tm
    in_specs = [pl.BlockSpec((tm, D_MODEL), lambda i: (i + tile_lo, 0)),
                pl.BlockSpec((TOP_K, tm, D_MODEL // 2), lambda i: (0, i, 0)),
                pl.BlockSpec((tm, LANES), lambda i: (i + tile_lo, 0)),
                pl.BlockSpec((1, D_MODEL), lambda i: (0, 0)),
                pl.BlockSpec((1, D_MODEL), lambda i: (0, 0))]
    args = [x1, yg, gates, g, b]
    if out_prev is not None:
        in_specs.append(pl.BlockSpec(memory_space=pl.ANY))
        args.append(out_prev)
    return pl.pallas_call(
        _combine_kernel,
        grid=(yg.shape[1] // tm,),
        in_specs=in_specs,
        out_specs=pl.BlockSpec((tm, D_MODEL), lambda i: (i + tile_lo, 0)),
        out_shape=jax.ShapeDtypeStruct((t_tokens, D_MODEL), F32),
        input_output_aliases={len(args) - 1: 0} if out_prev is not None else {},
        compiler_params=_cparams(1),
        name="combine_ln",
    )(*args)


def _moe(x1, x1p, idx, rank, gates, counts, layer, wgu, bgu, wdn, bdn, g, b):
    t_tokens = x1.shape[0]
    tm = MOE_TM
    counts = counts[0, :N_EXPERTS].astype(jnp.int32)
    padded = ((counts + tm - 1) // tm) * tm
    pend = jnp.cumsum(padded)
    pstart = pend - padded
    dest = _slot_table(idx, rank, pstart)[:TOP_K]
    n_blocks = (t_tokens * TOP_K) // tm + N_EXPERTS
    p_rows = n_blocks * tm
    block_start = jnp.arange(n_blocks, dtype=jnp.int32) * tm
    block_expert = jnp.minimum(
        jnp.sum((pend[None, :] <= block_start[:, None]).astype(jnp.int32), axis=1), N_EXPERTS - 1)
    n_used = (pend[-1:] // tm).astype(jnp.int32)
    mine = block_expert[:, None] == jnp.arange(N_EXPERTS, dtype=jnp.int32)[None, :]
    live_end = jnp.sum(jnp.where(mine, (pstart + counts)[None, :], 0), axis=1)
    block_rows = jnp.clip(live_end - block_start, 0, tm)
    y = None
    for lo, hi in ((0, n_blocks // EXPERT_SPLIT), (n_blocks // EXPERT_SPLIT, n_blocks)):
        xb = _sc_dispatch_rows(x1p, dest.reshape(-1), lo * tm, (hi - lo) * tm)
        y = _experts(block_expert, n_used, block_rows, xb, lo, p_rows, y, layer, wgu, bgu, wdn, bdn)
    out = None
    for lo, hi in ((0, t_tokens // MOE_SPLIT), (t_tokens // MOE_SPLIT, t_tokens)):
        yg = _sc_gather_rows(y, dest[:, lo:hi].reshape(-1)).reshape(TOP_K, hi - lo, D_MODEL // 2)
        out = _combine(x1, yg, gates, g, b, lo, out)
    return out


def _row(v, width=None):
    v = v.astype(F32).reshape(1, -1)
    if width is not None and v.shape[1] < width:
        v = jnp.pad(v, ((0, 0), (0, width - v.shape[1])))
    return v


def _pad_cols(w, width):
    return jnp.pad(w, ((0, 0), (0, width - w.shape[1])))


def kernel(x, mem, positions, a_w_in, a_b_f, a_w_out, b_w_in, b_g_q, b_w_uq, b_w_out,
           kv_w_dkv, kv_g, kv_w_ukv, mem_w_kv, ln_g, ln_b,
           moe_w_r, moe_b_r, moe_w_gu, moe_b_gu, moe_w_dn, moe_b_dn):
    nb, seq, d = x.shape
    t_tokens = nb * seq
    n_a = a_w_in.shape[0]
    x2d = x.reshape(t_tokens, d)
    mem2d = mem.reshape(nb * N_MEM, d)
    pos2d = positions.reshape(1, t_tokens)
    half = QK_ROPE // 2
    invf = (ROPE_THETA ** (-jnp.arange(half, dtype=F32) * 2.0 / QK_ROPE)).reshape(half, 1)

    shared_kv = None
    for l in range(DEPTH):
        mk, mvt = _mem_proj(mem2d, mem_w_kv[l].astype(BF16), nb)
        if l < n_a:
            w_in = a_w_in[l]
            w = jnp.concatenate([w_in[:, :3 * FOX_WIDTH],
                                 _pad_cols(w_in[:, 3 * FOX_WIDTH:3 * FOX_WIDTH + FOX_HEADS], LANES),
                                 w_in[:, 3 * FOX_WIDTH + FOX_HEADS:]], axis=1).astype(BF16)
            qt, k, vt, mqt, cum, cumt, cfirst, clast, knorm = _fox_proj(
                x2d, w, _row(a_b_f[l], LANES), seq)
            stats = [s.reshape(-1, LANES) for s in (cfirst, clast, knorm)]
            att = _causal_attention(qt, k.reshape(nb, seq, -1), vt,
                                    [cum.reshape(nb, seq, LANES), cumt] + stats, slab=False)
            w_out = a_w_out[l]
        else:
            bl = l - n_a
            w = jnp.concatenate([b_w_in[bl], kv_w_dkv[:, :KV_LORA],
                                 jnp.zeros((d, QK_NOPE), F32), kv_w_dkv[:, KV_LORA:],
                                 jnp.zeros((d, LANES - QK_NOPE - QK_ROPE), F32)], axis=1).astype(BF16)
            wuq = b_w_uq[bl].reshape(Q_LORA, MLA_HEADS, QK_NOPE + QK_ROPE)
            wuq = jnp.pad(wuq, ((0, 0), (0, 0), (0, MLA_SLAB - QK_NOPE - QK_ROPE)))
            wuq = wuq.reshape(Q_LORA, MLA_HEADS * MLA_SLAB).astype(BF16)
            wukv = kv_w_ukv.reshape(KV_LORA, MLA_HEADS, QK_NOPE + V_DIM)
            wk = jnp.pad(wukv[:, :, :QK_NOPE], ((0, 0), (0, 0), (0, MLA_SLAB - QK_NOPE)))
            wk = wk.reshape(KV_LORA, MLA_HEADS * MLA_SLAB).astype(BF16)
            wv = wukv[:, :, QK_NOPE:].reshape(KV_LORA, MLA_V_WIDTH).astype(BF16)
            qt, mqt, k_new, vt_new = _mla_proj(x2d, pos2d, invf, w, _row(b_g_q[bl]), wuq,
                                              _row(kv_g), wk, wv, seq)
            if shared_kv is None:
                shared_kv = (k_new.reshape(nb, seq, -1), vt_new)
            att = _causal_attention(qt, shared_kv[0], shared_kv[1], None, slab=True)
            w_out = b_w_out[bl]
        memo = _memory_attention(mqt, mk, mvt)
        n_att = w_out.shape[0] - MEM_WIDTH
        x1, x1p, idx, rank, gates, counts = _post_attn(
            x2d, att.reshape(t_tokens, -1), memo.reshape(t_tokens, MEM_WIDTH),
            w_out[:n_att].astype(BF16), w_out[n_att:].astype(BF16),
            _row(ln_g[l, 0]), _row(ln_b[l, 0]),
            _pad_cols(moe_w_r[l], LANES).astype(BF16), _row(moe_b_r[l], LANES))
        x2d = _moe(x1, x1p, idx, rank, gates, counts, l,
                   moe_w_gu, moe_b_gu.reshape(DEPTH, N_EXPERTS, 1, -1),
                   moe_w_dn, moe_b_dn.reshape(DEPTH, N_EXPERTS, 1, -1),
                   _row(ln_g[l, 1]), _row(ln_b[l, 1]))
    return x2d.reshape(nb, seq, d)
```

```python
import functools
import math

import jax
import jax.numpy as jnp
from jax import lax
from jax.experimental import pallas as pl
from jax.experimental.pallas import tpu as pltpu
from jax.experimental.pallas import tpu_sc as plsc

F32 = jnp.float32
BF16 = jnp.bfloat16

D_MODEL = 1024
DEPTH = 2
N_MEM = 256
HEAD_DIM = 64
FOX_HEADS = 12
MEM_HEADS = 4
MLA_HEADS = 12
Q_LORA = 384
KV_LORA = 256
QK_NOPE = 64
QK_ROPE = 32
V_DIM = 64
ROPE_THETA = 10000.0
N_EXPERTS = 32
TOP_K = 4
D_EXPERT = D_MODEL
SWIGLU_LIMIT = 7.0
SWIGLU_ALPHA = 1.702
LN_EPS = 1e-5
RMS_EPS = 1e-6
NEG_INF = -1e30
DEEPNORM_ALPHA = (2 * DEPTH) ** 0.25
FOX_WIDTH = FOX_HEADS * HEAD_DIM
MEM_WIDTH = MEM_HEADS * HEAD_DIM
MLA_V_WIDTH = MLA_HEADS * V_DIM

LANES = 128
LOG2E = math.log2(math.e)
VMEM_LIMIT = 48 * 1024 * 1024

PROJ_TM = 512
ATT_TQ = 512
FOX_TK = 256
MOE_TM = 512
COMBINE_TM = 512
MOE_SPLIT = 4
MLA_SLAB = LANES
VT_ROWS = HEAD_DIM + 16
SC_CORES = 2
SC_SUBCORES = 16
SC_WORKERS = SC_CORES * SC_SUBCORES
SC_CHUNK = 32
SC_LANES = 16
SC_SCAN = 8192


def _cparams(n_axes):
    return pltpu.CompilerParams(dimension_semantics=("arbitrary",) * n_axes,
                                vmem_limit_bytes=VMEM_LIMIT)


def _split3(x):
    hi = x.astype(BF16)
    r1 = x - hi.astype(F32)
    mid = r1.astype(BF16)
    lo = (r1 - mid.astype(F32)).astype(BF16)
    return hi, mid, lo


def _pack_bf16_pairs(v):
    bits = pltpu.bitcast(v.astype(F32), jnp.uint32)
    half = v.shape[1] // 2
    return (bits[:, :half] >> 16) | bits[:, half:]


def _unpack_bf16_pairs(words):
    return jnp.concatenate([pltpu.bitcast(words << 16, F32),
                            pltpu.bitcast(words & jnp.uint32(0xFFFF0000), F32)], axis=1)


def _layer_norm(y, g, b):
    mu = jnp.mean(y, axis=-1, keepdims=True)
    yc = y - mu
    var = jnp.mean(yc * yc, axis=-1, keepdims=True)
    return yc * lax.rsqrt(var + LN_EPS) * g + b


def _rms_norm(y, g):
    return y * lax.rsqrt(jnp.mean(y * y, axis=-1, keepdims=True) + RMS_EPS) * g


def _store_transposed(dst_ref, val):
    for s in range(val.shape[1] // LANES):
        sl = slice(s * LANES, (s + 1) * LANES)
        dst_ref[0, sl, :] = val[:, sl].T.astype(dst_ref.dtype)


def _store_values_transposed(vt_ref, val):
    tm = val.shape[0]
    ones = jnp.ones((VT_ROWS - HEAD_DIM, tm), vt_ref.dtype)
    for s in range(val.shape[1] // LANES):
        pair_t = val[:, s * LANES:(s + 1) * LANES].T.astype(vt_ref.dtype)
        for h in range(2):
            r0 = (2 * s + h) * VT_ROWS
            vt_ref[0, r0:r0 + HEAD_DIM, :] = pair_t[h * HEAD_DIM:(h + 1) * HEAD_DIM]
            vt_ref[0, r0 + HEAD_DIM:r0 + VT_ROWS, :] = ones


def _fox_proj_kernel(x_ref, w_ref, bf_ref, seg_ref, qt_ref, k_ref, vt_ref, mqt_ref, cum_ref,
                     cumt_ref, cfirst_ref, clast_ref, knorm_ref, carry_ref, knmax_ref, *,
                     tiles_per_batch):
    t = pl.program_id(0)

    @pl.when(t % tiles_per_batch == 0)
    def _():
        carry_ref[...] = jnp.zeros_like(carry_ref)
        knmax_ref[...] = jnp.zeros_like(knmax_ref)

    tm = x_ref.shape[0]
    proj = jnp.dot(x_ref[...].astype(BF16), w_ref[...], preferred_element_type=F32)
    qscale = HEAD_DIM ** -0.5 * LOG2E
    _store_transposed(qt_ref, proj[:, :FOX_WIDTH] * qscale)
    kb = proj[:, FOX_WIDTH:2 * FOX_WIDTH].astype(BF16)
    k_ref[...] = kb
    _store_values_transposed(vt_ref, proj[:, 2 * FOX_WIDTH:3 * FOX_WIDTH])
    kf = kb.astype(F32)
    ksq = jnp.dot((kf * kf).astype(BF16), seg_ref[...], preferred_element_type=F32)
    n_sub = tm // FOX_TK
    for sub in range(n_sub):
        tile_max = jnp.sqrt(jnp.max(ksq[sub * FOX_TK:(sub + 1) * FOX_TK], axis=0, keepdims=True))
        knmax_ref[...] = jnp.maximum(knmax_ref[...], tile_max)
        knorm_ref[sub] = knmax_ref[...]
    f = proj[:, 3 * FOX_WIDTH:3 * FOX_WIDTH + LANES] + bf_ref[...]
    _store_transposed(mqt_ref, proj[:, 3 * FOX_WIDTH + LANES:] * qscale)
    log_f = jnp.minimum(f, 0.0) - jnp.log1p(jnp.exp(-jnp.abs(f)))
    row = lax.broadcasted_iota(jnp.int32, (tm, tm), 0)
    col = lax.broadcasted_iota(jnp.int32, (tm, tm), 1)
    tri = jnp.where(row >= col, 1.0, 0.0).astype(BF16)
    hi, mid, lo = _split3(log_f)
    cum = (jnp.dot(tri, hi, preferred_element_type=F32)
           + jnp.dot(tri, mid, preferred_element_type=F32)
           + jnp.dot(tri, lo, preferred_element_type=F32)) + carry_ref[...]
    carry_ref[...] = cum[tm - 1:tm, :]
    cum2 = cum * LOG2E
    cum_ref[...] = cum2
    cumt_ref[0] = cum2.T[:16, :]
    for sub in range(n_sub):
        cfirst_ref[sub] = cum2[sub * FOX_TK:sub * FOX_TK + 1, :]
        clast_ref[sub] = cum2[(sub + 1) * FOX_TK - 1:(sub + 1) * FOX_TK, :]


def _fox_proj(x2d, w, bf, seq):
    t_tokens = x2d.shape[0]
    tm = PROJ_TM
    nb = t_tokens // seq
    n = w.shape[1]
    tiles_per_batch = seq // tm
    n_tiles = t_tokens // tm
    seg = (jnp.arange(FOX_WIDTH)[:, None] // HEAD_DIM == jnp.arange(LANES)[None, :]).astype(BF16)
    row_spec = lambda width: pl.BlockSpec((tm, width), lambda i: (i, 0))
    t_spec = lambda rows: pl.BlockSpec(
        (1, rows, tm), lambda i: (i // tiles_per_batch, 0, i % tiles_per_batch))
    n_sub = tm // FOX_TK
    stat_spec = pl.BlockSpec((n_sub, 1, LANES), lambda i: (i, 0, 0))
    stat_shape = jax.ShapeDtypeStruct((n_tiles * n_sub, 1, LANES), F32)
    return pl.pallas_call(
        functools.partial(_fox_proj_kernel, tiles_per_batch=tiles_per_batch),
        grid=(n_tiles,),
        in_specs=[row_spec(D_MODEL),
                  pl.BlockSpec((D_MODEL, n), lambda i: (0, 0)),
                  pl.BlockSpec((1, LANES), lambda i: (0, 0)),
                  pl.BlockSpec((FOX_WIDTH, LANES), lambda i: (0, 0))],
        out_specs=[t_spec(FOX_WIDTH), row_spec(FOX_WIDTH), t_spec(FOX_HEADS * VT_ROWS),
                   t_spec(MEM_WIDTH), row_spec(LANES), t_spec(16),
                   stat_spec, stat_spec, stat_spec],
        out_shape=[jax.ShapeDtypeStruct((nb, FOX_WIDTH, seq), BF16),
                   jax.ShapeDtypeStruct((t_tokens, FOX_WIDTH), BF16),
                   jax.ShapeDtypeStruct((nb, FOX_HEADS * VT_ROWS, seq), BF16),
                   jax.ShapeDtypeStruct((nb, MEM_WIDTH, seq), BF16),
                   jax.ShapeDtypeStruct((t_tokens, LANES), F32),
                   jax.ShapeDtypeStruct((nb, 16, seq), F32),
                   stat_shape, stat_shape, stat_shape],
        scratch_shapes=[pltpu.VMEM((1, LANES), F32), pltpu.VMEM((1, LANES), F32)],
        compiler_params=_cparams(1),
        name="fox_proj",
    )(x2d, w, bf, seg)


def _rope_slab_t(slab_t, cos_t, sin_t):
    half = QK_ROPE // 2
    x1 = slab_t[QK_NOPE:QK_NOPE + half]
    x2 = slab_t[QK_NOPE + half:QK_NOPE + QK_ROPE]
    return jnp.concatenate([slab_t[:QK_NOPE], x1 * cos_t - x2 * sin_t, x1 * sin_t + x2 * cos_t,
                            slab_t[QK_NOPE + QK_ROPE:]], axis=0)


def _mla_proj_kernel(x_ref, pos_ref, invf_ref, w_ref, gq_ref, wuq_ref, gkv_ref, wk_ref, wv_ref,
                     qt_ref, mqt_ref, k_ref, vt_ref):
    proj = jnp.dot(x_ref[...].astype(BF16), w_ref[...], preferred_element_type=F32)
    ang_t = invf_ref[...] * pos_ref[...].astype(F32)
    cos_t = jnp.cos(ang_t)
    sin_t = jnp.sin(ang_t)
    c_q = _rms_norm(proj[:, :Q_LORA], gq_ref[...])
    _store_transposed(mqt_ref, proj[:, Q_LORA:Q_LORA + MEM_WIDTH] * (HEAD_DIM ** -0.5 * LOG2E))
    kv_off = Q_LORA + MEM_WIDTH
    c_kv = _rms_norm(proj[:, kv_off:kv_off + KV_LORA], gkv_ref[...])
    kr = _rope_slab_t(proj[:, kv_off + KV_LORA:].T, cos_t, sin_t).T
    q = jnp.dot(c_q.astype(BF16), wuq_ref[...], preferred_element_type=F32)
    kn = jnp.dot(c_kv.astype(BF16), wk_ref[...], preferred_element_type=F32)
    qscale = (QK_NOPE + QK_ROPE) ** -0.5 * LOG2E
    for h in range(MLA_HEADS):
        sl = slice(h * MLA_SLAB, (h + 1) * MLA_SLAB)
        qt_ref[0, sl, :] = (_rope_slab_t(q[:, sl].T, cos_t, sin_t) * qscale).astype(BF16)
        k_ref[:, sl] = (kn[:, sl] + kr).astype(BF16)
    _store_values_transposed(vt_ref, jnp.dot(c_kv.astype(BF16), wv_ref[...],
                                             preferred_element_type=F32))


def _mla_proj(x2d, pos2d, invf, w, gq, wuq, gkv, wk, wv, seq):
    t_tokens = x2d.shape[0]
    tm = PROJ_TM
    nb = t_tokens // seq
    tiles_per_batch = seq // tm
    row_spec = lambda width: pl.BlockSpec((tm, width), lambda i: (i, 0))
    t_spec = lambda rows: pl.BlockSpec(
        (1, rows, tm), lambda i: (i // tiles_per_batch, 0, i % tiles_per_batch))
    full = lambda a: pl.BlockSpec(a.shape, lambda i: (0, 0))
    slabs = MLA_HEADS * MLA_SLAB
    return pl.pallas_call(
        _mla_proj_kernel,
        grid=(t_tokens // tm,),
        in_specs=[row_spec(D_MODEL), pl.BlockSpec((1, tm), lambda i: (0, i)), full(invf), full(w),
                  full(gq), full(wuq), full(gkv), full(wk), full(wv)],
        out_specs=[t_spec(slabs), t_spec(MEM_WIDTH), row_spec(slabs),
                   t_spec(MLA_HEADS * VT_ROWS)],
        out_shape=[jax.ShapeDtypeStruct((nb, slabs, seq), BF16),
                   jax.ShapeDtypeStruct((nb, MEM_WIDTH, seq), BF16),
                   jax.ShapeDtypeStruct((t_tokens, slabs), BF16),
                   jax.ShapeDtypeStruct((nb, MLA_HEADS * VT_ROWS, seq), BF16)],
        compiler_params=_cparams(1),
        name="mla_proj",
    )(x2d, pos2d, invf, w, gq, wuq, gkv, wk, wv)


SKIP_LOG2 = -160.0
NORM_SLACK = 1.02


def _causal_attn_kernel(*refs, tq, tk, n_chunks, fox, slab, long_body):
    if fox:
        (cf_ref, cl_ref, kn_ref, qt_ref, k_ref, vt_ref, cq_ref, ck_ref,
         o_ref, m_ref, acc_ref, s_ref, cmax_ref) = refs
    else:
        qt_ref, k_ref, vt_ref, o_ref, m_ref, acc_ref, s_ref, cmax_ref = refs
    b = pl.program_id(0)
    pair = pl.program_id(1)
    i = pl.program_id(2)
    m_ref[...] = jnp.full(m_ref.shape, NEG_INF, F32)
    acc_ref[...] = jnp.zeros(acc_ref.shape, F32)
    qt = qt_ref[0]
    if slab:
        qth = [qt[:MLA_SLAB], qt[MLA_SLAB:]]
    else:
        rowi = lax.broadcasted_iota(jnp.int32, qt.shape, 0)
        zero = jnp.zeros_like(qt)
        qth = [jnp.where(rowi < HEAD_DIM, qt, zero), jnp.where(rowi >= HEAD_DIM, qt, zero)]
    if fox:
        lane = lax.broadcasted_iota(jnp.int32, (1, LANES), 1)
        cq = [cq_ref[0, pl.ds(2 * pair + h, 1), :] for h in range(2)]

    def logits_to(slot, j):
        off = pl.multiple_of(j * tk, tk)
        kc = k_ref[0, pl.ds(off, tk), :]
        for h in range(2):
            kh = kc[:, h * MLA_SLAB:(h + 1) * MLA_SLAB] if slab else kc
            s = jnp.dot(kh, qth[h], preferred_element_type=F32)
            if fox:
                ck_blk = ck_ref[0, pl.ds(off, tk), :]
                ck = jnp.sum(jnp.where(lane == 2 * pair + h, ck_blk, 0.0), axis=1, keepdims=True)
                s = s + cq[h] - ck
            s_ref[slot, h] = s
            cmax_ref[slot, h] = jnp.max(s, axis=0, keepdims=True)

    def softmax_pv(slot, j, causal_mask):
        off = pl.multiple_of(j * tk, tk)
        for h in range(2):
            s = s_ref[slot, h]
            if causal_mask is not None:
                s = jnp.where(causal_mask, s, NEG_INF)
                chunk_max = jnp.max(s, axis=0, keepdims=True)
            else:
                chunk_max = cmax_ref[slot, h]
            m_prev = m_ref[h]
            m_new = jnp.maximum(m_prev, chunk_max)
            alpha = jnp.exp2(m_prev - m_new)
            p = jnp.exp2(s - m_new).astype(BF16)
            vth = vt_ref[0, h * VT_ROWS:(h + 1) * VT_ROWS, pl.ds(off, tk)]
            acc_ref[h] = acc_ref[h] * alpha + jnp.dot(vth, p, preferred_element_type=F32)
            m_ref[h] = m_new

    per_tile = tq // tk
    d0 = per_tile * i
    if fox:
        base = b * n_chunks
        qn = []
        for h in range(2):
            qf = qth[h].astype(F32)
            qn.append(jnp.sqrt(jnp.max(jnp.sum(qf * qf, axis=0, keepdims=True))) * NORM_SLACK)

        top = [2.0 * qn[h] * kn_ref[base + d0 + per_tile - 1, 2 * pair + h]
               + cf_ref[base + d0, 2 * pair + h] for h in range(2)]

        def live(j):
            ub = [top[h] - cl_ref[base + jnp.maximum(j, 0), 2 * pair + h] for h in range(2)]
            return (j >= 0) & (jnp.maximum(ub[0], ub[1]) > SKIP_LOG2)

        j0 = lax.while_loop(live, lambda j: j - 1, d0 - 1) + 1
    else:
        j0 = 0
    n_before = d0 - j0

    def causal_mask(shift):
        key = lax.broadcasted_iota(jnp.int32, (tk, tq), 0)
        qry = lax.broadcasted_iota(jnp.int32, (tk, tq), 1)
        return key + shift <= qry

    def finish():
        if per_tile == 2:
            logits_to(1, d0 + 1)
        softmax_pv(0, d0, causal_mask(0))
        if per_tile == 2:
            softmax_pv(1, d0 + 1, causal_mask(tk))

    def pipelined():
        odd = (n_before % 2) == 1

        @pl.when(odd)
        def _():
            logits_to(1, j0)
            logits_to(0, j0 + 1)
            softmax_pv(1, j0, None)

        @pl.when(jnp.logical_not(odd))
        def _():
            logits_to(0, j0)

        def two_steps(j):
            logits_to(1, j + 1)
            softmax_pv(0, j, None)
            logits_to(0, j + 2)
            softmax_pv(1, j + 1, None)

        j_even = j0 + (n_before % 2)
        n_two = n_before // 2
        if long_body:
            @pl.when((n_two % 2) == 1)
            def _():
                two_steps(j_even)

            j_quad = j_even + 2 * (n_two % 2)

            def body(t, carry):
                two_steps(j_quad + 4 * t)
                two_steps(j_quad + 4 * t + 2)
                return carry

            lax.fori_loop(0, n_two // 2, body, 0)
        else:
            def body(t, carry):
                two_steps(j_even + 2 * t)
                return carry

            lax.fori_loop(0, n_two, body, 0)
        finish()

    if fox:
        @pl.when(n_before == 1)
        def _():
            logits_to(1, j0)
            logits_to(0, d0)
            softmax_pv(1, j0, None)
            finish()

        @pl.when(n_before != 1)
        def _():
            pipelined()
    else:
        pipelined()
    out_t = jnp.concatenate(
        [acc_ref[h, :HEAD_DIM] * (1.0 / acc_ref[h, HEAD_DIM:HEAD_DIM + 1]) for h in range(2)],
        axis=0)
    o_ref[0] = out_t.T.astype(o_ref.dtype)


def _causal_attention(qt, k, vt, fox_args, *, slab):
    nb, seq, _ = k.shape
    fox = fox_args is not None
    tq = ATT_TQ
    tk = FOX_TK if fox else ATT_TQ
    n_tiles = seq // tq
    n_pairs = FOX_HEADS // 2
    rows = 2 * MLA_SLAB if slab else LANES
    in_specs = [pl.BlockSpec((1, rows, tq), lambda b, p, i: (b, p, i)),
                pl.BlockSpec((1, seq, rows), lambda b, p, i: (b, 0, p)),
                pl.BlockSpec((1, 2 * VT_ROWS, seq), lambda b, p, i: (b, p, 0))]
    args = [qt, k, vt]
    if fox:
        cum, cumt, cfirst, clast, knorm = fox_args
        smem = pl.BlockSpec(memory_space=pltpu.SMEM)
        in_specs = [smem, smem, smem] + in_specs + [
            pl.BlockSpec((1, 16, tq), lambda b, p, i: (b, 0, i)),
            pl.BlockSpec((1, seq, LANES), lambda b, p, i: (b, 0, 0))]
        args = [cfirst, clast, knorm] + args + [cumt, cum]
    return pl.pallas_call(
        functools.partial(_causal_attn_kernel, tq=tq, tk=tk, n_chunks=seq // tk, fox=fox, slab=slab,
                          long_body=not fox),
        grid=(nb, n_pairs, n_tiles),
        in_specs=in_specs,
        out_specs=pl.BlockSpec((1, tq, LANES), lambda b, p, i: (b, i, p)),
        out_shape=jax.ShapeDtypeStruct((nb, seq, n_pairs * LANES), BF16),
        scratch_shapes=[pltpu.VMEM((2, 1, tq), F32), pltpu.VMEM((2, VT_ROWS, tq), F32),
                        pltpu.VMEM((2, 2, tk, tq), F32), pltpu.VMEM((2, 2, 1, tq), F32)],
        compiler_params=_cparams(3),
        name="fox_attention" if fox else "mla_attention",
    )(*args)


def _mem_proj_kernel(mem_ref, w_ref, mk_ref, mvt_ref):
    mkv = jnp.dot(mem_ref[...].astype(BF16), w_ref[...], preferred_element_type=F32)
    mk_ref[0] = mkv[:, :MEM_WIDTH].astype(BF16)
    _store_values_transposed(mvt_ref, mkv[:, MEM_WIDTH:])


def _mem_proj(mem2d, w, nb):
    return pl.pallas_call(
        _mem_proj_kernel,
        grid=(nb,),
        in_specs=[pl.BlockSpec((N_MEM, D_MODEL), lambda b: (b, 0)),
                  pl.BlockSpec(w.shape, lambda b: (0, 0))],
        out_specs=[pl.BlockSpec((1, N_MEM, MEM_WIDTH), lambda b: (b, 0, 0)),
                   pl.BlockSpec((1, MEM_HEADS * VT_ROWS, N_MEM), lambda b: (b, 0, 0))],
        out_shape=[jax.ShapeDtypeStruct((nb, N_MEM, MEM_WIDTH), BF16),
                   jax.ShapeDtypeStruct((nb, MEM_HEADS * VT_ROWS, N_MEM), BF16)],
        compiler_params=_cparams(1),
        name="mem_proj",
    )(mem2d, w)


def _mem_attn_kernel(qt_ref, k_ref, vt_ref, o_ref):
    qt = qt_ref[0]
    kc = k_ref[0]
    for pair in range(MEM_HEADS // 2):
        qt_p = qt[pair * LANES:(pair + 1) * LANES]
        k_p = kc[:, pair * LANES:(pair + 1) * LANES]
        rowi = lax.broadcasted_iota(jnp.int32, qt_p.shape, 0)
        zero = jnp.zeros_like(qt_p)
        outs = []
        for h in range(2):
            mine = (rowi < HEAD_DIM) if h == 0 else (rowi >= HEAD_DIM)
            s = jnp.dot(k_p, jnp.where(mine, qt_p, zero), preferred_element_type=F32)
            p = jnp.exp2(s - jnp.max(s, axis=0, keepdims=True)).astype(BF16)
            r0 = (2 * pair + h) * VT_ROWS
            acc = jnp.dot(vt_ref[0, r0:r0 + VT_ROWS, :], p, preferred_element_type=F32)
            outs.append(acc[:HEAD_DIM] * (1.0 / acc[HEAD_DIM:HEAD_DIM + 1]))
        o_ref[0, :, pair * LANES:(pair + 1) * LANES] = (
            jnp.concatenate(outs, axis=0).T.astype(o_ref.dtype))


def _memory_attention(mqt, mk, mvt):
    nb, _, seq = mqt.shape
    tq = ATT_TQ
    return pl.pallas_call(
        _mem_attn_kernel,
        grid=(nb, seq // tq),
        in_specs=[pl.BlockSpec((1, MEM_WIDTH, tq), lambda b, i: (b, 0, i)),
                  pl.BlockSpec((1, N_MEM, MEM_WIDTH), lambda b, i: (b, 0, 0)),
                  pl.BlockSpec((1, MEM_HEADS * VT_ROWS, N_MEM), lambda b, i: (b, 0, 0))],
        out_specs=pl.BlockSpec((1, tq, MEM_WIDTH), lambda b, i: (b, i, 0)),
        out_shape=jax.ShapeDtypeStruct((nb, seq, MEM_WIDTH), BF16),
        compiler_params=_cparams(2),
        name="memory_attention",
    )(mqt, mk, mvt)


def _post_attn_kernel(x_ref, att_ref, memo_ref, wa_ref, wm_ref, g_ref, b_ref, wr_ref, br_ref,
                      x1_ref, x1p_ref, idx_ref, rank_ref, gate_ref, cnt_ref, carry_ref):
    t = pl.program_id(0)

    @pl.when(t == 0)
    def _():
        carry_ref[...] = jnp.zeros_like(carry_ref)

    tm = x_ref.shape[0]
    mix = (jnp.dot(att_ref[...], wa_ref[...], preferred_element_type=F32)
           + jnp.dot(memo_ref[...], wm_ref[...], preferred_element_type=F32))
    x1 = _layer_norm(DEEPNORM_ALPHA * x_ref[...] + mix, g_ref[...], b_ref[...])
    x1_ref[...] = x1
    x1b = x1.astype(BF16)
    x1p_ref[...] = _pack_bf16_pairs(x1b)
    logits = jnp.dot(x1b, wr_ref[...], preferred_element_type=F32) + br_ref[...]
    work = logits.T[:N_EXPERTS]
    row_f = lax.broadcasted_iota(jnp.int32, work.shape, 0).astype(F32)
    idxs, vals = [], []
    onehot_t = jnp.zeros(work.shape, F32)
    for _ in range(TOP_K):
        best = jnp.max(work, axis=0, keepdims=True)
        where_best = jnp.min(jnp.where(work == best, row_f, float(N_EXPERTS)), axis=0, keepdims=True)
        hit = row_f == where_best
        onehot_t = jnp.where(hit, 1.0, onehot_t)
        work = jnp.where(hit, -jnp.inf, work)
        idxs.append(where_best)
        vals.append(best)
    exps = [jnp.exp(v - vals[0]) for v in vals]
    denom = exps[0] + exps[1] + exps[2] + exps[3]
    src = lax.broadcasted_iota(jnp.int32, (tm, tm), 0)
    dst = lax.broadcasted_iota(jnp.int32, (tm, tm), 1)
    earlier = jnp.where(src < dst, 1.0, 0.0).astype(BF16)
    before_t = (jnp.dot(onehot_t.astype(BF16), earlier, preferred_element_type=F32)
                + carry_ref[...])
    ranks = [jnp.sum(jnp.where(row_f == idxs[r], before_t, 0.0), axis=0, keepdims=True)
             for r in range(TOP_K)]
    pad = jnp.zeros((8 - TOP_K, tm), F32)
    idx_ref[...] = jnp.concatenate(idxs + [pad], axis=0).astype(jnp.int32)
    rank_ref[...] = jnp.concatenate(ranks + [pad], axis=0).astype(jnp.int32)
    gates_t = jnp.concatenate([e / denom for e in exps] + [jnp.zeros((LANES - TOP_K, tm), F32)], axis=0)
    gate_ref[...] = gates_t.T
    total = carry_ref[...] + jnp.sum(onehot_t, axis=1, keepdims=True)
    carry_ref[...] = total
    cnt_ref[...] = jnp.broadcast_to(total, cnt_ref.shape)


def _post_attn(x2d, att, memo, wa, wm, g, b, wr, br):
    t_tokens = x2d.shape[0]
    tm = PROJ_TM
    row_spec = lambda width: pl.BlockSpec((tm, width), lambda i: (i, 0))
    full = lambda a: pl.BlockSpec(a.shape, lambda i: (0, 0))
    return pl.pallas_call(
        _post_attn_kernel,
        grid=(t_tokens // tm,),
        in_specs=[row_spec(D_MODEL), row_spec(att.shape[1]), row_spec(MEM_WIDTH),
                  full(wa), full(wm), full(g), full(b), full(wr), full(br)],
        out_specs=[row_spec(D_MODEL), row_spec(D_MODEL // 2),
                   pl.BlockSpec((8, tm), lambda i: (0, i)), pl.BlockSpec((8, tm), lambda i: (0, i)),
                   row_spec(LANES), pl.BlockSpec((N_EXPERTS, LANES), lambda i: (0, 0))],
        out_shape=[jax.ShapeDtypeStruct((t_tokens, D_MODEL), F32),
                   jax.ShapeDtypeStruct((t_tokens, D_MODEL // 2), jnp.uint32),
                   jax.ShapeDtypeStruct((8, t_tokens), jnp.int32),
                   jax.ShapeDtypeStruct((8, t_tokens), jnp.int32),
                   jax.ShapeDtypeStruct((t_tokens, LANES), F32),
                   jax.ShapeDtypeStruct((N_EXPERTS, LANES), F32)],
        scratch_shapes=[pltpu.VMEM((N_EXPERTS, 1), F32)],
        compiler_params=_cparams(1),
        name="outproj_ln_router",
    )(x2d, att, memo, wa, wm, g, b, wr, br)


def _slot_kernel(pstart_ref, idx_ref, rank_ref, o_ref):
    idx = idx_ref[...]
    dest = rank_ref[...]
    for e in range(N_EXPERTS):
        dest = dest + jnp.where(idx == e, pstart_ref[e], 0)
    o_ref[...] = dest


def _slot_table(idx, rank, pstart):
    return pl.pallas_call(
        _slot_kernel,
        in_specs=[pl.BlockSpec(memory_space=pltpu.SMEM), pl.BlockSpec(memory_space=pltpu.VMEM),
                  pl.BlockSpec(memory_space=pltpu.VMEM)],
        out_specs=pl.BlockSpec(memory_space=pltpu.VMEM),
        out_shape=jax.ShapeDtypeStruct(idx.shape, jnp.int32),
        name="slot_table",
    )(pstart, idx, rank)


def _expert_kernel(be_ref, nu_ref, br_ref, x_ref, wgu_hbm, bgu_ref, wdn_hbm, bdn_ref, *rest,
                   n_chunks, blk_lo, n_blocks, layer, has_prev):
    o_ref, wgu_st, wdn_st, wgu_bf, wdn_bf, h_ref, sem = rest[1:] if has_prev else rest
    step = pl.program_id(0)
    blk = step + blk_lo
    live_end = jnp.minimum(blk_lo + pl.num_programs(0), nu_ref[0])
    expert = be_ref[blk]

    def fetch(e):
        return (pltpu.make_async_copy(wgu_hbm.at[layer, e], wgu_st, sem.at[0]),
                pltpu.make_async_copy(wdn_hbm.at[layer, e], wdn_st, sem.at[1]))

    @pl.when(((step == 0) | (expert != be_ref[jnp.maximum(blk - 1, 0)])) & (blk < live_end))
    def _():
        @pl.when(step == 0)
        def _():
            for copy in fetch(expert):
                copy.start()

        for copy in fetch(expert):
            copy.wait()
        wgu_bf[...] = wgu_st[...].astype(BF16)
        wdn_bf[...] = wdn_st[...].astype(BF16)
        nxt = lax.while_loop(
            lambda j: (j < live_end) & (be_ref[jnp.minimum(j, n_blocks - 1)] == expert),
            lambda j: j + 1, blk + 1)

        @pl.when(nxt < live_end)
        def _():
            for copy in fetch(be_ref[jnp.minimum(nxt, n_blocks - 1)]):
                copy.start()

    def mlp(rows):
        x = _unpack_bf16_pairs(x_ref[:rows]).astype(BF16)
        cw = D_EXPERT // n_chunks
        for c in range(n_chunks):
            gs = slice(c * cw, (c + 1) * cw)
            us = slice(D_EXPERT + c * cw, D_EXPERT + (c + 1) * cw)
            g = jnp.dot(x, wgu_bf[:, gs], preferred_element_type=F32) + bgu_ref[0, 0, :, gs]
            u = jnp.dot(x, wgu_bf[:, us], preferred_element_type=F32) + bgu_ref[0, 0, :, us]
            g = jnp.minimum(g, SWIGLU_LIMIT)
            u = jnp.clip(u, -SWIGLU_LIMIT, SWIGLU_LIMIT)
            h_ref[:rows, gs] = ((u + 1.0) * (g * jax.nn.sigmoid(SWIGLU_ALPHA * g))).astype(BF16)
        y = jnp.dot(h_ref[:rows], wdn_bf[...], preferred_element_type=F32) + bdn_ref[0, 0]
        return _pack_bf16_pairs(y.astype(BF16))

    tm = o_ref.shape[0]
    live_rows = br_ref[blk]

    @pl.when(live_rows > tm // 2)
    def _():
        o_ref[...] = mlp(tm)

    @pl.when((live_rows > 0) & (live_rows <= tm // 2))
    def _():
        o_ref[:tm // 2] = mlp(tm // 2)
        o_ref[tm // 2:] = jnp.zeros((tm - tm // 2, o_ref.shape[1]), o_ref.dtype)

    @pl.when(live_rows == 0)
    def _():
        o_ref[...] = jnp.zeros_like(o_ref)


def _experts(block_expert, n_used, block_rows, xb, blk_lo, p_rows, y_prev, layer, wgu, bgu, wdn, bdn):
    tm = MOE_TM
    has_prev = y_prev is not None
    b_map = lambda i, be, nu, br: (layer, be[i + blk_lo], 0, 0)
    in_specs = [pl.BlockSpec((tm, D_MODEL // 2), lambda i, be, nu, br: (i, 0)),
                pl.BlockSpec(memory_space=pl.ANY),
                pl.BlockSpec((1, 1, 1, 2 * D_EXPERT), b_map),
                pl.BlockSpec(memory_space=pl.ANY),
                pl.BlockSpec((1, 1, 1, D_MODEL), b_map)]
    args = [block_expert, n_used, block_rows, xb, wgu, bgu, wdn, bdn]
    if has_prev:
        in_specs.append(pl.BlockSpec(memory_space=pl.ANY))
        args.append(y_prev)
    grid_spec = pltpu.PrefetchScalarGridSpec(
        num_scalar_prefetch=3,
        grid=(xb.shape[0] // tm,),
        in_specs=in_specs,
        out_specs=pl.BlockSpec((tm, D_MODEL // 2), lambda i, be, nu, br: (i + blk_lo, 0)),
        scratch_shapes=[pltpu.VMEM((D_MODEL, 2 * D_EXPERT), F32),
                        pltpu.VMEM((D_EXPERT, D_MODEL), F32),
                        pltpu.VMEM((D_MODEL, 2 * D_EXPERT), BF16),
                        pltpu.VMEM((D_EXPERT, D_MODEL), BF16),
                        pltpu.VMEM((tm, D_EXPERT), BF16),
                        pltpu.SemaphoreType.DMA((2,))],
    )
    return pl.pallas_call(
        functools.partial(_expert_kernel, n_chunks=4, blk_lo=blk_lo, n_blocks=p_rows // tm,
                          layer=layer, has_prev=has_prev),
        grid_spec=grid_spec,
        out_shape=jax.ShapeDtypeStruct((p_rows, D_MODEL // 2), jnp.uint32),
        input_output_aliases={len(args) - 1: 0} if has_prev else {},
        compiler_params=_cparams(1),
        name="experts",
    )(*args)


def _sc_move_rows(table_hbm, idx_v, out_hbm, base, n_chunks, rows_v, gsem, wsem):
    def gather(c, slot):
        off = pl.multiple_of(c * SC_CHUNK, SC_CHUNK)
        return pltpu.make_async_copy(table_hbm.at[idx_v.at[pl.ds(off, SC_CHUNK)]],
                                     rows_v.at[slot], gsem.at[slot])

    def put(c, slot):
        off = pl.multiple_of(c * SC_CHUNK, SC_CHUNK)
        return pltpu.make_async_copy(rows_v.at[slot], out_hbm.at[pl.ds(base + off, SC_CHUNK)],
                                     wsem.at[slot])

    gather(0, 0).start()
    gather(1, 1).start()

    @pl.loop(0, n_chunks, step=2)
    def _(c):
        for slot in range(2):
            gather(c + slot, slot).wait()
            put(c + slot, slot).start()
        for slot in range(2):
            put(c + slot, slot).wait()

            @pl.when(c + 2 + slot < n_chunks)
            def _():
                gather(c + 2 + slot, slot).start()


def _sc_gather_rows(table, idx):
    n_idx = idx.shape[0]
    width = table.shape[1]
    per_worker = n_idx // SC_WORKERS
    n_chunks = per_worker // SC_CHUNK
    assert per_worker * SC_WORKERS == n_idx and n_chunks * SC_CHUNK == per_worker and n_chunks % 2 == 0
    mesh = plsc.VectorSubcoreMesh(core_axis_name="c", subcore_axis_name="s",
                                  num_cores=SC_CORES, num_subcores=SC_SUBCORES)

    def body(table_hbm, idx_hbm, out_hbm, idx_v, rows_v, gsem, wsem):
        wid = lax.axis_index("s") * SC_CORES + lax.axis_index("c")
        base = wid * per_worker
        pltpu.sync_copy(idx_hbm.at[pl.ds(base, per_worker)], idx_v)
        _sc_move_rows(table_hbm, idx_v, out_hbm, base, n_chunks, rows_v, gsem, wsem)

    return pl.kernel(
        body,
        out_type=jax.ShapeDtypeStruct((n_idx, width), table.dtype),
        mesh=mesh,
        scratch_types=[pltpu.VMEM((per_worker,), jnp.int32),
                       pltpu.VMEM((2, SC_CHUNK, width), table.dtype),
                       pltpu.SemaphoreType.DMA((2,)),
                       pltpu.SemaphoreType.DMA((2,))],
        name="sc_gather_rows",
    )(table, idx)


def _sc_dispatch_rows(table, dest, slot_lo, n_slots):
    n_tok, width = table.shape
    n_pairs = dest.shape[0]
    per_worker = n_slots // SC_WORKERS
    n_chunks = per_worker // SC_CHUNK
    assert per_worker * SC_WORKERS == n_slots and n_chunks * SC_CHUNK == per_worker and n_chunks % 2 == 0
    n_scan = n_pairs // SC_SCAN
    assert n_scan * SC_SCAN == n_pairs
    mesh = plsc.VectorSubcoreMesh(core_axis_name="c", subcore_axis_name="s",
                                  num_cores=SC_CORES, num_subcores=SC_SUBCORES)

    def body(table_hbm, dest_hbm, out_hbm, tok_v, dest_v, rows_v, gsem, wsem):
        wid = lax.axis_index("s") * SC_CORES + lax.axis_index("c")
        base = wid * per_worker
        first = slot_lo + base
        lane = lax.iota(jnp.int32, SC_LANES)

        @pl.loop(0, per_worker // SC_LANES)
        def _(i):
            off = pl.multiple_of(i * SC_LANES, SC_LANES)
            tok_v[pl.ds(off, SC_LANES)] = lax.rem(first + off + lane, n_tok)

        @pl.loop(0, n_scan)
        def _(g):
            goff = pl.multiple_of(g * SC_SCAN, SC_SCAN)
            pltpu.sync_copy(dest_hbm.at[pl.ds(goff, SC_SCAN)], dest_v)

            @plsc.parallel_loop(0, SC_SCAN // SC_LANES, unroll=8)
            def _(i):
                off = pl.multiple_of(i * SC_LANES, SC_LANES)
                local = dest_v[pl.ds(off, SC_LANES)] - first
                mine = (local >= 0) & (local < per_worker)
                pair = goff + off + lane
                plsc.store_scatter(tok_v, [jnp.where(mine, local, 0)], lax.rem(pair, n_tok), mask=mine)

        _sc_move_rows(table_hbm, tok_v, out_hbm, base, n_chunks, rows_v, gsem, wsem)

    return pl.kernel(
        body,
        out_type=jax.ShapeDtypeStruct((n_slots, width), table.dtype),
        mesh=mesh,
        scratch_types=[pltpu.VMEM((per_worker,), jnp.int32),
                       pltpu.VMEM((SC_SCAN,), jnp.int32),
                       pltpu.VMEM((2, SC_CHUNK, width), table.dtype),
                       pltpu.SemaphoreType.DMA((2,)),
                       pltpu.SemaphoreType.DMA((2,))],
        compiler_params=pltpu.CompilerParams(needs_layout_passes=False),
        name="sc_dispatch_rows",
    )(table, dest)


def _combine_kernel(x1_ref, yg_ref, gate_ref, g_ref, b_ref, *rest):
    o_ref = rest[-1]
    gates = gate_ref[...]
    ffn = _unpack_bf16_pairs(yg_ref[0]) * gates[:, 0:1]
    for r in range(1, TOP_K):
        ffn = ffn + _unpack_bf16_pairs(yg_ref[r]) * gates[:, r:r + 1]
    o_ref[...] = _layer_norm(DEEPNORM_ALPHA * x1_ref[...] + ffn, g_ref[...], b_ref[...])


def _combine(x1, yg, gates, g, b, tok_lo, out_prev):
    t_tokens = x1.shape[0]
    tm = COMBINE_TM
    tile_lo = tok_lo // tm
    in_specs = [pl.BlockSpec((tm, D_MODEL), lambda i: (i + tile_lo, 0)),
                pl.BlockSpec((TOP_K, tm, D_MODEL // 2), lambda i: (0, i, 0)),
                pl.BlockSpec((tm, LANES), lambda i: (i + tile_lo, 0)),
                pl.BlockSpec((1, D_MODEL), lambda i: (0, 0)),
                pl.BlockSpec((1, D_MODEL), lambda i: (0, 0))]
    args = [x1, yg, gates, g, b]
    if out_prev is not None:
        in_specs.append(pl.BlockSpec(memory_space=pl.ANY))
        args.append(out_prev)
    return pl.pallas_call(
        _combine_kernel,
        grid=(yg.shape[1] // tm,),
        in_specs=in_specs,
        out_specs=pl.BlockSpec((tm, D_MODEL), lambda i: (i + tile_lo, 0)),
        out_shape=jax.ShapeDtypeStruct((t_tokens, D_MODEL), F32),
        input_output_aliases={len(args) - 1: 0} if out_prev is not None else {},
        compiler_params=_cparams(1),
        name="combine_ln",
    )(*args)


def _moe(x1, x1p, idx, rank, gates, counts, layer, wgu, bgu, wdn, bdn, g, b):
    t_tokens = x1.shape[0]
    tm = MOE_TM
    counts = counts[:, 0].astype(jnp.int32)
    padded = ((counts + tm - 1) // tm) * tm
    pend = jnp.cumsum(padded)
    pstart = pend - padded
    dest = _slot_table(idx, rank, pstart)[:TOP_K]
    n_blocks = (t_tokens * TOP_K) // tm + N_EXPERTS
    p_rows = n_blocks * tm
    block_start = jnp.arange(n_blocks, dtype=jnp.int32) * tm
    block_expert = jnp.minimum(
        jnp.sum((pend[None, :] <= block_start[:, None]).astype(jnp.int32), axis=1), N_EXPERTS - 1)
    n_used = (pend[-1:] // tm).astype(jnp.int32)
    mine = block_expert[:, None] == jnp.arange(N_EXPERTS, dtype=jnp.int32)[None, :]
    live_end = jnp.sum(jnp.where(mine, (pstart + counts)[None, :], 0), axis=1)
    block_rows = jnp.clip(live_end - block_start, 0, tm)
    y = None
    for lo, hi in ((0, n_blocks // MOE_SPLIT), (n_blocks // MOE_SPLIT, n_blocks)):
        xb = _sc_dispatch_rows(x1p, dest.reshape(-1), lo * tm, (hi - lo) * tm)
        y = _experts(block_expert, n_used, block_rows, xb, lo, p_rows, y, layer, wgu, bgu, wdn, bdn)
    out = None
    for lo, hi in ((0, t_tokens // MOE_SPLIT), (t_tokens // MOE_SPLIT, t_tokens)):
        yg = _sc_gather_rows(y, dest[:, lo:hi].reshape(-1)).reshape(TOP_K, hi - lo, D_MODEL // 2)
        out = _combine(x1, yg, gates, g, b, lo, out)
    return out


def _row(v, width=None):
    v = v.astype(F32).reshape(1, -1)
    if width is not None and v.shape[1] < width:
        v = jnp.pad(v, ((0, 0), (0, width - v.shape[1])))
    return v


def _pad_cols(w, width):
    return jnp.pad(w, ((0, 0), (0, width - w.shape[1])))


def kernel(x, mem, positions, a_w_in, a_b_f, a_w_out, b_w_in, b_g_q, b_w_uq, b_w_out,
           kv_w_dkv, kv_g, kv_w_ukv, mem_w_kv, ln_g, ln_b,
           moe_w_r, moe_b_r, moe_w_gu, moe_b_gu, moe_w_dn, moe_b_dn):
    nb, seq, d = x.shape
    t_tokens = nb * seq
    n_a = a_w_in.shape[0]
    x2d = x.reshape(t_tokens, d)
    mem2d = mem.reshape(nb * N_MEM, d)
    pos2d = positions.reshape(1, t_tokens)
    half = QK_ROPE // 2
    invf = (ROPE_THETA ** (-jnp.arange(half, dtype=F32) * 2.0 / QK_ROPE)).reshape(half, 1)

    shared_kv = None
    for l in range(DEPTH):
        mk, mvt = _mem_proj(mem2d, mem_w_kv[l].astype(BF16), nb)
        if l < n_a:
            w_in = a_w_in[l]
            w = jnp.concatenate([w_in[:, :3 * FOX_WIDTH],
                                 _pad_cols(w_in[:, 3 * FOX_WIDTH:3 * FOX_WIDTH + FOX_HEADS], LANES),
                                 w_in[:, 3 * FOX_WIDTH + FOX_HEADS:]], axis=1).astype(BF16)
            qt, k, vt, mqt, cum, cumt, cfirst, clast, knorm = _fox_proj(
                x2d, w, _row(a_b_f[l], LANES), seq)
            stats = [s.reshape(-1, LANES) for s in (cfirst, clast, knorm)]
            att = _causal_attention(qt, k.reshape(nb, seq, -1), vt,
                                    [cum.reshape(nb, seq, LANES), cumt] + stats, slab=False)
            w_out = a_w_out[l]
        else:
            bl = l - n_a
            w = jnp.concatenate([b_w_in[bl], kv_w_dkv[:, :KV_LORA],
                                 jnp.zeros((d, QK_NOPE), F32), kv_w_dkv[:, KV_LORA:],
                                 jnp.zeros((d, LANES - QK_NOPE - QK_ROPE), F32)], axis=1).astype(BF16)
            wuq = b_w_uq[bl].reshape(Q_LORA, MLA_HEADS, QK_NOPE + QK_ROPE)
            wuq = jnp.pad(wuq, ((0, 0), (0, 0), (0, MLA_SLAB - QK_NOPE - QK_ROPE)))
            wuq = wuq.reshape(Q_LORA, MLA_HEADS * MLA_SLAB).astype(BF16)
            wukv = kv_w_ukv.reshape(KV_LORA, MLA_HEADS, QK_NOPE + V_DIM)
            wk = jnp.pad(wukv[:, :, :QK_NOPE], ((0, 0), (0, 0), (0, MLA_SLAB - QK_NOPE)))
            wk = wk.reshape(KV_LORA, MLA_HEADS * MLA_SLAB).astype(BF16)
            wv = wukv[:, :, QK_NOPE:].reshape(KV_LORA, MLA_V_WIDTH).astype(BF16)
            qt, mqt, k_new, vt_new = _mla_proj(x2d, pos2d, invf, w, _row(b_g_q[bl]), wuq,
                                              _row(kv_g), wk, wv, seq)
            if shared_kv is None:
                shared_kv = (k_new.reshape(nb, seq, -1), vt_new)
            att = _causal_attention(qt, shared_kv[0], shared_kv[1], None, slab=True)
            w_out = b_w_out[bl]
        memo = _memory_attention(mqt, mk, mvt)
        n_att = w_out.shape[0] - MEM_WIDTH
        x1, x1p, idx, rank, gates, counts = _post_attn(
            x2d, att.reshape(t_tokens, -1), memo.reshape(t_tokens, MEM_WIDTH),
            w_out[:n_att].astype(BF16), w_out[n_att:].astype(BF16),
            _row(ln_g[l, 0]), _row(ln_b[l, 0]),
            _pad_cols(moe_w_r[l], LANES).astype(BF16), _row(moe_b_r[l], LANES))
        x2d = _moe(x1, x1p, idx, rank, gates, counts, l,
                   moe_w_gu, moe_b_gu.reshape(DEPTH, N_EXPERTS, 1, -1),
                   moe_w_dn, moe_b_dn.reshape(DEPTH, N_EXPERTS, 1, -1),
                   _row(ln_g[l, 1]), _row(ln_b[l, 1]))
    return x2d.reshape(nb, seq, d)
```

```python
import functools
import math

import jax
import jax.numpy as jnp
from jax import lax
from jax.experimental import pallas as pl
from jax.experimental.pallas import tpu as pltpu
from jax.experimental.pallas import tpu_sc as plsc

F32 = jnp.float32
BF16 = jnp.bfloat16

D_MODEL = 1024
DEPTH = 2
N_MEM = 256
HEAD_DIM = 64
FOX_HEADS = 12
MEM_HEADS = 4
MLA_HEADS = 12
Q_LORA = 384
KV_LORA = 256
QK_NOPE = 64
QK_ROPE = 32
V_DIM = 64
ROPE_THETA = 10000.0
N_EXPERTS = 32
TOP_K = 4
D_EXPERT = D_MODEL
SWIGLU_LIMIT = 7.0
SWIGLU_ALPHA = 1.702
LN_EPS = 1e-5
RMS_EPS = 1e-6
NEG_INF = -1e30
DEEPNORM_ALPHA = (2 * DEPTH) ** 0.25
FOX_WIDTH = FOX_HEADS * HEAD_DIM
MEM_WIDTH = MEM_HEADS * HEAD_DIM
MLA_V_WIDTH = MLA_HEADS * V_DIM

LANES = 128
LOG2E = math.log2(math.e)
VMEM_LIMIT = 48 * 1024 * 1024

PROJ_TM = 512
ATT_TQ = 512
FOX_TK = 256
MOE_TM = 512
COMBINE_TM = 1024
MOE_SPLIT = 4
MLA_SLAB = LANES
VT_ROWS = HEAD_DIM + 16
SC_CORES = 2
SC_SUBCORES = 16
SC_WORKERS = SC_CORES * SC_SUBCORES
SC_CHUNK = 32
SC_LANES = 16
SC_SCAN = 8192


def _cparams(n_axes):
    return pltpu.CompilerParams(dimension_semantics=("arbitrary",) * n_axes,
                                vmem_limit_bytes=VMEM_LIMIT)


def _split3(x):
    hi = x.astype(BF16)
    r1 = x - hi.astype(F32)
    mid = r1.astype(BF16)
    lo = (r1 - mid.astype(F32)).astype(BF16)
    return hi, mid, lo


def _pack_bf16_pairs(v):
    bits = pltpu.bitcast(v.astype(F32), jnp.uint32)
    half = v.shape[1] // 2
    return (bits[:, :half] >> 16) | bits[:, half:]


def _unpack_bf16_pairs(words):
    return jnp.concatenate([pltpu.bitcast(words << 16, F32),
                            pltpu.bitcast(words & jnp.uint32(0xFFFF0000), F32)], axis=1)


def _layer_norm(y, g, b):
    mu = jnp.mean(y, axis=-1, keepdims=True)
    yc = y - mu
    var = jnp.mean(yc * yc, axis=-1, keepdims=True)
    return yc * lax.rsqrt(var + LN_EPS) * g + b


def _rms_norm(y, g):
    return y * lax.rsqrt(jnp.mean(y * y, axis=-1, keepdims=True) + RMS_EPS) * g


def _store_transposed(dst_ref, val):
    for s in range(val.shape[1] // LANES):
        sl = slice(s * LANES, (s + 1) * LANES)
        dst_ref[0, sl, :] = val[:, sl].T.astype(dst_ref.dtype)


def _store_values_transposed(vt_ref, val):
    tm = val.shape[0]
    ones = jnp.ones((VT_ROWS - HEAD_DIM, tm), vt_ref.dtype)
    for s in range(val.shape[1] // LANES):
        pair_t = val[:, s * LANES:(s + 1) * LANES].T.astype(vt_ref.dtype)
        for h in range(2):
            r0 = (2 * s + h) * VT_ROWS
            vt_ref[0, r0:r0 + HEAD_DIM, :] = pair_t[h * HEAD_DIM:(h + 1) * HEAD_DIM]
            vt_ref[0, r0 + HEAD_DIM:r0 + VT_ROWS, :] = ones


def _fox_proj_kernel(x_ref, w_ref, bf_ref, seg_ref, qt_ref, k_ref, vt_ref, mqt_ref, cum_ref,
                     cumt_ref, cfirst_ref, clast_ref, knorm_ref, carry_ref, knmax_ref, *,
                     tiles_per_batch):
    t = pl.program_id(0)

    @pl.when(t % tiles_per_batch == 0)
    def _():
        carry_ref[...] = jnp.zeros_like(carry_ref)
        knmax_ref[...] = jnp.zeros_like(knmax_ref)

    tm = x_ref.shape[0]
    proj = jnp.dot(x_ref[...].astype(BF16), w_ref[...], preferred_element_type=F32)
    qscale = HEAD_DIM ** -0.5 * LOG2E
    _store_transposed(qt_ref, proj[:, :FOX_WIDTH] * qscale)
    kb = proj[:, FOX_WIDTH:2 * FOX_WIDTH].astype(BF16)
    k_ref[...] = kb
    _store_values_transposed(vt_ref, proj[:, 2 * FOX_WIDTH:3 * FOX_WIDTH])
    kf = kb.astype(F32)
    ksq = jnp.dot((kf * kf).astype(BF16), seg_ref[...], preferred_element_type=F32)
    n_sub = tm // FOX_TK
    for sub in range(n_sub):
        tile_max = jnp.sqrt(jnp.max(ksq[sub * FOX_TK:(sub + 1) * FOX_TK], axis=0, keepdims=True))
        knmax_ref[...] = jnp.maximum(knmax_ref[...], tile_max)
        knorm_ref[sub] = knmax_ref[...]
    f = proj[:, 3 * FOX_WIDTH:3 * FOX_WIDTH + LANES] + bf_ref[...]
    _store_transposed(mqt_ref, proj[:, 3 * FOX_WIDTH + LANES:] * qscale)
    log_f = jnp.minimum(f, 0.0) - jnp.log1p(jnp.exp(-jnp.abs(f)))
    row = lax.broadcasted_iota(jnp.int32, (tm, tm), 0)
    col = lax.broadcasted_iota(jnp.int32, (tm, tm), 1)
    tri = jnp.where(row >= col, 1.0, 0.0).astype(BF16)
    hi, mid, lo = _split3(log_f)
    cum = (jnp.dot(tri, hi, preferred_element_type=F32)
           + jnp.dot(tri, mid, preferred_element_type=F32)
           + jnp.dot(tri, lo, preferred_element_type=F32)) + carry_ref[...]
    carry_ref[...] = cum[tm - 1:tm, :]
    cum2 = cum * LOG2E
    cum_ref[...] = cum2
    cumt_ref[0] = cum2.T[:16, :]
    for sub in range(n_sub):
        cfirst_ref[sub] = cum2[sub * FOX_TK:sub * FOX_TK + 1, :]
        clast_ref[sub] = cum2[(sub + 1) * FOX_TK - 1:(sub + 1) * FOX_TK, :]


def _fox_proj(x2d, w, bf, seq):
    t_tokens = x2d.shape[0]
    tm = PROJ_TM
    nb = t_tokens // seq
    n = w.shape[1]
    tiles_per_batch = seq // tm
    n_tiles = t_tokens // tm
    seg = (jnp.arange(FOX_WIDTH)[:, None] // HEAD_DIM == jnp.arange(LANES)[None, :]).astype(BF16)
    row_spec = lambda width: pl.BlockSpec((tm, width), lambda i: (i, 0))
    t_spec = lambda rows: pl.BlockSpec(
        (1, rows, tm), lambda i: (i // tiles_per_batch, 0, i % tiles_per_batch))
    n_sub = tm // FOX_TK
    stat_spec = pl.BlockSpec((n_sub, 1, LANES), lambda i: (i, 0, 0))
    stat_shape = jax.ShapeDtypeStruct((n_tiles * n_sub, 1, LANES), F32)
    return pl.pallas_call(
        functools.partial(_fox_proj_kernel, tiles_per_batch=tiles_per_batch),
        grid=(n_tiles,),
        in_specs=[row_spec(D_MODEL),
                  pl.BlockSpec((D_MODEL, n), lambda i: (0, 0)),
                  pl.BlockSpec((1, LANES), lambda i: (0, 0)),
                  pl.BlockSpec((FOX_WIDTH, LANES), lambda i: (0, 0))],
        out_specs=[t_spec(FOX_WIDTH), row_spec(FOX_WIDTH), t_spec(FOX_HEADS * VT_ROWS),
                   t_spec(MEM_WIDTH), row_spec(LANES), t_spec(16),
                   stat_spec, stat_spec, stat_spec],
        out_shape=[jax.ShapeDtypeStruct((nb, FOX_WIDTH, seq), BF16),
                   jax.ShapeDtypeStruct((t_tokens, FOX_WIDTH), BF16),
                   jax.ShapeDtypeStruct((nb, FOX_HEADS * VT_ROWS, seq), BF16),
                   jax.ShapeDtypeStruct((nb, MEM_WIDTH, seq), BF16),
                   jax.ShapeDtypeStruct((t_tokens, LANES), F32),
                   jax.ShapeDtypeStruct((nb, 16, seq), F32),
                   stat_shape, stat_shape, stat_shape],
        scratch_shapes=[pltpu.VMEM((1, LANES), F32), pltpu.VMEM((1, LANES), F32)],
        compiler_params=_cparams(1),
        name="fox_proj",
    )(x2d, w, bf, seg)


def _rope_slab_t(slab_t, cos_t, sin_t):
    half = QK_ROPE // 2
    x1 = slab_t[QK_NOPE:QK_NOPE + half]
    x2 = slab_t[QK_NOPE + half:QK_NOPE + QK_ROPE]
    return jnp.concatenate([slab_t[:QK_NOPE], x1 * cos_t - x2 * sin_t, x1 * sin_t + x2 * cos_t,
                            slab_t[QK_NOPE + QK_ROPE:]], axis=0)


def _mla_proj_kernel(x_ref, pos_ref, invf_ref, w_ref, gq_ref, wuq_ref, gkv_ref, wk_ref, wv_ref,
                     qt_ref, mqt_ref, k_ref, vt_ref):
    proj = jnp.dot(x_ref[...].astype(BF16), w_ref[...], preferred_element_type=F32)
    ang_t = invf_ref[...] * pos_ref[...].astype(F32)
    cos_t = jnp.cos(ang_t)
    sin_t = jnp.sin(ang_t)
    c_q = _rms_norm(proj[:, :Q_LORA], gq_ref[...])
    _store_transposed(mqt_ref, proj[:, Q_LORA:Q_LORA + MEM_WIDTH] * (HEAD_DIM ** -0.5 * LOG2E))
    kv_off = Q_LORA + MEM_WIDTH
    c_kv = _rms_norm(proj[:, kv_off:kv_off + KV_LORA], gkv_ref[...])
    kr = _rope_slab_t(proj[:, kv_off + KV_LORA:].T, cos_t, sin_t).T
    q = jnp.dot(c_q.astype(BF16), wuq_ref[...], preferred_element_type=F32)
    kn = jnp.dot(c_kv.astype(BF16), wk_ref[...], preferred_element_type=F32)
    qscale = (QK_NOPE + QK_ROPE) ** -0.5 * LOG2E
    for h in range(MLA_HEADS):
        sl = slice(h * MLA_SLAB, (h + 1) * MLA_SLAB)
        qt_ref[0, sl, :] = (_rope_slab_t(q[:, sl].T, cos_t, sin_t) * qscale).astype(BF16)
        k_ref[:, sl] = (kn[:, sl] + kr).astype(BF16)
    _store_values_transposed(vt_ref, jnp.dot(c_kv.astype(BF16), wv_ref[...],
                                             preferred_element_type=F32))


def _mla_proj(x2d, pos2d, invf, w, gq, wuq, gkv, wk, wv, seq):
    t_tokens = x2d.shape[0]
    tm = PROJ_TM
    nb = t_tokens // seq
    tiles_per_batch = seq // tm
    row_spec = lambda width: pl.BlockSpec((tm, width), lambda i: (i, 0))
    t_spec = lambda rows: pl.BlockSpec(
        (1, rows, tm), lambda i: (i // tiles_per_batch, 0, i % tiles_per_batch))
    full = lambda a: pl.BlockSpec(a.shape, lambda i: (0, 0))
    slabs = MLA_HEADS * MLA_SLAB
    return pl.pallas_call(
        _mla_proj_kernel,
        grid=(t_tokens // tm,),
        in_specs=[row_spec(D_MODEL), pl.BlockSpec((1, tm), lambda i: (0, i)), full(invf), full(w),
                  full(gq), full(wuq), full(gkv), full(wk), full(wv)],
        out_specs=[t_spec(slabs), t_spec(MEM_WIDTH), row_spec(slabs),
                   t_spec(MLA_HEADS * VT_ROWS)],
        out_shape=[jax.ShapeDtypeStruct((nb, slabs, seq), BF16),
                   jax.ShapeDtypeStruct((nb, MEM_WIDTH, seq), BF16),
                   jax.ShapeDtypeStruct((t_tokens, slabs), BF16),
                   jax.ShapeDtypeStruct((nb, MLA_HEADS * VT_ROWS, seq), BF16)],
        compiler_params=_cparams(1),
        name="mla_proj",
    )(x2d, pos2d, invf, w, gq, wuq, gkv, wk, wv)


SKIP_LOG2 = -160.0
NORM_SLACK = 1.02


def _causal_attn_kernel(*refs, tq, tk, n_chunks, fox, slab, long_body):
    if fox:
        (cf_ref, cl_ref, kn_ref, qt_ref, k_ref, vt_ref, cq_ref, ck_ref,
         o_ref, m_ref, acc_ref, s_ref, cmax_ref) = refs
    else:
        qt_ref, k_ref, vt_ref, o_ref, m_ref, acc_ref, s_ref, cmax_ref = refs
    b = pl.program_id(0)
    pair = pl.program_id(1)
    i = pl.program_id(2)
    m_ref[...] = jnp.full(m_ref.shape, NEG_INF, F32)
    acc_ref[...] = jnp.zeros(acc_ref.shape, F32)
    qt = qt_ref[0]
    if slab:
        qth = [qt[:MLA_SLAB], qt[MLA_SLAB:]]
    else:
        rowi = lax.broadcasted_iota(jnp.int32, qt.shape, 0)
        zero = jnp.zeros_like(qt)
        qth = [jnp.where(rowi < HEAD_DIM, qt, zero), jnp.where(rowi >= HEAD_DIM, qt, zero)]
    if fox:
        lane = lax.broadcasted_iota(jnp.int32, (1, LANES), 1)
        cq = [cq_ref[0, pl.ds(2 * pair + h, 1), :] for h in range(2)]

    def logits_to(slot, j):
        off = pl.multiple_of(j * tk, tk)
        kc = k_ref[0, pl.ds(off, tk), :]
        for h in range(2):
            kh = kc[:, h * MLA_SLAB:(h + 1) * MLA_SLAB] if slab else kc
            s = jnp.dot(kh, qth[h], preferred_element_type=F32)
            if fox:
                ck_blk = ck_ref[0, pl.ds(off, tk), :]
                ck = jnp.sum(jnp.where(lane == 2 * pair + h, ck_blk, 0.0), axis=1, keepdims=True)
                s = s + cq[h] - ck
            s_ref[slot, h] = s
            cmax_ref[slot, h] = jnp.max(s, axis=0, keepdims=True)

    def softmax_pv(slot, j, causal_mask):
        off = pl.multiple_of(j * tk, tk)
        for h in range(2):
            s = s_ref[slot, h]
            if causal_mask is not None:
                s = jnp.where(causal_mask, s, NEG_INF)
                chunk_max = jnp.max(s, axis=0, keepdims=True)
            else:
                chunk_max = cmax_ref[slot, h]
            m_prev = m_ref[h]
            m_new = jnp.maximum(m_prev, chunk_max)
            alpha = jnp.exp2(m_prev - m_new)
            p = jnp.exp2(s - m_new).astype(BF16)
            vth = vt_ref[0, h * VT_ROWS:(h + 1) * VT_ROWS, pl.ds(off, tk)]
            acc_ref[h] = acc_ref[h] * alpha + jnp.dot(vth, p, preferred_element_type=F32)
            m_ref[h] = m_new

    per_tile = tq // tk
    d0 = per_tile * i
    if fox:
        base = b * n_chunks
        qn = []
        for h in range(2):
            qf = qth[h].astype(F32)
            qn.append(jnp.sqrt(jnp.max(jnp.sum(qf * qf, axis=0, keepdims=True))) * NORM_SLACK)

        top = [2.0 * qn[h] * kn_ref[base + d0 + per_tile - 1, 2 * pair + h]
               + cf_ref[base + d0, 2 * pair + h] for h in range(2)]

        def live(j):
            ub = [top[h] - cl_ref[base + jnp.maximum(j, 0), 2 * pair + h] for h in range(2)]
            return (j >= 0) & (jnp.maximum(ub[0], ub[1]) > SKIP_LOG2)

        j0 = lax.while_loop(live, lambda j: j - 1, d0 - 1) + 1
    else:
        j0 = 0
    n_before = d0 - j0

    def causal_mask(shift):
        key = lax.broadcasted_iota(jnp.int32, (tk, tq), 0)
        qry = lax.broadcasted_iota(jnp.int32, (tk, tq), 1)
        return key + shift <= qry

    def finish():
        if per_tile == 2:
            logits_to(1, d0 + 1)
        softmax_pv(0, d0, causal_mask(0))
        if per_tile == 2:
            softmax_pv(1, d0 + 1, causal_mask(tk))

    def pipelined():
        odd = (n_before % 2) == 1

        @pl.when(odd)
        def _():
            logits_to(1, j0)
            logits_to(0, j0 + 1)
            softmax_pv(1, j0, None)

        @pl.when(jnp.logical_not(odd))
        def _():
            logits_to(0, j0)

        def two_steps(j):
            logits_to(1, j + 1)
            softmax_pv(0, j, None)
            logits_to(0, j + 2)
            softmax_pv(1, j + 1, None)

        j_even = j0 + (n_before % 2)
        n_two = n_before // 2
        if long_body:
            @pl.when((n_two % 2) == 1)
            def _():
                two_steps(j_even)

            j_quad = j_even + 2 * (n_two % 2)

            def body(t, carry):
                two_steps(j_quad + 4 * t)
                two_steps(j_quad + 4 * t + 2)
                return carry

            lax.fori_loop(0, n_two // 2, body, 0)
        else:
            def body(t, carry):
                two_steps(j_even + 2 * t)
                return carry

            lax.fori_loop(0, n_two, body, 0)
        finish()

    if fox:
        @pl.when(n_before == 1)
        def _():
            logits_to(1, j0)
            logits_to(0, d0)
            softmax_pv(1, j0, None)
            finish()

        @pl.when(n_before != 1)
        def _():
            pipelined()
    else:
        pipelined()
    out_t = jnp.concatenate(
        [acc_ref[h, :HEAD_DIM] * (1.0 / acc_ref[h, HEAD_DIM:HEAD_DIM + 1]) for h in range(2)],
        axis=0)
    o_ref[0] = out_t.T.astype(o_ref.dtype)


def _causal_attention(qt, k, vt, fox_args, *, slab):
    nb, seq, _ = k.shape
    fox = fox_args is not None
    tq = ATT_TQ
    tk = FOX_TK if fox else ATT_TQ
    n_tiles = seq // tq
    n_pairs = FOX_HEADS // 2
    rows = 2 * MLA_SLAB if slab else LANES
    in_specs = [pl.BlockSpec((1, rows, tq), lambda b, p, i: (b, p, i)),
                pl.BlockSpec((1, seq, rows), lambda b, p, i: (b, 0, p)),
                pl.BlockSpec((1, 2 * VT_ROWS, seq), lambda b, p, i: (b, p, 0))]
    args = [qt, k, vt]
    if fox:
        cum, cumt, cfirst, clast, knorm = fox_args
        smem = pl.BlockSpec(memory_space=pltpu.SMEM)
        in_specs = [smem, smem, smem] + in_specs + [
            pl.BlockSpec((1, 16, tq), lambda b, p, i: (b, 0, i)),
            pl.BlockSpec((1, seq, LANES), lambda b, p, i: (b, 0, 0))]
        args = [cfirst, clast, knorm] + args + [cumt, cum]
    return pl.pallas_call(
        functools.partial(_causal_attn_kernel, tq=tq, tk=tk, n_chunks=seq // tk, fox=fox, slab=slab,
                          long_body=not fox),
        grid=(nb, n_pairs, n_tiles),
        in_specs=in_specs,
        out_specs=pl.BlockSpec((1, tq, LANES), lambda b, p, i: (b, i, p)),
        out_shape=jax.ShapeDtypeStruct((nb, seq, n_pairs * LANES), BF16),
        scratch_shapes=[pltpu.VMEM((2, 1, tq), F32), pltpu.VMEM((2, VT_ROWS, tq), F32),
                        pltpu.VMEM((2, 2, tk, tq), F32), pltpu.VMEM((2, 2, 1, tq), F32)],
        compiler_params=_cparams(3),
        name="fox_attention" if fox else "mla_attention",
    )(*args)


def _mem_proj_kernel(mem_ref, w_ref, mk_ref, mvt_ref):
    mkv = jnp.dot(mem_ref[...].astype(BF16), w_ref[...], preferred_element_type=F32)
    mk_ref[0] = mkv[:, :MEM_WIDTH].astype(BF16)
    _store_values_transposed(mvt_ref, mkv[:, MEM_WIDTH:])


def _mem_proj(mem2d, w, nb):
    return pl.pallas_call(
        _mem_proj_kernel,
        grid=(nb,),
        in_specs=[pl.BlockSpec((N_MEM, D_MODEL), lambda b: (b, 0)),
                  pl.BlockSpec(w.shape, lambda b: (0, 0))],
        out_specs=[pl.BlockSpec((1, N_MEM, MEM_WIDTH), lambda b: (b, 0, 0)),
                   pl.BlockSpec((1, MEM_HEADS * VT_ROWS, N_MEM), lambda b: (b, 0, 0))],
        out_shape=[jax.ShapeDtypeStruct((nb, N_MEM, MEM_WIDTH), BF16),
                   jax.ShapeDtypeStruct((nb, MEM_HEADS * VT_ROWS, N_MEM), BF16)],
        compiler_params=_cparams(1),
        name="mem_proj",
    )(mem2d, w)


def _mem_attn_kernel(qt_ref, k_ref, vt_ref, o_ref):
    qt = qt_ref[0]
    kc = k_ref[0]
    for pair in range(MEM_HEADS // 2):
        qt_p = qt[pair * LANES:(pair + 1) * LANES]
        k_p = kc[:, pair * LANES:(pair + 1) * LANES]
        rowi = lax.broadcasted_iota(jnp.int32, qt_p.shape, 0)
        zero = jnp.zeros_like(qt_p)
        outs = []
        for h in range(2):
            mine = (rowi < HEAD_DIM) if h == 0 else (rowi >= HEAD_DIM)
            s = jnp.dot(k_p, jnp.where(mine, qt_p, zero), preferred_element_type=F32)
            p = jnp.exp2(s - jnp.max(s, axis=0, keepdims=True)).astype(BF16)
            r0 = (2 * pair + h) * VT_ROWS
            acc = jnp.dot(vt_ref[0, r0:r0 + VT_ROWS, :], p, preferred_element_type=F32)
            outs.append(acc[:HEAD_DIM] * (1.0 / acc[HEAD_DIM:HEAD_DIM + 1]))
        o_ref[0, :, pair * LANES:(pair + 1) * LANES] = (
            jnp.concatenate(outs, axis=0).T.astype(o_ref.dtype))


def _memory_attention(mqt, mk, mvt):
    nb, _, seq = mqt.shape
    tq = ATT_TQ
    return pl.pallas_call(
        _mem_attn_kernel,
        grid=(nb, seq // tq),
        in_specs=[pl.BlockSpec((1, MEM_WIDTH, tq), lambda b, i: (b, 0, i)),
                  pl.BlockSpec((1, N_MEM, MEM_WIDTH), lambda b, i: (b, 0, 0)),
                  pl.BlockSpec((1, MEM_HEADS * VT_ROWS, N_MEM), lambda b, i: (b, 0, 0))],
        out_specs=pl.BlockSpec((1, tq, MEM_WIDTH), lambda b, i: (b, i, 0)),
        out_shape=jax.ShapeDtypeStruct((nb, seq, MEM_WIDTH), BF16),
        compiler_params=_cparams(2),
        name="memory_attention",
    )(mqt, mk, mvt)


def _post_attn_kernel(x_ref, att_ref, memo_ref, wa_ref, wm_ref, g_ref, b_ref, wr_ref, br_ref,
                      x1_ref, x1p_ref, idx_ref, rank_ref, gate_ref, cnt_ref, carry_ref):
    t = pl.program_id(0)

    @pl.when(t == 0)
    def _():
        carry_ref[...] = jnp.zeros_like(carry_ref)

    tm = x_ref.shape[0]
    mix = (jnp.dot(att_ref[...], wa_ref[...], preferred_element_type=F32)
           + jnp.dot(memo_ref[...], wm_ref[...], preferred_element_type=F32))
    x1 = _layer_norm(DEEPNORM_ALPHA * x_ref[...] + mix, g_ref[...], b_ref[...])
    x1_ref[...] = x1
    x1b = x1.astype(BF16)
    x1p_ref[...] = _pack_bf16_pairs(x1b)
    logits = jnp.dot(x1b, wr_ref[...], preferred_element_type=F32) + br_ref[...]
    work = logits.T[:N_EXPERTS]
    row_f = lax.broadcasted_iota(jnp.int32, work.shape, 0).astype(F32)
    idxs, vals = [], []
    onehot_t = jnp.zeros(work.shape, F32)
    for _ in range(TOP_K):
        best = jnp.max(work, axis=0, keepdims=True)
        where_best = jnp.min(jnp.where(work == best, row_f, float(N_EXPERTS)), axis=0, keepdims=True)
        hit = row_f == where_best
        onehot_t = jnp.where(hit, 1.0, onehot_t)
        work = jnp.where(hit, -jnp.inf, work)
        idxs.append(where_best)
        vals.append(best)
    exps = [jnp.exp(v - vals[0]) for v in vals]
    denom = exps[0] + exps[1] + exps[2] + exps[3]
    src = lax.broadcasted_iota(jnp.int32, (tm, tm), 0)
    dst = lax.broadcasted_iota(jnp.int32, (tm, tm), 1)
    earlier = jnp.where(src < dst, 1.0, 0.0).astype(BF16)
    before_t = (jnp.dot(onehot_t.astype(BF16), earlier, preferred_element_type=F32)
                + carry_ref[...])
    ranks = [jnp.sum(jnp.where(row_f == idxs[r], before_t, 0.0), axis=0, keepdims=True)
             for r in range(TOP_K)]
    pad = jnp.zeros((8 - TOP_K, tm), F32)
    idx_ref[...] = jnp.concatenate(idxs + [pad], axis=0).astype(jnp.int32)
    rank_ref[...] = jnp.concatenate(ranks + [pad], axis=0).astype(jnp.int32)
    gates_t = jnp.concatenate([e / denom for e in exps] + [jnp.zeros((LANES - TOP_K, tm), F32)], axis=0)
    gate_ref[...] = gates_t.T
    total = carry_ref[...] + jnp.sum(onehot_t, axis=1, keepdims=True)
    carry_ref[...] = total
    cnt_ref[...] = jnp.broadcast_to(total, cnt_ref.shape)


def _post_attn(x2d, att, memo, wa, wm, g, b, wr, br):
    t_tokens = x2d.shape[0]
    tm = PROJ_TM
    row_spec = lambda width: pl.BlockSpec((tm, width), lambda i: (i, 0))
    full = lambda a: pl.BlockSpec(a.shape, lambda i: (0, 0))
    return pl.pallas_call(
        _post_attn_kernel,
        grid=(t_tokens // tm,),
        in_specs=[row_spec(D_MODEL), row_spec(att.shape[1]), row_spec(MEM_WIDTH),
                  full(wa), full(wm), full(g), full(b), full(wr), full(br)],
        out_specs=[row_spec(D_MODEL), row_spec(D_MODEL // 2),
                   pl.BlockSpec((8, tm), lambda i: (0, i)), pl.BlockSpec((8, tm), lambda i: (0, i)),
                   row_spec(LANES), pl.BlockSpec((N_EXPERTS, LANES), lambda i: (0, 0))],
        out_shape=[jax.ShapeDtypeStruct((t_tokens, D_MODEL), F32),
                   jax.ShapeDtypeStruct((t_tokens, D_MODEL // 2), jnp.uint32),
                   jax.ShapeDtypeStruct((8, t_tokens), jnp.int32),
                   jax.ShapeDtypeStruct((8, t_tokens), jnp.int32),
                   jax.ShapeDtypeStruct((t_tokens, LANES), F32),
                   jax.ShapeDtypeStruct((N_EXPERTS, LANES), F32)],
        scratch_shapes=[pltpu.VMEM((N_EXPERTS, 1), F32)],
        compiler_params=_cparams(1),
        name="outproj_ln_router",
    )(x2d, att, memo, wa, wm, g, b, wr, br)


def _slot_kernel(pstart_ref, idx_ref, rank_ref, o_ref):
    idx = idx_ref[...]
    dest = rank_ref[...]
    for e in range(N_EXPERTS):
        dest = dest + jnp.where(idx == e, pstart_ref[e], 0)
    o_ref[...] = dest


def _slot_table(idx, rank, pstart):
    return pl.pallas_call(
        _slot_kernel,
        in_specs=[pl.BlockSpec(memory_space=pltpu.SMEM), pl.BlockSpec(memory_space=pltpu.VMEM),
                  pl.BlockSpec(memory_space=pltpu.VMEM)],
        out_specs=pl.BlockSpec(memory_space=pltpu.VMEM),
        out_shape=jax.ShapeDtypeStruct(idx.shape, jnp.int32),
        name="slot_table",
    )(pstart, idx, rank)


def _expert_kernel(be_ref, nu_ref, br_ref, x_ref, wgu_hbm, bgu_ref, wdn_hbm, bdn_ref, *rest,
                   n_chunks, blk_lo, n_blocks, layer, has_prev):
    o_ref, wgu_st, wdn_st, wgu_bf, wdn_bf, h_ref, sem = rest[1:] if has_prev else rest
    step = pl.program_id(0)
    blk = step + blk_lo
    live_end = jnp.minimum(blk_lo + pl.num_programs(0), nu_ref[0])
    expert = be_ref[blk]

    def fetch(e):
        return (pltpu.make_async_copy(wgu_hbm.at[layer, e], wgu_st, sem.at[0]),
                pltpu.make_async_copy(wdn_hbm.at[layer, e], wdn_st, sem.at[1]))

    @pl.when(((step == 0) | (expert != be_ref[jnp.maximum(blk - 1, 0)])) & (blk < live_end))
    def _():
        @pl.when(step == 0)
        def _():
            for copy in fetch(expert):
                copy.start()

        for copy in fetch(expert):
            copy.wait()
        wgu_bf[...] = wgu_st[...].astype(BF16)
        wdn_bf[...] = wdn_st[...].astype(BF16)
        nxt = lax.while_loop(
            lambda j: (j < live_end) & (be_ref[jnp.minimum(j, n_blocks - 1)] == expert),
            lambda j: j + 1, blk + 1)

        @pl.when(nxt < live_end)
        def _():
            for copy in fetch(be_ref[jnp.minimum(nxt, n_blocks - 1)]):
                copy.start()

    def mlp(rows):
        x = _unpack_bf16_pairs(x_ref[:rows]).astype(BF16)
        cw = D_EXPERT // n_chunks
        for c in range(n_chunks):
            gs = slice(c * cw, (c + 1) * cw)
            us = slice(D_EXPERT + c * cw, D_EXPERT + (c + 1) * cw)
            g = jnp.dot(x, wgu_bf[:, gs], preferred_element_type=F32) + bgu_ref[0, 0, :, gs]
            u = jnp.dot(x, wgu_bf[:, us], preferred_element_type=F32) + bgu_ref[0, 0, :, us]
            g = jnp.minimum(g, SWIGLU_LIMIT)
            u = jnp.clip(u, -SWIGLU_LIMIT, SWIGLU_LIMIT)
            h_ref[:rows, gs] = ((u + 1.0) * (g * jax.nn.sigmoid(SWIGLU_ALPHA * g))).astype(BF16)
        y = jnp.dot(h_ref[:rows], wdn_bf[...], preferred_element_type=F32) + bdn_ref[0, 0]
        return _pack_bf16_pairs(y.astype(BF16))

    tm = o_ref.shape[0]
    live_rows = br_ref[blk]

    @pl.when(live_rows > tm // 2)
    def _():
        o_ref[...] = mlp(tm)

    @pl.when((live_rows > 0) & (live_rows <= tm // 2))
    def _():
        o_ref[:tm // 2] = mlp(tm // 2)
        o_ref[tm // 2:] = jnp.zeros((tm - tm // 2, o_ref.shape[1]), o_ref.dtype)

    @pl.when(live_rows == 0)
    def _():
        o_ref[...] = jnp.zeros_like(o_ref)


def _experts(block_expert, n_used, block_rows, xb, blk_lo, p_rows, y_prev, layer, wgu, bgu, wdn, bdn):
    tm = MOE_TM
    has_prev = y_prev is not None
    b_map = lambda i, be, nu, br: (layer, be[i + blk_lo], 0, 0)
    in_specs = [pl.BlockSpec((tm, D_MODEL // 2), lambda i, be, nu, br: (i, 0)),
                pl.BlockSpec(memory_space=pl.ANY),
                pl.BlockSpec((1, 1, 1, 2 * D_EXPERT), b_map),
                pl.BlockSpec(memory_space=pl.ANY),
                pl.BlockSpec((1, 1, 1, D_MODEL), b_map)]
    args = [block_expert, n_used, block_rows, xb, wgu, bgu, wdn, bdn]
    if has_prev:
        in_specs.append(pl.BlockSpec(memory_space=pl.ANY))
        args.append(y_prev)
    grid_spec = pltpu.PrefetchScalarGridSpec(
        num_scalar_prefetch=3,
        grid=(xb.shape[0] // tm,),
        in_specs=in_specs,
        out_specs=pl.BlockSpec((tm, D_MODEL // 2), lambda i, be, nu, br: (i + blk_lo, 0)),
        scratch_shapes=[pltpu.VMEM((D_MODEL, 2 * D_EXPERT), F32),
                        pltpu.VMEM((D_EXPERT, D_MODEL), F32),
                        pltpu.VMEM((D_MODEL, 2 * D_EXPERT), BF16),
                        pltpu.VMEM((D_EXPERT, D_MODEL), BF16),
                        pltpu.VMEM((tm, D_EXPERT), BF16),
                        pltpu.SemaphoreType.DMA((2,))],
    )
    return pl.pallas_call(
        functools.partial(_expert_kernel, n_chunks=4, blk_lo=blk_lo, n_blocks=p_rows // tm,
                          layer=layer, has_prev=has_prev),
        grid_spec=grid_spec,
        out_shape=jax.ShapeDtypeStruct((p_rows, D_MODEL // 2), jnp.uint32),
        input_output_aliases={len(args) - 1: 0} if has_prev else {},
        compiler_params=_cparams(1),
        name="experts",
    )(*args)


def _sc_move_rows(table_hbm, idx_v, out_hbm, base, n_chunks, rows_v, gsem, wsem):
    def gather(c, slot):
        off = pl.multiple_of(c * SC_CHUNK, SC_CHUNK)
        return pltpu.make_async_copy(table_hbm.at[idx_v.at[pl.ds(off, SC_CHUNK)]],
                                     rows_v.at[slot], gsem.at[slot])

    def put(c, slot):
        off = pl.multiple_of(c * SC_CHUNK, SC_CHUNK)
        return pltpu.make_async_copy(rows_v.at[slot], out_hbm.at[pl.ds(base + off, SC_CHUNK)],
                                     wsem.at[slot])

    gather(0, 0).start()
    gather(1, 1).start()

    @pl.loop(0, n_chunks, step=2)
    def _(c):
        for slot in range(2):
            gather(c + slot, slot).wait()
            put(c + slot, slot).start()
        for slot in range(2):
            put(c + slot, slot).wait()

            @pl.when(c + 2 + slot < n_chunks)
            def _():
                gather(c + 2 + slot, slot).start()


def _sc_gather_rows(table, idx):
    n_idx = idx.shape[0]
    width = table.shape[1]
    per_worker = n_idx // SC_WORKERS
    n_chunks = per_worker // SC_CHUNK
    assert per_worker * SC_WORKERS == n_idx and n_chunks * SC_CHUNK == per_worker and n_chunks % 2 == 0
    mesh = plsc.VectorSubcoreMesh(core_axis_name="c", subcore_axis_name="s",
                                  num_cores=SC_CORES, num_subcores=SC_SUBCORES)

    def body(table_hbm, idx_hbm, out_hbm, idx_v, rows_v, gsem, wsem):
        wid = lax.axis_index("s") * SC_CORES + lax.axis_index("c")
        base = wid * per_worker
        pltpu.sync_copy(idx_hbm.at[pl.ds(base, per_worker)], idx_v)
        _sc_move_rows(table_hbm, idx_v, out_hbm, base, n_chunks, rows_v, gsem, wsem)

    return pl.kernel(
        body,
        out_type=jax.ShapeDtypeStruct((n_idx, width), table.dtype),
        mesh=mesh,
        scratch_types=[pltpu.VMEM((per_worker,), jnp.int32),
                       pltpu.VMEM((2, SC_CHUNK, width), table.dtype),
                       pltpu.SemaphoreType.DMA((2,)),
                       pltpu.SemaphoreType.DMA((2,))],
        name="sc_gather_rows",
    )(table, idx)


def _sc_dispatch_rows(table, dest, slot_lo, n_slots):
    n_tok, width = table.shape
    n_pairs = dest.shape[0]
    per_worker = n_slots // SC_WORKERS
    n_chunks = per_worker // SC_CHUNK
    assert per_worker * SC_WORKERS == n_slots and n_chunks * SC_CHUNK == per_worker and n_chunks % 2 == 0
    n_scan = n_pairs // SC_SCAN
    assert n_scan * SC_SCAN == n_pairs
    mesh = plsc.VectorSubcoreMesh(core_axis_name="c", subcore_axis_name="s",
                                  num_cores=SC_CORES, num_subcores=SC_SUBCORES)

    def body(table_hbm, dest_hbm, out_hbm, tok_v, dest_v, rows_v, gsem, wsem):
        wid = lax.axis_index("s") * SC_CORES + lax.axis_index("c")
        base = wid * per_worker
        first = slot_lo + base
        lane = lax.iota(jnp.int32, SC_LANES)

        @pl.loop(0, per_worker // SC_LANES)
        def _(i):
            off = pl.multiple_of(i * SC_LANES, SC_LANES)
            tok_v[pl.ds(off, SC_LANES)] = lax.rem(first + off + lane, n_tok)

        @pl.loop(0, n_scan)
        def _(g):
            goff = pl.multiple_of(g * SC_SCAN, SC_SCAN)
            pltpu.sync_copy(dest_hbm.at[pl.ds(goff, SC_SCAN)], dest_v)

            @plsc.parallel_loop(0, SC_SCAN // SC_LANES, unroll=8)
            def _(i):
                off = pl.multiple_of(i * SC_LANES, SC_LANES)
                local = dest_v[pl.ds(off, SC_LANES)] - first
                mine = (local >= 0) & (local < per_worker)
                pair = goff + off + lane
                plsc.store_scatter(tok_v, [jnp.where(mine, local, 0)], lax.rem(pair, n_tok), mask=mine)

        _sc_move_rows(table_hbm, tok_v, out_hbm, base, n_chunks, rows_v, gsem, wsem)

    return pl.kernel(
        body,
        out_type=jax.ShapeDtypeStruct((n_slots, width), table.dtype),
        mesh=mesh,
        scratch_types=[pltpu.VMEM((per_worker,), jnp.int32),
                       pltpu.VMEM((SC_SCAN,), jnp.int32),
                       pltpu.VMEM((2, SC_CHUNK, width), table.dtype),
                       pltpu.SemaphoreType.DMA((2,)),
                       pltpu.SemaphoreType.DMA((2,))],
        compiler_params=pltpu.CompilerParams(needs_layout_passes=False),
        name="sc_dispatch_rows",
    )(table, dest)


def _combine_kernel(x1_ref, yg_ref, gate_ref, g_ref, b_ref, *rest):
    o_ref = rest[-1]
    gates = gate_ref[...]
    ffn = _unpack_bf16_pairs(yg_ref[0]) * gates[:, 0:1]
    for r in range(1, TOP_K):
        ffn = ffn + _unpack_bf16_pairs(yg_ref[r]) * gates[:, r:r + 1]
    o_ref[...] = _layer_norm(DEEPNORM_ALPHA * x1_ref[...] + ffn, g_ref[...], b_ref[...])


def _combine(x1, yg, gates, g, b, tok_lo, out_prev):
    t_tokens = x1.shape[0]
    tm = COMBINE_TM
    tile_lo = tok_lo // tm
    in_specs = [pl.BlockSpec((tm, D_MODEL), lambda i: (i + tile_lo, 0)),
                pl.BlockSpec((TOP_K, tm, D_MODEL // 2), lambda i: (0, i, 0)),
                pl.BlockSpec((tm, LANES), lambda i: (i + tile_lo, 0)),
                pl.BlockSpec((1, D_MODEL), lambda i: (0, 0)),
                pl.BlockSpec((1, D_MODEL), lambda i: (0, 0))]
    args = [x1, yg, gates, g, b]
    if out_prev is not None:
        in_specs.append(pl.BlockSpec(memory_space=pl.ANY))
        args.append(out_prev)
    return pl.pallas_call(
        _combine_kernel,
        grid=(yg.shape[1] // tm,),
        in_specs=in_specs,
        out_specs=pl.BlockSpec((tm, D_MODEL), lambda i: (i + tile_lo, 0)),
        out_shape=jax.ShapeDtypeStruct((t_tokens, D_MODEL), F32),
        input_output_aliases={len(args) - 1: 0} if out_prev is not None else {},
        compiler_params=_cparams(1),
        name="combine_ln",
    )(*args)


def _moe(x1, x1p, idx, rank, gates, counts, layer, wgu, bgu, wdn, bdn, g, b):
    t_tokens = x1.shape[0]
    tm = MOE_TM
    counts = counts[:, 0].astype(jnp.int32)
    padded = ((counts + tm - 1) // tm) * tm
    pend = jnp.cumsum(padded)
    pstart = pend - padded
    dest = _slot_table(idx, rank, pstart)[:TOP_K]
    n_blocks = (t_tokens * TOP_K) // tm + N_EXPERTS
    p_rows = n_blocks * tm
    block_start = jnp.arange(n_blocks, dtype=jnp.int32) * tm
    block_expert = jnp.minimum(
        jnp.sum((pend[None, :] <= block_start[:, None]).astype(jnp.int32), axis=1), N_EXPERTS - 1)
    n_used = (pend[-1:] // tm).astype(jnp.int32)
    mine = block_expert[:, None] == jnp.arange(N_EXPERTS, dtype=jnp.int32)[None, :]
    live_end = jnp.sum(jnp.where(mine, (pstart + counts)[None, :], 0), axis=1)
    block_rows = jnp.clip(live_end - block_start, 0, tm)
    y = None
    for lo, hi in ((0, n_blocks // MOE_SPLIT), (n_blocks // MOE_SPLIT, n_blocks)):
        xb = _sc_dispatch_rows(x1p, dest.reshape(-1), lo * tm, (hi - lo) * tm)
        y = _experts(block_expert, n_used, block_rows, xb, lo, p_rows, y, layer, wgu, bgu, wdn, bdn)
    out = None
    for lo, hi in ((0, t_tokens // MOE_SPLIT), (t_tokens // MOE_SPLIT, t_tokens)):
        yg = _sc_gather_rows(y, dest[:, lo:hi].reshape(-1)).reshape(TOP_K, hi - lo, D_MODEL // 2)
        out = _combine(x1, yg, gates, g, b, lo, out)
    return out


def _row(v, width=None):
    v = v.astype(F32).reshape(1, -1)
    if width is not None and v.shape[1] < width:
        v = jnp.pad(v, ((0, 0), (0, width - v.shape[1])))
    return v


def _pad_cols(w, width):
    return jnp.pad(w, ((0, 0), (0, width - w.shape[1])))


def kernel(x, mem, positions, a_w_in, a_b_f, a_w_out, b_w_in, b_g_q, b_w_uq, b_w_out,
           kv_w_dkv, kv_g, kv_w_ukv, mem_w_kv, ln_g, ln_b,
           moe_w_r, moe_b_r, moe_w_gu, moe_b_gu, moe_w_dn, moe_b_dn):
    nb, seq, d = x.shape
    t_tokens = nb * seq
    n_a = a_w_in.shape[0]
    x2d = x.reshape(t_tokens, d)
    mem2d = mem.reshape(nb * N_MEM, d)
    pos2d = positions.reshape(1, t_tokens)
    half = QK_ROPE // 2
    invf = (ROPE_THETA ** (-jnp.arange(half, dtype=F32) * 2.0 / QK_ROPE)).reshape(half, 1)

    shared_kv = None
    for l in range(DEPTH):
        mk, mvt = _mem_proj(mem2d, mem_w_kv[l].astype(BF16), nb)
        if l < n_a:
            w_in = a_w_in[l]
            w = jnp.concatenate([w_in[:, :3 * FOX_WIDTH],
                                 _pad_cols(w_in[:, 3 * FOX_WIDTH:3 * FOX_WIDTH + FOX_HEADS], LANES),
                                 w_in[:, 3 * FOX_WIDTH + FOX_HEADS:]], axis=1).astype(BF16)
            qt, k, vt, mqt, cum, cumt, cfirst, clast, knorm = _fox_proj(
                x2d, w, _row(a_b_f[l], LANES), seq)
            stats = [s.reshape(-1, LANES) for s in (cfirst, clast, knorm)]
            att = _causal_attention(qt, k.reshape(nb, seq, -1), vt,
                                    [cum.reshape(nb, seq, LANES), cumt] + stats, slab=False)
            w_out = a_w_out[l]
        else:
            bl = l - n_a
            w = jnp.concatenate([b_w_in[bl], kv_w_dkv[:, :KV_LORA],
                                 jnp.zeros((d, QK_NOPE), F32), kv_w_dkv[:, KV_LORA:],
                                 jnp.zeros((d, LANES - QK_NOPE - QK_ROPE), F32)], axis=1).astype(BF16)
            wuq = b_w_uq[bl].reshape(Q_LORA, MLA_HEADS, QK_NOPE + QK_ROPE)
            wuq = jnp.pad(wuq, ((0, 0), (0, 0), (0, MLA_SLAB - QK_NOPE - QK_ROPE)))
            wuq = wuq.reshape(Q_LORA, MLA_HEADS * MLA_SLAB).astype(BF16)
            wukv = kv_w_ukv.reshape(KV_LORA, MLA_HEADS, QK_NOPE + V_DIM)
            wk = jnp.pad(wukv[:, :, :QK_NOPE], ((0, 0), (0, 0), (0, MLA_SLAB - QK_NOPE)))
            wk = wk.reshape(KV_LORA, MLA_HEADS * MLA_SLAB).astype(BF16)
            wv = wukv[:, :, QK_NOPE:].reshape(KV_LORA, MLA_V_WIDTH).astype(BF16)
            qt, mqt, k_new, vt_new = _mla_proj(x2d, pos2d, invf, w, _row(b_g_q[bl]), wuq,
                                              _row(kv_g), wk, wv, seq)
            if shared_kv is None:
                shared_kv = (k_new.reshape(nb, seq, -1), vt_new)
            att = _causal_attention(qt, shared_kv[0], shared_kv[1], None, slab=True)
            w_out = b_w_out[bl]
        memo = _memory_attention(mqt, mk, mvt)
        n_att = w_out.shape[0] - MEM_WIDTH
        x1, x1p, idx, rank, gates, counts = _post_attn(
            x2d, att.reshape(t_tokens, -1), memo.reshape(t_tokens, MEM_WIDTH),
            w_out[:n_att].astype(BF16), w_out[n_att:].astype(BF16),
            _row(ln_g[l, 0]), _row(ln_b[l, 0]),
            _pad_cols(moe_w_r[l], LANES).astype(BF16), _row(moe_b_r[l], LANES))
        x2d = _moe(x1, x1p, idx, rank, gates, counts, l,
                   moe_w_gu, moe_b_gu.reshape(DEPTH, N_EXPERTS, 1, -1),
                   moe_w_dn, moe_b_dn.reshape(DEPTH, N_EXPERTS, 1, -1),
                   _row(ln_g[l, 1]), _row(ln_b[l, 1]))
    return x2d.reshape(nb, seq, d)
```

```python
import functools
import math

import jax
import jax.numpy as jnp
from jax import lax
from jax.experimental import pallas as pl
from jax.experimental.pallas import tpu as pltpu
from jax.experimental.pallas import tpu_sc as plsc

F32 = jnp.float32
BF16 = jnp.bfloat16

D_MODEL = 1024
DEPTH = 2
N_MEM = 256
HEAD_DIM = 64
FOX_HEADS = 12
MEM_HEADS = 4
MLA_HEADS = 12
Q_LORA = 384
KV_LORA = 256
QK_NOPE = 64
QK_ROPE = 32
V_DIM = 64
ROPE_THETA = 10000.0
N_EXPERTS = 32
TOP_K = 4
D_EXPERT = D_MODEL
SWIGLU_LIMIT = 7.0
SWIGLU_ALPHA = 1.702
LN_EPS = 1e-5
RMS_EPS = 1e-6
NEG_INF = -1e30
DEEPNORM_ALPHA = (2 * DEPTH) ** 0.25
FOX_WIDTH = FOX_HEADS * HEAD_DIM
MEM_WIDTH = MEM_HEADS * HEAD_DIM
MLA_V_WIDTH = MLA_HEADS * V_DIM

LANES = 128
LOG2E = math.log2(math.e)
VMEM_LIMIT = 48 * 1024 * 1024

PROJ_TM = 512
ATT_TQ = 512
FOX_TK = 256
MOE_TM = 512
COMBINE_TM = 1024
MOE_SPLIT = 4
MLA_SLAB = LANES
VT_ROWS = HEAD_DIM + 16
SC_CORES = 2
SC_SUBCORES = 16
SC_WORKERS = SC_CORES * SC_SUBCORES
SC_CHUNK = 32
SC_LANES = 16
SC_SCAN = 8192


def _cparams(n_axes):
    return pltpu.CompilerParams(dimension_semantics=("arbitrary",) * n_axes,
                                vmem_limit_bytes=VMEM_LIMIT)


def _split3(x):
    hi = x.astype(BF16)
    r1 = x - hi.astype(F32)
    mid = r1.astype(BF16)
    lo = (r1 - mid.astype(F32)).astype(BF16)
    return hi, mid, lo


def _pack_bf16_pairs(v):
    bits = pltpu.bitcast(v.astype(F32), jnp.uint32)
    half = v.shape[1] // 2
    return (bits[:, :half] >> 16) | bits[:, half:]


def _unpack_bf16_pairs(words):
    return jnp.concatenate([pltpu.bitcast(words << 16, F32),
                            pltpu.bitcast(words & jnp.uint32(0xFFFF0000), F32)], axis=1)


def _layer_norm(y, g, b):
    mu = jnp.mean(y, axis=-1, keepdims=True)
    yc = y - mu
    var = jnp.mean(yc * yc, axis=-1, keepdims=True)
    return yc * lax.rsqrt(var + LN_EPS) * g + b


def _rms_norm(y, g):
    return y * lax.rsqrt(jnp.mean(y * y, axis=-1, keepdims=True) + RMS_EPS) * g


def _store_transposed(dst_ref, val):
    for s in range(val.shape[1] // LANES):
        sl = slice(s * LANES, (s + 1) * LANES)
        dst_ref[0, sl, :] = val[:, sl].T.astype(dst_ref.dtype)


def _store_values_transposed(vt_ref, val):
    tm = val.shape[0]
    ones = jnp.ones((VT_ROWS - HEAD_DIM, tm), vt_ref.dtype)
    for s in range(val.shape[1] // LANES):
        pair_t = val[:, s * LANES:(s + 1) * LANES].T.astype(vt_ref.dtype)
        for h in range(2):
            r0 = (2 * s + h) * VT_ROWS
            vt_ref[0, r0:r0 + HEAD_DIM, :] = pair_t[h * HEAD_DIM:(h + 1) * HEAD_DIM]
            vt_ref[0, r0 + HEAD_DIM:r0 + VT_ROWS, :] = ones


def _fox_proj_kernel(x_ref, w_ref, bf_ref, seg_ref, qt_ref, k_ref, vt_ref, mqt_ref, cum_ref,
                     cumt_ref, cfirst_ref, clast_ref, knorm_ref, carry_ref, knmax_ref, *,
                     tiles_per_batch):
    t = pl.program_id(0)

    @pl.when(t % tiles_per_batch == 0)
    def _():
        carry_ref[...] = jnp.zeros_like(carry_ref)
        knmax_ref[...] = jnp.zeros_like(knmax_ref)

    tm = x_ref.shape[0]
    proj = jnp.dot(x_ref[...].astype(BF16), w_ref[...], preferred_element_type=F32)
    qscale = HEAD_DIM ** -0.5 * LOG2E
    _store_transposed(qt_ref, proj[:, :FOX_WIDTH] * qscale)
    kb = proj[:, FOX_WIDTH:2 * FOX_WIDTH].astype(BF16)
    k_ref[...] = kb
    _store_values_transposed(vt_ref, proj[:, 2 * FOX_WIDTH:3 * FOX_WIDTH])
    kf = kb.astype(F32)
    ksq = jnp.dot((kf * kf).astype(BF16), seg_ref[...], preferred_element_type=F32)
    n_sub = tm // FOX_TK
    for sub in range(n_sub):
        tile_max = jnp.sqrt(jnp.max(ksq[sub * FOX_TK:(sub + 1) * FOX_TK], axis=0, keepdims=True))
        knmax_ref[...] = jnp.maximum(knmax_ref[...], tile_max)
        knorm_ref[sub] = knmax_ref[...]
    f = proj[:, 3 * FOX_WIDTH:3 * FOX_WIDTH + LANES] + bf_ref[...]
    _store_transposed(mqt_ref, proj[:, 3 * FOX_WIDTH + LANES:] * qscale)
    log_f = jnp.minimum(f, 0.0) - jnp.log1p(jnp.exp(-jnp.abs(f)))
    row = lax.broadcasted_iota(jnp.int32, (tm, tm), 0)
    col = lax.broadcasted_iota(jnp.int32, (tm, tm), 1)
    tri = jnp.where(row >= col, 1.0, 0.0).astype(BF16)
    hi, mid, lo = _split3(log_f)
    cum = (jnp.dot(tri, hi, preferred_element_type=F32)
           + jnp.dot(tri, mid, preferred_element_type=F32)
           + jnp.dot(tri, lo, preferred_element_type=F32)) + carry_ref[...]
    carry_ref[...] = cum[tm - 1:tm, :]
    cum2 = cum * LOG2E
    cum_ref[...] = cum2
    cumt_ref[0] = cum2.T[:16, :]
    for sub in range(n_sub):
        cfirst_ref[sub] = cum2[sub * FOX_TK:sub * FOX_TK + 1, :]
        clast_ref[sub] = cum2[(sub + 1) * FOX_TK - 1:(sub + 1) * FOX_TK, :]


def _fox_proj(x2d, w, bf, seq):
    t_tokens = x2d.shape[0]
    tm = PROJ_TM
    nb = t_tokens // seq
    n = w.shape[1]
    tiles_per_batch = seq // tm
    n_tiles = t_tokens // tm
    seg = (jnp.arange(FOX_WIDTH)[:, None] // HEAD_DIM == jnp.arange(LANES)[None, :]).astype(BF16)
    row_spec = lambda width: pl.BlockSpec((tm, width), lambda i: (i, 0))
    t_spec = lambda rows: pl.BlockSpec(
        (1, rows, tm), lambda i: (i // tiles_per_batch, 0, i % tiles_per_batch))
    n_sub = tm // FOX_TK
    stat_spec = pl.BlockSpec((n_sub, 1, LANES), lambda i: (i, 0, 0))
    stat_shape = jax.ShapeDtypeStruct((n_tiles * n_sub, 1, LANES), F32)
    return pl.pallas_call(
        functools.partial(_fox_proj_kernel, tiles_per_batch=tiles_per_batch),
        grid=(n_tiles,),
        in_specs=[row_spec(D_MODEL),
                  pl.BlockSpec((D_MODEL, n), lambda i: (0, 0)),
                  pl.BlockSpec((1, LANES), lambda i: (0, 0)),
                  pl.BlockSpec((FOX_WIDTH, LANES), lambda i: (0, 0))],
        out_specs=[t_spec(FOX_WIDTH), row_spec(FOX_WIDTH), t_spec(FOX_HEADS * VT_ROWS),
                   t_spec(MEM_WIDTH), row_spec(LANES), t_spec(16),
                   stat_spec, stat_spec, stat_spec],
        out_shape=[jax.ShapeDtypeStruct((nb, FOX_WIDTH, seq), BF16),
                   jax.ShapeDtypeStruct((t_tokens, FOX_WIDTH), BF16),
                   jax.ShapeDtypeStruct((nb, FOX_HEADS * VT_ROWS, seq), BF16),
                   jax.ShapeDtypeStruct((nb, MEM_WIDTH, seq), BF16),
                   jax.ShapeDtypeStruct((t_tokens, LANES), F32),
                   jax.ShapeDtypeStruct((nb, 16, seq), F32),
                   stat_shape, stat_shape, stat_shape],
        scratch_shapes=[pltpu.VMEM((1, LANES), F32), pltpu.VMEM((1, LANES), F32)],
        compiler_params=_cparams(1),
        name="fox_proj",
    )(x2d, w, bf, seg)


def _rope_slab_t(slab_t, cos_t, sin_t):
    half = QK_ROPE // 2
    x1 = slab_t[QK_NOPE:QK_NOPE + half]
    x2 = slab_t[QK_NOPE + half:QK_NOPE + QK_ROPE]
    return jnp.concatenate([slab_t[:QK_NOPE], x1 * cos_t - x2 * sin_t, x1 * sin_t + x2 * cos_t,
                            slab_t[QK_NOPE + QK_ROPE:]], axis=0)


def _mla_proj_kernel(x_ref, pos_ref, invf_ref, w_ref, gq_ref, wuq_ref, gkv_ref, wk_ref, wv_ref,
                     qt_ref, mqt_ref, k_ref, vt_ref):
    proj = jnp.dot(x_ref[...].astype(BF16), w_ref[...], preferred_element_type=F32)
    ang_t = invf_ref[...] * pos_ref[...].astype(F32)
    cos_t = jnp.cos(ang_t)
    sin_t = jnp.sin(ang_t)
    c_q = _rms_norm(proj[:, :Q_LORA], gq_ref[...])
    _store_transposed(mqt_ref, proj[:, Q_LORA:Q_LORA + MEM_WIDTH] * (HEAD_DIM ** -0.5 * LOG2E))
    kv_off = Q_LORA + MEM_WIDTH
    c_kv = _rms_norm(proj[:, kv_off:kv_off + KV_LORA], gkv_ref[...])
    kr = _rope_slab_t(proj[:, kv_off + KV_LORA:].T, cos_t, sin_t).T
    q = jnp.dot(c_q.astype(BF16), wuq_ref[...], preferred_element_type=F32)
    kn = jnp.dot(c_kv.astype(BF16), wk_ref[...], preferred_element_type=F32)
    qscale = (QK_NOPE + QK_ROPE) ** -0.5 * LOG2E
    for h in range(MLA_HEADS):
        sl = slice(h * MLA_SLAB, (h + 1) * MLA_SLAB)
        qt_ref[0, sl, :] = (_rope_slab_t(q[:, sl].T, cos_t, sin_t) * qscale).astype(BF16)
        k_ref[:, sl] = (kn[:, sl] + kr).astype(BF16)
    _store_values_transposed(vt_ref, jnp.dot(c_kv.astype(BF16), wv_ref[...],
                                             preferred_element_type=F32))


def _mla_proj(x2d, pos2d, invf, w, gq, wuq, gkv, wk, wv, seq):
    t_tokens = x2d.shape[0]
    tm = PROJ_TM
    nb = t_tokens // seq
    tiles_per_batch = seq // tm
    row_spec = lambda width: pl.BlockSpec((tm, width), lambda i: (i, 0))
    t_spec = lambda rows: pl.BlockSpec(
        (1, rows, tm), lambda i: (i // tiles_per_batch, 0, i % tiles_per_batch))
    full = lambda a: pl.BlockSpec(a.shape, lambda i: (0, 0))
    slabs = MLA_HEADS * MLA_SLAB
    return pl.pallas_call(
        _mla_proj_kernel,
        grid=(t_tokens // tm,),
        in_specs=[row_spec(D_MODEL), pl.BlockSpec((1, tm), lambda i: (0, i)), full(invf), full(w),
                  full(gq), full(wuq), full(gkv), full(wk), full(wv)],
        out_specs=[t_spec(slabs), t_spec(MEM_WIDTH), row_spec(slabs),
                   t_spec(MLA_HEADS * VT_ROWS)],
        out_shape=[jax.ShapeDtypeStruct((nb, slabs, seq), BF16),
                   jax.ShapeDtypeStruct((nb, MEM_WIDTH, seq), BF16),
                   jax.ShapeDtypeStruct((t_tokens, slabs), BF16),
                   jax.ShapeDtypeStruct((nb, MLA_HEADS * VT_ROWS, seq), BF16)],
        compiler_params=_cparams(1),
        name="mla_proj",
    )(x2d, pos2d, invf, w, gq, wuq, gkv, wk, wv)


SKIP_LOG2 = -160.0
NORM_SLACK = 1.02


def _causal_attn_kernel(*refs, tq, tk, qw, n_chunks, fox, slab, long_body):
    if fox:
        (cf_ref, cl_ref, kn_ref, qt_ref, k_ref, vt_ref, cq_ref, ck_ref,
         o_ref, m_ref, acc_ref, s_ref, cmax_ref) = refs
    else:
        qt_ref, k_ref, vt_ref, o_ref, m_ref, acc_ref, s_ref, cmax_ref = refs
    b = pl.program_id(0)
    pair = pl.program_id(1)
    i = pl.program_id(2)
    m_ref[...] = jnp.full(m_ref.shape, NEG_INF, F32)
    acc_ref[...] = jnp.zeros(acc_ref.shape, F32)
    qt = qt_ref[0]
    if slab:
        qth = [qt[:MLA_SLAB], qt[MLA_SLAB:]]
    else:
        rowi = lax.broadcasted_iota(jnp.int32, qt.shape, 0)
        zero = jnp.zeros_like(qt)
        qth = [jnp.where(rowi < HEAD_DIM, qt, zero), jnp.where(rowi >= HEAD_DIM, qt, zero)]
    if fox:
        lane = lax.broadcasted_iota(jnp.int32, (1, LANES), 1)
        cq = [cq_ref[0, pl.ds(2 * pair + h, 1), :] for h in range(2)]

    def logits_to(slot, j):
        off = pl.multiple_of(j * tk, tk)
        kc = k_ref[0, pl.ds(off, tk), :]
        for h in range(2):
            kh = kc[:, h * MLA_SLAB:(h + 1) * MLA_SLAB] if slab else kc
            for q0 in range(0, tq, qw):
                qs = slice(q0, q0 + qw)
                s = jnp.dot(kh, qth[h][:, qs], preferred_element_type=F32)
                if fox:
                    ck_blk = ck_ref[0, pl.ds(off, tk), :]
                    ck = jnp.sum(jnp.where(lane == 2 * pair + h, ck_blk, 0.0), axis=1, keepdims=True)
                    s = s + cq[h][:, qs] - ck
                s_ref[slot, h, :, qs] = s
                cmax_ref[slot, h, :, qs] = jnp.max(s, axis=0, keepdims=True)

    def softmax_pv(slot, j, causal_mask):
        off = pl.multiple_of(j * tk, tk)
        for h in range(2):
            vth = vt_ref[0, h * VT_ROWS:(h + 1) * VT_ROWS, pl.ds(off, tk)]
            for q0 in range(0, tq, qw):
                qs = slice(q0, q0 + qw)
                s = s_ref[slot, h, :, qs]
                if causal_mask is not None:
                    s = jnp.where(causal_mask[:, qs], s, NEG_INF)
                    chunk_max = jnp.max(s, axis=0, keepdims=True)
                else:
                    chunk_max = cmax_ref[slot, h, :, qs]
                m_prev = m_ref[h, :, qs]
                m_new = jnp.maximum(m_prev, chunk_max)
                alpha = jnp.exp2(m_prev - m_new)
                p = jnp.exp2(s - m_new).astype(BF16)
                acc_ref[h, :, qs] = (acc_ref[h, :, qs] * alpha
                                     + jnp.dot(vth, p, preferred_element_type=F32))
                m_ref[h, :, qs] = m_new

    per_tile = tq // tk
    d0 = per_tile * i
    if fox:
        base = b * n_chunks
        qn = []
        for h in range(2):
            qf = qth[h].astype(F32)
            qn.append(jnp.sqrt(jnp.max(jnp.sum(qf * qf, axis=0, keepdims=True))) * NORM_SLACK)

        top = [2.0 * qn[h] * kn_ref[base + d0 + per_tile - 1, 2 * pair + h]
               + cf_ref[base + d0, 2 * pair + h] for h in range(2)]

        def live(j):
            ub = [top[h] - cl_ref[base + jnp.maximum(j, 0), 2 * pair + h] for h in range(2)]
            return (j >= 0) & (jnp.maximum(ub[0], ub[1]) > SKIP_LOG2)

        j0 = lax.while_loop(live, lambda j: j - 1, d0 - 1) + 1
    else:
        j0 = 0
    n_before = d0 - j0

    def causal_mask(shift):
        key = lax.broadcasted_iota(jnp.int32, (tk, tq), 0)
        qry = lax.broadcasted_iota(jnp.int32, (tk, tq), 1)
        return key + shift <= qry

    def finish():
        if per_tile == 2:
            logits_to(1, d0 + 1)
        softmax_pv(0, d0, causal_mask(0))
        if per_tile == 2:
            softmax_pv(1, d0 + 1, causal_mask(tk))

    def pipelined():
        odd = (n_before % 2) == 1

        @pl.when(odd)
        def _():
            logits_to(1, j0)
            logits_to(0, j0 + 1)
            softmax_pv(1, j0, None)

        @pl.when(jnp.logical_not(odd))
        def _():
            logits_to(0, j0)

        def two_steps(j):
            logits_to(1, j + 1)
            softmax_pv(0, j, None)
            logits_to(0, j + 2)
            softmax_pv(1, j + 1, None)

        j_even = j0 + (n_before % 2)
        n_two = n_before // 2
        if long_body:
            @pl.when((n_two % 2) == 1)
            def _():
                two_steps(j_even)

            j_quad = j_even + 2 * (n_two % 2)

            def body(t, carry):
                two_steps(j_quad + 4 * t)
                two_steps(j_quad + 4 * t + 2)
                return carry

            lax.fori_loop(0, n_two // 2, body, 0)
        else:
            def body(t, carry):
                two_steps(j_even + 2 * t)
                return carry

            lax.fori_loop(0, n_two, body, 0)
        finish()

    if fox:
        @pl.when(n_before == 1)
        def _():
            logits_to(1, j0)
            logits_to(0, d0)
            softmax_pv(1, j0, None)
            finish()

        @pl.when(n_before != 1)
        def _():
            pipelined()
    else:
        pipelined()
    out_t = jnp.concatenate(
        [acc_ref[h, :HEAD_DIM] * (1.0 / acc_ref[h, HEAD_DIM:HEAD_DIM + 1]) for h in range(2)],
        axis=0)
    o_ref[0] = out_t.T.astype(o_ref.dtype)


def _causal_attention(qt, k, vt, fox_args, *, slab):
    nb, seq, _ = k.shape
    fox = fox_args is not None
    tq = ATT_TQ
    tk = FOX_TK
    n_tiles = seq // tq
    n_pairs = FOX_HEADS // 2
    rows = 2 * MLA_SLAB if slab else LANES
    in_specs = [pl.BlockSpec((1, rows, tq), lambda b, p, i: (b, p, i)),
                pl.BlockSpec((1, seq, rows), lambda b, p, i: (b, 0, p)),
                pl.BlockSpec((1, 2 * VT_ROWS, seq), lambda b, p, i: (b, p, 0))]
    args = [qt, k, vt]
    if fox:
        cum, cumt, cfirst, clast, knorm = fox_args
        smem = pl.BlockSpec(memory_space=pltpu.SMEM)
        in_specs = [smem, smem, smem] + in_specs + [
            pl.BlockSpec((1, 16, tq), lambda b, p, i: (b, 0, i)),
            pl.BlockSpec((1, seq, LANES), lambda b, p, i: (b, 0, 0))]
        args = [cfirst, clast, knorm] + args + [cumt, cum]
    return pl.pallas_call(
        functools.partial(_causal_attn_kernel, tq=tq, tk=tk, qw=tq if fox else tq // 2,
                          n_chunks=seq // tk, fox=fox, slab=slab, long_body=not fox),
        grid=(nb, n_pairs, n_tiles),
        in_specs=in_specs,
        out_specs=pl.BlockSpec((1, tq, LANES), lambda b, p, i: (b, i, p)),
        out_shape=jax.ShapeDtypeStruct((nb, seq, n_pairs * LANES), BF16),
        scratch_shapes=[pltpu.VMEM((2, 1, tq), F32), pltpu.VMEM((2, VT_ROWS, tq), F32),
                        pltpu.VMEM((2, 2, tk, tq), F32), pltpu.VMEM((2, 2, 1, tq), F32)],
        compiler_params=_cparams(3),
        name="fox_attention" if fox else "mla_attention",
    )(*args)


def _mem_proj_kernel(mem_ref, w_ref, mk_ref, mvt_ref):
    mkv = jnp.dot(mem_ref[...].astype(BF16), w_ref[...], preferred_element_type=F32)
    mk_ref[0] = mkv[:, :MEM_WIDTH].astype(BF16)
    _store_values_transposed(mvt_ref, mkv[:, MEM_WIDTH:])


def _mem_proj(mem2d, w, nb):
    return pl.pallas_call(
        _mem_proj_kernel,
        grid=(nb,),
        in_specs=[pl.BlockSpec((N_MEM, D_MODEL), lambda b: (b, 0)),
                  pl.BlockSpec(w.shape, lambda b: (0, 0))],
        out_specs=[pl.BlockSpec((1, N_MEM, MEM_WIDTH), lambda b: (b, 0, 0)),
                   pl.BlockSpec((1, MEM_HEADS * VT_ROWS, N_MEM), lambda b: (b, 0, 0))],
        out_shape=[jax.ShapeDtypeStruct((nb, N_MEM, MEM_WIDTH), BF16),
                   jax.ShapeDtypeStruct((nb, MEM_HEADS * VT_ROWS, N_MEM), BF16)],
        compiler_params=_cparams(1),
        name="mem_proj",
    )(mem2d, w)


def _mem_attn_kernel(qt_ref, k_ref, vt_ref, o_ref):
    qt = qt_ref[0]
    kc = k_ref[0]
    for pair in range(MEM_HEADS // 2):
        qt_p = qt[pair * LANES:(pair + 1) * LANES]
        k_p = kc[:, pair * LANES:(pair + 1) * LANES]
        rowi = lax.broadcasted_iota(jnp.int32, qt_p.shape, 0)
        zero = jnp.zeros_like(qt_p)
        outs = []
        for h in range(2):
            mine = (rowi < HEAD_DIM) if h == 0 else (rowi >= HEAD_DIM)
            s = jnp.dot(k_p, jnp.where(mine, qt_p, zero), preferred_element_type=F32)
            p = jnp.exp2(s - jnp.max(s, axis=0, keepdims=True)).astype(BF16)
            r0 = (2 * pair + h) * VT_ROWS
            acc = jnp.dot(vt_ref[0, r0:r0 + VT_ROWS, :], p, preferred_element_type=F32)
            outs.append(acc[:HEAD_DIM] * (1.0 / acc[HEAD_DIM:HEAD_DIM + 1]))
        o_ref[0, :, pair * LANES:(pair + 1) * LANES] = (
            jnp.concatenate(outs, axis=0).T.astype(o_ref.dtype))


def _memory_attention(mqt, mk, mvt):
    nb, _, seq = mqt.shape
    tq = ATT_TQ
    return pl.pallas_call(
        _mem_attn_kernel,
        grid=(nb, seq // tq),
        in_specs=[pl.BlockSpec((1, MEM_WIDTH, tq), lambda b, i: (b, 0, i)),
                  pl.BlockSpec((1, N_MEM, MEM_WIDTH), lambda b, i: (b, 0, 0)),
                  pl.BlockSpec((1, MEM_HEADS * VT_ROWS, N_MEM), lambda b, i: (b, 0, 0))],
        out_specs=pl.BlockSpec((1, tq, MEM_WIDTH), lambda b, i: (b, i, 0)),
        out_shape=jax.ShapeDtypeStruct((nb, seq, MEM_WIDTH), BF16),
        compiler_params=_cparams(2),
        name="memory_attention",
    )(mqt, mk, mvt)


def _post_attn_kernel(x_ref, att_ref, memo_ref, wa_ref, wm_ref, g_ref, b_ref, wr_ref, br_ref,
                      x1_ref, x1p_ref, idx_ref, rank_ref, gate_ref, cnt_ref, carry_ref):
    t = pl.program_id(0)

    @pl.when(t == 0)
    def _():
        carry_ref[...] = jnp.zeros_like(carry_ref)

    tm = x_ref.shape[0]
    mix = (jnp.dot(att_ref[...], wa_ref[...], preferred_element_type=F32)
           + jnp.dot(memo_ref[...], wm_ref[...], preferred_element_type=F32))
    x1 = _layer_norm(DEEPNORM_ALPHA * x_ref[...] + mix, g_ref[...], b_ref[...])
    x1_ref[...] = x1
    x1b = x1.astype(BF16)
    x1p_ref[...] = _pack_bf16_pairs(x1b)
    logits = jnp.dot(x1b, wr_ref[...], preferred_element_type=F32) + br_ref[...]
    work = logits.T[:N_EXPERTS]
    row_f = lax.broadcasted_iota(jnp.int32, work.shape, 0).astype(F32)
    idxs, vals = [], []
    onehot_t = jnp.zeros(work.shape, F32)
    for _ in range(TOP_K):
        best = jnp.max(work, axis=0, keepdims=True)
        where_best = jnp.min(jnp.where(work == best, row_f, float(N_EXPERTS)), axis=0, keepdims=True)
        hit = row_f == where_best
        onehot_t = jnp.where(hit, 1.0, onehot_t)
        work = jnp.where(hit, -jnp.inf, work)
        idxs.append(where_best)
        vals.append(best)
    exps = [jnp.exp(v - vals[0]) for v in vals]
    denom = exps[0] + exps[1] + exps[2] + exps[3]
    src = lax.broadcasted_iota(jnp.int32, (tm, tm), 0)
    dst = lax.broadcasted_iota(jnp.int32, (tm, tm), 1)
    earlier = jnp.where(src < dst, 1.0, 0.0).astype(BF16)
    before_t = (jnp.dot(onehot_t.astype(BF16), earlier, preferred_element_type=F32)
                + carry_ref[...])
    ranks = [jnp.sum(jnp.where(row_f == idxs[r], before_t, 0.0), axis=0, keepdims=True)
             for r in range(TOP_K)]
    pad = jnp.zeros((8 - TOP_K, tm), F32)
    idx_ref[...] = jnp.concatenate(idxs + [pad], axis=0).astype(jnp.int32)
    rank_ref[...] = jnp.concatenate(ranks + [pad], axis=0).astype(jnp.int32)
    gates_t = jnp.concatenate([e / denom for e in exps] + [jnp.zeros((LANES - TOP_K, tm), F32)], axis=0)
    gate_ref[...] = gates_t.T
    total = carry_ref[...] + jnp.sum(onehot_t, axis=1, keepdims=True)
    carry_ref[...] = total
    cnt_ref[...] = jnp.broadcast_to(total, cnt_ref.shape)


def _post_attn(x2d, att, memo, wa, wm, g, b, wr, br):
    t_tokens = x2d.shape[0]
    tm = PROJ_TM
    row_spec = lambda width: pl.BlockSpec((tm, width), lambda i: (i, 0))
    full = lambda a: pl.BlockSpec(a.shape, lambda i: (0, 0))
    return pl.pallas_call(
        _post_attn_kernel,
        grid=(t_tokens // tm,),
        in_specs=[row_spec(D_MODEL), row_spec(att.shape[1]), row_spec(MEM_WIDTH),
                  full(wa), full(wm), full(g), full(b), full(wr), full(br)],
        out_specs=[row_spec(D_MODEL), row_spec(D_MODEL // 2),
                   pl.BlockSpec((8, tm), lambda i: (0, i)), pl.BlockSpec((8, tm), lambda i: (0, i)),
                   row_spec(LANES), pl.BlockSpec((N_EXPERTS, LANES), lambda i: (0, 0))],
        out_shape=[jax.ShapeDtypeStruct((t_tokens, D_MODEL), F32),
                   jax.ShapeDtypeStruct((t_tokens, D_MODEL // 2), jnp.uint32),
                   jax.ShapeDtypeStruct((8, t_tokens), jnp.int32),
                   jax.ShapeDtypeStruct((8, t_tokens), jnp.int32),
                   jax.ShapeDtypeStruct((t_tokens, LANES), F32),
                   jax.ShapeDtypeStruct((N_EXPERTS, LANES), F32)],
        scratch_shapes=[pltpu.VMEM((N_EXPERTS, 1), F32)],
        compiler_params=_cparams(1),
        name="outproj_ln_router",
    )(x2d, att, memo, wa, wm, g, b, wr, br)


def _slot_kernel(pstart_ref, idx_ref, rank_ref, o_ref):
    idx = idx_ref[...]
    dest = rank_ref[...]
    for e in range(N_EXPERTS):
        dest = dest + jnp.where(idx == e, pstart_ref[e], 0)
    o_ref[...] = dest


def _slot_table(idx, rank, pstart):
    return pl.pallas_call(
        _slot_kernel,
        in_specs=[pl.BlockSpec(memory_space=pltpu.SMEM), pl.BlockSpec(memory_space=pltpu.VMEM),
                  pl.BlockSpec(memory_space=pltpu.VMEM)],
        out_specs=pl.BlockSpec(memory_space=pltpu.VMEM),
        out_shape=jax.ShapeDtypeStruct(idx.shape, jnp.int32),
        name="slot_table",
    )(pstart, idx, rank)


def _expert_kernel(be_ref, nu_ref, br_ref, x_ref, wgu_hbm, bgu_ref, wdn_hbm, bdn_ref, *rest,
                   n_chunks, blk_lo, n_blocks, layer, has_prev):
    o_ref, wgu_st, wdn_st, wgu_bf, wdn_bf, h_ref, sem = rest[1:] if has_prev else rest
    step = pl.program_id(0)
    blk = step + blk_lo
    live_end = jnp.minimum(blk_lo + pl.num_programs(0), nu_ref[0])
    expert = be_ref[blk]

    def fetch(e):
        return (pltpu.make_async_copy(wgu_hbm.at[layer, e], wgu_st, sem.at[0]),
                pltpu.make_async_copy(wdn_hbm.at[layer, e], wdn_st, sem.at[1]))

    @pl.when(((step == 0) | (expert != be_ref[jnp.maximum(blk - 1, 0)])) & (blk < live_end))
    def _():
        @pl.when(step == 0)
        def _():
            for copy in fetch(expert):
                copy.start()

        for copy in fetch(expert):
            copy.wait()
        wgu_bf[...] = wgu_st[...].astype(BF16)
        wdn_bf[...] = wdn_st[...].astype(BF16)
        nxt = lax.while_loop(
            lambda j: (j < live_end) & (be_ref[jnp.minimum(j, n_blocks - 1)] == expert),
            lambda j: j + 1, blk + 1)

        @pl.when(nxt < live_end)
        def _():
            for copy in fetch(be_ref[jnp.minimum(nxt, n_blocks - 1)]):
                copy.start()

    def mlp(rows):
        x = _unpack_bf16_pairs(x_ref[:rows]).astype(BF16)
        cw = D_EXPERT // n_chunks
        for c in range(n_chunks):
            gs = slice(c * cw, (c + 1) * cw)
            us = slice(D_EXPERT + c * cw, D_EXPERT + (c + 1) * cw)
            g = jnp.dot(x, wgu_bf[:, gs], preferred_element_type=F32) + bgu_ref[0, 0, :, gs]
            u = jnp.dot(x, wgu_bf[:, us], preferred_element_type=F32) + bgu_ref[0, 0, :, us]
            g = jnp.minimum(g, SWIGLU_LIMIT)
            u = jnp.clip(u, -SWIGLU_LIMIT, SWIGLU_LIMIT)
            h_ref[:rows, gs] = ((u + 1.0) * (g * jax.nn.sigmoid(SWIGLU_ALPHA * g))).astype(BF16)
        y = jnp.dot(h_ref[:rows], wdn_bf[...], preferred_element_type=F32) + bdn_ref[0, 0]
        return _pack_bf16_pairs(y.astype(BF16))

    tm = o_ref.shape[0]
    live_rows = br_ref[blk]

    @pl.when(live_rows > tm // 2)
    def _():
        o_ref[...] = mlp(tm)

    @pl.when((live_rows > 0) & (live_rows <= tm // 2))
    def _():
        o_ref[:tm // 2] = mlp(tm // 2)
        o_ref[tm // 2:] = jnp.zeros((tm - tm // 2, o_ref.shape[1]), o_ref.dtype)

    @pl.when(live_rows == 0)
    def _():
        o_ref[...] = jnp.zeros_like(o_ref)


def _experts(block_expert, n_used, block_rows, xb, blk_lo, p_rows, y_prev, layer, wgu, bgu, wdn, bdn):
    tm = MOE_TM
    has_prev = y_prev is not None
    b_map = lambda i, be, nu, br: (layer, be[i + blk_lo], 0, 0)
    in_specs = [pl.BlockSpec((tm, D_MODEL // 2), lambda i, be, nu, br: (i, 0)),
                pl.BlockSpec(memory_space=pl.ANY),
                pl.BlockSpec((1, 1, 1, 2 * D_EXPERT), b_map),
                pl.BlockSpec(memory_space=pl.ANY),
                pl.BlockSpec((1, 1, 1, D_MODEL), b_map)]
    args = [block_expert, n_used, block_rows, xb, wgu, bgu, wdn, bdn]
    if has_prev:
        in_specs.append(pl.BlockSpec(memory_space=pl.ANY))
        args.append(y_prev)
    grid_spec = pltpu.PrefetchScalarGridSpec(
        num_scalar_prefetch=3,
        grid=(xb.shape[0] // tm,),
        in_specs=in_specs,
        out_specs=pl.BlockSpec((tm, D_MODEL // 2), lambda i, be, nu, br: (i + blk_lo, 0)),
        scratch_shapes=[pltpu.VMEM((D_MODEL, 2 * D_EXPERT), F32),
                        pltpu.VMEM((D_EXPERT, D_MODEL), F32),
                        pltpu.VMEM((D_MODEL, 2 * D_EXPERT), BF16),
                        pltpu.VMEM((D_EXPERT, D_MODEL), BF16),
                        pltpu.VMEM((tm, D_EXPERT), BF16),
                        pltpu.SemaphoreType.DMA((2,))],
    )
    return pl.pallas_call(
        functools.partial(_expert_kernel, n_chunks=4, blk_lo=blk_lo, n_blocks=p_rows // tm,
                          layer=layer, has_prev=has_prev),
        grid_spec=grid_spec,
        out_shape=jax.ShapeDtypeStruct((p_rows, D_MODEL // 2), jnp.uint32),
        input_output_aliases={len(args) - 1: 0} if has_prev else {},
        compiler_params=_cparams(1),
        name="experts",
    )(*args)


def _sc_move_rows(table_hbm, idx_v, out_hbm, base, n_chunks, rows_v, gsem, wsem):
    def gather(c, slot):
        off = pl.multiple_of(c * SC_CHUNK, SC_CHUNK)
        return pltpu.make_async_copy(table_hbm.at[idx_v.at[pl.ds(off, SC_CHUNK)]],
                                     rows_v.at[slot], gsem.at[slot])

    def put(c, slot):
        off = pl.multiple_of(c * SC_CHUNK, SC_CHUNK)
        return pltpu.make_async_copy(rows_v.at[slot], out_hbm.at[pl.ds(base + off, SC_CHUNK)],
                                     wsem.at[slot])

    gather(0, 0).start()
    gather(1, 1).start()

    @pl.loop(0, n_chunks, step=2)
    def _(c):
        for slot in range(2):
            gather(c + slot, slot).wait()
            put(c + slot, slot).start()
        for slot in range(2):
            put(c + slot, slot).wait()

            @pl.when(c + 2 + slot < n_chunks)
            def _():
                gather(c + 2 + slot, slot).start()


def _sc_gather_rows(table, idx):
    n_idx = idx.shape[0]
    width = table.shape[1]
    per_worker = n_idx // SC_WORKERS
    n_chunks = per_worker // SC_CHUNK
    assert per_worker * SC_WORKERS == n_idx and n_chunks * SC_CHUNK == per_worker and n_chunks % 2 == 0
    mesh = plsc.VectorSubcoreMesh(core_axis_name="c", subcore_axis_name="s",
                                  num_cores=SC_CORES, num_subcores=SC_SUBCORES)

    def body(table_hbm, idx_hbm, out_hbm, idx_v, rows_v, gsem, wsem):
        wid = lax.axis_index("s") * SC_CORES + lax.axis_index("c")
        base = wid * per_worker
        pltpu.sync_copy(idx_hbm.at[pl.ds(base, per_worker)], idx_v)
        _sc_move_rows(table_hbm, idx_v, out_hbm, base, n_chunks, rows_v, gsem, wsem)

    return pl.kernel(
        body,
        out_type=jax.ShapeDtypeStruct((n_idx, width), table.dtype),
        mesh=mesh,
        scratch_types=[pltpu.VMEM((per_worker,), jnp.int32),
                       pltpu.VMEM((2, SC_CHUNK, width), table.dtype),
                       pltpu.SemaphoreType.DMA((2,)),
                       pltpu.SemaphoreType.DMA((2,))],
        name="sc_gather_rows",
    )(table, idx)


def _sc_dispatch_rows(table, dest, slot_lo, n_slots):
    n_tok, width = table.shape
    n_pairs = dest.shape[0]
    per_worker = n_slots // SC_WORKERS
    n_chunks = per_worker // SC_CHUNK
    assert per_worker * SC_WORKERS == n_slots and n_chunks * SC_CHUNK == per_worker and n_chunks % 2 == 0
    n_scan = n_pairs // SC_SCAN
    assert n_scan * SC_SCAN == n_pairs
    mesh = plsc.VectorSubcoreMesh(core_axis_name="c", subcore_axis_name="s",
                                  num_cores=SC_CORES, num_subcores=SC_SUBCORES)

    def body(table_hbm, dest_hbm, out_hbm, tok_v, dest_v, rows_v, gsem, wsem):
        wid = lax.axis_index("s") * SC_CORES + lax.axis_index("c")
        base = wid * per_worker
        first = slot_lo + base
        lane = lax.iota(jnp.int32, SC_LANES)

        @pl.loop(0, per_worker // SC_LANES)
        def _(i):
            off = pl.multiple_of(i * SC_LANES, SC_LANES)
            tok_v[pl.ds(off, SC_LANES)] = lax.rem(first + off + lane, n_tok)

        @pl.loop(0, n_scan)
        def _(g):
            goff = pl.multiple_of(g * SC_SCAN, SC_SCAN)
            pltpu.sync_copy(dest_hbm.at[pl.ds(goff, SC_SCAN)], dest_v)

            @plsc.parallel_loop(0, SC_SCAN // SC_LANES, unroll=8)
            def _(i):
                off = pl.multiple_of(i * SC_LANES, SC_LANES)
                local = dest_v[pl.ds(off, SC_LANES)] - first
                mine = (local >= 0) & (local < per_worker)
                pair = goff + off + lane
                plsc.store_scatter(tok_v, [jnp.where(mine, local, 0)], lax.rem(pair, n_tok), mask=mine)

        _sc_move_rows(table_hbm, tok_v, out_hbm, base, n_chunks, rows_v, gsem, wsem)

    return pl.kernel(
        body,
        out_type=jax.ShapeDtypeStruct((n_slots, width), table.dtype),
        mesh=mesh,
        scratch_types=[pltpu.VMEM((per_worker,), jnp.int32),
                       pltpu.VMEM((SC_SCAN,), jnp.int32),
                       pltpu.VMEM((2, SC_CHUNK, width), table.dtype),
                       pltpu.SemaphoreType.DMA((2,)),
                       pltpu.SemaphoreType.DMA((2,))],
        compiler_params=pltpu.CompilerParams(needs_layout_passes=False),
        name="sc_dispatch_rows",
    )(table, dest)


def _combine_kernel(x1_ref, yg_ref, gate_ref, g_ref, b_ref, *rest):
    o_ref = rest[-1]
    gates = gate_ref[...]
    ffn = _unpack_bf16_pairs(yg_ref[0]) * gates[:, 0:1]
    for r in range(1, TOP_K):
        ffn = ffn + _unpack_bf16_pairs(yg_ref[r]) * gates[:, r:r + 1]
    o_ref[...] = _layer_norm(DEEPNORM_ALPHA * x1_ref[...] + ffn, g_ref[...], b_ref[...])


def _combine(x1, yg, gates, g, b, tok_lo, out_prev):
    t_tokens = x1.shape[0]
    tm = COMBINE_TM
    tile_lo = tok_lo // tm
    in_specs = [pl.BlockSpec((tm, D_MODEL), lambda i: (i + tile_lo, 0)),
                pl.BlockSpec((TOP_K, tm, D_MODEL // 2), lambda i: (0, i, 0)),
                pl.BlockSpec((tm, LANES), lambda i: (i + tile_lo, 0)),
                pl.BlockSpec((1, D_MODEL), lambda i: (0, 0)),
                pl.BlockSpec((1, D_MODEL), lambda i: (0, 0))]
    args = [x1, yg, gates, g, b]
    if out_prev is not None:
        in_specs.append(pl.BlockSpec(memory_space=pl.ANY))
        args.append(out_prev)
    return pl.pallas_call(
        _combine_kernel,
        grid=(yg.shape[1] // tm,),
        in_specs=in_specs,
        out_specs=pl.BlockSpec((tm, D_MODEL), lambda i: (i + tile_lo, 0)),
        out_shape=jax.ShapeDtypeStruct((t_tokens, D_MODEL), F32),
        input_output_aliases={len(args) - 1: 0} if out_prev is not None else {},
        compiler_params=_cparams(1),
        name="combine_ln",
    )(*args)


def _moe(x1, x1p, idx, rank, gates, counts, layer, wgu, bgu, wdn, bdn, g, b):
    t_tokens = x1.shape[0]
    tm = MOE_TM
    counts = counts[:, 0].astype(jnp.int32)
    padded = ((counts + tm - 1) // tm) * tm
    pend = jnp.cumsum(padded)
    pstart = pend - padded
    dest = _slot_table(idx, rank, pstart)[:TOP_K]
    n_blocks = (t_tokens * TOP_K) // tm + N_EXPERTS
    p_rows = n_blocks * tm
    block_start = jnp.arange(n_blocks, dtype=jnp.int32) * tm
    block_expert = jnp.minimum(
        jnp.sum((pend[None, :] <= block_start[:, None]).astype(jnp.int32), axis=1), N_EXPERTS - 1)
    n_used = (pend[-1:] // tm).astype(jnp.int32)
    mine = block_expert[:, None] == jnp.arange(N_EXPERTS, dtype=jnp.int32)[None, :]
    live_end = jnp.sum(jnp.where(mine, (pstart + counts)[None, :], 0), axis=1)
    block_rows = jnp.clip(live_end - block_start, 0, tm)
    y = None
    for lo, hi in ((0, n_blocks // MOE_SPLIT), (n_blocks // MOE_SPLIT, n_blocks)):
        xb = _sc_dispatch_rows(x1p, dest.reshape(-1), lo * tm, (hi - lo) * tm)
        y = _experts(block_expert, n_used, block_rows, xb, lo, p_rows, y, layer, wgu, bgu, wdn, bdn)
    out = None
    for lo, hi in ((0, t_tokens // MOE_SPLIT), (t_tokens // MOE_SPLIT, t_tokens)):
        yg = _sc_gather_rows(y, dest[:, lo:hi].reshape(-1)).reshape(TOP_K, hi - lo, D_MODEL // 2)
        out = _combine(x1, yg, gates, g, b, lo, out)
    return out


def _row(v, width=None):
    v = v.astype(F32).reshape(1, -1)
    if width is not None and v.shape[1] < width:
        v = jnp.pad(v, ((0, 0), (0, width - v.shape[1])))
    return v


def _pad_cols(w, width):
    return jnp.pad(w, ((0, 0), (0, width - w.shape[1])))


def kernel(x, mem, positions, a_w_in, a_b_f, a_w_out, b_w_in, b_g_q, b_w_uq, b_w_out,
           kv_w_dkv, kv_g, kv_w_ukv, mem_w_kv, ln_g, ln_b,
           moe_w_r, moe_b_r, moe_w_gu, moe_b_gu, moe_w_dn, moe_b_dn):
    nb, seq, d = x.shape
    t_tokens = nb * seq
    n_a = a_w_in.shape[0]
    x2d = x.reshape(t_tokens, d)
    mem2d = mem.reshape(nb * N_MEM, d)
    pos2d = positions.reshape(1, t_tokens)
    half = QK_ROPE // 2
    invf = (ROPE_THETA ** (-jnp.arange(half, dtype=F32) * 2.0 / QK_ROPE)).reshape(half, 1)

    shared_kv = None
    for l in range(DEPTH):
        mk, mvt = _mem_proj(mem2d, mem_w_kv[l].astype(BF16), nb)
        if l < n_a:
            w_in = a_w_in[l]
            w = jnp.concatenate([w_in[:, :3 * FOX_WIDTH],
                                 _pad_cols(w_in[:, 3 * FOX_WIDTH:3 * FOX_WIDTH + FOX_HEADS], LANES),
                                 w_in[:, 3 * FOX_WIDTH + FOX_HEADS:]], axis=1).astype(BF16)
            qt, k, vt, mqt, cum, cumt, cfirst, clast, knorm = _fox_proj(
                x2d, w, _row(a_b_f[l], LANES), seq)
            stats = [s.reshape(-1, LANES) for s in (cfirst, clast, knorm)]
            att = _causal_attention(qt, k.reshape(nb, seq, -1), vt,
                                    [cum.reshape(nb, seq, LANES), cumt] + stats, slab=False)
            w_out = a_w_out[l]
        else:
            bl = l - n_a
            w = jnp.concatenate([b_w_in[bl], kv_w_dkv[:, :KV_LORA],
                                 jnp.zeros((d, QK_NOPE), F32), kv_w_dkv[:, KV_LORA:],
                                 jnp.zeros((d, LANES - QK_NOPE - QK_ROPE), F32)], axis=1).astype(BF16)
            wuq = b_w_uq[bl].reshape(Q_LORA, MLA_HEADS, QK_NOPE + QK_ROPE)
            wuq = jnp.pad(wuq, ((0, 0), (0, 0), (0, MLA_SLAB - QK_NOPE - QK_ROPE)))
            wuq = wuq.reshape(Q_LORA, MLA_HEADS * MLA_SLAB).astype(BF16)
            wukv = kv_w_ukv.reshape(KV_LORA, MLA_HEADS, QK_NOPE + V_DIM)
            wk = jnp.pad(wukv[:, :, :QK_NOPE], ((0, 0), (0, 0), (0, MLA_SLAB - QK_NOPE)))
            wk = wk.reshape(KV_LORA, MLA_HEADS * MLA_SLAB).astype(BF16)
            wv = wukv[:, :, QK_NOPE:].reshape(KV_LORA, MLA_V_WIDTH).astype(BF16)
            qt, mqt, k_new, vt_new = _mla_proj(x2d, pos2d, invf, w, _row(b_g_q[bl]), wuq,
                                              _row(kv_g), wk, wv, seq)
            if shared_kv is None:
                shared_kv = (k_new.reshape(nb, seq, -1), vt_new)
            att = _causal_attention(qt, shared_kv[0], shared_kv[1], None, slab=True)
            w_out = b_w_out[bl]
        memo = _memory_attention(mqt, mk, mvt)
        n_att = w_out.shape[0] - MEM_WIDTH
        x1, x1p, idx, rank, gates, counts = _post_attn(
            x2d, att.reshape(t_tokens, -1), memo.reshape(t_tokens, MEM_WIDTH),
            w_out[:n_att].astype(BF16), w_out[n_att:].astype(BF16),
            _row(ln_g[l, 0]), _row(ln_b[l, 0]),
            _pad_cols(moe_w_r[l], LANES).astype(BF16), _row(moe_b_r[l], LANES))
        x2d = _moe(x1, x1p, idx, rank, gates, counts, l,
                   moe_w_gu, moe_b_gu.reshape(DEPTH, N_EXPERTS, 1, -1),
                   moe_w_dn, moe_b_dn.reshape(DEPTH, N_EXPERTS, 1, -1),
                   _row(ln_g[l, 1]), _row(ln_b[l, 1]))
    return x2d.reshape(nb, seq, d)
```

```python
import functools
import math

import jax
import jax.numpy as jnp
from jax import lax
from jax.experimental import pallas as pl
from jax.experimental.pallas import tpu as pltpu
from jax.experimental.pallas import tpu_sc as plsc

F32 = jnp.float32
BF16 = jnp.bfloat16

D_MODEL = 1024
DEPTH = 2
N_MEM = 256
HEAD_DIM = 64
FOX_HEADS = 12
MEM_HEADS = 4
MLA_HEADS = 12
Q_LORA = 384
KV_LORA = 256
QK_NOPE = 64
QK_ROPE = 32
V_DIM = 64
ROPE_THETA = 10000.0
N_EXPERTS = 32
TOP_K = 4
D_EXPERT = D_MODEL
SWIGLU_LIMIT = 7.0
SWIGLU_ALPHA = 1.702
LN_EPS = 1e-5
RMS_EPS = 1e-6
NEG_INF = -1e30
DEEPNORM_ALPHA = (2 * DEPTH) ** 0.25
FOX_WIDTH = FOX_HEADS * HEAD_DIM
MEM_WIDTH = MEM_HEADS * HEAD_DIM
MLA_V_WIDTH = MLA_HEADS * V_DIM

LANES = 128
LOG2E = math.log2(math.e)
VMEM_LIMIT = 48 * 1024 * 1024

PROJ_TM = 512
ATT_TQ = 512
FOX_TK = 256
MOE_TM = 512
COMBINE_TM = 1024
MOE_SPLIT = 4
MLA_SLAB = LANES
VT_ROWS = HEAD_DIM + 16
SC_CORES = 2
SC_SUBCORES = 16
SC_WORKERS = SC_CORES * SC_SUBCORES
SC_CHUNK = 32
SC_LANES = 16
SC_SCAN = 8192


def _cparams(n_axes):
    return pltpu.CompilerParams(dimension_semantics=("arbitrary",) * n_axes,
                                vmem_limit_bytes=VMEM_LIMIT)


def _split3(x):
    hi = x.astype(BF16)
    r1 = x - hi.astype(F32)
    mid = r1.astype(BF16)
    lo = (r1 - mid.astype(F32)).astype(BF16)
    return hi, mid, lo


def _pack_bf16_pairs(v):
    bits = pltpu.bitcast(v.astype(F32), jnp.uint32)
    half = v.shape[1] // 2
    return (bits[:, :half] >> 16) | bits[:, half:]


def _unpack_bf16_pairs(words):
    return jnp.concatenate([pltpu.bitcast(words << 16, F32),
                            pltpu.bitcast(words & jnp.uint32(0xFFFF0000), F32)], axis=1)


def _layer_norm(y, g, b):
    mu = jnp.mean(y, axis=-1, keepdims=True)
    yc = y - mu
    var = jnp.mean(yc * yc, axis=-1, keepdims=True)
    return yc * lax.rsqrt(var + LN_EPS) * g + b


def _rms_norm(y, g):
    return y * lax.rsqrt(jnp.mean(y * y, axis=-1, keepdims=True) + RMS_EPS) * g


def _store_transposed(dst_ref, val):
    for s in range(val.shape[1] // LANES):
        sl = slice(s * LANES, (s + 1) * LANES)
        dst_ref[0, sl, :] = val[:, sl].T.astype(dst_ref.dtype)


def _store_values_transposed(vt_ref, val):
    tm = val.shape[0]
    ones = jnp.ones((VT_ROWS - HEAD_DIM, tm), vt_ref.dtype)
    for s in range(val.shape[1] // LANES):
        pair_t = val[:, s * LANES:(s + 1) * LANES].T.astype(vt_ref.dtype)
        for h in range(2):
            r0 = (2 * s + h) * VT_ROWS
            vt_ref[0, r0:r0 + HEAD_DIM, :] = pair_t[h * HEAD_DIM:(h + 1) * HEAD_DIM]
            vt_ref[0, r0 + HEAD_DIM:r0 + VT_ROWS, :] = ones


def _fox_proj_kernel(x_ref, w_ref, bf_ref, seg_ref, qt_ref, k_ref, vt_ref, mqt_ref, cum_ref,
                     cumt_ref, cfirst_ref, clast_ref, knorm_ref, carry_ref, knmax_ref, *,
                     tiles_per_batch):
    t = pl.program_id(0)

    @pl.when(t % tiles_per_batch == 0)
    def _():
        carry_ref[...] = jnp.zeros_like(carry_ref)
        knmax_ref[...] = jnp.zeros_like(knmax_ref)

    tm = x_ref.shape[0]
    proj = jnp.dot(x_ref[...].astype(BF16), w_ref[...], preferred_element_type=F32)
    qscale = HEAD_DIM ** -0.5 * LOG2E
    _store_transposed(qt_ref, proj[:, :FOX_WIDTH] * qscale)
    kb = proj[:, FOX_WIDTH:2 * FOX_WIDTH].astype(BF16)
    k_ref[...] = kb
    _store_values_transposed(vt_ref, proj[:, 2 * FOX_WIDTH:3 * FOX_WIDTH])
    kf = kb.astype(F32)
    ksq = jnp.dot((kf * kf).astype(BF16), seg_ref[...], preferred_element_type=F32)
    n_sub = tm // FOX_TK
    for sub in range(n_sub):
        tile_max = jnp.sqrt(jnp.max(ksq[sub * FOX_TK:(sub + 1) * FOX_TK], axis=0, keepdims=True))
        knmax_ref[...] = jnp.maximum(knmax_ref[...], tile_max)
        knorm_ref[sub] = knmax_ref[...]
    f = proj[:, 3 * FOX_WIDTH:3 * FOX_WIDTH + LANES] + bf_ref[...]
    _store_transposed(mqt_ref, proj[:, 3 * FOX_WIDTH + LANES:] * qscale)
    log_f = jnp.minimum(f, 0.0) - jnp.log1p(jnp.exp(-jnp.abs(f)))
    row = lax.broadcasted_iota(jnp.int32, (tm, tm), 0)
    col = lax.broadcasted_iota(jnp.int32, (tm, tm), 1)
    tri = jnp.where(row >= col, 1.0, 0.0).astype(BF16)
    hi, mid, lo = _split3(log_f)
    cum = (jnp.dot(tri, hi, preferred_element_type=F32)
           + jnp.dot(tri, mid, preferred_element_type=F32)
           + jnp.dot(tri, lo, preferred_element_type=F32)) + carry_ref[...]
    carry_ref[...] = cum[tm - 1:tm, :]
    cum2 = cum * LOG2E
    cum_ref[...] = cum2
    cumt_ref[0] = cum2.T[:16, :]
    for sub in range(n_sub):
        cfirst_ref[sub] = cum2[sub * FOX_TK:sub * FOX_TK + 1, :]
        clast_ref[sub] = cum2[(sub + 1) * FOX_TK - 1:(sub + 1) * FOX_TK, :]


def _fox_proj(x2d, w, bf, seq):
    t_tokens = x2d.shape[0]
    tm = PROJ_TM
    nb = t_tokens // seq
    n = w.shape[1]
    tiles_per_batch = seq // tm
    n_tiles = t_tokens // tm
    seg = (jnp.arange(FOX_WIDTH)[:, None] // HEAD_DIM == jnp.arange(LANES)[None, :]).astype(BF16)
    row_spec = lambda width: pl.BlockSpec((tm, width), lambda i: (i, 0))
    t_spec = lambda rows: pl.BlockSpec(
        (1, rows, tm), lambda i: (i // tiles_per_batch, 0, i % tiles_per_batch))
    n_sub = tm // FOX_TK
    stat_spec = pl.BlockSpec((n_sub, 1, LANES), lambda i: (i, 0, 0))
    stat_shape = jax.ShapeDtypeStruct((n_tiles * n_sub, 1, LANES), F32)
    return pl.pallas_call(
        functools.partial(_fox_proj_kernel, tiles_per_batch=tiles_per_batch),
        grid=(n_tiles,),
        in_specs=[row_spec(D_MODEL),
                  pl.BlockSpec((D_MODEL, n), lambda i: (0, 0)),
                  pl.BlockSpec((1, LANES), lambda i: (0, 0)),
                  pl.BlockSpec((FOX_WIDTH, LANES), lambda i: (0, 0))],
        out_specs=[t_spec(FOX_WIDTH), row_spec(FOX_WIDTH), t_spec(FOX_HEADS * VT_ROWS),
                   t_spec(MEM_WIDTH), row_spec(LANES), t_spec(16),
                   stat_spec, stat_spec, stat_spec],
        out_shape=[jax.ShapeDtypeStruct((nb, FOX_WIDTH, seq), BF16),
                   jax.ShapeDtypeStruct((t_tokens, FOX_WIDTH), BF16),
                   jax.ShapeDtypeStruct((nb, FOX_HEADS * VT_ROWS, seq), BF16),
                   jax.ShapeDtypeStruct((nb, MEM_WIDTH, seq), BF16),
                   jax.ShapeDtypeStruct((t_tokens, LANES), F32),
                   jax.ShapeDtypeStruct((nb, 16, seq), F32),
                   stat_shape, stat_shape, stat_shape],
        scratch_shapes=[pltpu.VMEM((1, LANES), F32), pltpu.VMEM((1, LANES), F32)],
        compiler_params=pltpu.CompilerParams(
            dimension_semantics=("arbitrary",), vmem_limit_bytes=VMEM_LIMIT,
            allow_input_fusion=[False, True, False, False]),
        name="fox_proj",
    )(x2d, w, bf, seg)


def _rope_slab_t(slab_t, cos_t, sin_t):
    half = QK_ROPE // 2
    x1 = slab_t[QK_NOPE:QK_NOPE + half]
    x2 = slab_t[QK_NOPE + half:QK_NOPE + QK_ROPE]
    return jnp.concatenate([slab_t[:QK_NOPE], x1 * cos_t - x2 * sin_t, x1 * sin_t + x2 * cos_t,
                            slab_t[QK_NOPE + QK_ROPE:]], axis=0)


def _mla_proj_kernel(x_ref, pos_ref, invf_ref, w_ref, gq_ref, wuq_ref, gkv_ref, wk_ref, wv_ref,
                     qt_ref, mqt_ref, k_ref, vt_ref):
    proj = jnp.dot(x_ref[...].astype(BF16), w_ref[...], preferred_element_type=F32)
    ang_t = invf_ref[...] * pos_ref[...].astype(F32)
    cos_t = jnp.cos(ang_t)
    sin_t = jnp.sin(ang_t)
    c_q = _rms_norm(proj[:, :Q_LORA], gq_ref[...])
    _store_transposed(mqt_ref, proj[:, Q_LORA:Q_LORA + MEM_WIDTH] * (HEAD_DIM ** -0.5 * LOG2E))
    kv_off = Q_LORA + MEM_WIDTH
    c_kv = _rms_norm(proj[:, kv_off:kv_off + KV_LORA], gkv_ref[...])
    kr = _rope_slab_t(proj[:, kv_off + KV_LORA:].T, cos_t, sin_t).T
    q = jnp.dot(c_q.astype(BF16), wuq_ref[...], preferred_element_type=F32)
    kn = jnp.dot(c_kv.astype(BF16), wk_ref[...], preferred_element_type=F32)
    qscale = (QK_NOPE + QK_ROPE) ** -0.5 * LOG2E
    for h in range(MLA_HEADS):
        sl = slice(h * MLA_SLAB, (h + 1) * MLA_SLAB)
        qt_ref[0, sl, :] = (_rope_slab_t(q[:, sl].T, cos_t, sin_t) * qscale).astype(BF16)
        k_ref[:, sl] = (kn[:, sl] + kr).astype(BF16)
    _store_values_transposed(vt_ref, jnp.dot(c_kv.astype(BF16), wv_ref[...],
                                             preferred_element_type=F32))


def _mla_proj(x2d, pos2d, invf, w, gq, wuq, gkv, wk, wv, seq):
    t_tokens = x2d.shape[0]
    tm = PROJ_TM
    nb = t_tokens // seq
    tiles_per_batch = seq // tm
    row_spec = lambda width: pl.BlockSpec((tm, width), lambda i: (i, 0))
    t_spec = lambda rows: pl.BlockSpec(
        (1, rows, tm), lambda i: (i // tiles_per_batch, 0, i % tiles_per_batch))
    full = lambda a: pl.BlockSpec(a.shape, lambda i: (0, 0))
    slabs = MLA_HEADS * MLA_SLAB
    return pl.pallas_call(
        _mla_proj_kernel,
        grid=(t_tokens // tm,),
        in_specs=[row_spec(D_MODEL), pl.BlockSpec((1, tm), lambda i: (0, i)), full(invf), full(w),
                  full(gq), full(wuq), full(gkv), full(wk), full(wv)],
        out_specs=[t_spec(slabs), t_spec(MEM_WIDTH), row_spec(slabs),
                   t_spec(MLA_HEADS * VT_ROWS)],
        out_shape=[jax.ShapeDtypeStruct((nb, slabs, seq), BF16),
                   jax.ShapeDtypeStruct((nb, MEM_WIDTH, seq), BF16),
                   jax.ShapeDtypeStruct((t_tokens, slabs), BF16),
                   jax.ShapeDtypeStruct((nb, MLA_HEADS * VT_ROWS, seq), BF16)],
        compiler_params=_cparams(1),
        name="mla_proj",
    )(x2d, pos2d, invf, w, gq, wuq, gkv, wk, wv)


SKIP_LOG2 = -160.0
NORM_SLACK = 1.02


def _causal_attn_kernel(*refs, tq, tk, n_chunks, fox, slab, long_body):
    if fox:
        (cf_ref, cl_ref, kn_ref, qt_ref, k_ref, vt_ref, cq_ref, ck_ref,
         o_ref, m_ref, acc_ref, s_ref, cmax_ref) = refs
    else:
        qt_ref, k_ref, vt_ref, o_ref, m_ref, acc_ref, s_ref, cmax_ref = refs
    b = pl.program_id(0)
    pair = pl.program_id(1)
    i = pl.program_id(2)
    m_ref[...] = jnp.full(m_ref.shape, NEG_INF, F32)
    acc_ref[...] = jnp.zeros(acc_ref.shape, F32)
    qt = qt_ref[0]
    if slab:
        qth = [qt[:MLA_SLAB], qt[MLA_SLAB:]]
    else:
        rowi = lax.broadcasted_iota(jnp.int32, qt.shape, 0)
        zero = jnp.zeros_like(qt)
        qth = [jnp.where(rowi < HEAD_DIM, qt, zero), jnp.where(rowi >= HEAD_DIM, qt, zero)]
    if fox:
        lane = lax.broadcasted_iota(jnp.int32, (1, LANES), 1)
        cq = [cq_ref[0, pl.ds(2 * pair + h, 1), :] for h in range(2)]

    def logits_to(slot, j):
        off = pl.multiple_of(j * tk, tk)
        kc = k_ref[0, pl.ds(off, tk), :]
        for h in range(2):
            kh = kc[:, h * MLA_SLAB:(h + 1) * MLA_SLAB] if slab else kc
            s = jnp.dot(kh, qth[h], preferred_element_type=F32)
            if fox:
                ck_blk = ck_ref[0, pl.ds(off, tk), :]
                ck = jnp.sum(jnp.where(lane == 2 * pair + h, ck_blk, 0.0), axis=1, keepdims=True)
                s = s + cq[h] - ck
            s_ref[slot, h] = s
            cmax_ref[slot, h] = jnp.max(s, axis=0, keepdims=True)

    def softmax_pv(slot, j, causal_mask):
        off = pl.multiple_of(j * tk, tk)
        for h in range(2):
            s = s_ref[slot, h]
            if causal_mask is not None:
                s = jnp.where(causal_mask, s, NEG_INF)
                chunk_max = jnp.max(s, axis=0, keepdims=True)
            else:
                chunk_max = cmax_ref[slot, h]
            m_prev = m_ref[h]
            m_new = jnp.maximum(m_prev, chunk_max)
            alpha = jnp.exp2(m_prev - m_new)
            p = jnp.exp2(s - m_new).astype(BF16)
            vth = vt_ref[0, h * VT_ROWS:(h + 1) * VT_ROWS, pl.ds(off, tk)]
            acc_ref[h] = acc_ref[h] * alpha + jnp.dot(vth, p, preferred_element_type=F32)
            m_ref[h] = m_new

    per_tile = tq // tk
    d0 = per_tile * i
    if fox:
        base = b * n_chunks
        qn = []
        for h in range(2):
            qf = qth[h].astype(F32)
            qn.append(jnp.sqrt(jnp.max(jnp.sum(qf * qf, axis=0, keepdims=True))) * NORM_SLACK)

        top = [2.0 * qn[h] * kn_ref[base + d0 + per_tile - 1, 2 * pair + h]
               + cf_ref[base + d0, 2 * pair + h] for h in range(2)]

        def live(j):
            ub = [top[h] - cl_ref[base + jnp.maximum(j, 0), 2 * pair + h] for h in range(2)]
            return (j >= 0) & (jnp.maximum(ub[0], ub[1]) > SKIP_LOG2)

        j0 = lax.while_loop(live, lambda j: j - 1, d0 - 1) + 1
    else:
        j0 = 0
    n_before = d0 - j0

    def causal_mask(shift):
        key = lax.broadcasted_iota(jnp.int32, (tk, tq), 0)
        qry = lax.broadcasted_iota(jnp.int32, (tk, tq), 1)
        return key + shift <= qry

    def finish():
        if per_tile == 2:
            logits_to(1, d0 + 1)
        softmax_pv(0, d0, causal_mask(0))
        if per_tile == 2:
            softmax_pv(1, d0 + 1, causal_mask(tk))

    def pipelined():
        odd = (n_before % 2) == 1

        @pl.when(odd)
        def _():
            logits_to(1, j0)
            logits_to(0, j0 + 1)
            softmax_pv(1, j0, None)

        @pl.when(jnp.logical_not(odd))
        def _():
            logits_to(0, j0)

        def two_steps(j):
            logits_to(1, j + 1)
            softmax_pv(0, j, None)
            logits_to(0, j + 2)
            softmax_pv(1, j + 1, None)

        j_even = j0 + (n_before % 2)
        n_two = n_before // 2
        if long_body:
            @pl.when((n_two % 2) == 1)
            def _():
                two_steps(j_even)

            j_quad = j_even + 2 * (n_two % 2)

            def body(t, carry):
                two_steps(j_quad + 4 * t)
                two_steps(j_quad + 4 * t + 2)
                return carry

            lax.fori_loop(0, n_two // 2, body, 0)
        else:
            def body(t, carry):
                two_steps(j_even + 2 * t)
                return carry

            lax.fori_loop(0, n_two, body, 0)
        finish()

    if fox:
        @pl.when(n_before == 1)
        def _():
            logits_to(1, j0)
            logits_to(0, d0)
            softmax_pv(1, j0, None)
            finish()

        @pl.when(n_before != 1)
        def _():
            pipelined()
    else:
        pipelined()
    out_t = jnp.concatenate(
        [acc_ref[h, :HEAD_DIM] * (1.0 / acc_ref[h, HEAD_DIM:HEAD_DIM + 1]) for h in range(2)],
        axis=0)
    o_ref[0] = out_t.T.astype(o_ref.dtype)


def _causal_attention(qt, k, vt, fox_args, *, slab):
    nb, seq, _ = k.shape
    fox = fox_args is not None
    tq = ATT_TQ
    tk = FOX_TK if fox else ATT_TQ
    n_tiles = seq // tq
    n_pairs = FOX_HEADS // 2
    rows = 2 * MLA_SLAB if slab else LANES
    in_specs = [pl.BlockSpec((1, rows, tq), lambda b, p, i: (b, p, i)),
                pl.BlockSpec((1, seq, rows), lambda b, p, i: (b, 0, p)),
                pl.BlockSpec((1, 2 * VT_ROWS, seq), lambda b, p, i: (b, p, 0))]
    args = [qt, k, vt]
    if fox:
        cum, cumt, cfirst, clast, knorm = fox_args
        smem = pl.BlockSpec(memory_space=pltpu.SMEM)
        in_specs = [smem, smem, smem] + in_specs + [
            pl.BlockSpec((1, 16, tq), lambda b, p, i: (b, 0, i)),
            pl.BlockSpec((1, seq, LANES), lambda b, p, i: (b, 0, 0))]
        args = [cfirst, clast, knorm] + args + [cumt, cum]
    return pl.pallas_call(
        functools.partial(_causal_attn_kernel, tq=tq, tk=tk, n_chunks=seq // tk, fox=fox, slab=slab,
                          long_body=not fox),
        grid=(nb, n_pairs, n_tiles),
        in_specs=in_specs,
        out_specs=pl.BlockSpec((1, tq, LANES), lambda b, p, i: (b, i, p)),
        out_shape=jax.ShapeDtypeStruct((nb, seq, n_pairs * LANES), BF16),
        scratch_shapes=[pltpu.VMEM((2, 1, tq), F32), pltpu.VMEM((2, VT_ROWS, tq), F32),
                        pltpu.VMEM((2, 2, tk, tq), F32), pltpu.VMEM((2, 2, 1, tq), F32)],
        compiler_params=_cparams(3),
        name="fox_attention" if fox else "mla_attention",
    )(*args)


def _mem_proj_kernel(mem_ref, w_ref, mk_ref, mvt_ref):
    mkv = jnp.dot(mem_ref[...].astype(BF16), w_ref[...], preferred_element_type=F32)
    mk_ref[0] = mkv[:, :MEM_WIDTH].astype(BF16)
    _store_values_transposed(mvt_ref, mkv[:, MEM_WIDTH:])


def _mem_proj(mem2d, w, nb):
    return pl.pallas_call(
        _mem_proj_kernel,
        grid=(nb,),
        in_specs=[pl.BlockSpec((N_MEM, D_MODEL), lambda b: (b, 0)),
                  pl.BlockSpec(w.shape, lambda b: (0, 0))],
        out_specs=[pl.BlockSpec((1, N_MEM, MEM_WIDTH), lambda b: (b, 0, 0)),
                   pl.BlockSpec((1, MEM_HEADS * VT_ROWS, N_MEM), lambda b: (b, 0, 0))],
        out_shape=[jax.ShapeDtypeStruct((nb, N_MEM, MEM_WIDTH), BF16),
                   jax.ShapeDtypeStruct((nb, MEM_HEADS * VT_ROWS, N_MEM), BF16)],
        compiler_params=_cparams(1),
        name="mem_proj",
    )(mem2d, w)


def _mem_attn_kernel(qt_ref, k_ref, vt_ref, o_ref):
    qt = qt_ref[0]
    kc = k_ref[0]
    for pair in range(MEM_HEADS // 2):
        qt_p = qt[pair * LANES:(pair + 1) * LANES]
        k_p = kc[:, pair * LANES:(pair + 1) * LANES]
        rowi = lax.broadcasted_iota(jnp.int32, qt_p.shape, 0)
        zero = jnp.zeros_like(qt_p)
        outs = []
        for h in range(2):
            mine = (rowi < HEAD_DIM) if h == 0 else (rowi >= HEAD_DIM)
            s = jnp.dot(k_p, jnp.where(mine, qt_p, zero), preferred_element_type=F32)
            p = jnp.exp2(s - jnp.max(s, axis=0, keepdims=True)).astype(BF16)
            r0 = (2 * pair + h) * VT_ROWS
            acc = jnp.dot(vt_ref[0, r0:r0 + VT_ROWS, :], p, preferred_element_type=F32)
            outs.append(acc[:HEAD_DIM] * (1.0 / acc[HEAD_DIM:HEAD_DIM + 1]))
        o_ref[0, :, pair * LANES:(pair + 1) * LANES] = (
            jnp.concatenate(outs, axis=0).T.astype(o_ref.dtype))


def _memory_attention(mqt, mk, mvt):
    nb, _, seq = mqt.shape
    tq = ATT_TQ
    return pl.pallas_call(
        _mem_attn_kernel,
        grid=(nb, seq // tq),
        in_specs=[pl.BlockSpec((1, MEM_WIDTH, tq), lambda b, i: (b, 0, i)),
                  pl.BlockSpec((1, N_MEM, MEM_WIDTH), lambda b, i: (b, 0, 0)),
                  pl.BlockSpec((1, MEM_HEADS * VT_ROWS, N_MEM), lambda b, i: (b, 0, 0))],
        out_specs=pl.BlockSpec((1, tq, MEM_WIDTH), lambda b, i: (b, i, 0)),
        out_shape=jax.ShapeDtypeStruct((nb, seq, MEM_WIDTH), BF16),
        compiler_params=_cparams(2),
        name="memory_attention",
    )(mqt, mk, mvt)


def _post_attn_kernel(x_ref, att_ref, memo_ref, wa_ref, wm_ref, g_ref, b_ref, wr_ref, br_ref,
                      x1_ref, x1p_ref, idx_ref, rank_ref, gate_ref, cnt_ref, carry_ref):
    t = pl.program_id(0)

    @pl.when(t == 0)
    def _():
        carry_ref[...] = jnp.zeros_like(carry_ref)

    tm = x_ref.shape[0]
    mix = (jnp.dot(att_ref[...], wa_ref[...], preferred_element_type=F32)
           + jnp.dot(memo_ref[...], wm_ref[...], preferred_element_type=F32))
    x1 = _layer_norm(DEEPNORM_ALPHA * x_ref[...] + mix, g_ref[...], b_ref[...])
    x1_ref[...] = x1
    x1b = x1.astype(BF16)
    x1p_ref[...] = _pack_bf16_pairs(x1b)
    logits = jnp.dot(x1b, wr_ref[...], preferred_element_type=F32) + br_ref[...]
    work = logits.T[:N_EXPERTS]
    row_f = lax.broadcasted_iota(jnp.int32, work.shape, 0).astype(F32)
    idxs, vals = [], []
    onehot_t = jnp.zeros(work.shape, F32)
    for _ in range(TOP_K):
        best = jnp.max(work, axis=0, keepdims=True)
        where_best = jnp.min(jnp.where(work == best, row_f, float(N_EXPERTS)), axis=0, keepdims=True)
        hit = row_f == where_best
        onehot_t = jnp.where(hit, 1.0, onehot_t)
        work = jnp.where(hit, -jnp.inf, work)
        idxs.append(where_best)
        vals.append(best)
    exps = [jnp.exp(v - vals[0]) for v in vals]
    denom = exps[0] + exps[1] + exps[2] + exps[3]
    src = lax.broadcasted_iota(jnp.int32, (tm, tm), 0)
    dst = lax.broadcasted_iota(jnp.int32, (tm, tm), 1)
    earlier = jnp.where(src < dst, 1.0, 0.0).astype(BF16)
    before_t = (jnp.dot(onehot_t.astype(BF16), earlier, preferred_element_type=F32)
                + carry_ref[...])
    ranks = [jnp.sum(jnp.where(row_f == idxs[r], before_t, 0.0), axis=0, keepdims=True)
             for r in range(TOP_K)]
    pad = jnp.zeros((8 - TOP_K, tm), F32)
    idx_ref[...] = jnp.concatenate(idxs + [pad], axis=0).astype(jnp.int32)
    rank_ref[...] = jnp.concatenate(ranks + [pad], axis=0).astype(jnp.int32)
    gates_t = jnp.concatenate([e / denom for e in exps] + [jnp.zeros((LANES - TOP_K, tm), F32)], axis=0)
    gate_ref[...] = gates_t.T
    total = carry_ref[...] + jnp.sum(onehot_t, axis=1, keepdims=True)
    carry_ref[...] = total
    cnt_ref[...] = jnp.broadcast_to(total, cnt_ref.shape)


def _post_attn(x2d, att, memo, wa, wm, g, b, wr, br):
    t_tokens = x2d.shape[0]
    tm = PROJ_TM
    row_spec = lambda width: pl.BlockSpec((tm, width), lambda i: (i, 0))
    full = lambda a: pl.BlockSpec(a.shape, lambda i: (0, 0))
    return pl.pallas_call(
        _post_attn_kernel,
        grid=(t_tokens // tm,),
        in_specs=[row_spec(D_MODEL), row_spec(att.shape[1]), row_spec(MEM_WIDTH),
                  full(wa), full(wm), full(g), full(b), full(wr), full(br)],
        out_specs=[row_spec(D_MODEL), row_spec(D_MODEL // 2),
                   pl.BlockSpec((8, tm), lambda i: (0, i)), pl.BlockSpec((8, tm), lambda i: (0, i)),
                   row_spec(LANES), pl.BlockSpec((N_EXPERTS, LANES), lambda i: (0, 0))],
        out_shape=[jax.ShapeDtypeStruct((t_tokens, D_MODEL), F32),
                   jax.ShapeDtypeStruct((t_tokens, D_MODEL // 2), jnp.uint32),
                   jax.ShapeDtypeStruct((8, t_tokens), jnp.int32),
                   jax.ShapeDtypeStruct((8, t_tokens), jnp.int32),
                   jax.ShapeDtypeStruct((t_tokens, LANES), F32),
                   jax.ShapeDtypeStruct((N_EXPERTS, LANES), F32)],
        scratch_shapes=[pltpu.VMEM((N_EXPERTS, 1), F32)],
        compiler_params=_cparams(1),
        name="outproj_ln_router",
    )(x2d, att, memo, wa, wm, g, b, wr, br)


def _slot_kernel(pstart_ref, idx_ref, rank_ref, o_ref):
    idx = idx_ref[...]
    dest = rank_ref[...]
    for e in range(N_EXPERTS):
        dest = dest + jnp.where(idx == e, pstart_ref[e], 0)
    o_ref[...] = dest


def _slot_table(idx, rank, pstart):
    return pl.pallas_call(
        _slot_kernel,
        in_specs=[pl.BlockSpec(memory_space=pltpu.SMEM), pl.BlockSpec(memory_space=pltpu.VMEM),
                  pl.BlockSpec(memory_space=pltpu.VMEM)],
        out_specs=pl.BlockSpec(memory_space=pltpu.VMEM),
        out_shape=jax.ShapeDtypeStruct(idx.shape, jnp.int32),
        name="slot_table",
    )(pstart, idx, rank)


def _expert_kernel(be_ref, nu_ref, br_ref, x_ref, wgu_hbm, bgu_ref, wdn_hbm, bdn_ref, *rest,
                   n_chunks, blk_lo, n_blocks, layer, has_prev):
    o_ref, wgu_st, wdn_st, wgu_bf, wdn_bf, h_ref, sem = rest[1:] if has_prev else rest
    step = pl.program_id(0)
    blk = step + blk_lo
    live_end = jnp.minimum(blk_lo + pl.num_programs(0), nu_ref[0])
    expert = be_ref[blk]

    def fetch(e):
        return (pltpu.make_async_copy(wgu_hbm.at[layer, e], wgu_st, sem.at[0]),
                pltpu.make_async_copy(wdn_hbm.at[layer, e], wdn_st, sem.at[1]))

    @pl.when(((step == 0) | (expert != be_ref[jnp.maximum(blk - 1, 0)])) & (blk < live_end))
    def _():
        @pl.when(step == 0)
        def _():
            for copy in fetch(expert):
                copy.start()

        for copy in fetch(expert):
            copy.wait()
        wgu_bf[...] = wgu_st[...].astype(BF16)
        wdn_bf[...] = wdn_st[...].astype(BF16)
        nxt = lax.while_loop(
            lambda j: (j < live_end) & (be_ref[jnp.minimum(j, n_blocks - 1)] == expert),
            lambda j: j + 1, blk + 1)

        @pl.when(nxt < live_end)
        def _():
            for copy in fetch(be_ref[jnp.minimum(nxt, n_blocks - 1)]):
                copy.start()

    def mlp(rows):
        x = _unpack_bf16_pairs(x_ref[:rows]).astype(BF16)
        cw = D_EXPERT // n_chunks
        for c in range(n_chunks):
            gs = slice(c * cw, (c + 1) * cw)
            us = slice(D_EXPERT + c * cw, D_EXPERT + (c + 1) * cw)
            g = jnp.dot(x, wgu_bf[:, gs], preferred_element_type=F32) + bgu_ref[0, 0, :, gs]
            u = jnp.dot(x, wgu_bf[:, us], preferred_element_type=F32) + bgu_ref[0, 0, :, us]
            g = jnp.minimum(g, SWIGLU_LIMIT)
            u = jnp.clip(u, -SWIGLU_LIMIT, SWIGLU_LIMIT)
            h_ref[:rows, gs] = ((u + 1.0) * (g * jax.nn.sigmoid(SWIGLU_ALPHA * g))).astype(BF16)
        y = jnp.dot(h_ref[:rows], wdn_bf[...], preferred_element_type=F32) + bdn_ref[0, 0]
        return _pack_bf16_pairs(y.astype(BF16))

    tm = o_ref.shape[0]
    live_rows = br_ref[blk]

    @pl.when(live_rows > tm // 2)
    def _():
        o_ref[...] = mlp(tm)

    @pl.when((live_rows > 0) & (live_rows <= tm // 2))
    def _():
        o_ref[:tm // 2] = mlp(tm // 2)
        o_ref[tm // 2:] = jnp.zeros((tm - tm // 2, o_ref.shape[1]), o_ref.dtype)

    @pl.when(live_rows == 0)
    def _():
        o_ref[...] = jnp.zeros_like(o_ref)


def _experts(block_expert, n_used, block_rows, xb, blk_lo, p_rows, y_prev, layer, wgu, bgu, wdn, bdn):
    tm = MOE_TM
    has_prev = y_prev is not None
    b_map = lambda i, be, nu, br: (layer, be[i + blk_lo], 0, 0)
    in_specs = [pl.BlockSpec((tm, D_MODEL // 2), lambda i, be, nu, br: (i, 0)),
                pl.BlockSpec(memory_space=pl.ANY),
                pl.BlockSpec((1, 1, 1, 2 * D_EXPERT), b_map),
                pl.BlockSpec(memory_space=pl.ANY),
                pl.BlockSpec((1, 1, 1, D_MODEL), b_map)]
    args = [block_expert, n_used, block_rows, xb, wgu, bgu, wdn, bdn]
    if has_prev:
        in_specs.append(pl.BlockSpec(memory_space=pl.ANY))
        args.append(y_prev)
    grid_spec = pltpu.PrefetchScalarGridSpec(
        num_scalar_prefetch=3,
        grid=(xb.shape[0] // tm,),
        in_specs=in_specs,
        out_specs=pl.BlockSpec((tm, D_MODEL // 2), lambda i, be, nu, br: (i + blk_lo, 0)),
        scratch_shapes=[pltpu.VMEM((D_MODEL, 2 * D_EXPERT), F32),
                        pltpu.VMEM((D_EXPERT, D_MODEL), F32),
                        pltpu.VMEM((D_MODEL, 2 * D_EXPERT), BF16),
                        pltpu.VMEM((D_EXPERT, D_MODEL), BF16),
                        pltpu.VMEM((tm, D_EXPERT), BF16),
                        pltpu.SemaphoreType.DMA((2,))],
    )
    return pl.pallas_call(
        functools.partial(_expert_kernel, n_chunks=4, blk_lo=blk_lo, n_blocks=p_rows // tm,
                          layer=layer, has_prev=has_prev),
        grid_spec=grid_spec,
        out_shape=jax.ShapeDtypeStruct((p_rows, D_MODEL // 2), jnp.uint32),
        input_output_aliases={len(args) - 1: 0} if has_prev else {},
        compiler_params=_cparams(1),
        name="experts",
    )(*args)


def _sc_move_rows(table_hbm, idx_v, out_hbm, base, n_chunks, rows_v, gsem, wsem):
    def gather(c, slot):
        off = pl.multiple_of(c * SC_CHUNK, SC_CHUNK)
        return pltpu.make_async_copy(table_hbm.at[idx_v.at[pl.ds(off, SC_CHUNK)]],
                                     rows_v.at[slot], gsem.at[slot])

    def put(c, slot):
        off = pl.multiple_of(c * SC_CHUNK, SC_CHUNK)
        return pltpu.make_async_copy(rows_v.at[slot], out_hbm.at[pl.ds(base + off, SC_CHUNK)],
                                     wsem.at[slot])

    gather(0, 0).start()
    gather(1, 1).start()

    @pl.loop(0, n_chunks, step=2)
    def _(c):
        for slot in range(2):
            gather(c + slot, slot).wait()
            put(c + slot, slot).start()
        for slot in range(2):
            put(c + slot, slot).wait()

            @pl.when(c + 2 + slot < n_chunks)
            def _():
                gather(c + 2 + slot, slot).start()


def _sc_gather_rows(table, idx):
    n_idx = idx.shape[0]
    width = table.shape[1]
    per_worker = n_idx // SC_WORKERS
    n_chunks = per_worker // SC_CHUNK
    assert per_worker * SC_WORKERS == n_idx and n_chunks * SC_CHUNK == per_worker and n_chunks % 2 == 0
    mesh = plsc.VectorSubcoreMesh(core_axis_name="c", subcore_axis_name="s",
                                  num_cores=SC_CORES, num_subcores=SC_SUBCORES)

    def body(table_hbm, idx_hbm, out_hbm, idx_v, rows_v, gsem, wsem):
        wid = lax.axis_index("s") * SC_CORES + lax.axis_index("c")
        base = wid * per_worker
        pltpu.sync_copy(idx_hbm.at[pl.ds(base, per_worker)], idx_v)
        _sc_move_rows(table_hbm, idx_v, out_hbm, base, n_chunks, rows_v, gsem, wsem)

    return pl.kernel(
        body,
        out_type=jax.ShapeDtypeStruct((n_idx, width), table.dtype),
        mesh=mesh,
        scratch_types=[pltpu.VMEM((per_worker,), jnp.int32),
                       pltpu.VMEM((2, SC_CHUNK, width), table.dtype),
                       pltpu.SemaphoreType.DMA((2,)),
                       pltpu.SemaphoreType.DMA((2,))],
        name="sc_gather_rows",
    )(table, idx)


def _sc_dispatch_rows(table, dest, slot_lo, n_slots):
    n_tok, width = table.shape
    n_pairs = dest.shape[0]
    per_worker = n_slots // SC_WORKERS
    n_chunks = per_worker // SC_CHUNK
    assert per_worker * SC_WORKERS == n_slots and n_chunks * SC_CHUNK == per_worker and n_chunks % 2 == 0
    n_scan = n_pairs // SC_SCAN
    assert n_scan * SC_SCAN == n_pairs
    mesh = plsc.VectorSubcoreMesh(core_axis_name="c", subcore_axis_name="s",
                                  num_cores=SC_CORES, num_subcores=SC_SUBCORES)

    def body(table_hbm, dest_hbm, out_hbm, tok_v, dest_v, rows_v, gsem, wsem):
        wid = lax.axis_index("s") * SC_CORES + lax.axis_index("c")
        base = wid * per_worker
        first = slot_lo + base
        lane = lax.iota(jnp.int32, SC_LANES)

        @pl.loop(0, per_worker // SC_LANES)
        def _(i):
            off = pl.multiple_of(i * SC_LANES, SC_LANES)
            tok_v[pl.ds(off, SC_LANES)] = lax.rem(first + off + lane, n_tok)

        @pl.loop(0, n_scan)
        def _(g):
            goff = pl.multiple_of(g * SC_SCAN, SC_SCAN)
            pltpu.sync_copy(dest_hbm.at[pl.ds(goff, SC_SCAN)], dest_v)

            @plsc.parallel_loop(0, SC_SCAN // SC_LANES, unroll=8)
            def _(i):
                off = pl.multiple_of(i * SC_LANES, SC_LANES)
                local = dest_v[pl.ds(off, SC_LANES)] - first
                mine = (local >= 0) & (local < per_worker)
                pair = goff + off + lane
                plsc.store_scatter(tok_v, [jnp.where(mine, local, 0)], lax.rem(pair, n_tok), mask=mine)

        _sc_move_rows(table_hbm, tok_v, out_hbm, base, n_chunks, rows_v, gsem, wsem)

    return pl.kernel(
        body,
        out_type=jax.ShapeDtypeStruct((n_slots, width), table.dtype),
        mesh=mesh,
        scratch_types=[pltpu.VMEM((per_worker,), jnp.int32),
                       pltpu.VMEM((SC_SCAN,), jnp.int32),
                       pltpu.VMEM((2, SC_CHUNK, width), table.dtype),
                       pltpu.SemaphoreType.DMA((2,)),
                       pltpu.SemaphoreType.DMA((2,))],
        compiler_params=pltpu.CompilerParams(needs_layout_passes=False),
        name="sc_dispatch_rows",
    )(table, dest)


def _combine_kernel(x1_ref, yg_ref, gate_ref, g_ref, b_ref, *rest):
    o_ref = rest[-1]
    gates = gate_ref[...]
    ffn = _unpack_bf16_pairs(yg_ref[0]) * gates[:, 0:1]
    for r in range(1, TOP_K):
        ffn = ffn + _unpack_bf16_pairs(yg_ref[r]) * gates[:, r:r + 1]
    o_ref[...] = _layer_norm(DEEPNORM_ALPHA * x1_ref[...] + ffn, g_ref[...], b_ref[...])


def _combine(x1, yg, gates, g, b, tok_lo, out_prev):
    t_tokens = x1.shape[0]
    tm = COMBINE_TM
    tile_lo = tok_lo // tm
    in_specs = [pl.BlockSpec((tm, D_MODEL), lambda i: (i + tile_lo, 0)),
                pl.BlockSpec((TOP_K, tm, D_MODEL // 2), lambda i: (0, i, 0)),
                pl.BlockSpec((tm, LANES), lambda i: (i + tile_lo, 0)),
                pl.BlockSpec((1, D_MODEL), lambda i: (0, 0)),
                pl.BlockSpec((1, D_MODEL), lambda i: (0, 0))]
    args = [x1, yg, gates, g, b]
    if out_prev is not None:
        in_specs.append(pl.BlockSpec(memory_space=pl.ANY))
        args.append(out_prev)
    return pl.pallas_call(
        _combine_kernel,
        grid=(yg.shape[1] // tm,),
        in_specs=in_specs,
        out_specs=pl.BlockSpec((tm, D_MODEL), lambda i: (i + tile_lo, 0)),
        out_shape=jax.ShapeDtypeStruct((t_tokens, D_MODEL), F32),
        input_output_aliases={len(args) - 1: 0} if out_prev is not None else {},
        compiler_params=_cparams(1),
        name="combine_ln",
    )(*args)


def _moe(x1, x1p, idx, rank, gates, counts, layer, wgu, bgu, wdn, bdn, g, b):
    t_tokens = x1.shape[0]
    tm = MOE_TM
    counts = counts[:, 0].astype(jnp.int32)
    padded = ((counts + tm - 1) // tm) * tm
    pend = jnp.cumsum(padded)
    pstart = pend - padded
    dest = _slot_table(idx, rank, pstart)[:TOP_K]
    n_blocks = (t_tokens * TOP_K) // tm + N_EXPERTS
    p_rows = n_blocks * tm
    block_start = jnp.arange(n_blocks, dtype=jnp.int32) * tm
    block_expert = jnp.minimum(
        jnp.sum((pend[None, :] <= block_start[:, None]).astype(jnp.int32), axis=1), N_EXPERTS - 1)
    n_used = (pend[-1:] // tm).astype(jnp.int32)
    mine = block_expert[:, None] == jnp.arange(N_EXPERTS, dtype=jnp.int32)[None, :]
    live_end = jnp.sum(jnp.where(mine, (pstart + counts)[None, :], 0), axis=1)
    block_rows = jnp.clip(live_end - block_start, 0, tm)
    y = None
    for lo, hi in ((0, n_blocks // MOE_SPLIT), (n_blocks // MOE_SPLIT, n_blocks)):
        xb = _sc_dispatch_rows(x1p, dest.reshape(-1), lo * tm, (hi - lo) * tm)
        y = _experts(block_expert, n_used, block_rows, xb, lo, p_rows, y, layer, wgu, bgu, wdn, bdn)
    out = None
    for lo, hi in ((0, t_tokens // MOE_SPLIT), (t_tokens // MOE_SPLIT, t_tokens)):
        yg = _sc_gather_rows(y, dest[:, lo:hi].reshape(-1)).reshape(TOP_K, hi - lo, D_MODEL // 2)
        out = _combine(x1, yg, gates, g, b, lo, out)
    return out


def _row(v, width=None):
    v = v.astype(F32).reshape(1, -1)
    if width is not None and v.shape[1] < width:
        v = jnp.pad(v, ((0, 0), (0, width - v.shape[1])))
    return v


def _pad_cols(w, width):
    return jnp.pad(w, ((0, 0), (0, width - w.shape[1])))


def kernel(x, mem, positions, a_w_in, a_b_f, a_w_out, b_w_in, b_g_q, b_w_uq, b_w_out,
           kv_w_dkv, kv_g, kv_w_ukv, mem_w_kv, ln_g, ln_b,
           moe_w_r, moe_b_r, moe_w_gu, moe_b_gu, moe_w_dn, moe_b_dn):
    nb, seq, d = x.shape
    t_tokens = nb * seq
    n_a = a_w_in.shape[0]
    x2d = x.reshape(t_tokens, d)
    mem2d = mem.reshape(nb * N_MEM, d)
    pos2d = positions.reshape(1, t_tokens)
    half = QK_ROPE // 2
    invf = (ROPE_THETA ** (-jnp.arange(half, dtype=F32) * 2.0 / QK_ROPE)).reshape(half, 1)

    shared_kv = None
    for l in range(DEPTH):
        mk, mvt = _mem_proj(mem2d, mem_w_kv[l].astype(BF16), nb)
        if l < n_a:
            w_in = a_w_in[l]
            w = jnp.concatenate([w_in[:, :3 * FOX_WIDTH],
                                 _pad_cols(w_in[:, 3 * FOX_WIDTH:3 * FOX_WIDTH + FOX_HEADS], LANES),
                                 w_in[:, 3 * FOX_WIDTH + FOX_HEADS:]], axis=1).astype(BF16)
            qt, k, vt, mqt, cum, cumt, cfirst, clast, knorm = _fox_proj(
                x2d, w, _row(a_b_f[l], LANES), seq)
            stats = [s.reshape(-1, LANES) for s in (cfirst, clast, knorm)]
            att = _causal_attention(qt, k.reshape(nb, seq, -1), vt,
                                    [cum.reshape(nb, seq, LANES), cumt] + stats, slab=False)
            w_out = a_w_out[l]
        else:
            bl = l - n_a
            w = jnp.concatenate([b_w_in[bl], kv_w_dkv[:, :KV_LORA],
                                 jnp.zeros((d, QK_NOPE), F32), kv_w_dkv[:, KV_LORA:],
                                 jnp.zeros((d, LANES - QK_NOPE - QK_ROPE), F32)], axis=1).astype(BF16)
            wuq = b_w_uq[bl].reshape(Q_LORA, MLA_HEADS, QK_NOPE + QK_ROPE)
            wuq = jnp.pad(wuq, ((0, 0), (0, 0), (0, MLA_SLAB - QK_NOPE - QK_ROPE)))
            wuq = wuq.reshape(Q_LORA, MLA_HEADS * MLA_SLAB).astype(BF16)
            wukv = kv_w_ukv.reshape(KV_LORA, MLA_HEADS, QK_NOPE + V_DIM)
            wk = jnp.pad(wukv[:, :, :QK_NOPE], ((0, 0), (0, 0), (0, MLA_SLAB - QK_NOPE)))
            wk = wk.reshape(KV_LORA, MLA_HEADS * MLA_SLAB).astype(BF16)
            wv = wukv[:, :, QK_NOPE:].reshape(KV_LORA, MLA_V_WIDTH).astype(BF16)
            qt, mqt, k_new, vt_new = _mla_proj(x2d, pos2d, invf, w, _row(b_g_q[bl]), wuq,
                                              _row(kv_g), wk, wv, seq)
            if shared_kv is None:
                shared_kv = (k_new.reshape(nb, seq, -1), vt_new)
            att = _causal_attention(qt, shared_kv[0], shared_kv[1], None, slab=True)
            w_out = b_w_out[bl]
        memo = _memory_attention(mqt, mk, mvt)
        n_att = w_out.shape[0] - MEM_WIDTH
        x1, x1p, idx, rank, gates, counts = _post_attn(
            x2d, att.reshape(t_tokens, -1), memo.reshape(t_tokens, MEM_WIDTH),
            w_out[:n_att].astype(BF16), w_out[n_att:].astype(BF16),
            _row(ln_g[l, 0]), _row(ln_b[l, 0]),
            _pad_cols(moe_w_r[l], LANES).astype(BF16), _row(moe_b_r[l], LANES))
        x2d = _moe(x1, x1p, idx, rank, gates, counts, l,
                   moe_w_gu, moe_b_gu.reshape(DEPTH, N_EXPERTS, 1, -1),
                   moe_w_dn, moe_b_dn.reshape(DEPTH, N_EXPERTS, 1, -1),
                   _row(ln_g[l, 1]), _row(ln_b[l, 1]))
    return x2d.reshape(nb, seq, d)
```
